```python
import jax, jax.numpy as jnp
from jax import lax
import numpy as np

D_MODEL = 2048
BATCH = 1
SEQ = 8192
DEPTH = 1

D_SSM = 1024
SSM_GROUP = 16
SSM_GROUPS = D_SSM // SSM_GROUP
SSM_STATE = 64
NA_HEADS = 16
NA_HEAD_DIM = 64
D_NA = NA_HEADS * NA_HEAD_DIM
D_MIX = D_SSM + D_NA
D_IN_PROJ = D_SSM + 3 * D_NA
GRID_W = 64
WIN_ROWS_MAX = 8
WIN_COLS = 16
RPB_ROWS = 2 * WIN_ROWS_MAX - 1
RPB_COLS = 2 * WIN_COLS - 1
N_EXPERTS = 16
EXPERT_FF = 2048
EC_CAPACITY_FACTOR = 2
RMS_EPS = 1e-6
DT_MIN = 1e-3
DT_MAX = 1e-1

kernel_name = "hybrid_s5_natten_ecmoe_block"


def rms_norm(x, g):
    xf = x.astype(jnp.float32)
    y = xf * lax.rsqrt(jnp.mean(xf * xf, axis=-1, keepdims=True) + RMS_EPS)
    return (y * g.astype(jnp.float32)).astype(x.dtype)


def s5_direction(u, a_re, a_im, log_dt, b_re, b_im, c_re, c_im, reverse):
    lam = lax.complex(a_re.astype(jnp.float32), a_im.astype(jnp.float32))
    dt = jnp.exp(log_dt.astype(jnp.float32))[:, None]
    a_bar = jnp.exp(lam * dt)
    b = lax.complex(b_re.astype(jnp.float32), b_im.astype(jnp.float32))
    b_bar = ((a_bar - 1.0) / lam)[..., None] * b
    bu = jnp.einsum('blgc,gpc->blgp', u.astype(jnp.complex64), b_bar)
    a = jnp.broadcast_to(a_bar, bu.shape)

    def combine(e1, e2):
        a1, s1 = e1
        a2, s2 = e2
        return a1 * a2, a2 * s1 + s2

    _, states = lax.associative_scan(combine, (a, bu), axis=1, reverse=reverse)
    c = lax.complex(c_re.astype(jnp.float32), c_im.astype(jnp.float32))
    return jnp.real(jnp.einsum('blgp,gcp->blgc', states, c))


def s5_mixer(u, fwd, bwd, d_skip, w_glu, b_glu):
    bsz, seq_len, _ = u.shape
    uf = u.astype(jnp.float32)
    ug = uf.reshape(bsz, seq_len, SSM_GROUPS, SSM_GROUP)
    y = s5_direction(ug, *fwd, reverse=False) + s5_direction(ug, *bwd, reverse=True)
    y = y.reshape(bsz, seq_len, D_SSM) + d_skip.astype(jnp.float32) * uf
    y = jax.nn.gelu(y).astype(u.dtype)
    return y * jax.nn.sigmoid(y @ w_glu + b_glu)


def neighbourhood_attention(q, k, v, rpb):
    bsz, seq_len, _ = q.shape
    rows = seq_len // GRID_W
    wr = min(WIN_ROWS_MAX, rows)

    def to_grid(t):
        return t.reshape(bsz, rows, GRID_W, NA_HEADS, NA_HEAD_DIM)

    qg, kg, vg = to_grid(q), to_grid(k), to_grid(v)
    cols = np.arange(GRID_W)
    col_start = np.clip(cols - WIN_COLS // 2, 0, GRID_W - WIN_COLS)
    col_idx = (col_start[:, None] + np.arange(WIN_COLS)[None, :]).astype(np.int32)
    dx_idx = (col_idx - cols[:, None] + (WIN_COLS - 1)).astype(np.int32)
    scale = NA_HEAD_DIM ** -0.5
    rpb_f = rpb.astype(jnp.float32)

    def one_row(r):
        rs = jnp.clip(r - wr // 2, 0, rows - wr)
        k_rows = lax.dynamic_slice_in_dim(kg, rs, wr, axis=1)
        v_rows = lax.dynamic_slice_in_dim(vg, rs, wr, axis=1)
        k_win = k_rows[:, :, col_idx]
        v_win = v_rows[:, :, col_idx]
        q_r = lax.dynamic_index_in_dim(qg, r, axis=1, keepdims=False)
        s = jnp.einsum('bchd,bicjhd->bhcij', q_r, k_win).astype(jnp.float32) * scale
        dy_idx = rs + jnp.arange(wr, dtype=jnp.int32) - r + (WIN_ROWS_MAX - 1)
        bias = rpb_f[:, dy_idx][:, :, dx_idx]
        s = s + jnp.transpose(bias, (0, 2, 1, 3))[None]
        p = jax.nn.softmax(s.reshape(bsz, NA_HEADS, GRID_W, wr * WIN_COLS), axis=-1)
        p = p.reshape(s.shape).astype(v.dtype)
        return jnp.einsum('bhcij,bicjhd->bchd', p, v_win)

    out = lax.map(one_row, jnp.arange(rows, dtype=jnp.int32))
    return jnp.moveaxis(out, 0, 1).reshape(bsz, seq_len, D_NA)


def expert_choice_moe(h, w_router, w_gate, w_up, w_down):
    bsz, seq_len, _ = h.shape
    cap = EC_CAPACITY_FACTOR * seq_len // N_EXPERTS
    affinity = jax.nn.softmax((h @ w_router).astype(jnp.float32), axis=-1)
    gates, idx = lax.top_k(jnp.swapaxes(affinity, 1, 2), cap)
    b_idx = jnp.arange(bsz, dtype=jnp.int32)[:, None, None]
    xe = h[b_idx, idx]
    g = jnp.einsum('becd,edf->becf', xe, w_gate)
    u = jnp.einsum('becd,edf->becf', xe, w_up)
    ye = jnp.einsum('becf,efd->becd', jax.nn.silu(g) * u, w_down)
    ye = ye * gates[..., None].astype(ye.dtype)
    return jnp.zeros_like(h).at[b_idx, idx].add(ye)


def setup_inputs(seed: int = 0) -> dict:
    key = jax.random.key(seed)
    ks = iter(jax.random.split(key, 40))
    f32 = jnp.float32

    def nrm(shape, std):
        return jax.random.normal(next(ks), shape, f32) * std

    def gain(shape):
        return 1.0 + nrm(shape, 0.01)

    n = jnp.arange(SSM_STATE, dtype=f32)

    def ssm_dir():
        a_re = -0.5 + nrm((DEPTH, SSM_GROUPS, SSM_STATE), 0.01)
        a_im = jnp.pi * n + nrm((DEPTH, SSM_GROUPS, SSM_STATE), 0.01)
        log_dt = jax.random.uniform(next(ks), (DEPTH, SSM_GROUPS), f32,
                                    np.log(DT_MIN), np.log(DT_MAX))
        b_std = (2.0 * SSM_GROUP) ** -0.5
        c_std = (2.0 * SSM_STATE) ** -0.5
        b_re = nrm((DEPTH, SSM_GROUPS, SSM_STATE, SSM_GROUP), b_std)
        b_im = nrm((DEPTH, SSM_GROUPS, SSM_STATE, SSM_GROUP), b_std)
        c_re = nrm((DEPTH, SSM_GROUPS, SSM_GROUP, SSM_STATE), c_std)
        c_im = nrm((DEPTH, SSM_GROUPS, SSM_GROUP, SSM_STATE), c_std)
        return a_re, a_im, log_dt, b_re, b_im, c_re, c_im

    x = nrm((BATCH, SEQ, D_MODEL), 1.0)
    norm_mix_g = gain((DEPTH, D_MODEL))
    w_in = nrm((DEPTH, D_MODEL, D_IN_PROJ), D_MODEL ** -0.5)
    a_re_fwd, a_im_fwd, log_dt_fwd, b_re_fwd, b_im_fwd, c_re_fwd, c_im_fwd = ssm_dir()
    a_re_bwd, a_im_bwd, log_dt_bwd, b_re_bwd, b_im_bwd, c_re_bwd, c_im_bwd = ssm_dir()
    ssm_d = nrm((DEPTH, D_SSM), 1.0)
    w_glu = nrm((DEPTH, D_SSM, D_SSM), D_SSM ** -0.5)
    b_glu = nrm((DEPTH, D_SSM), 0.01)
    na_rpb = nrm((DEPTH, NA_HEADS, RPB_ROWS, RPB_COLS), 0.02)
    g_ssm_out = gain((DEPTH, D_SSM))
    g_na_out = gain((DEPTH, D_NA))
    w_out = nrm((DEPTH, D_MIX, D_MODEL), D_MIX ** -0.5)
    norm_ffn_g = gain((DEPTH, D_MODEL))
    w_router = nrm((DEPTH, D_MODEL, N_EXPERTS), D_MODEL ** -0.5)
    w_gate = nrm((DEPTH, N_EXPERTS, D_MODEL, EXPERT_FF), D_MODEL ** -0.5)
    w_up = nrm((DEPTH, N_EXPERTS, D_MODEL, EXPERT_FF), D_MODEL ** -0.5)
    w_down = nrm((DEPTH, N_EXPERTS, EXPERT_FF, D_MODEL), EXPERT_FF ** -0.5)
    norm_final_g = gain((D_MODEL,))
    return {
        "x": x, "norm_mix_g": norm_mix_g, "w_in": w_in,
        "a_re_fwd": a_re_fwd, "a_im_fwd": a_im_fwd, "log_dt_fwd": log_dt_fwd,
        "b_re_fwd": b_re_fwd, "b_im_fwd": b_im_fwd, "c_re_fwd": c_re_fwd, "c_im_fwd": c_im_fwd,
        "a_re_bwd": a_re_bwd, "a_im_bwd": a_im_bwd, "log_dt_bwd": log_dt_bwd,
        "b_re_bwd": b_re_bwd, "b_im_bwd": b_im_bwd, "c_re_bwd": c_re_bwd, "c_im_bwd": c_im_bwd,
        "ssm_d": ssm_d, "w_glu": w_glu, "b_glu": b_glu, "na_rpb": na_rpb,
        "g_ssm_out": g_ssm_out, "g_na_out": g_na_out, "w_out": w_out,
        "norm_ffn_g": norm_ffn_g, "w_router": w_router, "w_gate": w_gate,
        "w_up": w_up, "w_down": w_down, "norm_final_g": norm_final_g,
    }


def reference(x, norm_mix_g, w_in,
              a_re_fwd, a_im_fwd, log_dt_fwd, b_re_fwd, b_im_fwd, c_re_fwd, c_im_fwd,
              a_re_bwd, a_im_bwd, log_dt_bwd, b_re_bwd, b_im_bwd, c_re_bwd, c_im_bwd,
              ssm_d, w_glu, b_glu, na_rpb, g_ssm_out, g_na_out, w_out,
              norm_ffn_g, w_router, w_gate, w_up, w_down, norm_final_g):
    for layer in range(DEPTH):
        h = rms_norm(x, norm_mix_g[layer])
        proj = h @ w_in[layer]
        u, q, k, v = jnp.split(proj, [D_SSM, D_SSM + D_NA, D_SSM + 2 * D_NA], axis=-1)
        fwd = (a_re_fwd[layer], a_im_fwd[layer], log_dt_fwd[layer], b_re_fwd[layer],
               b_im_fwd[layer], c_re_fwd[layer], c_im_fwd[layer])
        bwd = (a_re_bwd[layer], a_im_bwd[layer], log_dt_bwd[layer], b_re_bwd[layer],
               b_im_bwd[layer], c_re_bwd[layer], c_im_bwd[layer])
        y_ssm = s5_mixer(u, fwd, bwd, ssm_d[layer], w_glu[layer], b_glu[layer])
        y_na = neighbourhood_attention(q, k, v, na_rpb[layer])
        y_mix = jnp.concatenate([rms_norm(y_ssm, g_ssm_out[layer]),
                                 rms_norm(y_na, g_na_out[layer])], axis=-1)
        x = x + y_mix @ w_out[layer]
        h2 = rms_norm(x, norm_ffn_g[layer])
        x = x + expert_choice_moe(h2, w_router[layer], w_gate[layer], w_up[layer], w_down[layer])
    return rms_norm(x, norm_final_g)
```

```python
import functools

import numpy as np
import jax
import jax.numpy as jnp
from jax import lax
from jax.experimental import pallas as pl
from jax.experimental.pallas import tpu as pltpu

F32 = jnp.float32
BF16 = jnp.bfloat16

RMS_EPS = 1e-6
SSM_GROUP = 16
SSM_STATE = 64
NA_HEADS = 16
NA_HEAD_DIM = 64
GRID_W = 64
WIN_ROWS = 8
WIN_COLS = 16
N_EXPERTS = 16
EC_CAPACITY_FACTOR = 2

S5_CHUNK = 16
S5_SEGS = 8
HEADS_PER_DOT = 4
MOE_TOK_BLOCK = 128
MOE_WIN_ALIGN = 16
MOE_WIN = MOE_TOK_BLOCK + MOE_WIN_ALIGN
MASK_NEG = -1e30

VMEM_LIMIT_BYTES = 56 * 1024 * 1024


def _params(*semantics):
    return pltpu.CompilerParams(dimension_semantics=semantics, vmem_limit_bytes=VMEM_LIMIT_BYTES)


def _rms(x, g):
    ms = jnp.mean(x * x, axis=-1, keepdims=True)
    return x * lax.rsqrt(ms + RMS_EPS) * g


def _inproj_kernel(x_ref, g_ref, w_ref, u_ref, qkv_ref, h_scr, *, n_u, n_q, q_scale):
    j = pl.program_id(1)

    @pl.when(j == 0)
    def _():
        h_scr[...] = _rms(x_ref[...], g_ref[...]).astype(BF16)

    acc = jnp.dot(h_scr[...], w_ref[...].astype(BF16), preferred_element_type=F32)

    @pl.when(j < n_u)
    def _():
        u_ref[...] = acc

    @pl.when(j >= n_u)
    def _():
        scale = jnp.where(j < n_u + n_q, q_scale, 1.0).astype(F32)
        qkv_ref[...] = (acc * scale).astype(BF16)


def _in_proj(x, g, w_in, d_ssm, d_na):
    seq, d_model = x.shape
    tm = min(1024, seq)
    tn = 512
    n_u, n_q = d_ssm // tn, d_na // tn
    n_cols = w_in.shape[1] // tn
    kern = functools.partial(_inproj_kernel, n_u=n_u, n_q=n_q, q_scale=NA_HEAD_DIM ** -0.5)
    return pl.pallas_call(
        kern,
        grid=(seq // tm, n_cols),
        in_specs=[
            pl.BlockSpec((tm, d_model), lambda i, j: (i, 0)),
            pl.BlockSpec((1, d_model), lambda i, j: (0, 0)),
            pl.BlockSpec((d_model, tn), lambda i, j: (0, j)),
        ],
        out_specs=[
            pl.BlockSpec((tm, tn), lambda i, j: (i, jnp.minimum(j, n_u - 1))),
            pl.BlockSpec((tm, tn), lambda i, j: (i, jnp.maximum(j - n_u, 0))),
        ],
        out_shape=[
            jax.ShapeDtypeStruct((seq, d_ssm), F32),
            jax.ShapeDtypeStruct((seq, 3 * d_na), BF16),
        ],
        scratch_shapes=[pltpu.VMEM((tm, d_model), BF16)],
        compiler_params=_params("arbitrary", "arbitrary"),
        name="in_proj",
    )(x, g.reshape(1, d_model), w_in)


def _s5_direction_tables(a_re, a_im, log_dt, b_re, b_im, c_re, c_im, n_tiles, reverse):
    t_len = S5_CHUNK
    lam = lax.complex(a_re.astype(F32), a_im.astype(F32))
    dt = jnp.exp(log_dt.astype(F32))[:, None]
    lam_dt = lam * dt
    steps = jnp.arange(t_len + 1, dtype=F32)
    apow = jnp.exp(lam_dt[None] * steps[:, None, None])
    b_bar = ((apow[1] - 1.0) / lam)[..., None] * lax.complex(b_re.astype(F32), b_im.astype(F32))
    c = lax.complex(c_re.astype(F32), c_im.astype(F32))
    lag = jnp.real(jnp.einsum('gcp,kgp,gpi->kgci', c, apow[:t_len], b_bar))
    j_idx = np.arange(t_len)[:, None]
    t_idx = np.arange(t_len)[None, :]
    d = (j_idx - t_idx) if reverse else (t_idx - j_idx)
    toep = jnp.where((d >= 0)[:, :, None, None, None], lag[np.clip(d, 0, t_len - 1)], 0.0)
    n_ch = c.shape[1]
    toep = jnp.transpose(toep, (2, 0, 4, 1, 3)).reshape(-1, t_len * n_ch, t_len * n_ch)
    e_in = (np.arange(t_len)) if reverse else (t_len - 1 - np.arange(t_len))
    st = apow[e_in][:, :, :, None] * b_bar[None]
    st = jnp.transpose(st, (1, 0, 3, 2)).reshape(c.shape[0], t_len * n_ch, -1)
    e_out = (t_len - np.arange(t_len)) if reverse else (np.arange(t_len) + 1)
    wout = c[:, None, :, :] * apow[e_out].transpose(1, 0, 2)[:, :, None, :]
    wout = jnp.transpose(wout, (0, 3, 1, 2)).reshape(c.shape[0], -1, t_len * n_ch)
    a_chunk = apow[t_len]
    a_seg = jnp.exp(lam_dt * float(t_len * n_tiles))
    tiles = jnp.arange(n_tiles, dtype=F32)
    if reverse:
        tiles = tiles[::-1]
    a_tile = jnp.exp(lam_dt[None] * (float(t_len) * tiles)[:, None, None])
    return toep, st, wout, a_chunk, a_seg, a_tile


def _pair_block_diag(x, lead):
    k, g, a, b = x.shape
    x = x.reshape(k, g // 2, 2, a, b)
    eye = jnp.eye(2, dtype=x.dtype)
    if lead:
        y = jnp.einsum('kigab,gh->igakhb', x, eye)
        return y.reshape(g // 2, 2 * a, k * 2 * b)
    y = jnp.einsum('kigab,gh->ikgahb', x, eye)
    return y.reshape(g // 2, k * 2 * a, 2 * b)


def _s5_a_kernel(u_ref, wm_ref, e_ref, y_ref, s_ref):
    half = wm_ref.shape[-1]
    u2 = u_ref[...]
    y_ref[:, :half] = jnp.dot(u2[:, :half], wm_ref[0], preferred_element_type=F32)
    y_ref[:, half:] = jnp.dot(u2[:, half:], wm_ref[1], preferred_element_type=F32)
    s = jnp.dot(u2, e_ref[0], preferred_element_type=F32)
    w = s_ref.shape[-1]
    for k in range(4):
        s_ref[k] = s[:, k * w:(k + 1) * w]


def _s5_scan_kernel(s_ref, ap_ref, pw_ref, o_ref, *, n_tiles):
    lanes = s_ref.shape[-1]
    for dirn in range(2):
        kre, kim = 2 * dirn, 2 * dirn + 1
        ar = ap_ref[kre:kre + 1, :]
        ai = ap_ref[kim:kim + 1, :]

        def tile_rows(n, dirn=dirn):
            j = n if dirn == 0 else n_tiles - 1 - n
            return pl.ds(pl.multiple_of(j * S5_SEGS, S5_SEGS), S5_SEGS)

        def step(n, carry, kre=kre, kim=kim, ar=ar, ai=ai, tile_rows=tile_rows):
            zr, zi = carry
            rows = tile_rows(n)
            o_ref[kre, rows, :] = zr
            o_ref[kim, rows, :] = zi
            lr = s_ref[kre, rows, :]
            li = s_ref[kim, rows, :]
            return ar * zr - ai * zi + lr, ar * zi + ai * zr + li

        z0 = jnp.zeros((S5_SEGS, lanes), F32)
        zr, zi = lax.fori_loop(0, n_tiles, step, (z0, z0))

        sr = ap_ref[4 + kre:5 + kre, :]
        si = ap_ref[4 + kim:5 + kim, :]
        cr = jnp.zeros((1, lanes), F32)
        ci = jnp.zeros((1, lanes), F32)
        seg_r = [None] * S5_SEGS
        seg_i = [None] * S5_SEGS
        order = range(S5_SEGS) if dirn == 0 else range(S5_SEGS - 1, -1, -1)
        for s in order:
            seg_r[s], seg_i[s] = cr, ci
            cr, ci = (zr[s:s + 1] + sr * cr - si * ci, zi[s:s + 1] + sr * ci + si * cr)
        car_r = jnp.concatenate(seg_r, axis=0)
        car_i = jnp.concatenate(seg_i, axis=0)

        def fix(n, _, kre=kre, kim=kim, car_r=car_r, car_i=car_i):
            rows = pl.ds(pl.multiple_of(n * S5_SEGS, S5_SEGS), S5_SEGS)
            pr = pw_ref[kre, pl.ds(n, 1), :]
            pi = pw_ref[kim, pl.ds(n, 1), :]
            o_ref[kre, rows, :] = o_ref[kre, rows, :] + (pr * car_r - pi * car_i)
            o_ref[kim, rows, :] = o_ref[kim, rows, :] + (pr * car_i + pi * car_r)
            return 0

        lax.fori_loop(0, n_tiles, fix, 0)


def _s5_c_kernel(yi_ref, s_ref, c_ref, y_ref):
    lhs = jnp.concatenate([s_ref[k] for k in range(4)], axis=1).astype(BF16)
    y_ref[...] = yi_ref[...] + jnp.dot(lhs, c_ref[0], preferred_element_type=F32)


def _s5_scan(u, fwd, bwd):
    seq, d_ssm = u.shape
    n_groups = d_ssm // SSM_GROUP
    n_state = SSM_STATE
    cw = S5_CHUNK * SSM_GROUP
    n_rows = seq // S5_CHUNK
    n_tiles = n_rows // S5_SEGS
    n_pairs = n_groups // 2

    tf = _s5_direction_tables(*fwd, n_tiles=n_tiles, reverse=False)
    tb = _s5_direction_tables(*bwd, n_tiles=n_tiles, reverse=True)
    w_intra = (tf[0] + tb[0]).astype(BF16)
    planes = jnp.stack([jnp.real(tf[1]), jnp.imag(tf[1]), jnp.real(tb[1]), jnp.imag(tb[1])])
    e_pair = _pair_block_diag(planes, lead=True).astype(BF16)
    cplanes = jnp.stack([jnp.real(tf[2]), -jnp.imag(tf[2]), jnp.real(tb[2]), -jnp.imag(tb[2])])
    c_pair = _pair_block_diag(cplanes, lead=False).astype(BF16)

    def flat(z):
        return z.reshape(z.shape[:-2] + (n_groups * n_state,))

    ap = jnp.stack([flat(jnp.real(tf[3])), flat(jnp.imag(tf[3])), flat(jnp.real(tb[3])), flat(jnp.imag(tb[3])),
                    flat(jnp.real(tf[4])), flat(jnp.imag(tf[4])), flat(jnp.real(tb[4])), flat(jnp.imag(tb[4]))])
    pw = jnp.stack([flat(jnp.real(tf[5])), flat(jnp.imag(tf[5])), flat(jnp.real(tb[5])), flat(jnp.imag(tb[5]))])

    uc = u.reshape(S5_SEGS, n_tiles, S5_CHUNK, n_groups, SSM_GROUP)
    uc = jnp.transpose(uc, (1, 0, 3, 2, 4)).reshape(n_rows, n_groups * cw).astype(BF16)

    pw2 = 2 * cw
    sw = 2 * n_state
    y_intra, s_loc = pl.pallas_call(
        _s5_a_kernel,
        grid=(n_pairs,),
        in_specs=[
            pl.BlockSpec((n_rows, pw2), lambda i: (0, i)),
            pl.BlockSpec((2, cw, cw), lambda i: (i, 0, 0)),
            pl.BlockSpec((1, pw2, 4 * sw), lambda i: (i, 0, 0)),
        ],
        out_specs=[
            pl.BlockSpec((n_rows, pw2), lambda i: (0, i)),
            pl.BlockSpec((4, n_rows, sw), lambda i: (0, 0, i)),
        ],
        out_shape=[
            jax.ShapeDtypeStruct((n_rows, n_groups * cw), F32),
            jax.ShapeDtypeStruct((4, n_rows, n_groups * n_state), F32),
        ],
        compiler_params=_params("arbitrary"),
        name="s5_a",
    )(uc, w_intra, e_pair)

    lb = min(1024, n_groups * n_state)
    s_in = pl.pallas_call(
        functools.partial(_s5_scan_kernel, n_tiles=n_tiles),
        grid=(n_groups * n_state // lb,),
        in_specs=[
            pl.BlockSpec((4, n_rows, lb), lambda i: (0, 0, i)),
            pl.BlockSpec((8, lb), lambda i: (0, i)),
            pl.BlockSpec((4, n_tiles, lb), lambda i: (0, 0, i)),
        ],
        out_specs=pl.BlockSpec((4, n_rows, lb), lambda i: (0, 0, i)),
        out_shape=jax.ShapeDtypeStruct((4, n_rows, n_groups * n_state), F32),
        compiler_params=_params("arbitrary"),
        name="s5_b",
    )(s_loc, ap, pw)

    y = pl.pallas_call(
        _s5_c_kernel,
        grid=(n_pairs,),
        in_specs=[
            pl.BlockSpec((n_rows, pw2), lambda i: (0, i)),
            pl.BlockSpec((4, n_rows, sw), lambda i: (0, 0, i)),
            pl.BlockSpec((1, 4 * sw, pw2), lambda i: (i, 0, 0)),
        ],
        out_specs=pl.BlockSpec((n_rows, pw2), lambda i: (0, i)),
        out_shape=jax.ShapeDtypeStruct((n_rows, n_groups * cw), F32),
        compiler_params=_params("arbitrary"),
        name="s5_c",
    )(y_intra, s_in, c_pair)

    y = y.reshape(n_tiles, S5_SEGS, n_groups, S5_CHUNK, SSM_GROUP)
    return jnp.transpose(y, (1, 0, 3, 2, 4)).reshape(seq, d_ssm)


def _s5_post_kernel(y_ref, u_ref, d_ref, w_ref, b_ref, g_ref, o_ref):
    y = y_ref[...] + d_ref[...] * u_ref[...]
    c0 = np.float32(np.sqrt(2.0 / np.pi))
    y = 0.5 * y * (1.0 + jnp.tanh(c0 * (y + np.float32(0.044715) * (y * y * y))))
    z = jnp.dot(y.astype(BF16), w_ref[...].astype(BF16), preferred_element_type=F32) + b_ref[...]
    o = y * (1.0 / (1.0 + jnp.exp(-z)))
    o_ref[...] = _rms(o, g_ref[...]).astype(BF16)


def _s5_post(y, u, d_skip, w_glu, b_glu, g):
    seq, d = y.shape
    tm = min(512, seq)
    row = lambda i: (i, 0)
    fix = lambda i: (0, 0)
    return pl.pallas_call(
        _s5_post_kernel,
        grid=(seq // tm,),
        in_specs=[
            pl.BlockSpec((tm, d), row), pl.BlockSpec((tm, d), row), pl.BlockSpec((1, d), fix),
            pl.BlockSpec((d, d), fix), pl.BlockSpec((1, d), fix), pl.BlockSpec((1, d), fix),
        ],
        out_specs=pl.BlockSpec((tm, d), row),
        out_shape=jax.ShapeDtypeStruct((seq, d), BF16),
        compiler_params=_params("arbitrary"),
        name="s5_post",
    )(y, u, d_skip.reshape(1, d), w_glu, b_glu.reshape(1, d), g.reshape(1, d))


def _na_bias_table(rpb):
    n_heads = rpb.shape[0]
    cols = np.arange(GRID_W)
    col_start = np.clip(cols - WIN_COLS // 2, 0, GRID_W - WIN_COLS)
    key_cols = np.arange(GRID_W)
    in_win = (key_cols[None, :] >= col_start[:, None]) & (key_cols[None, :] < col_start[:, None] + WIN_COLS)
    dx = np.clip(key_cols[None, :] - cols[:, None] + (WIN_COLS - 1), 0, 2 * WIN_COLS - 2)
    dy = np.arange(WIN_ROWS)[None, :] - np.arange(WIN_ROWS)[:, None] + (WIN_ROWS - 1)
    b = rpb.astype(F32)[:, dy][:, :, :, dx]
    b = jnp.where(in_win[None, None, None], b, MASK_NEG)
    b = jnp.transpose(b, (1, 0, 3, 2, 4))
    return b.reshape(WIN_ROWS, n_heads // HEADS_PER_DOT, HEADS_PER_DOT * GRID_W, WIN_ROWS * GRID_W)


def _na_kernel(q_ref, k_ref, v_ref, b_ref, g_ref, o_ref):
    n_keys = WIN_ROWS * GRID_W
    k = k_ref[...].reshape(n_keys, k_ref.shape[-1])
    v = v_ref[...].reshape(n_keys, v_ref.shape[-1])
    pw = HEADS_PER_DOT * NA_HEAD_DIM
    row_head = lax.broadcasted_iota(jnp.int32, (HEADS_PER_DOT * GRID_W, pw), 0) // GRID_W
    col_head = lax.broadcasted_iota(jnp.int32, (HEADS_PER_DOT * GRID_W, pw), 1) // NA_HEAD_DIM
    diag = row_head == col_head
    out_head = lax.broadcasted_iota(jnp.int32, (GRID_W, pw), 1) // NA_HEAD_DIM
    outs = []
    for p in range(k.shape[-1] // pw):
        sl = slice(p * pw, (p + 1) * pw)
        q4 = q_ref[:, sl]
        qbd = jnp.where(diag, jnp.concatenate([q4] * HEADS_PER_DOT, axis=0), jnp.zeros((), BF16))
        s = lax.dot_general(qbd, k[:, sl], (((1,), (1,)), ((), ())), preferred_element_type=F32)
        s = s + b_ref[0, p]
        m = jnp.max(s, axis=-1, keepdims=True)
        e = jnp.exp(s - m)
        l = jnp.sum(e, axis=-1, keepdims=True)
        o = jnp.dot(e.astype(BF16), v[:, sl], preferred_element_type=F32) / l
        acc = jnp.zeros((GRID_W, pw), F32)
        for h in range(HEADS_PER_DOT):
            acc = acc + jnp.where(out_head == h, o[h * GRID_W:(h + 1) * GRID_W], 0.0)
        outs.append(acc)
    y = jnp.concatenate(outs, axis=1)
    o_ref[...] = _rms(y, g_ref[...]).astype(BF16)


def _neighbourhood_attention(qkv, rpb, g):
    seq = qkv.shape[0]
    d_na = qkv.shape[1] // 3
    rows = seq // GRID_W
    bias = _na_bias_table(rpb)
    qkv3 = qkv.reshape(rows, GRID_W, 3 * d_na)

    def win_start(r):
        return jnp.clip(r - WIN_ROWS // 2, 0, rows - WIN_ROWS)

    window = (pl.Element(WIN_ROWS), pl.Element(GRID_W), pl.Element(d_na))

    return pl.pallas_call(
        _na_kernel,
        grid=(rows,),
        in_specs=[
            pl.BlockSpec((GRID_W, d_na), lambda r: (r, 0)),
            pl.BlockSpec(window, lambda r: (win_start(r), 0, d_na)),
            pl.BlockSpec(window, lambda r: (win_start(r), 0, 2 * d_na)),
            pl.BlockSpec((1,) + bias.shape[1:], lambda r: (r - win_start(r), 0, 0, 0)),
            pl.BlockSpec((1, d_na), lambda r: (0, 0)),
        ],
        out_specs=pl.BlockSpec((GRID_W, d_na), lambda r: (r, 0)),
        out_shape=jax.ShapeDtypeStruct((seq, d_na), BF16),
        compiler_params=_params("arbitrary"),
        name="na",
    )(qkv, qkv3, qkv3, bias, g.reshape(1, d_na))


def _outproj_kernel(a_ref, b_ref, w_ref, x_ref, o_ref):
    da = a_ref.shape[-1]
    acc = jnp.dot(a_ref[...], w_ref[:da, :].astype(BF16), preferred_element_type=F32)
    acc = acc + jnp.dot(b_ref[...], w_ref[da:, :].astype(BF16), preferred_element_type=F32)
    o_ref[...] = x_ref[...] + acc


def _out_proj(y_ssm, y_na, w_out, x):
    seq, d_model = x.shape
    da, db = y_ssm.shape[1], y_na.shape[1]
    tm = min(1024, seq)
    tn = 512
    return pl.pallas_call(
        _outproj_kernel,
        grid=(seq // tm, d_model // tn),
        in_specs=[
            pl.BlockSpec((tm, da), lambda i, j: (i, 0)),
            pl.BlockSpec((tm, db), lambda i, j: (i, 0)),
            pl.BlockSpec((da + db, tn), lambda i, j: (0, j)),
            pl.BlockSpec((tm, tn), lambda i, j: (i, j)),
        ],
        out_specs=pl.BlockSpec((tm, tn), lambda i, j: (i, j)),
        out_shape=jax.ShapeDtypeStruct((seq, d_model), F32),
        compiler_params=_params("arbitrary", "arbitrary"),
        name="out_proj",
    )(y_ssm, y_na, w_out, x)


def _router_kernel(x_ref, g_ref, wt_ref, h_ref, a_ref):
    h = _rms(x_ref[...], g_ref[...])
    h_ref[...] = h.astype(BF16)
    logits = lax.dot_general(wt_ref[...], h, (((1,), (1,)), ((), ())),
                             precision=lax.Precision.HIGHEST, preferred_element_type=F32)
    m = jnp.max(logits, axis=0, keepdims=True)
    e = jnp.exp(logits - m)
    a_ref[...] = e / jnp.sum(e, axis=0, keepdims=True)


def _router(x1, g, w_router):
    seq, d_model = x1.shape
    n_exp = w_router.shape[1]
    tm = min(512, seq)
    return pl.pallas_call(
        _router_kernel,
        grid=(seq // tm,),
        in_specs=[
            pl.BlockSpec((tm, d_model), lambda i: (i, 0)),
            pl.BlockSpec((1, d_model), lambda i: (0, 0)),
            pl.BlockSpec((n_exp, d_model), lambda i: (0, 0)),
        ],
        out_specs=[
            pl.BlockSpec((tm, d_model), lambda i: (i, 0)),
            pl.BlockSpec((n_exp, tm), lambda i: (0, i)),
        ],
        out_shape=[
            jax.ShapeDtypeStruct((seq, d_model), BF16),
            jax.ShapeDtypeStruct((n_exp, seq), F32),
        ],
        compiler_params=_params("arbitrary"),
        name="router",
    )(x1, g.reshape(1, d_model), w_router.T)


def _topk_kernel(a_ref, posw_ref, gate_ref, ws_ref, *, cap, blk, win):
    a = a_ref[...]
    n_exp, seq = a.shape
    n_blk = seq // blk
    bits = pltpu.bitcast(a, jnp.int32)

    def bit_step(i, thr):
        cand = thr | jnp.left_shift(jnp.int32(1), 30 - i)
        cnt = jnp.sum((bits >= cand).astype(jnp.int32), axis=-1, keepdims=True)
        return jnp.where(cnt >= cap, cand, thr)

    thr = lax.fori_loop(0, 31, bit_step, jnp.zeros((n_exp, 1), jnp.int32))
    gt = bits > thr
    eq = bits == thr
    need = cap - jnp.sum(gt.astype(jnp.int32), axis=-1, keepdims=True)

    tri = (lax.broadcasted_iota(jnp.int32, (blk, blk), 0)
           <= lax.broadcasted_iota(jnp.int32, (blk, blk), 1)).astype(BF16)
    blk_of_tok = lax.broadcasted_iota(jnp.int32, (seq, n_blk), 0) // blk
    tok_to_blk = (blk_of_tok == lax.broadcasted_iota(jnp.int32, (seq, n_blk), 1)).astype(BF16)
    blk_before = (lax.broadcasted_iota(jnp.int32, (n_blk, n_blk), 0)
                  < lax.broadcasted_iota(jnp.int32, (n_blk, n_blk), 1)).astype(BF16)
    erow = lax.broadcasted_iota(jnp.int32, (2 * n_blk, seq), 0)
    ecol = lax.broadcasted_iota(jnp.int32, (2 * n_blk, seq), 1) // blk
    expand = jnp.where(erow == ecol, 32.0, jnp.where(erow - n_blk == ecol, 1.0, 0.0)).astype(BF16)

    def prefix_counts(mask):
        mb = jnp.where(mask, 1.0, 0.0).astype(BF16)
        local = jnp.concatenate(
            [jnp.dot(mb[:, b * blk:(b + 1) * blk], tri, preferred_element_type=F32) for b in range(n_blk)],
            axis=1)
        per_blk = jnp.dot(mb, tok_to_blk, preferred_element_type=F32)
        start = jnp.dot(per_blk.astype(BF16), blk_before, preferred_element_type=F32)
        hi = jnp.floor(start * (1.0 / 32.0))
        parts = jnp.concatenate([hi, start - 32.0 * hi], axis=1).astype(BF16)
        start_tok = jnp.dot(parts, expand, preferred_element_type=F32)
        return local + start_tok, start, start_tok

    eq_incl, _, _ = prefix_counts(eq)
    sel = gt | (eq & (eq_incl - 1.0 < need.astype(F32)))
    incl, start, start_tok = prefix_counts(sel)

    def window(s):
        return jnp.minimum(jnp.floor(s * (1.0 / MOE_WIN_ALIGN)) * MOE_WIN_ALIGN, float(cap - win))

    posw_ref[...] = jnp.where(sel, (incl - 1.0 - window(start_tok)).astype(jnp.int32), -1)
    gate_ref[...] = jnp.where(sel, a, 0.0)
    ws_ref[...] = window(start).astype(jnp.int32)


def _topk(aff_t, cap, blk, win):
    n_exp, seq = aff_t.shape
    n_blk = seq // blk
    full = lambda *_: (0, 0)
    return pl.pallas_call(
        functools.partial(_topk_kernel, cap=cap, blk=blk, win=win),
        grid=(1,),
        in_specs=[pl.BlockSpec((n_exp, seq), full)],
        out_specs=[pl.BlockSpec((n_exp, seq), full), pl.BlockSpec((n_exp, seq), full),
                   pl.BlockSpec((n_exp, n_blk), full)],
        out_shape=[
            jax.ShapeDtypeStruct((n_exp, seq), jnp.int32),
            jax.ShapeDtypeStruct((n_exp, seq), F32),
            jax.ShapeDtypeStruct((n_exp, n_blk), jnp.int32),
        ],
        compiler_params=_params("arbitrary"),
        name="topk",
    )(aff_t)


def _gather_kernel(ws_ref, h_ref, posw_ref, xe_ref, acc_ref, *, blk, win, n_blk):
    e = pl.program_id(1)
    acc_ref[...] = jnp.zeros_like(acc_ref)
    slot = lax.broadcasted_iota(jnp.int32, (win, blk), 0)

    def body(b, _):
        ws = pl.multiple_of(ws_ref[e * n_blk + b], MOE_WIN_ALIGN)
        onehot = jnp.where(slot == posw_ref[0, pl.ds(b, 1), :], 1.0, 0.0).astype(BF16)
        rows = h_ref[pl.ds(pl.multiple_of(b * blk, blk), blk), :]
        acc_ref[pl.ds(ws, win), :] += jnp.dot(onehot, rows, preferred_element_type=F32)
        return 0

    lax.fori_loop(0, n_blk, body, 0)
    xe_ref[0] = acc_ref[...].astype(BF16)


def _moe_gather(ws_flat, h2, posw3, cap, blk, win):
    seq, d_model = h2.shape
    n_exp, n_blk, _ = posw3.shape
    dh = d_model // 2
    grid_spec = pltpu.PrefetchScalarGridSpec(
        num_scalar_prefetch=1,
        grid=(2, n_exp),
        in_specs=[
            pl.BlockSpec((seq, dh), lambda c, e, ws: (0, c)),
            pl.BlockSpec((1, n_blk, blk), lambda c, e, ws: (e, 0, 0)),
        ],
        out_specs=pl.BlockSpec((1, cap, dh), lambda c, e, ws: (e, 0, c)),
        scratch_shapes=[pltpu.VMEM((cap, dh), F32)],
    )
    return pl.pallas_call(
        functools.partial(_gather_kernel, blk=blk, win=win, n_blk=n_blk),
        grid_spec=grid_spec,
        out_shape=jax.ShapeDtypeStruct((n_exp, cap, d_model), BF16),
        compiler_params=_params("arbitrary", "arbitrary"),
        name="moe_gather",
    )(ws_flat, h2, posw3)


def _ffn_kernel(x_ref, wg_ref, wu_ref, wd_ref, y_ref, acc_ref):
    f = pl.program_id(1)
    x = x_ref[0]
    g = jnp.dot(x, wg_ref[0].astype(BF16), preferred_element_type=F32)
    u = jnp.dot(x, wu_ref[0].astype(BF16), preferred_element_type=F32)
    act = (g * (1.0 / (1.0 + jnp.exp(-g))) * u).astype(BF16)
    part = jnp.dot(act, wd_ref[0].astype(BF16), preferred_element_type=F32)

    @pl.when(f == 0)
    def _():
        acc_ref[...] = part

    @pl.when(f > 0)
    def _():
        acc_ref[...] += part

    @pl.when(f == pl.num_programs(1) - 1)
    def _():
        y_ref[0] = acc_ref[...].astype(BF16)


def _moe_ffn(xe, w_gate, w_up, w_down):
    n_exp, cap, d_model = xe.shape
    d_ff = w_gate.shape[-1]
    tf = 256
    return pl.pallas_call(
        _ffn_kernel,
        grid=(n_exp, d_ff // tf),
        in_specs=[
            pl.BlockSpec((1, cap, d_model), lambda e, f: (e, 0, 0)),
            pl.BlockSpec((1, d_model, tf), lambda e, f: (e, 0, f)),
            pl.BlockSpec((1, d_model, tf), lambda e, f: (e, 0, f)),
            pl.BlockSpec((1, tf, d_model), lambda e, f: (e, f, 0)),
        ],
        out_specs=pl.BlockSpec((1, cap, d_model), lambda e, f: (e, 0, 0)),
        out_shape=jax.ShapeDtypeStruct((n_exp, cap, d_model), BF16),
        scratch_shapes=[pltpu.VMEM((cap, d_model), F32)],
        compiler_params=_params("arbitrary", "arbitrary"),
        name="moe_ffn",
    )(xe, w_gate, w_up, w_down)


def _combine_kernel(ws_ref, ye_ref, x_ref, posw_ref, gate_ref, o_ref, *, win, n_blk):
    b = pl.program_id(1)
    n_exp = ye_ref.shape[0]
    blk = x_ref.shape[0]
    slot = lax.broadcasted_iota(jnp.int32, (blk, win), 1)
    acc = x_ref[...]
    for e in range(n_exp):
        ws = pl.multiple_of(ws_ref[e * n_blk + b], MOE_WIN_ALIGN)
        onehot = jnp.where(slot == posw_ref[:, e:e + 1], 1.0, 0.0).astype(BF16)
        part = jnp.dot(onehot, ye_ref[e, pl.ds(ws, win), :], preferred_element_type=F32)
        acc = acc + part * gate_ref[:, e:e + 1]
    o_ref[...] = acc


def _moe_combine(ws_flat, ye, x1, posw_t, gate_t, blk, win):
    seq, d_model = x1.shape
    n_exp, cap, _ = ye.shape
    n_blk = seq // blk
    dq = d_model // 4
    grid_spec = pltpu.PrefetchScalarGridSpec(
        num_scalar_prefetch=1,
        grid=(4, n_blk),
        in_specs=[
            pl.BlockSpec((n_exp, cap, dq), lambda c, b, ws: (0, 0, c)),
            pl.BlockSpec((blk, dq), lambda c, b, ws: (b, c)),
            pl.BlockSpec((blk, n_exp), lambda c, b, ws: (b, 0)),
            pl.BlockSpec((blk, n_exp), lambda c, b, ws: (b, 0)),
        ],
        out_specs=pl.BlockSpec((blk, dq), lambda c, b, ws: (b, c)),
    )
    return pl.pallas_call(
        functools.partial(_combine_kernel, win=win, n_blk=n_blk),
        grid_spec=grid_spec,
        out_shape=jax.ShapeDtypeStruct((seq, d_model), F32),
        compiler_params=_params("arbitrary", "arbitrary"),
        name="moe_combine",
    )(ws_flat, ye, x1, posw_t, gate_t)


def _final_norm_kernel(x_ref, g_ref, o_ref):
    o_ref[...] = _rms(x_ref[...], g_ref[...])


def _final_norm(x, g):
    seq, d_model = x.shape
    tm = min(512, seq)
    return pl.pallas_call(
        _final_norm_kernel,
        grid=(seq // tm,),
        in_specs=[pl.BlockSpec((tm, d_model), lambda i: (i, 0)), pl.BlockSpec((1, d_model), lambda i: (0, 0))],
        out_specs=pl.BlockSpec((tm, d_model), lambda i: (i, 0)),
        out_shape=jax.ShapeDtypeStruct((seq, d_model), F32),
        compiler_params=_params("arbitrary"),
        name="final_norm",
    )(x, g.reshape(1, d_model))


def _layer(x, norm_mix_g, w_in, fwd, bwd, ssm_d, w_glu, b_glu, na_rpb, g_ssm_out, g_na_out, w_out,
           norm_ffn_g, w_router, w_gate, w_up, w_down):
    seq, d_model = x.shape
    d_ssm = ssm_d.shape[0]
    d_na = g_na_out.shape[0]
    n_exp = w_router.shape[1]
    cap = EC_CAPACITY_FACTOR * seq // n_exp
    blk = min(MOE_TOK_BLOCK, cap // 2)
    win = blk + MOE_WIN_ALIGN

    u, qkv = _in_proj(x, norm_mix_g, w_in, d_ssm, d_na)
    y_ssm = _s5_post(_s5_scan(u, fwd, bwd), u, ssm_d, w_glu, b_glu, g_ssm_out)
    y_na = _neighbourhood_attention(qkv, na_rpb, g_na_out)
    x1 = _out_proj(y_ssm, y_na, w_out, x)

    h2, aff_t = _router(x1, norm_ffn_g, w_router)
    posw, gate, ws = _topk(aff_t, cap, blk, win)
    ws_flat = ws.reshape(-1)
    xe = _moe_gather(ws_flat, h2, posw.reshape(n_exp, seq // blk, blk), cap, blk, win)
    ye = _moe_ffn(xe, w_gate, w_up, w_down)
    return _moe_combine(ws_flat, ye, x1, posw.T, gate.T, blk, win)


def kernel(x, norm_mix_g, w_in, a_re_fwd, a_im_fwd, log_dt_fwd, b_re_fwd, b_im_fwd, c_re_fwd, c_im_fwd, a_re_bwd, a_im_bwd, log_dt_bwd, b_re_bwd, b_im_bwd, c_re_bwd, c_im_bwd, ssm_d, w_glu, b_glu, na_rpb, g_ssm_out, g_na_out, w_out, norm_ffn_g, w_router, w_gate, w_up, w_down, norm_final_g):
    bsz = x.shape[0]
    depth = w_in.shape[0]
    outs = []
    for b in range(bsz):
        xb = x[b]
        for l in range(depth):
            fwd = (a_re_fwd[l], a_im_fwd[l], log_dt_fwd[l], b_re_fwd[l], b_im_fwd[l], c_re_fwd[l], c_im_fwd[l])
            bwd = (a_re_bwd[l], a_im_bwd[l], log_dt_bwd[l], b_re_bwd[l], b_im_bwd[l], c_re_bwd[l], c_im_bwd[l])
            xb = _layer(xb, norm_mix_g[l], w_in[l], fwd, bwd, ssm_d[l], w_glu[l], b_glu[l], na_rpb[l],
                        g_ssm_out[l], g_na_out[l], w_out[l], norm_ffn_g[l], w_router[l],
                        w_gate[l], w_up[l], w_down[l])
        outs.append(_final_norm(xb, norm_final_g))
    return jnp.stack(outs)
```

```python
import functools

import numpy as np
import jax
import jax.numpy as jnp
from jax import lax
from jax.experimental import pallas as pl
from jax.experimental.pallas import tpu as pltpu

F32 = jnp.float32
BF16 = jnp.bfloat16

RMS_EPS = 1e-6
SSM_GROUP = 16
SSM_STATE = 64
NA_HEADS = 16
NA_HEAD_DIM = 64
GRID_W = 64
WIN_ROWS = 8
WIN_COLS = 16
N_EXPERTS = 16
EC_CAPACITY_FACTOR = 2

S5_CHUNK = 16
S5_SEGS = 8
HEADS_PER_DOT = 4
MOE_TOK_BLOCK = 128
MOE_WIN_ALIGN = 16
MOE_WIN = MOE_TOK_BLOCK + MOE_WIN_ALIGN
MASK_NEG = -1e30

VMEM_LIMIT_BYTES = 56 * 1024 * 1024


def _params(*semantics):
    return pltpu.CompilerParams(dimension_semantics=semantics, vmem_limit_bytes=VMEM_LIMIT_BYTES)


def _rms(x, g):
    ms = jnp.mean(x * x, axis=-1, keepdims=True)
    return x * lax.rsqrt(ms + RMS_EPS) * g


def _inproj_kernel(x_ref, g_ref, w_ref, u_ref, qkv_ref, h_scr, *, n_u, n_q, q_scale):
    j = pl.program_id(1)

    @pl.when(j == 0)
    def _():
        h_scr[...] = _rms(x_ref[...], g_ref[...]).astype(BF16)

    acc = jnp.dot(h_scr[...], w_ref[...].astype(BF16), preferred_element_type=F32)

    @pl.when(j < n_u)
    def _():
        u_ref[...] = acc

    @pl.when(j >= n_u)
    def _():
        scale = jnp.where(j < n_u + n_q, q_scale, 1.0).astype(F32)
        qkv_ref[...] = (acc * scale).astype(BF16)


def _in_proj(x, g, w_in, d_ssm, d_na):
    seq, d_model = x.shape
    tm = min(1024, seq)
    tn = 512
    n_u, n_q = d_ssm // tn, d_na // tn
    n_cols = w_in.shape[1] // tn
    kern = functools.partial(_inproj_kernel, n_u=n_u, n_q=n_q, q_scale=NA_HEAD_DIM ** -0.5)
    return pl.pallas_call(
        kern,
        grid=(seq // tm, n_cols),
        in_specs=[
            pl.BlockSpec((tm, d_model), lambda i, j: (i, 0)),
            pl.BlockSpec((1, d_model), lambda i, j: (0, 0)),
            pl.BlockSpec((d_model, tn), lambda i, j: (0, j)),
        ],
        out_specs=[
            pl.BlockSpec((tm, tn), lambda i, j: (i, jnp.minimum(j, n_u - 1))),
            pl.BlockSpec((tm, tn), lambda i, j: (i, jnp.maximum(j - n_u, 0))),
        ],
        out_shape=[
            jax.ShapeDtypeStruct((seq, d_ssm), F32),
            jax.ShapeDtypeStruct((seq, 3 * d_na), BF16),
        ],
        scratch_shapes=[pltpu.VMEM((tm, d_model), BF16)],
        compiler_params=_params("arbitrary", "arbitrary"),
        name="in_proj",
    )(x, g.reshape(1, d_model), w_in)


def _s5_direction_tables(a_re, a_im, log_dt, b_re, b_im, c_re, c_im, n_tiles, reverse):
    t_len = S5_CHUNK
    lam = lax.complex(a_re.astype(F32), a_im.astype(F32))
    dt = jnp.exp(log_dt.astype(F32))[:, None]
    lam_dt = lam * dt
    steps = jnp.arange(t_len + 1, dtype=F32)
    apow = jnp.exp(lam_dt[None] * steps[:, None, None])
    b_bar = ((apow[1] - 1.0) / lam)[..., None] * lax.complex(b_re.astype(F32), b_im.astype(F32))
    c = lax.complex(c_re.astype(F32), c_im.astype(F32))
    lag = jnp.real(jnp.einsum('gop,kgp,gpc->kgoc', c, apow[:t_len], b_bar))
    e_in = (np.arange(t_len)) if reverse else (t_len - 1 - np.arange(t_len))
    st = apow[e_in][:, :, :, None] * b_bar[None]
    e_out = (t_len - np.arange(t_len)) if reverse else (np.arange(t_len) + 1)
    wout = c[:, None, :, :] * apow[e_out].transpose(1, 0, 2)[:, :, None, :]
    a_chunk = apow[t_len]
    a_seg = jnp.exp(lam_dt * float(t_len * n_tiles))
    tiles = jnp.arange(n_tiles, dtype=F32)
    if reverse:
        tiles = tiles[::-1]
    a_tile = jnp.exp(lam_dt[None] * (float(t_len) * tiles)[:, None, None])
    return lag, st, wout, a_chunk, a_seg, a_tile


def _s5_tables(fwd, bwd, n_tiles):
    t_len = S5_CHUNK
    tf = _s5_direction_tables(*fwd, n_tiles=n_tiles, reverse=False)
    tb = _s5_direction_tables(*bwd, n_tiles=n_tiles, reverse=True)
    n_groups, n_ch = tf[0].shape[1], tf[0].shape[2]
    n_state = tf[1].shape[2]
    eye = jnp.eye(2, dtype=F32)

    k = np.arange(t_len)[:, None, None]
    j = np.arange(t_len)[None, :, None]
    t = np.arange(t_len)[None, None, :]
    sel = np.concatenate([(t - j == k), (j - t == k)]).astype(np.float32)
    lag = jnp.concatenate([tf[0], tb[0]], axis=0)
    w_intra = jnp.einsum('kgoc,kjt->gjcto', lag, sel, precision=lax.Precision.HIGHEST)
    w_intra = w_intra.reshape(n_groups, t_len * n_ch, t_len * n_ch)

    st = jnp.stack([jnp.real(tf[1]), jnp.imag(tf[1]), jnp.real(tb[1]), jnp.imag(tb[1])])
    st = st.reshape(4, t_len, n_groups // 2, 2, n_state, n_ch)
    e_tbl = jnp.einsum('ktigpc,gh->itgckhp', st, eye).reshape(n_groups // 2, t_len, 2 * n_ch, 8 * n_state)

    wo = jnp.stack([jnp.real(tf[2]), -jnp.imag(tf[2]), jnp.real(tb[2]), -jnp.imag(tb[2])])
    wo = wo.reshape(4, n_groups // 2, 2, t_len, n_ch, n_state)
    c_tbl = jnp.einsum('kigtop,gh->ikgphto', wo, eye).reshape(n_groups // 2, 8 * n_state, 2 * t_len * n_ch)

    def flat(z):
        return z.reshape(z.shape[:-2] + (n_groups * n_state,))

    ap = jnp.stack([flat(jnp.real(tf[3])), flat(jnp.imag(tf[3])), flat(jnp.real(tb[3])), flat(jnp.imag(tb[3])),
                    flat(jnp.real(tf[4])), flat(jnp.imag(tf[4])), flat(jnp.real(tb[4])), flat(jnp.imag(tb[4]))])
    pw = jnp.stack([flat(jnp.real(tf[5])), flat(jnp.imag(tf[5])), flat(jnp.real(tb[5])), flat(jnp.imag(tb[5]))])
    return w_intra.astype(BF16), e_tbl.astype(BF16), c_tbl.astype(BF16), ap, pw


def _s5_a_kernel(u_ref, wm_ref, e_ref, y_ref, s_ref, w_blk, e_blk, ucat, *, n_tiles, seg_stride):
    n_grp, cw = wm_ref.shape[0], wm_ref.shape[-1]
    lanes = u_ref.shape[-1]

    @pl.when(pl.program_id(0) == 0)
    def _():
        w_blk[...] = jnp.zeros_like(w_blk)
        e_blk[...] = jnp.zeros_like(e_blk)

    for g in range(n_grp):
        for t in range(S5_CHUNK):
            r0 = t * lanes + g * SSM_GROUP
            w_blk[r0:r0 + SSM_GROUP, g * cw:(g + 1) * cw] = wm_ref[g, t * SSM_GROUP:(t + 1) * SSM_GROUP, :]
    n_pair, pr, pc = e_ref.shape[0], e_ref.shape[2], e_ref.shape[3]
    for p in range(n_pair):
        for t in range(S5_CHUNK):
            r0 = t * lanes + p * pr
            e_blk[r0:r0 + pr, p * pc:(p + 1) * pc] = e_ref[p, t]

    def gather(jj, _):
        for t in range(S5_CHUNK):
            rows = [u_ref[pl.ds(S5_CHUNK * (2 * jj + h) + t, S5_SEGS, stride=seg_stride), :] for h in range(2)]
            dst = pl.ds(pl.multiple_of(jj * 2 * S5_SEGS, 2 * S5_SEGS), 2 * S5_SEGS)
            ucat[dst, t * lanes:(t + 1) * lanes] = jnp.concatenate(rows, axis=0).astype(BF16)
        return 0

    lax.fori_loop(0, n_tiles // 2, gather, 0)
    ub = ucat[...]
    y_ref[...] = jnp.dot(ub, w_blk[...], preferred_element_type=F32)
    s_ref[...] = jnp.dot(ub, e_blk[...], preferred_element_type=F32)


def _s5_bc_kernel(s_ref, yi_ref, c_ref, ap_ref, pw_ref, y_ref, sin, perm, ynat, *, n_tiles, seg_stride):
    n_pair, pc = c_ref.shape[0], c_ref.shape[1]
    pw_ = pc // 4
    lanes = y_ref.shape[-1]
    n_grp = 2 * n_pair
    cw = pc // 2

    @pl.when(pl.program_id(0) == 0)
    def _():
        rl = lax.broadcasted_iota(jnp.int32, (cw, n_grp * cw), 0)
        cl = lax.broadcasted_iota(jnp.int32, (cw, n_grp * cw), 1)
        for g in range(n_grp):
            tgt = (rl // SSM_GROUP) * lanes + g * SSM_GROUP + (rl % SSM_GROUP)
            perm[g * cw:(g + 1) * cw, :] = jnp.where(cl == tgt, 1.0, 0.0).astype(BF16)

    chains = [(q, d) for q in range(n_pair) for d in range(2)]

    def plane_lanes(q, d):
        base = q * pc + d * 2 * pw_
        return slice(base, base + pw_), slice(base + pw_, base + 2 * pw_)

    def coef(row, q):
        return ap_ref[row:row + 1, q * pw_:(q + 1) * pw_]

    def tile_rows(n, d):
        j = n if d == 0 else n_tiles - 1 - n
        return pl.ds(pl.multiple_of(j * S5_SEGS, S5_SEGS), S5_SEGS)

    def step(n, carry):
        out = []
        for (q, d), (zr, zi) in zip(chains, carry):
            re, im = plane_lanes(q, d)
            rows = tile_rows(n, d)
            sin[rows, re] = zr
            sin[rows, im] = zi
            ar, ai = coef(2 * d, q), coef(2 * d + 1, q)
            out.append((ar * zr - ai * zi + s_ref[rows, re], ar * zi + ai * zr + s_ref[rows, im]))
        return tuple(out)

    z0 = jnp.zeros((S5_SEGS, pw_), F32)
    ends = lax.fori_loop(0, n_tiles, step, tuple((z0, z0) for _ in chains))

    carries = []
    for (q, d), (zr, zi) in zip(chains, ends):
        sr, si = coef(4 + 2 * d, q), coef(5 + 2 * d, q)
        cr = jnp.zeros((1, pw_), F32)
        ci = jnp.zeros((1, pw_), F32)
        seg_r = [None] * S5_SEGS
        seg_i = [None] * S5_SEGS
        for s in (range(S5_SEGS) if d == 0 else range(S5_SEGS - 1, -1, -1)):
            seg_r[s], seg_i[s] = cr, ci
            cr, ci = (zr[s:s + 1] + sr * cr - si * ci, zi[s:s + 1] + sr * ci + si * cr)
        carries.append((jnp.concatenate(seg_r, axis=0), jnp.concatenate(seg_i, axis=0)))

    def fix(n8, _):
        tiles = pl.ds(pl.multiple_of(n8 * 8, 8), 8)
        for (q, d), (car_r, car_i) in zip(chains, carries):
            re, im = plane_lanes(q, d)
            pr8 = pw_ref[2 * d, tiles, q * pw_:(q + 1) * pw_]
            pi8 = pw_ref[2 * d + 1, tiles, q * pw_:(q + 1) * pw_]
            for r in range(8):
                rows = pl.ds(pl.multiple_of((n8 * 8 + r) * S5_SEGS, S5_SEGS), S5_SEGS)
                pr, pi = pr8[r:r + 1], pi8[r:r + 1]
                sin[rows, re] = sin[rows, re] + (pr * car_r - pi * car_i)
                sin[rows, im] = sin[rows, im] + (pr * car_i + pi * car_r)
        return 0

    lax.fori_loop(0, n_tiles // 8, fix, 0)

    parts = []
    for q in range(n_pair):
        lhs = sin[:, q * pc:(q + 1) * pc].astype(BF16)
        parts.append(yi_ref[:, q * pc:(q + 1) * pc] + jnp.dot(lhs, c_ref[q], preferred_element_type=F32))
    y_chunk = jnp.concatenate(parts, axis=1).astype(BF16)
    ynat[...] = jnp.dot(y_chunk, perm[...], preferred_element_type=F32)

    def scatter(j, _):
        src = pl.ds(pl.multiple_of(j * S5_SEGS, S5_SEGS), S5_SEGS)
        for t in range(S5_CHUNK):
            y_ref[pl.ds(S5_CHUNK * j + t, S5_SEGS, stride=seg_stride), :] = ynat[src, t * lanes:(t + 1) * lanes]
        return 0

    lax.fori_loop(0, n_tiles, scatter, 0)


def _s5_scan(u, fwd, bwd):
    seq, d_ssm = u.shape
    n_groups = d_ssm // SSM_GROUP
    n_state = SSM_STATE
    cw = S5_CHUNK * SSM_GROUP
    n_rows = seq // S5_CHUNK
    n_tiles = n_rows // S5_SEGS
    lanes = 128
    gpb = lanes // SSM_GROUP
    n_blocks = d_ssm // lanes
    bw = gpb * cw
    sw = gpb * 4 * n_state
    seg_stride = n_tiles * S5_CHUNK

    w_intra, e_tbl, c_tbl, ap, pw = _s5_tables(fwd, bwd, n_tiles)

    y_intra, s_loc = pl.pallas_call(
        functools.partial(_s5_a_kernel, n_tiles=n_tiles, seg_stride=seg_stride),
        grid=(n_blocks,),
        in_specs=[
            pl.BlockSpec((seq, lanes), lambda i: (0, i)),
            pl.BlockSpec((gpb, cw, cw), lambda i: (i, 0, 0)),
            pl.BlockSpec((gpb // 2,) + e_tbl.shape[1:], lambda i: (i, 0, 0, 0)),
        ],
        out_specs=[
            pl.BlockSpec((n_rows, bw), lambda i: (0, i)),
            pl.BlockSpec((n_rows, sw), lambda i: (0, i)),
        ],
        out_shape=[
            jax.ShapeDtypeStruct((n_rows, n_blocks * bw), F32),
            jax.ShapeDtypeStruct((n_rows, n_blocks * sw), F32),
        ],
        scratch_shapes=[
            pltpu.VMEM((S5_CHUNK * lanes, bw), BF16),
            pltpu.VMEM((S5_CHUNK * lanes, sw), BF16),
            pltpu.VMEM((n_rows, S5_CHUNK * lanes), BF16),
        ],
        compiler_params=_params("arbitrary"),
        name="s5_a",
    )(u, w_intra, e_tbl)

    pl_lanes = gpb * n_state
    return pl.pallas_call(
        functools.partial(_s5_bc_kernel, n_tiles=n_tiles, seg_stride=seg_stride),
        grid=(n_blocks,),
        in_specs=[
            pl.BlockSpec((n_rows, sw), lambda i: (0, i)),
            pl.BlockSpec((n_rows, bw), lambda i: (0, i)),
            pl.BlockSpec((gpb // 2,) + c_tbl.shape[1:], lambda i: (i, 0, 0)),
            pl.BlockSpec((8, pl_lanes), lambda i: (0, i)),
            pl.BlockSpec((4, n_tiles, pl_lanes), lambda i: (0, 0, i)),
        ],
        out_specs=pl.BlockSpec((seq, lanes), lambda i: (0, i)),
        out_shape=jax.ShapeDtypeStruct((seq, d_ssm), F32),
        scratch_shapes=[
            pltpu.VMEM((n_rows, sw), F32),
            pltpu.VMEM((bw, bw), BF16),
            pltpu.VMEM((n_rows, bw), F32),
        ],
        compiler_params=_params("arbitrary"),
        name="s5_bc",
    )(s_loc, y_intra, c_tbl, ap, pw)


def _s5_post_kernel(y_ref, u_ref, d_ref, w_ref, b_ref, g_ref, o_ref):
    y = y_ref[...] + d_ref[...] * u_ref[...]
    c0 = np.float32(np.sqrt(2.0 / np.pi))
    y = 0.5 * y * (1.0 + jnp.tanh(c0 * (y + np.float32(0.044715) * (y * y * y))))
    z = jnp.dot(y.astype(BF16), w_ref[...].astype(BF16), preferred_element_type=F32) + b_ref[...]
    o = y * (1.0 / (1.0 + jnp.exp(-z)))
    o_ref[...] = _rms(o, g_ref[...]).astype(BF16)


def _s5_post(y, u, d_skip, w_glu, b_glu, g):
    seq, d = y.shape
    tm = min(512, seq)
    row = lambda i: (i, 0)
    fix = lambda i: (0, 0)
    return pl.pallas_call(
        _s5_post_kernel,
        grid=(seq // tm,),
        in_specs=[
            pl.BlockSpec((tm, d), row), pl.BlockSpec((tm, d), row), pl.BlockSpec((1, d), fix),
            pl.BlockSpec((d, d), fix), pl.BlockSpec((1, d), fix), pl.BlockSpec((1, d), fix),
        ],
        out_specs=pl.BlockSpec((tm, d), row),
        out_shape=jax.ShapeDtypeStruct((seq, d), BF16),
        compiler_params=_params("arbitrary"),
        name="s5_post",
    )(y, u, d_skip.reshape(1, d), w_glu, b_glu.reshape(1, d), g.reshape(1, d))


def _na_bias_table(rpb):
    n_heads = rpb.shape[0]
    cols = np.arange(GRID_W)
    col_start = np.clip(cols - WIN_COLS // 2, 0, GRID_W - WIN_COLS)
    key_cols = np.arange(GRID_W)
    in_win = (key_cols[None, :] >= col_start[:, None]) & (key_cols[None, :] < col_start[:, None] + WIN_COLS)
    dx = key_cols[None, :] - cols[:, None] + (WIN_COLS - 1)
    dy = np.arange(WIN_ROWS)[None, :] - np.arange(WIN_ROWS)[:, None] + (WIN_ROWS - 1)
    pick_x = (dx[:, :, None] == np.arange(2 * WIN_COLS - 1)).astype(np.float32)
    pick_y = (dy[:, :, None] == np.arange(2 * WIN_ROWS - 1)).astype(np.float32)
    hp = lax.Precision.HIGHEST
    b = jnp.einsum('hyx,ckx->hyck', rpb.astype(F32), pick_x, precision=hp)
    b = jnp.einsum('viy,hyck->vhcik', pick_y, b, precision=hp)
    b = jnp.where(in_win[None, None, :, None, :], b, MASK_NEG)
    return b.reshape(WIN_ROWS, n_heads // HEADS_PER_DOT, HEADS_PER_DOT * GRID_W, WIN_ROWS * GRID_W)


def _na_kernel(q_ref, k_ref, v_ref, b_ref, g_ref, o_ref):
    n_keys = WIN_ROWS * GRID_W
    k = k_ref[...].reshape(n_keys, k_ref.shape[-1])
    v = v_ref[...].reshape(n_keys, v_ref.shape[-1])
    pw = HEADS_PER_DOT * NA_HEAD_DIM
    row_head = lax.broadcasted_iota(jnp.int32, (HEADS_PER_DOT * GRID_W, pw), 0) // GRID_W
    col_head = lax.broadcasted_iota(jnp.int32, (HEADS_PER_DOT * GRID_W, pw), 1) // NA_HEAD_DIM
    diag = row_head == col_head
    out_head = lax.broadcasted_iota(jnp.int32, (GRID_W, pw), 1) // NA_HEAD_DIM
    outs = []
    for p in range(k.shape[-1] // pw):
        sl = slice(p * pw, (p + 1) * pw)
        q4 = q_ref[:, sl]
        qbd = jnp.where(diag, jnp.concatenate([q4] * HEADS_PER_DOT, axis=0), jnp.zeros((), BF16))
        s = lax.dot_general(qbd, k[:, sl], (((1,), (1,)), ((), ())), preferred_element_type=F32)
        s = s + b_ref[0, p]
        m = jnp.max(s, axis=-1, keepdims=True)
        e = jnp.exp(s - m)
        l = jnp.sum(e, axis=-1, keepdims=True)
        o = jnp.dot(e.astype(BF16), v[:, sl], preferred_element_type=F32) / l
        acc = jnp.zeros((GRID_W, pw), F32)
        for h in range(HEADS_PER_DOT):
            acc = acc + jnp.where(out_head == h, o[h * GRID_W:(h + 1) * GRID_W], 0.0)
        outs.append(acc)
    y = jnp.concatenate(outs, axis=1)
    o_ref[...] = _rms(y, g_ref[...]).astype(BF16)


def _neighbourhood_attention(qkv, rpb, g):
    seq = qkv.shape[0]
    d_na = qkv.shape[1] // 3
    rows = seq // GRID_W
    bias = _na_bias_table(rpb)
    qkv3 = qkv.reshape(rows, GRID_W, 3 * d_na)

    def win_start(r):
        return jnp.clip(r - WIN_ROWS // 2, 0, rows - WIN_ROWS)

    window = (pl.Element(WIN_ROWS), pl.Element(GRID_W), pl.Element(d_na))

    return pl.pallas_call(
        _na_kernel,
        grid=(rows,),
        in_specs=[
            pl.BlockSpec((GRID_W, d_na), lambda r: (r, 0)),
            pl.BlockSpec(window, lambda r: (win_start(r), 0, d_na)),
            pl.BlockSpec(window, lambda r: (win_start(r), 0, 2 * d_na)),
            pl.BlockSpec((1,) + bias.shape[1:], lambda r: (r - win_start(r), 0, 0, 0)),
            pl.BlockSpec((1, d_na), lambda r: (0, 0)),
        ],
        out_specs=pl.BlockSpec((GRID_W, d_na), lambda r: (r, 0)),
        out_shape=jax.ShapeDtypeStruct((seq, d_na), BF16),
        compiler_params=_params("arbitrary"),
        name="na",
    )(qkv, qkv3, qkv3, bias, g.reshape(1, d_na))


def _outproj_kernel(a_ref, b_ref, w_ref, x_ref, o_ref):
    da = a_ref.shape[-1]
    acc = jnp.dot(a_ref[...], w_ref[:da, :].astype(BF16), preferred_element_type=F32)
    acc = acc + jnp.dot(b_ref[...], w_ref[da:, :].astype(BF16), preferred_element_type=F32)
    o_ref[...] = x_ref[...] + acc


def _out_proj(y_ssm, y_na, w_out, x):
    seq, d_model = x.shape
    da, db = y_ssm.shape[1], y_na.shape[1]
    tm = min(1024, seq)
    tn = 512
    return pl.pallas_call(
        _outproj_kernel,
        grid=(seq // tm, d_model // tn),
        in_specs=[
            pl.BlockSpec((tm, da), lambda i, j: (i, 0)),
            pl.BlockSpec((tm, db), lambda i, j: (i, 0)),
            pl.BlockSpec((da + db, tn), lambda i, j: (0, j)),
            pl.BlockSpec((tm, tn), lambda i, j: (i, j)),
        ],
        out_specs=pl.BlockSpec((tm, tn), lambda i, j: (i, j)),
        out_shape=jax.ShapeDtypeStruct((seq, d_model), F32),
        compiler_params=_params("arbitrary", "arbitrary"),
        name="out_proj",
    )(y_ssm, y_na, w_out, x)


def _router_kernel(x_ref, g_ref, wt_ref, h_ref, a_ref):
    h = _rms(x_ref[...], g_ref[...])
    h_ref[...] = h.astype(BF16)
    logits = lax.dot_general(wt_ref[...], h, (((1,), (1,)), ((), ())),
                             precision=lax.Precision.HIGHEST, preferred_element_type=F32)
    m = jnp.max(logits, axis=0, keepdims=True)
    e = jnp.exp(logits - m)
    a_ref[...] = e / jnp.sum(e, axis=0, keepdims=True)


def _router(x1, g, w_router):
    seq, d_model = x1.shape
    n_exp = w_router.shape[1]
    tm = min(512, seq)
    return pl.pallas_call(
        _router_kernel,
        grid=(seq // tm,),
        in_specs=[
            pl.BlockSpec((tm, d_model), lambda i: (i, 0)),
            pl.BlockSpec((1, d_model), lambda i: (0, 0)),
            pl.BlockSpec((n_exp, d_model), lambda i: (0, 0)),
        ],
        out_specs=[
            pl.BlockSpec((tm, d_model), lambda i: (i, 0)),
            pl.BlockSpec((n_exp, tm), lambda i: (0, i)),
        ],
        out_shape=[
            jax.ShapeDtypeStruct((seq, d_model), BF16),
            jax.ShapeDtypeStruct((n_exp, seq), F32),
        ],
        compiler_params=_params("arbitrary"),
        name="router",
    )(x1, g.reshape(1, d_model), w_router.T)


def _topk_kernel(a_ref, posw_ref, gate_ref, ws_ref, *, cap, blk, win):
    a = a_ref[...]
    n_exp, seq = a.shape
    n_blk = seq // blk
    bits = pltpu.bitcast(a, jnp.int32)

    def bit_step(i, thr):
        cand = thr | jnp.left_shift(jnp.int32(1), 30 - i)
        cnt = jnp.sum((bits >= cand).astype(jnp.int32), axis=-1, keepdims=True)
        return jnp.where(cnt >= cap, cand, thr)

    thr = lax.fori_loop(0, 31, bit_step, jnp.zeros((n_exp, 1), jnp.int32))
    gt = bits > thr
    eq = bits == thr
    need = cap - jnp.sum(gt.astype(jnp.int32), axis=-1, keepdims=True)

    tri = (lax.broadcasted_iota(jnp.int32, (blk, blk), 0)
           <= lax.broadcasted_iota(jnp.int32, (blk, blk), 1)).astype(BF16)
    blk_of_tok = lax.broadcasted_iota(jnp.int32, (seq, n_blk), 0) // blk
    tok_to_blk = (blk_of_tok == lax.broadcasted_iota(jnp.int32, (seq, n_blk), 1)).astype(BF16)
    blk_before = (lax.broadcasted_iota(jnp.int32, (n_blk, n_blk), 0)
                  < lax.broadcasted_iota(jnp.int32, (n_blk, n_blk), 1)).astype(BF16)
    erow = lax.broadcasted_iota(jnp.int32, (2 * n_blk, seq), 0)
    ecol = lax.broadcasted_iota(jnp.int32, (2 * n_blk, seq), 1) // blk
    expand = jnp.where(erow == ecol, 32.0, jnp.where(erow - n_blk == ecol, 1.0, 0.0)).astype(BF16)

    def prefix_counts(mask):
        mb = jnp.where(mask, 1.0, 0.0).astype(BF16)
        local = jnp.concatenate(
            [jnp.dot(mb[:, b * blk:(b + 1) * blk], tri, preferred_element_type=F32) for b in range(n_blk)],
            axis=1)
        per_blk = jnp.dot(mb, tok_to_blk, preferred_element_type=F32)
        start = jnp.dot(per_blk.astype(BF16), blk_before, preferred_element_type=F32)
        hi = jnp.floor(start * (1.0 / 32.0))
        parts = jnp.concatenate([hi, start - 32.0 * hi], axis=1).astype(BF16)
        start_tok = jnp.dot(parts, expand, preferred_element_type=F32)
        return local + start_tok, start, start_tok

    eq_incl, _, _ = prefix_counts(eq)
    sel = gt | (eq & (eq_incl - 1.0 < need.astype(F32)))
    incl, start, start_tok = prefix_counts(sel)

    def window(s):
        return jnp.minimum(jnp.floor(s * (1.0 / MOE_WIN_ALIGN)) * MOE_WIN_ALIGN, float(cap - win))

    posw_ref[...] = jnp.where(sel, (incl - 1.0 - window(start_tok)).astype(jnp.int32), -1)
    gate_ref[...] = jnp.where(sel, a, 0.0)
    ws_ref[...] = window(start).astype(jnp.int32)


def _topk(aff_t, cap, blk, win):
    n_exp, seq = aff_t.shape
    n_blk = seq // blk
    full = lambda *_: (0, 0)
    return pl.pallas_call(
        functools.partial(_topk_kernel, cap=cap, blk=blk, win=win),
        grid=(1,),
        in_specs=[pl.BlockSpec((n_exp, seq), full)],
        out_specs=[pl.BlockSpec((n_exp, seq), full), pl.BlockSpec((n_exp, seq), full),
                   pl.BlockSpec((n_exp, n_blk), full)],
        out_shape=[
            jax.ShapeDtypeStruct((n_exp, seq), jnp.int32),
            jax.ShapeDtypeStruct((n_exp, seq), F32),
            jax.ShapeDtypeStruct((n_exp, n_blk), jnp.int32),
        ],
        compiler_params=_params("arbitrary"),
        name="topk",
    )(aff_t)


def _gather_kernel(ws_ref, h_ref, posw_ref, xe_ref, acc_ref, *, blk, win, n_blk):
    e = pl.program_id(1)
    acc_ref[...] = jnp.zeros_like(acc_ref)
    slot = lax.broadcasted_iota(jnp.int32, (win, blk), 0)

    def body(b, _):
        ws = pl.multiple_of(ws_ref[e * n_blk + b], MOE_WIN_ALIGN)
        onehot = jnp.where(slot == posw_ref[0, pl.ds(b, 1), :], 1.0, 0.0).astype(BF16)
        rows = h_ref[pl.ds(pl.multiple_of(b * blk, blk), blk), :]
        acc_ref[pl.ds(ws, win), :] += jnp.dot(onehot, rows, preferred_element_type=F32)
        return 0

    lax.fori_loop(0, n_blk, body, 0)
    xe_ref[0] = acc_ref[...].astype(BF16)


def _moe_gather(ws_flat, h2, posw3, cap, blk, win):
    seq, d_model = h2.shape
    n_exp, n_blk, _ = posw3.shape
    dh = d_model // 2
    grid_spec = pltpu.PrefetchScalarGridSpec(
        num_scalar_prefetch=1,
        grid=(2, n_exp),
        in_specs=[
            pl.BlockSpec((seq, dh), lambda c, e, ws: (0, c)),
            pl.BlockSpec((1, n_blk, blk), lambda c, e, ws: (e, 0, 0)),
        ],
        out_specs=pl.BlockSpec((1, cap, dh), lambda c, e, ws: (e, 0, c)),
        scratch_shapes=[pltpu.VMEM((cap, dh), F32)],
    )
    return pl.pallas_call(
        functools.partial(_gather_kernel, blk=blk, win=win, n_blk=n_blk),
        grid_spec=grid_spec,
        out_shape=jax.ShapeDtypeStruct((n_exp, cap, d_model), BF16),
        compiler_params=_params("arbitrary", "arbitrary"),
        name="moe_gather",
    )(ws_flat, h2, posw3)


def _ffn_kernel(x_ref, wg_ref, wu_ref, wd_ref, y_ref, acc_ref):
    f = pl.program_id(1)
    x = x_ref[0]
    g = jnp.dot(x, wg_ref[0].astype(BF16), preferred_element_type=F32)
    u = jnp.dot(x, wu_ref[0].astype(BF16), preferred_element_type=F32)
    act = (g * (1.0 / (1.0 + jnp.exp(-g))) * u).astype(BF16)
    part = jnp.dot(act, wd_ref[0].astype(BF16), preferred_element_type=F32)

    @pl.when(f == 0)
    def _():
        acc_ref[...] = part

    @pl.when(f > 0)
    def _():
        acc_ref[...] += part

    @pl.when(f == pl.num_programs(1) - 1)
    def _():
        y_ref[0] = acc_ref[...].astype(BF16)


def _moe_ffn(xe, w_gate, w_up, w_down):
    n_exp, cap, d_model = xe.shape
    d_ff = w_gate.shape[-1]
    tf = 256
    return pl.pallas_call(
        _ffn_kernel,
        grid=(n_exp, d_ff // tf),
        in_specs=[
            pl.BlockSpec((1, cap, d_model), lambda e, f: (e, 0, 0)),
            pl.BlockSpec((1, d_model, tf), lambda e, f: (e, 0, f)),
            pl.BlockSpec((1, d_model, tf), lambda e, f: (e, 0, f)),
            pl.BlockSpec((1, tf, d_model), lambda e, f: (e, f, 0)),
        ],
        out_specs=pl.BlockSpec((1, cap, d_model), lambda e, f: (e, 0, 0)),
        out_shape=jax.ShapeDtypeStruct((n_exp, cap, d_model), BF16),
        scratch_shapes=[pltpu.VMEM((cap, d_model), F32)],
        compiler_params=_params("arbitrary", "arbitrary"),
        name="moe_ffn",
    )(xe, w_gate, w_up, w_down)


def _combine_kernel(ws_ref, ye_ref, x_ref, posw_ref, gate_ref, o_ref, *, win, n_blk):
    b = pl.program_id(1)
    n_exp = ye_ref.shape[0]
    blk = x_ref.shape[0]
    slot = lax.broadcasted_iota(jnp.int32, (blk, win), 1)
    acc = x_ref[...]
    for e in range(n_exp):
        ws = pl.multiple_of(ws_ref[e * n_blk + b], MOE_WIN_ALIGN)
        onehot = jnp.where(slot == posw_ref[:, e:e + 1], 1.0, 0.0).astype(BF16)
        part = jnp.dot(onehot, ye_ref[e, pl.ds(ws, win), :], preferred_element_type=F32)
        acc = acc + part * gate_ref[:, e:e + 1]
    o_ref[...] = acc


def _moe_combine(ws_flat, ye, x1, posw_t, gate_t, blk, win):
    seq, d_model = x1.shape
    n_exp, cap, _ = ye.shape
    n_blk = seq // blk
    dq = d_model // 4
    grid_spec = pltpu.PrefetchScalarGridSpec(
        num_scalar_prefetch=1,
        grid=(4, n_blk),
        in_specs=[
            pl.BlockSpec((n_exp, cap, dq), lambda c, b, ws: (0, 0, c)),
            pl.BlockSpec((blk, dq), lambda c, b, ws: (b, c)),
            pl.BlockSpec((blk, n_exp), lambda c, b, ws: (b, 0)),
            pl.BlockSpec((blk, n_exp), lambda c, b, ws: (b, 0)),
        ],
        out_specs=pl.BlockSpec((blk, dq), lambda c, b, ws: (b, c)),
    )
    return pl.pallas_call(
        functools.partial(_combine_kernel, win=win, n_blk=n_blk),
        grid_spec=grid_spec,
        out_shape=jax.ShapeDtypeStruct((seq, d_model), F32),
        compiler_params=_params("arbitrary", "arbitrary"),
        name="moe_combine",
    )(ws_flat, ye, x1, posw_t, gate_t)


def _final_norm_kernel(x_ref, g_ref, o_ref):
    o_ref[...] = _rms(x_ref[...], g_ref[...])


def _final_norm(x, g):
    seq, d_model = x.shape
    tm = min(512, seq)
    return pl.pallas_call(
        _final_norm_kernel,
        grid=(seq // tm,),
        in_specs=[pl.BlockSpec((tm, d_model), lambda i: (i, 0)), pl.BlockSpec((1, d_model), lambda i: (0, 0))],
        out_specs=pl.BlockSpec((tm, d_model), lambda i: (i, 0)),
        out_shape=jax.ShapeDtypeStruct((seq, d_model), F32),
        compiler_params=_params("arbitrary"),
        name="final_norm",
    )(x, g.reshape(1, d_model))


def _layer(x, norm_mix_g, w_in, fwd, bwd, ssm_d, w_glu, b_glu, na_rpb, g_ssm_out, g_na_out, w_out,
           norm_ffn_g, w_router, w_gate, w_up, w_down):
    seq, d_model = x.shape
    d_ssm = ssm_d.shape[0]
    d_na = g_na_out.shape[0]
    n_exp = w_router.shape[1]
    cap = EC_CAPACITY_FACTOR * seq // n_exp
    blk = min(MOE_TOK_BLOCK, cap // 2)
    win = blk + MOE_WIN_ALIGN

    u, qkv = _in_proj(x, norm_mix_g, w_in, d_ssm, d_na)
    y_ssm = _s5_post(_s5_scan(u, fwd, bwd), u, ssm_d, w_glu, b_glu, g_ssm_out)
    y_na = _neighbourhood_attention(qkv, na_rpb, g_na_out)
    x1 = _out_proj(y_ssm, y_na, w_out, x)

    h2, aff_t = _router(x1, norm_ffn_g, w_router)
    posw, gate, ws = _topk(aff_t, cap, blk, win)
    ws_flat = ws.reshape(-1)
    xe = _moe_gather(ws_flat, h2, posw.reshape(n_exp, seq // blk, blk), cap, blk, win)
    ye = _moe_ffn(xe, w_gate, w_up, w_down)
    return _moe_combine(ws_flat, ye, x1, posw.T, gate.T, blk, win)


def kernel(x, norm_mix_g, w_in, a_re_fwd, a_im_fwd, log_dt_fwd, b_re_fwd, b_im_fwd, c_re_fwd, c_im_fwd, a_re_bwd, a_im_bwd, log_dt_bwd, b_re_bwd, b_im_bwd, c_re_bwd, c_im_bwd, ssm_d, w_glu, b_glu, na_rpb, g_ssm_out, g_na_out, w_out, norm_ffn_g, w_router, w_gate, w_up, w_down, norm_final_g):
    bsz = x.shape[0]
    depth = w_in.shape[0]
    outs = []
    for b in range(bsz):
        xb = x[b]
        for l in range(depth):
            fwd = (a_re_fwd[l], a_im_fwd[l], log_dt_fwd[l], b_re_fwd[l], b_im_fwd[l], c_re_fwd[l], c_im_fwd[l])
            bwd = (a_re_bwd[l], a_im_bwd[l], log_dt_bwd[l], b_re_bwd[l], b_im_bwd[l], c_re_bwd[l], c_im_bwd[l])
            xb = _layer(xb, norm_mix_g[l], w_in[l], fwd, bwd, ssm_d[l], w_glu[l], b_glu[l], na_rpb[l],
                        g_ssm_out[l], g_na_out[l], w_out[l], norm_ffn_g[l], w_router[l],
                        w_gate[l], w_up[l], w_down[l])
        outs.append(_final_norm(xb, norm_final_g))
    return jnp.stack(outs)
```

```python
import functools

import numpy as np
import jax
import jax.numpy as jnp
from jax import lax
from jax.experimental import pallas as pl
from jax.experimental.pallas import tpu as pltpu

F32 = jnp.float32
BF16 = jnp.bfloat16

RMS_EPS = 1e-6
SSM_GROUP = 16
SSM_STATE = 64
NA_HEADS = 16
NA_HEAD_DIM = 64
GRID_W = 64
WIN_ROWS = 8
WIN_COLS = 16
N_EXPERTS = 16
EC_CAPACITY_FACTOR = 2

S5_CHUNK = 16
S5_SEGS = 8
HEADS_PER_DOT = 4
MOE_TOK_BLOCK = 128
MOE_WIN_ALIGN = 16
MOE_WIN = MOE_TOK_BLOCK + MOE_WIN_ALIGN
MASK_NEG = -1e30

VMEM_LIMIT_BYTES = 56 * 1024 * 1024


def _params(*semantics):
    return pltpu.CompilerParams(dimension_semantics=semantics, vmem_limit_bytes=VMEM_LIMIT_BYTES)


def _rms(x, g):
    ms = jnp.mean(x * x, axis=-1, keepdims=True)
    return x * lax.rsqrt(ms + RMS_EPS) * g


def _inproj_kernel(x_ref, g_ref, w_ref, u_ref, qkv_ref, h_scr, *, n_u, n_q, q_scale):
    j = pl.program_id(1)

    @pl.when(j == 0)
    def _():
        h_scr[...] = _rms(x_ref[...], g_ref[...]).astype(BF16)

    acc = jnp.dot(h_scr[...], w_ref[...].astype(BF16), preferred_element_type=F32)

    @pl.when(j < n_u)
    def _():
        u_ref[...] = acc

    @pl.when(j >= n_u)
    def _():
        scale = jnp.where(j < n_u + n_q, q_scale, 1.0).astype(F32)
        qkv_ref[...] = (acc * scale).astype(BF16)


def _in_proj(x, g, w_in, d_ssm, d_na):
    seq, d_model = x.shape
    tm = min(1024, seq)
    tn = 512
    n_u, n_q = d_ssm // tn, d_na // tn
    n_cols = w_in.shape[1] // tn
    kern = functools.partial(_inproj_kernel, n_u=n_u, n_q=n_q, q_scale=NA_HEAD_DIM ** -0.5)
    return pl.pallas_call(
        kern,
        grid=(seq // tm, n_cols),
        in_specs=[
            pl.BlockSpec((tm, d_model), lambda i, j: (i, 0)),
            pl.BlockSpec((1, d_model), lambda i, j: (0, 0)),
            pl.BlockSpec((d_model, tn), lambda i, j: (0, j)),
        ],
        out_specs=[
            pl.BlockSpec((tm, tn), lambda i, j: (i, jnp.minimum(j, n_u - 1))),
            pl.BlockSpec((tm, tn), lambda i, j: (i, jnp.maximum(j - n_u, 0))),
        ],
        out_shape=[
            jax.ShapeDtypeStruct((seq, d_ssm), F32),
            jax.ShapeDtypeStruct((seq, 3 * d_na), BF16),
        ],
        scratch_shapes=[pltpu.VMEM((tm, d_model), BF16)],
        compiler_params=_params("arbitrary", "arbitrary"),
        name="in_proj",
    )(x, g.reshape(1, d_model), w_in)


def _s5_direction_tables(a_re, a_im, log_dt, b_re, b_im, c_re, c_im, n_tiles, reverse):
    t_len = S5_CHUNK
    n_ch = b_re.shape[-1]
    hp = lax.Precision.HIGHEST
    dt = jnp.exp(log_dt.astype(F32))[:, None]
    xr, xi = a_re.astype(F32) * dt, a_im.astype(F32) * dt

    def power(xr_, xi_, k):
        k = k.reshape((-1,) + (1,) * xr_.ndim)
        mag = jnp.exp(xr_[None] * k)
        return mag * jnp.cos(xi_[None] * k), mag * jnp.sin(xi_[None] * k)

    pr, pi = power(xr, xi, jnp.arange(t_len + 1, dtype=F32))
    nr, ni = pr[1] - 1.0, pi[1]
    den = a_re * a_re + a_im * a_im
    qr, qi = (nr * a_re + ni * a_im) / den, (ni * a_re - nr * a_im) / den
    bb_r, bb_i = _cmul(qr[:, None, :], qi[:, None, :], jnp.swapaxes(b_re, 1, 2), jnp.swapaxes(b_im, 1, 2))
    ct_r, ct_i = jnp.swapaxes(c_re, 1, 2).astype(F32), jnp.swapaxes(c_im, 1, 2).astype(F32)
    apt_r, apt_i = jnp.transpose(pr, (1, 2, 0)), jnp.transpose(pi, (1, 2, 0))

    def c_times_power(n_lanes, lag_of_lane):
        lane = np.arange(n_lanes)
        pick_c = (lane[None, :] % n_ch == np.arange(n_ch)[:, None]).astype(np.float32)
        pick_k = (lag_of_lane[lane // n_ch][None, :] == np.arange(t_len + 1)[:, None]).astype(np.float32)
        cr = jnp.einsum('gpo,oq->gpq', ct_r, pick_c, precision=hp)
        ci = jnp.einsum('gpo,oq->gpq', ct_i, pick_c, precision=hp)
        ar = jnp.einsum('gpk,kq->gpq', apt_r, pick_k, precision=hp)
        ai = jnp.einsum('gpk,kq->gpq', apt_i, pick_k, precision=hp)
        return _cmul(cr, ci, ar, ai)

    slots = np.arange(2 * t_len)
    lag_of_slot = (t_len - 1 - slots) if reverse else (slots - (t_len - 1))
    lag_of_slot = np.where((lag_of_slot >= 0) & (lag_of_slot < t_len), lag_of_slot, -1)
    yr, yi = c_times_power(2 * t_len * n_ch, lag_of_slot)
    strip_lhs = jnp.concatenate([bb_r, -bb_i], axis=-1)
    strip_rhs = jnp.concatenate([yr, yi], axis=1)
    e_in = (np.arange(t_len)) if reverse else (t_len - 1 - np.arange(t_len))
    st_r, st_i = _cmul(jnp.swapaxes(pr[e_in], 0, 1)[:, :, None, :], jnp.swapaxes(pi[e_in], 0, 1)[:, :, None, :],
                       bb_r[:, None], bb_i[:, None])
    e_out = (t_len - np.arange(t_len)) if reverse else (np.arange(t_len) + 1)
    wr, wi = c_times_power(t_len * n_ch, e_out)
    xrf, xif = xr.reshape(-1), xi.reshape(-1)
    tiles = jnp.arange(n_tiles, dtype=F32)
    if reverse:
        tiles = tiles[::-1]
    a_chunk = power(xrf, xif, jnp.full((1,), float(t_len), F32))
    a_seg = power(xrf, xif, jnp.full((1,), float(t_len * n_tiles), F32))
    a_tile = power(xrf, xif, float(t_len) * tiles)
    return (strip_lhs, strip_rhs), (st_r, st_i), (wr, -wi), a_chunk, a_seg, a_tile


def _cmul(ar, ai, br, bi):
    return ar * br - ai * bi, ar * bi + ai * br


def _s5_tables(fwd, bwd, n_tiles):
    tf = _s5_direction_tables(*fwd, n_tiles=n_tiles, reverse=False)
    tb = _s5_direction_tables(*bwd, n_tiles=n_tiles, reverse=True)
    n_groups = tf[1][0].shape[0]

    strip = jnp.einsum('gck,gkq->gcq', jnp.concatenate([tf[0][0], tb[0][0]], axis=-1),
                       jnp.concatenate([tf[0][1], tb[0][1]], axis=1), precision=lax.Precision.HIGHEST)

    odd = (jnp.arange(n_groups) % 2 == 1)[:, None, None, None]
    pieces = []
    for plane in (tf[1][0], tf[1][1], tb[1][0], tb[1][1]):
        pieces += [jnp.where(odd, 0.0, plane), jnp.where(odd, plane, 0.0)]
    e_grp = jnp.concatenate(pieces, axis=-1).astype(BF16)

    c_grp = jnp.stack([tf[2][0], tf[2][1], tb[2][0], tb[2][1]], axis=1).astype(BF16)

    ap = jnp.concatenate([tf[3][0], tf[3][1], tb[3][0], tb[3][1], tf[4][0], tf[4][1], tb[4][0], tb[4][1]], axis=0)
    pw = jnp.stack([tf[5][0], tf[5][1], tb[5][0], tb[5][1]])
    return strip, e_grp, c_grp, ap, pw


def _s5_a_kernel(u_ref, strip_ref, e_ref, y_ref, s_ref, w_blk, e_blk, ucat, *, n_tiles, seg_stride):
    n_grp = strip_ref.shape[0]
    cw = S5_CHUNK * SSM_GROUP
    pc = e_ref.shape[-1]
    lanes = u_ref.shape[-1]

    @pl.when(pl.program_id(0) == 0)
    def _():
        w_blk[...] = jnp.zeros_like(w_blk)
        e_blk[...] = jnp.zeros_like(e_blk)

    for g in range(n_grp):
        strip = strip_ref[g]
        for t in range(S5_CHUNK):
            r0 = t * lanes + g * SSM_GROUP
            lo = (S5_CHUNK - 1 - t) * SSM_GROUP
            w_blk[r0:r0 + SSM_GROUP, g * cw:(g + 1) * cw] = strip[:, lo:lo + cw].astype(BF16)
            e_blk[r0:r0 + SSM_GROUP, (g // 2) * pc:(g // 2 + 1) * pc] = e_ref[g, t]

    def gather(jj, _):
        for t in range(S5_CHUNK):
            rows = [u_ref[pl.ds(S5_CHUNK * (2 * jj + h) + t, S5_SEGS, stride=seg_stride), :] for h in range(2)]
            dst = pl.ds(pl.multiple_of(jj * 2 * S5_SEGS, 2 * S5_SEGS), 2 * S5_SEGS)
            ucat[dst, t * lanes:(t + 1) * lanes] = jnp.concatenate(rows, axis=0).astype(BF16)
        return 0

    lax.fori_loop(0, n_tiles // 2, gather, 0)
    ub = ucat[...]
    y_ref[...] = jnp.dot(ub, w_blk[...], preferred_element_type=F32)
    s_ref[...] = jnp.dot(ub, e_blk[...], preferred_element_type=F32)


def _s5_bc_kernel(s_ref, yi_ref, c_ref, ap_ref, pw_ref, y_ref, sin, perm, ynat, c_blk, *, n_tiles, seg_stride):
    n_grp, n_plane, n_state, cw = c_ref.shape
    n_pair = n_grp // 2
    pw_ = 2 * n_state
    pc = n_plane * pw_
    lanes = y_ref.shape[-1]

    @pl.when(pl.program_id(0) == 0)
    def _():
        c_blk[...] = jnp.zeros_like(c_blk)
        rl = lax.broadcasted_iota(jnp.int32, (cw, n_grp * cw), 0)
        cl = lax.broadcasted_iota(jnp.int32, (cw, n_grp * cw), 1)
        for g in range(n_grp):
            tgt = (rl // SSM_GROUP) * lanes + g * SSM_GROUP + (rl % SSM_GROUP)
            perm[g * cw:(g + 1) * cw, :] = jnp.where(cl == tgt, 1.0, 0.0).astype(BF16)

    for g in range(n_grp):
        gl = g % 2
        for k in range(n_plane):
            r0 = k * pw_ + gl * n_state
            c_blk[g // 2, r0:r0 + n_state, gl * cw:(gl + 1) * cw] = c_ref[g, k]

    chains = [(q, d) for q in range(n_pair) for d in range(2)]

    def plane_lanes(q, d):
        base = q * pc + d * 2 * pw_
        return slice(base, base + pw_), slice(base + pw_, base + 2 * pw_)

    def coef(row, q):
        return ap_ref[row:row + 1, q * pw_:(q + 1) * pw_]

    def tile_rows(n, d):
        j = n if d == 0 else n_tiles - 1 - n
        return pl.ds(pl.multiple_of(j * S5_SEGS, S5_SEGS), S5_SEGS)

    def step(n, carry):
        out = []
        for (q, d), (zr, zi) in zip(chains, carry):
            re, im = plane_lanes(q, d)
            rows = tile_rows(n, d)
            sin[rows, re] = zr
            sin[rows, im] = zi
            ar, ai = coef(2 * d, q), coef(2 * d + 1, q)
            out.append((ar * zr - ai * zi + s_ref[rows, re], ar * zi + ai * zr + s_ref[rows, im]))
        return tuple(out)

    z0 = jnp.zeros((S5_SEGS, pw_), F32)
    ends = lax.fori_loop(0, n_tiles, step, tuple((z0, z0) for _ in chains))

    carries = []
    for (q, d), (zr, zi) in zip(chains, ends):
        sr, si = coef(4 + 2 * d, q), coef(5 + 2 * d, q)
        cr = jnp.zeros((1, pw_), F32)
        ci = jnp.zeros((1, pw_), F32)
        seg_r = [None] * S5_SEGS
        seg_i = [None] * S5_SEGS
        for s in (range(S5_SEGS) if d == 0 else range(S5_SEGS - 1, -1, -1)):
            seg_r[s], seg_i[s] = cr, ci
            cr, ci = (zr[s:s + 1] + sr * cr - si * ci, zi[s:s + 1] + sr * ci + si * cr)
        carries.append((jnp.concatenate(seg_r, axis=0), jnp.concatenate(seg_i, axis=0)))

    def fix(n8, _):
        tiles = pl.ds(pl.multiple_of(n8 * 8, 8), 8)
        for (q, d), (car_r, car_i) in zip(chains, carries):
            re, im = plane_lanes(q, d)
            pr8 = pw_ref[2 * d, tiles, q * pw_:(q + 1) * pw_]
            pi8 = pw_ref[2 * d + 1, tiles, q * pw_:(q + 1) * pw_]
            for r in range(8):
                rows = pl.ds(pl.multiple_of((n8 * 8 + r) * S5_SEGS, S5_SEGS), S5_SEGS)
                pr, pi = pr8[r:r + 1], pi8[r:r + 1]
                sin[rows, re] = sin[rows, re] + (pr * car_r - pi * car_i)
                sin[rows, im] = sin[rows, im] + (pr * car_i + pi * car_r)
        return 0

    lax.fori_loop(0, n_tiles // 8, fix, 0)

    parts = []
    for q in range(n_pair):
        lhs = sin[:, q * pc:(q + 1) * pc].astype(BF16)
        parts.append(yi_ref[:, q * 2 * cw:(q + 1) * 2 * cw] + jnp.dot(lhs, c_blk[q], preferred_element_type=F32))
    y_chunk = jnp.concatenate(parts, axis=1).astype(BF16)
    ynat[...] = jnp.dot(y_chunk, perm[...], preferred_element_type=F32)

    def scatter(j, _):
        src = pl.ds(pl.multiple_of(j * S5_SEGS, S5_SEGS), S5_SEGS)
        for t in range(S5_CHUNK):
            y_ref[pl.ds(S5_CHUNK * j + t, S5_SEGS, stride=seg_stride), :] = ynat[src, t * lanes:(t + 1) * lanes]
        return 0

    lax.fori_loop(0, n_tiles, scatter, 0)


def _s5_scan(u, fwd, bwd):
    seq, d_ssm = u.shape
    n_groups = d_ssm // SSM_GROUP
    n_state = SSM_STATE
    cw = S5_CHUNK * SSM_GROUP
    n_rows = seq // S5_CHUNK
    n_tiles = n_rows // S5_SEGS
    lanes = 128
    gpb = lanes // SSM_GROUP
    n_blocks = d_ssm // lanes
    bw = gpb * cw
    sw = gpb * 4 * n_state
    seg_stride = n_tiles * S5_CHUNK

    strip, e_grp, c_grp, ap, pw = _s5_tables(fwd, bwd, n_tiles)

    y_intra, s_loc = pl.pallas_call(
        functools.partial(_s5_a_kernel, n_tiles=n_tiles, seg_stride=seg_stride),
        grid=(n_blocks,),
        in_specs=[
            pl.BlockSpec((seq, lanes), lambda i: (0, i)),
            pl.BlockSpec((gpb,) + strip.shape[1:], lambda i: (i, 0, 0)),
            pl.BlockSpec((gpb,) + e_grp.shape[1:], lambda i: (i, 0, 0, 0)),
        ],
        out_specs=[
            pl.BlockSpec((n_rows, bw), lambda i: (0, i)),
            pl.BlockSpec((n_rows, sw), lambda i: (0, i)),
        ],
        out_shape=[
            jax.ShapeDtypeStruct((n_rows, n_blocks * bw), F32),
            jax.ShapeDtypeStruct((n_rows, n_blocks * sw), F32),
        ],
        scratch_shapes=[
            pltpu.VMEM((S5_CHUNK * lanes, bw), BF16),
            pltpu.VMEM((S5_CHUNK * lanes, sw), BF16),
            pltpu.VMEM((n_rows, S5_CHUNK * lanes), BF16),
        ],
        compiler_params=_params("arbitrary"),
        name="s5_a",
    )(u, strip, e_grp)

    pl_lanes = gpb * n_state
    return pl.pallas_call(
        functools.partial(_s5_bc_kernel, n_tiles=n_tiles, seg_stride=seg_stride),
        grid=(n_blocks,),
        in_specs=[
            pl.BlockSpec((n_rows, sw), lambda i: (0, i)),
            pl.BlockSpec((n_rows, bw), lambda i: (0, i)),
            pl.BlockSpec((gpb,) + c_grp.shape[1:], lambda i: (i, 0, 0, 0)),
            pl.BlockSpec((8, pl_lanes), lambda i: (0, i)),
            pl.BlockSpec((4, n_tiles, pl_lanes), lambda i: (0, 0, i)),
        ],
        out_specs=pl.BlockSpec((seq, lanes), lambda i: (0, i)),
        out_shape=jax.ShapeDtypeStruct((seq, d_ssm), F32),
        scratch_shapes=[
            pltpu.VMEM((n_rows, sw), F32),
            pltpu.VMEM((bw, bw), BF16),
            pltpu.VMEM((n_rows, bw), F32),
            pltpu.VMEM((gpb // 2, 8 * n_state, 2 * cw), BF16),
        ],
        compiler_params=_params("arbitrary"),
        name="s5_bc",
    )(s_loc, y_intra, c_grp, ap, pw)


def _s5_post_kernel(y_ref, u_ref, d_ref, w_ref, b_ref, g_ref, o_ref):
    y = y_ref[...] + d_ref[...] * u_ref[...]
    c0 = np.float32(np.sqrt(2.0 / np.pi))
    y = 0.5 * y * (1.0 + jnp.tanh(c0 * (y + np.float32(0.044715) * (y * y * y))))
    z = jnp.dot(y.astype(BF16), w_ref[...].astype(BF16), preferred_element_type=F32) + b_ref[...]
    o = y * (1.0 / (1.0 + jnp.exp(-z)))
    o_ref[...] = _rms(o, g_ref[...]).astype(BF16)


def _s5_post(y, u, d_skip, w_glu, b_glu, g):
    seq, d = y.shape
    tm = min(512, seq)
    row = lambda i: (i, 0)
    fix = lambda i: (0, 0)
    return pl.pallas_call(
        _s5_post_kernel,
        grid=(seq // tm,),
        in_specs=[
            pl.BlockSpec((tm, d), row), pl.BlockSpec((tm, d), row), pl.BlockSpec((1, d), fix),
            pl.BlockSpec((d, d), fix), pl.BlockSpec((1, d), fix), pl.BlockSpec((1, d), fix),
        ],
        out_specs=pl.BlockSpec((tm, d), row),
        out_shape=jax.ShapeDtypeStruct((seq, d), BF16),
        compiler_params=_params("arbitrary"),
        name="s5_post",
    )(y, u, d_skip.reshape(1, d), w_glu, b_glu.reshape(1, d), g.reshape(1, d))


def _na_bias_table(rpb):
    n_heads = rpb.shape[0]
    cols = np.arange(GRID_W)
    col_start = np.clip(cols - WIN_COLS // 2, 0, GRID_W - WIN_COLS)
    key_cols = np.arange(GRID_W)
    in_win = (key_cols[None, :] >= col_start[:, None]) & (key_cols[None, :] < col_start[:, None] + WIN_COLS)
    dx = key_cols[None, :] - cols[:, None] + (WIN_COLS - 1)
    dy = np.arange(WIN_ROWS)[None, :] - np.arange(WIN_ROWS)[:, None] + (WIN_ROWS - 1)
    pick_x = (dx[:, :, None] == np.arange(2 * WIN_COLS - 1)).astype(np.float32)
    pick_y = (dy[:, :, None] == np.arange(2 * WIN_ROWS - 1)).astype(np.float32)
    hp = lax.Precision.HIGHEST
    b = jnp.einsum('hyx,ckx->hyck', rpb.astype(F32), pick_x, precision=hp)
    b = jnp.einsum('viy,hyck->vhcik', pick_y, b, precision=hp)
    b = jnp.where(in_win[None, None, :, None, :], b, MASK_NEG)
    return b.reshape(WIN_ROWS, n_heads // HEADS_PER_DOT, HEADS_PER_DOT * GRID_W, WIN_ROWS * GRID_W)


def _na_kernel(q_ref, k_ref, v_ref, b_ref, g_ref, o_ref):
    n_keys = WIN_ROWS * GRID_W
    k = k_ref[...].reshape(n_keys, k_ref.shape[-1])
    v = v_ref[...].reshape(n_keys, v_ref.shape[-1])
    pw = HEADS_PER_DOT * NA_HEAD_DIM
    row_head = lax.broadcasted_iota(jnp.int32, (HEADS_PER_DOT * GRID_W, pw), 0) // GRID_W
    col_head = lax.broadcasted_iota(jnp.int32, (HEADS_PER_DOT * GRID_W, pw), 1) // NA_HEAD_DIM
    diag = row_head == col_head
    out_head = lax.broadcasted_iota(jnp.int32, (GRID_W, pw), 1) // NA_HEAD_DIM
    outs = []
    for p in range(k.shape[-1] // pw):
        sl = slice(p * pw, (p + 1) * pw)
        q4 = q_ref[:, sl]
        qbd = jnp.where(diag, jnp.concatenate([q4] * HEADS_PER_DOT, axis=0), jnp.zeros((), BF16))
        s = lax.dot_general(qbd, k[:, sl], (((1,), (1,)), ((), ())), preferred_element_type=F32)
        s = s + b_ref[0, p]
        m = jnp.max(s, axis=-1, keepdims=True)
        e = jnp.exp(s - m)
        l = jnp.sum(e, axis=-1, keepdims=True)
        o = jnp.dot(e.astype(BF16), v[:, sl], preferred_element_type=F32) / l
        acc = jnp.zeros((GRID_W, pw), F32)
        for h in range(HEADS_PER_DOT):
            acc = acc + jnp.where(out_head == h, o[h * GRID_W:(h + 1) * GRID_W], 0.0)
        outs.append(acc)
    y = jnp.concatenate(outs, axis=1)
    o_ref[...] = _rms(y, g_ref[...]).astype(BF16)


def _neighbourhood_attention(qkv, rpb, g):
    seq = qkv.shape[0]
    d_na = qkv.shape[1] // 3
    rows = seq // GRID_W
    bias = _na_bias_table(rpb)
    qkv3 = qkv.reshape(rows, GRID_W, 3 * d_na)

    def win_start(r):
        return jnp.clip(r - WIN_ROWS // 2, 0, rows - WIN_ROWS)

    window = (pl.Element(WIN_ROWS), pl.Element(GRID_W), pl.Element(d_na))

    return pl.pallas_call(
        _na_kernel,
        grid=(rows,),
        in_specs=[
            pl.BlockSpec((GRID_W, d_na), lambda r: (r, 0)),
            pl.BlockSpec(window, lambda r: (win_start(r), 0, d_na)),
            pl.BlockSpec(window, lambda r: (win_start(r), 0, 2 * d_na)),
            pl.BlockSpec((1,) + bias.shape[1:], lambda r: (r - win_start(r), 0, 0, 0)),
            pl.BlockSpec((1, d_na), lambda r: (0, 0)),
        ],
        out_specs=pl.BlockSpec((GRID_W, d_na), lambda r: (r, 0)),
        out_shape=jax.ShapeDtypeStruct((seq, d_na), BF16),
        compiler_params=_params("arbitrary"),
        name="na",
    )(qkv, qkv3, qkv3, bias, g.reshape(1, d_na))


def _outproj_kernel(a_ref, b_ref, w_ref, x_ref, o_ref):
    da = a_ref.shape[-1]
    acc = jnp.dot(a_ref[...], w_ref[:da, :].astype(BF16), preferred_element_type=F32)
    acc = acc + jnp.dot(b_ref[...], w_ref[da:, :].astype(BF16), preferred_element_type=F32)
    o_ref[...] = x_ref[...] + acc


def _out_proj(y_ssm, y_na, w_out, x):
    seq, d_model = x.shape
    da, db = y_ssm.shape[1], y_na.shape[1]
    tm = min(1024, seq)
    tn = 512
    return pl.pallas_call(
        _outproj_kernel,
        grid=(seq // tm, d_model // tn),
        in_specs=[
            pl.BlockSpec((tm, da), lambda i, j: (i, 0)),
            pl.BlockSpec((tm, db), lambda i, j: (i, 0)),
            pl.BlockSpec((da + db, tn), lambda i, j: (0, j)),
            pl.BlockSpec((tm, tn), lambda i, j: (i, j)),
        ],
        out_specs=pl.BlockSpec((tm, tn), lambda i, j: (i, j)),
        out_shape=jax.ShapeDtypeStruct((seq, d_model), F32),
        compiler_params=_params("arbitrary", "arbitrary"),
        name="out_proj",
    )(y_ssm, y_na, w_out, x)


def _router_kernel(x_ref, g_ref, wt_ref, h_ref, a_ref):
    h = _rms(x_ref[...], g_ref[...])
    h_ref[...] = h.astype(BF16)
    logits = lax.dot_general(wt_ref[...], h, (((1,), (1,)), ((), ())),
                             precision=lax.Precision.HIGHEST, preferred_element_type=F32)
    m = jnp.max(logits, axis=0, keepdims=True)
    e = jnp.exp(logits - m)
    a_ref[...] = e / jnp.sum(e, axis=0, keepdims=True)


def _router(x1, g, w_router):
    seq, d_model = x1.shape
    n_exp = w_router.shape[1]
    tm = min(512, seq)
    return pl.pallas_call(
        _router_kernel,
        grid=(seq // tm,),
        in_specs=[
            pl.BlockSpec((tm, d_model), lambda i: (i, 0)),
            pl.BlockSpec((1, d_model), lambda i: (0, 0)),
            pl.BlockSpec((n_exp, d_model), lambda i: (0, 0)),
        ],
        out_specs=[
            pl.BlockSpec((tm, d_model), lambda i: (i, 0)),
            pl.BlockSpec((n_exp, tm), lambda i: (0, i)),
        ],
        out_shape=[
            jax.ShapeDtypeStruct((seq, d_model), BF16),
            jax.ShapeDtypeStruct((n_exp, seq), F32),
        ],
        compiler_params=_params("arbitrary"),
        name="router",
    )(x1, g.reshape(1, d_model), w_router.T)


def _topk_kernel(a_ref, posw_ref, gate_ref, ws_ref, *, cap, blk, win):
    a = a_ref[...]
    n_exp, seq = a.shape
    n_blk = seq // blk
    bits = pltpu.bitcast(a, jnp.int32)

    def bit_step(i, thr):
        cand = thr | jnp.left_shift(jnp.int32(1), 30 - i)
        cnt = jnp.sum((bits >= cand).astype(jnp.int32), axis=-1, keepdims=True)
        return jnp.where(cnt >= cap, cand, thr)

    thr = lax.fori_loop(0, 31, bit_step, jnp.zeros((n_exp, 1), jnp.int32))
    gt = bits > thr
    eq = bits == thr
    need = cap - jnp.sum(gt.astype(jnp.int32), axis=-1, keepdims=True)

    tri = (lax.broadcasted_iota(jnp.int32, (blk, blk), 0)
           <= lax.broadcasted_iota(jnp.int32, (blk, blk), 1)).astype(BF16)
    blk_of_tok = lax.broadcasted_iota(jnp.int32, (seq, n_blk), 0) // blk
    tok_to_blk = (blk_of_tok == lax.broadcasted_iota(jnp.int32, (seq, n_blk), 1)).astype(BF16)
    blk_before = (lax.broadcasted_iota(jnp.int32, (n_blk, n_blk), 0)
                  < lax.broadcasted_iota(jnp.int32, (n_blk, n_blk), 1)).astype(BF16)
    erow = lax.broadcasted_iota(jnp.int32, (2 * n_blk, seq), 0)
    ecol = lax.broadcasted_iota(jnp.int32, (2 * n_blk, seq), 1) // blk
    expand = jnp.where(erow == ecol, 32.0, jnp.where(erow - n_blk == ecol, 1.0, 0.0)).astype(BF16)

    def prefix_counts(mask):
        mb = jnp.where(mask, 1.0, 0.0).astype(BF16)
        local = jnp.concatenate(
            [jnp.dot(mb[:, b * blk:(b + 1) * blk], tri, preferred_element_type=F32) for b in range(n_blk)],
            axis=1)
        per_blk = jnp.dot(mb, tok_to_blk, preferred_element_type=F32)
        start = jnp.dot(per_blk.astype(BF16), blk_before, preferred_element_type=F32)
        hi = jnp.floor(start * (1.0 / 32.0))
        parts = jnp.concatenate([hi, start - 32.0 * hi], axis=1).astype(BF16)
        start_tok = jnp.dot(parts, expand, preferred_element_type=F32)
        return local + start_tok, start, start_tok

    eq_incl, _, _ = prefix_counts(eq)
    sel = gt | (eq & (eq_incl - 1.0 < need.astype(F32)))
    incl, start, start_tok = prefix_counts(sel)

    def window(s):
        return jnp.floor(s * (1.0 / MOE_WIN_ALIGN)) * MOE_WIN_ALIGN

    posw_ref[...] = jnp.where(sel, (incl - 1.0 - window(start_tok)).astype(jnp.int32), -1)
    gate_ref[...] = jnp.where(sel, a, 0.0)
    ws_ref[...] = window(start).astype(jnp.int32)


def _topk(aff_t, cap, blk, win):
    n_exp, seq = aff_t.shape
    n_blk = seq // blk
    full = lambda *_: (0, 0)
    return pl.pallas_call(
        functools.partial(_topk_kernel, cap=cap, blk=blk, win=win),
        grid=(1,),
        in_specs=[pl.BlockSpec((n_exp, seq), full)],
        out_specs=[pl.BlockSpec((n_exp, seq), full), pl.BlockSpec((n_exp, seq), full),
                   pl.BlockSpec((n_exp, n_blk), full)],
        out_shape=[
            jax.ShapeDtypeStruct((n_exp, seq), jnp.int32),
            jax.ShapeDtypeStruct((n_exp, seq), F32),
            jax.ShapeDtypeStruct((n_exp, n_blk), jnp.int32),
        ],
        compiler_params=_params("arbitrary"),
        name="topk",
    )(aff_t)


def _gather_kernel(ws_ref, h_ref, posw_ref, xe_ref, acc_ref, *, blk, win, n_blk):
    e = pl.program_id(1)
    cap = xe_ref.shape[1]
    head = MOE_WIN_ALIGN
    acc_ref[0:head, :] = jnp.zeros((head, acc_ref.shape[1]), acc_ref.dtype)
    slot = lax.broadcasted_iota(jnp.int32, (win, blk), 0)

    def body(b, _):
        ws = pl.multiple_of(ws_ref[e * n_blk + b], MOE_WIN_ALIGN)
        onehot = jnp.where(slot == posw_ref[0, pl.ds(b, 1), :], 1.0, 0.0).astype(BF16)
        rows = h_ref[pl.ds(pl.multiple_of(b * blk, blk), blk), :]
        part = jnp.dot(onehot, rows, preferred_element_type=F32)
        first = acc_ref[pl.ds(ws, head), :].astype(F32) + part[:head]
        acc_ref[pl.ds(ws, head), :] = first.astype(BF16)
        acc_ref[pl.ds(pl.multiple_of(ws + head, MOE_WIN_ALIGN), win - head), :] = part[head:].astype(BF16)
        return 0

    lax.fori_loop(0, n_blk, body, 0, unroll=8)
    xe_ref[0] = acc_ref[0:cap, :]


def _moe_gather(ws_flat, h2, posw3, cap, blk, win):
    seq, d_model = h2.shape
    n_exp, n_blk, _ = posw3.shape
    dh = d_model // 2
    grid_spec = pltpu.PrefetchScalarGridSpec(
        num_scalar_prefetch=1,
        grid=(2, n_exp),
        in_specs=[
            pl.BlockSpec((seq, dh), lambda c, e, ws: (0, c)),
            pl.BlockSpec((1, n_blk, blk), lambda c, e, ws: (e, 0, 0)),
        ],
        out_specs=pl.BlockSpec((1, cap, dh), lambda c, e, ws: (e, 0, c)),
        scratch_shapes=[pltpu.VMEM((cap + win, dh), BF16)],
    )
    return pl.pallas_call(
        functools.partial(_gather_kernel, blk=blk, win=win, n_blk=n_blk),
        grid_spec=grid_spec,
        out_shape=jax.ShapeDtypeStruct((n_exp, cap, d_model), BF16),
        compiler_params=_params("arbitrary", "arbitrary"),
        name="moe_gather",
    )(ws_flat, h2, posw3)


def _ffn_kernel(x_ref, wg_ref, wu_ref, wd_ref, y_ref, acc_ref):
    f = pl.program_id(1)
    x = x_ref[0]
    g = jnp.dot(x, wg_ref[0].astype(BF16), preferred_element_type=F32)
    u = jnp.dot(x, wu_ref[0].astype(BF16), preferred_element_type=F32)
    act = (g * (1.0 / (1.0 + jnp.exp(-g))) * u).astype(BF16)
    part = jnp.dot(act, wd_ref[0].astype(BF16), preferred_element_type=F32)

    @pl.when(f == 0)
    def _():
        acc_ref[...] = part

    @pl.when(f > 0)
    def _():
        acc_ref[...] += part

    @pl.when(f == pl.num_programs(1) - 1)
    def _():
        y_ref[0] = acc_ref[...].astype(BF16)


def _moe_ffn(xe, w_gate, w_up, w_down):
    n_exp, cap, d_model = xe.shape
    d_ff = w_gate.shape[-1]
    tf = 256
    return pl.pallas_call(
        _ffn_kernel,
        grid=(n_exp, d_ff // tf),
        in_specs=[
            pl.BlockSpec((1, cap, d_model), lambda e, f: (e, 0, 0)),
            pl.BlockSpec((1, d_model, tf), lambda e, f: (e, 0, f)),
            pl.BlockSpec((1, d_model, tf), lambda e, f: (e, 0, f)),
            pl.BlockSpec((1, tf, d_model), lambda e, f: (e, f, 0)),
        ],
        out_specs=pl.BlockSpec((1, cap, d_model), lambda e, f: (e, 0, 0)),
        out_shape=jax.ShapeDtypeStruct((n_exp, cap, d_model), BF16),
        scratch_shapes=[pltpu.VMEM((cap, d_model), F32)],
        compiler_params=_params("arbitrary", "arbitrary"),
        name="moe_ffn",
    )(xe, w_gate, w_up, w_down)


def _combine_kernel(ws_ref, ye_ref, x_ref, posw_ref, gate_ref, o_ref, *, win, n_blk):
    b = pl.program_id(1)
    n_exp = ye_ref.shape[0]
    blk = x_ref.shape[0]
    cap = ye_ref.shape[1]
    slot = lax.broadcasted_iota(jnp.int32, (blk, win), 1)
    acc = x_ref[...]
    for e in range(n_exp):
        ws0 = ws_ref[e * n_blk + b]
        ws = pl.multiple_of(jnp.minimum(ws0, cap - win), MOE_WIN_ALIGN)
        pw = posw_ref[:, e:e + 1]
        pw = jnp.where(pw >= 0, pw + (ws0 - ws), -1)
        onehot = jnp.where(slot == pw, 1.0, 0.0).astype(BF16)
        part = jnp.dot(onehot, ye_ref[e, pl.ds(ws, win), :], preferred_element_type=F32)
        acc = acc + part * gate_ref[:, e:e + 1]
    o_ref[...] = acc


def _moe_combine(ws_flat, ye, x1, posw_t, gate_t, blk, win):
    seq, d_model = x1.shape
    n_exp, cap, _ = ye.shape
    n_blk = seq // blk
    dq = d_model // 4
    grid_spec = pltpu.PrefetchScalarGridSpec(
        num_scalar_prefetch=1,
        grid=(4, n_blk),
        in_specs=[
            pl.BlockSpec((n_exp, cap, dq), lambda c, b, ws: (0, 0, c)),
            pl.BlockSpec((blk, dq), lambda c, b, ws: (b, c)),
            pl.BlockSpec((blk, n_exp), lambda c, b, ws: (b, 0)),
            pl.BlockSpec((blk, n_exp), lambda c, b, ws: (b, 0)),
        ],
        out_specs=pl.BlockSpec((blk, dq), lambda c, b, ws: (b, c)),
    )
    return pl.pallas_call(
        functools.partial(_combine_kernel, win=win, n_blk=n_blk),
        grid_spec=grid_spec,
        out_shape=jax.ShapeDtypeStruct((seq, d_model), F32),
        compiler_params=_params("arbitrary", "arbitrary"),
        name="moe_combine",
    )(ws_flat, ye, x1, posw_t, gate_t)


def _final_norm_kernel(x_ref, g_ref, o_ref):
    o_ref[...] = _rms(x_ref[...], g_ref[...])


def _final_norm(x, g):
    seq, d_model = x.shape
    tm = min(512, seq)
    return pl.pallas_call(
        _final_norm_kernel,
        grid=(seq // tm,),
        in_specs=[pl.BlockSpec((tm, d_model), lambda i: (i, 0)), pl.BlockSpec((1, d_model), lambda i: (0, 0))],
        out_specs=pl.BlockSpec((tm, d_model), lambda i: (i, 0)),
        out_shape=jax.ShapeDtypeStruct((seq, d_model), F32),
        compiler_params=_params("arbitrary"),
        name="final_norm",
    )(x, g.reshape(1, d_model))


def _layer(x, norm_mix_g, w_in, fwd, bwd, ssm_d, w_glu, b_glu, na_rpb, g_ssm_out, g_na_out, w_out,
           norm_ffn_g, w_router, w_gate, w_up, w_down):
    seq, d_model = x.shape
    d_ssm = ssm_d.shape[0]
    d_na = g_na_out.shape[0]
    n_exp = w_router.shape[1]
    cap = EC_CAPACITY_FACTOR * seq // n_exp
    blk = min(MOE_TOK_BLOCK, cap // 2)
    win = blk + MOE_WIN_ALIGN

    u, qkv = _in_proj(x, norm_mix_g, w_in, d_ssm, d_na)
    y_ssm = _s5_post(_s5_scan(u, fwd, bwd), u, ssm_d, w_glu, b_glu, g_ssm_out)
    y_na = _neighbourhood_attention(qkv, na_rpb, g_na_out)
    x1 = _out_proj(y_ssm, y_na, w_out, x)

    h2, aff_t = _router(x1, norm_ffn_g, w_router)
    posw, gate, ws = _topk(aff_t, cap, blk, win)
    ws_flat = ws.reshape(-1)
    xe = _moe_gather(ws_flat, h2, posw.reshape(n_exp, seq // blk, blk), cap, blk, win)
    ye = _moe_ffn(xe, w_gate, w_up, w_down)
    return _moe_combine(ws_flat, ye, x1, posw.T, gate.T, blk, win)


def kernel(x, norm_mix_g, w_in, a_re_fwd, a_im_fwd, log_dt_fwd, b_re_fwd, b_im_fwd, c_re_fwd, c_im_fwd, a_re_bwd, a_im_bwd, log_dt_bwd, b_re_bwd, b_im_bwd, c_re_bwd, c_im_bwd, ssm_d, w_glu, b_glu, na_rpb, g_ssm_out, g_na_out, w_out, norm_ffn_g, w_router, w_gate, w_up, w_down, norm_final_g):
    bsz = x.shape[0]
    depth = w_in.shape[0]
    outs = []
    for b in range(bsz):
        xb = x[b]
        for l in range(depth):
            fwd = (a_re_fwd[l], a_im_fwd[l], log_dt_fwd[l], b_re_fwd[l], b_im_fwd[l], c_re_fwd[l], c_im_fwd[l])
            bwd = (a_re_bwd[l], a_im_bwd[l], log_dt_bwd[l], b_re_bwd[l], b_im_bwd[l], c_re_bwd[l], c_im_bwd[l])
            xb = _layer(xb, norm_mix_g[l], w_in[l], fwd, bwd, ssm_d[l], w_glu[l], b_glu[l], na_rpb[l],
                        g_ssm_out[l], g_na_out[l], w_out[l], norm_ffn_g[l], w_router[l],
                        w_gate[l], w_up[l], w_down[l])
        outs.append(_final_norm(xb, norm_final_g))
    return jnp.stack(outs)
```

```python
import functools

import numpy as np
import jax
import jax.numpy as jnp
from jax import lax
from jax.experimental import pallas as pl
from jax.experimental.pallas import tpu as pltpu

F32 = jnp.float32
BF16 = jnp.bfloat16

RMS_EPS = 1e-6
SSM_GROUP = 16
SSM_STATE = 64
NA_HEADS = 16
NA_HEAD_DIM = 64
GRID_W = 64
WIN_ROWS = 8
WIN_COLS = 16
N_EXPERTS = 16
EC_CAPACITY_FACTOR = 2

S5_CHUNK = 16
S5_SEGS = 8
HEADS_PER_DOT = 4
MOE_TOK_BLOCK = 128
MOE_WIN_ALIGN = 16
MOE_WIN = 64
MASK_NEG = -1e30

VMEM_LIMIT_BYTES = 56 * 1024 * 1024


def _params(*semantics):
    return pltpu.CompilerParams(dimension_semantics=semantics, vmem_limit_bytes=VMEM_LIMIT_BYTES)


def _rms(x, g):
    ms = jnp.mean(x * x, axis=-1, keepdims=True)
    return x * lax.rsqrt(ms + RMS_EPS) * g


def _inproj_kernel(x_ref, g_ref, w_ref, u_ref, qkv_ref, h_scr, *, n_u, n_q, q_scale):
    j = pl.program_id(1)

    @pl.when(j == 0)
    def _():
        h_scr[...] = _rms(x_ref[...], g_ref[...]).astype(BF16)

    acc = jnp.dot(h_scr[...], w_ref[...].astype(BF16), preferred_element_type=F32)

    @pl.when(j < n_u)
    def _():
        u_ref[...] = acc

    @pl.when(j >= n_u)
    def _():
        scale = jnp.where(j < n_u + n_q, q_scale, 1.0).astype(F32)
        qkv_ref[...] = (acc * scale).astype(BF16)


def _in_proj(x, g, w_in, d_ssm, d_na):
    seq, d_model = x.shape
    tm = min(1024, seq)
    tn = 512
    n_u, n_q = d_ssm // tn, d_na // tn
    n_cols = w_in.shape[1] // tn
    kern = functools.partial(_inproj_kernel, n_u=n_u, n_q=n_q, q_scale=NA_HEAD_DIM ** -0.5)
    return pl.pallas_call(
        kern,
        grid=(seq // tm, n_cols),
        in_specs=[
            pl.BlockSpec((tm, d_model), lambda i, j: (i, 0)),
            pl.BlockSpec((1, d_model), lambda i, j: (0, 0)),
            pl.BlockSpec((d_model, tn), lambda i, j: (0, j)),
        ],
        out_specs=[
            pl.BlockSpec((tm, tn), lambda i, j: (i, jnp.minimum(j, n_u - 1))),
            pl.BlockSpec((tm, tn), lambda i, j: (i, jnp.maximum(j - n_u, 0))),
        ],
        out_shape=[
            jax.ShapeDtypeStruct((seq, d_ssm), F32),
            jax.ShapeDtypeStruct((seq, 3 * d_na), BF16),
        ],
        scratch_shapes=[pltpu.VMEM((tm, d_model), BF16)],
        compiler_params=_params("arbitrary", "arbitrary"),
        name="in_proj",
    )(x, g.reshape(1, d_model), w_in)


def _s5_direction_tables(a_re, a_im, log_dt, b_re, b_im, c_re, c_im, n_tiles, reverse):
    t_len = S5_CHUNK
    n_ch = b_re.shape[-1]
    hp = lax.Precision.HIGHEST
    dt = jnp.exp(log_dt.astype(F32))[:, None]
    xr, xi = a_re.astype(F32) * dt, a_im.astype(F32) * dt

    def power(xr_, xi_, k):
        k = k.reshape((-1,) + (1,) * xr_.ndim)
        mag = jnp.exp(xr_[None] * k)
        return mag * jnp.cos(xi_[None] * k), mag * jnp.sin(xi_[None] * k)

    pr, pi = power(xr, xi, jnp.arange(t_len + 1, dtype=F32))
    nr, ni = pr[1] - 1.0, pi[1]
    den = a_re * a_re + a_im * a_im
    qr, qi = (nr * a_re + ni * a_im) / den, (ni * a_re - nr * a_im) / den
    bb_r, bb_i = _cmul(qr[:, None, :], qi[:, None, :], jnp.swapaxes(b_re, 1, 2), jnp.swapaxes(b_im, 1, 2))
    ct_r, ct_i = jnp.swapaxes(c_re, 1, 2).astype(F32), jnp.swapaxes(c_im, 1, 2).astype(F32)
    apt_r, apt_i = jnp.transpose(pr, (1, 2, 0)), jnp.transpose(pi, (1, 2, 0))

    def c_times_power(n_lanes, lag_of_lane):
        lane = np.arange(n_lanes)
        pick_c = (lane[None, :] % n_ch == np.arange(n_ch)[:, None]).astype(np.float32)
        pick_k = (lag_of_lane[lane // n_ch][None, :] == np.arange(t_len + 1)[:, None]).astype(np.float32)
        cr = jnp.einsum('gpo,oq->gpq', ct_r, pick_c, precision=hp)
        ci = jnp.einsum('gpo,oq->gpq', ct_i, pick_c, precision=hp)
        ar = jnp.einsum('gpk,kq->gpq', apt_r, pick_k, precision=hp)
        ai = jnp.einsum('gpk,kq->gpq', apt_i, pick_k, precision=hp)
        return _cmul(cr, ci, ar, ai)

    slots = np.arange(2 * t_len)
    lag_of_slot = (t_len - 1 - slots) if reverse else (slots - (t_len - 1))
    lag_of_slot = np.where((lag_of_slot >= 0) & (lag_of_slot < t_len), lag_of_slot, -1)
    yr, yi = c_times_power(2 * t_len * n_ch, lag_of_slot)
    strip_lhs = jnp.concatenate([bb_r, -bb_i], axis=-1)
    strip_rhs = jnp.concatenate([yr, yi], axis=1)
    e_in = (np.arange(t_len)) if reverse else (t_len - 1 - np.arange(t_len))
    st_r, st_i = _cmul(jnp.swapaxes(pr[e_in], 0, 1)[:, :, None, :], jnp.swapaxes(pi[e_in], 0, 1)[:, :, None, :],
                       bb_r[:, None], bb_i[:, None])
    e_out = (t_len - np.arange(t_len)) if reverse else (np.arange(t_len) + 1)
    wr, wi = c_times_power(t_len * n_ch, e_out)
    xrf, xif = xr.reshape(-1), xi.reshape(-1)
    tiles = jnp.arange(n_tiles, dtype=F32)
    if reverse:
        tiles = tiles[::-1]
    a_chunk = power(xrf, xif, jnp.full((1,), float(t_len), F32))
    a_seg = power(xrf, xif, jnp.full((1,), float(t_len * n_tiles), F32))
    a_tile = power(xrf, xif, float(t_len) * tiles)
    return (strip_lhs, strip_rhs), (st_r, st_i), (wr, -wi), a_chunk, a_seg, a_tile


def _cmul(ar, ai, br, bi):
    return ar * br - ai * bi, ar * bi + ai * br


def _s5_tables(fwd, bwd, n_tiles):
    tf = _s5_direction_tables(*fwd, n_tiles=n_tiles, reverse=False)
    tb = _s5_direction_tables(*bwd, n_tiles=n_tiles, reverse=True)
    n_groups = tf[1][0].shape[0]

    strip = jnp.einsum('gck,gkq->gcq', jnp.concatenate([tf[0][0], tb[0][0]], axis=-1),
                       jnp.concatenate([tf[0][1], tb[0][1]], axis=1), precision=lax.Precision.HIGHEST)

    odd = (jnp.arange(n_groups) % 2 == 1)[:, None, None, None]
    pieces = []
    for plane in (tf[1][0], tf[1][1], tb[1][0], tb[1][1]):
        pieces += [jnp.where(odd, 0.0, plane), jnp.where(odd, plane, 0.0)]
    e_grp = jnp.concatenate(pieces, axis=-1).astype(BF16)

    c_grp = jnp.stack([tf[2][0], tf[2][1], tb[2][0], tb[2][1]], axis=1).astype(BF16)

    ap = jnp.concatenate([tf[3][0], tf[3][1], tb[3][0], tb[3][1], tf[4][0], tf[4][1], tb[4][0], tb[4][1]], axis=0)
    pw = jnp.stack([tf[5][0], tf[5][1], tb[5][0], tb[5][1]])
    return strip, e_grp, c_grp, ap, pw


def _s5_a_kernel(u_ref, strip_ref, e_ref, y_ref, s_ref, w_blk, e_blk, ucat, *, n_tiles, seg_stride):
    n_grp = strip_ref.shape[0]
    cw = S5_CHUNK * SSM_GROUP
    pc = e_ref.shape[-1]
    lanes = u_ref.shape[-1]

    @pl.when(pl.program_id(0) == 0)
    def _():
        w_blk[...] = jnp.zeros_like(w_blk)
        e_blk[...] = jnp.zeros_like(e_blk)

    for g in range(n_grp):
        strip = strip_ref[g]
        for t in range(S5_CHUNK):
            r0 = t * lanes + g * SSM_GROUP
            lo = (S5_CHUNK - 1 - t) * SSM_GROUP
            w_blk[r0:r0 + SSM_GROUP, g * cw:(g + 1) * cw] = strip[:, lo:lo + cw].astype(BF16)
            e_blk[r0:r0 + SSM_GROUP, (g // 2) * pc:(g // 2 + 1) * pc] = e_ref[g, t]

    def gather(jj, _):
        for t in range(S5_CHUNK):
            rows = [u_ref[pl.ds(S5_CHUNK * (2 * jj + h) + t, S5_SEGS, stride=seg_stride), :] for h in range(2)]
            dst = pl.ds(pl.multiple_of(jj * 2 * S5_SEGS, 2 * S5_SEGS), 2 * S5_SEGS)
            ucat[dst, t * lanes:(t + 1) * lanes] = jnp.concatenate(rows, axis=0).astype(BF16)
        return 0

    lax.fori_loop(0, n_tiles // 2, gather, 0)
    ub = ucat[...]
    y_ref[...] = jnp.dot(ub, w_blk[...], preferred_element_type=F32)
    s_ref[...] = jnp.dot(ub, e_blk[...], preferred_element_type=F32)


def _s5_bc_kernel(s_ref, yi_ref, c_ref, ap_ref, pw_ref, y_ref, sin, perm, ynat, c_blk, *, n_tiles, seg_stride):
    n_grp, n_plane, n_state, cw = c_ref.shape
    n_pair = n_grp // 2
    pw_ = 2 * n_state
    pc = n_plane * pw_
    lanes = y_ref.shape[-1]

    @pl.when(pl.program_id(0) == 0)
    def _():
        c_blk[...] = jnp.zeros_like(c_blk)
        rl = lax.broadcasted_iota(jnp.int32, (cw, n_grp * cw), 0)
        cl = lax.broadcasted_iota(jnp.int32, (cw, n_grp * cw), 1)
        for g in range(n_grp):
            tgt = (rl // SSM_GROUP) * lanes + g * SSM_GROUP + (rl % SSM_GROUP)
            perm[g * cw:(g + 1) * cw, :] = jnp.where(cl == tgt, 1.0, 0.0).astype(BF16)

    for g in range(n_grp):
        gl = g % 2
        for k in range(n_plane):
            r0 = k * pw_ + gl * n_state
            c_blk[g // 2, r0:r0 + n_state, gl * cw:(gl + 1) * cw] = c_ref[g, k]

    chains = [(q, d) for q in range(n_pair) for d in range(2)]

    def plane_lanes(q, d):
        base = q * pc + d * 2 * pw_
        return slice(base, base + pw_), slice(base + pw_, base + 2 * pw_)

    def coef(row, q):
        return ap_ref[row:row + 1, q * pw_:(q + 1) * pw_]

    def tile_rows(n, d):
        j = n if d == 0 else n_tiles - 1 - n
        return pl.ds(pl.multiple_of(j * S5_SEGS, S5_SEGS), S5_SEGS)

    def step(n, carry):
        out = []
        for (q, d), (zr, zi) in zip(chains, carry):
            re, im = plane_lanes(q, d)
            rows = tile_rows(n, d)
            sin[rows, re] = zr
            sin[rows, im] = zi
            ar, ai = coef(2 * d, q), coef(2 * d + 1, q)
            out.append((ar * zr - ai * zi + s_ref[rows, re], ar * zi + ai * zr + s_ref[rows, im]))
        return tuple(out)

    z0 = jnp.zeros((S5_SEGS, pw_), F32)
    ends = lax.fori_loop(0, n_tiles, step, tuple((z0, z0) for _ in chains))

    carries = []
    for (q, d), (zr, zi) in zip(chains, ends):
        sr, si = coef(4 + 2 * d, q), coef(5 + 2 * d, q)
        cr = jnp.zeros((1, pw_), F32)
        ci = jnp.zeros((1, pw_), F32)
        seg_r = [None] * S5_SEGS
        seg_i = [None] * S5_SEGS
        for s in (range(S5_SEGS) if d == 0 else range(S5_SEGS - 1, -1, -1)):
            seg_r[s], seg_i[s] = cr, ci
            cr, ci = (zr[s:s + 1] + sr * cr - si * ci, zi[s:s + 1] + sr * ci + si * cr)
        carries.append((jnp.concatenate(seg_r, axis=0), jnp.concatenate(seg_i, axis=0)))

    def fix(n8, _):
        tiles = pl.ds(pl.multiple_of(n8 * 8, 8), 8)
        for (q, d), (car_r, car_i) in zip(chains, carries):
            re, im = plane_lanes(q, d)
            pr8 = pw_ref[2 * d, tiles, q * pw_:(q + 1) * pw_]
            pi8 = pw_ref[2 * d + 1, tiles, q * pw_:(q + 1) * pw_]
            for r in range(8):
                rows = pl.ds(pl.multiple_of((n8 * 8 + r) * S5_SEGS, S5_SEGS), S5_SEGS)
                pr, pi = pr8[r:r + 1], pi8[r:r + 1]
                sin[rows, re] = sin[rows, re] + (pr * car_r - pi * car_i)
                sin[rows, im] = sin[rows, im] + (pr * car_i + pi * car_r)
        return 0

    lax.fori_loop(0, n_tiles // 8, fix, 0)

    parts = []
    for q in range(n_pair):
        lhs = sin[:, q * pc:(q + 1) * pc].astype(BF16)
        parts.append(yi_ref[:, q * 2 * cw:(q + 1) * 2 * cw] + jnp.dot(lhs, c_blk[q], preferred_element_type=F32))
    y_chunk = jnp.concatenate(parts, axis=1).astype(BF16)
    ynat[...] = jnp.dot(y_chunk, perm[...], preferred_element_type=F32)

    def scatter(j, _):
        src = pl.ds(pl.multiple_of(j * S5_SEGS, S5_SEGS), S5_SEGS)
        for t in range(S5_CHUNK):
            y_ref[pl.ds(S5_CHUNK * j + t, S5_SEGS, stride=seg_stride), :] = ynat[src, t * lanes:(t + 1) * lanes]
        return 0

    lax.fori_loop(0, n_tiles, scatter, 0)


def _s5_scan(u, fwd, bwd):
    seq, d_ssm = u.shape
    n_groups = d_ssm // SSM_GROUP
    n_state = SSM_STATE
    cw = S5_CHUNK * SSM_GROUP
    n_rows = seq // S5_CHUNK
    n_tiles = n_rows // S5_SEGS
    lanes = 128
    gpb = lanes // SSM_GROUP
    n_blocks = d_ssm // lanes
    bw = gpb * cw
    sw = gpb * 4 * n_state
    seg_stride = n_tiles * S5_CHUNK

    strip, e_grp, c_grp, ap, pw = _s5_tables(fwd, bwd, n_tiles)

    y_intra, s_loc = pl.pallas_call(
        functools.partial(_s5_a_kernel, n_tiles=n_tiles, seg_stride=seg_stride),
        grid=(n_blocks,),
        in_specs=[
            pl.BlockSpec((seq, lanes), lambda i: (0, i)),
            pl.BlockSpec((gpb,) + strip.shape[1:], lambda i: (i, 0, 0)),
            pl.BlockSpec((gpb,) + e_grp.shape[1:], lambda i: (i, 0, 0, 0)),
        ],
        out_specs=[
            pl.BlockSpec((n_rows, bw), lambda i: (0, i)),
            pl.BlockSpec((n_rows, sw), lambda i: (0, i)),
        ],
        out_shape=[
            jax.ShapeDtypeStruct((n_rows, n_blocks * bw), F32),
            jax.ShapeDtypeStruct((n_rows, n_blocks * sw), F32),
        ],
        scratch_shapes=[
            pltpu.VMEM((S5_CHUNK * lanes, bw), BF16),
            pltpu.VMEM((S5_CHUNK * lanes, sw), BF16),
            pltpu.VMEM((n_rows, S5_CHUNK * lanes), BF16),
        ],
        compiler_params=_params("arbitrary"),
        name="s5_a",
    )(u, strip, e_grp)

    pl_lanes = gpb * n_state
    return pl.pallas_call(
        functools.partial(_s5_bc_kernel, n_tiles=n_tiles, seg_stride=seg_stride),
        grid=(n_blocks,),
        in_specs=[
            pl.BlockSpec((n_rows, sw), lambda i: (0, i)),
            pl.BlockSpec((n_rows, bw), lambda i: (0, i)),
            pl.BlockSpec((gpb,) + c_grp.shape[1:], lambda i: (i, 0, 0, 0)),
            pl.BlockSpec((8, pl_lanes), lambda i: (0, i)),
            pl.BlockSpec((4, n_tiles, pl_lanes), lambda i: (0, 0, i)),
        ],
        out_specs=pl.BlockSpec((seq, lanes), lambda i: (0, i)),
        out_shape=jax.ShapeDtypeStruct((seq, d_ssm), F32),
        scratch_shapes=[
            pltpu.VMEM((n_rows, sw), F32),
            pltpu.VMEM((bw, bw), BF16),
            pltpu.VMEM((n_rows, bw), F32),
            pltpu.VMEM((gpb // 2, 8 * n_state, 2 * cw), BF16),
        ],
        compiler_params=_params("arbitrary"),
        name="s5_bc",
    )(s_loc, y_intra, c_grp, ap, pw)


def _s5_post_kernel(y_ref, u_ref, d_ref, w_ref, b_ref, g_ref, o_ref):
    y = y_ref[...] + d_ref[...] * u_ref[...]
    c0 = np.float32(np.sqrt(2.0 / np.pi))
    y = 0.5 * y * (1.0 + jnp.tanh(c0 * (y + np.float32(0.044715) * (y * y * y))))
    z = jnp.dot(y.astype(BF16), w_ref[...].astype(BF16), preferred_element_type=F32) + b_ref[...]
    o = y * (1.0 / (1.0 + jnp.exp(-z)))
    o_ref[...] = _rms(o, g_ref[...]).astype(BF16)


def _s5_post(y, u, d_skip, w_glu, b_glu, g):
    seq, d = y.shape
    tm = min(512, seq)
    row = lambda i: (i, 0)
    fix = lambda i: (0, 0)
    return pl.pallas_call(
        _s5_post_kernel,
        grid=(seq // tm,),
        in_specs=[
            pl.BlockSpec((tm, d), row), pl.BlockSpec((tm, d), row), pl.BlockSpec((1, d), fix),
            pl.BlockSpec((d, d), fix), pl.BlockSpec((1, d), fix), pl.BlockSpec((1, d), fix),
        ],
        out_specs=pl.BlockSpec((tm, d), row),
        out_shape=jax.ShapeDtypeStruct((seq, d), BF16),
        compiler_params=_params("arbitrary"),
        name="s5_post",
    )(y, u, d_skip.reshape(1, d), w_glu, b_glu.reshape(1, d), g.reshape(1, d))


def _na_bias_table(rpb):
    n_heads = rpb.shape[0]
    cols = np.arange(GRID_W)
    col_start = np.clip(cols - WIN_COLS // 2, 0, GRID_W - WIN_COLS)
    key_cols = np.arange(GRID_W)
    in_win = (key_cols[None, :] >= col_start[:, None]) & (key_cols[None, :] < col_start[:, None] + WIN_COLS)
    dx = key_cols[None, :] - cols[:, None] + (WIN_COLS - 1)
    dy = np.arange(WIN_ROWS)[None, :] - np.arange(WIN_ROWS)[:, None] + (WIN_ROWS - 1)
    pick_x = (dx[:, :, None] == np.arange(2 * WIN_COLS - 1)).astype(np.float32)
    pick_y = (dy[:, :, None] == np.arange(2 * WIN_ROWS - 1)).astype(np.float32)
    hp = lax.Precision.HIGHEST
    b = jnp.einsum('hyx,ckx->hyck', rpb.astype(F32), pick_x, precision=hp)
    b = jnp.einsum('viy,hyck->vhcik', pick_y, b, precision=hp)
    b = jnp.where(in_win[None, None, :, None, :], b, MASK_NEG)
    return b.reshape(WIN_ROWS, n_heads // HEADS_PER_DOT, HEADS_PER_DOT * GRID_W, WIN_ROWS * GRID_W)


def _na_kernel(q_ref, k_ref, v_ref, b_ref, g_ref, o_ref):
    n_keys = WIN_ROWS * GRID_W
    k = k_ref[...].reshape(n_keys, k_ref.shape[-1])
    v = v_ref[...].reshape(n_keys, v_ref.shape[-1])
    pw = HEADS_PER_DOT * NA_HEAD_DIM
    row_head = lax.broadcasted_iota(jnp.int32, (HEADS_PER_DOT * GRID_W, pw), 0) // GRID_W
    col_head = lax.broadcasted_iota(jnp.int32, (HEADS_PER_DOT * GRID_W, pw), 1) // NA_HEAD_DIM
    diag = row_head == col_head
    out_head = lax.broadcasted_iota(jnp.int32, (GRID_W, pw), 1) // NA_HEAD_DIM
    outs = []
    for p in range(k.shape[-1] // pw):
        sl = slice(p * pw, (p + 1) * pw)
        q4 = q_ref[:, sl]
        qbd = jnp.where(diag, jnp.concatenate([q4] * HEADS_PER_DOT, axis=0), jnp.zeros((), BF16))
        s = lax.dot_general(qbd, k[:, sl], (((1,), (1,)), ((), ())), preferred_element_type=F32)
        s = s + b_ref[0, p]
        m = jnp.max(s, axis=-1, keepdims=True)
        e = jnp.exp(s - m)
        l = jnp.sum(e, axis=-1, keepdims=True)
        o = jnp.dot(e.astype(BF16), v[:, sl], preferred_element_type=F32) / l
        acc = jnp.zeros((GRID_W, pw), F32)
        for h in range(HEADS_PER_DOT):
            acc = acc + jnp.where(out_head == h, o[h * GRID_W:(h + 1) * GRID_W], 0.0)
        outs.append(acc)
    y = jnp.concatenate(outs, axis=1)
    o_ref[...] = _rms(y, g_ref[...]).astype(BF16)


def _neighbourhood_attention(qkv, rpb, g):
    seq = qkv.shape[0]
    d_na = qkv.shape[1] // 3
    rows = seq // GRID_W
    bias = _na_bias_table(rpb)
    qkv3 = qkv.reshape(rows, GRID_W, 3 * d_na)

    def win_start(r):
        return jnp.clip(r - WIN_ROWS // 2, 0, rows - WIN_ROWS)

    window = (pl.Element(WIN_ROWS), pl.Element(GRID_W), pl.Element(d_na))

    return pl.pallas_call(
        _na_kernel,
        grid=(rows,),
        in_specs=[
            pl.BlockSpec((GRID_W, d_na), lambda r: (r, 0)),
            pl.BlockSpec(window, lambda r: (win_start(r), 0, d_na)),
            pl.BlockSpec(window, lambda r: (win_start(r), 0, 2 * d_na)),
            pl.BlockSpec((1,) + bias.shape[1:], lambda r: (r - win_start(r), 0, 0, 0)),
            pl.BlockSpec((1, d_na), lambda r: (0, 0)),
        ],
        out_specs=pl.BlockSpec((GRID_W, d_na), lambda r: (r, 0)),
        out_shape=jax.ShapeDtypeStruct((seq, d_na), BF16),
        compiler_params=_params("arbitrary"),
        name="na",
    )(qkv, qkv3, qkv3, bias, g.reshape(1, d_na))


def _outproj_kernel(a_ref, b_ref, w_ref, x_ref, o_ref):
    da = a_ref.shape[-1]
    acc = jnp.dot(a_ref[...], w_ref[:da, :].astype(BF16), preferred_element_type=F32)
    acc = acc + jnp.dot(b_ref[...], w_ref[da:, :].astype(BF16), preferred_element_type=F32)
    o_ref[...] = x_ref[...] + acc


def _out_proj(y_ssm, y_na, w_out, x):
    seq, d_model = x.shape
    da, db = y_ssm.shape[1], y_na.shape[1]
    tm = min(1024, seq)
    tn = 512
    return pl.pallas_call(
        _outproj_kernel,
        grid=(seq // tm, d_model // tn),
        in_specs=[
            pl.BlockSpec((tm, da), lambda i, j: (i, 0)),
            pl.BlockSpec((tm, db), lambda i, j: (i, 0)),
            pl.BlockSpec((da + db, tn), lambda i, j: (0, j)),
            pl.BlockSpec((tm, tn), lambda i, j: (i, j)),
        ],
        out_specs=pl.BlockSpec((tm, tn), lambda i, j: (i, j)),
        out_shape=jax.ShapeDtypeStruct((seq, d_model), F32),
        compiler_params=_params("arbitrary", "arbitrary"),
        name="out_proj",
    )(y_ssm, y_na, w_out, x)


def _router_kernel(x_ref, g_ref, wt_ref, h_ref, a_ref):
    h = _rms(x_ref[...], g_ref[...])
    h_ref[...] = h.astype(BF16)
    logits = lax.dot_general(wt_ref[...], h, (((1,), (1,)), ((), ())),
                             precision=lax.Precision.HIGHEST, preferred_element_type=F32)
    m = jnp.max(logits, axis=0, keepdims=True)
    e = jnp.exp(logits - m)
    a_ref[...] = e / jnp.sum(e, axis=0, keepdims=True)


def _router(x1, g, w_router):
    seq, d_model = x1.shape
    n_exp = w_router.shape[1]
    tm = min(512, seq)
    return pl.pallas_call(
        _router_kernel,
        grid=(seq // tm,),
        in_specs=[
            pl.BlockSpec((tm, d_model), lambda i: (i, 0)),
            pl.BlockSpec((1, d_model), lambda i: (0, 0)),
            pl.BlockSpec((n_exp, d_model), lambda i: (0, 0)),
        ],
        out_specs=[
            pl.BlockSpec((tm, d_model), lambda i: (i, 0)),
            pl.BlockSpec((n_exp, tm), lambda i: (0, i)),
        ],
        out_shape=[
            jax.ShapeDtypeStruct((seq, d_model), BF16),
            jax.ShapeDtypeStruct((n_exp, seq), F32),
        ],
        compiler_params=_params("arbitrary"),
        name="router",
    )(x1, g.reshape(1, d_model), w_router.T)


def _topk_kernel(a_ref, posw_ref, gate_ref, ws_ref, nr_ref, *, cap, blk, win):
    a = a_ref[...]
    n_exp, seq = a.shape
    n_blk = seq // blk
    bits = pltpu.bitcast(a, jnp.int32)

    def bit_step(i, thr):
        cand = thr | jnp.left_shift(jnp.int32(1), 30 - i)
        cnt = jnp.sum((bits >= cand).astype(jnp.int32), axis=-1, keepdims=True)
        return jnp.where(cnt >= cap, cand, thr)

    thr = lax.fori_loop(0, 31, bit_step, jnp.zeros((n_exp, 1), jnp.int32))
    gt = bits > thr
    eq = bits == thr
    need = cap - jnp.sum(gt.astype(jnp.int32), axis=-1, keepdims=True)

    tri = (lax.broadcasted_iota(jnp.int32, (blk, blk), 0)
           <= lax.broadcasted_iota(jnp.int32, (blk, blk), 1)).astype(BF16)
    blk_of_tok = lax.broadcasted_iota(jnp.int32, (seq, n_blk), 0) // blk
    tok_to_blk = (blk_of_tok == lax.broadcasted_iota(jnp.int32, (seq, n_blk), 1)).astype(BF16)
    blk_before = (lax.broadcasted_iota(jnp.int32, (n_blk, n_blk), 0)
                  < lax.broadcasted_iota(jnp.int32, (n_blk, n_blk), 1)).astype(BF16)
    erow = lax.broadcasted_iota(jnp.int32, (2 * n_blk, seq), 0)
    ecol = lax.broadcasted_iota(jnp.int32, (2 * n_blk, seq), 1) // blk
    expand = jnp.where(erow == ecol, 32.0, jnp.where(erow - n_blk == ecol, 1.0, 0.0)).astype(BF16)

    def prefix_counts(mask):
        mb = jnp.where(mask, 1.0, 0.0).astype(BF16)
        local = jnp.concatenate(
            [jnp.dot(mb[:, b * blk:(b + 1) * blk], tri, preferred_element_type=F32) for b in range(n_blk)],
            axis=1)
        per_blk = jnp.dot(mb, tok_to_blk, preferred_element_type=F32)
        start = jnp.dot(per_blk.astype(BF16), blk_before, preferred_element_type=F32)
        hi = jnp.floor(start * (1.0 / 32.0))
        parts = jnp.concatenate([hi, start - 32.0 * hi], axis=1).astype(BF16)
        start_tok = jnp.dot(parts, expand, preferred_element_type=F32)
        return local + start_tok, start, start_tok, per_blk

    eq_incl, _, _, _ = prefix_counts(eq)
    sel = gt | (eq & (eq_incl - 1.0 < need.astype(F32)))
    incl, start, start_tok, per_blk = prefix_counts(sel)

    def window(s):
        return jnp.floor(s * (1.0 / MOE_WIN_ALIGN)) * MOE_WIN_ALIGN

    posw_ref[...] = jnp.where(sel, (incl - 1.0 - window(start_tok)).astype(jnp.int32), -1)
    gate_ref[...] = jnp.where(sel, a, 0.0)
    ws_ref[...] = window(start).astype(jnp.int32)
    span = start - window(start) + per_blk
    rounds = jnp.floor((span + float(win - 1)) * (1.0 / win))
    nr_ref[...] = jnp.max(rounds, axis=0, keepdims=True).astype(jnp.int32)


def _topk(aff_t, cap, blk, win):
    n_exp, seq = aff_t.shape
    n_blk = seq // blk
    full = lambda *_: (0, 0)
    return pl.pallas_call(
        functools.partial(_topk_kernel, cap=cap, blk=blk, win=win),
        grid=(1,),
        in_specs=[pl.BlockSpec((n_exp, seq), full)],
        out_specs=[pl.BlockSpec((n_exp, seq), full), pl.BlockSpec((n_exp, seq), full),
                   pl.BlockSpec((n_exp, n_blk), full), pl.BlockSpec((1, n_blk), full)],
        out_shape=[
            jax.ShapeDtypeStruct((n_exp, seq), jnp.int32),
            jax.ShapeDtypeStruct((n_exp, seq), F32),
            jax.ShapeDtypeStruct((n_exp, n_blk), jnp.int32),
            jax.ShapeDtypeStruct((1, n_blk), jnp.int32),
        ],
        compiler_params=_params("arbitrary"),
        name="topk",
    )(aff_t)


def _window(ws_ref, e, b, r, n_blk, win, cap):
    ws = ws_ref[e * n_blk + b] + r * win
    start = jnp.minimum(ws, cap - win)
    return pl.multiple_of(start, MOE_WIN_ALIGN), ws - start


def _gather_kernel(ws_ref, nr_ref, h_ref, rel_ref, xe_ref, *, blk, win, n_blk):
    n_exp, cap, _ = xe_ref.shape
    xe_ref[...] = jnp.zeros_like(xe_ref)
    slot = lax.broadcasted_iota(jnp.int32, (win, blk), 0)

    def block(b, _):
        rows = h_ref[pl.ds(pl.multiple_of(b * blk, blk), blk), :]
        rel = rel_ref[b]

        def one_round(r, _):
            starts, hots = [], []
            for e in range(n_exp):
                start, shift = _window(ws_ref, e, b, r, n_blk, win, cap)
                relr = rel[e:e + 1, :] - r * win
                key = jnp.where(relr >= 0, relr + shift, -1)
                hots.append(jnp.where(slot == key, 1.0, 0.0).astype(BF16))
                starts.append(start)
            res = jnp.dot(jnp.concatenate(hots, axis=0), rows, preferred_element_type=F32)
            for e in range(n_exp):
                dst = pl.ds(starts[e], win)
                xe_ref[e, dst, :] = (xe_ref[e, dst, :].astype(F32) + res[e * win:(e + 1) * win]).astype(BF16)
            return 0

        lax.fori_loop(0, nr_ref[b], one_round, 0)
        return 0

    lax.fori_loop(0, n_blk, block, 0)


def _moe_gather(ws_flat, n_rounds, h2, rel3, cap, win):
    seq, d_model = h2.shape
    n_blk, n_exp, blk = rel3.shape
    dq = d_model // 4
    grid_spec = pltpu.PrefetchScalarGridSpec(
        num_scalar_prefetch=2,
        grid=(4,),
        in_specs=[
            pl.BlockSpec((seq, dq), lambda c, ws, nr: (0, c)),
            pl.BlockSpec((n_blk, n_exp, blk), lambda c, ws, nr: (0, 0, 0)),
        ],
        out_specs=pl.BlockSpec((n_exp, cap, dq), lambda c, ws, nr: (0, 0, c)),
    )
    return pl.pallas_call(
        functools.partial(_gather_kernel, blk=blk, win=win, n_blk=n_blk),
        grid_spec=grid_spec,
        out_shape=jax.ShapeDtypeStruct((n_exp, cap, d_model), BF16),
        compiler_params=_params("arbitrary"),
        name="moe_gather",
    )(ws_flat, n_rounds, h2, rel3)


def _ffn_kernel(x_ref, wg_ref, wu_ref, wd_ref, y_ref, acc_ref):
    f = pl.program_id(1)
    x = x_ref[0]
    g = jnp.dot(x, wg_ref[0].astype(BF16), preferred_element_type=F32)
    u = jnp.dot(x, wu_ref[0].astype(BF16), preferred_element_type=F32)
    act = (g * (1.0 / (1.0 + jnp.exp(-g))) * u).astype(BF16)
    part = jnp.dot(act, wd_ref[0].astype(BF16), preferred_element_type=F32)

    @pl.when(f == 0)
    def _():
        acc_ref[...] = part

    @pl.when(f > 0)
    def _():
        acc_ref[...] += part

    @pl.when(f == pl.num_programs(1) - 1)
    def _():
        y_ref[0] = acc_ref[...].astype(BF16)


def _moe_ffn(xe, w_gate, w_up, w_down):
    n_exp, cap, d_model = xe.shape
    d_ff = w_gate.shape[-1]
    tf = 256
    return pl.pallas_call(
        _ffn_kernel,
        grid=(n_exp, d_ff // tf),
        in_specs=[
            pl.BlockSpec((1, cap, d_model), lambda e, f: (e, 0, 0)),
            pl.BlockSpec((1, d_model, tf), lambda e, f: (e, 0, f)),
            pl.BlockSpec((1, d_model, tf), lambda e, f: (e, 0, f)),
            pl.BlockSpec((1, tf, d_model), lambda e, f: (e, f, 0)),
        ],
        out_specs=pl.BlockSpec((1, cap, d_model), lambda e, f: (e, 0, 0)),
        out_shape=jax.ShapeDtypeStruct((n_exp, cap, d_model), BF16),
        scratch_shapes=[pltpu.VMEM((cap, d_model), F32)],
        compiler_params=_params("arbitrary", "arbitrary"),
        name="moe_ffn",
    )(xe, w_gate, w_up, w_down)


def _combine_kernel(ws_ref, nr_ref, ye_ref, x_ref, rel_ref, gate_ref, o_ref, *, blk, win, n_blk):
    n_exp, cap, _ = ye_ref.shape
    sub = x_ref.shape[0] // blk
    slot = lax.broadcasted_iota(jnp.int32, (win, blk), 0)

    def block(s, _):
        b = pl.program_id(1) * sub + s
        tok = pl.ds(pl.multiple_of(s * blk, blk), blk)
        rel = rel_ref[b]
        gate = gate_ref[b]

        def one_round(r, acc):
            his, los, wins = [], [], []
            for e in range(n_exp):
                start, shift = _window(ws_ref, e, b, r, n_blk, win, cap)
                relr = rel[e:e + 1, :] - r * win
                key = jnp.where(relr >= 0, relr + shift, -1)
                g = jnp.where(slot == key, gate[e:e + 1, :], 0.0)
                g_hi = g.astype(BF16)
                his.append(g_hi)
                los.append((g - g_hi.astype(F32)).astype(BF16))
                wins.append(ye_ref[e, pl.ds(start, win), :])
            lhs_t = jnp.concatenate([jnp.concatenate(his, axis=0), jnp.concatenate(los, axis=0)], axis=1)
            res = lax.dot_general(lhs_t, jnp.concatenate(wins, axis=0), (((0,), (0,)), ((), ())),
                                  preferred_element_type=F32)
            return acc + res[:blk] + res[blk:]

        o_ref[tok, :] = lax.fori_loop(0, nr_ref[b], one_round, x_ref[tok, :])
        return 0

    lax.fori_loop(0, sub, block, 0)


def _moe_combine(ws_flat, n_rounds, ye, x1, rel3, gate3, win):
    seq, d_model = x1.shape
    n_exp, cap, _ = ye.shape
    n_blk, _, blk = rel3.shape
    dq = d_model // 4
    tile = min(8, n_blk) * blk
    whole = lambda c, t, ws, nr: (0, 0, 0)
    grid_spec = pltpu.PrefetchScalarGridSpec(
        num_scalar_prefetch=2,
        grid=(4, seq // tile),
        in_specs=[
            pl.BlockSpec((n_exp, cap, dq), lambda c, t, ws, nr: (0, 0, c)),
            pl.BlockSpec((tile, dq), lambda c, t, ws, nr: (t, c)),
            pl.BlockSpec(rel3.shape, whole),
            pl.BlockSpec(gate3.shape, whole),
        ],
        out_specs=pl.BlockSpec((tile, dq), lambda c, t, ws, nr: (t, c)),
    )
    return pl.pallas_call(
        functools.partial(_combine_kernel, blk=blk, win=win, n_blk=n_blk),
        grid_spec=grid_spec,
        out_shape=jax.ShapeDtypeStruct((seq, d_model), F32),
        compiler_params=_params("arbitrary", "arbitrary"),
        name="moe_combine",
    )(ws_flat, n_rounds, ye, x1, rel3, gate3)


def _final_norm_kernel(x_ref, g_ref, o_ref):
    o_ref[...] = _rms(x_ref[...], g_ref[...])


def _final_norm(x, g):
    seq, d_model = x.shape
    tm = min(512, seq)
    return pl.pallas_call(
        _final_norm_kernel,
        grid=(seq // tm,),
        in_specs=[pl.BlockSpec((tm, d_model), lambda i: (i, 0)), pl.BlockSpec((1, d_model), lambda i: (0, 0))],
        out_specs=pl.BlockSpec((tm, d_model), lambda i: (i, 0)),
        out_shape=jax.ShapeDtypeStruct((seq, d_model), F32),
        compiler_params=_params("arbitrary"),
        name="final_norm",
    )(x, g.reshape(1, d_model))


def _layer(x, norm_mix_g, w_in, fwd, bwd, ssm_d, w_glu, b_glu, na_rpb, g_ssm_out, g_na_out, w_out,
           norm_ffn_g, w_router, w_gate, w_up, w_down):
    seq, d_model = x.shape
    d_ssm = ssm_d.shape[0]
    d_na = g_na_out.shape[0]
    n_exp = w_router.shape[1]
    cap = EC_CAPACITY_FACTOR * seq // n_exp
    blk = min(MOE_TOK_BLOCK, cap // 2)
    win = min(MOE_WIN, cap)

    u, qkv = _in_proj(x, norm_mix_g, w_in, d_ssm, d_na)
    y_ssm = _s5_post(_s5_scan(u, fwd, bwd), u, ssm_d, w_glu, b_glu, g_ssm_out)
    y_na = _neighbourhood_attention(qkv, na_rpb, g_na_out)
    x1 = _out_proj(y_ssm, y_na, w_out, x)

    h2, aff_t = _router(x1, norm_ffn_g, w_router)
    rel, gate, ws, n_rounds = _topk(aff_t, cap, blk, win)
    ws_flat = ws.reshape(-1)
    n_rounds = n_rounds.reshape(-1)
    rel3 = jnp.swapaxes(rel.reshape(n_exp, seq // blk, blk), 0, 1)
    gate3 = jnp.swapaxes(gate.reshape(n_exp, seq // blk, blk), 0, 1)
    xe = _moe_gather(ws_flat, n_rounds, h2, rel3, cap, win)
    ye = _moe_ffn(xe, w_gate, w_up, w_down)
    return _moe_combine(ws_flat, n_rounds, ye, x1, rel3, gate3, win)


def kernel(x, norm_mix_g, w_in, a_re_fwd, a_im_fwd, log_dt_fwd, b_re_fwd, b_im_fwd, c_re_fwd, c_im_fwd, a_re_bwd, a_im_bwd, log_dt_bwd, b_re_bwd, b_im_bwd, c_re_bwd, c_im_bwd, ssm_d, w_glu, b_glu, na_rpb, g_ssm_out, g_na_out, w_out, norm_ffn_g, w_router, w_gate, w_up, w_down, norm_final_g):
    bsz = x.shape[0]
    depth = w_in.shape[0]
    outs = []
    for b in range(bsz):
        xb = x[b]
        for l in range(depth):
            fwd = (a_re_fwd[l], a_im_fwd[l], log_dt_fwd[l], b_re_fwd[l], b_im_fwd[l], c_re_fwd[l], c_im_fwd[l])
            bwd = (a_re_bwd[l], a_im_bwd[l], log_dt_bwd[l], b_re_bwd[l], b_im_bwd[l], c_re_bwd[l], c_im_bwd[l])
            xb = _layer(xb, norm_mix_g[l], w_in[l], fwd, bwd, ssm_d[l], w_glu[l], b_glu[l], na_rpb[l],
                        g_ssm_out[l], g_na_out[l], w_out[l], norm_ffn_g[l], w_router[l],
                        w_gate[l], w_up[l], w_down[l])
        outs.append(_final_norm(xb, norm_final_g))
    return jnp.stack(outs)
```

```python
import functools

import numpy as np
import jax
import jax.numpy as jnp
from jax import lax
from jax.experimental import pallas as pl
from jax.experimental.pallas import tpu as pltpu

F32 = jnp.float32
BF16 = jnp.bfloat16

RMS_EPS = 1e-6
SSM_GROUP = 16
SSM_STATE = 64
NA_HEADS = 16
NA_HEAD_DIM = 64
GRID_W = 64
WIN_ROWS = 8
WIN_COLS = 16
N_EXPERTS = 16
EC_CAPACITY_FACTOR = 2

S5_CHUNK = 16
S5_SEGS = 8
HEADS_PER_DOT = 4
MOE_TOK_BLOCK = 128
MOE_WIN_ALIGN = 16
MOE_WIN = 64
MASK_NEG = -1e30

VMEM_LIMIT_BYTES = 56 * 1024 * 1024


def _params(*semantics):
    return pltpu.CompilerParams(dimension_semantics=semantics, vmem_limit_bytes=VMEM_LIMIT_BYTES)


def _rms(x, g):
    ms = jnp.mean(x * x, axis=-1, keepdims=True)
    return x * lax.rsqrt(ms + RMS_EPS) * g


def _inproj_kernel(x_ref, g_ref, w_ref, u_ref, qkv_ref, h_scr, *, n_u, n_q, q_scale):
    j = pl.program_id(1)

    @pl.when(j == 0)
    def _():
        h_scr[...] = _rms(x_ref[...], g_ref[...]).astype(BF16)

    def project():
        return jnp.dot(h_scr[...], w_ref[...].astype(BF16), preferred_element_type=F32)

    @pl.when(j < n_u)
    def _():
        u_ref[...] = project()

    @pl.when(j >= n_u)
    def _():
        scale = jnp.where(j < n_u + n_q, q_scale, 1.0).astype(F32)
        qkv_ref[...] = (project() * scale).astype(BF16)


def _in_proj(x, g, w_in, d_ssm, d_na):
    seq, d_model = x.shape
    tm = min(1024, seq)
    tn = 512
    n_u, n_q = d_ssm // tn, d_na // tn
    n_cols = w_in.shape[1] // tn
    kern = functools.partial(_inproj_kernel, n_u=n_u, n_q=n_q, q_scale=NA_HEAD_DIM ** -0.5)
    return pl.pallas_call(
        kern,
        grid=(seq // tm, n_cols),
        in_specs=[
            pl.BlockSpec((tm, d_model), lambda i, j: (i, 0)),
            pl.BlockSpec((1, d_model), lambda i, j: (0, 0)),
            pl.BlockSpec((d_model, tn), lambda i, j: (0, j)),
        ],
        out_specs=[
            pl.BlockSpec((tm, tn), lambda i, j: (i, jnp.minimum(j, n_u - 1))),
            pl.BlockSpec((tm, tn), lambda i, j: (i, jnp.maximum(j - n_u, 0))),
        ],
        out_shape=[
            jax.ShapeDtypeStruct((seq, d_ssm), F32),
            jax.ShapeDtypeStruct((seq, 3 * d_na), BF16),
        ],
        scratch_shapes=[pltpu.VMEM((tm, d_model), BF16)],
        compiler_params=_params("arbitrary", "arbitrary"),
        name="in_proj",
    )(x, g.reshape(1, d_model), w_in)


def _s5_direction_tables(a_re, a_im, log_dt, b_re, b_im, c_re, c_im, n_tiles, reverse):
    t_len = S5_CHUNK
    n_ch = b_re.shape[-1]
    hp = lax.Precision.HIGHEST
    dt = jnp.exp(log_dt.astype(F32))[:, None]
    xr, xi = a_re.astype(F32) * dt, a_im.astype(F32) * dt

    def power(xr_, xi_, k):
        k = k.reshape((-1,) + (1,) * xr_.ndim)
        mag = jnp.exp(xr_[None] * k)
        return mag * jnp.cos(xi_[None] * k), mag * jnp.sin(xi_[None] * k)

    pr, pi = power(xr, xi, jnp.arange(t_len + 1, dtype=F32))
    nr, ni = pr[1] - 1.0, pi[1]
    den = a_re * a_re + a_im * a_im
    qr, qi = (nr * a_re + ni * a_im) / den, (ni * a_re - nr * a_im) / den
    bb_r, bb_i = _cmul(qr[:, None, :], qi[:, None, :], jnp.swapaxes(b_re, 1, 2), jnp.swapaxes(b_im, 1, 2))
    ct_r, ct_i = jnp.swapaxes(c_re, 1, 2).astype(F32), jnp.swapaxes(c_im, 1, 2).astype(F32)
    apt_r, apt_i = jnp.transpose(pr, (1, 2, 0)), jnp.transpose(pi, (1, 2, 0))

    def c_times_power(n_lanes, lag_of_lane):
        lane = np.arange(n_lanes)
        pick_c = (lane[None, :] % n_ch == np.arange(n_ch)[:, None]).astype(np.float32)
        pick_k = (lag_of_lane[lane // n_ch][None, :] == np.arange(t_len + 1)[:, None]).astype(np.float32)
        cr = jnp.einsum('gpo,oq->gpq', ct_r, pick_c, precision=hp)
        ci = jnp.einsum('gpo,oq->gpq', ct_i, pick_c, precision=hp)
        ar = jnp.einsum('gpk,kq->gpq', apt_r, pick_k, precision=hp)
        ai = jnp.einsum('gpk,kq->gpq', apt_i, pick_k, precision=hp)
        return _cmul(cr, ci, ar, ai)

    slots = np.arange(2 * t_len)
    lag_of_slot = (t_len - 1 - slots) if reverse else (slots - (t_len - 1))
    lag_of_slot = np.where((lag_of_slot >= 0) & (lag_of_slot < t_len), lag_of_slot, -1)
    yr, yi = c_times_power(2 * t_len * n_ch, lag_of_slot)
    strip_lhs = jnp.concatenate([bb_r, -bb_i], axis=-1)
    strip_rhs = jnp.concatenate([yr, yi], axis=1)
    e_in = (np.arange(t_len)) if reverse else (t_len - 1 - np.arange(t_len))
    st_r, st_i = _cmul(jnp.swapaxes(pr[e_in], 0, 1)[:, :, None, :], jnp.swapaxes(pi[e_in], 0, 1)[:, :, None, :],
                       bb_r[:, None], bb_i[:, None])
    e_out = (t_len - np.arange(t_len)) if reverse else (np.arange(t_len) + 1)
    wr, wi = c_times_power(t_len * n_ch, e_out)
    xrf, xif = xr.reshape(-1), xi.reshape(-1)
    tiles = jnp.arange(n_tiles, dtype=F32)
    if reverse:
        tiles = tiles[::-1]
    a_chunk = power(xrf, xif, jnp.full((1,), float(t_len), F32))
    a_seg = power(xrf, xif, jnp.full((1,), float(t_len * n_tiles), F32))
    a_tile = power(xrf, xif, float(t_len) * tiles)
    return (strip_lhs, strip_rhs), (st_r, st_i), (wr, -wi), a_chunk, a_seg, a_tile


def _cmul(ar, ai, br, bi):
    return ar * br - ai * bi, ar * bi + ai * br


def _s5_tables(fwd, bwd, n_tiles):
    tf = _s5_direction_tables(*fwd, n_tiles=n_tiles, reverse=False)
    tb = _s5_direction_tables(*bwd, n_tiles=n_tiles, reverse=True)
    n_groups = tf[1][0].shape[0]

    strip = jnp.einsum('gck,gkq->gcq', jnp.concatenate([tf[0][0], tb[0][0]], axis=-1),
                       jnp.concatenate([tf[0][1], tb[0][1]], axis=1), precision=lax.Precision.HIGHEST)

    odd = (jnp.arange(n_groups) % 2 == 1)[:, None, None, None]
    pieces = []
    for plane in (tf[1][0], tf[1][1], tb[1][0], tb[1][1]):
        pieces += [jnp.where(odd, 0.0, plane), jnp.where(odd, plane, 0.0)]
    e_grp = jnp.concatenate(pieces, axis=-1).astype(BF16)

    c_grp = jnp.stack([tf[2][0], tf[2][1], tb[2][0], tb[2][1]], axis=1).astype(BF16)

    ap = jnp.concatenate([tf[3][0], tf[3][1], tb[3][0], tb[3][1], tf[4][0], tf[4][1], tb[4][0], tb[4][1]], axis=0)
    pw = jnp.stack([tf[5][0], tf[5][1], tb[5][0], tb[5][1]])
    return strip, e_grp, c_grp, ap, pw


def _s5_a_kernel(u_ref, strip_ref, e_ref, y_ref, s_ref, w_blk, e_blk, ucat, *, n_tiles, seg_stride):
    n_grp = strip_ref.shape[0]
    cw = S5_CHUNK * SSM_GROUP
    pc = e_ref.shape[-1]
    lanes = u_ref.shape[-1]

    @pl.when(pl.program_id(0) == 0)
    def _():
        w_blk[...] = jnp.zeros_like(w_blk)
        e_blk[...] = jnp.zeros_like(e_blk)

    for g in range(n_grp):
        strip = strip_ref[g]
        for t in range(S5_CHUNK):
            r0 = t * lanes + g * SSM_GROUP
            lo = (S5_CHUNK - 1 - t) * SSM_GROUP
            w_blk[r0:r0 + SSM_GROUP, g * cw:(g + 1) * cw] = strip[:, lo:lo + cw].astype(BF16)
            e_blk[r0:r0 + SSM_GROUP, (g // 2) * pc:(g // 2 + 1) * pc] = e_ref[g, t]

    def gather(jj, _):
        for t in range(S5_CHUNK):
            rows = [u_ref[pl.ds(S5_CHUNK * (2 * jj + h) + t, S5_SEGS, stride=seg_stride), :] for h in range(2)]
            dst = pl.ds(pl.multiple_of(jj * 2 * S5_SEGS, 2 * S5_SEGS), 2 * S5_SEGS)
            ucat[dst, t * lanes:(t + 1) * lanes] = jnp.concatenate(rows, axis=0).astype(BF16)
        return 0

    lax.fori_loop(0, n_tiles // 2, gather, 0)
    ub = ucat[...]
    y_ref[...] = jnp.dot(ub, w_blk[...], preferred_element_type=F32)
    s_ref[...] = jnp.dot(ub, e_blk[...], preferred_element_type=F32)


def _s5_bc_kernel(s_ref, yi_ref, c_ref, ap_ref, pw_ref, y_ref, sin, perm, ynat, c_blk, *, n_tiles, seg_stride):
    n_grp, n_plane, n_state, cw = c_ref.shape
    n_pair = n_grp // 2
    pw_ = 2 * n_state
    pc = n_plane * pw_
    lanes = y_ref.shape[-1]

    @pl.when(pl.program_id(0) == 0)
    def _():
        c_blk[...] = jnp.zeros_like(c_blk)
        rl = lax.broadcasted_iota(jnp.int32, (cw, n_grp * cw), 0)
        cl = lax.broadcasted_iota(jnp.int32, (cw, n_grp * cw), 1)
        for g in range(n_grp):
            tgt = (rl // SSM_GROUP) * lanes + g * SSM_GROUP + (rl % SSM_GROUP)
            perm[g * cw:(g + 1) * cw, :] = jnp.where(cl == tgt, 1.0, 0.0).astype(BF16)

    for g in range(n_grp):
        gl = g % 2
        for k in range(n_plane):
            r0 = k * pw_ + gl * n_state
            c_blk[g // 2, r0:r0 + n_state, gl * cw:(gl + 1) * cw] = c_ref[g, k]

    chains = [(q, d) for q in range(n_pair) for d in range(2)]

    def plane_lanes(q, d):
        base = q * pc + d * 2 * pw_
        return slice(base, base + pw_), slice(base + pw_, base + 2 * pw_)

    def coef(row, q):
        return ap_ref[row:row + 1, q * pw_:(q + 1) * pw_]

    def tile_rows(n, d):
        j = n if d == 0 else n_tiles - 1 - n
        return pl.ds(pl.multiple_of(j * S5_SEGS, S5_SEGS), S5_SEGS)

    def step(n, carry):
        out = []
        for (q, d), (zr, zi) in zip(chains, carry):
            re, im = plane_lanes(q, d)
            rows = tile_rows(n, d)
            sin[rows, re] = zr
            sin[rows, im] = zi
            ar, ai = coef(2 * d, q), coef(2 * d + 1, q)
            out.append((ar * zr - ai * zi + s_ref[rows, re], ar * zi + ai * zr + s_ref[rows, im]))
        return tuple(out)

    z0 = jnp.zeros((S5_SEGS, pw_), F32)
    ends = lax.fori_loop(0, n_tiles, step, tuple((z0, z0) for _ in chains))

    carries = []
    for (q, d), (zr, zi) in zip(chains, ends):
        sr, si = coef(4 + 2 * d, q), coef(5 + 2 * d, q)
        cr = jnp.zeros((1, pw_), F32)
        ci = jnp.zeros((1, pw_), F32)
        seg_r = [None] * S5_SEGS
        seg_i = [None] * S5_SEGS
        for s in (range(S5_SEGS) if d == 0 else range(S5_SEGS - 1, -1, -1)):
            seg_r[s], seg_i[s] = cr, ci
            cr, ci = (zr[s:s + 1] + sr * cr - si * ci, zi[s:s + 1] + sr * ci + si * cr)
        carries.append((jnp.concatenate(seg_r, axis=0), jnp.concatenate(seg_i, axis=0)))

    def fix(n8, _):
        tiles = pl.ds(pl.multiple_of(n8 * 8, 8), 8)
        for (q, d), (car_r, car_i) in zip(chains, carries):
            re, im = plane_lanes(q, d)
            pr8 = pw_ref[2 * d, tiles, q * pw_:(q + 1) * pw_]
            pi8 = pw_ref[2 * d + 1, tiles, q * pw_:(q + 1) * pw_]
            for r in range(8):
                rows = pl.ds(pl.multiple_of((n8 * 8 + r) * S5_SEGS, S5_SEGS), S5_SEGS)
                pr, pi = pr8[r:r + 1], pi8[r:r + 1]
                sin[rows, re] = sin[rows, re] + (pr * car_r - pi * car_i)
                sin[rows, im] = sin[rows, im] + (pr * car_i + pi * car_r)
        return 0

    lax.fori_loop(0, n_tiles // 8, fix, 0)

    parts = []
    for q in range(n_pair):
        lhs = sin[:, q * pc:(q + 1) * pc].astype(BF16)
        parts.append(yi_ref[:, q * 2 * cw:(q + 1) * 2 * cw] + jnp.dot(lhs, c_blk[q], preferred_element_type=F32))
    y_chunk = jnp.concatenate(parts, axis=1).astype(BF16)
    ynat[...] = jnp.dot(y_chunk, perm[...], preferred_element_type=F32)

    def scatter(j, _):
        src = pl.ds(pl.multiple_of(j * S5_SEGS, S5_SEGS), S5_SEGS)
        for t in range(S5_CHUNK):
            y_ref[pl.ds(S5_CHUNK * j + t, S5_SEGS, stride=seg_stride), :] = ynat[src, t * lanes:(t + 1) * lanes]
        return 0

    lax.fori_loop(0, n_tiles, scatter, 0)


def _s5_scan(u, fwd, bwd):
    seq, d_ssm = u.shape
    n_groups = d_ssm // SSM_GROUP
    n_state = SSM_STATE
    cw = S5_CHUNK * SSM_GROUP
    n_rows = seq // S5_CHUNK
    n_tiles = n_rows // S5_SEGS
    lanes = 128
    gpb = lanes // SSM_GROUP
    n_blocks = d_ssm // lanes
    bw = gpb * cw
    sw = gpb * 4 * n_state
    seg_stride = n_tiles * S5_CHUNK

    strip, e_grp, c_grp, ap, pw = _s5_tables(fwd, bwd, n_tiles)

    y_intra, s_loc = pl.pallas_call(
        functools.partial(_s5_a_kernel, n_tiles=n_tiles, seg_stride=seg_stride),
        grid=(n_blocks,),
        in_specs=[
            pl.BlockSpec((seq, lanes), lambda i: (0, i)),
            pl.BlockSpec((gpb,) + strip.shape[1:], lambda i: (i, 0, 0)),
            pl.BlockSpec((gpb,) + e_grp.shape[1:], lambda i: (i, 0, 0, 0)),
        ],
        out_specs=[
            pl.BlockSpec((n_rows, bw), lambda i: (0, i)),
            pl.BlockSpec((n_rows, sw), lambda i: (0, i)),
        ],
        out_shape=[
            jax.ShapeDtypeStruct((n_rows, n_blocks * bw), F32),
            jax.ShapeDtypeStruct((n_rows, n_blocks * sw), F32),
        ],
        scratch_shapes=[
            pltpu.VMEM((S5_CHUNK * lanes, bw), BF16),
            pltpu.VMEM((S5_CHUNK * lanes, sw), BF16),
            pltpu.VMEM((n_rows, S5_CHUNK * lanes), BF16),
        ],
        compiler_params=_params("arbitrary"),
        name="s5_a",
    )(u, strip, e_grp)

    pl_lanes = gpb * n_state
    return pl.pallas_call(
        functools.partial(_s5_bc_kernel, n_tiles=n_tiles, seg_stride=seg_stride),
        grid=(n_blocks,),
        in_specs=[
            pl.BlockSpec((n_rows, sw), lambda i: (0, i)),
            pl.BlockSpec((n_rows, bw), lambda i: (0, i)),
            pl.BlockSpec((gpb,) + c_grp.shape[1:], lambda i: (i, 0, 0, 0)),
            pl.BlockSpec((8, pl_lanes), lambda i: (0, i)),
            pl.BlockSpec((4, n_tiles, pl_lanes), lambda i: (0, 0, i)),
        ],
        out_specs=pl.BlockSpec((seq, lanes), lambda i: (0, i)),
        out_shape=jax.ShapeDtypeStruct((seq, d_ssm), F32),
        scratch_shapes=[
            pltpu.VMEM((n_rows, sw), F32),
            pltpu.VMEM((bw, bw), BF16),
            pltpu.VMEM((n_rows, bw), F32),
            pltpu.VMEM((gpb // 2, 8 * n_state, 2 * cw), BF16),
        ],
        compiler_params=_params("arbitrary"),
        name="s5_bc",
    )(s_loc, y_intra, c_grp, ap, pw)


def _s5_post_kernel(y_ref, u_ref, d_ref, w_ref, b_ref, g_ref, o_ref):
    y = y_ref[...] + d_ref[...] * u_ref[...]
    c0 = np.float32(np.sqrt(2.0 / np.pi))
    y = 0.5 * y * (1.0 + jnp.tanh(c0 * (y + np.float32(0.044715) * (y * y * y))))
    z = jnp.dot(y.astype(BF16), w_ref[...].astype(BF16), preferred_element_type=F32) + b_ref[...]
    o = y * (1.0 / (1.0 + jnp.exp(-z)))
    o_ref[...] = _rms(o, g_ref[...]).astype(BF16)


def _s5_post(y, u, d_skip, w_glu, b_glu, g):
    seq, d = y.shape
    tm = min(512, seq)
    row = lambda i: (i, 0)
    fix = lambda i: (0, 0)
    return pl.pallas_call(
        _s5_post_kernel,
        grid=(seq // tm,),
        in_specs=[
            pl.BlockSpec((tm, d), row), pl.BlockSpec((tm, d), row), pl.BlockSpec((1, d), fix),
            pl.BlockSpec((d, d), fix), pl.BlockSpec((1, d), fix), pl.BlockSpec((1, d), fix),
        ],
        out_specs=pl.BlockSpec((tm, d), row),
        out_shape=jax.ShapeDtypeStruct((seq, d), BF16),
        compiler_params=_params("arbitrary"),
        name="s5_post",
    )(y, u, d_skip.reshape(1, d), w_glu, b_glu.reshape(1, d), g.reshape(1, d))


def _na_bias_table(rpb):
    n_heads = rpb.shape[0]
    cols = np.arange(GRID_W)
    col_start = np.clip(cols - WIN_COLS // 2, 0, GRID_W - WIN_COLS)
    key_cols = np.arange(GRID_W)
    in_win = (key_cols[None, :] >= col_start[:, None]) & (key_cols[None, :] < col_start[:, None] + WIN_COLS)
    dx = key_cols[None, :] - cols[:, None] + (WIN_COLS - 1)
    dy = np.arange(WIN_ROWS)[None, :] - np.arange(WIN_ROWS)[:, None] + (WIN_ROWS - 1)
    pick_x = (dx[:, :, None] == np.arange(2 * WIN_COLS - 1)).astype(np.float32)
    pick_y = (dy[:, :, None] == np.arange(2 * WIN_ROWS - 1)).astype(np.float32)
    hp = lax.Precision.HIGHEST
    b = jnp.einsum('hyx,ckx->hyck', rpb.astype(F32), pick_x, precision=hp)
    b = jnp.einsum('viy,hyck->vhcik', pick_y, b, precision=hp)
    b = jnp.where(in_win[None, None, :, None, :], b, MASK_NEG)
    return b.reshape(WIN_ROWS, n_heads // HEADS_PER_DOT, HEADS_PER_DOT * GRID_W, WIN_ROWS * GRID_W)


def _na_kernel(q_ref, k_ref, v_ref, b_ref, g_ref, o_ref):
    n_keys = WIN_ROWS * GRID_W
    k = k_ref[...].reshape(n_keys, k_ref.shape[-1])
    v = v_ref[...].reshape(n_keys, v_ref.shape[-1])
    pw = HEADS_PER_DOT * NA_HEAD_DIM
    row_head = lax.broadcasted_iota(jnp.int32, (HEADS_PER_DOT * GRID_W, pw), 0) // GRID_W
    col_head = lax.broadcasted_iota(jnp.int32, (HEADS_PER_DOT * GRID_W, pw), 1) // NA_HEAD_DIM
    diag = row_head == col_head
    out_head = lax.broadcasted_iota(jnp.int32, (GRID_W, pw), 1) // NA_HEAD_DIM
    outs = []
    for p in range(k.shape[-1] // pw):
        sl = slice(p * pw, (p + 1) * pw)
        q4 = q_ref[:, sl]
        qbd = jnp.where(diag, jnp.concatenate([q4] * HEADS_PER_DOT, axis=0), jnp.zeros((), BF16))
        s = lax.dot_general(qbd, k[:, sl], (((1,), (1,)), ((), ())), preferred_element_type=F32)
        s = s + b_ref[0, p]
        m = jnp.max(s, axis=-1, keepdims=True)
        e = jnp.exp(s - m)
        l = jnp.sum(e, axis=-1, keepdims=True)
        o = jnp.dot(e.astype(BF16), v[:, sl], preferred_element_type=F32) / l
        acc = jnp.zeros((GRID_W, pw), F32)
        for h in range(HEADS_PER_DOT):
            acc = acc + jnp.where(out_head == h, o[h * GRID_W:(h + 1) * GRID_W], 0.0)
        outs.append(acc)
    y = jnp.concatenate(outs, axis=1)
    o_ref[...] = _rms(y, g_ref[...]).astype(BF16)


def _neighbourhood_attention(qkv, rpb, g):
    seq = qkv.shape[0]
    d_na = qkv.shape[1] // 3
    rows = seq // GRID_W
    bias = _na_bias_table(rpb)
    qkv3 = qkv.reshape(rows, GRID_W, 3 * d_na)

    def win_start(r):
        return jnp.clip(r - WIN_ROWS // 2, 0, rows - WIN_ROWS)

    window = (pl.Element(WIN_ROWS), pl.Element(GRID_W), pl.Element(d_na))

    return pl.pallas_call(
        _na_kernel,
        grid=(rows,),
        in_specs=[
            pl.BlockSpec((GRID_W, d_na), lambda r: (r, 0)),
            pl.BlockSpec(window, lambda r: (win_start(r), 0, d_na)),
            pl.BlockSpec(window, lambda r: (win_start(r), 0, 2 * d_na)),
            pl.BlockSpec((1,) + bias.shape[1:], lambda r: (r - win_start(r), 0, 0, 0)),
            pl.BlockSpec((1, d_na), lambda r: (0, 0)),
        ],
        out_specs=pl.BlockSpec((GRID_W, d_na), lambda r: (r, 0)),
        out_shape=jax.ShapeDtypeStruct((seq, d_na), BF16),
        compiler_params=_params("arbitrary"),
        name="na",
    )(qkv, qkv3, qkv3, bias, g.reshape(1, d_na))


def _outproj_kernel(a_ref, b_ref, w_ref, x_ref, o_ref):
    da = a_ref.shape[-1]
    acc = jnp.dot(a_ref[...], w_ref[:da, :].astype(BF16), preferred_element_type=F32)
    acc = acc + jnp.dot(b_ref[...], w_ref[da:, :].astype(BF16), preferred_element_type=F32)
    o_ref[...] = x_ref[...] + acc


def _out_proj(y_ssm, y_na, w_out, x):
    seq, d_model = x.shape
    da, db = y_ssm.shape[1], y_na.shape[1]
    tm = min(2048, seq)
    tn = 512
    return pl.pallas_call(
        _outproj_kernel,
        grid=(seq // tm, d_model // tn),
        in_specs=[
            pl.BlockSpec((tm, da), lambda i, j: (i, 0)),
            pl.BlockSpec((tm, db), lambda i, j: (i, 0)),
            pl.BlockSpec((da + db, tn), lambda i, j: (0, j)),
            pl.BlockSpec((tm, tn), lambda i, j: (i, j)),
        ],
        out_specs=pl.BlockSpec((tm, tn), lambda i, j: (i, j)),
        out_shape=jax.ShapeDtypeStruct((seq, d_model), F32),
        compiler_params=_params("arbitrary", "arbitrary"),
        name="out_proj",
    )(y_ssm, y_na, w_out, x)


def _router_kernel(x_ref, g_ref, wt_ref, h_ref, a_ref):
    h = _rms(x_ref[...], g_ref[...])
    h_ref[...] = h.astype(BF16)
    logits = lax.dot_general(wt_ref[...], h, (((1,), (1,)), ((), ())),
                             precision=lax.Precision.HIGHEST, preferred_element_type=F32)
    m = jnp.max(logits, axis=0, keepdims=True)
    e = jnp.exp(logits - m)
    a_ref[...] = e / jnp.sum(e, axis=0, keepdims=True)


def _router(x1, g, w_router):
    seq, d_model = x1.shape
    n_exp = w_router.shape[1]
    tm = min(512, seq)
    return pl.pallas_call(
        _router_kernel,
        grid=(seq // tm,),
        in_specs=[
            pl.BlockSpec((tm, d_model), lambda i: (i, 0)),
            pl.BlockSpec((1, d_model), lambda i: (0, 0)),
            pl.BlockSpec((n_exp, d_model), lambda i: (0, 0)),
        ],
        out_specs=[
            pl.BlockSpec((tm, d_model), lambda i: (i, 0)),
            pl.BlockSpec((n_exp, tm), lambda i: (0, i)),
        ],
        out_shape=[
            jax.ShapeDtypeStruct((seq, d_model), BF16),
            jax.ShapeDtypeStruct((n_exp, seq), F32),
        ],
        compiler_params=_params("arbitrary"),
        name="router",
    )(x1, g.reshape(1, d_model), w_router.T)


def _topk_kernel(a_ref, posw_ref, gate_ref, ws_ref, nr_ref, *, cap, blk, win):
    a = a_ref[...]
    n_exp, seq = a.shape
    n_blk = seq // blk
    bits = pltpu.bitcast(a, jnp.int32)

    def bit_step(i, thr):
        cand = thr | jnp.left_shift(jnp.int32(1), 30 - i)
        cnt = jnp.sum((bits >= cand).astype(jnp.int32), axis=-1, keepdims=True)
        return jnp.where(cnt >= cap, cand, thr)

    thr = lax.fori_loop(0, 31, bit_step, jnp.zeros((n_exp, 1), jnp.int32))
    gt = bits > thr
    eq = bits == thr
    need = cap - jnp.sum(gt.astype(jnp.int32), axis=-1, keepdims=True)

    tri = (lax.broadcasted_iota(jnp.int32, (blk, blk), 0)
           <= lax.broadcasted_iota(jnp.int32, (blk, blk), 1)).astype(BF16)
    blk_of_tok = lax.broadcasted_iota(jnp.int32, (seq, n_blk), 0) // blk
    tok_to_blk = (blk_of_tok == lax.broadcasted_iota(jnp.int32, (seq, n_blk), 1)).astype(BF16)
    blk_before = (lax.broadcasted_iota(jnp.int32, (n_blk, n_blk), 0)
                  < lax.broadcasted_iota(jnp.int32, (n_blk, n_blk), 1)).astype(BF16)
    erow = lax.broadcasted_iota(jnp.int32, (2 * n_blk, seq), 0)
    ecol = lax.broadcasted_iota(jnp.int32, (2 * n_blk, seq), 1) // blk
    expand = jnp.where(erow == ecol, 32.0, jnp.where(erow - n_blk == ecol, 1.0, 0.0)).astype(BF16)

    def prefix_counts(mask):
        mb = jnp.where(mask, 1.0, 0.0).astype(BF16)
        local = jnp.concatenate(
            [jnp.dot(mb[:, b * blk:(b + 1) * blk], tri, preferred_element_type=F32) for b in range(n_blk)],
            axis=1)
        per_blk = jnp.dot(mb, tok_to_blk, preferred_element_type=F32)
        start = jnp.dot(per_blk.astype(BF16), blk_before, preferred_element_type=F32)
        hi = jnp.floor(start * (1.0 / 32.0))
        parts = jnp.concatenate([hi, start - 32.0 * hi], axis=1).astype(BF16)
        start_tok = jnp.dot(parts, expand, preferred_element_type=F32)
        return local + start_tok, start, start_tok, per_blk

    eq_incl, _, _, _ = prefix_counts(eq)
    sel = gt | (eq & (eq_incl - 1.0 < need.astype(F32)))
    incl, start, start_tok, per_blk = prefix_counts(sel)

    def window(s):
        return jnp.floor(s * (1.0 / MOE_WIN_ALIGN)) * MOE_WIN_ALIGN

    posw_ref[...] = jnp.where(sel, (incl - 1.0 - window(start_tok)).astype(jnp.int32), -1)
    gate_ref[...] = jnp.where(sel, a, 0.0)
    ws_ref[...] = window(start).astype(jnp.int32)
    span = start - window(start) + per_blk
    rounds = jnp.floor((span + float(win - 1)) * (1.0 / win))
    nr_ref[...] = jnp.max(rounds, axis=0, keepdims=True).astype(jnp.int32)


def _topk(aff_t, cap, blk, win):
    n_exp, seq = aff_t.shape
    n_blk = seq // blk
    full = lambda *_: (0, 0)
    return pl.pallas_call(
        functools.partial(_topk_kernel, cap=cap, blk=blk, win=win),
        grid=(1,),
        in_specs=[pl.BlockSpec((n_exp, seq), full)],
        out_specs=[pl.BlockSpec((n_exp, seq), full), pl.BlockSpec((n_exp, seq), full),
                   pl.BlockSpec((n_exp, n_blk), full), pl.BlockSpec((1, n_blk), full)],
        out_shape=[
            jax.ShapeDtypeStruct((n_exp, seq), jnp.int32),
            jax.ShapeDtypeStruct((n_exp, seq), F32),
            jax.ShapeDtypeStruct((n_exp, n_blk), jnp.int32),
            jax.ShapeDtypeStruct((1, n_blk), jnp.int32),
        ],
        compiler_params=_params("arbitrary"),
        name="topk",
    )(aff_t)


def _window(ws_ref, e, b, r, n_blk, win, cap):
    ws = ws_ref[e * n_blk + b] + r * win
    start = jnp.minimum(ws, cap - win)
    return pl.multiple_of(start, MOE_WIN_ALIGN), ws - start


def _gather_kernel(ws_ref, nr_ref, h_ref, rel_ref, xe_ref, *, blk, win, n_blk):
    n_exp, cap, _ = xe_ref.shape
    xe_ref[...] = jnp.zeros_like(xe_ref)
    slot = lax.broadcasted_iota(jnp.int32, (win, blk), 0)

    def block(b, _):
        rows = h_ref[pl.ds(pl.multiple_of(b * blk, blk), blk), :]
        rel = rel_ref[b]

        def one_round(r, _):
            starts, hots = [], []
            for e in range(n_exp):
                start, shift = _window(ws_ref, e, b, r, n_blk, win, cap)
                relr = rel[e:e + 1, :] - r * win
                key = jnp.where(relr >= 0, relr + shift, -1)
                hots.append(jnp.where(slot == key, 1.0, 0.0).astype(BF16))
                starts.append(start)
            res = jnp.dot(jnp.concatenate(hots, axis=0), rows, preferred_element_type=F32)
            for e in range(n_exp):
                dst = pl.ds(starts[e], win)
                xe_ref[e, dst, :] = (xe_ref[e, dst, :].astype(F32) + res[e * win:(e + 1) * win]).astype(BF16)
            return 0

        lax.fori_loop(0, nr_ref[b], one_round, 0)
        return 0

    lax.fori_loop(0, n_blk, block, 0)


def _moe_gather(ws_flat, n_rounds, h2, rel3, cap, win):
    seq, d_model = h2.shape
    n_blk, n_exp, blk = rel3.shape
    dq = d_model // 4
    grid_spec = pltpu.PrefetchScalarGridSpec(
        num_scalar_prefetch=2,
        grid=(4,),
        in_specs=[
            pl.BlockSpec((seq, dq), lambda c, ws, nr: (0, c)),
            pl.BlockSpec((n_blk, n_exp, blk), lambda c, ws, nr: (0, 0, 0)),
        ],
        out_specs=pl.BlockSpec((n_exp, cap, dq), lambda c, ws, nr: (0, 0, c)),
    )
    return pl.pallas_call(
        functools.partial(_gather_kernel, blk=blk, win=win, n_blk=n_blk),
        grid_spec=grid_spec,
        out_shape=jax.ShapeDtypeStruct((n_exp, cap, d_model), BF16),
        compiler_params=_params("arbitrary"),
        name="moe_gather",
    )(ws_flat, n_rounds, h2, rel3)


def _ffn_kernel(x_ref, wg_ref, wu_ref, wd_ref, y_ref, act_ref, *, n_f):
    s = pl.program_id(1)
    tf = wg_ref.shape[-1]

    @pl.when(s < n_f)
    def _():
        x = x_ref[0]
        g = jnp.dot(x, wg_ref[0].astype(BF16), preferred_element_type=F32)
        u = jnp.dot(x, wu_ref[0].astype(BF16), preferred_element_type=F32)
        act_ref[s] = (g * (1.0 / (1.0 + jnp.exp(-g))) * u).astype(BF16)

    @pl.when(s >= n_f)
    def _():
        acc = jnp.dot(act_ref[0], wd_ref[0, 0:tf, :].astype(BF16), preferred_element_type=F32)
        for f in range(1, n_f):
            acc = acc + jnp.dot(act_ref[f], wd_ref[0, f * tf:(f + 1) * tf, :].astype(BF16),
                                preferred_element_type=F32)
        y_ref[0] = acc.astype(BF16)


def _moe_ffn(xe, w_gate, w_up, w_down):
    n_exp, cap, d_model = xe.shape
    d_ff = w_gate.shape[-1]
    tf = 256
    tn = 256
    n_f, n_n = d_ff // tf, d_model // tn
    up_tile = lambda e, s: (e, 0, jnp.minimum(s, n_f - 1))
    down_tile = lambda e, s: (e, 0, jnp.maximum(s - n_f, 0))
    return pl.pallas_call(
        functools.partial(_ffn_kernel, n_f=n_f),
        grid=(n_exp, n_f + n_n),
        in_specs=[
            pl.BlockSpec((1, cap, d_model), lambda e, s: (e, 0, 0)),
            pl.BlockSpec((1, d_model, tf), up_tile),
            pl.BlockSpec((1, d_model, tf), up_tile),
            pl.BlockSpec((1, d_ff, tn), down_tile),
        ],
        out_specs=pl.BlockSpec((1, cap, tn), down_tile),
        out_shape=jax.ShapeDtypeStruct((n_exp, cap, d_model), BF16),
        scratch_shapes=[pltpu.VMEM((n_f, cap, tf), BF16)],
        compiler_params=_params("arbitrary", "arbitrary"),
        name="moe_ffn",
    )(xe, w_gate, w_up, w_down)


def _combine_kernel(ws_ref, nr_ref, ye_ref, x_ref, rel_ref, gate_ref, o_ref, *, blk, win, n_blk):
    n_exp, cap, _ = ye_ref.shape
    sub = x_ref.shape[0] // blk
    slot = lax.broadcasted_iota(jnp.int32, (win, blk), 0)

    def block(s, _):
        b = pl.program_id(1) * sub + s
        tok = pl.ds(pl.multiple_of(s * blk, blk), blk)
        rel = rel_ref[b]
        gate = gate_ref[b]

        def one_round(r, acc):
            his, los, wins = [], [], []
            for e in range(n_exp):
                start, shift = _window(ws_ref, e, b, r, n_blk, win, cap)
                relr = rel[e:e + 1, :] - r * win
                key = jnp.where(relr >= 0, relr + shift, -1)
                g = jnp.where(slot == key, gate[e:e + 1, :], 0.0)
                g_hi = g.astype(BF16)
                his.append(g_hi)
                los.append((g - g_hi.astype(F32)).astype(BF16))
                wins.append(ye_ref[e, pl.ds(start, win), :])
            lhs_t = jnp.concatenate([jnp.concatenate(his, axis=0), jnp.concatenate(los, axis=0)], axis=1)
            res = lax.dot_general(lhs_t, jnp.concatenate(wins, axis=0), (((0,), (0,)), ((), ())),
                                  preferred_element_type=F32)
            return acc + res[:blk] + res[blk:]

        o_ref[tok, :] = lax.fori_loop(0, nr_ref[b], one_round, x_ref[tok, :])
        return 0

    lax.fori_loop(0, sub, block, 0)


def _moe_combine(ws_flat, n_rounds, ye, x1, rel3, gate3, win):
    seq, d_model = x1.shape
    n_exp, cap, _ = ye.shape
    n_blk, _, blk = rel3.shape
    dq = d_model // 4
    tile = min(8, n_blk) * blk
    whole = lambda c, t, ws, nr: (0, 0, 0)
    grid_spec = pltpu.PrefetchScalarGridSpec(
        num_scalar_prefetch=2,
        grid=(4, seq // tile),
        in_specs=[
            pl.BlockSpec((n_exp, cap, dq), lambda c, t, ws, nr: (0, 0, c)),
            pl.BlockSpec((tile, dq), lambda c, t, ws, nr: (t, c)),
            pl.BlockSpec(rel3.shape, whole),
            pl.BlockSpec(gate3.shape, whole),
        ],
        out_specs=pl.BlockSpec((tile, dq), lambda c, t, ws, nr: (t, c)),
    )
    return pl.pallas_call(
        functools.partial(_combine_kernel, blk=blk, win=win, n_blk=n_blk),
        grid_spec=grid_spec,
        out_shape=jax.ShapeDtypeStruct((seq, d_model), F32),
        compiler_params=_params("arbitrary", "arbitrary"),
        name="moe_combine",
    )(ws_flat, n_rounds, ye, x1, rel3, gate3)


def _final_norm_kernel(x_ref, g_ref, o_ref):
    o_ref[...] = _rms(x_ref[...], g_ref[...])


def _final_norm(x, g):
    seq, d_model = x.shape
    tm = min(512, seq)
    return pl.pallas_call(
        _final_norm_kernel,
        grid=(seq // tm,),
        in_specs=[pl.BlockSpec((tm, d_model), lambda i: (i, 0)), pl.BlockSpec((1, d_model), lambda i: (0, 0))],
        out_specs=pl.BlockSpec((tm, d_model), lambda i: (i, 0)),
        out_shape=jax.ShapeDtypeStruct((seq, d_model), F32),
        compiler_params=_params("arbitrary"),
        name="final_norm",
    )(x, g.reshape(1, d_model))


def _layer(x, norm_mix_g, w_in, fwd, bwd, ssm_d, w_glu, b_glu, na_rpb, g_ssm_out, g_na_out, w_out,
           norm_ffn_g, w_router, w_gate, w_up, w_down):
    seq, d_model = x.shape
    d_ssm = ssm_d.shape[0]
    d_na = g_na_out.shape[0]
    n_exp = w_router.shape[1]
    cap = EC_CAPACITY_FACTOR * seq // n_exp
    blk = min(MOE_TOK_BLOCK, cap // 2)
    win = min(MOE_WIN, cap)

    u, qkv = _in_proj(x, norm_mix_g, w_in, d_ssm, d_na)
    y_ssm = _s5_post(_s5_scan(u, fwd, bwd), u, ssm_d, w_glu, b_glu, g_ssm_out)
    y_na = _neighbourhood_attention(qkv, na_rpb, g_na_out)
    x1 = _out_proj(y_ssm, y_na, w_out, x)

    h2, aff_t = _router(x1, norm_ffn_g, w_router)
    rel, gate, ws, n_rounds = _topk(aff_t, cap, blk, win)
    ws_flat = ws.reshape(-1)
    n_rounds = n_rounds.reshape(-1)
    rel3 = jnp.swapaxes(rel.reshape(n_exp, seq // blk, blk), 0, 1)
    gate3 = jnp.swapaxes(gate.reshape(n_exp, seq // blk, blk), 0, 1)
    xe = _moe_gather(ws_flat, n_rounds, h2, rel3, cap, win)
    ye = _moe_ffn(xe, w_gate, w_up, w_down)
    return _moe_combine(ws_flat, n_rounds, ye, x1, rel3, gate3, win)


def kernel(x, norm_mix_g, w_in, a_re_fwd, a_im_fwd, log_dt_fwd, b_re_fwd, b_im_fwd, c_re_fwd, c_im_fwd, a_re_bwd, a_im_bwd, log_dt_bwd, b_re_bwd, b_im_bwd, c_re_bwd, c_im_bwd, ssm_d, w_glu, b_glu, na_rpb, g_ssm_out, g_na_out, w_out, norm_ffn_g, w_router, w_gate, w_up, w_down, norm_final_g):
    bsz = x.shape[0]
    depth = w_in.shape[0]
    outs = []
    for b in range(bsz):
        xb = x[b]
        for l in range(depth):
            fwd = (a_re_fwd[l], a_im_fwd[l], log_dt_fwd[l], b_re_fwd[l], b_im_fwd[l], c_re_fwd[l], c_im_fwd[l])
            bwd = (a_re_bwd[l], a_im_bwd[l], log_dt_bwd[l], b_re_bwd[l], b_im_bwd[l], c_re_bwd[l], c_im_bwd[l])
            xb = _layer(xb, norm_mix_g[l], w_in[l], fwd, bwd, ssm_d[l], w_glu[l], b_glu[l], na_rpb[l],
                        g_ssm_out[l], g_na_out[l], w_out[l], norm_ffn_g[l], w_router[l],
                        w_gate[l], w_up[l], w_down[l])
        outs.append(_final_norm(xb, norm_final_g))
    return jnp.stack(outs)
```

```python
import functools

import numpy as np
import jax
import jax.numpy as jnp
from jax import lax
from jax.experimental import pallas as pl
from jax.experimental.pallas import tpu as pltpu

F32 = jnp.float32
BF16 = jnp.bfloat16

RMS_EPS = 1e-6
SSM_GROUP = 16
SSM_STATE = 64
NA_HEADS = 16
NA_HEAD_DIM = 64
GRID_W = 64
WIN_ROWS = 8
WIN_COLS = 16
N_EXPERTS = 16
EC_CAPACITY_FACTOR = 2

S5_CHUNK = 16
S5_SEGS = 8
HEADS_PER_DOT = 4
MOE_TOK_BLOCK = 128
MOE_WIN_ALIGN = 16
MOE_WIN = 64
MASK_NEG = -1e30

VMEM_LIMIT_BYTES = 56 * 1024 * 1024


def _params(*semantics):
    return pltpu.CompilerParams(dimension_semantics=semantics, vmem_limit_bytes=VMEM_LIMIT_BYTES)


def _rms(x, g):
    ms = jnp.mean(x * x, axis=-1, keepdims=True)
    return x * lax.rsqrt(ms + RMS_EPS) * g


def _inproj_kernel(x_ref, g_ref, w_ref, u_ref, qkv_ref, h_scr, *, n_u, n_q, q_scale):
    j = pl.program_id(1)

    @pl.when(j == 0)
    def _():
        h_scr[...] = _rms(x_ref[...], g_ref[...]).astype(BF16)

    def project():
        return jnp.dot(h_scr[...], w_ref[...].astype(BF16), preferred_element_type=F32)

    @pl.when(j < n_u)
    def _():
        u_ref[...] = project()

    @pl.when(j >= n_u)
    def _():
        scale = jnp.where(j < n_u + n_q, q_scale, 1.0).astype(F32)
        qkv_ref[...] = (project() * scale).astype(BF16)


def _in_proj(x, g, w_in, d_ssm, d_na):
    seq, d_model = x.shape
    tm = min(1024, seq)
    tn = 512
    n_u, n_q = d_ssm // tn, d_na // tn
    n_cols = w_in.shape[1] // tn
    kern = functools.partial(_inproj_kernel, n_u=n_u, n_q=n_q, q_scale=NA_HEAD_DIM ** -0.5)
    return pl.pallas_call(
        kern,
        grid=(seq // tm, n_cols),
        in_specs=[
            pl.BlockSpec((tm, d_model), lambda i, j: (i, 0)),
            pl.BlockSpec((1, d_model), lambda i, j: (0, 0)),
            pl.BlockSpec((d_model, tn), lambda i, j: (0, j)),
        ],
        out_specs=[
            pl.BlockSpec((tm, tn), lambda i, j: (i, jnp.minimum(j, n_u - 1))),
            pl.BlockSpec((tm, tn), lambda i, j: (i, jnp.maximum(j - n_u, 0))),
        ],
        out_shape=[
            jax.ShapeDtypeStruct((seq, d_ssm), F32),
            jax.ShapeDtypeStruct((seq, 3 * d_na), BF16),
        ],
        scratch_shapes=[pltpu.VMEM((tm, d_model), BF16)],
        compiler_params=_params("arbitrary", "arbitrary"),
        name="in_proj",
    )(x, g.reshape(1, d_model), w_in)


def _s5_direction_tables(a_re, a_im, log_dt, b_re, b_im, c_re, c_im, n_tiles, reverse):
    t_len = S5_CHUNK
    n_ch = b_re.shape[-1]
    hp = lax.Precision.HIGHEST
    dt = jnp.exp(log_dt.astype(F32))[:, None]
    xr, xi = a_re.astype(F32) * dt, a_im.astype(F32) * dt

    def power(xr_, xi_, k):
        k = k.reshape((-1,) + (1,) * xr_.ndim)
        mag = jnp.exp(xr_[None] * k)
        return mag * jnp.cos(xi_[None] * k), mag * jnp.sin(xi_[None] * k)

    pr, pi = power(xr, xi, jnp.arange(t_len + 1, dtype=F32))
    nr, ni = pr[1] - 1.0, pi[1]
    den = a_re * a_re + a_im * a_im
    qr, qi = (nr * a_re + ni * a_im) / den, (ni * a_re - nr * a_im) / den
    bb_r, bb_i = _cmul(qr[:, None, :], qi[:, None, :], jnp.swapaxes(b_re, 1, 2), jnp.swapaxes(b_im, 1, 2))
    ct_r, ct_i = jnp.swapaxes(c_re, 1, 2).astype(F32), jnp.swapaxes(c_im, 1, 2).astype(F32)
    apt_r, apt_i = jnp.transpose(pr, (1, 2, 0)), jnp.transpose(pi, (1, 2, 0))

    def c_times_power(n_lanes, lag_of_lane):
        lane = np.arange(n_lanes)
        pick_c = (lane[None, :] % n_ch == np.arange(n_ch)[:, None]).astype(np.float32)
        pick_k = (lag_of_lane[lane // n_ch][None, :] == np.arange(t_len + 1)[:, None]).astype(np.float32)
        cr = jnp.einsum('gpo,oq->gpq', ct_r, pick_c, precision=hp)
        ci = jnp.einsum('gpo,oq->gpq', ct_i, pick_c, precision=hp)
        ar = jnp.einsum('gpk,kq->gpq', apt_r, pick_k, precision=hp)
        ai = jnp.einsum('gpk,kq->gpq', apt_i, pick_k, precision=hp)
        return _cmul(cr, ci, ar, ai)

    slots = np.arange(2 * t_len)
    lag_of_slot = (t_len - 1 - slots) if reverse else (slots - (t_len - 1))
    lag_of_slot = np.where((lag_of_slot >= 0) & (lag_of_slot < t_len), lag_of_slot, -1)
    yr, yi = c_times_power(2 * t_len * n_ch, lag_of_slot)
    strip_lhs = jnp.concatenate([bb_r, -bb_i], axis=-1)
    strip_rhs = jnp.concatenate([yr, yi], axis=1)
    e_in = (np.arange(t_len)) if reverse else (t_len - 1 - np.arange(t_len))
    st_r, st_i = _cmul(jnp.swapaxes(pr[e_in], 0, 1)[:, :, None, :], jnp.swapaxes(pi[e_in], 0, 1)[:, :, None, :],
                       bb_r[:, None], bb_i[:, None])
    e_out = (t_len - np.arange(t_len)) if reverse else (np.arange(t_len) + 1)
    wr, wi = c_times_power(t_len * n_ch, e_out)
    xrf, xif = xr.reshape(-1), xi.reshape(-1)
    tiles = jnp.arange(n_tiles, dtype=F32)
    if reverse:
        tiles = tiles[::-1]
    a_chunk = power(xrf, xif, jnp.full((1,), float(t_len), F32))
    a_seg = power(xrf, xif, jnp.full((1,), float(t_len * n_tiles), F32))
    a_tile = power(xrf, xif, float(t_len) * tiles)
    return (strip_lhs, strip_rhs), (st_r, st_i), (wr, -wi), a_chunk, a_seg, a_tile


def _cmul(ar, ai, br, bi):
    return ar * br - ai * bi, ar * bi + ai * br


def _s5_tables(fwd, bwd, n_tiles):
    tf = _s5_direction_tables(*fwd, n_tiles=n_tiles, reverse=False)
    tb = _s5_direction_tables(*bwd, n_tiles=n_tiles, reverse=True)
    n_groups = tf[1][0].shape[0]

    strip = jnp.einsum('gck,gkq->gcq', jnp.concatenate([tf[0][0], tb[0][0]], axis=-1),
                       jnp.concatenate([tf[0][1], tb[0][1]], axis=1), precision=lax.Precision.HIGHEST)

    odd = (jnp.arange(n_groups) % 2 == 1)[:, None, None, None]
    pieces = []
    zero = jnp.zeros((), BF16)
    for plane in (tf[1][0], tf[1][1], tb[1][0], tb[1][1]):
        plane = plane.astype(BF16)
        pieces += [jnp.where(odd, zero, plane), jnp.where(odd, plane, zero)]
    e_grp = jnp.concatenate(pieces, axis=-1)

    c_grp = jnp.stack([tf[2][0], tf[2][1], tb[2][0], tb[2][1]], axis=1).astype(BF16)

    ap = jnp.concatenate([tf[3][0], tf[3][1], tb[3][0], tb[3][1], tf[4][0], tf[4][1], tb[4][0], tb[4][1]], axis=0)
    pw = jnp.stack([tf[5][0], tf[5][1], tb[5][0], tb[5][1]])
    return strip, e_grp, c_grp, ap, pw


def _s5_a_kernel(u_ref, strip_ref, e_ref, y_ref, s_ref, w_blk, e_blk, ucat, *, n_tiles, seg_stride):
    n_grp = strip_ref.shape[0]
    cw = S5_CHUNK * SSM_GROUP
    pc = e_ref.shape[-1]
    lanes = u_ref.shape[-1]

    @pl.when(pl.program_id(0) == 0)
    def _():
        w_blk[...] = jnp.zeros_like(w_blk)
        e_blk[...] = jnp.zeros_like(e_blk)

    for g in range(n_grp):
        strip = strip_ref[g]
        for t in range(S5_CHUNK):
            r0 = t * lanes + g * SSM_GROUP
            lo = (S5_CHUNK - 1 - t) * SSM_GROUP
            w_blk[r0:r0 + SSM_GROUP, g * cw:(g + 1) * cw] = strip[:, lo:lo + cw].astype(BF16)
            e_blk[r0:r0 + SSM_GROUP, (g // 2) * pc:(g // 2 + 1) * pc] = e_ref[g, t]

    def gather(jj, _):
        for t in range(S5_CHUNK):
            rows = [u_ref[pl.ds(S5_CHUNK * (2 * jj + h) + t, S5_SEGS, stride=seg_stride), :] for h in range(2)]
            dst = pl.ds(pl.multiple_of(jj * 2 * S5_SEGS, 2 * S5_SEGS), 2 * S5_SEGS)
            ucat[dst, t * lanes:(t + 1) * lanes] = jnp.concatenate(rows, axis=0).astype(BF16)
        return 0

    lax.fori_loop(0, n_tiles // 2, gather, 0)
    ub = ucat[...]
    y_ref[...] = jnp.dot(ub, w_blk[...], preferred_element_type=F32)
    s_ref[...] = jnp.dot(ub, e_blk[...], preferred_element_type=F32)


def _s5_bc_kernel(s_ref, yi_ref, c_ref, ap_ref, pw_ref, y_ref, sin, perm, ynat, c_blk, *, n_tiles, seg_stride):
    n_grp, n_plane, n_state, cw = c_ref.shape
    n_pair = n_grp // 2
    pw_ = 2 * n_state
    pc = n_plane * pw_
    lanes = y_ref.shape[-1]

    @pl.when(pl.program_id(0) == 0)
    def _():
        c_blk[...] = jnp.zeros_like(c_blk)
        rl = lax.broadcasted_iota(jnp.int32, (cw, n_grp * cw), 0)
        cl = lax.broadcasted_iota(jnp.int32, (cw, n_grp * cw), 1)
        for g in range(n_grp):
            tgt = (rl // SSM_GROUP) * lanes + g * SSM_GROUP + (rl % SSM_GROUP)
            perm[g * cw:(g + 1) * cw, :] = jnp.where(cl == tgt, 1.0, 0.0).astype(BF16)

    for g in range(n_grp):
        gl = g % 2
        for k in range(n_plane):
            r0 = k * pw_ + gl * n_state
            c_blk[g // 2, r0:r0 + n_state, gl * cw:(gl + 1) * cw] = c_ref[g, k]

    chains = [(q, d) for q in range(n_pair) for d in range(2)]

    def plane_lanes(q, d):
        base = q * pc + d * 2 * pw_
        return slice(base, base + pw_), slice(base + pw_, base + 2 * pw_)

    def coef(row, q):
        return ap_ref[row:row + 1, q * pw_:(q + 1) * pw_]

    def tile_rows(n, d):
        j = n if d == 0 else n_tiles - 1 - n
        return pl.ds(pl.multiple_of(j * S5_SEGS, S5_SEGS), S5_SEGS)

    def step(n, carry):
        out = []
        for (q, d), (zr, zi) in zip(chains, carry):
            re, im = plane_lanes(q, d)
            rows = tile_rows(n, d)
            sin[rows, re] = zr
            sin[rows, im] = zi
            ar, ai = coef(2 * d, q), coef(2 * d + 1, q)
            out.append((ar * zr - ai * zi + s_ref[rows, re], ar * zi + ai * zr + s_ref[rows, im]))
        return tuple(out)

    z0 = jnp.zeros((S5_SEGS, pw_), F32)
    ends = lax.fori_loop(0, n_tiles, step, tuple((z0, z0) for _ in chains))

    carries = []
    for (q, d), (zr, zi) in zip(chains, ends):
        sr, si = coef(4 + 2 * d, q), coef(5 + 2 * d, q)
        cr = jnp.zeros((1, pw_), F32)
        ci = jnp.zeros((1, pw_), F32)
        seg_r = [None] * S5_SEGS
        seg_i = [None] * S5_SEGS
        for s in (range(S5_SEGS) if d == 0 else range(S5_SEGS - 1, -1, -1)):
            seg_r[s], seg_i[s] = cr, ci
            cr, ci = (zr[s:s + 1] + sr * cr - si * ci, zi[s:s + 1] + sr * ci + si * cr)
        carries.append((jnp.concatenate(seg_r, axis=0), jnp.concatenate(seg_i, axis=0)))

    def fix(n8, _):
        tiles = pl.ds(pl.multiple_of(n8 * 8, 8), 8)
        for (q, d), (car_r, car_i) in zip(chains, carries):
            re, im = plane_lanes(q, d)
            pr8 = pw_ref[2 * d, tiles, q * pw_:(q + 1) * pw_]
            pi8 = pw_ref[2 * d + 1, tiles, q * pw_:(q + 1) * pw_]
            for r in range(8):
                rows = pl.ds(pl.multiple_of((n8 * 8 + r) * S5_SEGS, S5_SEGS), S5_SEGS)
                pr, pi = pr8[r:r + 1], pi8[r:r + 1]
                sin[rows, re] = sin[rows, re] + (pr * car_r - pi * car_i)
                sin[rows, im] = sin[rows, im] + (pr * car_i + pi * car_r)
        return 0

    lax.fori_loop(0, n_tiles // 8, fix, 0)

    parts = []
    for q in range(n_pair):
        lhs = sin[:, q * pc:(q + 1) * pc].astype(BF16)
        parts.append(yi_ref[:, q * 2 * cw:(q + 1) * 2 * cw] + jnp.dot(lhs, c_blk[q], preferred_element_type=F32))
    y_chunk = jnp.concatenate(parts, axis=1).astype(BF16)
    ynat[...] = jnp.dot(y_chunk, perm[...], preferred_element_type=F32)

    def scatter(j, _):
        src = pl.ds(pl.multiple_of(j * S5_SEGS, S5_SEGS), S5_SEGS)
        for t in range(S5_CHUNK):
            y_ref[pl.ds(S5_CHUNK * j + t, S5_SEGS, stride=seg_stride), :] = ynat[src, t * lanes:(t + 1) * lanes]
        return 0

    lax.fori_loop(0, n_tiles, scatter, 0)


def _s5_scan(u, fwd, bwd):
    seq, d_ssm = u.shape
    n_groups = d_ssm // SSM_GROUP
    n_state = SSM_STATE
    cw = S5_CHUNK * SSM_GROUP
    n_rows = seq // S5_CHUNK
    n_tiles = n_rows // S5_SEGS
    lanes = 128
    gpb = lanes // SSM_GROUP
    n_blocks = d_ssm // lanes
    bw = gpb * cw
    sw = gpb * 4 * n_state
    seg_stride = n_tiles * S5_CHUNK

    strip, e_grp, c_grp, ap, pw = _s5_tables(fwd, bwd, n_tiles)

    y_intra, s_loc = pl.pallas_call(
        functools.partial(_s5_a_kernel, n_tiles=n_tiles, seg_stride=seg_stride),
        grid=(n_blocks,),
        in_specs=[
            pl.BlockSpec((seq, lanes), lambda i: (0, i)),
            pl.BlockSpec((gpb,) + strip.shape[1:], lambda i: (i, 0, 0)),
            pl.BlockSpec((gpb,) + e_grp.shape[1:], lambda i: (i, 0, 0, 0)),
        ],
        out_specs=[
            pl.BlockSpec((n_rows, bw), lambda i: (0, i)),
            pl.BlockSpec((n_rows, sw), lambda i: (0, i)),
        ],
        out_shape=[
            jax.ShapeDtypeStruct((n_rows, n_blocks * bw), F32),
            jax.ShapeDtypeStruct((n_rows, n_blocks * sw), F32),
        ],
        scratch_shapes=[
            pltpu.VMEM((S5_CHUNK * lanes, bw), BF16),
            pltpu.VMEM((S5_CHUNK * lanes, sw), BF16),
            pltpu.VMEM((n_rows, S5_CHUNK * lanes), BF16),
        ],
        compiler_params=_params("arbitrary"),
        name="s5_a",
    )(u, strip, e_grp)

    pl_lanes = gpb * n_state
    return pl.pallas_call(
        functools.partial(_s5_bc_kernel, n_tiles=n_tiles, seg_stride=seg_stride),
        grid=(n_blocks,),
        in_specs=[
            pl.BlockSpec((n_rows, sw), lambda i: (0, i)),
            pl.BlockSpec((n_rows, bw), lambda i: (0, i)),
            pl.BlockSpec((gpb,) + c_grp.shape[1:], lambda i: (i, 0, 0, 0)),
            pl.BlockSpec((8, pl_lanes), lambda i: (0, i)),
            pl.BlockSpec((4, n_tiles, pl_lanes), lambda i: (0, 0, i)),
        ],
        out_specs=pl.BlockSpec((seq, lanes), lambda i: (0, i)),
        out_shape=jax.ShapeDtypeStruct((seq, d_ssm), F32),
        scratch_shapes=[
            pltpu.VMEM((n_rows, sw), F32),
            pltpu.VMEM((bw, bw), BF16),
            pltpu.VMEM((n_rows, bw), F32),
            pltpu.VMEM((gpb // 2, 8 * n_state, 2 * cw), BF16),
        ],
        compiler_params=_params("arbitrary"),
        name="s5_bc",
    )(s_loc, y_intra, c_grp, ap, pw)


def _s5_post_kernel(y_ref, u_ref, d_ref, w_ref, b_ref, g_ref, o_ref):
    y = y_ref[...] + d_ref[...] * u_ref[...]
    c0 = np.float32(np.sqrt(2.0 / np.pi))
    y = 0.5 * y * (1.0 + jnp.tanh(c0 * (y + np.float32(0.044715) * (y * y * y))))
    z = jnp.dot(y.astype(BF16), w_ref[...].astype(BF16), preferred_element_type=F32) + b_ref[...]
    o = y * (1.0 / (1.0 + jnp.exp(-z)))
    o_ref[...] = _rms(o, g_ref[...]).astype(BF16)


def _s5_post(y, u, d_skip, w_glu, b_glu, g):
    seq, d = y.shape
    tm = min(512, seq)
    row = lambda i: (i, 0)
    fix = lambda i: (0, 0)
    return pl.pallas_call(
        _s5_post_kernel,
        grid=(seq // tm,),
        in_specs=[
            pl.BlockSpec((tm, d), row), pl.BlockSpec((tm, d), row), pl.BlockSpec((1, d), fix),
            pl.BlockSpec((d, d), fix), pl.BlockSpec((1, d), fix), pl.BlockSpec((1, d), fix),
        ],
        out_specs=pl.BlockSpec((tm, d), row),
        out_shape=jax.ShapeDtypeStruct((seq, d), BF16),
        compiler_params=_params("arbitrary"),
        name="s5_post",
    )(y, u, d_skip.reshape(1, d), w_glu, b_glu.reshape(1, d), g.reshape(1, d))


def _na_bias_table(rpb):
    n_heads = rpb.shape[0]
    n_dy = 2 * WIN_ROWS - 1
    cols = np.arange(GRID_W)
    col_start = np.clip(cols - WIN_COLS // 2, 0, GRID_W - WIN_COLS)
    key_cols = np.arange(GRID_W)
    in_win = (key_cols[None, :] >= col_start[:, None]) & (key_cols[None, :] < col_start[:, None] + WIN_COLS)
    dx = key_cols[None, :] - cols[:, None] + (WIN_COLS - 1)
    pick_x = (dx[:, :, None] == np.arange(2 * WIN_COLS - 1)).astype(np.float32)
    b = jnp.einsum('hyx,ckx->hyck', rpb.astype(F32), pick_x, precision=lax.Precision.HIGHEST)
    b = jnp.where(in_win[None, None], b, MASK_NEG)
    b = jnp.pad(b, ((0, 0), (0, 2 * WIN_ROWS + 1 - n_dy), (0, 0), (0, 0)))
    pairs = jnp.stack([b[:, s:s + 2 * WIN_ROWS].reshape(n_heads, WIN_ROWS, 2, GRID_W, GRID_W) for s in (0, 1)])
    pairs = pairs.reshape(2, n_heads // HEADS_PER_DOT, HEADS_PER_DOT, WIN_ROWS, 2, GRID_W, GRID_W)
    pairs = jnp.transpose(pairs, (0, 1, 3, 2, 5, 4, 6))
    return pairs.reshape(2, n_heads // HEADS_PER_DOT, WIN_ROWS, HEADS_PER_DOT * GRID_W, 2 * GRID_W)


def _na_kernel(q_ref, k_ref, v_ref, b_ref, g_ref, o_ref, *, rows):
    n_keys = WIN_ROWS * GRID_W
    k = k_ref[...].reshape(n_keys, k_ref.shape[-1])
    v = v_ref[...].reshape(n_keys, v_ref.shape[-1])
    r = pl.program_id(0)
    dy0 = jnp.clip(r - WIN_ROWS // 2, 0, rows - WIN_ROWS) - r + (WIN_ROWS - 1)
    par, pair0 = dy0 % 2, dy0 // 2
    pw = HEADS_PER_DOT * NA_HEAD_DIM
    row_head = lax.broadcasted_iota(jnp.int32, (HEADS_PER_DOT * GRID_W, pw), 0) // GRID_W
    col_head = lax.broadcasted_iota(jnp.int32, (HEADS_PER_DOT * GRID_W, pw), 1) // NA_HEAD_DIM
    diag = row_head == col_head
    out_head = lax.broadcasted_iota(jnp.int32, (GRID_W, pw), 1) // NA_HEAD_DIM
    outs = []
    for p in range(k.shape[-1] // pw):
        sl = slice(p * pw, (p + 1) * pw)
        q4 = q_ref[:, sl]
        qbd = jnp.where(diag, jnp.concatenate([q4] * HEADS_PER_DOT, axis=0), jnp.zeros((), BF16))
        s = lax.dot_general(qbd, k[:, sl], (((1,), (1,)), ((), ())), preferred_element_type=F32)
        s = s + jnp.concatenate([b_ref[par, p, pair0 + i] for i in range(WIN_ROWS // 2)], axis=1)
        m = jnp.max(s, axis=-1, keepdims=True)
        e = jnp.exp(s - m)
        l = jnp.sum(e, axis=-1, keepdims=True)
        o = jnp.dot(e.astype(BF16), v[:, sl], preferred_element_type=F32) / l
        acc = jnp.zeros((GRID_W, pw), F32)
        for h in range(HEADS_PER_DOT):
            acc = acc + jnp.where(out_head == h, o[h * GRID_W:(h + 1) * GRID_W], 0.0)
        outs.append(acc)
    y = jnp.concatenate(outs, axis=1)
    o_ref[...] = _rms(y, g_ref[...]).astype(BF16)


def _neighbourhood_attention(qkv, rpb, g):
    seq = qkv.shape[0]
    d_na = qkv.shape[1] // 3
    rows = seq // GRID_W
    bias = _na_bias_table(rpb)
    qkv3 = qkv.reshape(rows, GRID_W, 3 * d_na)

    def win_start(r):
        return jnp.clip(r - WIN_ROWS // 2, 0, rows - WIN_ROWS)

    window = (pl.Element(WIN_ROWS), pl.Element(GRID_W), pl.Element(d_na))

    return pl.pallas_call(
        functools.partial(_na_kernel, rows=rows),
        grid=(rows,),
        in_specs=[
            pl.BlockSpec((GRID_W, d_na), lambda r: (r, 0)),
            pl.BlockSpec(window, lambda r: (win_start(r), 0, d_na)),
            pl.BlockSpec(window, lambda r: (win_start(r), 0, 2 * d_na)),
            pl.BlockSpec(bias.shape, lambda r: (0, 0, 0, 0, 0)),
            pl.BlockSpec((1, d_na), lambda r: (0, 0)),
        ],
        out_specs=pl.BlockSpec((GRID_W, d_na), lambda r: (r, 0)),
        out_shape=jax.ShapeDtypeStruct((seq, d_na), BF16),
        compiler_params=_params("arbitrary"),
        name="na",
    )(qkv, qkv3, qkv3, bias, g.reshape(1, d_na))


def _outproj_kernel(a_ref, b_ref, w_ref, x_ref, o_ref):
    da = a_ref.shape[-1]
    acc = jnp.dot(a_ref[...], w_ref[:da, :].astype(BF16), preferred_element_type=F32)
    acc = acc + jnp.dot(b_ref[...], w_ref[da:, :].astype(BF16), preferred_element_type=F32)
    o_ref[...] = x_ref[...] + acc


def _out_proj(y_ssm, y_na, w_out, x):
    seq, d_model = x.shape
    da, db = y_ssm.shape[1], y_na.shape[1]
    tm = min(2048, seq)
    tn = 512
    return pl.pallas_call(
        _outproj_kernel,
        grid=(seq // tm, d_model // tn),
        in_specs=[
            pl.BlockSpec((tm, da), lambda i, j: (i, 0)),
            pl.BlockSpec((tm, db), lambda i, j: (i, 0)),
            pl.BlockSpec((da + db, tn), lambda i, j: (0, j)),
            pl.BlockSpec((tm, tn), lambda i, j: (i, j)),
        ],
        out_specs=pl.BlockSpec((tm, tn), lambda i, j: (i, j)),
        out_shape=jax.ShapeDtypeStruct((seq, d_model), F32),
        compiler_params=_params("arbitrary", "arbitrary"),
        name="out_proj",
    )(y_ssm, y_na, w_out, x)


def _router_kernel(x_ref, g_ref, wt_ref, h_ref, a_ref):
    h = _rms(x_ref[...], g_ref[...])
    h_ref[...] = h.astype(BF16)
    logits = lax.dot_general(wt_ref[...], h, (((1,), (1,)), ((), ())),
                             precision=lax.Precision.HIGHEST, preferred_element_type=F32)
    m = jnp.max(logits, axis=0, keepdims=True)
    e = jnp.exp(logits - m)
    a_ref[...] = e / jnp.sum(e, axis=0, keepdims=True)


def _router(x1, g, w_router):
    seq, d_model = x1.shape
    n_exp = w_router.shape[1]
    tm = min(512, seq)
    return pl.pallas_call(
        _router_kernel,
        grid=(seq // tm,),
        in_specs=[
            pl.BlockSpec((tm, d_model), lambda i: (i, 0)),
            pl.BlockSpec((1, d_model), lambda i: (0, 0)),
            pl.BlockSpec((n_exp, d_model), lambda i: (0, 0)),
        ],
        out_specs=[
            pl.BlockSpec((tm, d_model), lambda i: (i, 0)),
            pl.BlockSpec((n_exp, tm), lambda i: (0, i)),
        ],
        out_shape=[
            jax.ShapeDtypeStruct((seq, d_model), BF16),
            jax.ShapeDtypeStruct((n_exp, seq), F32),
        ],
        compiler_params=_params("arbitrary"),
        name="router",
    )(x1, g.reshape(1, d_model), w_router.T)


def _topk_kernel(a_ref, posw_ref, gate_ref, ws_ref, nr_ref, *, cap, blk, win):
    a = a_ref[...]
    n_exp, seq = a.shape
    n_blk = seq // blk
    bits = pltpu.bitcast(a, jnp.int32)

    def bit_step(i, thr):
        cand = thr | jnp.left_shift(jnp.int32(1), 30 - i)
        cnt = jnp.sum((bits >= cand).astype(jnp.int32), axis=-1, keepdims=True)
        return jnp.where(cnt >= cap, cand, thr)

    thr = lax.fori_loop(0, 31, bit_step, jnp.zeros((n_exp, 1), jnp.int32))
    gt = bits > thr
    eq = bits == thr
    need = cap - jnp.sum(gt.astype(jnp.int32), axis=-1, keepdims=True)

    tri = (lax.broadcasted_iota(jnp.int32, (blk, blk), 0)
           <= lax.broadcasted_iota(jnp.int32, (blk, blk), 1)).astype(BF16)
    blk_of_tok = lax.broadcasted_iota(jnp.int32, (seq, n_blk), 0) // blk
    tok_to_blk = (blk_of_tok == lax.broadcasted_iota(jnp.int32, (seq, n_blk), 1)).astype(BF16)
    blk_before = (lax.broadcasted_iota(jnp.int32, (n_blk, n_blk), 0)
                  < lax.broadcasted_iota(jnp.int32, (n_blk, n_blk), 1)).astype(BF16)
    erow = lax.broadcasted_iota(jnp.int32, (2 * n_blk, seq), 0)
    ecol = lax.broadcasted_iota(jnp.int32, (2 * n_blk, seq), 1) // blk
    expand = jnp.where(erow == ecol, 32.0, jnp.where(erow - n_blk == ecol, 1.0, 0.0)).astype(BF16)

    def prefix_counts(mask):
        mb = jnp.where(mask, 1.0, 0.0).astype(BF16)
        local = jnp.concatenate(
            [jnp.dot(mb[:, b * blk:(b + 1) * blk], tri, preferred_element_type=F32) for b in range(n_blk)],
            axis=1)
        per_blk = jnp.dot(mb, tok_to_blk, preferred_element_type=F32)
        start = jnp.dot(per_blk.astype(BF16), blk_before, preferred_element_type=F32)
        hi = jnp.floor(start * (1.0 / 32.0))
        parts = jnp.concatenate([hi, start - 32.0 * hi], axis=1).astype(BF16)
        start_tok = jnp.dot(parts, expand, preferred_element_type=F32)
        return local + start_tok, start, start_tok, per_blk

    eq_incl, _, _, _ = prefix_counts(eq)
    sel = gt | (eq & (eq_incl - 1.0 < need.astype(F32)))
    incl, start, start_tok, per_blk = prefix_counts(sel)

    def window(s):
        return jnp.floor(s * (1.0 / MOE_WIN_ALIGN)) * MOE_WIN_ALIGN

    posw_ref[...] = jnp.where(sel, (incl - 1.0 - window(start_tok)).astype(jnp.int32), -1)
    gate_ref[...] = jnp.where(sel, a, 0.0)
    ws_ref[...] = window(start).astype(jnp.int32)
    span = start - window(start) + per_blk
    rounds = jnp.floor((span + float(win - 1)) * (1.0 / win))
    nr_ref[...] = jnp.max(rounds, axis=0, keepdims=True).astype(jnp.int32)


def _topk(aff_t, cap, blk, win):
    n_exp, seq = aff_t.shape
    n_blk = seq // blk
    full = lambda *_: (0, 0)
    return pl.pallas_call(
        functools.partial(_topk_kernel, cap=cap, blk=blk, win=win),
        grid=(1,),
        in_specs=[pl.BlockSpec((n_exp, seq), full)],
        out_specs=[pl.BlockSpec((n_exp, seq), full), pl.BlockSpec((n_exp, seq), full),
                   pl.BlockSpec((n_exp, n_blk), full), pl.BlockSpec((1, n_blk), full)],
        out_shape=[
            jax.ShapeDtypeStruct((n_exp, seq), jnp.int32),
            jax.ShapeDtypeStruct((n_exp, seq), F32),
            jax.ShapeDtypeStruct((n_exp, n_blk), jnp.int32),
            jax.ShapeDtypeStruct((1, n_blk), jnp.int32),
        ],
        compiler_params=_params("arbitrary"),
        name="topk",
    )(aff_t)


def _window(ws_ref, e, b, r, n_blk, win, cap):
    ws = ws_ref[e * n_blk + b] + r * win
    start = jnp.minimum(ws, cap - win)
    return pl.multiple_of(start, MOE_WIN_ALIGN), ws - start


def _gather_kernel(ws_ref, nr_ref, h_ref, rel_ref, xe_ref, *, blk, win, n_blk):
    n_exp, cap, _ = xe_ref.shape
    xe_ref[...] = jnp.zeros_like(xe_ref)
    slot = lax.broadcasted_iota(jnp.int32, (win, blk), 0)

    def block(b, _):
        rows = h_ref[pl.ds(pl.multiple_of(b * blk, blk), blk), :]
        rel = rel_ref[b]

        def one_round(r, _):
            starts, hots = [], []
            for e in range(n_exp):
                start, shift = _window(ws_ref, e, b, r, n_blk, win, cap)
                relr = rel[e:e + 1, :] - r * win
                key = jnp.where(relr >= 0, relr + shift, -1)
                hots.append(jnp.where(slot == key, 1.0, 0.0).astype(BF16))
                starts.append(start)
            res = jnp.dot(jnp.concatenate(hots, axis=0), rows, preferred_element_type=F32)
            for e in range(n_exp):
                dst = pl.ds(starts[e], win)
                xe_ref[e, dst, :] = (xe_ref[e, dst, :].astype(F32) + res[e * win:(e + 1) * win]).astype(BF16)
            return 0

        lax.fori_loop(0, nr_ref[b], one_round, 0)
        return 0

    lax.fori_loop(0, n_blk, block, 0)


def _moe_gather(ws_flat, n_rounds, h2, rel3, cap, win):
    seq, d_model = h2.shape
    n_blk, n_exp, blk = rel3.shape
    dq = d_model // 4
    grid_spec = pltpu.PrefetchScalarGridSpec(
        num_scalar_prefetch=2,
        grid=(4,),
        in_specs=[
            pl.BlockSpec((seq, dq), lambda c, ws, nr: (0, c)),
            pl.BlockSpec((n_blk, n_exp, blk), lambda c, ws, nr: (0, 0, 0)),
        ],
        out_specs=pl.BlockSpec((n_exp, cap, dq), lambda c, ws, nr: (0, 0, c)),
    )
    return pl.pallas_call(
        functools.partial(_gather_kernel, blk=blk, win=win, n_blk=n_blk),
        grid_spec=grid_spec,
        out_shape=jax.ShapeDtypeStruct((n_exp, cap, d_model), BF16),
        compiler_params=_params("arbitrary"),
        name="moe_gather",
    )(ws_flat, n_rounds, h2, rel3)


def _ffn_kernel(x_ref, wg_ref, wu_ref, wd_ref, y_ref, act_ref, *, n_f):
    s = pl.program_id(1)
    tf = wg_ref.shape[-1]

    @pl.when(s < n_f)
    def _():
        x = x_ref[0]
        g = jnp.dot(x, wg_ref[0].astype(BF16), preferred_element_type=F32)
        u = jnp.dot(x, wu_ref[0].astype(BF16), preferred_element_type=F32)
        act_ref[s] = (g * (1.0 / (1.0 + jnp.exp(-g))) * u).astype(BF16)

    @pl.when(s >= n_f)
    def _():
        acc = jnp.dot(act_ref[0], wd_ref[0, 0:tf, :].astype(BF16), preferred_element_type=F32)
        for f in range(1, n_f):
            acc = acc + jnp.dot(act_ref[f], wd_ref[0, f * tf:(f + 1) * tf, :].astype(BF16),
                                preferred_element_type=F32)
        y_ref[0] = acc.astype(BF16)


def _moe_ffn(xe, w_gate, w_up, w_down):
    n_exp, cap, d_model = xe.shape
    d_ff = w_gate.shape[-1]
    tf = 512
    tn = 512
    n_f, n_n = d_ff // tf, d_model // tn
    up_tile = lambda e, s: (e, 0, jnp.minimum(s, n_f - 1))
    down_tile = lambda e, s: (e, 0, jnp.maximum(s - n_f, 0))
    return pl.pallas_call(
        functools.partial(_ffn_kernel, n_f=n_f),
        grid=(n_exp, n_f + n_n),
        in_specs=[
            pl.BlockSpec((1, cap, d_model), lambda e, s: (e, 0, 0)),
            pl.BlockSpec((1, d_model, tf), up_tile),
            pl.BlockSpec((1, d_model, tf), up_tile),
            pl.BlockSpec((1, d_ff, tn), down_tile),
        ],
        out_specs=pl.BlockSpec((1, cap, tn), down_tile),
        out_shape=jax.ShapeDtypeStruct((n_exp, cap, d_model), BF16),
        scratch_shapes=[pltpu.VMEM((n_f, cap, tf), BF16)],
        compiler_params=_params("arbitrary", "arbitrary"),
        name="moe_ffn",
    )(xe, w_gate, w_up, w_down)


def _combine_kernel(ws_ref, nr_ref, ye_ref, x_ref, rel_ref, gate_ref, o_ref, *, blk, win, n_blk):
    n_exp, cap, _ = ye_ref.shape
    sub = x_ref.shape[0] // blk
    slot = lax.broadcasted_iota(jnp.int32, (win, blk), 0)

    def block(s, _):
        b = pl.program_id(1) * sub + s
        tok = pl.ds(pl.multiple_of(s * blk, blk), blk)
        rel = rel_ref[b]
        gate = gate_ref[b]

        def one_round(r, acc):
            his, los, wins = [], [], []
            for e in range(n_exp):
                start, shift = _window(ws_ref, e, b, r, n_blk, win, cap)
                relr = rel[e:e + 1, :] - r * win
                key = jnp.where(relr >= 0, relr + shift, -1)
                g = jnp.where(slot == key, gate[e:e + 1, :], 0.0)
                g_hi = g.astype(BF16)
                his.append(g_hi)
                los.append((g - g_hi.astype(F32)).astype(BF16))
                wins.append(ye_ref[e, pl.ds(start, win), :])
            lhs_t = jnp.concatenate([jnp.concatenate(his, axis=0), jnp.concatenate(los, axis=0)], axis=1)
            res = lax.dot_general(lhs_t, jnp.concatenate(wins, axis=0), (((0,), (0,)), ((), ())),
                                  preferred_element_type=F32)
            return acc + res[:blk] + res[blk:]

        o_ref[tok, :] = lax.fori_loop(0, nr_ref[b], one_round, x_ref[tok, :])
        return 0

    lax.fori_loop(0, sub, block, 0)


def _moe_combine(ws_flat, n_rounds, ye, x1, rel3, gate3, win):
    seq, d_model = x1.shape
    n_exp, cap, _ = ye.shape
    n_blk, _, blk = rel3.shape
    dq = d_model // 4
    tile = min(8, n_blk) * blk
    whole = lambda c, t, ws, nr: (0, 0, 0)
    grid_spec = pltpu.PrefetchScalarGridSpec(
        num_scalar_prefetch=2,
        grid=(4, seq // tile),
        in_specs=[
            pl.BlockSpec((n_exp, cap, dq), lambda c, t, ws, nr: (0, 0, c)),
            pl.BlockSpec((tile, dq), lambda c, t, ws, nr: (t, c)),
            pl.BlockSpec(rel3.shape, whole),
            pl.BlockSpec(gate3.shape, whole),
        ],
        out_specs=pl.BlockSpec((tile, dq), lambda c, t, ws, nr: (t, c)),
    )
    return pl.pallas_call(
        functools.partial(_combine_kernel, blk=blk, win=win, n_blk=n_blk),
        grid_spec=grid_spec,
        out_shape=jax.ShapeDtypeStruct((seq, d_model), F32),
        compiler_params=_params("arbitrary", "arbitrary"),
        name="moe_combine",
    )(ws_flat, n_rounds, ye, x1, rel3, gate3)


def _final_norm_kernel(x_ref, g_ref, o_ref):
    o_ref[...] = _rms(x_ref[...], g_ref[...])


def _final_norm(x, g):
    seq, d_model = x.shape
    tm = min(512, seq)
    return pl.pallas_call(
        _final_norm_kernel,
        grid=(seq // tm,),
        in_specs=[pl.BlockSpec((tm, d_model), lambda i: (i, 0)), pl.BlockSpec((1, d_model), lambda i: (0, 0))],
        out_specs=pl.BlockSpec((tm, d_model), lambda i: (i, 0)),
        out_shape=jax.ShapeDtypeStruct((seq, d_model), F32),
        compiler_params=_params("arbitrary"),
        name="final_norm",
    )(x, g.reshape(1, d_model))


def _layer(x, norm_mix_g, w_in, fwd, bwd, ssm_d, w_glu, b_glu, na_rpb, g_ssm_out, g_na_out, w_out,
           norm_ffn_g, w_router, w_gate, w_up, w_down):
    seq, d_model = x.shape
    d_ssm = ssm_d.shape[0]
    d_na = g_na_out.shape[0]
    n_exp = w_router.shape[1]
    cap = EC_CAPACITY_FACTOR * seq // n_exp
    blk = min(MOE_TOK_BLOCK, cap // 2)
    win = min(MOE_WIN, cap)

    u, qkv = _in_proj(x, norm_mix_g, w_in, d_ssm, d_na)
    y_ssm = _s5_post(_s5_scan(u, fwd, bwd), u, ssm_d, w_glu, b_glu, g_ssm_out)
    y_na = _neighbourhood_attention(qkv, na_rpb, g_na_out)
    x1 = _out_proj(y_ssm, y_na, w_out, x)

    h2, aff_t = _router(x1, norm_ffn_g, w_router)
    rel, gate, ws, n_rounds = _topk(aff_t, cap, blk, win)
    ws_flat = ws.reshape(-1)
    n_rounds = n_rounds.reshape(-1)
    rel3 = jnp.swapaxes(rel.reshape(n_exp, seq // blk, blk), 0, 1)
    gate3 = jnp.swapaxes(gate.reshape(n_exp, seq // blk, blk), 0, 1)
    xe = _moe_gather(ws_flat, n_rounds, h2, rel3, cap, win)
    ye = _moe_ffn(xe, w_gate, w_up, w_down)
    return _moe_combine(ws_flat, n_rounds, ye, x1, rel3, gate3, win)


def kernel(x, norm_mix_g, w_in, a_re_fwd, a_im_fwd, log_dt_fwd, b_re_fwd, b_im_fwd, c_re_fwd, c_im_fwd, a_re_bwd, a_im_bwd, log_dt_bwd, b_re_bwd, b_im_bwd, c_re_bwd, c_im_bwd, ssm_d, w_glu, b_glu, na_rpb, g_ssm_out, g_na_out, w_out, norm_ffn_g, w_router, w_gate, w_up, w_down, norm_final_g):
    bsz = x.shape[0]
    depth = w_in.shape[0]
    outs = []
    for b in range(bsz):
        xb = x[b]
        for l in range(depth):
            fwd = (a_re_fwd[l], a_im_fwd[l], log_dt_fwd[l], b_re_fwd[l], b_im_fwd[l], c_re_fwd[l], c_im_fwd[l])
            bwd = (a_re_bwd[l], a_im_bwd[l], log_dt_bwd[l], b_re_bwd[l], b_im_bwd[l], c_re_bwd[l], c_im_bwd[l])
            xb = _layer(xb, norm_mix_g[l], w_in[l], fwd, bwd, ssm_d[l], w_glu[l], b_glu[l], na_rpb[l],
                        g_ssm_out[l], g_na_out[l], w_out[l], norm_ffn_g[l], w_router[l],
                        w_gate[l], w_up[l], w_down[l])
        outs.append(_final_norm(xb, norm_final_g))
    return jnp.stack(outs)
```

```python
import functools

import numpy as np
import jax
import jax.numpy as jnp
from jax import lax
from jax.experimental import pallas as pl
from jax.experimental.pallas import tpu as pltpu

F32 = jnp.float32
BF16 = jnp.bfloat16

RMS_EPS = 1e-6
SSM_GROUP = 16
SSM_STATE = 64
NA_HEADS = 16
NA_HEAD_DIM = 64
GRID_W = 64
WIN_ROWS = 8
WIN_COLS = 16
N_EXPERTS = 16
EC_CAPACITY_FACTOR = 2

S5_CHUNK = 16
S5_SEGS = 8
HEADS_PER_DOT = 4
MOE_TOK_BLOCK = 128
MOE_WIN_ALIGN = 16
MOE_WIN = 64
MASK_NEG = -1e30

VMEM_LIMIT_BYTES = 56 * 1024 * 1024


def _params(*semantics):
    return pltpu.CompilerParams(dimension_semantics=semantics, vmem_limit_bytes=VMEM_LIMIT_BYTES)


def _rms(x, g):
    ms = jnp.mean(x * x, axis=-1, keepdims=True)
    return x * lax.rsqrt(ms + RMS_EPS) * g


def _inproj_kernel(x_ref, g_ref, w_ref, u_ref, qkv_ref, h_scr, *, n_u, n_q, q_scale):
    j = pl.program_id(1)

    @pl.when(j == 0)
    def _():
        h_scr[...] = _rms(x_ref[...], g_ref[...]).astype(BF16)

    def project():
        return jnp.dot(h_scr[...], w_ref[...].astype(BF16), preferred_element_type=F32)

    @pl.when(j < n_u)
    def _():
        u_ref[...] = project()

    @pl.when(j >= n_u)
    def _():
        scale = jnp.where(j < n_u + n_q, q_scale, 1.0).astype(F32)
        qkv_ref[...] = (project() * scale).astype(BF16)


def _in_proj(x, g, w_in, d_ssm, d_na):
    seq, d_model = x.shape
    tm = min(1024, seq)
    tn = 512
    n_u, n_q = d_ssm // tn, d_na // tn
    n_cols = w_in.shape[1] // tn
    kern = functools.partial(_inproj_kernel, n_u=n_u, n_q=n_q, q_scale=NA_HEAD_DIM ** -0.5)
    return pl.pallas_call(
        kern,
        grid=(seq // tm, n_cols),
        in_specs=[
            pl.BlockSpec((tm, d_model), lambda i, j: (i, 0)),
            pl.BlockSpec((1, d_model), lambda i, j: (0, 0)),
            pl.BlockSpec((d_model, tn), lambda i, j: (0, j)),
        ],
        out_specs=[
            pl.BlockSpec((tm, tn), lambda i, j: (i, jnp.minimum(j, n_u - 1))),
            pl.BlockSpec((tm, tn), lambda i, j: (i, jnp.maximum(j - n_u, 0))),
        ],
        out_shape=[
            jax.ShapeDtypeStruct((seq, d_ssm), F32),
            jax.ShapeDtypeStruct((seq, 3 * d_na), BF16),
        ],
        scratch_shapes=[pltpu.VMEM((tm, d_model), BF16)],
        compiler_params=_params("arbitrary", "arbitrary"),
        name="in_proj",
    )(x, g.reshape(1, d_model), w_in)


def _s5_direction_tables(a_re, a_im, log_dt, b_re, b_im, c_re, c_im, n_tiles, reverse):
    t_len = S5_CHUNK
    n_ch = b_re.shape[-1]
    hp = lax.Precision.HIGHEST
    dt = jnp.exp(log_dt.astype(F32))[:, None]
    xr, xi = a_re.astype(F32) * dt, a_im.astype(F32) * dt

    def power(xr_, xi_, k):
        k = k.reshape((-1,) + (1,) * xr_.ndim)
        mag = jnp.exp(xr_[None] * k)
        return mag * jnp.cos(xi_[None] * k), mag * jnp.sin(xi_[None] * k)

    pr, pi = power(xr, xi, jnp.arange(t_len + 1, dtype=F32))
    nr, ni = pr[1] - 1.0, pi[1]
    den = a_re * a_re + a_im * a_im
    qr, qi = (nr * a_re + ni * a_im) / den, (ni * a_re - nr * a_im) / den
    bb_r, bb_i = _cmul(qr[:, None, :], qi[:, None, :], jnp.swapaxes(b_re, 1, 2), jnp.swapaxes(b_im, 1, 2))
    cr, ci = c_re.astype(F32), c_im.astype(F32)

    def steps(e):
        return jnp.swapaxes(pr[e], 0, 1), jnp.swapaxes(pi[e], 0, 1)

    d_r, d_i = _cmul(bb_r[:, :, None, :], bb_i[:, :, None, :], cr[:, None, :, :], ci[:, None, :, :])
    am_r, am_i = steps(np.arange(t_len))
    lag = (jnp.einsum('gxop,gmp->gxom', d_r, am_r, precision=hp)
           - jnp.einsum('gxop,gmp->gxom', d_i, am_i, precision=hp))
    lag = jnp.swapaxes(lag, 2, 3)
    e_in = (np.arange(t_len)) if reverse else (t_len - 1 - np.arange(t_len))
    pe_r, pe_i = steps(e_in)
    st_r, st_i = _cmul(pe_r[:, :, None, :], pe_i[:, :, None, :], bb_r[:, None], bb_i[:, None])
    e_out = (t_len - np.arange(t_len)) if reverse else (np.arange(t_len) + 1)
    po_r, po_i = steps(e_out)
    wr, wi = _cmul(cr[:, None, :, :], ci[:, None, :, :], po_r[:, :, None, :], po_i[:, :, None, :])
    n_grp, n_st = a_re.shape
    wr = jnp.swapaxes(wr.reshape(n_grp, t_len * n_ch, n_st), 1, 2)
    wi = jnp.swapaxes(wi.reshape(n_grp, t_len * n_ch, n_st), 1, 2)
    xrf, xif = xr.reshape(-1), xi.reshape(-1)
    tiles = jnp.arange(n_tiles, dtype=F32)
    if reverse:
        tiles = tiles[::-1]
    a_chunk = power(xrf, xif, jnp.full((1,), float(t_len), F32))
    a_seg = power(xrf, xif, jnp.full((1,), float(t_len * n_tiles), F32))
    a_tile = power(xrf, xif, float(t_len) * tiles)
    return lag, (st_r, st_i), (wr, -wi), a_chunk, a_seg, a_tile


def _cmul(ar, ai, br, bi):
    return ar * br - ai * bi, ar * bi + ai * br


def _s5_tables(fwd, bwd, n_tiles):
    tf = _s5_direction_tables(*fwd, n_tiles=n_tiles, reverse=False)
    tb = _s5_direction_tables(*bwd, n_tiles=n_tiles, reverse=True)
    lag_f, lag_b = tf[0], tb[0][:, :, ::-1, :]
    n_groups, n_ch, t_len, _ = lag_f.shape
    strip = jnp.concatenate([lag_b[:, :, :t_len - 1], lag_b[:, :, t_len - 1:] + lag_f[:, :, :1], lag_f[:, :, 1:],
                             jnp.zeros_like(lag_f[:, :, :1])], axis=2)
    strip = strip.reshape(n_groups, n_ch, 2 * t_len * n_ch)

    e_grp = jnp.stack([tf[1][0], tf[1][1], tb[1][0], tb[1][1]]).astype(BF16)
    c_grp = jnp.stack([tf[2][0], tf[2][1], tb[2][0], tb[2][1]], axis=1).astype(BF16)

    ap = jnp.concatenate([tf[3][0], tf[3][1], tb[3][0], tb[3][1], tf[4][0], tf[4][1], tb[4][0], tb[4][1]], axis=0)
    pw = jnp.stack([tf[5][0], tf[5][1], tb[5][0], tb[5][1]])
    return strip, e_grp, c_grp, ap, pw


def _s5_a_kernel(u_ref, strip_ref, e_ref, y_ref, s_ref, w_blk, e_blk, ucat, *, n_tiles, seg_stride):
    n_grp = strip_ref.shape[0]
    cw = S5_CHUNK * SSM_GROUP
    n_plane, n_state = e_ref.shape[0], e_ref.shape[-1]
    lanes = u_ref.shape[-1]

    @pl.when(pl.program_id(0) == 0)
    def _():
        w_blk[...] = jnp.zeros_like(w_blk)
        e_blk[...] = jnp.zeros_like(e_blk)

    for g in range(n_grp):
        strip = strip_ref[g]
        for t in range(S5_CHUNK):
            r0 = t * lanes + g * SSM_GROUP
            lo = (S5_CHUNK - 1 - t) * SSM_GROUP
            w_blk[r0:r0 + SSM_GROUP, g * cw:(g + 1) * cw] = strip[:, lo:lo + cw].astype(BF16)
            for k in range(n_plane):
                c0 = ((g // 2) * n_plane + k) * 2 * n_state + (g % 2) * n_state
                e_blk[r0:r0 + SSM_GROUP, c0:c0 + n_state] = e_ref[k, g, t]

    def gather(jj, _):
        for t in range(S5_CHUNK):
            rows = [u_ref[pl.ds(S5_CHUNK * (2 * jj + h) + t, S5_SEGS, stride=seg_stride), :] for h in range(2)]
            dst = pl.ds(pl.multiple_of(jj * 2 * S5_SEGS, 2 * S5_SEGS), 2 * S5_SEGS)
            ucat[dst, t * lanes:(t + 1) * lanes] = jnp.concatenate(rows, axis=0).astype(BF16)
        return 0

    lax.fori_loop(0, n_tiles // 2, gather, 0)
    ub = ucat[...]
    y_ref[...] = jnp.dot(ub, w_blk[...], preferred_element_type=F32)
    s_ref[...] = jnp.dot(ub, e_blk[...], preferred_element_type=F32)


def _s5_bc_kernel(s_ref, yi_ref, c_ref, ap_ref, pw_ref, y_ref, sin, perm, ynat, c_blk, *, n_tiles, seg_stride):
    n_grp, n_plane, n_state, cw = c_ref.shape
    n_pair = n_grp // 2
    pw_ = 2 * n_state
    pc = n_plane * pw_
    lanes = y_ref.shape[-1]

    @pl.when(pl.program_id(0) == 0)
    def _():
        c_blk[...] = jnp.zeros_like(c_blk)
        rl = lax.broadcasted_iota(jnp.int32, (cw, n_grp * cw), 0)
        cl = lax.broadcasted_iota(jnp.int32, (cw, n_grp * cw), 1)
        for g in range(n_grp):
            tgt = (rl // SSM_GROUP) * lanes + g * SSM_GROUP + (rl % SSM_GROUP)
            perm[g * cw:(g + 1) * cw, :] = jnp.where(cl == tgt, 1.0, 0.0).astype(BF16)

    for g in range(n_grp):
        gl = g % 2
        for k in range(n_plane):
            r0 = k * pw_ + gl * n_state
            c_blk[g // 2, r0:r0 + n_state, gl * cw:(gl + 1) * cw] = c_ref[g, k]

    chains = [(q, d) for q in range(n_pair) for d in range(2)]

    def plane_lanes(q, d):
        base = q * pc + d * 2 * pw_
        return slice(base, base + pw_), slice(base + pw_, base + 2 * pw_)

    def coef(row, q):
        return ap_ref[row:row + 1, q * pw_:(q + 1) * pw_]

    def tile_rows(n, d):
        j = n if d == 0 else n_tiles - 1 - n
        return pl.ds(pl.multiple_of(j * S5_SEGS, S5_SEGS), S5_SEGS)

    def step(n, carry):
        out = []
        for (q, d), (zr, zi) in zip(chains, carry):
            re, im = plane_lanes(q, d)
            rows = tile_rows(n, d)
            sin[rows, re] = zr
            sin[rows, im] = zi
            ar, ai = coef(2 * d, q), coef(2 * d + 1, q)
            out.append((ar * zr - ai * zi + s_ref[rows, re], ar * zi + ai * zr + s_ref[rows, im]))
        return tuple(out)

    z0 = jnp.zeros((S5_SEGS, pw_), F32)
    ends = lax.fori_loop(0, n_tiles, step, tuple((z0, z0) for _ in chains))

    carries = []
    for (q, d), (zr, zi) in zip(chains, ends):
        sr, si = coef(4 + 2 * d, q), coef(5 + 2 * d, q)
        cr = jnp.zeros((1, pw_), F32)
        ci = jnp.zeros((1, pw_), F32)
        seg_r = [None] * S5_SEGS
        seg_i = [None] * S5_SEGS
        for s in (range(S5_SEGS) if d == 0 else range(S5_SEGS - 1, -1, -1)):
            seg_r[s], seg_i[s] = cr, ci
            cr, ci = (zr[s:s + 1] + sr * cr - si * ci, zi[s:s + 1] + sr * ci + si * cr)
        carries.append((jnp.concatenate(seg_r, axis=0), jnp.concatenate(seg_i, axis=0)))

    def fix(n8, _):
        tiles = pl.ds(pl.multiple_of(n8 * 8, 8), 8)
        for (q, d), (car_r, car_i) in zip(chains, carries):
            re, im = plane_lanes(q, d)
            pr8 = pw_ref[2 * d, tiles, q * pw_:(q + 1) * pw_]
            pi8 = pw_ref[2 * d + 1, tiles, q * pw_:(q + 1) * pw_]
            for r in range(8):
                rows = pl.ds(pl.multiple_of((n8 * 8 + r) * S5_SEGS, S5_SEGS), S5_SEGS)
                pr, pi = pr8[r:r + 1], pi8[r:r + 1]
                sin[rows, re] = sin[rows, re] + (pr * car_r - pi * car_i)
                sin[rows, im] = sin[rows, im] + (pr * car_i + pi * car_r)
        return 0

    lax.fori_loop(0, n_tiles // 8, fix, 0)

    parts = []
    for q in range(n_pair):
        lhs = sin[:, q * pc:(q + 1) * pc].astype(BF16)
        parts.append(yi_ref[:, q * 2 * cw:(q + 1) * 2 * cw] + jnp.dot(lhs, c_blk[q], preferred_element_type=F32))
    y_chunk = jnp.concatenate(parts, axis=1).astype(BF16)
    ynat[...] = jnp.dot(y_chunk, perm[...], preferred_element_type=F32)

    def scatter(j, _):
        src = pl.ds(pl.multiple_of(j * S5_SEGS, S5_SEGS), S5_SEGS)
        for t in range(S5_CHUNK):
            y_ref[pl.ds(S5_CHUNK * j + t, S5_SEGS, stride=seg_stride), :] = ynat[src, t * lanes:(t + 1) * lanes]
        return 0

    lax.fori_loop(0, n_tiles, scatter, 0)


def _s5_scan(u, fwd, bwd):
    seq, d_ssm = u.shape
    n_groups = d_ssm // SSM_GROUP
    n_state = SSM_STATE
    cw = S5_CHUNK * SSM_GROUP
    n_rows = seq // S5_CHUNK
    n_tiles = n_rows // S5_SEGS
    lanes = 128
    gpb = lanes // SSM_GROUP
    n_blocks = d_ssm // lanes
    bw = gpb * cw
    sw = gpb * 4 * n_state
    seg_stride = n_tiles * S5_CHUNK

    strip, e_grp, c_grp, ap, pw = _s5_tables(fwd, bwd, n_tiles)

    y_intra, s_loc = pl.pallas_call(
        functools.partial(_s5_a_kernel, n_tiles=n_tiles, seg_stride=seg_stride),
        grid=(n_blocks,),
        in_specs=[
            pl.BlockSpec((seq, lanes), lambda i: (0, i)),
            pl.BlockSpec((gpb,) + strip.shape[1:], lambda i: (i, 0, 0)),
            pl.BlockSpec((e_grp.shape[0], gpb) + e_grp.shape[2:], lambda i: (0, i, 0, 0, 0)),
        ],
        out_specs=[
            pl.BlockSpec((n_rows, bw), lambda i: (0, i)),
            pl.BlockSpec((n_rows, sw), lambda i: (0, i)),
        ],
        out_shape=[
            jax.ShapeDtypeStruct((n_rows, n_blocks * bw), F32),
            jax.ShapeDtypeStruct((n_rows, n_blocks * sw), F32),
        ],
        scratch_shapes=[
            pltpu.VMEM((S5_CHUNK * lanes, bw), BF16),
            pltpu.VMEM((S5_CHUNK * lanes, sw), BF16),
            pltpu.VMEM((n_rows, S5_CHUNK * lanes), BF16),
        ],
        compiler_params=_params("arbitrary"),
        name="s5_a",
    )(u, strip, e_grp)

    pl_lanes = gpb * n_state
    return pl.pallas_call(
        functools.partial(_s5_bc_kernel, n_tiles=n_tiles, seg_stride=seg_stride),
        grid=(n_blocks,),
        in_specs=[
            pl.BlockSpec((n_rows, sw), lambda i: (0, i)),
            pl.BlockSpec((n_rows, bw), lambda i: (0, i)),
            pl.BlockSpec((gpb,) + c_grp.shape[1:], lambda i: (i, 0, 0, 0)),
            pl.BlockSpec((8, pl_lanes), lambda i: (0, i)),
            pl.BlockSpec((4, n_tiles, pl_lanes), lambda i: (0, 0, i)),
        ],
        out_specs=pl.BlockSpec((seq, lanes), lambda i: (0, i)),
        out_shape=jax.ShapeDtypeStruct((seq, d_ssm), F32),
        scratch_shapes=[
            pltpu.VMEM((n_rows, sw), F32),
            pltpu.VMEM((bw, bw), BF16),
            pltpu.VMEM((n_rows, bw), F32),
            pltpu.VMEM((gpb // 2, 8 * n_state, 2 * cw), BF16),
        ],
        compiler_params=_params("arbitrary"),
        name="s5_bc",
    )(s_loc, y_intra, c_grp, ap, pw)


def _s5_post_kernel(y_ref, u_ref, d_ref, w_ref, b_ref, g_ref, o_ref):
    y = y_ref[...] + d_ref[...] * u_ref[...]
    c0 = np.float32(np.sqrt(2.0 / np.pi))
    y = 0.5 * y * (1.0 + jnp.tanh(c0 * (y + np.float32(0.044715) * (y * y * y))))
    z = jnp.dot(y.astype(BF16), w_ref[...].astype(BF16), preferred_element_type=F32) + b_ref[...]
    o = y * (1.0 / (1.0 + jnp.exp(-z)))
    o_ref[...] = _rms(o, g_ref[...]).astype(BF16)


def _s5_post(y, u, d_skip, w_glu, b_glu, g):
    seq, d = y.shape
    tm = min(512, seq)
    row = lambda i: (i, 0)
    fix = lambda i: (0, 0)
    return pl.pallas_call(
        _s5_post_kernel,
        grid=(seq // tm,),
        in_specs=[
            pl.BlockSpec((tm, d), row), pl.BlockSpec((tm, d), row), pl.BlockSpec((1, d), fix),
            pl.BlockSpec((d, d), fix), pl.BlockSpec((1, d), fix), pl.BlockSpec((1, d), fix),
        ],
        out_specs=pl.BlockSpec((tm, d), row),
        out_shape=jax.ShapeDtypeStruct((seq, d), BF16),
        compiler_params=_params("arbitrary"),
        name="s5_post",
    )(y, u, d_skip.reshape(1, d), w_glu, b_glu.reshape(1, d), g.reshape(1, d))


def _na_bias_table(rpb):
    n_heads = rpb.shape[0]
    n_dy = 2 * WIN_ROWS - 1
    cols = np.arange(GRID_W)
    col_start = np.clip(cols - WIN_COLS // 2, 0, GRID_W - WIN_COLS)
    key_cols = np.arange(GRID_W)
    in_win = (key_cols[None, :] >= col_start[:, None]) & (key_cols[None, :] < col_start[:, None] + WIN_COLS)
    dx = key_cols[None, :] - cols[:, None] + (WIN_COLS - 1)
    pick_x = (dx[:, :, None] == np.arange(2 * WIN_COLS - 1)).astype(np.float32)
    b = jnp.einsum('hyx,ckx->hyck', rpb.astype(F32), pick_x, precision=lax.Precision.HIGHEST)
    b = jnp.where(in_win[None, None], b, MASK_NEG)
    b = jnp.pad(b, ((0, 0), (0, 2 * WIN_ROWS + 1 - n_dy), (0, 0), (0, 0)))
    pairs = jnp.stack([b[:, s:s + 2 * WIN_ROWS].reshape(n_heads, WIN_ROWS, 2, GRID_W, GRID_W) for s in (0, 1)])
    pairs = pairs.reshape(2, n_heads // HEADS_PER_DOT, HEADS_PER_DOT, WIN_ROWS, 2, GRID_W, GRID_W)
    pairs = jnp.transpose(pairs, (0, 1, 3, 2, 5, 4, 6))
    return pairs.reshape(2, n_heads // HEADS_PER_DOT, WIN_ROWS, HEADS_PER_DOT * GRID_W, 2 * GRID_W)


def _na_kernel(q_ref, k_ref, v_ref, b_ref, g_ref, o_ref, *, rows):
    n_keys = WIN_ROWS * GRID_W
    k = k_ref[...].reshape(n_keys, k_ref.shape[-1])
    v = v_ref[...].reshape(n_keys, v_ref.shape[-1])
    r = pl.program_id(0)
    dy0 = jnp.clip(r - WIN_ROWS // 2, 0, rows - WIN_ROWS) - r + (WIN_ROWS - 1)
    par, pair0 = dy0 % 2, dy0 // 2
    pw = HEADS_PER_DOT * NA_HEAD_DIM
    row_head = lax.broadcasted_iota(jnp.int32, (HEADS_PER_DOT * GRID_W, pw), 0) // GRID_W
    col_head = lax.broadcasted_iota(jnp.int32, (HEADS_PER_DOT * GRID_W, pw), 1) // NA_HEAD_DIM
    diag = row_head == col_head
    out_head = lax.broadcasted_iota(jnp.int32, (GRID_W, pw), 1) // NA_HEAD_DIM
    outs = []
    for p in range(k.shape[-1] // pw):
        sl = slice(p * pw, (p + 1) * pw)
        q4 = q_ref[:, sl]
        qbd = jnp.where(diag, jnp.concatenate([q4] * HEADS_PER_DOT, axis=0), jnp.zeros((), BF16))
        s = lax.dot_general(qbd, k[:, sl], (((1,), (1,)), ((), ())), preferred_element_type=F32)
        s = s + jnp.concatenate([b_ref[par, p, pair0 + i] for i in range(WIN_ROWS // 2)], axis=1)
        m = jnp.max(s, axis=-1, keepdims=True)
        e = jnp.exp(s - m)
        l = jnp.sum(e, axis=-1, keepdims=True)
        o = jnp.dot(e.astype(BF16), v[:, sl], preferred_element_type=F32) / l
        acc = jnp.zeros((GRID_W, pw), F32)
        for h in range(HEADS_PER_DOT):
            acc = acc + jnp.where(out_head == h, o[h * GRID_W:(h + 1) * GRID_W], 0.0)
        outs.append(acc)
    y = jnp.concatenate(outs, axis=1)
    o_ref[...] = _rms(y, g_ref[...]).astype(BF16)


def _neighbourhood_attention(qkv, rpb, g):
    seq = qkv.shape[0]
    d_na = qkv.shape[1] // 3
    rows = seq // GRID_W
    bias = _na_bias_table(rpb)
    qkv3 = qkv.reshape(rows, GRID_W, 3 * d_na)

    def win_start(r):
        return jnp.clip(r - WIN_ROWS // 2, 0, rows - WIN_ROWS)

    window = (pl.Element(WIN_ROWS), pl.Element(GRID_W), pl.Element(d_na))

    return pl.pallas_call(
        functools.partial(_na_kernel, rows=rows),
        grid=(rows,),
        in_specs=[
            pl.BlockSpec((GRID_W, d_na), lambda r: (r, 0)),
            pl.BlockSpec(window, lambda r: (win_start(r), 0, d_na)),
            pl.BlockSpec(window, lambda r: (win_start(r), 0, 2 * d_na)),
            pl.BlockSpec(bias.shape, lambda r: (0, 0, 0, 0, 0)),
            pl.BlockSpec((1, d_na), lambda r: (0, 0)),
        ],
        out_specs=pl.BlockSpec((GRID_W, d_na), lambda r: (r, 0)),
        out_shape=jax.ShapeDtypeStruct((seq, d_na), BF16),
        compiler_params=_params("arbitrary"),
        name="na",
    )(qkv, qkv3, qkv3, bias, g.reshape(1, d_na))


def _outproj_kernel(a_ref, b_ref, w_ref, x_ref, o_ref):
    da = a_ref.shape[-1]
    acc = jnp.dot(a_ref[...], w_ref[:da, :].astype(BF16), preferred_element_type=F32)
    acc = acc + jnp.dot(b_ref[...], w_ref[da:, :].astype(BF16), preferred_element_type=F32)
    o_ref[...] = x_ref[...] + acc


def _out_proj(y_ssm, y_na, w_out, x):
    seq, d_model = x.shape
    da, db = y_ssm.shape[1], y_na.shape[1]
    tm = min(2048, seq)
    tn = 512
    return pl.pallas_call(
        _outproj_kernel,
        grid=(seq // tm, d_model // tn),
        in_specs=[
            pl.BlockSpec((tm, da), lambda i, j: (i, 0)),
            pl.BlockSpec((tm, db), lambda i, j: (i, 0)),
            pl.BlockSpec((da + db, tn), lambda i, j: (0, j)),
            pl.BlockSpec((tm, tn), lambda i, j: (i, j)),
        ],
        out_specs=pl.BlockSpec((tm, tn), lambda i, j: (i, j)),
        out_shape=jax.ShapeDtypeStruct((seq, d_model), F32),
        compiler_params=_params("arbitrary", "arbitrary"),
        name="out_proj",
    )(y_ssm, y_na, w_out, x)


def _router_kernel(x_ref, g_ref, wt_ref, h_ref, a_ref):
    h = _rms(x_ref[...], g_ref[...])
    h_hi = h.astype(BF16)
    h_ref[...] = h_hi
    h_lo = (h - h_hi.astype(F32)).astype(BF16)
    w = wt_ref[...]
    w_hi = w.astype(BF16)
    w_lo = (w - w_hi.astype(F32)).astype(BF16)
    n_exp = w.shape[0]
    nt = (((1,), (1,)), ((), ()))
    both = lax.dot_general(jnp.concatenate([w_hi, w_lo], axis=0), h_hi, nt, preferred_element_type=F32)
    logits = both[:n_exp] + both[n_exp:] + lax.dot_general(w_hi, h_lo, nt, preferred_element_type=F32)
    m = jnp.max(logits, axis=0, keepdims=True)
    e = jnp.exp(logits - m)
    a_ref[...] = e / jnp.sum(e, axis=0, keepdims=True)


def _router(x1, g, w_router):
    seq, d_model = x1.shape
    n_exp = w_router.shape[1]
    tm = min(512, seq)
    return pl.pallas_call(
        _router_kernel,
        grid=(seq // tm,),
        in_specs=[
            pl.BlockSpec((tm, d_model), lambda i: (i, 0)),
            pl.BlockSpec((1, d_model), lambda i: (0, 0)),
            pl.BlockSpec((n_exp, d_model), lambda i: (0, 0)),
        ],
        out_specs=[
            pl.BlockSpec((tm, d_model), lambda i: (i, 0)),
            pl.BlockSpec((n_exp, tm), lambda i: (0, i)),
        ],
        out_shape=[
            jax.ShapeDtypeStruct((seq, d_model), BF16),
            jax.ShapeDtypeStruct((n_exp, seq), F32),
        ],
        compiler_params=_params("arbitrary"),
        name="router",
    )(x1, g.reshape(1, d_model), w_router.T)


def _topk_kernel(a_ref, posw_ref, gate_ref, ws_ref, nr_ref, *, cap, blk, win):
    a = a_ref[...]
    n_exp, seq = a.shape
    n_blk = seq // blk
    bits = pltpu.bitcast(a, jnp.int32)

    def bit_step(i, thr):
        cand = thr | jnp.left_shift(jnp.int32(1), 30 - i)
        cnt = jnp.sum((bits >= cand).astype(jnp.int32), axis=-1, keepdims=True)
        return jnp.where(cnt >= cap, cand, thr)

    thr = lax.fori_loop(0, 31, bit_step, jnp.zeros((n_exp, 1), jnp.int32))
    gt = bits > thr
    eq = bits == thr
    need = cap - jnp.sum(gt.astype(jnp.int32), axis=-1, keepdims=True)

    tri = (lax.broadcasted_iota(jnp.int32, (blk, blk), 0)
           <= lax.broadcasted_iota(jnp.int32, (blk, blk), 1)).astype(BF16)
    blk_of_tok = lax.broadcasted_iota(jnp.int32, (seq, n_blk), 0) // blk
    tok_to_blk = (blk_of_tok == lax.broadcasted_iota(jnp.int32, (seq, n_blk), 1)).astype(BF16)
    blk_before = (lax.broadcasted_iota(jnp.int32, (n_blk, n_blk), 0)
                  < lax.broadcasted_iota(jnp.int32, (n_blk, n_blk), 1)).astype(BF16)
    erow = lax.broadcasted_iota(jnp.int32, (2 * n_blk, seq), 0)
    ecol = lax.broadcasted_iota(jnp.int32, (2 * n_blk, seq), 1) // blk
    expand = jnp.where(erow == ecol, 32.0, jnp.where(erow - n_blk == ecol, 1.0, 0.0)).astype(BF16)

    def prefix_counts(mask):
        mb = jnp.where(mask, 1.0, 0.0).astype(BF16)
        local = jnp.concatenate(
            [jnp.dot(mb[:, b * blk:(b + 1) * blk], tri, preferred_element_type=F32) for b in range(n_blk)],
            axis=1)
        per_blk = jnp.dot(mb, tok_to_blk, preferred_element_type=F32)
        start = jnp.dot(per_blk.astype(BF16), blk_before, preferred_element_type=F32)
        hi = jnp.floor(start * (1.0 / 32.0))
        parts = jnp.concatenate([hi, start - 32.0 * hi], axis=1).astype(BF16)
        start_tok = jnp.dot(parts, expand, preferred_element_type=F32)
        return local + start_tok, start, start_tok, per_blk

    eq_incl, _, _, _ = prefix_counts(eq)
    sel = gt | (eq & (eq_incl - 1.0 < need.astype(F32)))
    incl, start, start_tok, per_blk = prefix_counts(sel)

    def window(s):
        return jnp.floor(s * (1.0 / MOE_WIN_ALIGN)) * MOE_WIN_ALIGN

    posw_ref[...] = jnp.where(sel, (incl - 1.0 - window(start_tok)).astype(jnp.int32), -1)
    gate_ref[...] = jnp.where(sel, a, 0.0)
    ws_ref[...] = window(start).astype(jnp.int32)
    span = start - window(start) + per_blk
    rounds = jnp.floor((span + float(win - 1)) * (1.0 / win))
    nr_ref[...] = jnp.max(rounds, axis=0, keepdims=True).astype(jnp.int32)


def _topk(aff_t, cap, blk, win):
    n_exp, seq = aff_t.shape
    n_blk = seq // blk
    full = lambda *_: (0, 0)
    return pl.pallas_call(
        functools.partial(_topk_kernel, cap=cap, blk=blk, win=win),
        grid=(1,),
        in_specs=[pl.BlockSpec((n_exp, seq), full)],
        out_specs=[pl.BlockSpec((n_exp, seq), full), pl.BlockSpec((n_exp, seq), full),
                   pl.BlockSpec((n_exp, n_blk), full), pl.BlockSpec((1, n_blk), full)],
        out_shape=[
            jax.ShapeDtypeStruct((n_exp, seq), jnp.int32),
            jax.ShapeDtypeStruct((n_exp, seq), F32),
            jax.ShapeDtypeStruct((n_exp, n_blk), jnp.int32),
            jax.ShapeDtypeStruct((1, n_blk), jnp.int32),
        ],
        compiler_params=_params("arbitrary"),
        name="topk",
    )(aff_t)


def _window(ws_ref, e, b, r, n_blk, win, cap):
    ws = ws_ref[e * n_blk + b] + r * win
    start = jnp.minimum(ws, cap - win)
    return pl.multiple_of(start, MOE_WIN_ALIGN), ws - start


def _gather_kernel(ws_ref, nr_ref, h_ref, rel_ref, xe_ref, *, blk, win, n_blk):
    n_exp, cap, _ = xe_ref.shape
    xe_ref[...] = jnp.zeros_like(xe_ref)
    slot = lax.broadcasted_iota(jnp.int32, (win, blk), 0)

    def block(b, _):
        rows = h_ref[pl.ds(pl.multiple_of(b * blk, blk), blk), :]
        rel = rel_ref[b]

        def one_round(r, _):
            starts, hots = [], []
            for e in range(n_exp):
                start, shift = _window(ws_ref, e, b, r, n_blk, win, cap)
                relr = rel[e:e + 1, :] - r * win
                key = jnp.where(relr >= 0, relr + shift, -1)
                hots.append(jnp.where(slot == key, 1.0, 0.0).astype(BF16))
                starts.append(start)
            res = jnp.dot(jnp.concatenate(hots, axis=0), rows, preferred_element_type=F32)
            for e in range(n_exp):
                dst = pl.ds(starts[e], win)
                xe_ref[e, dst, :] = (xe_ref[e, dst, :].astype(F32) + res[e * win:(e + 1) * win]).astype(BF16)
            return 0

        lax.fori_loop(0, nr_ref[b], one_round, 0)
        return 0

    lax.fori_loop(0, n_blk, block, 0)


def _moe_gather(ws_flat, n_rounds, h2, rel3, cap, win):
    seq, d_model = h2.shape
    n_blk, n_exp, blk = rel3.shape
    dq = d_model // 4
    grid_spec = pltpu.PrefetchScalarGridSpec(
        num_scalar_prefetch=2,
        grid=(4,),
        in_specs=[
            pl.BlockSpec((seq, dq), lambda c, ws, nr: (0, c)),
            pl.BlockSpec((n_blk, n_exp, blk), lambda c, ws, nr: (0, 0, 0)),
        ],
        out_specs=pl.BlockSpec((n_exp, cap, dq), lambda c, ws, nr: (0, 0, c)),
    )
    return pl.pallas_call(
        functools.partial(_gather_kernel, blk=blk, win=win, n_blk=n_blk),
        grid_spec=grid_spec,
        out_shape=jax.ShapeDtypeStruct((n_exp, cap, d_model), BF16),
        compiler_params=_params("arbitrary"),
        name="moe_gather",
    )(ws_flat, n_rounds, h2, rel3)


def _ffn_kernel(x_ref, wg_ref, wu_ref, wd_ref, y_ref, act_ref, *, n_f):
    s = pl.program_id(1)
    tf = wg_ref.shape[-1]

    @pl.when(s < n_f)
    def _():
        x = x_ref[0]
        g = jnp.dot(x, wg_ref[0].astype(BF16), preferred_element_type=F32)
        u = jnp.dot(x, wu_ref[0].astype(BF16), preferred_element_type=F32)
        act_ref[s] = (g * (1.0 / (1.0 + jnp.exp(-g))) * u).astype(BF16)

    @pl.when(s >= n_f)
    def _():
        acc = jnp.dot(act_ref[0], wd_ref[0, 0:tf, :].astype(BF16), preferred_element_type=F32)
        for f in range(1, n_f):
            acc = acc + jnp.dot(act_ref[f], wd_ref[0, f * tf:(f + 1) * tf, :].astype(BF16),
                                preferred_element_type=F32)
        y_ref[0] = acc.astype(BF16)


def _moe_ffn(xe, w_gate, w_up, w_down):
    n_exp, cap, d_model = xe.shape
    d_ff = w_gate.shape[-1]
    tf = 512
    tn = 512
    n_f, n_n = d_ff // tf, d_model // tn
    up_tile = lambda e, s: (e, 0, jnp.minimum(s, n_f - 1))
    down_tile = lambda e, s: (e, 0, jnp.maximum(s - n_f, 0))
    return pl.pallas_call(
        functools.partial(_ffn_kernel, n_f=n_f),
        grid=(n_exp, n_f + n_n),
        in_specs=[
            pl.BlockSpec((1, cap, d_model), lambda e, s: (e, 0, 0)),
            pl.BlockSpec((1, d_model, tf), up_tile),
            pl.BlockSpec((1, d_model, tf), up_tile),
            pl.BlockSpec((1, d_ff, tn), down_tile),
        ],
        out_specs=pl.BlockSpec((1, cap, tn), down_tile),
        out_shape=jax.ShapeDtypeStruct((n_exp, cap, d_model), BF16),
        scratch_shapes=[pltpu.VMEM((n_f, cap, tf), BF16)],
        compiler_params=_params("arbitrary", "arbitrary"),
        name="moe_ffn",
    )(xe, w_gate, w_up, w_down)


def _combine_kernel(ws_ref, nr_ref, ye_ref, x_ref, rel_ref, gate_ref, o_ref, *, blk, win, n_blk):
    n_exp, cap, _ = ye_ref.shape
    sub = x_ref.shape[0] // blk
    slot = lax.broadcasted_iota(jnp.int32, (win, blk), 0)

    def block(s, _):
        b = pl.program_id(1) * sub + s
        tok = pl.ds(pl.multiple_of(s * blk, blk), blk)
        rel = rel_ref[b]
        gate = gate_ref[b]

        def one_round(r, acc):
            his, los, wins = [], [], []
            for e in range(n_exp):
                start, shift = _window(ws_ref, e, b, r, n_blk, win, cap)
                relr = rel[e:e + 1, :] - r * win
                key = jnp.where(relr >= 0, relr + shift, -1)
                g = jnp.where(slot == key, gate[e:e + 1, :], 0.0)
                g_hi = g.astype(BF16)
                his.append(g_hi)
                los.append((g - g_hi.astype(F32)).astype(BF16))
                wins.append(ye_ref[e, pl.ds(start, win), :])
            lhs_t = jnp.concatenate([jnp.concatenate(his, axis=0), jnp.concatenate(los, axis=0)], axis=1)
            res = lax.dot_general(lhs_t, jnp.concatenate(wins, axis=0), (((0,), (0,)), ((), ())),
                                  preferred_element_type=F32)
            return acc + res[:blk] + res[blk:]

        o_ref[tok, :] = lax.fori_loop(0, nr_ref[b], one_round, x_ref[tok, :])
        return 0

    lax.fori_loop(0, sub, block, 0)


def _moe_combine(ws_flat, n_rounds, ye, x1, rel3, gate3, win):
    seq, d_model = x1.shape
    n_exp, cap, _ = ye.shape
    n_blk, _, blk = rel3.shape
    dq = d_model // 4
    tile = min(8, n_blk) * blk
    whole = lambda c, t, ws, nr: (0, 0, 0)
    grid_spec = pltpu.PrefetchScalarGridSpec(
        num_scalar_prefetch=2,
        grid=(4, seq // tile),
        in_specs=[
            pl.BlockSpec((n_exp, cap, dq), lambda c, t, ws, nr: (0, 0, c)),
            pl.BlockSpec((tile, dq), lambda c, t, ws, nr: (t, c)),
            pl.BlockSpec(rel3.shape, whole),
            pl.BlockSpec(gate3.shape, whole),
        ],
        out_specs=pl.BlockSpec((tile, dq), lambda c, t, ws, nr: (t, c)),
    )
    return pl.pallas_call(
        functools.partial(_combine_kernel, blk=blk, win=win, n_blk=n_blk),
        grid_spec=grid_spec,
        out_shape=jax.ShapeDtypeStruct((seq, d_model), F32),
        compiler_params=_params("arbitrary", "arbitrary"),
        name="moe_combine",
    )(ws_flat, n_rounds, ye, x1, rel3, gate3)


def _final_norm_kernel(x_ref, g_ref, o_ref):
    o_ref[...] = _rms(x_ref[...], g_ref[...])


def _final_norm(x, g):
    seq, d_model = x.shape
    tm = min(512, seq)
    return pl.pallas_call(
        _final_norm_kernel,
        grid=(seq // tm,),
        in_specs=[pl.BlockSpec((tm, d_model), lambda i: (i, 0)), pl.BlockSpec((1, d_model), lambda i: (0, 0))],
        out_specs=pl.BlockSpec((tm, d_model), lambda i: (i, 0)),
        out_shape=jax.ShapeDtypeStruct((seq, d_model), F32),
        compiler_params=_params("arbitrary"),
        name="final_norm",
    )(x, g.reshape(1, d_model))


def _layer(x, norm_mix_g, w_in, fwd, bwd, ssm_d, w_glu, b_glu, na_rpb, g_ssm_out, g_na_out, w_out,
           norm_ffn_g, w_router, w_gate, w_up, w_down):
    seq, d_model = x.shape
    d_ssm = ssm_d.shape[0]
    d_na = g_na_out.shape[0]
    n_exp = w_router.shape[1]
    cap = EC_CAPACITY_FACTOR * seq // n_exp
    blk = min(MOE_TOK_BLOCK, cap // 2)
    win = min(MOE_WIN, cap)

    u, qkv = _in_proj(x, norm_mix_g, w_in, d_ssm, d_na)
    y_ssm = _s5_post(_s5_scan(u, fwd, bwd), u, ssm_d, w_glu, b_glu, g_ssm_out)
    y_na = _neighbourhood_attention(qkv, na_rpb, g_na_out)
    x1 = _out_proj(y_ssm, y_na, w_out, x)

    h2, aff_t = _router(x1, norm_ffn_g, w_router)
    rel, gate, ws, n_rounds = _topk(aff_t, cap, blk, win)
    ws_flat = ws.reshape(-1)
    n_rounds = n_rounds.reshape(-1)
    rel3 = jnp.swapaxes(rel.reshape(n_exp, seq // blk, blk), 0, 1)
    gate3 = jnp.swapaxes(gate.reshape(n_exp, seq // blk, blk), 0, 1)
    xe = _moe_gather(ws_flat, n_rounds, h2, rel3, cap, win)
    ye = _moe_ffn(xe, w_gate, w_up, w_down)
    return _moe_combine(ws_flat, n_rounds, ye, x1, rel3, gate3, win)


def kernel(x, norm_mix_g, w_in, a_re_fwd, a_im_fwd, log_dt_fwd, b_re_fwd, b_im_fwd, c_re_fwd, c_im_fwd, a_re_bwd, a_im_bwd, log_dt_bwd, b_re_bwd, b_im_bwd, c_re_bwd, c_im_bwd, ssm_d, w_glu, b_glu, na_rpb, g_ssm_out, g_na_out, w_out, norm_ffn_g, w_router, w_gate, w_up, w_down, norm_final_g):
    bsz = x.shape[0]
    depth = w_in.shape[0]
    outs = []
    for b in range(bsz):
        xb = x[b]
        for l in range(depth):
            fwd = (a_re_fwd[l], a_im_fwd[l], log_dt_fwd[l], b_re_fwd[l], b_im_fwd[l], c_re_fwd[l], c_im_fwd[l])
            bwd = (a_re_bwd[l], a_im_bwd[l], log_dt_bwd[l], b_re_bwd[l], b_im_bwd[l], c_re_bwd[l], c_im_bwd[l])
            xb = _layer(xb, norm_mix_g[l], w_in[l], fwd, bwd, ssm_d[l], w_glu[l], b_glu[l], na_rpb[l],
                        g_ssm_out[l], g_na_out[l], w_out[l], norm_ffn_g[l], w_router[l],
                        w_gate[l], w_up[l], w_down[l])
        outs.append(_final_norm(xb, norm_final_g))
    return jnp.stack(outs)
```

```python
import functools

import numpy as np
import jax
import jax.numpy as jnp
from jax import lax
from jax.experimental import pallas as pl
from jax.experimental.pallas import tpu as pltpu

F32 = jnp.float32
BF16 = jnp.bfloat16

RMS_EPS = 1e-6
SSM_GROUP = 16
SSM_STATE = 64
NA_HEADS = 16
NA_HEAD_DIM = 64
GRID_W = 64
WIN_ROWS = 8
WIN_COLS = 16
N_EXPERTS = 16
EC_CAPACITY_FACTOR = 2

S5_CHUNK = 16
S5_SEGS = 8
HEADS_PER_DOT = 4
MOE_TOK_BLOCK = 128
MOE_WIN_ALIGN = 16
MOE_WIN = 64
MASK_NEG = -1e30

VMEM_LIMIT_BYTES = 56 * 1024 * 1024


def _params(*semantics):
    return pltpu.CompilerParams(dimension_semantics=semantics, vmem_limit_bytes=VMEM_LIMIT_BYTES)


def _rms(x, g):
    ms = jnp.mean(x * x, axis=-1, keepdims=True)
    return x * lax.rsqrt(ms + RMS_EPS) * g


def _inproj_kernel(x_ref, g_ref, w_ref, u_ref, qkv_ref, h_scr, *, n_u, n_q, q_scale):
    j = pl.program_id(1)

    @pl.when(j == 0)
    def _():
        h_scr[...] = _rms(x_ref[...], g_ref[...]).astype(BF16)

    def project():
        return jnp.dot(h_scr[...], w_ref[...].astype(BF16), preferred_element_type=F32)

    @pl.when(j < n_u)
    def _():
        u_ref[...] = project()

    @pl.when(j >= n_u)
    def _():
        scale = jnp.where(j < n_u + n_q, q_scale, 1.0).astype(F32)
        qkv_ref[...] = (project() * scale).astype(BF16)


def _in_proj(x, g, w_in, d_ssm, d_na):
    seq, d_model = x.shape
    tm = min(1024, seq)
    tn = 512
    n_u, n_q = d_ssm // tn, d_na // tn
    n_cols = w_in.shape[1] // tn
    kern = functools.partial(_inproj_kernel, n_u=n_u, n_q=n_q, q_scale=NA_HEAD_DIM ** -0.5)
    return pl.pallas_call(
        kern,
        grid=(seq // tm, n_cols),
        in_specs=[
            pl.BlockSpec((tm, d_model), lambda i, j: (i, 0)),
            pl.BlockSpec((1, d_model), lambda i, j: (0, 0)),
            pl.BlockSpec((d_model, tn), lambda i, j: (0, j)),
        ],
        out_specs=[
            pl.BlockSpec((tm, tn), lambda i, j: (i, jnp.minimum(j, n_u - 1))),
            pl.BlockSpec((tm, tn), lambda i, j: (i, jnp.maximum(j - n_u, 0))),
        ],
        out_shape=[
            jax.ShapeDtypeStruct((seq, d_ssm), F32),
            jax.ShapeDtypeStruct((seq, 3 * d_na), BF16),
        ],
        scratch_shapes=[pltpu.VMEM((tm, d_model), BF16)],
        compiler_params=_params("arbitrary", "arbitrary"),
        name="in_proj",
    )(x, g.reshape(1, d_model), w_in)


def _s5_direction_tables(a_re, a_im, log_dt, b_re, b_im, c_re, c_im, n_tiles, reverse):
    t_len = S5_CHUNK
    n_ch = b_re.shape[-1]
    hp = lax.Precision.HIGHEST
    dt = jnp.exp(log_dt.astype(F32))[:, None]
    xr, xi = a_re.astype(F32) * dt, a_im.astype(F32) * dt

    def power(xr_, xi_, k):
        k = k.reshape((-1,) + (1,) * xr_.ndim)
        mag = jnp.exp(xr_[None] * k)
        return mag * jnp.cos(xi_[None] * k), mag * jnp.sin(xi_[None] * k)

    pr, pi = power(xr, xi, jnp.arange(t_len + 1, dtype=F32))
    nr, ni = pr[1] - 1.0, pi[1]
    den = a_re * a_re + a_im * a_im
    qr, qi = (nr * a_re + ni * a_im) / den, (ni * a_re - nr * a_im) / den
    bb_r, bb_i = _cmul(qr[:, None, :], qi[:, None, :], jnp.swapaxes(b_re, 1, 2), jnp.swapaxes(b_im, 1, 2))
    cr, ci = c_re.astype(F32), c_im.astype(F32)

    def steps(e):
        return jnp.swapaxes(pr[e], 0, 1), jnp.swapaxes(pi[e], 0, 1)

    n_grp, n_st = a_re.shape
    lags = np.arange(t_len)[::-1] if reverse else np.arange(t_len)
    am_r, am_i = steps(lags)
    y_r, y_i = _cmul(cr[:, None, :, :], ci[:, None, :, :], am_r[:, :, None, :], am_i[:, :, None, :])
    y = jnp.concatenate([y_r, y_i], axis=-1).reshape(n_grp, t_len * n_ch, 2 * n_st)
    lag = jnp.einsum('gck,gqk->gcq', jnp.concatenate([bb_r, -bb_i], axis=-1), y, precision=hp)
    e_in = (np.arange(t_len)) if reverse else (t_len - 1 - np.arange(t_len))
    pe_r, pe_i = steps(e_in)
    st_r, st_i = _cmul(pe_r[:, :, None, :], pe_i[:, :, None, :], bb_r[:, None], bb_i[:, None])
    e_out = (t_len - np.arange(t_len)) if reverse else (np.arange(t_len) + 1)
    po_r, po_i = steps(e_out)
    wr, wi = _cmul(cr[:, None, :, :], ci[:, None, :, :], po_r[:, :, None, :], po_i[:, :, None, :])
    wr = wr.reshape(n_grp, t_len * n_ch, n_st)
    wi = wi.reshape(n_grp, t_len * n_ch, n_st)
    xrf, xif = xr.reshape(-1), xi.reshape(-1)
    tiles = jnp.arange(n_tiles, dtype=F32)
    if reverse:
        tiles = tiles[::-1]
    a_chunk = power(xrf, xif, jnp.full((1,), float(t_len), F32))
    a_seg = power(xrf, xif, jnp.full((1,), float(t_len * n_tiles), F32))
    a_tile = power(xrf, xif, float(t_len) * tiles)
    return lag, (st_r, st_i), (wr, -wi), a_chunk, a_seg, a_tile


def _cmul(ar, ai, br, bi):
    return ar * br - ai * bi, ar * bi + ai * br


def _s5_tables(fwd, bwd, n_tiles):
    tf = _s5_direction_tables(*fwd, n_tiles=n_tiles, reverse=False)
    tb = _s5_direction_tables(*bwd, n_tiles=n_tiles, reverse=True)
    strip = jnp.concatenate([jnp.zeros_like(tf[0]), tf[0], tb[0], jnp.zeros_like(tb[0])], axis=-1)

    e_grp = jnp.stack([tf[1][0], tf[1][1], tb[1][0], tb[1][1]]).astype(BF16)
    c_grp = jnp.stack([tf[2][0], tf[2][1], tb[2][0], tb[2][1]], axis=1).astype(BF16)

    ap = jnp.concatenate([tf[3][0], tf[3][1], tb[3][0], tb[3][1], tf[4][0], tf[4][1], tb[4][0], tb[4][1]], axis=0)
    pw = jnp.stack([tf[5][0], tf[5][1], tb[5][0], tb[5][1]])
    return strip, e_grp, c_grp, ap, pw


def _s5_a_kernel(u_ref, strip_ref, e_ref, y_ref, s_ref, w_blk, e_blk, ucat, *, n_tiles, seg_stride):
    n_grp = strip_ref.shape[0]
    cw = S5_CHUNK * SSM_GROUP
    n_plane, n_state = e_ref.shape[0], e_ref.shape[-1]
    lanes = u_ref.shape[-1]

    @pl.when(pl.program_id(0) == 0)
    def _():
        w_blk[...] = jnp.zeros_like(w_blk)
        e_blk[...] = jnp.zeros_like(e_blk)

    for g in range(n_grp):
        strip = strip_ref[g]
        for t in range(S5_CHUNK):
            r0 = t * lanes + g * SSM_GROUP
            fwd_lo = cw - t * SSM_GROUP
            bwd_lo = 2 * cw + (S5_CHUNK - 1 - t) * SSM_GROUP
            piece = strip[:, fwd_lo:fwd_lo + cw] + strip[:, bwd_lo:bwd_lo + cw]
            w_blk[r0:r0 + SSM_GROUP, g * cw:(g + 1) * cw] = piece.astype(BF16)
            for k in range(n_plane):
                c0 = ((g // 2) * n_plane + k) * 2 * n_state + (g % 2) * n_state
                e_blk[r0:r0 + SSM_GROUP, c0:c0 + n_state] = e_ref[k, g, t]

    def gather(jj, _):
        for t in range(S5_CHUNK):
            rows = [u_ref[pl.ds(S5_CHUNK * (2 * jj + h) + t, S5_SEGS, stride=seg_stride), :] for h in range(2)]
            dst = pl.ds(pl.multiple_of(jj * 2 * S5_SEGS, 2 * S5_SEGS), 2 * S5_SEGS)
            ucat[dst, t * lanes:(t + 1) * lanes] = jnp.concatenate(rows, axis=0).astype(BF16)
        return 0

    lax.fori_loop(0, n_tiles // 2, gather, 0)
    ub = ucat[...]
    y_ref[...] = jnp.dot(ub, w_blk[...], preferred_element_type=F32)
    s_ref[...] = jnp.dot(ub, e_blk[...], preferred_element_type=F32)


def _s5_bc_kernel(s_ref, yi_ref, c_ref, ap_ref, pw_ref, y_ref, sin, perm, ynat, c_blk, *, n_tiles, seg_stride):
    n_grp, n_plane, cw, n_state = c_ref.shape
    n_pair = n_grp // 2
    pw_ = 2 * n_state
    pc = n_plane * pw_
    lanes = y_ref.shape[-1]

    @pl.when(pl.program_id(0) == 0)
    def _():
        c_blk[...] = jnp.zeros_like(c_blk)
        rl = lax.broadcasted_iota(jnp.int32, (cw, n_grp * cw), 0)
        cl = lax.broadcasted_iota(jnp.int32, (cw, n_grp * cw), 1)
        for g in range(n_grp):
            tgt = (rl // SSM_GROUP) * lanes + g * SSM_GROUP + (rl % SSM_GROUP)
            perm[g * cw:(g + 1) * cw, :] = jnp.where(cl == tgt, 1.0, 0.0).astype(BF16)

    for g in range(n_grp):
        gl = g % 2
        for k in range(n_plane):
            c0 = k * pw_ + gl * n_state
            c_blk[g // 2, gl * cw:(gl + 1) * cw, c0:c0 + n_state] = c_ref[g, k]

    chains = [(q, d) for q in range(n_pair) for d in range(2)]

    def plane_lanes(q, d):
        base = q * pc + d * 2 * pw_
        return slice(base, base + pw_), slice(base + pw_, base + 2 * pw_)

    def coef(row, q):
        return ap_ref[row:row + 1, q * pw_:(q + 1) * pw_]

    def tile_rows(n, d):
        j = n if d == 0 else n_tiles - 1 - n
        return pl.ds(pl.multiple_of(j * S5_SEGS, S5_SEGS), S5_SEGS)

    def step(n, carry):
        out = []
        for (q, d), (zr, zi) in zip(chains, carry):
            re, im = plane_lanes(q, d)
            rows = tile_rows(n, d)
            sin[rows, re] = zr
            sin[rows, im] = zi
            ar, ai = coef(2 * d, q), coef(2 * d + 1, q)
            out.append((ar * zr - ai * zi + s_ref[rows, re], ar * zi + ai * zr + s_ref[rows, im]))
        return tuple(out)

    z0 = jnp.zeros((S5_SEGS, pw_), F32)
    ends = lax.fori_loop(0, n_tiles, step, tuple((z0, z0) for _ in chains))

    carries = []
    for (q, d), (zr, zi) in zip(chains, ends):
        sr, si = coef(4 + 2 * d, q), coef(5 + 2 * d, q)
        cr = jnp.zeros((1, pw_), F32)
        ci = jnp.zeros((1, pw_), F32)
        seg_r = [None] * S5_SEGS
        seg_i = [None] * S5_SEGS
        for s in (range(S5_SEGS) if d == 0 else range(S5_SEGS - 1, -1, -1)):
            seg_r[s], seg_i[s] = cr, ci
            cr, ci = (zr[s:s + 1] + sr * cr - si * ci, zi[s:s + 1] + sr * ci + si * cr)
        carries.append((jnp.concatenate(seg_r, axis=0), jnp.concatenate(seg_i, axis=0)))

    def fix(n8, _):
        tiles = pl.ds(pl.multiple_of(n8 * 8, 8), 8)
        for (q, d), (car_r, car_i) in zip(chains, carries):
            re, im = plane_lanes(q, d)
            pr8 = pw_ref[2 * d, tiles, q * pw_:(q + 1) * pw_]
            pi8 = pw_ref[2 * d + 1, tiles, q * pw_:(q + 1) * pw_]
            for r in range(8):
                rows = pl.ds(pl.multiple_of((n8 * 8 + r) * S5_SEGS, S5_SEGS), S5_SEGS)
                pr, pi = pr8[r:r + 1], pi8[r:r + 1]
                sin[rows, re] = sin[rows, re] + (pr * car_r - pi * car_i)
                sin[rows, im] = sin[rows, im] + (pr * car_i + pi * car_r)
        return 0

    lax.fori_loop(0, n_tiles // 8, fix, 0)

    parts = []
    for q in range(n_pair):
        lhs = sin[:, q * pc:(q + 1) * pc].astype(BF16)
        carried = lax.dot_general(lhs, c_blk[q], (((1,), (1,)), ((), ())), preferred_element_type=F32)
        parts.append(yi_ref[:, q * 2 * cw:(q + 1) * 2 * cw] + carried)
    y_chunk = jnp.concatenate(parts, axis=1).astype(BF16)
    ynat[...] = jnp.dot(y_chunk, perm[...], preferred_element_type=F32)

    def scatter(j, _):
        src = pl.ds(pl.multiple_of(j * S5_SEGS, S5_SEGS), S5_SEGS)
        for t in range(S5_CHUNK):
            y_ref[pl.ds(S5_CHUNK * j + t, S5_SEGS, stride=seg_stride), :] = ynat[src, t * lanes:(t + 1) * lanes]
        return 0

    lax.fori_loop(0, n_tiles, scatter, 0)


def _s5_scan(u, fwd, bwd):
    seq, d_ssm = u.shape
    n_groups = d_ssm // SSM_GROUP
    n_state = SSM_STATE
    cw = S5_CHUNK * SSM_GROUP
    n_rows = seq // S5_CHUNK
    n_tiles = n_rows // S5_SEGS
    lanes = 128
    gpb = lanes // SSM_GROUP
    n_blocks = d_ssm // lanes
    bw = gpb * cw
    sw = gpb * 4 * n_state
    seg_stride = n_tiles * S5_CHUNK

    strip, e_grp, c_grp, ap, pw = _s5_tables(fwd, bwd, n_tiles)

    y_intra, s_loc = pl.pallas_call(
        functools.partial(_s5_a_kernel, n_tiles=n_tiles, seg_stride=seg_stride),
        grid=(n_blocks,),
        in_specs=[
            pl.BlockSpec((seq, lanes), lambda i: (0, i)),
            pl.BlockSpec((gpb,) + strip.shape[1:], lambda i: (i, 0, 0)),
            pl.BlockSpec((e_grp.shape[0], gpb) + e_grp.shape[2:], lambda i: (0, i, 0, 0, 0)),
        ],
        out_specs=[
            pl.BlockSpec((n_rows, bw), lambda i: (0, i)),
            pl.BlockSpec((n_rows, sw), lambda i: (0, i)),
        ],
        out_shape=[
            jax.ShapeDtypeStruct((n_rows, n_blocks * bw), F32),
            jax.ShapeDtypeStruct((n_rows, n_blocks * sw), F32),
        ],
        scratch_shapes=[
            pltpu.VMEM((S5_CHUNK * lanes, bw), BF16),
            pltpu.VMEM((S5_CHUNK * lanes, sw), BF16),
            pltpu.VMEM((n_rows, S5_CHUNK * lanes), BF16),
        ],
        compiler_params=_params("arbitrary"),
        name="s5_a",
    )(u, strip, e_grp)

    pl_lanes = gpb * n_state
    return pl.pallas_call(
        functools.partial(_s5_bc_kernel, n_tiles=n_tiles, seg_stride=seg_stride),
        grid=(n_blocks,),
        in_specs=[
            pl.BlockSpec((n_rows, sw), lambda i: (0, i)),
            pl.BlockSpec((n_rows, bw), lambda i: (0, i)),
            pl.BlockSpec((gpb,) + c_grp.shape[1:], lambda i: (i, 0, 0, 0)),
            pl.BlockSpec((8, pl_lanes), lambda i: (0, i)),
            pl.BlockSpec((4, n_tiles, pl_lanes), lambda i: (0, 0, i)),
        ],
        out_specs=pl.BlockSpec((seq, lanes), lambda i: (0, i)),
        out_shape=jax.ShapeDtypeStruct((seq, d_ssm), F32),
        scratch_shapes=[
            pltpu.VMEM((n_rows, sw), F32),
            pltpu.VMEM((bw, bw), BF16),
            pltpu.VMEM((n_rows, bw), F32),
            pltpu.VMEM((gpb // 2, 2 * cw, 8 * n_state), BF16),
        ],
        compiler_params=_params("arbitrary"),
        name="s5_bc",
    )(s_loc, y_intra, c_grp, ap, pw)


def _s5_post_kernel(y_ref, u_ref, d_ref, w_ref, b_ref, g_ref, o_ref):
    y = y_ref[...] + d_ref[...] * u_ref[...]
    c0 = np.float32(np.sqrt(2.0 / np.pi))
    y = 0.5 * y * (1.0 + jnp.tanh(c0 * (y + np.float32(0.044715) * (y * y * y))))
    z = jnp.dot(y.astype(BF16), w_ref[...].astype(BF16), preferred_element_type=F32) + b_ref[...]
    o = y * (1.0 / (1.0 + jnp.exp(-z)))
    o_ref[...] = _rms(o, g_ref[...]).astype(BF16)


def _s5_post(y, u, d_skip, w_glu, b_glu, g):
    seq, d = y.shape
    tm = min(512, seq)
    row = lambda i: (i, 0)
    fix = lambda i: (0, 0)
    return pl.pallas_call(
        _s5_post_kernel,
        grid=(seq // tm,),
        in_specs=[
            pl.BlockSpec((tm, d), row), pl.BlockSpec((tm, d), row), pl.BlockSpec((1, d), fix),
            pl.BlockSpec((d, d), fix), pl.BlockSpec((1, d), fix), pl.BlockSpec((1, d), fix),
        ],
        out_specs=pl.BlockSpec((tm, d), row),
        out_shape=jax.ShapeDtypeStruct((seq, d), BF16),
        compiler_params=_params("arbitrary"),
        name="s5_post",
    )(y, u, d_skip.reshape(1, d), w_glu, b_glu.reshape(1, d), g.reshape(1, d))


def _na_bias_table(rpb):
    n_heads = rpb.shape[0]
    cols = np.arange(GRID_W)
    col_start = np.clip(cols - WIN_COLS // 2, 0, GRID_W - WIN_COLS)
    key_cols = np.arange(GRID_W)
    in_win = (key_cols[None, :] >= col_start[:, None]) & (key_cols[None, :] < col_start[:, None] + WIN_COLS)
    dx = key_cols[None, :] - cols[:, None] + (WIN_COLS - 1)
    pick_x = (dx[:, :, None] == np.arange(2 * WIN_COLS - 1)).astype(np.float32)
    halves = []
    for half in range(2):
        pad = ((0, 0), (half * GRID_W, (1 - half) * GRID_W))
        pick = np.pad(pick_x, pad + ((0, 0),))
        keep = np.pad(in_win, pad, constant_values=True)
        b = jnp.einsum('hyx,ckx->hyck', rpb.astype(F32), pick, precision=lax.Precision.HIGHEST)
        b = jnp.where(keep[None, None], b, MASK_NEG)
        b = b.reshape(n_heads // HEADS_PER_DOT, HEADS_PER_DOT, b.shape[1], GRID_W, 2 * GRID_W)
        halves.append(jnp.swapaxes(b, 1, 2).reshape(n_heads // HEADS_PER_DOT, b.shape[2],
                                                    HEADS_PER_DOT * GRID_W, 2 * GRID_W))
    return jnp.stack(halves)


def _na_kernel(q_ref, k_ref, v_ref, b_ref, g_ref, o_ref, *, rows):
    n_keys = WIN_ROWS * GRID_W
    k = k_ref[...].reshape(n_keys, k_ref.shape[-1])
    v = v_ref[...].reshape(n_keys, v_ref.shape[-1])
    r = pl.program_id(0)
    dy0 = jnp.clip(r - WIN_ROWS // 2, 0, rows - WIN_ROWS) - r + (WIN_ROWS - 1)
    pw = HEADS_PER_DOT * NA_HEAD_DIM
    row_head = lax.broadcasted_iota(jnp.int32, (HEADS_PER_DOT * GRID_W, pw), 0) // GRID_W
    col_head = lax.broadcasted_iota(jnp.int32, (HEADS_PER_DOT * GRID_W, pw), 1) // NA_HEAD_DIM
    diag = row_head == col_head
    out_head = lax.broadcasted_iota(jnp.int32, (GRID_W, pw), 1) // NA_HEAD_DIM
    outs = []
    for p in range(k.shape[-1] // pw):
        sl = slice(p * pw, (p + 1) * pw)
        q4 = q_ref[:, sl]
        qbd = jnp.where(diag, jnp.concatenate([q4] * HEADS_PER_DOT, axis=0), jnp.zeros((), BF16))
        s = lax.dot_general(qbd, k[:, sl], (((1,), (1,)), ((), ())), preferred_element_type=F32)
        s = s + jnp.concatenate([b_ref[0, p, dy0 + 2 * i] + b_ref[1, p, dy0 + 2 * i + 1]
                                 for i in range(WIN_ROWS // 2)], axis=1)
        m = jnp.max(s, axis=-1, keepdims=True)
        e = jnp.exp(s - m)
        l = jnp.sum(e, axis=-1, keepdims=True)
        o = jnp.dot(e.astype(BF16), v[:, sl], preferred_element_type=F32) / l
        acc = jnp.zeros((GRID_W, pw), F32)
        for h in range(HEADS_PER_DOT):
            acc = acc + jnp.where(out_head == h, o[h * GRID_W:(h + 1) * GRID_W], 0.0)
        outs.append(acc)
    y = jnp.concatenate(outs, axis=1)
    o_ref[...] = _rms(y, g_ref[...]).astype(BF16)


def _neighbourhood_attention(qkv, rpb, g):
    seq = qkv.shape[0]
    d_na = qkv.shape[1] // 3
    rows = seq // GRID_W
    bias = _na_bias_table(rpb)
    qkv3 = qkv.reshape(rows, GRID_W, 3 * d_na)

    def win_start(r):
        return jnp.clip(r - WIN_ROWS // 2, 0, rows - WIN_ROWS)

    window = (pl.Element(WIN_ROWS), pl.Element(GRID_W), pl.Element(d_na))

    return pl.pallas_call(
        functools.partial(_na_kernel, rows=rows),
        grid=(rows,),
        in_specs=[
            pl.BlockSpec((GRID_W, d_na), lambda r: (r, 0)),
            pl.BlockSpec(window, lambda r: (win_start(r), 0, d_na)),
            pl.BlockSpec(window, lambda r: (win_start(r), 0, 2 * d_na)),
            pl.BlockSpec(bias.shape, lambda r: (0, 0, 0, 0, 0)),
            pl.BlockSpec((1, d_na), lambda r: (0, 0)),
        ],
        out_specs=pl.BlockSpec((GRID_W, d_na), lambda r: (r, 0)),
        out_shape=jax.ShapeDtypeStruct((seq, d_na), BF16),
        compiler_params=_params("arbitrary"),
        name="na",
    )(qkv, qkv3, qkv3, bias, g.reshape(1, d_na))


def _outproj_kernel(a_ref, b_ref, w_ref, x_ref, o_ref):
    da = a_ref.shape[-1]
    acc = jnp.dot(a_ref[...], w_ref[:da, :].astype(BF16), preferred_element_type=F32)
    acc = acc + jnp.dot(b_ref[...], w_ref[da:, :].astype(BF16), preferred_element_type=F32)
    o_ref[...] = x_ref[...] + acc


def _out_proj(y_ssm, y_na, w_out, x):
    seq, d_model = x.shape
    da, db = y_ssm.shape[1], y_na.shape[1]
    tm = min(2048, seq)
    tn = 512
    return pl.pallas_call(
        _outproj_kernel,
        grid=(seq // tm, d_model // tn),
        in_specs=[
            pl.BlockSpec((tm, da), lambda i, j: (i, 0)),
            pl.BlockSpec((tm, db), lambda i, j: (i, 0)),
            pl.BlockSpec((da + db, tn), lambda i, j: (0, j)),
            pl.BlockSpec((tm, tn), lambda i, j: (i, j)),
        ],
        out_specs=pl.BlockSpec((tm, tn), lambda i, j: (i, j)),
        out_shape=jax.ShapeDtypeStruct((seq, d_model), F32),
        compiler_params=_params("arbitrary", "arbitrary"),
        name="out_proj",
    )(y_ssm, y_na, w_out, x)


def _router_kernel(x_ref, g_ref, wt_ref, h_ref, a_ref):
    h = _rms(x_ref[...], g_ref[...])
    h_hi = h.astype(BF16)
    h_ref[...] = h_hi
    h_lo = (h - h_hi.astype(F32)).astype(BF16)
    w = wt_ref[...]
    w_hi = w.astype(BF16)
    w_lo = (w - w_hi.astype(F32)).astype(BF16)
    n_exp = w.shape[0]
    nt = (((1,), (1,)), ((), ()))
    both = lax.dot_general(jnp.concatenate([w_hi, w_lo], axis=0), h_hi, nt, preferred_element_type=F32)
    logits = both[:n_exp] + both[n_exp:] + lax.dot_general(w_hi, h_lo, nt, preferred_element_type=F32)
    m = jnp.max(logits, axis=0, keepdims=True)
    e = jnp.exp(logits - m)
    a_ref[...] = e / jnp.sum(e, axis=0, keepdims=True)


def _router(x1, g, w_router):
    seq, d_model = x1.shape
    n_exp = w_router.shape[1]
    tm = min(512, seq)
    return pl.pallas_call(
        _router_kernel,
        grid=(seq // tm,),
        in_specs=[
            pl.BlockSpec((tm, d_model), lambda i: (i, 0)),
            pl.BlockSpec((1, d_model), lambda i: (0, 0)),
            pl.BlockSpec((n_exp, d_model), lambda i: (0, 0)),
        ],
        out_specs=[
            pl.BlockSpec((tm, d_model), lambda i: (i, 0)),
            pl.BlockSpec((n_exp, tm), lambda i: (0, i)),
        ],
        out_shape=[
            jax.ShapeDtypeStruct((seq, d_model), BF16),
            jax.ShapeDtypeStruct((n_exp, seq), F32),
        ],
        compiler_params=_params("arbitrary"),
        name="router",
    )(x1, g.reshape(1, d_model), w_router.T)


def _topk_kernel(a_ref, posw_ref, gate_ref, ws_ref, nr_ref, *, cap, blk, win):
    a = a_ref[...]
    n_exp, seq = a.shape
    n_blk = seq // blk
    bits = pltpu.bitcast(a, jnp.int32)

    def bit_step(i, thr):
        cand = thr | jnp.left_shift(jnp.int32(1), 30 - i)
        cnt = jnp.sum((bits >= cand).astype(jnp.int32), axis=-1, keepdims=True)
        return jnp.where(cnt >= cap, cand, thr)

    thr = lax.fori_loop(0, 31, bit_step, jnp.zeros((n_exp, 1), jnp.int32))
    gt = bits > thr
    eq = bits == thr
    need = cap - jnp.sum(gt.astype(jnp.int32), axis=-1, keepdims=True)

    tri = (lax.broadcasted_iota(jnp.int32, (blk, blk), 0)
           <= lax.broadcasted_iota(jnp.int32, (blk, blk), 1)).astype(BF16)
    blk_of_tok = lax.broadcasted_iota(jnp.int32, (seq, n_blk), 0) // blk
    tok_to_blk = (blk_of_tok == lax.broadcasted_iota(jnp.int32, (seq, n_blk), 1)).astype(BF16)
    blk_before = (lax.broadcasted_iota(jnp.int32, (n_blk, n_blk), 0)
                  < lax.broadcasted_iota(jnp.int32, (n_blk, n_blk), 1)).astype(BF16)
    erow = lax.broadcasted_iota(jnp.int32, (2 * n_blk, seq), 0)
    ecol = lax.broadcasted_iota(jnp.int32, (2 * n_blk, seq), 1) // blk
    expand = jnp.where(erow == ecol, 32.0, jnp.where(erow - n_blk == ecol, 1.0, 0.0)).astype(BF16)

    def prefix_counts(mask):
        mb = jnp.where(mask, 1.0, 0.0).astype(BF16)
        local = jnp.concatenate(
            [jnp.dot(mb[:, b * blk:(b + 1) * blk], tri, preferred_element_type=F32) for b in range(n_blk)],
            axis=1)
        per_blk = jnp.dot(mb, tok_to_blk, preferred_element_type=F32)
        start = jnp.dot(per_blk.astype(BF16), blk_before, preferred_element_type=F32)
        hi = jnp.floor(start * (1.0 / 32.0))
        parts = jnp.concatenate([hi, start - 32.0 * hi], axis=1).astype(BF16)
        start_tok = jnp.dot(parts, expand, preferred_element_type=F32)
        return local + start_tok, start, start_tok, per_blk

    eq_incl, _, _, _ = prefix_counts(eq)
    sel = gt | (eq & (eq_incl - 1.0 < need.astype(F32)))
    incl, start, start_tok, per_blk = prefix_counts(sel)

    def window(s):
        return jnp.floor(s * (1.0 / MOE_WIN_ALIGN)) * MOE_WIN_ALIGN

    posw_ref[...] = jnp.where(sel, (incl - 1.0 - window(start_tok)).astype(jnp.int32), -1)
    gate_ref[...] = jnp.where(sel, a, 0.0)
    ws_ref[...] = window(start).astype(jnp.int32)
    span = start - window(start) + per_blk
    rounds = jnp.floor((span + float(win - 1)) * (1.0 / win))
    nr_ref[...] = jnp.max(rounds, axis=0, keepdims=True).astype(jnp.int32)


def _topk(aff_t, cap, blk, win):
    n_exp, seq = aff_t.shape
    n_blk = seq // blk
    full = lambda *_: (0, 0)
    return pl.pallas_call(
        functools.partial(_topk_kernel, cap=cap, blk=blk, win=win),
        grid=(1,),
        in_specs=[pl.BlockSpec((n_exp, seq), full)],
        out_specs=[pl.BlockSpec((n_exp, seq), full), pl.BlockSpec((n_exp, seq), full),
                   pl.BlockSpec((n_exp, n_blk), full), pl.BlockSpec((1, n_blk), full)],
        out_shape=[
            jax.ShapeDtypeStruct((n_exp, seq), jnp.int32),
            jax.ShapeDtypeStruct((n_exp, seq), F32),
            jax.ShapeDtypeStruct((n_exp, n_blk), jnp.int32),
            jax.ShapeDtypeStruct((1, n_blk), jnp.int32),
        ],
        compiler_params=_params("arbitrary"),
        name="topk",
    )(aff_t)


def _window(ws_ref, e, b, r, n_blk, win, cap):
    ws = ws_ref[e * n_blk + b] + r * win
    start = jnp.minimum(ws, cap - win)
    return pl.multiple_of(start, MOE_WIN_ALIGN), ws - start


def _gather_kernel(ws_ref, nr_ref, h_ref, rel_ref, xe_ref, *, blk, win, n_blk):
    n_exp, cap, _ = xe_ref.shape
    xe_ref[...] = jnp.zeros_like(xe_ref)
    slot = lax.broadcasted_iota(jnp.int32, (win, blk), 0)

    def block(b, _):
        rows = h_ref[pl.ds(pl.multiple_of(b * blk, blk), blk), :]
        rel = rel_ref[b]

        def one_round(r, _):
            starts, hots = [], []
            for e in range(n_exp):
                start, shift = _window(ws_ref, e, b, r, n_blk, win, cap)
                relr = rel[e:e + 1, :] - r * win
                key = jnp.where(relr >= 0, relr + shift, -1)
                hots.append(jnp.where(slot == key, 1.0, 0.0).astype(BF16))
                starts.append(start)
            res = jnp.dot(jnp.concatenate(hots, axis=0), rows, preferred_element_type=F32)
            for e in range(n_exp):
                dst = pl.ds(starts[e], win)
                xe_ref[e, dst, :] = (xe_ref[e, dst, :].astype(F32) + res[e * win:(e + 1) * win]).astype(BF16)
            return 0

        lax.fori_loop(0, nr_ref[b], one_round, 0)
        return 0

    lax.fori_loop(0, n_blk, block, 0)


def _moe_gather(ws_flat, n_rounds, h2, rel3, cap, win):
    seq, d_model = h2.shape
    n_blk, n_exp, blk = rel3.shape
    dq = d_model // 4
    grid_spec = pltpu.PrefetchScalarGridSpec(
        num_scalar_prefetch=2,
        grid=(4,),
        in_specs=[
            pl.BlockSpec((seq, dq), lambda c, ws, nr: (0, c)),
            pl.BlockSpec((n_blk, n_exp, blk), lambda c, ws, nr: (0, 0, 0)),
        ],
        out_specs=pl.BlockSpec((n_exp, cap, dq), lambda c, ws, nr: (0, 0, c)),
    )
    return pl.pallas_call(
        functools.partial(_gather_kernel, blk=blk, win=win, n_blk=n_blk),
        grid_spec=grid_spec,
        out_shape=jax.ShapeDtypeStruct((n_exp, cap, d_model), BF16),
        compiler_params=_params("arbitrary"),
        name="moe_gather",
    )(ws_flat, n_rounds, h2, rel3)


def _ffn_kernel(x_ref, wg_ref, wu_ref, wd_ref, y_ref, act_ref, *, n_f):
    s = pl.program_id(1)
    tf = wg_ref.shape[-1]

    @pl.when(s < n_f)
    def _():
        x = x_ref[0]
        g = jnp.dot(x, wg_ref[0].astype(BF16), preferred_element_type=F32)
        u = jnp.dot(x, wu_ref[0].astype(BF16), preferred_element_type=F32)
        act_ref[s] = (g * (1.0 / (1.0 + jnp.exp(-g))) * u).astype(BF16)

    @pl.when(s >= n_f)
    def _():
        acc = jnp.dot(act_ref[0], wd_ref[0, 0:tf, :].astype(BF16), preferred_element_type=F32)
        for f in range(1, n_f):
            acc = acc + jnp.dot(act_ref[f], wd_ref[0, f * tf:(f + 1) * tf, :].astype(BF16),
                                preferred_element_type=F32)
        y_ref[0] = acc.astype(BF16)


def _moe_ffn(xe, w_gate, w_up, w_down):
    n_exp, cap, d_model = xe.shape
    d_ff = w_gate.shape[-1]
    tf = 512
    tn = 512
    n_f, n_n = d_ff // tf, d_model // tn
    up_tile = lambda e, s: (e, 0, jnp.minimum(s, n_f - 1))
    down_tile = lambda e, s: (e, 0, jnp.maximum(s - n_f, 0))
    return pl.pallas_call(
        functools.partial(_ffn_kernel, n_f=n_f),
        grid=(n_exp, n_f + n_n),
        in_specs=[
            pl.BlockSpec((1, cap, d_model), lambda e, s: (e, 0, 0)),
            pl.BlockSpec((1, d_model, tf), up_tile),
            pl.BlockSpec((1, d_model, tf), up_tile),
            pl.BlockSpec((1, d_ff, tn), down_tile),
        ],
        out_specs=pl.BlockSpec((1, cap, tn), down_tile),
        out_shape=jax.ShapeDtypeStruct((n_exp, cap, d_model), BF16),
        scratch_shapes=[pltpu.VMEM((n_f, cap, tf), BF16)],
        compiler_params=_params("arbitrary", "arbitrary"),
        name="moe_ffn",
    )(xe, w_gate, w_up, w_down)


def _combine_kernel(ws_ref, nr_ref, ye_ref, x_ref, rel_ref, gate_ref, o_ref, *, blk, win, n_blk):
    n_exp, cap, _ = ye_ref.shape
    sub = x_ref.shape[0] // blk
    slot = lax.broadcasted_iota(jnp.int32, (win, blk), 0)

    def block(s, _):
        b = pl.program_id(1) * sub + s
        tok = pl.ds(pl.multiple_of(s * blk, blk), blk)
        rel = rel_ref[b]
        gate = gate_ref[b]

        def one_round(r, acc):
            his, los, wins = [], [], []
            for e in range(n_exp):
                start, shift = _window(ws_ref, e, b, r, n_blk, win, cap)
                relr = rel[e:e + 1, :] - r * win
                key = jnp.where(relr >= 0, relr + shift, -1)
                g = jnp.where(slot == key, gate[e:e + 1, :], 0.0)
                g_hi = g.astype(BF16)
                his.append(g_hi)
                los.append((g - g_hi.astype(F32)).astype(BF16))
                wins.append(ye_ref[e, pl.ds(start, win), :])
            lhs_t = jnp.concatenate([jnp.concatenate(his, axis=0), jnp.concatenate(los, axis=0)], axis=1)
            res = lax.dot_general(lhs_t, jnp.concatenate(wins, axis=0), (((0,), (0,)), ((), ())),
                                  preferred_element_type=F32)
            return acc + res[:blk] + res[blk:]

        o_ref[tok, :] = lax.fori_loop(0, nr_ref[b], one_round, x_ref[tok, :])
        return 0

    lax.fori_loop(0, sub, block, 0)


def _moe_combine(ws_flat, n_rounds, ye, x1, rel3, gate3, win):
    seq, d_model = x1.shape
    n_exp, cap, _ = ye.shape
    n_blk, _, blk = rel3.shape
    dq = d_model // 4
    tile = min(8, n_blk) * blk
    whole = lambda c, t, ws, nr: (0, 0, 0)
    grid_spec = pltpu.PrefetchScalarGridSpec(
        num_scalar_prefetch=2,
        grid=(4, seq // tile),
        in_specs=[
            pl.BlockSpec((n_exp, cap, dq), lambda c, t, ws, nr: (0, 0, c)),
            pl.BlockSpec((tile, dq), lambda c, t, ws, nr: (t, c)),
            pl.BlockSpec(rel3.shape, whole),
            pl.BlockSpec(gate3.shape, whole),
        ],
        out_specs=pl.BlockSpec((tile, dq), lambda c, t, ws, nr: (t, c)),
    )
    return pl.pallas_call(
        functools.partial(_combine_kernel, blk=blk, win=win, n_blk=n_blk),
        grid_spec=grid_spec,
        out_shape=jax.ShapeDtypeStruct((seq, d_model), F32),
        compiler_params=_params("arbitrary", "arbitrary"),
        name="moe_combine",
    )(ws_flat, n_rounds, ye, x1, rel3, gate3)


def _final_norm_kernel(x_ref, g_ref, o_ref):
    o_ref[...] = _rms(x_ref[...], g_ref[...])


def _final_norm(x, g):
    seq, d_model = x.shape
    tm = min(512, seq)
    return pl.pallas_call(
        _final_norm_kernel,
        grid=(seq // tm,),
        in_specs=[pl.BlockSpec((tm, d_model), lambda i: (i, 0)), pl.BlockSpec((1, d_model), lambda i: (0, 0))],
        out_specs=pl.BlockSpec((tm, d_model), lambda i: (i, 0)),
        out_shape=jax.ShapeDtypeStruct((seq, d_model), F32),
        compiler_params=_params("arbitrary"),
        name="final_norm",
    )(x, g.reshape(1, d_model))


def _layer(x, norm_mix_g, w_in, fwd, bwd, ssm_d, w_glu, b_glu, na_rpb, g_ssm_out, g_na_out, w_out,
           norm_ffn_g, w_router, w_gate, w_up, w_down):
    seq, d_model = x.shape
    d_ssm = ssm_d.shape[0]
    d_na = g_na_out.shape[0]
    n_exp = w_router.shape[1]
    cap = EC_CAPACITY_FACTOR * seq // n_exp
    blk = min(MOE_TOK_BLOCK, cap // 2)
    win = min(MOE_WIN, cap)

    u, qkv = _in_proj(x, norm_mix_g, w_in, d_ssm, d_na)
    y_ssm = _s5_post(_s5_scan(u, fwd, bwd), u, ssm_d, w_glu, b_glu, g_ssm_out)
    y_na = _neighbourhood_attention(qkv, na_rpb, g_na_out)
    x1 = _out_proj(y_ssm, y_na, w_out, x)

    h2, aff_t = _router(x1, norm_ffn_g, w_router)
    rel, gate, ws, n_rounds = _topk(aff_t, cap, blk, win)
    ws_flat = ws.reshape(-1)
    n_rounds = n_rounds.reshape(-1)
    rel3 = jnp.swapaxes(rel.reshape(n_exp, seq // blk, blk), 0, 1)
    gate3 = jnp.swapaxes(gate.reshape(n_exp, seq // blk, blk), 0, 1)
    xe = _moe_gather(ws_flat, n_rounds, h2, rel3, cap, win)
    ye = _moe_ffn(xe, w_gate, w_up, w_down)
    return _moe_combine(ws_flat, n_rounds, ye, x1, rel3, gate3, win)


def kernel(x, norm_mix_g, w_in, a_re_fwd, a_im_fwd, log_dt_fwd, b_re_fwd, b_im_fwd, c_re_fwd, c_im_fwd, a_re_bwd, a_im_bwd, log_dt_bwd, b_re_bwd, b_im_bwd, c_re_bwd, c_im_bwd, ssm_d, w_glu, b_glu, na_rpb, g_ssm_out, g_na_out, w_out, norm_ffn_g, w_router, w_gate, w_up, w_down, norm_final_g):
    bsz = x.shape[0]
    depth = w_in.shape[0]
    outs = []
    for b in range(bsz):
        xb = x[b]
        for l in range(depth):
            fwd = (a_re_fwd[l], a_im_fwd[l], log_dt_fwd[l], b_re_fwd[l], b_im_fwd[l], c_re_fwd[l], c_im_fwd[l])
            bwd = (a_re_bwd[l], a_im_bwd[l], log_dt_bwd[l], b_re_bwd[l], b_im_bwd[l], c_re_bwd[l], c_im_bwd[l])
            xb = _layer(xb, norm_mix_g[l], w_in[l], fwd, bwd, ssm_d[l], w_glu[l], b_glu[l], na_rpb[l],
                        g_ssm_out[l], g_na_out[l], w_out[l], norm_ffn_g[l], w_router[l],
                        w_gate[l], w_up[l], w_down[l])
        outs.append(_final_norm(xb, norm_final_g))
    return jnp.stack(outs)
```

```python
import functools

import numpy as np
import jax
import jax.numpy as jnp
from jax import lax
from jax.experimental import pallas as pl
from jax.experimental.pallas import tpu as pltpu

F32 = jnp.float32
BF16 = jnp.bfloat16

RMS_EPS = 1e-6
SSM_GROUP = 16
SSM_STATE = 64
NA_HEADS = 16
NA_HEAD_DIM = 64
GRID_W = 64
WIN_ROWS = 8
WIN_COLS = 16
N_EXPERTS = 16
EC_CAPACITY_FACTOR = 2

S5_CHUNK = 16
S5_SEGS = 8
HEADS_PER_DOT = 4
MOE_TOK_BLOCK = 128
MOE_WIN_ALIGN = 16
MOE_WIN = 64
MASK_NEG = -1e30

VMEM_LIMIT_BYTES = 56 * 1024 * 1024


def _params(*semantics):
    return pltpu.CompilerParams(dimension_semantics=semantics, vmem_limit_bytes=VMEM_LIMIT_BYTES)


def _rms(x, g):
    ms = jnp.mean(x * x, axis=-1, keepdims=True)
    return x * lax.rsqrt(ms + RMS_EPS) * g


def _inproj_kernel(x_ref, g_ref, w_ref, u_ref, qkv_ref, h_scr, *, n_u, n_q, q_scale):
    j = pl.program_id(1)

    @pl.when(j == 0)
    def _():
        h_scr[...] = _rms(x_ref[...], g_ref[...]).astype(BF16)

    def project():
        return jnp.dot(h_scr[...], w_ref[...].astype(BF16), preferred_element_type=F32)

    @pl.when(j < n_u)
    def _():
        u_ref[...] = project()

    @pl.when(j >= n_u)
    def _():
        scale = jnp.where(j < n_u + n_q, q_scale, 1.0).astype(F32)
        qkv_ref[...] = (project() * scale).astype(BF16)


def _in_proj(x, g, w_in, d_ssm, d_na):
    seq, d_model = x.shape
    tm = min(1024, seq)
    tn = 512
    n_u, n_q = d_ssm // tn, d_na // tn
    n_cols = w_in.shape[1] // tn
    kern = functools.partial(_inproj_kernel, n_u=n_u, n_q=n_q, q_scale=NA_HEAD_DIM ** -0.5)
    return pl.pallas_call(
        kern,
        grid=(seq // tm, n_cols),
        in_specs=[
            pl.BlockSpec((tm, d_model), lambda i, j: (i, 0)),
            pl.BlockSpec((1, d_model), lambda i, j: (0, 0)),
            pl.BlockSpec((d_model, tn), lambda i, j: (0, j)),
        ],
        out_specs=[
            pl.BlockSpec((tm, tn), lambda i, j: (i, jnp.minimum(j, n_u - 1))),
            pl.BlockSpec((tm, tn), lambda i, j: (i, jnp.maximum(j - n_u, 0))),
        ],
        out_shape=[
            jax.ShapeDtypeStruct((seq, d_ssm), F32),
            jax.ShapeDtypeStruct((seq, 3 * d_na), BF16),
        ],
        scratch_shapes=[pltpu.VMEM((tm, d_model), BF16)],
        compiler_params=_params("arbitrary", "arbitrary"),
        name="in_proj",
    )(x, g.reshape(1, d_model), w_in)


def _s5_direction_tables(a_re, a_im, log_dt, b_re, b_im, c_re, c_im, n_tiles, reverse):
    t_len = S5_CHUNK
    n_ch = b_re.shape[-1]
    hp = lax.Precision.HIGHEST
    dt = jnp.exp(log_dt.astype(F32))[:, None]
    xr, xi = a_re.astype(F32) * dt, a_im.astype(F32) * dt

    def power(xr_, xi_, k):
        k = k.reshape((-1,) + (1,) * xr_.ndim)
        mag = jnp.exp(xr_[None] * k)
        return mag * jnp.cos(xi_[None] * k), mag * jnp.sin(xi_[None] * k)

    pr, pi = power(xr, xi, jnp.arange(t_len + 1, dtype=F32))
    nr, ni = pr[1] - 1.0, pi[1]
    den = a_re * a_re + a_im * a_im
    qr, qi = (nr * a_re + ni * a_im) / den, (ni * a_re - nr * a_im) / den
    bb_r, bb_i = _cmul(qr[:, None, :], qi[:, None, :], jnp.swapaxes(b_re, 1, 2), jnp.swapaxes(b_im, 1, 2))
    cr, ci = c_re.astype(F32), c_im.astype(F32)

    def steps(e):
        return jnp.swapaxes(pr[e], 0, 1), jnp.swapaxes(pi[e], 0, 1)

    n_grp, n_st = a_re.shape
    lags = np.arange(t_len)[::-1] if reverse else np.arange(t_len)
    am_r, am_i = steps(lags)
    y_r, y_i = _cmul(cr[:, None, :, :], ci[:, None, :, :], am_r[:, :, None, :], am_i[:, :, None, :])
    y = jnp.concatenate([y_r, y_i], axis=-1).reshape(n_grp, t_len * n_ch, 2 * n_st)
    lag = jnp.einsum('gck,gqk->gcq', jnp.concatenate([bb_r, -bb_i], axis=-1), y, precision=hp)
    e_in = (np.arange(t_len)) if reverse else (t_len - 1 - np.arange(t_len))
    pe_r, pe_i = steps(e_in)
    st_r, st_i = _cmul(pe_r[:, :, None, :], pe_i[:, :, None, :], bb_r[:, None], bb_i[:, None])
    e_out = (t_len - np.arange(t_len)) if reverse else (np.arange(t_len) + 1)
    po_r, po_i = steps(e_out)
    wr, wi = _cmul(cr[:, None, :, :], ci[:, None, :, :], po_r[:, :, None, :], po_i[:, :, None, :])
    wr = wr.reshape(n_grp, t_len * n_ch, n_st)
    wi = wi.reshape(n_grp, t_len * n_ch, n_st)
    xrf, xif = xr.reshape(-1), xi.reshape(-1)
    tiles = jnp.arange(n_tiles, dtype=F32)
    if reverse:
        tiles = tiles[::-1]
    a_chunk = power(xrf, xif, jnp.full((1,), float(t_len), F32))
    a_seg = power(xrf, xif, jnp.full((1,), float(t_len * n_tiles), F32))
    a_tile = power(xrf, xif, float(t_len) * tiles)
    return lag, (st_r, st_i), (wr, -wi), a_chunk, a_seg, a_tile


def _cmul(ar, ai, br, bi):
    return ar * br - ai * bi, ar * bi + ai * br


def _s5_tables(fwd, bwd, n_tiles):
    tf = _s5_direction_tables(*fwd, n_tiles=n_tiles, reverse=False)
    tb = _s5_direction_tables(*bwd, n_tiles=n_tiles, reverse=True)
    lags = (tf[0], tb[0])
    e_planes = tuple(p.astype(BF16) for p in (tf[1][0], tf[1][1], tb[1][0], tb[1][1]))
    c_planes = tuple(p.astype(BF16) for p in (tf[2][0], tf[2][1], tb[2][0], tb[2][1]))

    ap = jnp.concatenate([tf[3][0], tf[3][1], tb[3][0], tb[3][1], tf[4][0], tf[4][1], tb[4][0], tb[4][1]], axis=0)
    pw = jnp.stack([tf[5][0], tf[5][1], tb[5][0], tb[5][1]])
    return lags, e_planes, c_planes, ap, pw


def _s5_a_kernel(u_ref, lagf_ref, lagb_ref, e0_ref, e1_ref, e2_ref, e3_ref, y_ref, s_ref, w_blk, e_blk, ucat, *,
                 n_tiles, seg_stride):
    e_refs = (e0_ref, e1_ref, e2_ref, e3_ref)
    n_grp = lagf_ref.shape[0]
    cw = S5_CHUNK * SSM_GROUP
    n_plane, n_state = len(e_refs), e0_ref.shape[-1]
    lanes = u_ref.shape[-1]

    @pl.when(pl.program_id(0) == 0)
    def _():
        w_blk[...] = jnp.zeros_like(w_blk)
        e_blk[...] = jnp.zeros_like(e_blk)

    for g in range(n_grp):
        blank = jnp.zeros((SSM_GROUP, cw), F32)
        strip = jnp.concatenate([blank, lagf_ref[g], lagb_ref[g], blank], axis=1)
        for t in range(S5_CHUNK):
            r0 = t * lanes + g * SSM_GROUP
            fwd_lo = cw - t * SSM_GROUP
            bwd_lo = 2 * cw + (S5_CHUNK - 1 - t) * SSM_GROUP
            piece = strip[:, fwd_lo:fwd_lo + cw] + strip[:, bwd_lo:bwd_lo + cw]
            w_blk[r0:r0 + SSM_GROUP, g * cw:(g + 1) * cw] = piece.astype(BF16)
            for k in range(n_plane):
                c0 = ((g // 2) * n_plane + k) * 2 * n_state + (g % 2) * n_state
                e_blk[r0:r0 + SSM_GROUP, c0:c0 + n_state] = e_refs[k][g, t]

    def gather(jj, _):
        for t in range(S5_CHUNK):
            rows = [u_ref[pl.ds(S5_CHUNK * (2 * jj + h) + t, S5_SEGS, stride=seg_stride), :] for h in range(2)]
            dst = pl.ds(pl.multiple_of(jj * 2 * S5_SEGS, 2 * S5_SEGS), 2 * S5_SEGS)
            ucat[dst, t * lanes:(t + 1) * lanes] = jnp.concatenate(rows, axis=0).astype(BF16)
        return 0

    lax.fori_loop(0, n_tiles // 2, gather, 0)
    ub = ucat[...]
    y_ref[...] = jnp.dot(ub, w_blk[...], preferred_element_type=F32)
    s_ref[...] = jnp.dot(ub, e_blk[...], preferred_element_type=F32)


def _s5_bc_kernel(s_ref, yi_ref, c0_ref, c1_ref, c2_ref, c3_ref, ap_ref, pw_ref, y_ref, sin, perm, ynat, c_blk, *,
                  n_tiles, seg_stride):
    c_refs = (c0_ref, c1_ref, c2_ref, c3_ref)
    n_plane = len(c_refs)
    n_grp, cw, n_state = c0_ref.shape
    n_pair = n_grp // 2
    pw_ = 2 * n_state
    pc = n_plane * pw_
    lanes = y_ref.shape[-1]

    @pl.when(pl.program_id(0) == 0)
    def _():
        c_blk[...] = jnp.zeros_like(c_blk)
        rl = lax.broadcasted_iota(jnp.int32, (cw, n_grp * cw), 0)
        cl = lax.broadcasted_iota(jnp.int32, (cw, n_grp * cw), 1)
        for g in range(n_grp):
            tgt = (rl // SSM_GROUP) * lanes + g * SSM_GROUP + (rl % SSM_GROUP)
            perm[g * cw:(g + 1) * cw, :] = jnp.where(cl == tgt, 1.0, 0.0).astype(BF16)

    for g in range(n_grp):
        gl = g % 2
        for k in range(n_plane):
            c0 = k * pw_ + gl * n_state
            c_blk[g // 2, gl * cw:(gl + 1) * cw, c0:c0 + n_state] = c_refs[k][g]

    chains = [(q, d) for q in range(n_pair) for d in range(2)]

    def plane_lanes(q, d):
        base = q * pc + d * 2 * pw_
        return slice(base, base + pw_), slice(base + pw_, base + 2 * pw_)

    def coef(row, q):
        return ap_ref[row:row + 1, q * pw_:(q + 1) * pw_]

    def tile_rows(n, d):
        j = n if d == 0 else n_tiles - 1 - n
        return pl.ds(pl.multiple_of(j * S5_SEGS, S5_SEGS), S5_SEGS)

    def step(n, carry):
        out = []
        for (q, d), (zr, zi) in zip(chains, carry):
            re, im = plane_lanes(q, d)
            rows = tile_rows(n, d)
            sin[rows, re] = zr
            sin[rows, im] = zi
            ar, ai = coef(2 * d, q), coef(2 * d + 1, q)
            out.append((ar * zr - ai * zi + s_ref[rows, re], ar * zi + ai * zr + s_ref[rows, im]))
        return tuple(out)

    z0 = jnp.zeros((S5_SEGS, pw_), F32)
    ends = lax.fori_loop(0, n_tiles, step, tuple((z0, z0) for _ in chains))

    carries = []
    for (q, d), (zr, zi) in zip(chains, ends):
        sr, si = coef(4 + 2 * d, q), coef(5 + 2 * d, q)
        cr = jnp.zeros((1, pw_), F32)
        ci = jnp.zeros((1, pw_), F32)
        seg_r = [None] * S5_SEGS
        seg_i = [None] * S5_SEGS
        for s in (range(S5_SEGS) if d == 0 else range(S5_SEGS - 1, -1, -1)):
            seg_r[s], seg_i[s] = cr, ci
            cr, ci = (zr[s:s + 1] + sr * cr - si * ci, zi[s:s + 1] + sr * ci + si * cr)
        carries.append((jnp.concatenate(seg_r, axis=0), jnp.concatenate(seg_i, axis=0)))

    def fix(n8, _):
        tiles = pl.ds(pl.multiple_of(n8 * 8, 8), 8)
        for (q, d), (car_r, car_i) in zip(chains, carries):
            re, im = plane_lanes(q, d)
            pr8 = pw_ref[2 * d, tiles, q * pw_:(q + 1) * pw_]
            pi8 = pw_ref[2 * d + 1, tiles, q * pw_:(q + 1) * pw_]
            for r in range(8):
                rows = pl.ds(pl.multiple_of((n8 * 8 + r) * S5_SEGS, S5_SEGS), S5_SEGS)
                pr, pi = pr8[r:r + 1], pi8[r:r + 1]
                sin[rows, re] = sin[rows, re] + (pr * car_r - pi * car_i)
                sin[rows, im] = sin[rows, im] + (pr * car_i + pi * car_r)
        return 0

    lax.fori_loop(0, n_tiles // 8, fix, 0)

    parts = []
    for q in range(n_pair):
        lhs = sin[:, q * pc:(q + 1) * pc].astype(BF16)
        carried = lax.dot_general(lhs, c_blk[q], (((1,), (1,)), ((), ())), preferred_element_type=F32)
        parts.append(yi_ref[:, q * 2 * cw:(q + 1) * 2 * cw] + carried)
    y_chunk = jnp.concatenate(parts, axis=1).astype(BF16)
    ynat[...] = jnp.dot(y_chunk, perm[...], preferred_element_type=F32)

    def scatter(j, _):
        src = pl.ds(pl.multiple_of(j * S5_SEGS, S5_SEGS), S5_SEGS)
        for t in range(S5_CHUNK):
            y_ref[pl.ds(S5_CHUNK * j + t, S5_SEGS, stride=seg_stride), :] = ynat[src, t * lanes:(t + 1) * lanes]
        return 0

    lax.fori_loop(0, n_tiles, scatter, 0)


def _s5_scan(u, fwd, bwd):
    seq, d_ssm = u.shape
    n_groups = d_ssm // SSM_GROUP
    n_state = SSM_STATE
    cw = S5_CHUNK * SSM_GROUP
    n_rows = seq // S5_CHUNK
    n_tiles = n_rows // S5_SEGS
    lanes = 128
    gpb = lanes // SSM_GROUP
    n_blocks = d_ssm // lanes
    bw = gpb * cw
    sw = gpb * 4 * n_state
    seg_stride = n_tiles * S5_CHUNK

    lags, e_planes, c_planes, ap, pw = _s5_tables(fwd, bwd, n_tiles)
    grp3 = lambda i: (i, 0, 0)
    grp4 = lambda i: (i, 0, 0, 0)

    y_intra, s_loc = pl.pallas_call(
        functools.partial(_s5_a_kernel, n_tiles=n_tiles, seg_stride=seg_stride),
        grid=(n_blocks,),
        in_specs=[
            pl.BlockSpec((seq, lanes), lambda i: (0, i)),
            *[pl.BlockSpec((gpb,) + t.shape[1:], grp3) for t in lags],
            *[pl.BlockSpec((gpb,) + t.shape[1:], grp4) for t in e_planes],
        ],
        out_specs=[
            pl.BlockSpec((n_rows, bw), lambda i: (0, i)),
            pl.BlockSpec((n_rows, sw), lambda i: (0, i)),
        ],
        out_shape=[
            jax.ShapeDtypeStruct((n_rows, n_blocks * bw), F32),
            jax.ShapeDtypeStruct((n_rows, n_blocks * sw), F32),
        ],
        scratch_shapes=[
            pltpu.VMEM((S5_CHUNK * lanes, bw), BF16),
            pltpu.VMEM((S5_CHUNK * lanes, sw), BF16),
            pltpu.VMEM((n_rows, S5_CHUNK * lanes), BF16),
        ],
        compiler_params=_params("arbitrary"),
        name="s5_a",
    )(u, *lags, *e_planes)

    pl_lanes = gpb * n_state
    return pl.pallas_call(
        functools.partial(_s5_bc_kernel, n_tiles=n_tiles, seg_stride=seg_stride),
        grid=(n_blocks,),
        in_specs=[
            pl.BlockSpec((n_rows, sw), lambda i: (0, i)),
            pl.BlockSpec((n_rows, bw), lambda i: (0, i)),
            *[pl.BlockSpec((gpb,) + t.shape[1:], grp3) for t in c_planes],
            pl.BlockSpec((8, pl_lanes), lambda i: (0, i)),
            pl.BlockSpec((4, n_tiles, pl_lanes), lambda i: (0, 0, i)),
        ],
        out_specs=pl.BlockSpec((seq, lanes), lambda i: (0, i)),
        out_shape=jax.ShapeDtypeStruct((seq, d_ssm), F32),
        scratch_shapes=[
            pltpu.VMEM((n_rows, sw), F32),
            pltpu.VMEM((bw, bw), BF16),
            pltpu.VMEM((n_rows, bw), F32),
            pltpu.VMEM((gpb // 2, 2 * cw, 8 * n_state), BF16),
        ],
        compiler_params=_params("arbitrary"),
        name="s5_bc",
    )(s_loc, y_intra, *c_planes, ap, pw)


def _s5_post_kernel(y_ref, u_ref, d_ref, w_ref, b_ref, g_ref, o_ref):
    y = y_ref[...] + d_ref[...] * u_ref[...]
    c0 = np.float32(np.sqrt(2.0 / np.pi))
    y = 0.5 * y * (1.0 + jnp.tanh(c0 * (y + np.float32(0.044715) * (y * y * y))))
    z = jnp.dot(y.astype(BF16), w_ref[...].astype(BF16), preferred_element_type=F32) + b_ref[...]
    o = y * (1.0 / (1.0 + jnp.exp(-z)))
    o_ref[...] = _rms(o, g_ref[...]).astype(BF16)


def _s5_post(y, u, d_skip, w_glu, b_glu, g):
    seq, d = y.shape
    tm = min(512, seq)
    row = lambda i: (i, 0)
    fix = lambda i: (0, 0)
    return pl.pallas_call(
        _s5_post_kernel,
        grid=(seq // tm,),
        in_specs=[
            pl.BlockSpec((tm, d), row), pl.BlockSpec((tm, d), row), pl.BlockSpec((1, d), fix),
            pl.BlockSpec((d, d), fix), pl.BlockSpec((1, d), fix), pl.BlockSpec((1, d), fix),
        ],
        out_specs=pl.BlockSpec((tm, d), row),
        out_shape=jax.ShapeDtypeStruct((seq, d), BF16),
        compiler_params=_params("arbitrary"),
        name="s5_post",
    )(y, u, d_skip.reshape(1, d), w_glu, b_glu.reshape(1, d), g.reshape(1, d))


def _na_bias_table(rpb):
    n_heads = rpb.shape[0]
    cols = np.arange(GRID_W)
    col_start = np.clip(cols - WIN_COLS // 2, 0, GRID_W - WIN_COLS)
    key_cols = np.arange(GRID_W)
    in_win = (key_cols[None, :] >= col_start[:, None]) & (key_cols[None, :] < col_start[:, None] + WIN_COLS)
    dx = key_cols[None, :] - cols[:, None] + (WIN_COLS - 1)
    pick_x = (dx[:, :, None] == np.arange(2 * WIN_COLS - 1)).astype(np.float32)
    halves = []
    for half in range(2):
        pad = ((0, 0), (half * GRID_W, (1 - half) * GRID_W))
        pick = np.pad(pick_x, pad + ((0, 0),))
        keep = np.pad(in_win, pad, constant_values=True)
        b = jnp.einsum('hyx,ckx->hyck', rpb.astype(F32), pick, precision=lax.Precision.HIGHEST)
        b = jnp.where(keep[None, None], b, MASK_NEG)
        b = b.reshape(n_heads // HEADS_PER_DOT, HEADS_PER_DOT, b.shape[1], GRID_W, 2 * GRID_W)
        halves.append(jnp.swapaxes(b, 1, 2).reshape(n_heads // HEADS_PER_DOT, b.shape[2],
                                                    HEADS_PER_DOT * GRID_W, 2 * GRID_W))
    return halves


def _na_kernel(q_ref, k_ref, v_ref, b0_ref, b1_ref, g_ref, o_ref, *, rows, rows_per_step):
    n_keys = WIN_ROWS * GRID_W
    d_na = k_ref.shape[-1]
    pw = HEADS_PER_DOT * NA_HEAD_DIM
    row_head = lax.broadcasted_iota(jnp.int32, (HEADS_PER_DOT * GRID_W, pw), 0) // GRID_W
    col_head = lax.broadcasted_iota(jnp.int32, (HEADS_PER_DOT * GRID_W, pw), 1) // NA_HEAD_DIM
    diag = row_head == col_head
    out_head = lax.broadcasted_iota(jnp.int32, (GRID_W, pw), 1) // NA_HEAD_DIM
    first = pl.program_id(0) * rows_per_step
    block_start = jnp.clip(first - WIN_ROWS // 2, 0, rows - k_ref.shape[0])

    def one_row(i, _):
        r = first + i
        win_start = jnp.clip(r - WIN_ROWS // 2, 0, rows - WIN_ROWS)
        dy0 = win_start - r + (WIN_ROWS - 1)
        k = k_ref[pl.ds(win_start - block_start, WIN_ROWS)].reshape(n_keys, d_na)
        v = v_ref[pl.ds(win_start - block_start, WIN_ROWS)].reshape(n_keys, d_na)
        q_rows = pl.ds(pl.multiple_of(i * GRID_W, GRID_W), GRID_W)
        outs = []
        for p in range(d_na // pw):
            sl = slice(p * pw, (p + 1) * pw)
            q4 = q_ref[q_rows, sl]
            qbd = jnp.where(diag, jnp.concatenate([q4] * HEADS_PER_DOT, axis=0), jnp.zeros((), BF16))
            s = lax.dot_general(qbd, k[:, sl], (((1,), (1,)), ((), ())), preferred_element_type=F32)
            s = s + jnp.concatenate([b0_ref[p, dy0 + 2 * j] + b1_ref[p, dy0 + 2 * j + 1]
                                     for j in range(WIN_ROWS // 2)], axis=1)
            m = jnp.max(s, axis=-1, keepdims=True)
            e = jnp.exp(s - m)
            l = jnp.sum(e, axis=-1, keepdims=True)
            o = jnp.dot(e.astype(BF16), v[:, sl], preferred_element_type=F32) / l
            acc = jnp.zeros((GRID_W, pw), F32)
            for h in range(HEADS_PER_DOT):
                acc = acc + jnp.where(out_head == h, o[h * GRID_W:(h + 1) * GRID_W], 0.0)
            outs.append(acc)
        y = jnp.concatenate(outs, axis=1)
        o_ref[q_rows, :] = _rms(y, g_ref[...]).astype(BF16)
        return 0

    lax.fori_loop(0, rows_per_step, one_row, 0)


def _neighbourhood_attention(qkv, rpb, g):
    seq = qkv.shape[0]
    d_na = qkv.shape[1] // 3
    rows = seq // GRID_W
    bias = _na_bias_table(rpb)
    qkv3 = qkv.reshape(rows, GRID_W, 3 * d_na)

    rps = 4
    key_rows = 2 * WIN_ROWS

    def block_start(b):
        return jnp.clip(b * rps - WIN_ROWS // 2, 0, rows - key_rows)

    window = (pl.Element(key_rows), pl.Element(GRID_W), pl.Element(d_na))

    return pl.pallas_call(
        functools.partial(_na_kernel, rows=rows, rows_per_step=rps),
        grid=(rows // rps,),
        in_specs=[
            pl.BlockSpec((rps * GRID_W, d_na), lambda b: (b, 0)),
            pl.BlockSpec(window, lambda b: (block_start(b), 0, d_na)),
            pl.BlockSpec(window, lambda b: (block_start(b), 0, 2 * d_na)),
            pl.BlockSpec(bias[0].shape, lambda b: (0, 0, 0, 0)),
            pl.BlockSpec(bias[1].shape, lambda b: (0, 0, 0, 0)),
            pl.BlockSpec((1, d_na), lambda b: (0, 0)),
        ],
        out_specs=pl.BlockSpec((rps * GRID_W, d_na), lambda b: (b, 0)),
        out_shape=jax.ShapeDtypeStruct((seq, d_na), BF16),
        compiler_params=_params("arbitrary"),
        name="na",
    )(qkv, qkv3, qkv3, bias[0], bias[1], g.reshape(1, d_na))


def _outproj_kernel(a_ref, b_ref, w_ref, x_ref, o_ref):
    da = a_ref.shape[-1]
    acc = jnp.dot(a_ref[...], w_ref[:da, :].astype(BF16), preferred_element_type=F32)
    acc = acc + jnp.dot(b_ref[...], w_ref[da:, :].astype(BF16), preferred_element_type=F32)
    o_ref[...] = x_ref[...] + acc


def _out_proj(y_ssm, y_na, w_out, x):
    seq, d_model = x.shape
    da, db = y_ssm.shape[1], y_na.shape[1]
    tm = min(2048, seq)
    tn = 512
    return pl.pallas_call(
        _outproj_kernel,
        grid=(seq // tm, d_model // tn),
        in_specs=[
            pl.BlockSpec((tm, da), lambda i, j: (i, 0)),
            pl.BlockSpec((tm, db), lambda i, j: (i, 0)),
            pl.BlockSpec((da + db, tn), lambda i, j: (0, j)),
            pl.BlockSpec((tm, tn), lambda i, j: (i, j)),
        ],
        out_specs=pl.BlockSpec((tm, tn), lambda i, j: (i, j)),
        out_shape=jax.ShapeDtypeStruct((seq, d_model), F32),
        compiler_params=_params("arbitrary", "arbitrary"),
        name="out_proj",
    )(y_ssm, y_na, w_out, x)


def _router_kernel(x_ref, g_ref, wt_ref, h_ref, a_ref):
    h = _rms(x_ref[...], g_ref[...])
    h_hi = h.astype(BF16)
    h_ref[...] = h_hi
    h_lo = (h - h_hi.astype(F32)).astype(BF16)
    w = wt_ref[...]
    w_hi = w.astype(BF16)
    w_lo = (w - w_hi.astype(F32)).astype(BF16)
    n_exp = w.shape[0]
    nt = (((1,), (1,)), ((), ()))
    both = lax.dot_general(jnp.concatenate([w_hi, w_lo], axis=0), h_hi, nt, preferred_element_type=F32)
    logits = both[:n_exp] + both[n_exp:] + lax.dot_general(w_hi, h_lo, nt, preferred_element_type=F32)
    m = jnp.max(logits, axis=0, keepdims=True)
    e = jnp.exp(logits - m)
    a_ref[...] = e / jnp.sum(e, axis=0, keepdims=True)


def _router(x1, g, w_router):
    seq, d_model = x1.shape
    n_exp = w_router.shape[1]
    tm = min(512, seq)
    return pl.pallas_call(
        _router_kernel,
        grid=(seq // tm,),
        in_specs=[
            pl.BlockSpec((tm, d_model), lambda i: (i, 0)),
            pl.BlockSpec((1, d_model), lambda i: (0, 0)),
            pl.BlockSpec((n_exp, d_model), lambda i: (0, 0)),
        ],
        out_specs=[
            pl.BlockSpec((tm, d_model), lambda i: (i, 0)),
            pl.BlockSpec((n_exp, tm), lambda i: (0, i)),
        ],
        out_shape=[
            jax.ShapeDtypeStruct((seq, d_model), BF16),
            jax.ShapeDtypeStruct((n_exp, seq), F32),
        ],
        compiler_params=_params("arbitrary"),
        name="router",
    )(x1, g.reshape(1, d_model), w_router.T)


def _topk_kernel(a_ref, posw_ref, gate_ref, ws_ref, nr_ref, *, cap, blk, win):
    a = a_ref[...]
    n_exp, seq = a.shape
    n_blk = seq // blk
    bits = pltpu.bitcast(a, jnp.int32)

    def bit_step(i, thr):
        cand = thr | jnp.left_shift(jnp.int32(1), 30 - i)
        cnt = jnp.sum((bits >= cand).astype(jnp.int32), axis=-1, keepdims=True)
        return jnp.where(cnt >= cap, cand, thr)

    thr = lax.fori_loop(0, 31, bit_step, jnp.zeros((n_exp, 1), jnp.int32))
    gt = bits > thr
    eq = bits == thr
    need = cap - jnp.sum(gt.astype(jnp.int32), axis=-1, keepdims=True)

    tri = (lax.broadcasted_iota(jnp.int32, (blk, blk), 0)
           <= lax.broadcasted_iota(jnp.int32, (blk, blk), 1)).astype(BF16)
    blk_of_tok = lax.broadcasted_iota(jnp.int32, (seq, n_blk), 0) // blk
    tok_to_blk = (blk_of_tok == lax.broadcasted_iota(jnp.int32, (seq, n_blk), 1)).astype(BF16)
    blk_before = (lax.broadcasted_iota(jnp.int32, (n_blk, n_blk), 0)
                  < lax.broadcasted_iota(jnp.int32, (n_blk, n_blk), 1)).astype(BF16)
    erow = lax.broadcasted_iota(jnp.int32, (2 * n_blk, seq), 0)
    ecol = lax.broadcasted_iota(jnp.int32, (2 * n_blk, seq), 1) // blk
    expand = jnp.where(erow == ecol, 32.0, jnp.where(erow - n_blk == ecol, 1.0, 0.0)).astype(BF16)

    def prefix_counts(mask):
        mb = jnp.where(mask, 1.0, 0.0).astype(BF16)
        local = jnp.concatenate(
            [jnp.dot(mb[:, b * blk:(b + 1) * blk], tri, preferred_element_type=F32) for b in range(n_blk)],
            axis=1)
        per_blk = jnp.dot(mb, tok_to_blk, preferred_element_type=F32)
        start = jnp.dot(per_blk.astype(BF16), blk_before, preferred_element_type=F32)
        hi = jnp.floor(start * (1.0 / 32.0))
        parts = jnp.concatenate([hi, start - 32.0 * hi], axis=1).astype(BF16)
        start_tok = jnp.dot(parts, expand, preferred_element_type=F32)
        return local + start_tok, start, start_tok, per_blk

    eq_incl, _, _, _ = prefix_counts(eq)
    sel = gt | (eq & (eq_incl - 1.0 < need.astype(F32)))
    incl, start, start_tok, per_blk = prefix_counts(sel)

    def window(s):
        return jnp.floor(s * (1.0 / MOE_WIN_ALIGN)) * MOE_WIN_ALIGN

    posw_ref[...] = jnp.where(sel, (incl - 1.0 - window(start_tok)).astype(jnp.int32), -1)
    gate_ref[...] = jnp.where(sel, a, 0.0)
    ws_ref[...] = window(start).astype(jnp.int32)
    span = start - window(start) + per_blk
    rounds = jnp.floor((span + float(win - 1)) * (1.0 / win))
    nr_ref[...] = jnp.max(rounds, axis=0, keepdims=True).astype(jnp.int32)


def _topk(aff_t, cap, blk, win):
    n_exp, seq = aff_t.shape
    n_blk = seq // blk
    full = lambda *_: (0, 0)
    return pl.pallas_call(
        functools.partial(_topk_kernel, cap=cap, blk=blk, win=win),
        grid=(1,),
        in_specs=[pl.BlockSpec((n_exp, seq), full)],
        out_specs=[pl.BlockSpec((n_exp, seq), full), pl.BlockSpec((n_exp, seq), full),
                   pl.BlockSpec((n_exp, n_blk), full), pl.BlockSpec((1, n_blk), full)],
        out_shape=[
            jax.ShapeDtypeStruct((n_exp, seq), jnp.int32),
            jax.ShapeDtypeStruct((n_exp, seq), F32),
            jax.ShapeDtypeStruct((n_exp, n_blk), jnp.int32),
            jax.ShapeDtypeStruct((1, n_blk), jnp.int32),
        ],
        compiler_params=_params("arbitrary"),
        name="topk",
    )(aff_t)


def _window(ws_ref, e, b, r, n_blk, win, cap):
    ws = ws_ref[e * n_blk + b] + r * win
    start = jnp.minimum(ws, cap - win)
    return pl.multiple_of(start, MOE_WIN_ALIGN), ws - start


def _gather_kernel(ws_ref, nr_ref, h_ref, rel_ref, xe_ref, *, blk, win, n_blk):
    n_exp, cap, _ = xe_ref.shape
    xe_ref[...] = jnp.zeros_like(xe_ref)
    slot = lax.broadcasted_iota(jnp.int32, (win, blk), 0)

    def block(b, _):
        rows = h_ref[pl.ds(pl.multiple_of(b * blk, blk), blk), :]
        rel = rel_ref[b]

        def one_round(r, _):
            starts, hots = [], []
            for e in range(n_exp):
                start, shift = _window(ws_ref, e, b, r, n_blk, win, cap)
                relr = rel[e:e + 1, :] - r * win
                key = jnp.where(relr >= 0, relr + shift, -1)
                hots.append(jnp.where(slot == key, 1.0, 0.0).astype(BF16))
                starts.append(start)
            res = jnp.dot(jnp.concatenate(hots, axis=0), rows, preferred_element_type=F32)
            for e in range(n_exp):
                dst = pl.ds(starts[e], win)
                xe_ref[e, dst, :] = (xe_ref[e, dst, :].astype(F32) + res[e * win:(e + 1) * win]).astype(BF16)
            return 0

        lax.fori_loop(0, nr_ref[b], one_round, 0)
        return 0

    lax.fori_loop(0, n_blk, block, 0)


def _moe_gather(ws_flat, n_rounds, h2, rel3, cap, win):
    seq, d_model = h2.shape
    n_blk, n_exp, blk = rel3.shape
    dq = d_model // 4
    grid_spec = pltpu.PrefetchScalarGridSpec(
        num_scalar_prefetch=2,
        grid=(4,),
        in_specs=[
            pl.BlockSpec((seq, dq), lambda c, ws, nr: (0, c)),
            pl.BlockSpec((n_blk, n_exp, blk), lambda c, ws, nr: (0, 0, 0)),
        ],
        out_specs=pl.BlockSpec((n_exp, cap, dq), lambda c, ws, nr: (0, 0, c)),
    )
    return pl.pallas_call(
        functools.partial(_gather_kernel, blk=blk, win=win, n_blk=n_blk),
        grid_spec=grid_spec,
        out_shape=jax.ShapeDtypeStruct((n_exp, cap, d_model), BF16),
        compiler_params=_params("arbitrary"),
        name="moe_gather",
    )(ws_flat, n_rounds, h2, rel3)


def _ffn_kernel(x_ref, wg_ref, wu_ref, wd_ref, y_ref, act_ref, *, n_f):
    s = pl.program_id(1)
    tf = wg_ref.shape[-1]

    @pl.when(s < n_f)
    def _():
        x = x_ref[0]
        g = jnp.dot(x, wg_ref[0].astype(BF16), preferred_element_type=F32)
        u = jnp.dot(x, wu_ref[0].astype(BF16), preferred_element_type=F32)
        act_ref[s] = (g * (1.0 / (1.0 + jnp.exp(-g))) * u).astype(BF16)

    @pl.when(s >= n_f)
    def _():
        acc = jnp.dot(act_ref[0], wd_ref[0, 0:tf, :].astype(BF16), preferred_element_type=F32)
        for f in range(1, n_f):
            acc = acc + jnp.dot(act_ref[f], wd_ref[0, f * tf:(f + 1) * tf, :].astype(BF16),
                                preferred_element_type=F32)
        y_ref[0] = acc.astype(BF16)


def _moe_ffn(xe, w_gate, w_up, w_down):
    n_exp, cap, d_model = xe.shape
    d_ff = w_gate.shape[-1]
    tf = 512
    tn = 512
    n_f, n_n = d_ff // tf, d_model // tn
    up_tile = lambda e, s: (e, 0, jnp.minimum(s, n_f - 1))
    down_tile = lambda e, s: (e, 0, jnp.maximum(s - n_f, 0))
    return pl.pallas_call(
        functools.partial(_ffn_kernel, n_f=n_f),
        grid=(n_exp, n_f + n_n),
        in_specs=[
            pl.BlockSpec((1, cap, d_model), lambda e, s: (e, 0, 0)),
            pl.BlockSpec((1, d_model, tf), up_tile),
            pl.BlockSpec((1, d_model, tf), up_tile),
            pl.BlockSpec((1, d_ff, tn), down_tile),
        ],
        out_specs=pl.BlockSpec((1, cap, tn), down_tile),
        out_shape=jax.ShapeDtypeStruct((n_exp, cap, d_model), BF16),
        scratch_shapes=[pltpu.VMEM((n_f, cap, tf), BF16)],
        compiler_params=_params("arbitrary", "arbitrary"),
        name="moe_ffn",
    )(xe, w_gate, w_up, w_down)


def _combine_kernel(ws_ref, nr_ref, ye_ref, x_ref, rel_ref, gate_ref, o_ref, *, blk, win, n_blk):
    n_exp, cap, _ = ye_ref.shape
    sub = x_ref.shape[0] // blk
    slot = lax.broadcasted_iota(jnp.int32, (win, blk), 0)

    def block(s, _):
        b = pl.program_id(1) * sub + s
        tok = pl.ds(pl.multiple_of(s * blk, blk), blk)
        rel = rel_ref[b]
        gate = gate_ref[b]

        def one_round(r, acc):
            gates, wins = [], []
            for e in range(n_exp):
                start, shift = _window(ws_ref, e, b, r, n_blk, win, cap)
                relr = rel[e:e + 1, :] - r * win
                key = jnp.where(relr >= 0, relr + shift, -1)
                gates.append(jnp.where(slot == key, gate[e:e + 1, :], 0.0).astype(BF16))
                wins.append(ye_ref[e, pl.ds(start, win), :])
            res = lax.dot_general(jnp.concatenate(gates, axis=0), jnp.concatenate(wins, axis=0),
                                  (((0,), (0,)), ((), ())), preferred_element_type=F32)
            return acc + res

        o_ref[tok, :] = lax.fori_loop(0, nr_ref[b], one_round, x_ref[tok, :])
        return 0

    lax.fori_loop(0, sub, block, 0)


def _moe_combine(ws_flat, n_rounds, ye, x1, rel3, gate3, win):
    seq, d_model = x1.shape
    n_exp, cap, _ = ye.shape
    n_blk, _, blk = rel3.shape
    dq = d_model // 4
    tile = min(8, n_blk) * blk
    whole = lambda c, t, ws, nr: (0, 0, 0)
    grid_spec = pltpu.PrefetchScalarGridSpec(
        num_scalar_prefetch=2,
        grid=(4, seq // tile),
        in_specs=[
            pl.BlockSpec((n_exp, cap, dq), lambda c, t, ws, nr: (0, 0, c)),
            pl.BlockSpec((tile, dq), lambda c, t, ws, nr: (t, c)),
            pl.BlockSpec(rel3.shape, whole),
            pl.BlockSpec(gate3.shape, whole),
        ],
        out_specs=pl.BlockSpec((tile, dq), lambda c, t, ws, nr: (t, c)),
    )
    return pl.pallas_call(
        functools.partial(_combine_kernel, blk=blk, win=win, n_blk=n_blk),
        grid_spec=grid_spec,
        out_shape=jax.ShapeDtypeStruct((seq, d_model), F32),
        compiler_params=_params("arbitrary", "arbitrary"),
        name="moe_combine",
    )(ws_flat, n_rounds, ye, x1, rel3, gate3)


def _final_norm_kernel(x_ref, g_ref, o_ref):
    o_ref[...] = _rms(x_ref[...], g_ref[...])


def _final_norm(x, g):
    seq, d_model = x.shape
    tm = min(512, seq)
    return pl.pallas_call(
        _final_norm_kernel,
        grid=(seq // tm,),
        in_specs=[pl.BlockSpec((tm, d_model), lambda i: (i, 0)), pl.BlockSpec((1, d_model), lambda i: (0, 0))],
        out_specs=pl.BlockSpec((tm, d_model), lambda i: (i, 0)),
        out_shape=jax.ShapeDtypeStruct((seq, d_model), F32),
        compiler_params=_params("arbitrary"),
        name="final_norm",
    )(x, g.reshape(1, d_model))


def _layer(x, norm_mix_g, w_in, fwd, bwd, ssm_d, w_glu, b_glu, na_rpb, g_ssm_out, g_na_out, w_out,
           norm_ffn_g, w_router, w_gate, w_up, w_down):
    seq, d_model = x.shape
    d_ssm = ssm_d.shape[0]
    d_na = g_na_out.shape[0]
    n_exp = w_router.shape[1]
    cap = EC_CAPACITY_FACTOR * seq // n_exp
    blk = min(MOE_TOK_BLOCK, cap // 2)
    win = min(MOE_WIN, cap)

    u, qkv = _in_proj(x, norm_mix_g, w_in, d_ssm, d_na)
    y_ssm = _s5_post(_s5_scan(u, fwd, bwd), u, ssm_d, w_glu, b_glu, g_ssm_out)
    y_na = _neighbourhood_attention(qkv, na_rpb, g_na_out)
    x1 = _out_proj(y_ssm, y_na, w_out, x)

    h2, aff_t = _router(x1, norm_ffn_g, w_router)
    rel, gate, ws, n_rounds = _topk(aff_t, cap, blk, win)
    ws_flat = ws.reshape(-1)
    n_rounds = n_rounds.reshape(-1)
    rel3 = jnp.swapaxes(rel.reshape(n_exp, seq // blk, blk), 0, 1)
    gate3 = jnp.swapaxes(gate.reshape(n_exp, seq // blk, blk), 0, 1)
    xe = _moe_gather(ws_flat, n_rounds, h2, rel3, cap, win)
    ye = _moe_ffn(xe, w_gate, w_up, w_down)
    return _moe_combine(ws_flat, n_rounds, ye, x1, rel3, gate3, win)


def kernel(x, norm_mix_g, w_in, a_re_fwd, a_im_fwd, log_dt_fwd, b_re_fwd, b_im_fwd, c_re_fwd, c_im_fwd, a_re_bwd, a_im_bwd, log_dt_bwd, b_re_bwd, b_im_bwd, c_re_bwd, c_im_bwd, ssm_d, w_glu, b_glu, na_rpb, g_ssm_out, g_na_out, w_out, norm_ffn_g, w_router, w_gate, w_up, w_down, norm_final_g):
    bsz = x.shape[0]
    depth = w_in.shape[0]
    outs = []
    for b in range(bsz):
        xb = x[b]
        for l in range(depth):
            fwd = (a_re_fwd[l], a_im_fwd[l], log_dt_fwd[l], b_re_fwd[l], b_im_fwd[l], c_re_fwd[l], c_im_fwd[l])
            bwd = (a_re_bwd[l], a_im_bwd[l], log_dt_bwd[l], b_re_bwd[l], b_im_bwd[l], c_re_bwd[l], c_im_bwd[l])
            xb = _layer(xb, norm_mix_g[l], w_in[l], fwd, bwd, ssm_d[l], w_glu[l], b_glu[l], na_rpb[l],
                        g_ssm_out[l], g_na_out[l], w_out[l], norm_ffn_g[l], w_router[l],
                        w_gate[l], w_up[l], w_down[l])
        outs.append(_final_norm(xb, norm_final_g))
    return jnp.stack(outs)
```

```python
import functools

import numpy as np
import jax
import jax.numpy as jnp
from jax import lax
from jax.experimental import pallas as pl
from jax.experimental.pallas import tpu as pltpu

F32 = jnp.float32
BF16 = jnp.bfloat16

RMS_EPS = 1e-6
SSM_GROUP = 16
SSM_STATE = 64
NA_HEADS = 16
NA_HEAD_DIM = 64
GRID_W = 64
WIN_ROWS = 8
WIN_COLS = 16
N_EXPERTS = 16
EC_CAPACITY_FACTOR = 2

S5_CHUNK = 16
S5_SEGS = 8
HEADS_PER_DOT = 4
MOE_TOK_BLOCK = 256
MOE_WIN_ALIGN = 16
MOE_WIN = 64
MASK_NEG = -1e30
LOG2_E = float(np.log2(np.e))

VMEM_LIMIT_BYTES = 56 * 1024 * 1024


def _params(*semantics):
    return pltpu.CompilerParams(dimension_semantics=semantics, vmem_limit_bytes=VMEM_LIMIT_BYTES)


def _rms(x, g):
    ms = jnp.mean(x * x, axis=-1, keepdims=True)
    return x * lax.rsqrt(ms + RMS_EPS) * g


def _inproj_kernel(x_ref, g_ref, w_ref, u_ref, qkv_ref, h_scr, *, n_u, n_q, q_scale):
    j = pl.program_id(1)

    @pl.when(j == 0)
    def _():
        h_scr[...] = _rms(x_ref[...], g_ref[...]).astype(BF16)

    def project():
        return jnp.dot(h_scr[...], w_ref[...].astype(BF16), preferred_element_type=F32)

    @pl.when(j < n_u)
    def _():
        u_ref[...] = project()

    @pl.when(j >= n_u)
    def _():
        scale = jnp.where(j < n_u + n_q, q_scale, 1.0).astype(F32)
        qkv_ref[...] = (project() * scale).astype(BF16)


def _in_proj(x, g, w_in, d_ssm, d_na):
    seq, d_model = x.shape
    tm = min(1024, seq)
    tn = 512
    n_u, n_q = d_ssm // tn, d_na // tn
    n_cols = w_in.shape[1] // tn
    kern = functools.partial(_inproj_kernel, n_u=n_u, n_q=n_q, q_scale=NA_HEAD_DIM ** -0.5 * LOG2_E)
    return pl.pallas_call(
        kern,
        grid=(seq // tm, n_cols),
        in_specs=[
            pl.BlockSpec((tm, d_model), lambda i, j: (i, 0)),
            pl.BlockSpec((1, d_model), lambda i, j: (0, 0)),
            pl.BlockSpec((d_model, tn), lambda i, j: (0, j)),
        ],
        out_specs=[
            pl.BlockSpec((tm, tn), lambda i, j: (i, jnp.minimum(j, n_u - 1))),
            pl.BlockSpec((tm, tn), lambda i, j: (i, jnp.maximum(j - n_u, 0))),
        ],
        out_shape=[
            jax.ShapeDtypeStruct((seq, d_ssm), F32),
            jax.ShapeDtypeStruct((seq, 3 * d_na), BF16),
        ],
        scratch_shapes=[pltpu.VMEM((tm, d_model), BF16)],
        compiler_params=_params("arbitrary", "arbitrary"),
        name="in_proj",
    )(x, g.reshape(1, d_model), w_in)


def _s5_direction_tables(a_re, a_im, log_dt, b_re, b_im, c_re, c_im, n_tiles, reverse):
    t_len = S5_CHUNK
    n_ch = b_re.shape[-1]
    hp = lax.Precision.HIGHEST
    dt = jnp.exp(log_dt.astype(F32))[:, None]
    xr, xi = a_re.astype(F32) * dt, a_im.astype(F32) * dt

    def power(xr_, xi_, k):
        k = k.reshape((-1,) + (1,) * xr_.ndim)
        mag = jnp.exp(xr_[None] * k)
        return mag * jnp.cos(xi_[None] * k), mag * jnp.sin(xi_[None] * k)

    pr, pi = power(xr, xi, jnp.arange(t_len + 1, dtype=F32))
    nr, ni = pr[1] - 1.0, pi[1]
    den = a_re * a_re + a_im * a_im
    qr, qi = (nr * a_re + ni * a_im) / den, (ni * a_re - nr * a_im) / den
    bb_r, bb_i = _cmul(qr[:, None, :], qi[:, None, :], jnp.swapaxes(b_re, 1, 2), jnp.swapaxes(b_im, 1, 2))
    cr, ci = c_re.astype(F32), c_im.astype(F32)

    def steps(e):
        return jnp.swapaxes(pr[e], 0, 1), jnp.swapaxes(pi[e], 0, 1)

    n_grp, n_st = a_re.shape
    lags = np.arange(t_len)[::-1] if reverse else np.arange(t_len)
    am_r, am_i = steps(lags)
    y_r, y_i = _cmul(cr[:, None, :, :], ci[:, None, :, :], am_r[:, :, None, :], am_i[:, :, None, :])
    y = jnp.concatenate([y_r, y_i], axis=-1).reshape(n_grp, t_len * n_ch, 2 * n_st)
    lag = jnp.einsum('gck,gqk->gcq', jnp.concatenate([bb_r, -bb_i], axis=-1), y, precision=hp)
    e_in = (np.arange(t_len)) if reverse else (t_len - 1 - np.arange(t_len))
    pe_r, pe_i = steps(e_in)
    st_r, st_i = _cmul(pe_r[:, :, None, :], pe_i[:, :, None, :], bb_r[:, None], bb_i[:, None])
    e_out = (t_len - np.arange(t_len)) if reverse else (np.arange(t_len) + 1)
    po_r, po_i = steps(e_out)
    wr, wi = _cmul(cr[:, None, :, :], ci[:, None, :, :], po_r[:, :, None, :], po_i[:, :, None, :])
    wr = wr.reshape(n_grp, t_len * n_ch, n_st)
    wi = wi.reshape(n_grp, t_len * n_ch, n_st)
    xrf, xif = xr.reshape(-1), xi.reshape(-1)
    tiles = jnp.arange(n_tiles, dtype=F32)
    if reverse:
        tiles = tiles[::-1]
    a_chunk = power(xrf, xif, jnp.full((1,), float(t_len), F32))
    a_seg = power(xrf, xif, jnp.full((1,), float(t_len * n_tiles), F32))
    a_tile = power(xrf, xif, float(t_len) * tiles)
    return lag, (st_r, st_i), (wr, -wi), a_chunk, a_seg, a_tile


def _cmul(ar, ai, br, bi):
    return ar * br - ai * bi, ar * bi + ai * br


def _s5_tables(fwd, bwd, n_tiles):
    tf = _s5_direction_tables(*fwd, n_tiles=n_tiles, reverse=False)
    tb = _s5_direction_tables(*bwd, n_tiles=n_tiles, reverse=True)
    lags = (tf[0], tb[0])
    e_planes = tuple(p.astype(BF16) for p in (tf[1][0], tf[1][1], tb[1][0], tb[1][1]))
    c_planes = tuple(p.astype(BF16) for p in (tf[2][0], tf[2][1], tb[2][0], tb[2][1]))

    ap = jnp.concatenate([tf[3][0], tf[3][1], tb[3][0], tb[3][1], tf[4][0], tf[4][1], tb[4][0], tb[4][1]], axis=0)
    pw = jnp.stack([tf[5][0], tf[5][1], tb[5][0], tb[5][1]])
    return lags, e_planes, c_planes, ap, pw


def _s5_a_kernel(u_ref, lagf_ref, lagb_ref, e0_ref, e1_ref, e2_ref, e3_ref, y_ref, s_ref, w_blk, e_blk, ucat, *,
                 n_tiles, seg_stride):
    e_refs = (e0_ref, e1_ref, e2_ref, e3_ref)
    n_grp = lagf_ref.shape[0]
    cw = S5_CHUNK * SSM_GROUP
    n_plane, n_state = len(e_refs), e0_ref.shape[-1]
    lanes = u_ref.shape[-1]

    @pl.when(pl.program_id(0) == 0)
    def _():
        w_blk[...] = jnp.zeros_like(w_blk)
        e_blk[...] = jnp.zeros_like(e_blk)

    for g in range(n_grp):
        blank = jnp.zeros((SSM_GROUP, cw), F32)
        strip = jnp.concatenate([blank, lagf_ref[g], lagb_ref[g], blank], axis=1)
        for t in range(S5_CHUNK):
            r0 = t * lanes + g * SSM_GROUP
            fwd_lo = cw - t * SSM_GROUP
            bwd_lo = 2 * cw + (S5_CHUNK - 1 - t) * SSM_GROUP
            piece = strip[:, fwd_lo:fwd_lo + cw] + strip[:, bwd_lo:bwd_lo + cw]
            w_blk[r0:r0 + SSM_GROUP, g * cw:(g + 1) * cw] = piece.astype(BF16)
            for k in range(n_plane):
                c0 = ((g // 2) * n_plane + k) * 2 * n_state + (g % 2) * n_state
                e_blk[r0:r0 + SSM_GROUP, c0:c0 + n_state] = e_refs[k][g, t]

    def gather(jj, _):
        for t in range(S5_CHUNK):
            rows = [u_ref[pl.ds(S5_CHUNK * (2 * jj + h) + t, S5_SEGS, stride=seg_stride), :] for h in range(2)]
            dst = pl.ds(pl.multiple_of(jj * 2 * S5_SEGS, 2 * S5_SEGS), 2 * S5_SEGS)
            ucat[dst, t * lanes:(t + 1) * lanes] = jnp.concatenate(rows, axis=0).astype(BF16)
        return 0

    lax.fori_loop(0, n_tiles // 2, gather, 0)
    ub = ucat[...]
    y_ref[...] = jnp.dot(ub, w_blk[...], preferred_element_type=F32)
    s_ref[...] = jnp.dot(ub, e_blk[...], preferred_element_type=F32)


def _s5_bc_kernel(s_ref, yi_ref, c0_ref, c1_ref, c2_ref, c3_ref, ap_ref, pw_ref, y_ref, sin, perm, ynat, c_blk, *,
                  n_tiles, seg_stride):
    c_refs = (c0_ref, c1_ref, c2_ref, c3_ref)
    n_plane = len(c_refs)
    n_grp, cw, n_state = c0_ref.shape
    n_pair = n_grp // 2
    pw_ = 2 * n_state
    pc = n_plane * pw_
    lanes = y_ref.shape[-1]

    @pl.when(pl.program_id(0) == 0)
    def _():
        c_blk[...] = jnp.zeros_like(c_blk)
        rl = lax.broadcasted_iota(jnp.int32, (cw, n_grp * cw), 0)
        cl = lax.broadcasted_iota(jnp.int32, (cw, n_grp * cw), 1)
        for g in range(n_grp):
            tgt = (rl // SSM_GROUP) * lanes + g * SSM_GROUP + (rl % SSM_GROUP)
            perm[g * cw:(g + 1) * cw, :] = jnp.where(cl == tgt, 1.0, 0.0).astype(BF16)

    for g in range(n_grp):
        gl = g % 2
        for k in range(n_plane):
            c0 = k * pw_ + gl * n_state
            c_blk[g // 2, gl * cw:(gl + 1) * cw, c0:c0 + n_state] = c_refs[k][g]

    chains = [(q, d) for q in range(n_pair) for d in range(2)]

    def plane_lanes(q, d):
        base = q * pc + d * 2 * pw_
        return slice(base, base + pw_), slice(base + pw_, base + 2 * pw_)

    def coef(row, q):
        return ap_ref[row:row + 1, q * pw_:(q + 1) * pw_]

    def tile_rows(n, d):
        j = n if d == 0 else n_tiles - 1 - n
        return pl.ds(pl.multiple_of(j * S5_SEGS, S5_SEGS), S5_SEGS)

    def step(n, carry):
        out = []
        for (q, d), (zr, zi) in zip(chains, carry):
            re, im = plane_lanes(q, d)
            rows = tile_rows(n, d)
            sin[rows, re] = zr
            sin[rows, im] = zi
            ar, ai = coef(2 * d, q), coef(2 * d + 1, q)
            out.append((ar * zr - ai * zi + s_ref[rows, re], ar * zi + ai * zr + s_ref[rows, im]))
        return tuple(out)

    z0 = jnp.zeros((S5_SEGS, pw_), F32)
    ends = lax.fori_loop(0, n_tiles, step, tuple((z0, z0) for _ in chains))

    carries = []
    for (q, d), (zr, zi) in zip(chains, ends):
        sr, si = coef(4 + 2 * d, q), coef(5 + 2 * d, q)
        cr = jnp.zeros((1, pw_), F32)
        ci = jnp.zeros((1, pw_), F32)
        seg_r = [None] * S5_SEGS
        seg_i = [None] * S5_SEGS
        for s in (range(S5_SEGS) if d == 0 else range(S5_SEGS - 1, -1, -1)):
            seg_r[s], seg_i[s] = cr, ci
            cr, ci = (zr[s:s + 1] + sr * cr - si * ci, zi[s:s + 1] + sr * ci + si * cr)
        carries.append((jnp.concatenate(seg_r, axis=0), jnp.concatenate(seg_i, axis=0)))

    def fix(n8, _):
        tiles = pl.ds(pl.multiple_of(n8 * 8, 8), 8)
        for (q, d), (car_r, car_i) in zip(chains, carries):
            re, im = plane_lanes(q, d)
            pr8 = pw_ref[2 * d, tiles, q * pw_:(q + 1) * pw_]
            pi8 = pw_ref[2 * d + 1, tiles, q * pw_:(q + 1) * pw_]
            for r in range(8):
                rows = pl.ds(pl.multiple_of((n8 * 8 + r) * S5_SEGS, S5_SEGS), S5_SEGS)
                pr, pi = pr8[r:r + 1], pi8[r:r + 1]
                sin[rows, re] = sin[rows, re] + (pr * car_r - pi * car_i)
                sin[rows, im] = sin[rows, im] + (pr * car_i + pi * car_r)
        return 0

    lax.fori_loop(0, n_tiles // 8, fix, 0)

    parts = []
    for q in range(n_pair):
        lhs = sin[:, q * pc:(q + 1) * pc].astype(BF16)
        carried = lax.dot_general(lhs, c_blk[q], (((1,), (1,)), ((), ())), preferred_element_type=F32)
        parts.append(yi_ref[:, q * 2 * cw:(q + 1) * 2 * cw] + carried)
    y_chunk = jnp.concatenate(parts, axis=1).astype(BF16)
    ynat[...] = jnp.dot(y_chunk, perm[...], preferred_element_type=F32)

    def scatter(j, _):
        src = pl.ds(pl.multiple_of(j * S5_SEGS, S5_SEGS), S5_SEGS)
        for t in range(S5_CHUNK):
            y_ref[pl.ds(S5_CHUNK * j + t, S5_SEGS, stride=seg_stride), :] = ynat[src, t * lanes:(t + 1) * lanes]
        return 0

    lax.fori_loop(0, n_tiles, scatter, 0)


def _s5_scan(u, fwd, bwd):
    seq, d_ssm = u.shape
    n_groups = d_ssm // SSM_GROUP
    n_state = SSM_STATE
    cw = S5_CHUNK * SSM_GROUP
    n_rows = seq // S5_CHUNK
    n_tiles = n_rows // S5_SEGS
    lanes = 128
    gpb = lanes // SSM_GROUP
    n_blocks = d_ssm // lanes
    bw = gpb * cw
    sw = gpb * 4 * n_state
    seg_stride = n_tiles * S5_CHUNK

    lags, e_planes, c_planes, ap, pw = _s5_tables(fwd, bwd, n_tiles)
    grp3 = lambda i: (i, 0, 0)
    grp4 = lambda i: (i, 0, 0, 0)

    y_intra, s_loc = pl.pallas_call(
        functools.partial(_s5_a_kernel, n_tiles=n_tiles, seg_stride=seg_stride),
        grid=(n_blocks,),
        in_specs=[
            pl.BlockSpec((seq, lanes), lambda i: (0, i)),
            *[pl.BlockSpec((gpb,) + t.shape[1:], grp3) for t in lags],
            *[pl.BlockSpec((gpb,) + t.shape[1:], grp4) for t in e_planes],
        ],
        out_specs=[
            pl.BlockSpec((n_rows, bw), lambda i: (0, i)),
            pl.BlockSpec((n_rows, sw), lambda i: (0, i)),
        ],
        out_shape=[
            jax.ShapeDtypeStruct((n_rows, n_blocks * bw), F32),
            jax.ShapeDtypeStruct((n_rows, n_blocks * sw), F32),
        ],
        scratch_shapes=[
            pltpu.VMEM((S5_CHUNK * lanes, bw), BF16),
            pltpu.VMEM((S5_CHUNK * lanes, sw), BF16),
            pltpu.VMEM((n_rows, S5_CHUNK * lanes), BF16),
        ],
        compiler_params=_params("arbitrary"),
        name="s5_a",
    )(u, *lags, *e_planes)

    pl_lanes = gpb * n_state
    return pl.pallas_call(
        functools.partial(_s5_bc_kernel, n_tiles=n_tiles, seg_stride=seg_stride),
        grid=(n_blocks,),
        in_specs=[
            pl.BlockSpec((n_rows, sw), lambda i: (0, i)),
            pl.BlockSpec((n_rows, bw), lambda i: (0, i)),
            *[pl.BlockSpec((gpb,) + t.shape[1:], grp3) for t in c_planes],
            pl.BlockSpec((8, pl_lanes), lambda i: (0, i)),
            pl.BlockSpec((4, n_tiles, pl_lanes), lambda i: (0, 0, i)),
        ],
        out_specs=pl.BlockSpec((seq, lanes), lambda i: (0, i)),
        out_shape=jax.ShapeDtypeStruct((seq, d_ssm), F32),
        scratch_shapes=[
            pltpu.VMEM((n_rows, sw), F32),
            pltpu.VMEM((bw, bw), BF16),
            pltpu.VMEM((n_rows, bw), F32),
            pltpu.VMEM((gpb // 2, 2 * cw, 8 * n_state), BF16),
        ],
        compiler_params=_params("arbitrary"),
        name="s5_bc",
    )(s_loc, y_intra, *c_planes, ap, pw)


def _s5_post_kernel(y_ref, u_ref, d_ref, w_ref, b_ref, g_ref, o_ref):
    y = y_ref[...] + d_ref[...] * u_ref[...]
    c0 = np.float32(np.sqrt(2.0 / np.pi))
    y = 0.5 * y * (1.0 + jnp.tanh(c0 * (y + np.float32(0.044715) * (y * y * y))))
    z = jnp.dot(y.astype(BF16), w_ref[...].astype(BF16), preferred_element_type=F32) + b_ref[...]
    o = y * (1.0 / (1.0 + jnp.exp(-z)))
    o_ref[...] = _rms(o, g_ref[...]).astype(BF16)


def _s5_post(y, u, d_skip, w_glu, b_glu, g):
    seq, d = y.shape
    tm = min(512, seq)
    row = lambda i: (i, 0)
    fix = lambda i: (0, 0)
    return pl.pallas_call(
        _s5_post_kernel,
        grid=(seq // tm,),
        in_specs=[
            pl.BlockSpec((tm, d), row), pl.BlockSpec((tm, d), row), pl.BlockSpec((1, d), fix),
            pl.BlockSpec((d, d), fix), pl.BlockSpec((1, d), fix), pl.BlockSpec((1, d), fix),
        ],
        out_specs=pl.BlockSpec((tm, d), row),
        out_shape=jax.ShapeDtypeStruct((seq, d), BF16),
        compiler_params=_params("arbitrary"),
        name="s5_post",
    )(y, u, d_skip.reshape(1, d), w_glu, b_glu.reshape(1, d), g.reshape(1, d))


def _na_bias_table(rpb):
    n_heads = rpb.shape[0]
    cols = np.arange(GRID_W)
    col_start = np.clip(cols - WIN_COLS // 2, 0, GRID_W - WIN_COLS)
    key_cols = np.arange(GRID_W)
    in_win = (key_cols[None, :] >= col_start[:, None]) & (key_cols[None, :] < col_start[:, None] + WIN_COLS)
    dx = key_cols[None, :] - cols[:, None] + (WIN_COLS - 1)
    pick_x = (dx[:, :, None] == np.arange(2 * WIN_COLS - 1)).astype(np.float32)
    halves = []
    for half in range(2):
        pad = ((0, 0), (half * GRID_W, (1 - half) * GRID_W))
        pick = np.pad(pick_x, pad + ((0, 0),))
        keep = np.pad(in_win, pad, constant_values=True)
        b = jnp.einsum('hyx,ckx->hyck', rpb.astype(F32) * LOG2_E, pick, precision=lax.Precision.HIGHEST)
        b = jnp.where(keep[None, None], b, MASK_NEG)
        b = b.reshape(n_heads // HEADS_PER_DOT, HEADS_PER_DOT, b.shape[1], GRID_W, 2 * GRID_W)
        halves.append(jnp.swapaxes(b, 1, 2).reshape(n_heads // HEADS_PER_DOT, b.shape[2],
                                                    HEADS_PER_DOT * GRID_W, 2 * GRID_W))
    return halves


def _na_kernel(q_ref, k_ref, v_ref, b0_ref, b1_ref, g_ref, o_ref, *, rows, rows_per_step):
    n_keys = WIN_ROWS * GRID_W
    d_na = k_ref.shape[-1]
    pw = HEADS_PER_DOT * NA_HEAD_DIM
    row_head = lax.broadcasted_iota(jnp.int32, (HEADS_PER_DOT * GRID_W, pw), 0) // GRID_W
    col_head = lax.broadcasted_iota(jnp.int32, (HEADS_PER_DOT * GRID_W, pw), 1) // NA_HEAD_DIM
    diag = row_head == col_head
    out_head = lax.broadcasted_iota(jnp.int32, (GRID_W, pw), 1) // NA_HEAD_DIM
    first = pl.program_id(0) * rows_per_step
    block_start = jnp.clip(first - WIN_ROWS // 2, 0, rows - k_ref.shape[0])

    def one_row(i, _):
        r = first + i
        win_start = jnp.clip(r - WIN_ROWS // 2, 0, rows - WIN_ROWS)
        dy0 = win_start - r + (WIN_ROWS - 1)
        k = k_ref[pl.ds(win_start - block_start, WIN_ROWS)].reshape(n_keys, d_na)
        v = v_ref[pl.ds(win_start - block_start, WIN_ROWS)].reshape(n_keys, d_na)
        q_rows = pl.ds(pl.multiple_of(i * GRID_W, GRID_W), GRID_W)
        outs = []
        for p in range(d_na // pw):
            sl = slice(p * pw, (p + 1) * pw)
            q4 = q_ref[q_rows, sl]
            qbd = jnp.where(diag, jnp.concatenate([q4] * HEADS_PER_DOT, axis=0), jnp.zeros((), BF16))
            s = lax.dot_general(qbd, k[:, sl], (((1,), (1,)), ((), ())), preferred_element_type=F32)
            s = s + jnp.concatenate([b0_ref[p, dy0 + 2 * j] + b1_ref[p, dy0 + 2 * j + 1]
                                     for j in range(WIN_ROWS // 2)], axis=1)
            m = jnp.max(s, axis=-1, keepdims=True)
            e = jnp.exp2(s - m)
            l = jnp.sum(e, axis=-1, keepdims=True)
            o = jnp.dot(e.astype(BF16), v[:, sl], preferred_element_type=F32) / l
            acc = jnp.zeros((GRID_W, pw), F32)
            for h in range(HEADS_PER_DOT):
                acc = acc + jnp.where(out_head == h, o[h * GRID_W:(h + 1) * GRID_W], 0.0)
            outs.append(acc)
        y = jnp.concatenate(outs, axis=1)
        o_ref[q_rows, :] = _rms(y, g_ref[...]).astype(BF16)
        return 0

    lax.fori_loop(0, rows_per_step, one_row, 0)


def _neighbourhood_attention(qkv, rpb, g):
    seq = qkv.shape[0]
    d_na = qkv.shape[1] // 3
    rows = seq // GRID_W
    bias = _na_bias_table(rpb)
    qkv3 = qkv.reshape(rows, GRID_W, 3 * d_na)

    rps = 4
    key_rows = 2 * WIN_ROWS

    def block_start(b):
        return jnp.clip(b * rps - WIN_ROWS // 2, 0, rows - key_rows)

    window = (pl.Element(key_rows), pl.Element(GRID_W), pl.Element(d_na))

    return pl.pallas_call(
        functools.partial(_na_kernel, rows=rows, rows_per_step=rps),
        grid=(rows // rps,),
        in_specs=[
            pl.BlockSpec((rps * GRID_W, d_na), lambda b: (b, 0)),
            pl.BlockSpec(window, lambda b: (block_start(b), 0, d_na)),
            pl.BlockSpec(window, lambda b: (block_start(b), 0, 2 * d_na)),
            pl.BlockSpec(bias[0].shape, lambda b: (0, 0, 0, 0)),
            pl.BlockSpec(bias[1].shape, lambda b: (0, 0, 0, 0)),
            pl.BlockSpec((1, d_na), lambda b: (0, 0)),
        ],
        out_specs=pl.BlockSpec((rps * GRID_W, d_na), lambda b: (b, 0)),
        out_shape=jax.ShapeDtypeStruct((seq, d_na), BF16),
        compiler_params=_params("arbitrary"),
        name="na",
    )(qkv, qkv3, qkv3, bias[0], bias[1], g.reshape(1, d_na))


def _outproj_kernel(a_ref, b_ref, w_ref, x_ref, o_ref):
    da = a_ref.shape[-1]
    acc = jnp.dot(a_ref[...], w_ref[:da, :].astype(BF16), preferred_element_type=F32)
    acc = acc + jnp.dot(b_ref[...], w_ref[da:, :].astype(BF16), preferred_element_type=F32)
    o_ref[...] = x_ref[...] + acc


def _out_proj(y_ssm, y_na, w_out, x):
    seq, d_model = x.shape
    da, db = y_ssm.shape[1], y_na.shape[1]
    tm = min(2048, seq)
    tn = 512
    return pl.pallas_call(
        _outproj_kernel,
        grid=(seq // tm, d_model // tn),
        in_specs=[
            pl.BlockSpec((tm, da), lambda i, j: (i, 0)),
            pl.BlockSpec((tm, db), lambda i, j: (i, 0)),
            pl.BlockSpec((da + db, tn), lambda i, j: (0, j)),
            pl.BlockSpec((tm, tn), lambda i, j: (i, j)),
        ],
        out_specs=pl.BlockSpec((tm, tn), lambda i, j: (i, j)),
        out_shape=jax.ShapeDtypeStruct((seq, d_model), F32),
        compiler_params=_params("arbitrary", "arbitrary"),
        name="out_proj",
    )(y_ssm, y_na, w_out, x)


def _router_kernel(x_ref, g_ref, wt_ref, h_ref, a_ref):
    h = _rms(x_ref[...], g_ref[...])
    h_hi = h.astype(BF16)
    h_ref[...] = h_hi
    h_lo = (h - h_hi.astype(F32)).astype(BF16)
    w = wt_ref[...]
    w_hi = w.astype(BF16)
    w_lo = (w - w_hi.astype(F32)).astype(BF16)
    n_exp = w.shape[0]
    nt = (((1,), (1,)), ((), ()))
    both = lax.dot_general(jnp.concatenate([w_hi, w_lo], axis=0), h_hi, nt, preferred_element_type=F32)
    logits = both[:n_exp] + both[n_exp:] + lax.dot_general(w_hi, h_lo, nt, preferred_element_type=F32)
    m = jnp.max(logits, axis=0, keepdims=True)
    e = jnp.exp(logits - m)
    a_ref[...] = e / jnp.sum(e, axis=0, keepdims=True)


def _router(x1, g, w_router):
    seq, d_model = x1.shape
    n_exp = w_router.shape[1]
    tm = min(512, seq)
    return pl.pallas_call(
        _router_kernel,
        grid=(seq // tm,),
        in_specs=[
            pl.BlockSpec((tm, d_model), lambda i: (i, 0)),
            pl.BlockSpec((1, d_model), lambda i: (0, 0)),
            pl.BlockSpec((n_exp, d_model), lambda i: (0, 0)),
        ],
        out_specs=[
            pl.BlockSpec((tm, d_model), lambda i: (i, 0)),
            pl.BlockSpec((n_exp, tm), lambda i: (0, i)),
        ],
        out_shape=[
            jax.ShapeDtypeStruct((seq, d_model), BF16),
            jax.ShapeDtypeStruct((n_exp, seq), F32),
        ],
        compiler_params=_params("arbitrary"),
        name="router",
    )(x1, g.reshape(1, d_model), w_router.T)


def _topk_kernel(a_ref, posw_ref, gate_ref, ws_ref, nr_ref, *, cap, blk, win):
    a = a_ref[...]
    n_exp, seq = a.shape
    n_blk = seq // blk
    bits = pltpu.bitcast(a, jnp.int32)

    def bit_step(i, thr):
        cand = thr | jnp.left_shift(jnp.int32(1), 30 - i)
        cnt = jnp.sum((bits >= cand).astype(jnp.int32), axis=-1, keepdims=True)
        return jnp.where(cnt >= cap, cand, thr)

    thr = lax.fori_loop(0, 31, bit_step, jnp.zeros((n_exp, 1), jnp.int32))
    gt = bits > thr
    eq = bits == thr
    need = cap - jnp.sum(gt.astype(jnp.int32), axis=-1, keepdims=True)

    tri = (lax.broadcasted_iota(jnp.int32, (blk, blk), 0)
           <= lax.broadcasted_iota(jnp.int32, (blk, blk), 1)).astype(BF16)
    blk_of_tok = lax.broadcasted_iota(jnp.int32, (seq, n_blk), 0) // blk
    tok_to_blk = (blk_of_tok == lax.broadcasted_iota(jnp.int32, (seq, n_blk), 1)).astype(BF16)
    blk_before = (lax.broadcasted_iota(jnp.int32, (n_blk, n_blk), 0)
                  < lax.broadcasted_iota(jnp.int32, (n_blk, n_blk), 1)).astype(BF16)
    erow = lax.broadcasted_iota(jnp.int32, (2 * n_blk, seq), 0)
    ecol = lax.broadcasted_iota(jnp.int32, (2 * n_blk, seq), 1) // blk
    expand = jnp.where(erow == ecol, 32.0, jnp.where(erow - n_blk == ecol, 1.0, 0.0)).astype(BF16)

    def prefix_counts(mask):
        mb = jnp.where(mask, 1.0, 0.0).astype(BF16)
        local = jnp.concatenate(
            [jnp.dot(mb[:, b * blk:(b + 1) * blk], tri, preferred_element_type=F32) for b in range(n_blk)],
            axis=1)
        per_blk = jnp.dot(mb, tok_to_blk, preferred_element_type=F32)
        start = jnp.dot(per_blk.astype(BF16), blk_before, preferred_element_type=F32)
        hi = jnp.floor(start * (1.0 / 32.0))
        parts = jnp.concatenate([hi, start - 32.0 * hi], axis=1).astype(BF16)
        start_tok = jnp.dot(parts, expand, preferred_element_type=F32)
        return local + start_tok, start, start_tok, per_blk

    eq_incl, _, _, _ = prefix_counts(eq)
    sel = gt | (eq & (eq_incl - 1.0 < need.astype(F32)))
    incl, start, start_tok, per_blk = prefix_counts(sel)

    def window(s):
        return jnp.floor(s * (1.0 / MOE_WIN_ALIGN)) * MOE_WIN_ALIGN

    posw_ref[...] = jnp.where(sel, (incl - 1.0 - window(start_tok)).astype(jnp.int32), -1)
    gate_ref[...] = jnp.where(sel, a, 0.0)
    ws_ref[...] = window(start).astype(jnp.int32)
    span = start - window(start) + per_blk
    rounds = jnp.floor((span + float(win - 1)) * (1.0 / win))
    nr_ref[...] = jnp.max(rounds, axis=0, keepdims=True).astype(jnp.int32)


def _topk(aff_t, cap, blk, win):
    n_exp, seq = aff_t.shape
    n_blk = seq // blk
    full = lambda *_: (0, 0)
    return pl.pallas_call(
        functools.partial(_topk_kernel, cap=cap, blk=blk, win=win),
        grid=(1,),
        in_specs=[pl.BlockSpec((n_exp, seq), full)],
        out_specs=[pl.BlockSpec((n_exp, seq), full), pl.BlockSpec((n_exp, seq), full),
                   pl.BlockSpec((n_exp, n_blk), full), pl.BlockSpec((1, n_blk), full)],
        out_shape=[
            jax.ShapeDtypeStruct((n_exp, seq), jnp.int32),
            jax.ShapeDtypeStruct((n_exp, seq), F32),
            jax.ShapeDtypeStruct((n_exp, n_blk), jnp.int32),
            jax.ShapeDtypeStruct((1, n_blk), jnp.int32),
        ],
        compiler_params=_params("arbitrary"),
        name="topk",
    )(aff_t)


def _window(ws_ref, e, b, r, n_blk, win, cap):
    ws = ws_ref[e * n_blk + b] + r * win
    start = jnp.minimum(ws, cap - win)
    return pl.multiple_of(start, MOE_WIN_ALIGN), ws - start


def _gather_kernel(ws_ref, nr_ref, h_ref, rel_ref, xe_ref, *, blk, win, n_blk):
    n_exp, cap, _ = xe_ref.shape
    xe_ref[...] = jnp.zeros_like(xe_ref)
    slot = lax.broadcasted_iota(jnp.int32, (win, blk), 0)

    def block(b, _):
        rows = h_ref[pl.ds(pl.multiple_of(b * blk, blk), blk), :]
        rel = rel_ref[b]

        def one_round(r, _):
            starts, hots = [], []
            for e in range(n_exp):
                start, shift = _window(ws_ref, e, b, r, n_blk, win, cap)
                relr = rel[e:e + 1, :] - r * win
                key = jnp.where(relr >= 0, relr + shift, -1)
                hots.append(jnp.where(slot == key, 1.0, 0.0).astype(BF16))
                starts.append(start)
            res = jnp.dot(jnp.concatenate(hots, axis=0), rows, preferred_element_type=F32)
            for e in range(n_exp):
                dst = pl.ds(starts[e], win)
                xe_ref[e, dst, :] = (xe_ref[e, dst, :].astype(F32) + res[e * win:(e + 1) * win]).astype(BF16)
            return 0

        lax.fori_loop(0, nr_ref[b], one_round, 0)
        return 0

    lax.fori_loop(0, n_blk, block, 0)


def _moe_gather(ws_flat, n_rounds, h2, rel3, cap, win):
    seq, d_model = h2.shape
    n_blk, n_exp, blk = rel3.shape
    dq = d_model // 4
    grid_spec = pltpu.PrefetchScalarGridSpec(
        num_scalar_prefetch=2,
        grid=(4,),
        in_specs=[
            pl.BlockSpec((seq, dq), lambda c, ws, nr: (0, c)),
            pl.BlockSpec((n_blk, n_exp, blk), lambda c, ws, nr: (0, 0, 0)),
        ],
        out_specs=pl.BlockSpec((n_exp, cap, dq), lambda c, ws, nr: (0, 0, c)),
    )
    return pl.pallas_call(
        functools.partial(_gather_kernel, blk=blk, win=win, n_blk=n_blk),
        grid_spec=grid_spec,
        out_shape=jax.ShapeDtypeStruct((n_exp, cap, d_model), BF16),
        compiler_params=_params("arbitrary"),
        name="moe_gather",
    )(ws_flat, n_rounds, h2, rel3)


def _ffn_kernel(x_ref, wg_ref, wu_ref, wd_ref, y_ref, act_ref, *, n_f):
    s = pl.program_id(1)
    tf = wg_ref.shape[-1]

    @pl.when(s < n_f)
    def _():
        x = x_ref[0]
        g = jnp.dot(x, wg_ref[0].astype(BF16), preferred_element_type=F32)
        u = jnp.dot(x, wu_ref[0].astype(BF16), preferred_element_type=F32)
        act_ref[s] = (g * (1.0 / (1.0 + jnp.exp(-g))) * u).astype(BF16)

    @pl.when(s >= n_f)
    def _():
        acc = jnp.dot(act_ref[0], wd_ref[0, 0:tf, :].astype(BF16), preferred_element_type=F32)
        for f in range(1, n_f):
            acc = acc + jnp.dot(act_ref[f], wd_ref[0, f * tf:(f + 1) * tf, :].astype(BF16),
                                preferred_element_type=F32)
        y_ref[0] = acc.astype(BF16)


def _moe_ffn(xe, w_gate, w_up, w_down):
    n_exp, cap, d_model = xe.shape
    d_ff = w_gate.shape[-1]
    tf = 512
    tn = 512
    n_f, n_n = d_ff // tf, d_model // tn
    up_tile = lambda e, s: (e, 0, jnp.minimum(s, n_f - 1))
    down_tile = lambda e, s: (e, 0, jnp.maximum(s - n_f, 0))
    return pl.pallas_call(
        functools.partial(_ffn_kernel, n_f=n_f),
        grid=(n_exp, n_f + n_n),
        in_specs=[
            pl.BlockSpec((1, cap, d_model), lambda e, s: (e, 0, 0)),
            pl.BlockSpec((1, d_model, tf), up_tile),
            pl.BlockSpec((1, d_model, tf), up_tile),
            pl.BlockSpec((1, d_ff, tn), down_tile),
        ],
        out_specs=pl.BlockSpec((1, cap, tn), down_tile),
        out_shape=jax.ShapeDtypeStruct((n_exp, cap, d_model), BF16),
        scratch_shapes=[pltpu.VMEM((n_f, cap, tf), BF16)],
        compiler_params=_params("arbitrary", "arbitrary"),
        name="moe_ffn",
    )(xe, w_gate, w_up, w_down)


def _combine_kernel(ws_ref, nr_ref, ye_ref, x_ref, rel_ref, gate_ref, o_ref, *, blk, win, n_blk):
    n_exp, cap, _ = ye_ref.shape
    sub = x_ref.shape[0] // blk
    slot = lax.broadcasted_iota(jnp.int32, (win, blk), 0)

    def block(s, _):
        b = pl.program_id(1) * sub + s
        tok = pl.ds(pl.multiple_of(s * blk, blk), blk)
        rel = rel_ref[b]
        gate = gate_ref[b]

        def one_round(r, acc):
            gates, wins = [], []
            for e in range(n_exp):
                start, shift = _window(ws_ref, e, b, r, n_blk, win, cap)
                relr = rel[e:e + 1, :] - r * win
                key = jnp.where(relr >= 0, relr + shift, -1)
                gates.append(jnp.where(slot == key, gate[e:e + 1, :], 0.0).astype(BF16))
                wins.append(ye_ref[e, pl.ds(start, win), :])
            res = lax.dot_general(jnp.concatenate(gates, axis=0), jnp.concatenate(wins, axis=0),
                                  (((0,), (0,)), ((), ())), preferred_element_type=F32)
            return acc + res

        o_ref[tok, :] = lax.fori_loop(0, nr_ref[b], one_round, x_ref[tok, :])
        return 0

    lax.fori_loop(0, sub, block, 0)


def _moe_combine(ws_flat, n_rounds, ye, x1, rel3, gate3, win):
    seq, d_model = x1.shape
    n_exp, cap, _ = ye.shape
    n_blk, _, blk = rel3.shape
    dq = d_model // 4
    tile = min(4, n_blk) * blk
    whole = lambda c, t, ws, nr: (0, 0, 0)
    grid_spec = pltpu.PrefetchScalarGridSpec(
        num_scalar_prefetch=2,
        grid=(4, seq // tile),
        in_specs=[
            pl.BlockSpec((n_exp, cap, dq), lambda c, t, ws, nr: (0, 0, c)),
            pl.BlockSpec((tile, dq), lambda c, t, ws, nr: (t, c)),
            pl.BlockSpec(rel3.shape, whole),
            pl.BlockSpec(gate3.shape, whole),
        ],
        out_specs=pl.BlockSpec((tile, dq), lambda c, t, ws, nr: (t, c)),
    )
    return pl.pallas_call(
        functools.partial(_combine_kernel, blk=blk, win=win, n_blk=n_blk),
        grid_spec=grid_spec,
        out_shape=jax.ShapeDtypeStruct((seq, d_model), F32),
        compiler_params=_params("arbitrary", "arbitrary"),
        name="moe_combine",
    )(ws_flat, n_rounds, ye, x1, rel3, gate3)


def _final_norm_kernel(x_ref, g_ref, o_ref):
    o_ref[...] = _rms(x_ref[...], g_ref[...])


def _final_norm(x, g):
    seq, d_model = x.shape
    tm = min(512, seq)
    return pl.pallas_call(
        _final_norm_kernel,
        grid=(seq // tm,),
        in_specs=[pl.BlockSpec((tm, d_model), lambda i: (i, 0)), pl.BlockSpec((1, d_model), lambda i: (0, 0))],
        out_specs=pl.BlockSpec((tm, d_model), lambda i: (i, 0)),
        out_shape=jax.ShapeDtypeStruct((seq, d_model), F32),
        compiler_params=_params("arbitrary"),
        name="final_norm",
    )(x, g.reshape(1, d_model))


def _layer(x, norm_mix_g, w_in, fwd, bwd, ssm_d, w_glu, b_glu, na_rpb, g_ssm_out, g_na_out, w_out,
           norm_ffn_g, w_router, w_gate, w_up, w_down):
    seq, d_model = x.shape
    d_ssm = ssm_d.shape[0]
    d_na = g_na_out.shape[0]
    n_exp = w_router.shape[1]
    cap = EC_CAPACITY_FACTOR * seq // n_exp
    blk = min(MOE_TOK_BLOCK, cap // 2)
    win = min(MOE_WIN, cap)

    u, qkv = _in_proj(x, norm_mix_g, w_in, d_ssm, d_na)
    y_ssm = _s5_post(_s5_scan(u, fwd, bwd), u, ssm_d, w_glu, b_glu, g_ssm_out)
    y_na = _neighbourhood_attention(qkv, na_rpb, g_na_out)
    x1 = _out_proj(y_ssm, y_na, w_out, x)

    h2, aff_t = _router(x1, norm_ffn_g, w_router)
    rel, gate, ws, n_rounds = _topk(aff_t, cap, blk, win)
    ws_flat = ws.reshape(-1)
    n_rounds = n_rounds.reshape(-1)
    rel3 = jnp.swapaxes(rel.reshape(n_exp, seq // blk, blk), 0, 1)
    gate3 = jnp.swapaxes(gate.reshape(n_exp, seq // blk, blk), 0, 1)
    xe = _moe_gather(ws_flat, n_rounds, h2, rel3, cap, win)
    ye = _moe_ffn(xe, w_gate, w_up, w_down)
    return _moe_combine(ws_flat, n_rounds, ye, x1, rel3, gate3, win)


def kernel(x, norm_mix_g, w_in, a_re_fwd, a_im_fwd, log_dt_fwd, b_re_fwd, b_im_fwd, c_re_fwd, c_im_fwd, a_re_bwd, a_im_bwd, log_dt_bwd, b_re_bwd, b_im_bwd, c_re_bwd, c_im_bwd, ssm_d, w_glu, b_glu, na_rpb, g_ssm_out, g_na_out, w_out, norm_ffn_g, w_router, w_gate, w_up, w_down, norm_final_g):
    bsz = x.shape[0]
    depth = w_in.shape[0]
    outs = []
    for b in range(bsz):
        xb = x[b]
        for l in range(depth):
            fwd = (a_re_fwd[l], a_im_fwd[l], log_dt_fwd[l], b_re_fwd[l], b_im_fwd[l], c_re_fwd[l], c_im_fwd[l])
            bwd = (a_re_bwd[l], a_im_bwd[l], log_dt_bwd[l], b_re_bwd[l], b_im_bwd[l], c_re_bwd[l], c_im_bwd[l])
            xb = _layer(xb, norm_mix_g[l], w_in[l], fwd, bwd, ssm_d[l], w_glu[l], b_glu[l], na_rpb[l],
                        g_ssm_out[l], g_na_out[l], w_out[l], norm_ffn_g[l], w_router[l],
                        w_gate[l], w_up[l], w_down[l])
        outs.append(_final_norm(xb, norm_final_g))
    return jnp.stack(outs)
```

```python
import functools

import numpy as np
import jax
import jax.numpy as jnp
from jax import lax
from jax.experimental import pallas as pl
from jax.experimental.pallas import tpu as pltpu

F32 = jnp.float32
BF16 = jnp.bfloat16

RMS_EPS = 1e-6
SSM_GROUP = 16
SSM_STATE = 64
NA_HEADS = 16
NA_HEAD_DIM = 64
GRID_W = 64
WIN_ROWS = 8
WIN_COLS = 16
N_EXPERTS = 16
EC_CAPACITY_FACTOR = 2

S5_CHUNK = 16
S5_SEGS = 8
HEADS_PER_DOT = 4
MOE_TOK_BLOCK = 256
MOE_WIN_ALIGN = 16
MOE_WIN = 64
MASK_NEG = -1e30
LOG2_E = float(np.log2(np.e))

VMEM_LIMIT_BYTES = 56 * 1024 * 1024


def _params(*semantics):
    return pltpu.CompilerParams(dimension_semantics=semantics, vmem_limit_bytes=VMEM_LIMIT_BYTES)


def _rms(x, g):
    ms = jnp.mean(x * x, axis=-1, keepdims=True)
    return x * lax.rsqrt(ms + RMS_EPS) * g


def _inproj_kernel(x_ref, g_ref, w_ref, u_ref, qkv_ref, h_scr, *, n_u, n_q, q_scale):
    j = pl.program_id(1)

    @pl.when(j == 0)
    def _():
        h_scr[...] = _rms(x_ref[...], g_ref[...]).astype(BF16)

    def project():
        return jnp.dot(h_scr[...], w_ref[...].astype(BF16), preferred_element_type=F32)

    @pl.when(j < n_u)
    def _():
        u_ref[...] = project()

    @pl.when(j >= n_u)
    def _():
        scale = jnp.where(j < n_u + n_q, q_scale, 1.0).astype(F32)
        qkv_ref[...] = (project() * scale).astype(BF16)


def _in_proj(x, g, w_in, d_ssm, d_na):
    seq, d_model = x.shape
    tm = min(1024, seq)
    tn = 1024
    n_u, n_q = d_ssm // tn, d_na // tn
    n_cols = w_in.shape[1] // tn
    kern = functools.partial(_inproj_kernel, n_u=n_u, n_q=n_q, q_scale=NA_HEAD_DIM ** -0.5 * LOG2_E)
    return pl.pallas_call(
        kern,
        grid=(seq // tm, n_cols),
        in_specs=[
            pl.BlockSpec((tm, d_model), lambda i, j: (i, 0)),
            pl.BlockSpec((1, d_model), lambda i, j: (0, 0)),
            pl.BlockSpec((d_model, tn), lambda i, j: (0, j)),
        ],
        out_specs=[
            pl.BlockSpec((tm, tn), lambda i, j: (i, jnp.minimum(j, n_u - 1))),
            pl.BlockSpec((tm, tn), lambda i, j: (i, jnp.maximum(j - n_u, 0))),
        ],
        out_shape=[
            jax.ShapeDtypeStruct((seq, d_ssm), F32),
            jax.ShapeDtypeStruct((seq, 3 * d_na), BF16),
        ],
        scratch_shapes=[pltpu.VMEM((tm, d_model), BF16)],
        compiler_params=_params("arbitrary", "arbitrary"),
        name="in_proj",
    )(x, g.reshape(1, d_model), w_in)


def _s5_direction_tables(a_re, a_im, log_dt, b_re, b_im, c_re, c_im, n_tiles, reverse):
    t_len = S5_CHUNK
    n_ch = b_re.shape[-1]
    hp = lax.Precision.HIGHEST
    dt = jnp.exp(log_dt.astype(F32))[:, None]
    xr, xi = a_re.astype(F32) * dt, a_im.astype(F32) * dt

    def power(xr_, xi_, k):
        k = k.reshape((-1,) + (1,) * xr_.ndim)
        mag = jnp.exp(xr_[None] * k)
        return mag * jnp.cos(xi_[None] * k), mag * jnp.sin(xi_[None] * k)

    pr, pi = power(xr, xi, jnp.arange(t_len + 1, dtype=F32))
    nr, ni = pr[1] - 1.0, pi[1]
    den = a_re * a_re + a_im * a_im
    qr, qi = (nr * a_re + ni * a_im) / den, (ni * a_re - nr * a_im) / den
    bb_r, bb_i = _cmul(qr[:, None, :], qi[:, None, :], jnp.swapaxes(b_re, 1, 2), jnp.swapaxes(b_im, 1, 2))
    cr, ci = c_re.astype(F32), c_im.astype(F32)

    def steps(e):
        return jnp.swapaxes(pr[e], 0, 1), jnp.swapaxes(pi[e], 0, 1)

    n_grp, n_st = a_re.shape
    lags = np.arange(t_len)[::-1] if reverse else np.arange(t_len)
    am_r, am_i = steps(lags)
    y_r, y_i = _cmul(cr[:, None, :, :], ci[:, None, :, :], am_r[:, :, None, :], am_i[:, :, None, :])
    y = jnp.concatenate([y_r, y_i], axis=-1).reshape(n_grp, t_len * n_ch, 2 * n_st)
    lag = jnp.einsum('gck,gqk->gcq', jnp.concatenate([bb_r, -bb_i], axis=-1), y, precision=hp)
    e_in = (np.arange(t_len)) if reverse else (t_len - 1 - np.arange(t_len))
    pe_r, pe_i = steps(e_in)
    st_r, st_i = _cmul(pe_r[:, :, None, :], pe_i[:, :, None, :], bb_r[:, None], bb_i[:, None])
    e_out = (t_len - np.arange(t_len)) if reverse else (np.arange(t_len) + 1)
    po_r, po_i = steps(e_out)
    wr, wi = _cmul(cr[:, None, :, :], ci[:, None, :, :], po_r[:, :, None, :], po_i[:, :, None, :])
    wr = wr.reshape(n_grp, t_len * n_ch, n_st)
    wi = wi.reshape(n_grp, t_len * n_ch, n_st)
    xrf, xif = xr.reshape(-1), xi.reshape(-1)
    tiles = jnp.arange(n_tiles, dtype=F32)
    if reverse:
        tiles = tiles[::-1]
    a_chunk = power(xrf, xif, jnp.full((1,), float(t_len), F32))
    a_seg = power(xrf, xif, jnp.full((1,), float(t_len * n_tiles), F32))
    a_tile = power(xrf, xif, float(t_len) * tiles)
    return lag, (st_r, st_i), (wr, -wi), a_chunk, a_seg, a_tile


def _cmul(ar, ai, br, bi):
    return ar * br - ai * bi, ar * bi + ai * br


def _s5_tables(fwd, bwd, n_tiles):
    tf = _s5_direction_tables(*fwd, n_tiles=n_tiles, reverse=False)
    tb = _s5_direction_tables(*bwd, n_tiles=n_tiles, reverse=True)
    lags = (tf[0], tb[0])
    e_planes = tuple(p.astype(BF16) for p in (tf[1][0], tf[1][1], tb[1][0], tb[1][1]))
    c_planes = tuple(p.astype(BF16) for p in (tf[2][0], tf[2][1], tb[2][0], tb[2][1]))

    ap = jnp.concatenate([tf[3][0], tf[3][1], tb[3][0], tb[3][1], tf[4][0], tf[4][1], tb[4][0], tb[4][1]], axis=0)
    pw = jnp.stack([tf[5][0], tf[5][1], tb[5][0], tb[5][1]])
    return lags, e_planes, c_planes, ap, pw


def _s5_a_kernel(u_ref, lagf_ref, lagb_ref, e0_ref, e1_ref, e2_ref, e3_ref, y_ref, s_ref, perm, w_grp, e_pair, ucat,
                 *, n_tiles, seg_stride):
    e_refs = (e0_ref, e1_ref, e2_ref, e3_ref)
    n_grp = lagf_ref.shape[0]
    cw = S5_CHUNK * SSM_GROUP
    n_plane, n_state = len(e_refs), e0_ref.shape[-1]
    lanes = u_ref.shape[-1]
    pc = n_plane * 2 * n_state

    @pl.when(pl.program_id(0) == 0)
    def _():
        e_pair[...] = jnp.zeros_like(e_pair)
        rl = lax.broadcasted_iota(jnp.int32, (cw, n_grp * cw), 0)
        cl = lax.broadcasted_iota(jnp.int32, (cw, n_grp * cw), 1)
        for blk in range(n_grp):
            row = blk * cw + rl
            tgt = ((row % lanes) // SSM_GROUP) * cw + (row // lanes) * SSM_GROUP + row % SSM_GROUP
            perm[blk * cw:(blk + 1) * cw, :] = jnp.where(cl == tgt, 1.0, 0.0).astype(BF16)

    for g in range(n_grp):
        blank = jnp.zeros((SSM_GROUP, cw), F32)
        strip = jnp.concatenate([blank, lagf_ref[g], lagb_ref[g], blank], axis=1)
        for t in range(S5_CHUNK):
            r0 = t * SSM_GROUP
            fwd_lo = cw - t * SSM_GROUP
            bwd_lo = 2 * cw + (S5_CHUNK - 1 - t) * SSM_GROUP
            piece = strip[:, fwd_lo:fwd_lo + cw] + strip[:, bwd_lo:bwd_lo + cw]
            w_grp[g, r0:r0 + SSM_GROUP, :] = piece.astype(BF16)
            for k in range(n_plane):
                c0 = k * 2 * n_state + (g % 2) * n_state
                e_pair[g // 2, (g % 2) * cw + r0:(g % 2) * cw + r0 + SSM_GROUP, c0:c0 + n_state] = e_refs[k][g, t]

    def gather(jj, _):
        for t in range(S5_CHUNK):
            rows = [u_ref[pl.ds(S5_CHUNK * (2 * jj + h) + t, S5_SEGS, stride=seg_stride), :] for h in range(2)]
            dst = pl.ds(pl.multiple_of(jj * 2 * S5_SEGS, 2 * S5_SEGS), 2 * S5_SEGS)
            ucat[dst, t * lanes:(t + 1) * lanes] = jnp.concatenate(rows, axis=0).astype(BF16)
        return 0

    lax.fori_loop(0, n_tiles // 2, gather, 0)
    ug = jnp.dot(ucat[...], perm[...], preferred_element_type=F32).astype(BF16)
    for g in range(n_grp):
        y_ref[:, g * cw:(g + 1) * cw] = jnp.dot(ug[:, g * cw:(g + 1) * cw], w_grp[g], preferred_element_type=F32)
    for q in range(n_grp // 2):
        s_ref[:, q * pc:(q + 1) * pc] = jnp.dot(ug[:, q * 2 * cw:(q + 1) * 2 * cw], e_pair[q],
                                                preferred_element_type=F32)


def _s5_bc_kernel(s_ref, yi_ref, c0_ref, c1_ref, c2_ref, c3_ref, ap_ref, pw_ref, y_ref, sin, perm, ynat, c_blk, *,
                  n_tiles, seg_stride):
    c_refs = (c0_ref, c1_ref, c2_ref, c3_ref)
    n_plane = len(c_refs)
    n_grp, cw, n_state = c0_ref.shape
    n_pair = n_grp // 2
    pw_ = 2 * n_state
    pc = n_plane * pw_
    lanes = y_ref.shape[-1]

    @pl.when(pl.program_id(0) == 0)
    def _():
        c_blk[...] = jnp.zeros_like(c_blk)
        rl = lax.broadcasted_iota(jnp.int32, (cw, n_grp * cw), 0)
        cl = lax.broadcasted_iota(jnp.int32, (cw, n_grp * cw), 1)
        for g in range(n_grp):
            tgt = (rl // SSM_GROUP) * lanes + g * SSM_GROUP + (rl % SSM_GROUP)
            perm[g * cw:(g + 1) * cw, :] = jnp.where(cl == tgt, 1.0, 0.0).astype(BF16)

    for g in range(n_grp):
        gl = g % 2
        for k in range(n_plane):
            c0 = k * pw_ + gl * n_state
            c_blk[g // 2, gl * cw:(gl + 1) * cw, c0:c0 + n_state] = c_refs[k][g]

    chains = [(q, d) for q in range(n_pair) for d in range(2)]

    def plane_lanes(q, d):
        base = q * pc + d * 2 * pw_
        return slice(base, base + pw_), slice(base + pw_, base + 2 * pw_)

    def coef(row, q):
        return ap_ref[row:row + 1, q * pw_:(q + 1) * pw_]

    def tile_rows(n, d):
        j = n if d == 0 else n_tiles - 1 - n
        return pl.ds(pl.multiple_of(j * S5_SEGS, S5_SEGS), S5_SEGS)

    def step(n, carry):
        out = []
        for (q, d), (zr, zi) in zip(chains, carry):
            re, im = plane_lanes(q, d)
            rows = tile_rows(n, d)
            sin[rows, re] = zr
            sin[rows, im] = zi
            ar, ai = coef(2 * d, q), coef(2 * d + 1, q)
            out.append((ar * zr - ai * zi + s_ref[rows, re], ar * zi + ai * zr + s_ref[rows, im]))
        return tuple(out)

    z0 = jnp.zeros((S5_SEGS, pw_), F32)
    ends = lax.fori_loop(0, n_tiles, step, tuple((z0, z0) for _ in chains))

    carries = []
    for (q, d), (zr, zi) in zip(chains, ends):
        sr, si = coef(4 + 2 * d, q), coef(5 + 2 * d, q)
        cr = jnp.zeros((1, pw_), F32)
        ci = jnp.zeros((1, pw_), F32)
        seg_r = [None] * S5_SEGS
        seg_i = [None] * S5_SEGS
        for s in (range(S5_SEGS) if d == 0 else range(S5_SEGS - 1, -1, -1)):
            seg_r[s], seg_i[s] = cr, ci
            cr, ci = (zr[s:s + 1] + sr * cr - si * ci, zi[s:s + 1] + sr * ci + si * cr)
        carries.append((jnp.concatenate(seg_r, axis=0), jnp.concatenate(seg_i, axis=0)))

    def fix(n8, _):
        tiles = pl.ds(pl.multiple_of(n8 * 8, 8), 8)
        for (q, d), (car_r, car_i) in zip(chains, carries):
            re, im = plane_lanes(q, d)
            pr8 = pw_ref[2 * d, tiles, q * pw_:(q + 1) * pw_]
            pi8 = pw_ref[2 * d + 1, tiles, q * pw_:(q + 1) * pw_]
            for r in range(8):
                rows = pl.ds(pl.multiple_of((n8 * 8 + r) * S5_SEGS, S5_SEGS), S5_SEGS)
                pr, pi = pr8[r:r + 1], pi8[r:r + 1]
                sin[rows, re] = sin[rows, re] + (pr * car_r - pi * car_i)
                sin[rows, im] = sin[rows, im] + (pr * car_i + pi * car_r)
        return 0

    lax.fori_loop(0, n_tiles // 8, fix, 0)

    parts = []
    for q in range(n_pair):
        lhs = sin[:, q * pc:(q + 1) * pc].astype(BF16)
        carried = lax.dot_general(lhs, c_blk[q], (((1,), (1,)), ((), ())), preferred_element_type=F32)
        parts.append(yi_ref[:, q * 2 * cw:(q + 1) * 2 * cw] + carried)
    y_chunk = jnp.concatenate(parts, axis=1).astype(BF16)
    ynat[...] = jnp.dot(y_chunk, perm[...], preferred_element_type=F32)

    def scatter(j, _):
        src = pl.ds(pl.multiple_of(j * S5_SEGS, S5_SEGS), S5_SEGS)
        for t in range(S5_CHUNK):
            y_ref[pl.ds(S5_CHUNK * j + t, S5_SEGS, stride=seg_stride), :] = ynat[src, t * lanes:(t + 1) * lanes]
        return 0

    lax.fori_loop(0, n_tiles, scatter, 0)


def _s5_scan(u, fwd, bwd):
    seq, d_ssm = u.shape
    n_groups = d_ssm // SSM_GROUP
    n_state = SSM_STATE
    cw = S5_CHUNK * SSM_GROUP
    n_rows = seq // S5_CHUNK
    n_tiles = n_rows // S5_SEGS
    lanes = 128
    gpb = lanes // SSM_GROUP
    n_blocks = d_ssm // lanes
    bw = gpb * cw
    sw = gpb * 4 * n_state
    seg_stride = n_tiles * S5_CHUNK

    lags, e_planes, c_planes, ap, pw = _s5_tables(fwd, bwd, n_tiles)
    grp3 = lambda i: (i, 0, 0)
    grp4 = lambda i: (i, 0, 0, 0)

    y_intra, s_loc = pl.pallas_call(
        functools.partial(_s5_a_kernel, n_tiles=n_tiles, seg_stride=seg_stride),
        grid=(n_blocks,),
        in_specs=[
            pl.BlockSpec((seq, lanes), lambda i: (0, i)),
            *[pl.BlockSpec((gpb,) + t.shape[1:], grp3) for t in lags],
            *[pl.BlockSpec((gpb,) + t.shape[1:], grp4) for t in e_planes],
        ],
        out_specs=[
            pl.BlockSpec((n_rows, bw), lambda i: (0, i)),
            pl.BlockSpec((n_rows, sw), lambda i: (0, i)),
        ],
        out_shape=[
            jax.ShapeDtypeStruct((n_rows, n_blocks * bw), F32),
            jax.ShapeDtypeStruct((n_rows, n_blocks * sw), F32),
        ],
        scratch_shapes=[
            pltpu.VMEM((bw, bw), BF16),
            pltpu.VMEM((gpb, cw, cw), BF16),
            pltpu.VMEM((gpb // 2, 2 * cw, 8 * n_state), BF16),
            pltpu.VMEM((n_rows, S5_CHUNK * lanes), BF16),
        ],
        compiler_params=_params("arbitrary"),
        name="s5_a",
    )(u, *lags, *e_planes)

    pl_lanes = gpb * n_state
    return pl.pallas_call(
        functools.partial(_s5_bc_kernel, n_tiles=n_tiles, seg_stride=seg_stride),
        grid=(n_blocks,),
        in_specs=[
            pl.BlockSpec((n_rows, sw), lambda i: (0, i)),
            pl.BlockSpec((n_rows, bw), lambda i: (0, i)),
            *[pl.BlockSpec((gpb,) + t.shape[1:], grp3) for t in c_planes],
            pl.BlockSpec((8, pl_lanes), lambda i: (0, i)),
            pl.BlockSpec((4, n_tiles, pl_lanes), lambda i: (0, 0, i)),
        ],
        out_specs=pl.BlockSpec((seq, lanes), lambda i: (0, i)),
        out_shape=jax.ShapeDtypeStruct((seq, d_ssm), F32),
        scratch_shapes=[
            pltpu.VMEM((n_rows, sw), F32),
            pltpu.VMEM((bw, bw), BF16),
            pltpu.VMEM((n_rows, bw), F32),
            pltpu.VMEM((gpb // 2, 2 * cw, 8 * n_state), BF16),
        ],
        compiler_params=_params("arbitrary"),
        name="s5_bc",
    )(s_loc, y_intra, *c_planes, ap, pw)


def _s5_post_kernel(y_ref, u_ref, d_ref, w_ref, b_ref, g_ref, o_ref):
    y = y_ref[...] + d_ref[...] * u_ref[...]
    c0 = np.float32(np.sqrt(2.0 / np.pi))
    y = 0.5 * y * (1.0 + jnp.tanh(c0 * (y + np.float32(0.044715) * (y * y * y))))
    z = jnp.dot(y.astype(BF16), w_ref[...].astype(BF16), preferred_element_type=F32) + b_ref[...]
    o = y * (1.0 / (1.0 + jnp.exp(-z)))
    o_ref[...] = _rms(o, g_ref[...]).astype(BF16)


def _s5_post(y, u, d_skip, w_glu, b_glu, g):
    seq, d = y.shape
    tm = min(512, seq)
    row = lambda i: (i, 0)
    fix = lambda i: (0, 0)
    return pl.pallas_call(
        _s5_post_kernel,
        grid=(seq // tm,),
        in_specs=[
            pl.BlockSpec((tm, d), row), pl.BlockSpec((tm, d), row), pl.BlockSpec((1, d), fix),
            pl.BlockSpec((d, d), fix), pl.BlockSpec((1, d), fix), pl.BlockSpec((1, d), fix),
        ],
        out_specs=pl.BlockSpec((tm, d), row),
        out_shape=jax.ShapeDtypeStruct((seq, d), BF16),
        compiler_params=_params("arbitrary"),
        name="s5_post",
    )(y, u, d_skip.reshape(1, d), w_glu, b_glu.reshape(1, d), g.reshape(1, d))


def _na_bias_table(rpb):
    n_heads = rpb.shape[0]
    cols = np.arange(GRID_W)
    col_start = np.clip(cols - WIN_COLS // 2, 0, GRID_W - WIN_COLS)
    key_cols = np.arange(GRID_W)
    in_win = (key_cols[None, :] >= col_start[:, None]) & (key_cols[None, :] < col_start[:, None] + WIN_COLS)
    dx = key_cols[None, :] - cols[:, None] + (WIN_COLS - 1)
    pick_x = (dx[:, :, None] == np.arange(2 * WIN_COLS - 1)).astype(np.float32)
    halves = []
    for half in range(2):
        pad = ((0, 0), (half * GRID_W, (1 - half) * GRID_W))
        pick = np.pad(pick_x, pad + ((0, 0),))
        keep = np.pad(in_win, pad, constant_values=True)
        b = jnp.einsum('hyx,ckx->hyck', rpb.astype(F32) * LOG2_E, pick, precision=lax.Precision.HIGHEST)
        b = jnp.where(keep[None, None], b, MASK_NEG)
        b = b.reshape(n_heads // HEADS_PER_DOT, HEADS_PER_DOT, b.shape[1], GRID_W, 2 * GRID_W)
        halves.append(jnp.swapaxes(b, 1, 2).reshape(n_heads // HEADS_PER_DOT, b.shape[2],
                                                    HEADS_PER_DOT * GRID_W, 2 * GRID_W))
    return halves


def _na_kernel(q_ref, k_ref, v_ref, b0_ref, b1_ref, g_ref, o_ref, *, rows, rows_per_step):
    n_keys = WIN_ROWS * GRID_W
    d_na = k_ref.shape[-1]
    pw = HEADS_PER_DOT * NA_HEAD_DIM
    row_head = lax.broadcasted_iota(jnp.int32, (HEADS_PER_DOT * GRID_W, pw), 0) // GRID_W
    col_head = lax.broadcasted_iota(jnp.int32, (HEADS_PER_DOT * GRID_W, pw), 1) // NA_HEAD_DIM
    diag = row_head == col_head
    out_head = lax.broadcasted_iota(jnp.int32, (GRID_W, pw), 1) // NA_HEAD_DIM
    first = pl.program_id(0) * rows_per_step
    block_start = jnp.clip(first - WIN_ROWS // 2, 0, rows - k_ref.shape[0])

    def one_row(i, _):
        r = first + i
        win_start = jnp.clip(r - WIN_ROWS // 2, 0, rows - WIN_ROWS)
        dy0 = win_start - r + (WIN_ROWS - 1)
        k = k_ref[pl.ds(win_start - block_start, WIN_ROWS)].reshape(n_keys, d_na)
        v = v_ref[pl.ds(win_start - block_start, WIN_ROWS)].reshape(n_keys, d_na)
        q_rows = pl.ds(pl.multiple_of(i * GRID_W, GRID_W), GRID_W)
        outs = []
        for p in range(d_na // pw):
            sl = slice(p * pw, (p + 1) * pw)
            q4 = q_ref[q_rows, sl]
            qbd = jnp.where(diag, jnp.concatenate([q4] * HEADS_PER_DOT, axis=0), jnp.zeros((), BF16))
            s = lax.dot_general(qbd, k[:, sl], (((1,), (1,)), ((), ())), preferred_element_type=F32)
            s = s + jnp.concatenate([b0_ref[p, dy0 + 2 * j] + b1_ref[p, dy0 + 2 * j + 1]
                                     for j in range(WIN_ROWS // 2)], axis=1)
            m = jnp.max(s, axis=-1, keepdims=True)
            e = jnp.exp2(s - m)
            l = jnp.sum(e, axis=-1, keepdims=True)
            o = jnp.dot(e.astype(BF16), v[:, sl], preferred_element_type=F32) / l
            acc = jnp.zeros((GRID_W, pw), F32)
            for h in range(HEADS_PER_DOT):
                acc = acc + jnp.where(out_head == h, o[h * GRID_W:(h + 1) * GRID_W], 0.0)
            outs.append(acc)
        y = jnp.concatenate(outs, axis=1)
        o_ref[q_rows, :] = _rms(y, g_ref[...]).astype(BF16)
        return 0

    lax.fori_loop(0, rows_per_step, one_row, 0)


def _neighbourhood_attention(qkv, rpb, g):
    seq = qkv.shape[0]
    d_na = qkv.shape[1] // 3
    rows = seq // GRID_W
    bias = _na_bias_table(rpb)
    qkv3 = qkv.reshape(rows, GRID_W, 3 * d_na)

    rps = 4
    key_rows = 2 * WIN_ROWS

    def block_start(b):
        return jnp.clip(b * rps - WIN_ROWS // 2, 0, rows - key_rows)

    window = (pl.Element(key_rows), pl.Element(GRID_W), pl.Element(d_na))

    return pl.pallas_call(
        functools.partial(_na_kernel, rows=rows, rows_per_step=rps),
        grid=(rows // rps,),
        in_specs=[
            pl.BlockSpec((rps * GRID_W, d_na), lambda b: (b, 0)),
            pl.BlockSpec(window, lambda b: (block_start(b), 0, d_na)),
            pl.BlockSpec(window, lambda b: (block_start(b), 0, 2 * d_na)),
            pl.BlockSpec(bias[0].shape, lambda b: (0, 0, 0, 0)),
            pl.BlockSpec(bias[1].shape, lambda b: (0, 0, 0, 0)),
            pl.BlockSpec((1, d_na), lambda b: (0, 0)),
        ],
        out_specs=pl.BlockSpec((rps * GRID_W, d_na), lambda b: (b, 0)),
        out_shape=jax.ShapeDtypeStruct((seq, d_na), BF16),
        compiler_params=_params("arbitrary"),
        name="na",
    )(qkv, qkv3, qkv3, bias[0], bias[1], g.reshape(1, d_na))


def _outproj_kernel(a_ref, b_ref, w_ref, x_ref, o_ref):
    da = a_ref.shape[-1]
    acc = jnp.dot(a_ref[...], w_ref[:da, :].astype(BF16), preferred_element_type=F32)
    acc = acc + jnp.dot(b_ref[...], w_ref[da:, :].astype(BF16), preferred_element_type=F32)
    o_ref[...] = x_ref[...] + acc


def _out_proj(y_ssm, y_na, w_out, x):
    seq, d_model = x.shape
    da, db = y_ssm.shape[1], y_na.shape[1]
    tm = min(2048, seq)
    tn = 512
    return pl.pallas_call(
        _outproj_kernel,
        grid=(seq // tm, d_model // tn),
        in_specs=[
            pl.BlockSpec((tm, da), lambda i, j: (i, 0)),
            pl.BlockSpec((tm, db), lambda i, j: (i, 0)),
            pl.BlockSpec((da + db, tn), lambda i, j: (0, j)),
            pl.BlockSpec((tm, tn), lambda i, j: (i, j)),
        ],
        out_specs=pl.BlockSpec((tm, tn), lambda i, j: (i, j)),
        out_shape=jax.ShapeDtypeStruct((seq, d_model), F32),
        compiler_params=_params("arbitrary", "arbitrary"),
        name="out_proj",
    )(y_ssm, y_na, w_out, x)


def _router_kernel(x_ref, g_ref, wt_ref, h_ref, a_ref):
    h = _rms(x_ref[...], g_ref[...])
    h_hi = h.astype(BF16)
    h_ref[...] = h_hi
    h_lo = (h - h_hi.astype(F32)).astype(BF16)
    w = wt_ref[...]
    w_hi = w.astype(BF16)
    w_lo = (w - w_hi.astype(F32)).astype(BF16)
    n_exp = w.shape[0]
    nt = (((1,), (1,)), ((), ()))
    both = lax.dot_general(jnp.concatenate([w_hi, w_lo], axis=0), h_hi, nt, preferred_element_type=F32)
    logits = both[:n_exp] + both[n_exp:] + lax.dot_general(w_hi, h_lo, nt, preferred_element_type=F32)
    m = jnp.max(logits, axis=0, keepdims=True)
    e = jnp.exp(logits - m)
    a_ref[...] = e / jnp.sum(e, axis=0, keepdims=True)


def _router(x1, g, w_router):
    seq, d_model = x1.shape
    n_exp = w_router.shape[1]
    tm = min(512, seq)
    return pl.pallas_call(
        _router_kernel,
        grid=(seq // tm,),
        in_specs=[
            pl.BlockSpec((tm, d_model), lambda i: (i, 0)),
            pl.BlockSpec((1, d_model), lambda i: (0, 0)),
            pl.BlockSpec((n_exp, d_model), lambda i: (0, 0)),
        ],
        out_specs=[
            pl.BlockSpec((tm, d_model), lambda i: (i, 0)),
            pl.BlockSpec((n_exp, tm), lambda i: (0, i)),
        ],
        out_shape=[
            jax.ShapeDtypeStruct((seq, d_model), BF16),
            jax.ShapeDtypeStruct((n_exp, seq), F32),
        ],
        compiler_params=_params("arbitrary"),
        name="router",
    )(x1, g.reshape(1, d_model), w_router.T)


def _topk_kernel(a_ref, posw_ref, gate_ref, ws_ref, nr_ref, *, cap, blk, win):
    a = a_ref[...]
    n_exp, seq = a.shape
    n_blk = seq // blk
    bits = pltpu.bitcast(a, jnp.int32)

    def bit_step(i, thr):
        cand = thr | jnp.left_shift(jnp.int32(1), 30 - i)
        cnt = jnp.sum((bits >= cand).astype(jnp.int32), axis=-1, keepdims=True)
        return jnp.where(cnt >= cap, cand, thr)

    thr = lax.fori_loop(0, 31, bit_step, jnp.zeros((n_exp, 1), jnp.int32))
    gt = bits > thr
    eq = bits == thr
    need = cap - jnp.sum(gt.astype(jnp.int32), axis=-1, keepdims=True)

    tri = (lax.broadcasted_iota(jnp.int32, (blk, blk), 0)
           <= lax.broadcasted_iota(jnp.int32, (blk, blk), 1)).astype(BF16)
    blk_of_tok = lax.broadcasted_iota(jnp.int32, (seq, n_blk), 0) // blk
    tok_to_blk = (blk_of_tok == lax.broadcasted_iota(jnp.int32, (seq, n_blk), 1)).astype(BF16)
    blk_before = (lax.broadcasted_iota(jnp.int32, (n_blk, n_blk), 0)
                  < lax.broadcasted_iota(jnp.int32, (n_blk, n_blk), 1)).astype(BF16)
    erow = lax.broadcasted_iota(jnp.int32, (2 * n_blk, seq), 0)
    ecol = lax.broadcasted_iota(jnp.int32, (2 * n_blk, seq), 1) // blk
    expand = jnp.where(erow == ecol, 32.0, jnp.where(erow - n_blk == ecol, 1.0, 0.0)).astype(BF16)

    def prefix_counts(mask):
        mb = jnp.where(mask, 1.0, 0.0).astype(BF16)
        local = jnp.concatenate(
            [jnp.dot(mb[:, b * blk:(b + 1) * blk], tri, preferred_element_type=F32) for b in range(n_blk)],
            axis=1)
        per_blk = jnp.dot(mb, tok_to_blk, preferred_element_type=F32)
        start = jnp.dot(per_blk.astype(BF16), blk_before, preferred_element_type=F32)
        hi = jnp.floor(start * (1.0 / 32.0))
        parts = jnp.concatenate([hi, start - 32.0 * hi], axis=1).astype(BF16)
        start_tok = jnp.dot(parts, expand, preferred_element_type=F32)
        return local + start_tok, start, start_tok, per_blk

    eq_incl, _, _, _ = prefix_counts(eq)
    sel = gt | (eq & (eq_incl - 1.0 < need.astype(F32)))
    incl, start, start_tok, per_blk = prefix_counts(sel)

    def window(s):
        return jnp.floor(s * (1.0 / MOE_WIN_ALIGN)) * MOE_WIN_ALIGN

    posw_ref[...] = jnp.where(sel, (incl - 1.0 - window(start_tok)).astype(jnp.int32), -1)
    gate_ref[...] = jnp.where(sel, a, 0.0)
    ws_ref[...] = window(start).astype(jnp.int32)
    span = start - window(start) + per_blk
    rounds = jnp.floor((span + float(win - 1)) * (1.0 / win))
    nr_ref[...] = jnp.max(rounds, axis=0, keepdims=True).astype(jnp.int32)


def _topk(aff_t, cap, blk, win):
    n_exp, seq = aff_t.shape
    n_blk = seq // blk
    full = lambda *_: (0, 0)
    return pl.pallas_call(
        functools.partial(_topk_kernel, cap=cap, blk=blk, win=win),
        grid=(1,),
        in_specs=[pl.BlockSpec((n_exp, seq), full)],
        out_specs=[pl.BlockSpec((n_exp, seq), full), pl.BlockSpec((n_exp, seq), full),
                   pl.BlockSpec((n_exp, n_blk), full), pl.BlockSpec((1, n_blk), full)],
        out_shape=[
            jax.ShapeDtypeStruct((n_exp, seq), jnp.int32),
            jax.ShapeDtypeStruct((n_exp, seq), F32),
            jax.ShapeDtypeStruct((n_exp, n_blk), jnp.int32),
            jax.ShapeDtypeStruct((1, n_blk), jnp.int32),
        ],
        compiler_params=_params("arbitrary"),
        name="topk",
    )(aff_t)


def _window(ws_ref, e, b, r, n_blk, win, cap):
    ws = ws_ref[e * n_blk + b] + r * win
    start = jnp.minimum(ws, cap - win)
    return pl.multiple_of(start, MOE_WIN_ALIGN), ws - start


def _gather_kernel(ws_ref, nr_ref, h_ref, rel_ref, xe_ref, *, blk, win, n_blk):
    n_exp, cap, _ = xe_ref.shape
    xe_ref[...] = jnp.zeros_like(xe_ref)
    slot = lax.broadcasted_iota(jnp.int32, (win, blk), 0)

    def block(b, _):
        rows = h_ref[pl.ds(pl.multiple_of(b * blk, blk), blk), :]
        rel = rel_ref[b]

        def one_round(r, _):
            starts, hots = [], []
            for e in range(n_exp):
                start, shift = _window(ws_ref, e, b, r, n_blk, win, cap)
                relr = rel[e:e + 1, :] - r * win
                key = jnp.where(relr >= 0, relr + shift, -1)
                hots.append(jnp.where(slot == key, 1.0, 0.0).astype(BF16))
                starts.append(start)
            res = jnp.dot(jnp.concatenate(hots, axis=0), rows, preferred_element_type=F32)
            for e in range(n_exp):
                dst = pl.ds(starts[e], win)
                xe_ref[e, dst, :] = (xe_ref[e, dst, :].astype(F32) + res[e * win:(e + 1) * win]).astype(BF16)
            return 0

        lax.fori_loop(0, nr_ref[b], one_round, 0)
        return 0

    lax.fori_loop(0, n_blk, block, 0)


def _moe_gather(ws_flat, n_rounds, h2, rel3, cap, win):
    seq, d_model = h2.shape
    n_blk, n_exp, blk = rel3.shape
    dq = d_model // 4
    grid_spec = pltpu.PrefetchScalarGridSpec(
        num_scalar_prefetch=2,
        grid=(4,),
        in_specs=[
            pl.BlockSpec((seq, dq), lambda c, ws, nr: (0, c)),
            pl.BlockSpec((n_blk, n_exp, blk), lambda c, ws, nr: (0, 0, 0)),
        ],
        out_specs=pl.BlockSpec((n_exp, cap, dq), lambda c, ws, nr: (0, 0, c)),
    )
    return pl.pallas_call(
        functools.partial(_gather_kernel, blk=blk, win=win, n_blk=n_blk),
        grid_spec=grid_spec,
        out_shape=jax.ShapeDtypeStruct((n_exp, cap, d_model), BF16),
        compiler_params=_params("arbitrary"),
        name="moe_gather",
    )(ws_flat, n_rounds, h2, rel3)


def _ffn_kernel(x_ref, wg_ref, wu_ref, wd_ref, y_ref, act_ref, *, n_f):
    s = pl.program_id(1)
    tf = wg_ref.shape[-1]

    @pl.when(s < n_f)
    def _():
        x = x_ref[0]
        g = jnp.dot(x, wg_ref[0].astype(BF16), preferred_element_type=F32)
        u = jnp.dot(x, wu_ref[0].astype(BF16), preferred_element_type=F32)
        act_ref[s] = (g * (1.0 / (1.0 + jnp.exp(-g))) * u).astype(BF16)

    @pl.when(s >= n_f)
    def _():
        acc = jnp.dot(act_ref[0], wd_ref[0, 0:tf, :].astype(BF16), preferred_element_type=F32)
        for f in range(1, n_f):
            acc = acc + jnp.dot(act_ref[f], wd_ref[0, f * tf:(f + 1) * tf, :].astype(BF16),
                                preferred_element_type=F32)
        y_ref[0] = acc.astype(BF16)


def _moe_ffn(xe, w_gate, w_up, w_down):
    n_exp, cap, d_model = xe.shape
    d_ff = w_gate.shape[-1]
    tf = 512
    tn = 1024
    n_f, n_n = d_ff // tf, d_model // tn
    up_tile = lambda e, s: (e, 0, jnp.minimum(s, n_f - 1))
    down_tile = lambda e, s: (e, 0, jnp.maximum(s - n_f, 0))
    return pl.pallas_call(
        functools.partial(_ffn_kernel, n_f=n_f),
        grid=(n_exp, n_f + n_n),
        in_specs=[
            pl.BlockSpec((1, cap, d_model), lambda e, s: (e, 0, 0)),
            pl.BlockSpec((1, d_model, tf), up_tile),
            pl.BlockSpec((1, d_model, tf), up_tile),
            pl.BlockSpec((1, d_ff, tn), down_tile),
        ],
        out_specs=pl.BlockSpec((1, cap, tn), down_tile),
        out_shape=jax.ShapeDtypeStruct((n_exp, cap, d_model), BF16),
        scratch_shapes=[pltpu.VMEM((n_f, cap, tf), BF16)],
        compiler_params=_params("arbitrary", "arbitrary"),
        name="moe_ffn",
    )(xe, w_gate, w_up, w_down)


def _combine_kernel(ws_ref, nr_ref, ye_ref, x_ref, rel_ref, gate_ref, o_ref, *, blk, win, n_blk):
    n_exp, cap, _ = ye_ref.shape
    sub = x_ref.shape[0] // blk
    slot = lax.broadcasted_iota(jnp.int32, (win, blk), 0)

    def block(s, _):
        b = pl.program_id(1) * sub + s
        tok = pl.ds(pl.multiple_of(s * blk, blk), blk)
        rel = rel_ref[b]
        gate = gate_ref[b]

        def one_round(r, acc):
            gates, wins = [], []
            for e in range(n_exp):
                start, shift = _window(ws_ref, e, b, r, n_blk, win, cap)
                relr = rel[e:e + 1, :] - r * win
                key = jnp.where(relr >= 0, relr + shift, -1)
                gates.append(jnp.where(slot == key, gate[e:e + 1, :], 0.0).astype(BF16))
                wins.append(ye_ref[e, pl.ds(start, win), :])
            res = lax.dot_general(jnp.concatenate(gates, axis=0), jnp.concatenate(wins, axis=0),
                                  (((0,), (0,)), ((), ())), preferred_element_type=F32)
            return acc + res

        o_ref[tok, :] = lax.fori_loop(0, nr_ref[b], one_round, x_ref[tok, :])
        return 0

    lax.fori_loop(0, sub, block, 0)


def _moe_combine(ws_flat, n_rounds, ye, x1, rel3, gate3, win):
    seq, d_model = x1.shape
    n_exp, cap, _ = ye.shape
    n_blk, _, blk = rel3.shape
    dq = d_model // 4
    tile = min(4, n_blk) * blk
    whole = lambda c, t, ws, nr: (0, 0, 0)
    grid_spec = pltpu.PrefetchScalarGridSpec(
        num_scalar_prefetch=2,
        grid=(4, seq // tile),
        in_specs=[
            pl.BlockSpec((n_exp, cap, dq), lambda c, t, ws, nr: (0, 0, c)),
            pl.BlockSpec((tile, dq), lambda c, t, ws, nr: (t, c)),
            pl.BlockSpec(rel3.shape, whole),
            pl.BlockSpec(gate3.shape, whole),
        ],
        out_specs=pl.BlockSpec((tile, dq), lambda c, t, ws, nr: (t, c)),
    )
    return pl.pallas_call(
        functools.partial(_combine_kernel, blk=blk, win=win, n_blk=n_blk),
        grid_spec=grid_spec,
        out_shape=jax.ShapeDtypeStruct((seq, d_model), F32),
        compiler_params=_params("arbitrary", "arbitrary"),
        name="moe_combine",
    )(ws_flat, n_rounds, ye, x1, rel3, gate3)


def _final_norm_kernel(x_ref, g_ref, o_ref):
    o_ref[...] = _rms(x_ref[...], g_ref[...])


def _final_norm(x, g):
    seq, d_model = x.shape
    tm = min(512, seq)
    return pl.pallas_call(
        _final_norm_kernel,
        grid=(seq // tm,),
        in_specs=[pl.BlockSpec((tm, d_model), lambda i: (i, 0)), pl.BlockSpec((1, d_model), lambda i: (0, 0))],
        out_specs=pl.BlockSpec((tm, d_model), lambda i: (i, 0)),
        out_shape=jax.ShapeDtypeStruct((seq, d_model), F32),
        compiler_params=_params("arbitrary"),
        name="final_norm",
    )(x, g.reshape(1, d_model))


def _layer(x, norm_mix_g, w_in, fwd, bwd, ssm_d, w_glu, b_glu, na_rpb, g_ssm_out, g_na_out, w_out,
           norm_ffn_g, w_router, w_gate, w_up, w_down):
    seq, d_model = x.shape
    d_ssm = ssm_d.shape[0]
    d_na = g_na_out.shape[0]
    n_exp = w_router.shape[1]
    cap = EC_CAPACITY_FACTOR * seq // n_exp
    blk = min(MOE_TOK_BLOCK, cap // 2)
    win = min(MOE_WIN, cap)

    u, qkv = _in_proj(x, norm_mix_g, w_in, d_ssm, d_na)
    y_ssm = _s5_post(_s5_scan(u, fwd, bwd), u, ssm_d, w_glu, b_glu, g_ssm_out)
    y_na = _neighbourhood_attention(qkv, na_rpb, g_na_out)
    x1 = _out_proj(y_ssm, y_na, w_out, x)

    h2, aff_t = _router(x1, norm_ffn_g, w_router)
    rel, gate, ws, n_rounds = _topk(aff_t, cap, blk, win)
    ws_flat = ws.reshape(-1)
    n_rounds = n_rounds.reshape(-1)
    rel3 = jnp.swapaxes(rel.reshape(n_exp, seq // blk, blk), 0, 1)
    gate3 = jnp.swapaxes(gate.reshape(n_exp, seq // blk, blk), 0, 1)
    xe = _moe_gather(ws_flat, n_rounds, h2, rel3, cap, win)
    ye = _moe_ffn(xe, w_gate, w_up, w_down)
    return _moe_combine(ws_flat, n_rounds, ye, x1, rel3, gate3, win)


def kernel(x, norm_mix_g, w_in, a_re_fwd, a_im_fwd, log_dt_fwd, b_re_fwd, b_im_fwd, c_re_fwd, c_im_fwd, a_re_bwd, a_im_bwd, log_dt_bwd, b_re_bwd, b_im_bwd, c_re_bwd, c_im_bwd, ssm_d, w_glu, b_glu, na_rpb, g_ssm_out, g_na_out, w_out, norm_ffn_g, w_router, w_gate, w_up, w_down, norm_final_g):
    bsz = x.shape[0]
    depth = w_in.shape[0]
    outs = []
    for b in range(bsz):
        xb = x[b]
        for l in range(depth):
            fwd = (a_re_fwd[l], a_im_fwd[l], log_dt_fwd[l], b_re_fwd[l], b_im_fwd[l], c_re_fwd[l], c_im_fwd[l])
            bwd = (a_re_bwd[l], a_im_bwd[l], log_dt_bwd[l], b_re_bwd[l], b_im_bwd[l], c_re_bwd[l], c_im_bwd[l])
            xb = _layer(xb, norm_mix_g[l], w_in[l], fwd, bwd, ssm_d[l], w_glu[l], b_glu[l], na_rpb[l],
                        g_ssm_out[l], g_na_out[l], w_out[l], norm_ffn_g[l], w_router[l],
                        w_gate[l], w_up[l], w_down[l])
        outs.append(_final_norm(xb, norm_final_g))
    return jnp.stack(outs)
```

```python
import functools

import numpy as np
import jax
import jax.numpy as jnp
from jax import lax
from jax.experimental import pallas as pl
from jax.experimental.pallas import tpu as pltpu

F32 = jnp.float32
BF16 = jnp.bfloat16

RMS_EPS = 1e-6
SSM_GROUP = 16
SSM_STATE = 64
NA_HEADS = 16
NA_HEAD_DIM = 64
GRID_W = 64
WIN_ROWS = 8
WIN_COLS = 16
N_EXPERTS = 16
EC_CAPACITY_FACTOR = 2

S5_CHUNK = 16
S5_SEGS = 8
HEADS_PER_DOT = 4
MOE_TOK_BLOCK = 256
MOE_WIN_ALIGN = 16
MOE_WIN = 64
MASK_NEG = -1e30
LOG2_E = float(np.log2(np.e))

VMEM_LIMIT_BYTES = 56 * 1024 * 1024


def _params(*semantics):
    return pltpu.CompilerParams(dimension_semantics=semantics, vmem_limit_bytes=VMEM_LIMIT_BYTES)


def _rms(x, g):
    ms = jnp.mean(x * x, axis=-1, keepdims=True)
    return x * lax.rsqrt(ms + RMS_EPS) * g


def _inproj_kernel(x_ref, g_ref, w_ref, u_ref, qkv_ref, h_scr, *, n_u, n_q, q_scale):
    j = pl.program_id(1)

    @pl.when(j == 0)
    def _():
        h_scr[...] = _rms(x_ref[...], g_ref[...]).astype(BF16)

    def project():
        return jnp.dot(h_scr[...], w_ref[...].astype(BF16), preferred_element_type=F32)

    @pl.when(j < n_u)
    def _():
        u_ref[...] = project()

    @pl.when(j >= n_u)
    def _():
        scale = jnp.where(j < n_u + n_q, q_scale, 1.0).astype(F32)
        qkv_ref[...] = (project() * scale).astype(BF16)


def _in_proj(x, g, w_in, d_ssm, d_na):
    seq, d_model = x.shape
    tm = min(1024, seq)
    tn = 1024
    n_u, n_q = d_ssm // tn, d_na // tn
    n_cols = w_in.shape[1] // tn
    kern = functools.partial(_inproj_kernel, n_u=n_u, n_q=n_q, q_scale=NA_HEAD_DIM ** -0.5 * LOG2_E)
    return pl.pallas_call(
        kern,
        grid=(seq // tm, n_cols),
        in_specs=[
            pl.BlockSpec((tm, d_model), lambda i, j: (i, 0)),
            pl.BlockSpec((1, d_model), lambda i, j: (0, 0)),
            pl.BlockSpec((d_model, tn), lambda i, j: (0, j)),
        ],
        out_specs=[
            pl.BlockSpec((tm, tn), lambda i, j: (i, jnp.minimum(j, n_u - 1))),
            pl.BlockSpec((tm, tn), lambda i, j: (i, jnp.maximum(j - n_u, 0))),
        ],
        out_shape=[
            jax.ShapeDtypeStruct((seq, d_ssm), F32),
            jax.ShapeDtypeStruct((seq, 3 * d_na), BF16),
        ],
        scratch_shapes=[pltpu.VMEM((tm, d_model), BF16)],
        compiler_params=_params("arbitrary", "arbitrary"),
        name="in_proj",
    )(x, g.reshape(1, d_model), w_in)


def _cmul(ar, ai, br, bi):
    return ar * br - ai * bi, ar * bi + ai * br


def _s5_tables(fwd, bwd, n_tiles):
    t_len = S5_CHUNK
    a_re, a_im, log_dt, b_re, b_im, c_re, c_im = (jnp.stack([f, b]).astype(F32) for f, b in zip(fwd, bwd))
    _, n_grp, n_st = a_re.shape
    n_ch = b_re.shape[-1]
    dt = jnp.exp(log_dt)[:, :, None]
    xr, xi = a_re * dt, a_im * dt
    steps = np.arange(t_len)

    def power(x_r, x_i, exps):
        e = jnp.asarray(exps, F32).reshape(exps.shape + (1,) * (x_r.ndim - 1))
        mag = jnp.exp(x_r[None] * e)
        return mag * jnp.cos(x_i[None] * e), mag * jnp.sin(x_i[None] * e)

    def per_step(exps):
        p_r, p_i = power(xr, xi, exps)
        return jnp.transpose(p_r, (1, 2, 0, 3)), jnp.transpose(p_i, (1, 2, 0, 3))

    a1_r, a1_i = power(xr, xi, np.ones((1, 2)))
    nr, ni = a1_r[0] - 1.0, a1_i[0]
    den = a_re * a_re + a_im * a_im
    qr, qi = (nr * a_re + ni * a_im) / den, (ni * a_re - nr * a_im) / den
    bb_r, bb_i = _cmul(qr[:, :, None, :], qi[:, :, None, :], jnp.swapaxes(b_re, 2, 3), jnp.swapaxes(b_im, 2, 3))
    am_r, am_i = per_step(np.stack([steps, steps[::-1]], axis=1))
    y_r, y_i = _cmul(c_re[:, :, None], c_im[:, :, None], am_r[:, :, :, None, :], am_i[:, :, :, None, :])
    y = jnp.concatenate([y_r, y_i], axis=-1).reshape(2, n_grp, t_len * n_ch, 2 * n_st)
    lag = jnp.einsum('dgck,dgqk->dgcq', jnp.concatenate([bb_r, -bb_i], axis=-1), y,
                     precision=lax.Precision.HIGHEST)
    pe_r, pe_i = per_step(np.stack([t_len - 1 - steps, steps], axis=1))
    st_r, st_i = _cmul(pe_r[:, :, :, None, :], pe_i[:, :, :, None, :], bb_r[:, :, None], bb_i[:, :, None])
    po_r, po_i = per_step(np.stack([steps + 1, t_len - steps], axis=1))
    wr, wi = _cmul(c_re[:, :, None], c_im[:, :, None], po_r[:, :, :, None, :], po_i[:, :, :, None, :])
    wr = wr.reshape(2, n_grp, t_len * n_ch, n_st)
    wi = wi.reshape(2, n_grp, t_len * n_ch, n_st)
    xrf, xif = xr.reshape(2, -1), xi.reshape(2, -1)
    tiles = np.arange(n_tiles)
    ends = power(xrf, xif, np.array([[t_len, t_len], [t_len * n_tiles, t_len * n_tiles]]))
    ap = jnp.stack(ends, axis=2).reshape(8, -1)
    at_r, at_i = power(xrf, xif, t_len * np.stack([tiles, tiles[::-1]], axis=1))
    pw = jnp.transpose(jnp.stack([at_r, at_i], axis=2), (1, 2, 0, 3)).reshape(4, n_tiles, -1)
    return lag, (st_r.astype(BF16), st_i.astype(BF16)), (wr.astype(BF16), (-wi).astype(BF16)), ap, pw


def _s5_a_kernel(u_ref, lag_ref, st_r_ref, st_i_ref, y_ref, s_ref, perm, w_grp, e_pair, ucat,
                 *, n_tiles, seg_stride):
    n_grp = lag_ref.shape[1]
    cw = S5_CHUNK * SSM_GROUP
    n_plane, n_state = 4, st_r_ref.shape[-1]
    lanes = u_ref.shape[-1]
    pc = n_plane * 2 * n_state

    @pl.when(pl.program_id(0) == 0)
    def _():
        e_pair[...] = jnp.zeros_like(e_pair)
        rl = lax.broadcasted_iota(jnp.int32, (cw, n_grp * cw), 0)
        cl = lax.broadcasted_iota(jnp.int32, (cw, n_grp * cw), 1)
        for blk in range(n_grp):
            row = blk * cw + rl
            tgt = ((row % lanes) // SSM_GROUP) * cw + (row // lanes) * SSM_GROUP + row % SSM_GROUP
            perm[blk * cw:(blk + 1) * cw, :] = jnp.where(cl == tgt, 1.0, 0.0).astype(BF16)

    for g in range(n_grp):
        blank = jnp.zeros((SSM_GROUP, cw), F32)
        strip = jnp.concatenate([blank, lag_ref[0, g], lag_ref[1, g], blank], axis=1)
        for t in range(S5_CHUNK):
            r0 = t * SSM_GROUP
            fwd_lo = cw - t * SSM_GROUP
            bwd_lo = 2 * cw + (S5_CHUNK - 1 - t) * SSM_GROUP
            piece = strip[:, fwd_lo:fwd_lo + cw] + strip[:, bwd_lo:bwd_lo + cw]
            w_grp[g, r0:r0 + SSM_GROUP, :] = piece.astype(BF16)
            for k in range(n_plane):
                c0 = k * 2 * n_state + (g % 2) * n_state
                plane = (st_r_ref, st_i_ref)[k % 2][k // 2, g, t]
                e_pair[g // 2, (g % 2) * cw + r0:(g % 2) * cw + r0 + SSM_GROUP, c0:c0 + n_state] = plane

    def gather(jj, _):
        for t in range(S5_CHUNK):
            rows = [u_ref[pl.ds(S5_CHUNK * (2 * jj + h) + t, S5_SEGS, stride=seg_stride), :] for h in range(2)]
            dst = pl.ds(pl.multiple_of(jj * 2 * S5_SEGS, 2 * S5_SEGS), 2 * S5_SEGS)
            ucat[dst, t * lanes:(t + 1) * lanes] = jnp.concatenate(rows, axis=0).astype(BF16)
        return 0

    lax.fori_loop(0, n_tiles // 2, gather, 0)
    ug = jnp.dot(ucat[...], perm[...], preferred_element_type=F32).astype(BF16)
    for g in range(n_grp):
        y_ref[:, g * cw:(g + 1) * cw] = jnp.dot(ug[:, g * cw:(g + 1) * cw], w_grp[g], preferred_element_type=F32)
    for q in range(n_grp // 2):
        s_ref[:, q * pc:(q + 1) * pc] = jnp.dot(ug[:, q * 2 * cw:(q + 1) * 2 * cw], e_pair[q],
                                                preferred_element_type=F32)


def _s5_bc_kernel(s_ref, yi_ref, wo_r_ref, wo_i_ref, ap_ref, pw_ref, y_ref, sin, perm, ynat, c_blk, sin_bf, *,
                  n_tiles, seg_stride):
    n_plane = 4
    _, n_grp, cw, n_state = wo_r_ref.shape
    n_pair = n_grp // 2
    pw_ = 2 * n_state
    pc = n_plane * pw_
    lanes = y_ref.shape[-1]

    @pl.when(pl.program_id(0) == 0)
    def _():
        c_blk[...] = jnp.zeros_like(c_blk)
        rl = lax.broadcasted_iota(jnp.int32, (cw, n_grp * cw), 0)
        cl = lax.broadcasted_iota(jnp.int32, (cw, n_grp * cw), 1)
        for g in range(n_grp):
            tgt = (rl // SSM_GROUP) * lanes + g * SSM_GROUP + (rl % SSM_GROUP)
            perm[g * cw:(g + 1) * cw, :] = jnp.where(cl == tgt, 1.0, 0.0).astype(BF16)

    for g in range(n_grp):
        gl = g % 2
        for k in range(n_plane):
            c0 = k * pw_ + gl * n_state
            c_blk[g // 2, gl * cw:(gl + 1) * cw, c0:c0 + n_state] = (wo_r_ref, wo_i_ref)[k % 2][k // 2, g]

    chains = [(q, d) for q in range(n_pair) for d in range(2)]

    def plane_lanes(q, d):
        base = q * pc + d * 2 * pw_
        return slice(base, base + pw_), slice(base + pw_, base + 2 * pw_)

    def coef(row, q):
        return ap_ref[row:row + 1, q * pw_:(q + 1) * pw_]

    def tile_rows(n, d):
        j = n if d == 0 else n_tiles - 1 - n
        return pl.ds(pl.multiple_of(j * S5_SEGS, S5_SEGS), S5_SEGS)

    def step(n, carry):
        out = []
        for (q, d), (zr, zi) in zip(chains, carry):
            re, im = plane_lanes(q, d)
            rows = tile_rows(n, d)
            sin[rows, re] = zr
            sin[rows, im] = zi
            ar, ai = coef(2 * d, q), coef(2 * d + 1, q)
            out.append((ar * zr - ai * zi + s_ref[rows, re], ar * zi + ai * zr + s_ref[rows, im]))
        return tuple(out)

    z0 = jnp.zeros((S5_SEGS, pw_), F32)
    ends = lax.fori_loop(0, n_tiles, step, tuple((z0, z0) for _ in chains))

    carries = []
    for (q, d), (zr, zi) in zip(chains, ends):
        sr, si = coef(4 + 2 * d, q), coef(5 + 2 * d, q)
        cr = jnp.zeros((1, pw_), F32)
        ci = jnp.zeros((1, pw_), F32)
        seg_r = [None] * S5_SEGS
        seg_i = [None] * S5_SEGS
        for s in (range(S5_SEGS) if d == 0 else range(S5_SEGS - 1, -1, -1)):
            seg_r[s], seg_i[s] = cr, ci
            cr, ci = (zr[s:s + 1] + sr * cr - si * ci, zi[s:s + 1] + sr * ci + si * cr)
        carries.append((jnp.concatenate(seg_r, axis=0), jnp.concatenate(seg_i, axis=0)))

    def fix(n8, _):
        tiles = pl.ds(pl.multiple_of(n8 * 8, 8), 8)
        for (q, d), (car_r, car_i) in zip(chains, carries):
            re, im = plane_lanes(q, d)
            pr8 = pw_ref[2 * d, tiles, q * pw_:(q + 1) * pw_]
            pi8 = pw_ref[2 * d + 1, tiles, q * pw_:(q + 1) * pw_]
            car2_r = jnp.concatenate([car_r, car_r], axis=0)
            car2_i = jnp.concatenate([car_i, car_i], axis=0)
            for r in range(0, 8, 2):
                rows = pl.ds(pl.multiple_of((n8 * 8 + r) * S5_SEGS, 2 * S5_SEGS), 2 * S5_SEGS)
                pr = jnp.concatenate([jnp.broadcast_to(pr8[r + h:r + h + 1], (S5_SEGS, pw_)) for h in range(2)], axis=0)
                pi = jnp.concatenate([jnp.broadcast_to(pi8[r + h:r + h + 1], (S5_SEGS, pw_)) for h in range(2)], axis=0)
                sin_bf[rows, re] = (sin[rows, re] + (pr * car2_r - pi * car2_i)).astype(BF16)
                sin_bf[rows, im] = (sin[rows, im] + (pr * car2_i + pi * car2_r)).astype(BF16)
        return 0

    lax.fori_loop(0, n_tiles // 8, fix, 0)

    parts = []
    for q in range(n_pair):
        lhs = sin_bf[:, q * pc:(q + 1) * pc]
        carried = lax.dot_general(lhs, c_blk[q], (((1,), (1,)), ((), ())), preferred_element_type=F32)
        parts.append(yi_ref[:, q * 2 * cw:(q + 1) * 2 * cw] + carried)
    y_chunk = jnp.concatenate(parts, axis=1).astype(BF16)
    ynat[...] = jnp.dot(y_chunk, perm[...], preferred_element_type=F32)

    def scatter(j, _):
        src = pl.ds(pl.multiple_of(j * S5_SEGS, S5_SEGS), S5_SEGS)
        for t in range(S5_CHUNK):
            y_ref[pl.ds(S5_CHUNK * j + t, S5_SEGS, stride=seg_stride), :] = ynat[src, t * lanes:(t + 1) * lanes]
        return 0

    lax.fori_loop(0, n_tiles, scatter, 0)


def _s5_scan(u, fwd, bwd):
    seq, d_ssm = u.shape
    n_groups = d_ssm // SSM_GROUP
    n_state = SSM_STATE
    cw = S5_CHUNK * SSM_GROUP
    n_rows = seq // S5_CHUNK
    n_tiles = n_rows // S5_SEGS
    lanes = 128
    gpb = lanes // SSM_GROUP
    n_blocks = d_ssm // lanes
    bw = gpb * cw
    sw = gpb * 4 * n_state
    seg_stride = n_tiles * S5_CHUNK

    lag, st, wo, ap, pw = _s5_tables(fwd, bwd, n_tiles)

    def per_block(t):
        return pl.BlockSpec((t.shape[0], gpb) + t.shape[2:], lambda i: (0, i) + (0,) * (t.ndim - 2))

    y_intra, s_loc = pl.pallas_call(
        functools.partial(_s5_a_kernel, n_tiles=n_tiles, seg_stride=seg_stride),
        grid=(n_blocks,),
        in_specs=[
            pl.BlockSpec((seq, lanes), lambda i: (0, i)),
            per_block(lag), per_block(st[0]), per_block(st[1]),
        ],
        out_specs=[
            pl.BlockSpec((n_rows, bw), lambda i: (0, i)),
            pl.BlockSpec((n_rows, sw), lambda i: (0, i)),
        ],
        out_shape=[
            jax.ShapeDtypeStruct((n_rows, n_blocks * bw), F32),
            jax.ShapeDtypeStruct((n_rows, n_blocks * sw), F32),
        ],
        scratch_shapes=[
            pltpu.VMEM((bw, bw), BF16),
            pltpu.VMEM((gpb, cw, cw), BF16),
            pltpu.VMEM((gpb // 2, 2 * cw, 8 * n_state), BF16),
            pltpu.VMEM((n_rows, S5_CHUNK * lanes), BF16),
        ],
        compiler_params=_params("arbitrary"),
        name="s5_a",
    )(u, lag, *st)

    pl_lanes = gpb * n_state
    return pl.pallas_call(
        functools.partial(_s5_bc_kernel, n_tiles=n_tiles, seg_stride=seg_stride),
        grid=(n_blocks,),
        in_specs=[
            pl.BlockSpec((n_rows, sw), lambda i: (0, i)),
            pl.BlockSpec((n_rows, bw), lambda i: (0, i)),
            per_block(wo[0]), per_block(wo[1]),
            pl.BlockSpec((8, pl_lanes), lambda i: (0, i)),
            pl.BlockSpec((4, n_tiles, pl_lanes), lambda i: (0, 0, i)),
        ],
        out_specs=pl.BlockSpec((seq, lanes), lambda i: (0, i)),
        out_shape=jax.ShapeDtypeStruct((seq, d_ssm), F32),
        scratch_shapes=[
            pltpu.VMEM((n_rows, sw), F32),
            pltpu.VMEM((bw, bw), BF16),
            pltpu.VMEM((n_rows, bw), F32),
            pltpu.VMEM((gpb // 2, 2 * cw, 8 * n_state), BF16),
            pltpu.VMEM((n_rows, sw), BF16),
        ],
        compiler_params=_params("arbitrary"),
        name="s5_bc",
    )(s_loc, y_intra, *wo, ap, pw)


def _s5_post_kernel(y_ref, u_ref, d_ref, w_ref, b_ref, g_ref, o_ref):
    y = y_ref[...] + d_ref[...] * u_ref[...]
    c0 = np.float32(np.sqrt(2.0 / np.pi))
    y = 0.5 * y * (1.0 + jnp.tanh(c0 * (y + np.float32(0.044715) * (y * y * y))))
    z = jnp.dot(y.astype(BF16), w_ref[...].astype(BF16), preferred_element_type=F32) + b_ref[...]
    o = y * (1.0 / (1.0 + jnp.exp(-z)))
    o_ref[...] = _rms(o, g_ref[...]).astype(BF16)


def _s5_post(y, u, d_skip, w_glu, b_glu, g):
    seq, d = y.shape
    tm = min(512, seq)
    row = lambda i: (i, 0)
    fix = lambda i: (0, 0)
    return pl.pallas_call(
        _s5_post_kernel,
        grid=(seq // tm,),
        in_specs=[
            pl.BlockSpec((tm, d), row), pl.BlockSpec((tm, d), row), pl.BlockSpec((1, d), fix),
            pl.BlockSpec((d, d), fix), pl.BlockSpec((1, d), fix), pl.BlockSpec((1, d), fix),
        ],
        out_specs=pl.BlockSpec((tm, d), row),
        out_shape=jax.ShapeDtypeStruct((seq, d), BF16),
        compiler_params=_params("arbitrary"),
        name="s5_post",
    )(y, u, d_skip.reshape(1, d), w_glu, b_glu.reshape(1, d), g.reshape(1, d))


def _na_bias_table(rpb):
    n_heads = rpb.shape[0]
    cols = np.arange(GRID_W)
    col_start = np.clip(cols - WIN_COLS // 2, 0, GRID_W - WIN_COLS)
    key_cols = np.arange(GRID_W)
    in_win = (key_cols[None, :] >= col_start[:, None]) & (key_cols[None, :] < col_start[:, None] + WIN_COLS)
    dx = key_cols[None, :] - cols[:, None] + (WIN_COLS - 1)
    pick_x = (dx[:, :, None] == np.arange(2 * WIN_COLS - 1)).astype(np.float32)
    halves = []
    for half in range(2):
        pad = ((0, 0), (half * GRID_W, (1 - half) * GRID_W))
        pick = np.pad(pick_x, pad + ((0, 0),))
        keep = np.pad(in_win, pad, constant_values=True)
        b = jnp.einsum('hyx,ckx->hyck', rpb.astype(F32) * LOG2_E, pick, precision=lax.Precision.HIGHEST)
        b = jnp.where(keep[None, None], b, MASK_NEG)
        b = b.reshape(n_heads // HEADS_PER_DOT, HEADS_PER_DOT, b.shape[1], GRID_W, 2 * GRID_W)
        halves.append(jnp.swapaxes(b, 1, 2).reshape(n_heads // HEADS_PER_DOT, b.shape[2],
                                                    HEADS_PER_DOT * GRID_W, 2 * GRID_W))
    return halves


def _na_kernel(q_ref, k_ref, v_ref, b0_ref, b1_ref, g_ref, o_ref, *, rows, rows_per_step):
    n_keys = WIN_ROWS * GRID_W
    d_na = k_ref.shape[-1]
    pw = HEADS_PER_DOT * NA_HEAD_DIM
    row_head = lax.broadcasted_iota(jnp.int32, (HEADS_PER_DOT * GRID_W, pw), 0) // GRID_W
    col_head = lax.broadcasted_iota(jnp.int32, (HEADS_PER_DOT * GRID_W, pw), 1) // NA_HEAD_DIM
    diag = row_head == col_head
    out_head = lax.broadcasted_iota(jnp.int32, (GRID_W, pw), 1) // NA_HEAD_DIM
    first = pl.program_id(0) * rows_per_step
    block_start = jnp.clip(first - WIN_ROWS // 2, 0, rows - k_ref.shape[0])

    def one_row(i, _):
        r = first + i
        win_start = jnp.clip(r - WIN_ROWS // 2, 0, rows - WIN_ROWS)
        dy0 = win_start - r + (WIN_ROWS - 1)
        k = k_ref[pl.ds(win_start - block_start, WIN_ROWS)].reshape(n_keys, d_na)
        v = v_ref[pl.ds(win_start - block_start, WIN_ROWS)].reshape(n_keys, d_na)
        q_rows = pl.ds(pl.multiple_of(i * GRID_W, GRID_W), GRID_W)
        outs = []
        for p in range(d_na // pw):
            sl = slice(p * pw, (p + 1) * pw)
            q4 = q_ref[q_rows, sl]
            qbd = jnp.where(diag, jnp.concatenate([q4] * HEADS_PER_DOT, axis=0), jnp.zeros((), BF16))
            s = lax.dot_general(qbd, k[:, sl], (((1,), (1,)), ((), ())), preferred_element_type=F32)
            s = s + jnp.concatenate([b0_ref[p, dy0 + 2 * j] + b1_ref[p, dy0 + 2 * j + 1]
                                     for j in range(WIN_ROWS // 2)], axis=1)
            m = jnp.max(s, axis=-1, keepdims=True)
            e = jnp.exp2(s - m)
            l = jnp.sum(e, axis=-1, keepdims=True)
            o = jnp.dot(e.astype(BF16), v[:, sl], preferred_element_type=F32) / l
            acc = jnp.zeros((GRID_W, pw), F32)
            for h in range(HEADS_PER_DOT):
                acc = acc + jnp.where(out_head == h, o[h * GRID_W:(h + 1) * GRID_W], 0.0)
            outs.append(acc)
        y = jnp.concatenate(outs, axis=1)
        o_ref[q_rows, :] = _rms(y, g_ref[...]).astype(BF16)
        return 0

    lax.fori_loop(0, rows_per_step, one_row, 0)


def _neighbourhood_attention(qkv, rpb, g):
    seq = qkv.shape[0]
    d_na = qkv.shape[1] // 3
    rows = seq // GRID_W
    bias = _na_bias_table(rpb)
    qkv3 = qkv.reshape(rows, GRID_W, 3 * d_na)

    rps = 4
    key_rows = 2 * WIN_ROWS

    def block_start(b):
        return jnp.clip(b * rps - WIN_ROWS // 2, 0, rows - key_rows)

    window = (pl.Element(key_rows), pl.Element(GRID_W), pl.Element(d_na))

    return pl.pallas_call(
        functools.partial(_na_kernel, rows=rows, rows_per_step=rps),
        grid=(rows // rps,),
        in_specs=[
            pl.BlockSpec((rps * GRID_W, d_na), lambda b: (b, 0)),
            pl.BlockSpec(window, lambda b: (block_start(b), 0, d_na)),
            pl.BlockSpec(window, lambda b: (block_start(b), 0, 2 * d_na)),
            pl.BlockSpec(bias[0].shape, lambda b: (0, 0, 0, 0)),
            pl.BlockSpec(bias[1].shape, lambda b: (0, 0, 0, 0)),
            pl.BlockSpec((1, d_na), lambda b: (0, 0)),
        ],
        out_specs=pl.BlockSpec((rps * GRID_W, d_na), lambda b: (b, 0)),
        out_shape=jax.ShapeDtypeStruct((seq, d_na), BF16),
        compiler_params=_params("arbitrary"),
        name="na",
    )(qkv, qkv3, qkv3, bias[0], bias[1], g.reshape(1, d_na))


def _outproj_kernel(a_ref, b_ref, w_ref, x_ref, o_ref):
    da = a_ref.shape[-1]
    acc = jnp.dot(a_ref[...], w_ref[:da, :].astype(BF16), preferred_element_type=F32)
    acc = acc + jnp.dot(b_ref[...], w_ref[da:, :].astype(BF16), preferred_element_type=F32)
    o_ref[...] = x_ref[...] + acc


def _out_proj(y_ssm, y_na, w_out, x):
    seq, d_model = x.shape
    da, db = y_ssm.shape[1], y_na.shape[1]
    tm = min(2048, seq)
    tn = 512
    return pl.pallas_call(
        _outproj_kernel,
        grid=(seq // tm, d_model // tn),
        in_specs=[
            pl.BlockSpec((tm, da), lambda i, j: (i, 0)),
            pl.BlockSpec((tm, db), lambda i, j: (i, 0)),
            pl.BlockSpec((da + db, tn), lambda i, j: (0, j)),
            pl.BlockSpec((tm, tn), lambda i, j: (i, j)),
        ],
        out_specs=pl.BlockSpec((tm, tn), lambda i, j: (i, j)),
        out_shape=jax.ShapeDtypeStruct((seq, d_model), F32),
        compiler_params=_params("arbitrary", "arbitrary"),
        name="out_proj",
    )(y_ssm, y_na, w_out, x)


def _router_kernel(x_ref, g_ref, wt_ref, h_ref, a_ref):
    h = _rms(x_ref[...], g_ref[...])
    h_hi = h.astype(BF16)
    h_ref[...] = h_hi
    h_lo = (h - h_hi.astype(F32)).astype(BF16)
    w = wt_ref[...]
    w_hi = w.astype(BF16)
    w_lo = (w - w_hi.astype(F32)).astype(BF16)
    n_exp = w.shape[0]
    nt = (((1,), (1,)), ((), ()))
    both = lax.dot_general(jnp.concatenate([w_hi, w_lo], axis=0), h_hi, nt, preferred_element_type=F32)
    logits = both[:n_exp] + both[n_exp:] + lax.dot_general(w_hi, h_lo, nt, preferred_element_type=F32)
    m = jnp.max(logits, axis=0, keepdims=True)
    e = jnp.exp(logits - m)
    a_ref[...] = e / jnp.sum(e, axis=0, keepdims=True)


def _router(x1, g, w_router):
    seq, d_model = x1.shape
    n_exp = w_router.shape[1]
    tm = min(512, seq)
    return pl.pallas_call(
        _router_kernel,
        grid=(seq // tm,),
        in_specs=[
            pl.BlockSpec((tm, d_model), lambda i: (i, 0)),
            pl.BlockSpec((1, d_model), lambda i: (0, 0)),
            pl.BlockSpec((n_exp, d_model), lambda i: (0, 0)),
        ],
        out_specs=[
            pl.BlockSpec((tm, d_model), lambda i: (i, 0)),
            pl.BlockSpec((n_exp, tm), lambda i: (0, i)),
        ],
        out_shape=[
            jax.ShapeDtypeStruct((seq, d_model), BF16),
            jax.ShapeDtypeStruct((n_exp, seq), F32),
        ],
        compiler_params=_params("arbitrary"),
        name="router",
    )(x1, g.reshape(1, d_model), w_router.T)


def _topk_kernel(a_ref, posw_ref, gate_ref, ws_ref, nr_ref, *, cap, blk, win):
    a = a_ref[...]
    n_exp, seq = a.shape
    n_blk = seq // blk
    bits = pltpu.bitcast(a, jnp.int32)

    def bit_step(i, thr):
        cand = thr | jnp.left_shift(jnp.int32(1), 30 - i)
        cnt = jnp.sum((bits >= cand).astype(jnp.int32), axis=-1, keepdims=True)
        return jnp.where(cnt >= cap, cand, thr)

    thr = lax.fori_loop(0, 31, bit_step, jnp.zeros((n_exp, 1), jnp.int32))
    gt = bits > thr
    eq = bits == thr
    need = cap - jnp.sum(gt.astype(jnp.int32), axis=-1, keepdims=True)

    tri = (lax.broadcasted_iota(jnp.int32, (blk, blk), 0)
           <= lax.broadcasted_iota(jnp.int32, (blk, blk), 1)).astype(BF16)
    blk_of_tok = lax.broadcasted_iota(jnp.int32, (seq, n_blk), 0) // blk
    tok_to_blk = (blk_of_tok == lax.broadcasted_iota(jnp.int32, (seq, n_blk), 1)).astype(BF16)
    blk_before = (lax.broadcasted_iota(jnp.int32, (n_blk, n_blk), 0)
                  < lax.broadcasted_iota(jnp.int32, (n_blk, n_blk), 1)).astype(BF16)
    erow = lax.broadcasted_iota(jnp.int32, (2 * n_blk, seq), 0)
    ecol = lax.broadcasted_iota(jnp.int32, (2 * n_blk, seq), 1) // blk
    expand = jnp.where(erow == ecol, 32.0, jnp.where(erow - n_blk == ecol, 1.0, 0.0)).astype(BF16)

    def prefix_counts(mask):
        mb = jnp.where(mask, 1.0, 0.0).astype(BF16)
        local = jnp.concatenate(
            [jnp.dot(mb[:, b * blk:(b + 1) * blk], tri, preferred_element_type=F32) for b in range(n_blk)],
            axis=1)
        per_blk = jnp.dot(mb, tok_to_blk, preferred_element_type=F32)
        start = jnp.dot(per_blk.astype(BF16), blk_before, preferred_element_type=F32)
        hi = jnp.floor(start * (1.0 / 32.0))
        parts = jnp.concatenate([hi, start - 32.0 * hi], axis=1).astype(BF16)
        start_tok = jnp.dot(parts, expand, preferred_element_type=F32)
        return local + start_tok, start, start_tok, per_blk

    eq_incl, _, _, _ = prefix_counts(eq)
    sel = gt | (eq & (eq_incl - 1.0 < need.astype(F32)))
    incl, start, start_tok, per_blk = prefix_counts(sel)

    def window(s):
        return jnp.floor(s * (1.0 / MOE_WIN_ALIGN)) * MOE_WIN_ALIGN

    posw_ref[...] = jnp.where(sel, (incl - 1.0 - window(start_tok)).astype(jnp.int32), -1)
    gate_ref[...] = jnp.where(sel, a, 0.0)
    ws_ref[...] = window(start).astype(jnp.int32)
    span = start - window(start) + per_blk
    rounds = jnp.floor((span + float(win - 1)) * (1.0 / win))
    nr_ref[...] = jnp.max(rounds, axis=0, keepdims=True).astype(jnp.int32)


def _topk(aff_t, cap, blk, win):
    n_exp, seq = aff_t.shape
    n_blk = seq // blk
    full = lambda *_: (0, 0)
    return pl.pallas_call(
        functools.partial(_topk_kernel, cap=cap, blk=blk, win=win),
        grid=(1,),
        in_specs=[pl.BlockSpec((n_exp, seq), full)],
        out_specs=[pl.BlockSpec((n_exp, seq), full), pl.BlockSpec((n_exp, seq), full),
                   pl.BlockSpec((n_exp, n_blk), full), pl.BlockSpec((1, n_blk), full)],
        out_shape=[
            jax.ShapeDtypeStruct((n_exp, seq), jnp.int32),
            jax.ShapeDtypeStruct((n_exp, seq), F32),
            jax.ShapeDtypeStruct((n_exp, n_blk), jnp.int32),
            jax.ShapeDtypeStruct((1, n_blk), jnp.int32),
        ],
        compiler_params=_params("arbitrary"),
        name="topk",
    )(aff_t)


def _window(ws_ref, e, b, r, n_blk, win, cap):
    ws = ws_ref[e * n_blk + b] + r * win
    start = jnp.minimum(ws, cap - win)
    return pl.multiple_of(start, MOE_WIN_ALIGN), ws - start


def _gather_kernel(ws_ref, nr_ref, h_ref, rel_ref, xe_ref, *, blk, win, n_blk):
    n_exp, cap, _ = xe_ref.shape
    xe_ref[...] = jnp.zeros_like(xe_ref)
    slot = lax.broadcasted_iota(jnp.int32, (win, blk), 0)

    def block(b, _):
        rows = h_ref[pl.ds(pl.multiple_of(b * blk, blk), blk), :]
        rel = rel_ref[b]

        def one_round(r, _):
            starts, hots = [], []
            for e in range(n_exp):
                start, shift = _window(ws_ref, e, b, r, n_blk, win, cap)
                relr = rel[e:e + 1, :] - r * win
                key = jnp.where(relr >= 0, relr + shift, -1)
                hots.append(jnp.where(slot == key, 1.0, 0.0).astype(BF16))
                starts.append(start)
            res = jnp.dot(jnp.concatenate(hots, axis=0), rows, preferred_element_type=F32)
            for e in range(n_exp):
                dst = pl.ds(starts[e], win)
                xe_ref[e, dst, :] = (xe_ref[e, dst, :].astype(F32) + res[e * win:(e + 1) * win]).astype(BF16)
            return 0

        lax.fori_loop(0, nr_ref[b], one_round, 0)
        return 0

    lax.fori_loop(0, n_blk, block, 0)


def _moe_gather(ws_flat, n_rounds, h2, rel3, cap, win):
    seq, d_model = h2.shape
    n_blk, n_exp, blk = rel3.shape
    dq = d_model // 4
    grid_spec = pltpu.PrefetchScalarGridSpec(
        num_scalar_prefetch=2,
        grid=(4,),
        in_specs=[
            pl.BlockSpec((seq, dq), lambda c, ws, nr: (0, c)),
            pl.BlockSpec((n_blk, n_exp, blk), lambda c, ws, nr: (0, 0, 0)),
        ],
        out_specs=pl.BlockSpec((n_exp, cap, dq), lambda c, ws, nr: (0, 0, c)),
    )
    return pl.pallas_call(
        functools.partial(_gather_kernel, blk=blk, win=win, n_blk=n_blk),
        grid_spec=grid_spec,
        out_shape=jax.ShapeDtypeStruct((n_exp, cap, d_model), BF16),
        compiler_params=_params("arbitrary"),
        name="moe_gather",
    )(ws_flat, n_rounds, h2, rel3)


def _ffn_kernel(x_ref, wg_ref, wu_ref, wd_ref, y_ref, act_ref, *, n_f):
    s = pl.program_id(1)
    tf = wg_ref.shape[-1]

    @pl.when(s < n_f)
    def _():
        x = x_ref[0]
        g = jnp.dot(x, wg_ref[0].astype(BF16), preferred_element_type=F32)
        u = jnp.dot(x, wu_ref[0].astype(BF16), preferred_element_type=F32)
        act_ref[s] = (g * (1.0 / (1.0 + jnp.exp(-g))) * u).astype(BF16)

    @pl.when(s >= n_f)
    def _():
        acc = jnp.dot(act_ref[0], wd_ref[0, 0:tf, :].astype(BF16), preferred_element_type=F32)
        for f in range(1, n_f):
            acc = acc + jnp.dot(act_ref[f], wd_ref[0, f * tf:(f + 1) * tf, :].astype(BF16),
                                preferred_element_type=F32)
        y_ref[0] = acc.astype(BF16)


def _moe_ffn(xe, w_gate, w_up, w_down):
    n_exp, cap, d_model = xe.shape
    d_ff = w_gate.shape[-1]
    tf = 512
    tn = 1024
    n_f, n_n = d_ff // tf, d_model // tn
    up_tile = lambda e, s: (e, 0, jnp.minimum(s, n_f - 1))
    down_tile = lambda e, s: (e, 0, jnp.maximum(s - n_f, 0))
    return pl.pallas_call(
        functools.partial(_ffn_kernel, n_f=n_f),
        grid=(n_exp, n_f + n_n),
        in_specs=[
            pl.BlockSpec((1, cap, d_model), lambda e, s: (e, 0, 0)),
            pl.BlockSpec((1, d_model, tf), up_tile),
            pl.BlockSpec((1, d_model, tf), up_tile),
            pl.BlockSpec((1, d_ff, tn), down_tile),
        ],
        out_specs=pl.BlockSpec((1, cap, tn), down_tile),
        out_shape=jax.ShapeDtypeStruct((n_exp, cap, d_model), BF16),
        scratch_shapes=[pltpu.VMEM((n_f, cap, tf), BF16)],
        compiler_params=_params("arbitrary", "arbitrary"),
        name="moe_ffn",
    )(xe, w_gate, w_up, w_down)


def _combine_kernel(ws_ref, nr_ref, ye_ref, x_ref, rel_ref, gate_ref, o_ref, *, blk, win, n_blk):
    n_exp, cap, _ = ye_ref.shape
    sub = x_ref.shape[0] // blk
    slot = lax.broadcasted_iota(jnp.int32, (win, blk), 0)

    def block(s, _):
        b = pl.program_id(1) * sub + s
        tok = pl.ds(pl.multiple_of(s * blk, blk), blk)
        rel = rel_ref[b]
        gate = gate_ref[b]

        def one_round(r, acc):
            gates, wins = [], []
            for e in range(n_exp):
                start, shift = _window(ws_ref, e, b, r, n_blk, win, cap)
                relr = rel[e:e + 1, :] - r * win
                key = jnp.where(relr >= 0, relr + shift, -1)
                gates.append(jnp.where(slot == key, gate[e:e + 1, :], 0.0).astype(BF16))
                wins.append(ye_ref[e, pl.ds(start, win), :])
            res = lax.dot_general(jnp.concatenate(gates, axis=0), jnp.concatenate(wins, axis=0),
                                  (((0,), (0,)), ((), ())), preferred_element_type=F32)
            return acc + res

        o_ref[tok, :] = lax.fori_loop(1, nr_ref[b], one_round, one_round(0, x_ref[tok, :]))
        return 0

    lax.fori_loop(0, sub, block, 0)


def _moe_combine(ws_flat, n_rounds, ye, x1, rel3, gate3, win):
    seq, d_model = x1.shape
    n_exp, cap, _ = ye.shape
    n_blk, _, blk = rel3.shape
    dq = d_model // 4
    tile = min(4, n_blk) * blk
    whole = lambda c, t, ws, nr: (0, 0, 0)
    grid_spec = pltpu.PrefetchScalarGridSpec(
        num_scalar_prefetch=2,
        grid=(4, seq // tile),
        in_specs=[
            pl.BlockSpec((n_exp, cap, dq), lambda c, t, ws, nr: (0, 0, c)),
            pl.BlockSpec((tile, dq), lambda c, t, ws, nr: (t, c)),
            pl.BlockSpec(rel3.shape, whole),
            pl.BlockSpec(gate3.shape, whole),
        ],
        out_specs=pl.BlockSpec((tile, dq), lambda c, t, ws, nr: (t, c)),
    )
    return pl.pallas_call(
        functools.partial(_combine_kernel, blk=blk, win=win, n_blk=n_blk),
        grid_spec=grid_spec,
        out_shape=jax.ShapeDtypeStruct((seq, d_model), F32),
        compiler_params=_params("arbitrary", "arbitrary"),
        name="moe_combine",
    )(ws_flat, n_rounds, ye, x1, rel3, gate3)


def _final_norm_kernel(x_ref, g_ref, o_ref):
    o_ref[...] = _rms(x_ref[...], g_ref[...])


def _final_norm(x, g):
    seq, d_model = x.shape
    tm = min(512, seq)
    return pl.pallas_call(
        _final_norm_kernel,
        grid=(seq // tm,),
        in_specs=[pl.BlockSpec((tm, d_model), lambda i: (i, 0)), pl.BlockSpec((1, d_model), lambda i: (0, 0))],
        out_specs=pl.BlockSpec((tm, d_model), lambda i: (i, 0)),
        out_shape=jax.ShapeDtypeStruct((seq, d_model), F32),
        compiler_params=_params("arbitrary"),
        name="final_norm",
    )(x, g.reshape(1, d_model))


def _layer(x, norm_mix_g, w_in, fwd, bwd, ssm_d, w_glu, b_glu, na_rpb, g_ssm_out, g_na_out, w_out,
           norm_ffn_g, w_router, w_gate, w_up, w_down):
    seq, d_model = x.shape
    d_ssm = ssm_d.shape[0]
    d_na = g_na_out.shape[0]
    n_exp = w_router.shape[1]
    cap = EC_CAPACITY_FACTOR * seq // n_exp
    blk = min(MOE_TOK_BLOCK, cap // 2)
    win = min(MOE_WIN, cap)

    u, qkv = _in_proj(x, norm_mix_g, w_in, d_ssm, d_na)
    y_ssm = _s5_post(_s5_scan(u, fwd, bwd), u, ssm_d, w_glu, b_glu, g_ssm_out)
    y_na = _neighbourhood_attention(qkv, na_rpb, g_na_out)
    x1 = _out_proj(y_ssm, y_na, w_out, x)

    h2, aff_t = _router(x1, norm_ffn_g, w_router)
    rel, gate, ws, n_rounds = _topk(aff_t, cap, blk, win)
    ws_flat = ws.reshape(-1)
    n_rounds = n_rounds.reshape(-1)
    rel3 = jnp.swapaxes(rel.reshape(n_exp, seq // blk, blk), 0, 1)
    gate3 = jnp.swapaxes(gate.reshape(n_exp, seq // blk, blk), 0, 1)
    xe = _moe_gather(ws_flat, n_rounds, h2, rel3, cap, win)
    ye = _moe_ffn(xe, w_gate, w_up, w_down)
    return _moe_combine(ws_flat, n_rounds, ye, x1, rel3, gate3, win)


def kernel(x, norm_mix_g, w_in, a_re_fwd, a_im_fwd, log_dt_fwd, b_re_fwd, b_im_fwd, c_re_fwd, c_im_fwd, a_re_bwd, a_im_bwd, log_dt_bwd, b_re_bwd, b_im_bwd, c_re_bwd, c_im_bwd, ssm_d, w_glu, b_glu, na_rpb, g_ssm_out, g_na_out, w_out, norm_ffn_g, w_router, w_gate, w_up, w_down, norm_final_g):
    bsz = x.shape[0]
    depth = w_in.shape[0]
    outs = []
    for b in range(bsz):
        xb = x[b]
        for l in range(depth):
            fwd = (a_re_fwd[l], a_im_fwd[l], log_dt_fwd[l], b_re_fwd[l], b_im_fwd[l], c_re_fwd[l], c_im_fwd[l])
            bwd = (a_re_bwd[l], a_im_bwd[l], log_dt_bwd[l], b_re_bwd[l], b_im_bwd[l], c_re_bwd[l], c_im_bwd[l])
            xb = _layer(xb, norm_mix_g[l], w_in[l], fwd, bwd, ssm_d[l], w_glu[l], b_glu[l], na_rpb[l],
                        g_ssm_out[l], g_na_out[l], w_out[l], norm_ffn_g[l], w_router[l],
                        w_gate[l], w_up[l], w_down[l])
        outs.append(_final_norm(xb, norm_final_g))
    return jnp.stack(outs)
```

```python
import functools

import numpy as np
import jax
import jax.numpy as jnp
from jax import lax
from jax.experimental import pallas as pl
from jax.experimental.pallas import tpu as pltpu

F32 = jnp.float32
BF16 = jnp.bfloat16

RMS_EPS = 1e-6
SSM_GROUP = 16
SSM_STATE = 64
NA_HEADS = 16
NA_HEAD_DIM = 64
GRID_W = 64
WIN_ROWS = 8
WIN_COLS = 16
N_EXPERTS = 16
EC_CAPACITY_FACTOR = 2

S5_CHUNK = 16
S5_SEGS = 8
HEADS_PER_DOT = 4
MOE_TOK_BLOCK = 256
MOE_WIN_ALIGN = 16
MOE_WIN = 64
MASK_NEG = -1e30
LOG2_E = float(np.log2(np.e))

VMEM_LIMIT_BYTES = 56 * 1024 * 1024


def _params(*semantics):
    return pltpu.CompilerParams(dimension_semantics=semantics, vmem_limit_bytes=VMEM_LIMIT_BYTES)


def _rms(x, g):
    ms = jnp.mean(x * x, axis=-1, keepdims=True)
    return x * lax.rsqrt(ms + RMS_EPS) * g


def _inproj_kernel(x_ref, g_ref, w_ref, u_ref, qkv_ref, h_scr, *, n_u, n_q, q_scale):
    j = pl.program_id(1)

    @pl.when(j == 0)
    def _():
        h_scr[...] = _rms(x_ref[...], g_ref[...]).astype(BF16)

    def project():
        return jnp.dot(h_scr[...], w_ref[...].astype(BF16), preferred_element_type=F32)

    @pl.when(j < n_u)
    def _():
        u_ref[...] = project()

    @pl.when(j >= n_u)
    def _():
        scale = jnp.where(j < n_u + n_q, q_scale, 1.0).astype(F32)
        qkv_ref[...] = (project() * scale).astype(BF16)


def _in_proj(x, g, w_in, d_ssm, d_na):
    seq, d_model = x.shape
    tm = min(1024, seq)
    tn = 1024
    n_u, n_q = d_ssm // tn, d_na // tn
    n_cols = w_in.shape[1] // tn
    kern = functools.partial(_inproj_kernel, n_u=n_u, n_q=n_q, q_scale=NA_HEAD_DIM ** -0.5 * LOG2_E)
    return pl.pallas_call(
        kern,
        grid=(seq // tm, n_cols),
        in_specs=[
            pl.BlockSpec((tm, d_model), lambda i, j: (i, 0)),
            pl.BlockSpec((1, d_model), lambda i, j: (0, 0)),
            pl.BlockSpec((d_model, tn), lambda i, j: (0, j)),
        ],
        out_specs=[
            pl.BlockSpec((tm, tn), lambda i, j: (i, jnp.minimum(j, n_u - 1))),
            pl.BlockSpec((tm, tn), lambda i, j: (i, jnp.maximum(j - n_u, 0))),
        ],
        out_shape=[
            jax.ShapeDtypeStruct((seq, d_ssm), F32),
            jax.ShapeDtypeStruct((seq, 3 * d_na), BF16),
        ],
        scratch_shapes=[pltpu.VMEM((tm, d_model), BF16)],
        compiler_params=_params("arbitrary", "arbitrary"),
        name="in_proj",
    )(x, g.reshape(1, d_model), w_in)


def _cmul(ar, ai, br, bi):
    return ar * br - ai * bi, ar * bi + ai * br


def _s5_tables(fwd, bwd, n_tiles):
    t_len = S5_CHUNK
    a_re, a_im, log_dt, b_re, b_im, c_re, c_im = (jnp.stack([f, b]).astype(F32) for f, b in zip(fwd, bwd))
    _, n_grp, n_st = a_re.shape
    n_ch = b_re.shape[-1]
    dt = jnp.exp(log_dt)[:, :, None]
    xr, xi = a_re * dt, a_im * dt
    steps = np.arange(t_len)

    def power(x_r, x_i, exps):
        e = jnp.asarray(exps, F32).reshape(exps.shape + (1,) * (x_r.ndim - 1))
        mag = jnp.exp(x_r[None] * e)
        return mag * jnp.cos(x_i[None] * e), mag * jnp.sin(x_i[None] * e)

    def per_step(exps):
        p_r, p_i = power(xr, xi, exps)
        return jnp.transpose(p_r, (1, 2, 0, 3)), jnp.transpose(p_i, (1, 2, 0, 3))

    a1_r, a1_i = power(xr, xi, np.ones((1, 2)))
    nr, ni = a1_r[0] - 1.0, a1_i[0]
    den = a_re * a_re + a_im * a_im
    qr, qi = (nr * a_re + ni * a_im) / den, (ni * a_re - nr * a_im) / den
    bb_r, bb_i = _cmul(qr[:, :, None, :], qi[:, :, None, :], jnp.swapaxes(b_re, 2, 3), jnp.swapaxes(b_im, 2, 3))
    am_r, am_i = per_step(np.stack([steps, steps[::-1]], axis=1))
    y_r, y_i = _cmul(c_re[:, :, None], c_im[:, :, None], am_r[:, :, :, None, :], am_i[:, :, :, None, :])
    y = jnp.concatenate([y_r, y_i], axis=-1).reshape(2, n_grp, t_len * n_ch, 2 * n_st)
    lag = jnp.einsum('dgck,dgqk->dgcq', jnp.concatenate([bb_r, -bb_i], axis=-1), y,
                     precision=lax.Precision.HIGHEST)
    st = per_step(np.stack([t_len - 1 - steps, steps], axis=1)) + (bb_r, bb_i)
    wo = per_step(np.stack([steps + 1, t_len - steps], axis=1)) + (c_re, c_im)
    xrf, xif = xr.reshape(2, -1), xi.reshape(2, -1)
    tiles = np.arange(n_tiles)
    ends = power(xrf, xif, np.array([[t_len, t_len], [t_len * n_tiles, t_len * n_tiles]]))
    ap = jnp.stack(ends, axis=2).reshape(8, -1)
    at_r, at_i = power(xrf, xif, t_len * np.stack([tiles, tiles[::-1]], axis=1))
    pw = jnp.transpose(jnp.stack([at_r, at_i], axis=2), (1, 2, 0, 3)).reshape(4, n_tiles, -1)
    return lag, st, wo, ap, pw


def _s5_a_kernel(u_ref, lag_ref, pe_r_ref, pe_i_ref, bb_r_ref, bb_i_ref, y_ref, s_ref, perm, w_grp, e_pair, ucat,
                 *, n_tiles, seg_stride):
    n_grp = lag_ref.shape[1]
    cw = S5_CHUNK * SSM_GROUP
    n_plane, n_state = 4, pe_r_ref.shape[-1]
    lanes = u_ref.shape[-1]
    pc = n_plane * 2 * n_state

    @pl.when(pl.program_id(0) == 0)
    def _():
        e_pair[...] = jnp.zeros_like(e_pair)
        rl = lax.broadcasted_iota(jnp.int32, (cw, n_grp * cw), 0)
        cl = lax.broadcasted_iota(jnp.int32, (cw, n_grp * cw), 1)
        for blk in range(n_grp):
            row = blk * cw + rl
            tgt = ((row % lanes) // SSM_GROUP) * cw + (row // lanes) * SSM_GROUP + row % SSM_GROUP
            perm[blk * cw:(blk + 1) * cw, :] = jnp.where(cl == tgt, 1.0, 0.0).astype(BF16)

    for g in range(n_grp):
        blank = jnp.zeros((SSM_GROUP, cw), F32)
        strip = jnp.concatenate([blank, lag_ref[0, g], lag_ref[1, g], blank], axis=1)
        for t in range(S5_CHUNK):
            r0 = t * SSM_GROUP
            fwd_lo = cw - t * SSM_GROUP
            bwd_lo = 2 * cw + (S5_CHUNK - 1 - t) * SSM_GROUP
            piece = strip[:, fwd_lo:fwd_lo + cw] + strip[:, bwd_lo:bwd_lo + cw]
            w_grp[g, r0:r0 + SSM_GROUP, :] = piece.astype(BF16)
            for d in range(2):
                planes = _cmul(pe_r_ref[d, g, t:t + 1, :], pe_i_ref[d, g, t:t + 1, :], bb_r_ref[d, g], bb_i_ref[d, g])
                for ri in range(2):
                    c0 = (2 * d + ri) * 2 * n_state + (g % 2) * n_state
                    rows = slice((g % 2) * cw + r0, (g % 2) * cw + r0 + SSM_GROUP)
                    e_pair[g // 2, rows, c0:c0 + n_state] = planes[ri].astype(BF16)

    def gather(jj, _):
        for t in range(S5_CHUNK):
            rows = [u_ref[pl.ds(S5_CHUNK * (2 * jj + h) + t, S5_SEGS, stride=seg_stride), :] for h in range(2)]
            dst = pl.ds(pl.multiple_of(jj * 2 * S5_SEGS, 2 * S5_SEGS), 2 * S5_SEGS)
            ucat[dst, t * lanes:(t + 1) * lanes] = jnp.concatenate(rows, axis=0).astype(BF16)
        return 0

    lax.fori_loop(0, n_tiles // 2, gather, 0)
    ug = jnp.dot(ucat[...], perm[...], preferred_element_type=F32).astype(BF16)
    for g in range(n_grp):
        y_ref[:, g * cw:(g + 1) * cw] = jnp.dot(ug[:, g * cw:(g + 1) * cw], w_grp[g], preferred_element_type=F32)
    for q in range(n_grp // 2):
        s_ref[:, q * pc:(q + 1) * pc] = jnp.dot(ug[:, q * 2 * cw:(q + 1) * 2 * cw], e_pair[q],
                                                preferred_element_type=F32)


def _s5_bc_kernel(s_ref, yi_ref, po_r_ref, po_i_ref, c_r_ref, c_i_ref, ap_ref, pw_ref, y_ref, sin, perm, ynat, c_blk,
                  sin_bf, *, n_tiles, seg_stride):
    n_plane = 4
    _, n_grp, _, n_state = po_r_ref.shape
    cw = S5_CHUNK * SSM_GROUP
    n_pair = n_grp // 2
    pw_ = 2 * n_state
    pc = n_plane * pw_
    lanes = y_ref.shape[-1]

    @pl.when(pl.program_id(0) == 0)
    def _():
        c_blk[...] = jnp.zeros_like(c_blk)
        rl = lax.broadcasted_iota(jnp.int32, (cw, n_grp * cw), 0)
        cl = lax.broadcasted_iota(jnp.int32, (cw, n_grp * cw), 1)
        for g in range(n_grp):
            tgt = (rl // SSM_GROUP) * lanes + g * SSM_GROUP + (rl % SSM_GROUP)
            perm[g * cw:(g + 1) * cw, :] = jnp.where(cl == tgt, 1.0, 0.0).astype(BF16)

    for g in range(n_grp):
        gl = g % 2
        for d in range(2):
            for t in range(S5_CHUNK):
                w_r, w_i = _cmul(c_r_ref[d, g], c_i_ref[d, g], po_r_ref[d, g, t:t + 1, :], po_i_ref[d, g, t:t + 1, :])
                rows = slice(gl * cw + t * SSM_GROUP, gl * cw + (t + 1) * SSM_GROUP)
                for ri, plane in enumerate((w_r, -w_i)):
                    c0 = (2 * d + ri) * pw_ + gl * n_state
                    c_blk[g // 2, rows, c0:c0 + n_state] = plane.astype(BF16)

    chains = [(q, d) for q in range(n_pair) for d in range(2)]

    def plane_lanes(q, d):
        base = q * pc + d * 2 * pw_
        return slice(base, base + pw_), slice(base + pw_, base + 2 * pw_)

    def coef(row, q):
        return ap_ref[row:row + 1, q * pw_:(q + 1) * pw_]

    def tile_rows(n, d):
        j = n if d == 0 else n_tiles - 1 - n
        return pl.ds(pl.multiple_of(j * S5_SEGS, S5_SEGS), S5_SEGS)

    def step(n, carry):
        out = []
        for (q, d), (zr, zi) in zip(chains, carry):
            re, im = plane_lanes(q, d)
            rows = tile_rows(n, d)
            sin[rows, re] = zr
            sin[rows, im] = zi
            ar, ai = coef(2 * d, q), coef(2 * d + 1, q)
            out.append((ar * zr - ai * zi + s_ref[rows, re], ar * zi + ai * zr + s_ref[rows, im]))
        return tuple(out)

    z0 = jnp.zeros((S5_SEGS, pw_), F32)
    ends = lax.fori_loop(0, n_tiles, step, tuple((z0, z0) for _ in chains))

    carries = []
    for (q, d), (zr, zi) in zip(chains, ends):
        sr, si = coef(4 + 2 * d, q), coef(5 + 2 * d, q)
        cr = jnp.zeros((1, pw_), F32)
        ci = jnp.zeros((1, pw_), F32)
        seg_r = [None] * S5_SEGS
        seg_i = [None] * S5_SEGS
        for s in (range(S5_SEGS) if d == 0 else range(S5_SEGS - 1, -1, -1)):
            seg_r[s], seg_i[s] = cr, ci
            cr, ci = (zr[s:s + 1] + sr * cr - si * ci, zi[s:s + 1] + sr * ci + si * cr)
        carries.append((jnp.concatenate(seg_r, axis=0), jnp.concatenate(seg_i, axis=0)))

    def fix(n8, _):
        tiles = pl.ds(pl.multiple_of(n8 * 8, 8), 8)
        for (q, d), (car_r, car_i) in zip(chains, carries):
            re, im = plane_lanes(q, d)
            pr8 = pw_ref[2 * d, tiles, q * pw_:(q + 1) * pw_]
            pi8 = pw_ref[2 * d + 1, tiles, q * pw_:(q + 1) * pw_]
            car2_r = jnp.concatenate([car_r, car_r], axis=0)
            car2_i = jnp.concatenate([car_i, car_i], axis=0)
            for r in range(0, 8, 2):
                rows = pl.ds(pl.multiple_of((n8 * 8 + r) * S5_SEGS, 2 * S5_SEGS), 2 * S5_SEGS)
                pr = jnp.concatenate([jnp.broadcast_to(pr8[r + h:r + h + 1], (S5_SEGS, pw_)) for h in range(2)], axis=0)
                pi = jnp.concatenate([jnp.broadcast_to(pi8[r + h:r + h + 1], (S5_SEGS, pw_)) for h in range(2)], axis=0)
                sin_bf[rows, re] = (sin[rows, re] + (pr * car2_r - pi * car2_i)).astype(BF16)
                sin_bf[rows, im] = (sin[rows, im] + (pr * car2_i + pi * car2_r)).astype(BF16)
        return 0

    lax.fori_loop(0, n_tiles // 8, fix, 0)

    parts = []
    for q in range(n_pair):
        lhs = sin_bf[:, q * pc:(q + 1) * pc]
        carried = lax.dot_general(lhs, c_blk[q], (((1,), (1,)), ((), ())), preferred_element_type=F32)
        parts.append(yi_ref[:, q * 2 * cw:(q + 1) * 2 * cw] + carried)
    y_chunk = jnp.concatenate(parts, axis=1).astype(BF16)
    ynat[...] = jnp.dot(y_chunk, perm[...], preferred_element_type=F32)

    def scatter(j, _):
        src = pl.ds(pl.multiple_of(j * S5_SEGS, S5_SEGS), S5_SEGS)
        for t in range(S5_CHUNK):
            y_ref[pl.ds(S5_CHUNK * j + t, S5_SEGS, stride=seg_stride), :] = ynat[src, t * lanes:(t + 1) * lanes]
        return 0

    lax.fori_loop(0, n_tiles, scatter, 0)


def _s5_scan(u, fwd, bwd):
    seq, d_ssm = u.shape
    n_groups = d_ssm // SSM_GROUP
    n_state = SSM_STATE
    cw = S5_CHUNK * SSM_GROUP
    n_rows = seq // S5_CHUNK
    n_tiles = n_rows // S5_SEGS
    lanes = 128
    gpb = lanes // SSM_GROUP
    n_blocks = d_ssm // lanes
    bw = gpb * cw
    sw = gpb * 4 * n_state
    seg_stride = n_tiles * S5_CHUNK

    lag, st, wo, ap, pw = _s5_tables(fwd, bwd, n_tiles)

    def per_block(t):
        return pl.BlockSpec((t.shape[0], gpb) + t.shape[2:], lambda i: (0, i) + (0,) * (t.ndim - 2))

    y_intra, s_loc = pl.pallas_call(
        functools.partial(_s5_a_kernel, n_tiles=n_tiles, seg_stride=seg_stride),
        grid=(n_blocks,),
        in_specs=[
            pl.BlockSpec((seq, lanes), lambda i: (0, i)),
            per_block(lag), *[per_block(t) for t in st],
        ],
        out_specs=[
            pl.BlockSpec((n_rows, bw), lambda i: (0, i)),
            pl.BlockSpec((n_rows, sw), lambda i: (0, i)),
        ],
        out_shape=[
            jax.ShapeDtypeStruct((n_rows, n_blocks * bw), F32),
            jax.ShapeDtypeStruct((n_rows, n_blocks * sw), F32),
        ],
        scratch_shapes=[
            pltpu.VMEM((bw, bw), BF16),
            pltpu.VMEM((gpb, cw, cw), BF16),
            pltpu.VMEM((gpb // 2, 2 * cw, 8 * n_state), BF16),
            pltpu.VMEM((n_rows, S5_CHUNK * lanes), BF16),
        ],
        compiler_params=_params("arbitrary"),
        name="s5_a",
    )(u, lag, *st)

    pl_lanes = gpb * n_state
    return pl.pallas_call(
        functools.partial(_s5_bc_kernel, n_tiles=n_tiles, seg_stride=seg_stride),
        grid=(n_blocks,),
        in_specs=[
            pl.BlockSpec((n_rows, sw), lambda i: (0, i)),
            pl.BlockSpec((n_rows, bw), lambda i: (0, i)),
            *[per_block(t) for t in wo],
            pl.BlockSpec((8, pl_lanes), lambda i: (0, i)),
            pl.BlockSpec((4, n_tiles, pl_lanes), lambda i: (0, 0, i)),
        ],
        out_specs=pl.BlockSpec((seq, lanes), lambda i: (0, i)),
        out_shape=jax.ShapeDtypeStruct((seq, d_ssm), F32),
        scratch_shapes=[
            pltpu.VMEM((n_rows, sw), F32),
            pltpu.VMEM((bw, bw), BF16),
            pltpu.VMEM((n_rows, bw), F32),
            pltpu.VMEM((gpb // 2, 2 * cw, 8 * n_state), BF16),
            pltpu.VMEM((n_rows, sw), BF16),
        ],
        compiler_params=_params("arbitrary"),
        name="s5_bc",
    )(s_loc, y_intra, *wo, ap, pw)


def _s5_post_kernel(y_ref, u_ref, d_ref, w_ref, b_ref, g_ref, o_ref):
    y = y_ref[...] + d_ref[...] * u_ref[...]
    c0 = np.float32(np.sqrt(2.0 / np.pi))
    y = 0.5 * y * (1.0 + jnp.tanh(c0 * (y + np.float32(0.044715) * (y * y * y))))
    z = jnp.dot(y.astype(BF16), w_ref[...].astype(BF16), preferred_element_type=F32) + b_ref[...]
    o = y * (1.0 / (1.0 + jnp.exp(-z)))
    o_ref[...] = _rms(o, g_ref[...]).astype(BF16)


def _s5_post(y, u, d_skip, w_glu, b_glu, g):
    seq, d = y.shape
    tm = min(512, seq)
    row = lambda i: (i, 0)
    fix = lambda i: (0, 0)
    return pl.pallas_call(
        _s5_post_kernel,
        grid=(seq // tm,),
        in_specs=[
            pl.BlockSpec((tm, d), row), pl.BlockSpec((tm, d), row), pl.BlockSpec((1, d), fix),
            pl.BlockSpec((d, d), fix), pl.BlockSpec((1, d), fix), pl.BlockSpec((1, d), fix),
        ],
        out_specs=pl.BlockSpec((tm, d), row),
        out_shape=jax.ShapeDtypeStruct((seq, d), BF16),
        compiler_params=_params("arbitrary"),
        name="s5_post",
    )(y, u, d_skip.reshape(1, d), w_glu, b_glu.reshape(1, d), g.reshape(1, d))


def _na_bias_table(rpb):
    n_heads = rpb.shape[0]
    cols = np.arange(GRID_W)
    col_start = np.clip(cols - WIN_COLS // 2, 0, GRID_W - WIN_COLS)
    key_cols = np.arange(GRID_W)
    in_win = (key_cols[None, :] >= col_start[:, None]) & (key_cols[None, :] < col_start[:, None] + WIN_COLS)
    dx = key_cols[None, :] - cols[:, None] + (WIN_COLS - 1)
    pick_x = (dx[:, :, None] == np.arange(2 * WIN_COLS - 1)).astype(np.float32)
    halves = []
    for half in range(2):
        pad = ((0, 0), (half * GRID_W, (1 - half) * GRID_W))
        pick = np.pad(pick_x, pad + ((0, 0),))
        keep = np.pad(in_win, pad, constant_values=True)
        b = jnp.einsum('hyx,ckx->hyck', rpb.astype(F32) * LOG2_E, pick, precision=lax.Precision.HIGHEST)
        b = jnp.where(keep[None, None], b, MASK_NEG)
        b = b.reshape(n_heads // HEADS_PER_DOT, HEADS_PER_DOT, b.shape[1], GRID_W, 2 * GRID_W)
        halves.append(jnp.swapaxes(b, 1, 2).reshape(n_heads // HEADS_PER_DOT, b.shape[2],
                                                    HEADS_PER_DOT * GRID_W, 2 * GRID_W))
    return halves


def _na_kernel(q_ref, k_ref, v_ref, b0_ref, b1_ref, g_ref, o_ref, *, rows, rows_per_step):
    n_keys = WIN_ROWS * GRID_W
    d_na = k_ref.shape[-1]
    pw = HEADS_PER_DOT * NA_HEAD_DIM
    row_head = lax.broadcasted_iota(jnp.int32, (HEADS_PER_DOT * GRID_W, pw), 0) // GRID_W
    col_head = lax.broadcasted_iota(jnp.int32, (HEADS_PER_DOT * GRID_W, pw), 1) // NA_HEAD_DIM
    diag = row_head == col_head
    out_head = lax.broadcasted_iota(jnp.int32, (GRID_W, pw), 1) // NA_HEAD_DIM
    first = pl.program_id(0) * rows_per_step
    block_start = jnp.clip(first - WIN_ROWS // 2, 0, rows - k_ref.shape[0])

    def one_row(i, _):
        r = first + i
        win_start = jnp.clip(r - WIN_ROWS // 2, 0, rows - WIN_ROWS)
        dy0 = win_start - r + (WIN_ROWS - 1)
        k = k_ref[pl.ds(win_start - block_start, WIN_ROWS)].reshape(n_keys, d_na)
        v = v_ref[pl.ds(win_start - block_start, WIN_ROWS)].reshape(n_keys, d_na)
        q_rows = pl.ds(pl.multiple_of(i * GRID_W, GRID_W), GRID_W)
        outs = []
        for p in range(d_na // pw):
            sl = slice(p * pw, (p + 1) * pw)
            q4 = q_ref[q_rows, sl]
            qbd = jnp.where(diag, jnp.concatenate([q4] * HEADS_PER_DOT, axis=0), jnp.zeros((), BF16))
            s = lax.dot_general(qbd, k[:, sl], (((1,), (1,)), ((), ())), preferred_element_type=F32)
            s = s + jnp.concatenate([b0_ref[p, dy0 + 2 * j] + b1_ref[p, dy0 + 2 * j + 1]
                                     for j in range(WIN_ROWS // 2)], axis=1)
            m = jnp.max(s, axis=-1, keepdims=True)
            e = jnp.exp2(s - m)
            l = jnp.sum(e, axis=-1, keepdims=True)
            o = jnp.dot(e.astype(BF16), v[:, sl], preferred_element_type=F32) / l
            acc = jnp.zeros((GRID_W, pw), F32)
            for h in range(HEADS_PER_DOT):
                acc = acc + jnp.where(out_head == h, o[h * GRID_W:(h + 1) * GRID_W], 0.0)
            outs.append(acc)
        y = jnp.concatenate(outs, axis=1)
        o_ref[q_rows, :] = _rms(y, g_ref[...]).astype(BF16)
        return 0

    lax.fori_loop(0, rows_per_step, one_row, 0, unroll=True)


def _neighbourhood_attention(qkv, rpb, g):
    seq = qkv.shape[0]
    d_na = qkv.shape[1] // 3
    rows = seq // GRID_W
    bias = _na_bias_table(rpb)
    qkv3 = qkv.reshape(rows, GRID_W, 3 * d_na)

    rps = 4
    key_rows = 2 * WIN_ROWS

    def block_start(b):
        return jnp.clip(b * rps - WIN_ROWS // 2, 0, rows - key_rows)

    window = (pl.Element(key_rows), pl.Element(GRID_W), pl.Element(d_na))

    return pl.pallas_call(
        functools.partial(_na_kernel, rows=rows, rows_per_step=rps),
        grid=(rows // rps,),
        in_specs=[
            pl.BlockSpec((rps * GRID_W, d_na), lambda b: (b, 0)),
            pl.BlockSpec(window, lambda b: (block_start(b), 0, d_na)),
            pl.BlockSpec(window, lambda b: (block_start(b), 0, 2 * d_na)),
            pl.BlockSpec(bias[0].shape, lambda b: (0, 0, 0, 0)),
            pl.BlockSpec(bias[1].shape, lambda b: (0, 0, 0, 0)),
            pl.BlockSpec((1, d_na), lambda b: (0, 0)),
        ],
        out_specs=pl.BlockSpec((rps * GRID_W, d_na), lambda b: (b, 0)),
        out_shape=jax.ShapeDtypeStruct((seq, d_na), BF16),
        compiler_params=_params("arbitrary"),
        name="na",
    )(qkv, qkv3, qkv3, bias[0], bias[1], g.reshape(1, d_na))


def _outproj_kernel(a_ref, b_ref, w_ref, x_ref, o_ref):
    da = a_ref.shape[-1]
    acc = jnp.dot(a_ref[...], w_ref[:da, :].astype(BF16), preferred_element_type=F32)
    acc = acc + jnp.dot(b_ref[...], w_ref[da:, :].astype(BF16), preferred_element_type=F32)
    o_ref[...] = x_ref[...] + acc


def _out_proj(y_ssm, y_na, w_out, x):
    seq, d_model = x.shape
    da, db = y_ssm.shape[1], y_na.shape[1]
    tm = min(2048, seq)
    tn = 512
    return pl.pallas_call(
        _outproj_kernel,
        grid=(seq // tm, d_model // tn),
        in_specs=[
            pl.BlockSpec((tm, da), lambda i, j: (i, 0)),
            pl.BlockSpec((tm, db), lambda i, j: (i, 0)),
            pl.BlockSpec((da + db, tn), lambda i, j: (0, j)),
            pl.BlockSpec((tm, tn), lambda i, j: (i, j)),
        ],
        out_specs=pl.BlockSpec((tm, tn), lambda i, j: (i, j)),
        out_shape=jax.ShapeDtypeStruct((seq, d_model), F32),
        compiler_params=_params("arbitrary", "arbitrary"),
        name="out_proj",
    )(y_ssm, y_na, w_out, x)


def _router_kernel(x_ref, g_ref, wt_ref, h_ref, a_ref):
    h = _rms(x_ref[...], g_ref[...])
    h_hi = h.astype(BF16)
    h_ref[...] = h_hi
    h_lo = (h - h_hi.astype(F32)).astype(BF16)
    w = wt_ref[...]
    w_hi = w.astype(BF16)
    w_lo = (w - w_hi.astype(F32)).astype(BF16)
    n_exp = w.shape[0]
    nt = (((1,), (1,)), ((), ()))
    both = lax.dot_general(jnp.concatenate([w_hi, w_lo], axis=0), h_hi, nt, preferred_element_type=F32)
    logits = both[:n_exp] + both[n_exp:] + lax.dot_general(w_hi, h_lo, nt, preferred_element_type=F32)
    m = jnp.max(logits, axis=0, keepdims=True)
    e = jnp.exp(logits - m)
    a_ref[...] = e / jnp.sum(e, axis=0, keepdims=True)


def _router(x1, g, w_router):
    seq, d_model = x1.shape
    n_exp = w_router.shape[1]
    tm = min(512, seq)
    return pl.pallas_call(
        _router_kernel,
        grid=(seq // tm,),
        in_specs=[
            pl.BlockSpec((tm, d_model), lambda i: (i, 0)),
            pl.BlockSpec((1, d_model), lambda i: (0, 0)),
            pl.BlockSpec((n_exp, d_model), lambda i: (0, 0)),
        ],
        out_specs=[
            pl.BlockSpec((tm, d_model), lambda i: (i, 0)),
            pl.BlockSpec((n_exp, tm), lambda i: (0, i)),
        ],
        out_shape=[
            jax.ShapeDtypeStruct((seq, d_model), BF16),
            jax.ShapeDtypeStruct((n_exp, seq), F32),
        ],
        compiler_params=_params("arbitrary"),
        name="router",
    )(x1, g.reshape(1, d_model), w_router.T)


def _topk_kernel(a_ref, posw_ref, gate_ref, ws_ref, nr_ref, *, cap, blk, win):
    a = a_ref[...]
    n_exp, seq = a.shape
    n_blk = seq // blk
    bits = pltpu.bitcast(a, jnp.int32)

    def bit_step(i, thr):
        cand = thr | jnp.left_shift(jnp.int32(1), 30 - i)
        cnt = jnp.sum((bits >= cand).astype(jnp.int32), axis=-1, keepdims=True)
        return jnp.where(cnt >= cap, cand, thr)

    thr = lax.fori_loop(0, 31, bit_step, jnp.zeros((n_exp, 1), jnp.int32))
    gt = bits > thr
    eq = bits == thr
    need = cap - jnp.sum(gt.astype(jnp.int32), axis=-1, keepdims=True)

    tri = (lax.broadcasted_iota(jnp.int32, (blk, blk), 0)
           <= lax.broadcasted_iota(jnp.int32, (blk, blk), 1)).astype(BF16)
    blk_of_tok = lax.broadcasted_iota(jnp.int32, (seq, n_blk), 0) // blk
    tok_to_blk = (blk_of_tok == lax.broadcasted_iota(jnp.int32, (seq, n_blk), 1)).astype(BF16)
    blk_before = (lax.broadcasted_iota(jnp.int32, (n_blk, n_blk), 0)
                  < lax.broadcasted_iota(jnp.int32, (n_blk, n_blk), 1)).astype(BF16)
    erow = lax.broadcasted_iota(jnp.int32, (2 * n_blk, seq), 0)
    ecol = lax.broadcasted_iota(jnp.int32, (2 * n_blk, seq), 1) // blk
    expand = jnp.where(erow == ecol, 32.0, jnp.where(erow - n_blk == ecol, 1.0, 0.0)).astype(BF16)

    def prefix_counts(mask):
        mb = jnp.where(mask, 1.0, 0.0).astype(BF16)
        local = jnp.concatenate(
            [jnp.dot(mb[:, b * blk:(b + 1) * blk], tri, preferred_element_type=F32) for b in range(n_blk)],
            axis=1)
        per_blk = jnp.dot(mb, tok_to_blk, preferred_element_type=F32)
        start = jnp.dot(per_blk.astype(BF16), blk_before, preferred_element_type=F32)
        hi = jnp.floor(start * (1.0 / 32.0))
        parts = jnp.concatenate([hi, start - 32.0 * hi], axis=1).astype(BF16)
        start_tok = jnp.dot(parts, expand, preferred_element_type=F32)
        return local + start_tok, start, start_tok, per_blk

    eq_incl, _, _, _ = prefix_counts(eq)
    sel = gt | (eq & (eq_incl - 1.0 < need.astype(F32)))
    incl, start, start_tok, per_blk = prefix_counts(sel)

    def window(s):
        return jnp.floor(s * (1.0 / MOE_WIN_ALIGN)) * MOE_WIN_ALIGN

    posw_ref[...] = jnp.where(sel, (incl - 1.0 - window(start_tok)).astype(jnp.int32), -1)
    gate_ref[...] = jnp.where(sel, a, 0.0)
    ws_ref[...] = window(start).astype(jnp.int32)
    span = start - window(start) + per_blk
    rounds = jnp.floor((span + float(win - 1)) * (1.0 / win))
    nr_ref[...] = jnp.max(rounds, axis=0, keepdims=True).astype(jnp.int32)


def _topk(aff_t, cap, blk, win):
    n_exp, seq = aff_t.shape
    n_blk = seq // blk
    full = lambda *_: (0, 0)
    return pl.pallas_call(
        functools.partial(_topk_kernel, cap=cap, blk=blk, win=win),
        grid=(1,),
        in_specs=[pl.BlockSpec((n_exp, seq), full)],
        out_specs=[pl.BlockSpec((n_exp, seq), full), pl.BlockSpec((n_exp, seq), full),
                   pl.BlockSpec((n_exp, n_blk), full), pl.BlockSpec((1, n_blk), full)],
        out_shape=[
            jax.ShapeDtypeStruct((n_exp, seq), jnp.int32),
            jax.ShapeDtypeStruct((n_exp, seq), F32),
            jax.ShapeDtypeStruct((n_exp, n_blk), jnp.int32),
            jax.ShapeDtypeStruct((1, n_blk), jnp.int32),
        ],
        compiler_params=_params("arbitrary"),
        name="topk",
    )(aff_t)


def _window(ws_ref, e, b, r, n_blk, win, cap):
    ws = ws_ref[e * n_blk + b] + r * win
    start = jnp.minimum(ws, cap - win)
    return pl.multiple_of(start, MOE_WIN_ALIGN), ws - start


def _gather_kernel(ws_ref, nr_ref, h_ref, rel_ref, xe_ref, *, blk, win, n_blk):
    n_exp, cap, _ = xe_ref.shape
    xe_ref[...] = jnp.zeros_like(xe_ref)
    slot = lax.broadcasted_iota(jnp.int32, (win, blk), 0)

    def block(b, _):
        rows = h_ref[pl.ds(pl.multiple_of(b * blk, blk), blk), :]
        rel = rel_ref[b]

        def one_round(r, _):
            starts, hots = [], []
            for e in range(n_exp):
                start, shift = _window(ws_ref, e, b, r, n_blk, win, cap)
                relr = rel[e:e + 1, :] - r * win
                key = jnp.where(relr >= 0, relr + shift, -1)
                hots.append(jnp.where(slot == key, 1.0, 0.0).astype(BF16))
                starts.append(start)
            res = jnp.dot(jnp.concatenate(hots, axis=0), rows, preferred_element_type=F32)
            for e in range(n_exp):
                dst = pl.ds(starts[e], win)
                xe_ref[e, dst, :] = (xe_ref[e, dst, :].astype(F32) + res[e * win:(e + 1) * win]).astype(BF16)
            return 0

        lax.fori_loop(0, nr_ref[b], one_round, 0)
        return 0

    lax.fori_loop(0, n_blk, block, 0)


def _moe_gather(ws_flat, n_rounds, h2, rel3, cap, win):
    seq, d_model = h2.shape
    n_blk, n_exp, blk = rel3.shape
    dq = d_model // 4
    grid_spec = pltpu.PrefetchScalarGridSpec(
        num_scalar_prefetch=2,
        grid=(4,),
        in_specs=[
            pl.BlockSpec((seq, dq), lambda c, ws, nr: (0, c)),
            pl.BlockSpec((n_blk, n_exp, blk), lambda c, ws, nr: (0, 0, 0)),
        ],
        out_specs=pl.BlockSpec((n_exp, cap, dq), lambda c, ws, nr: (0, 0, c)),
    )
    return pl.pallas_call(
        functools.partial(_gather_kernel, blk=blk, win=win, n_blk=n_blk),
        grid_spec=grid_spec,
        out_shape=jax.ShapeDtypeStruct((n_exp, cap, d_model), BF16),
        compiler_params=_params("arbitrary"),
        name="moe_gather",
    )(ws_flat, n_rounds, h2, rel3)


def _ffn_kernel(x_ref, wg_ref, wu_ref, wd_ref, y_ref, act_ref, *, n_f):
    s = pl.program_id(1)
    tf = wg_ref.shape[-1]

    @pl.when(s < n_f)
    def _():
        x = x_ref[0]
        g = jnp.dot(x, wg_ref[0].astype(BF16), preferred_element_type=F32)
        u = jnp.dot(x, wu_ref[0].astype(BF16), preferred_element_type=F32)
        act_ref[s] = (g * (1.0 / (1.0 + jnp.exp(-g))) * u).astype(BF16)

    @pl.when(s >= n_f)
    def _():
        acc = jnp.dot(act_ref[0], wd_ref[0, 0:tf, :].astype(BF16), preferred_element_type=F32)
        for f in range(1, n_f):
            acc = acc + jnp.dot(act_ref[f], wd_ref[0, f * tf:(f + 1) * tf, :].astype(BF16),
                                preferred_element_type=F32)
        y_ref[0] = acc.astype(BF16)


def _moe_ffn(xe, w_gate, w_up, w_down):
    n_exp, cap, d_model = xe.shape
    d_ff = w_gate.shape[-1]
    tf = 512
    tn = 1024
    n_f, n_n = d_ff // tf, d_model // tn
    up_tile = lambda e, s: (e, 0, jnp.minimum(s, n_f - 1))
    down_tile = lambda e, s: (e, 0, jnp.maximum(s - n_f, 0))
    return pl.pallas_call(
        functools.partial(_ffn_kernel, n_f=n_f),
        grid=(n_exp, n_f + n_n),
        in_specs=[
            pl.BlockSpec((1, cap, d_model), lambda e, s: (e, 0, 0)),
            pl.BlockSpec((1, d_model, tf), up_tile),
            pl.BlockSpec((1, d_model, tf), up_tile),
            pl.BlockSpec((1, d_ff, tn), down_tile),
        ],
        out_specs=pl.BlockSpec((1, cap, tn), down_tile),
        out_shape=jax.ShapeDtypeStruct((n_exp, cap, d_model), BF16),
        scratch_shapes=[pltpu.VMEM((n_f, cap, tf), BF16)],
        compiler_params=_params("arbitrary", "arbitrary"),
        name="moe_ffn",
    )(xe, w_gate, w_up, w_down)


def _combine_kernel(ws_ref, nr_ref, ye_ref, x_ref, rel_ref, gate_ref, o_ref, *, blk, win, n_blk):
    n_exp, cap, _ = ye_ref.shape
    sub = x_ref.shape[0] // blk
    slot = lax.broadcasted_iota(jnp.int32, (win, blk), 0)

    def block(s, _):
        b = pl.program_id(1) * sub + s
        tok = pl.ds(pl.multiple_of(s * blk, blk), blk)
        rel = rel_ref[b]
        gate = gate_ref[b]

        def one_round(r, acc):
            gates, wins = [], []
            for e in range(n_exp):
                start, shift = _window(ws_ref, e, b, r, n_blk, win, cap)
                relr = rel[e:e + 1, :] - r * win
                key = jnp.where(relr >= 0, relr + shift, -1)
                gates.append(jnp.where(slot == key, gate[e:e + 1, :], 0.0).astype(BF16))
                wins.append(ye_ref[e, pl.ds(start, win), :])
            res = lax.dot_general(jnp.concatenate(gates, axis=0), jnp.concatenate(wins, axis=0),
                                  (((0,), (0,)), ((), ())), preferred_element_type=F32)
            return acc + res

        o_ref[tok, :] = lax.fori_loop(1, nr_ref[b], one_round, one_round(0, x_ref[tok, :]))
        return 0

    lax.fori_loop(0, sub, block, 0)


def _moe_combine(ws_flat, n_rounds, ye, x1, rel3, gate3, win):
    seq, d_model = x1.shape
    n_exp, cap, _ = ye.shape
    n_blk, _, blk = rel3.shape
    dq = d_model // 4
    tile = min(4, n_blk) * blk
    whole = lambda c, t, ws, nr: (0, 0, 0)
    grid_spec = pltpu.PrefetchScalarGridSpec(
        num_scalar_prefetch=2,
        grid=(4, seq // tile),
        in_specs=[
            pl.BlockSpec((n_exp, cap, dq), lambda c, t, ws, nr: (0, 0, c)),
            pl.BlockSpec((tile, dq), lambda c, t, ws, nr: (t, c)),
            pl.BlockSpec(rel3.shape, whole),
            pl.BlockSpec(gate3.shape, whole),
        ],
        out_specs=pl.BlockSpec((tile, dq), lambda c, t, ws, nr: (t, c)),
    )
    return pl.pallas_call(
        functools.partial(_combine_kernel, blk=blk, win=win, n_blk=n_blk),
        grid_spec=grid_spec,
        out_shape=jax.ShapeDtypeStruct((seq, d_model), F32),
        compiler_params=_params("arbitrary", "arbitrary"),
        name="moe_combine",
    )(ws_flat, n_rounds, ye, x1, rel3, gate3)


def _final_norm_kernel(x_ref, g_ref, o_ref):
    o_ref[...] = _rms(x_ref[...], g_ref[...])


def _final_norm(x, g):
    seq, d_model = x.shape
    tm = min(512, seq)
    return pl.pallas_call(
        _final_norm_kernel,
        grid=(seq // tm,),
        in_specs=[pl.BlockSpec((tm, d_model), lambda i: (i, 0)), pl.BlockSpec((1, d_model), lambda i: (0, 0))],
        out_specs=pl.BlockSpec((tm, d_model), lambda i: (i, 0)),
        out_shape=jax.ShapeDtypeStruct((seq, d_model), F32),
        compiler_params=_params("arbitrary"),
        name="final_norm",
    )(x, g.reshape(1, d_model))


def _layer(x, norm_mix_g, w_in, fwd, bwd, ssm_d, w_glu, b_glu, na_rpb, g_ssm_out, g_na_out, w_out,
           norm_ffn_g, w_router, w_gate, w_up, w_down):
    seq, d_model = x.shape
    d_ssm = ssm_d.shape[0]
    d_na = g_na_out.shape[0]
    n_exp = w_router.shape[1]
    cap = EC_CAPACITY_FACTOR * seq // n_exp
    blk = min(MOE_TOK_BLOCK, cap // 2)
    win = min(MOE_WIN, cap)

    u, qkv = _in_proj(x, norm_mix_g, w_in, d_ssm, d_na)
    y_ssm = _s5_post(_s5_scan(u, fwd, bwd), u, ssm_d, w_glu, b_glu, g_ssm_out)
    y_na = _neighbourhood_attention(qkv, na_rpb, g_na_out)
    x1 = _out_proj(y_ssm, y_na, w_out, x)

    h2, aff_t = _router(x1, norm_ffn_g, w_router)
    rel, gate, ws, n_rounds = _topk(aff_t, cap, blk, win)
    ws_flat = ws.reshape(-1)
    n_rounds = n_rounds.reshape(-1)
    rel3 = jnp.swapaxes(rel.reshape(n_exp, seq // blk, blk), 0, 1)
    gate3 = jnp.swapaxes(gate.reshape(n_exp, seq // blk, blk), 0, 1)
    xe = _moe_gather(ws_flat, n_rounds, h2, rel3, cap, win)
    ye = _moe_ffn(xe, w_gate, w_up, w_down)
    return _moe_combine(ws_flat, n_rounds, ye, x1, rel3, gate3, win)


def kernel(x, norm_mix_g, w_in, a_re_fwd, a_im_fwd, log_dt_fwd, b_re_fwd, b_im_fwd, c_re_fwd, c_im_fwd, a_re_bwd, a_im_bwd, log_dt_bwd, b_re_bwd, b_im_bwd, c_re_bwd, c_im_bwd, ssm_d, w_glu, b_glu, na_rpb, g_ssm_out, g_na_out, w_out, norm_ffn_g, w_router, w_gate, w_up, w_down, norm_final_g):
    bsz = x.shape[0]
    depth = w_in.shape[0]
    outs = []
    for b in range(bsz):
        xb = x[b]
        for l in range(depth):
            fwd = (a_re_fwd[l], a_im_fwd[l], log_dt_fwd[l], b_re_fwd[l], b_im_fwd[l], c_re_fwd[l], c_im_fwd[l])
            bwd = (a_re_bwd[l], a_im_bwd[l], log_dt_bwd[l], b_re_bwd[l], b_im_bwd[l], c_re_bwd[l], c_im_bwd[l])
            xb = _layer(xb, norm_mix_g[l], w_in[l], fwd, bwd, ssm_d[l], w_glu[l], b_glu[l], na_rpb[l],
                        g_ssm_out[l], g_na_out[l], w_out[l], norm_ffn_g[l], w_router[l],
                        w_gate[l], w_up[l], w_down[l])
        outs.append(_final_norm(xb, norm_final_g))
    return jnp.stack(outs)
```

```python
import functools

import numpy as np
import jax
import jax.numpy as jnp
from jax import lax
from jax.experimental import pallas as pl
from jax.experimental.pallas import tpu as pltpu

F32 = jnp.float32
BF16 = jnp.bfloat16

RMS_EPS = 1e-6
SSM_GROUP = 16
SSM_STATE = 64
NA_HEADS = 16
NA_HEAD_DIM = 64
GRID_W = 64
WIN_ROWS = 8
WIN_COLS = 16
N_EXPERTS = 16
EC_CAPACITY_FACTOR = 2

S5_CHUNK = 16
S5_SEGS = 8
HEADS_PER_DOT = 4
MOE_TOK_BLOCK = 256
MOE_WIN_ALIGN = 16
MOE_WIN = 64
MASK_NEG = -1e30
LOG2_E = float(np.log2(np.e))

VMEM_LIMIT_BYTES = 56 * 1024 * 1024


def _params(*semantics):
    return pltpu.CompilerParams(dimension_semantics=semantics, vmem_limit_bytes=VMEM_LIMIT_BYTES)


def _rms(x, g):
    ms = jnp.mean(x * x, axis=-1, keepdims=True)
    return x * lax.rsqrt(ms + RMS_EPS) * g


def _inproj_kernel(x_ref, g_ref, w_ref, u_ref, qkv_ref, h_scr, *, n_u, n_q, q_scale):
    j = pl.program_id(1)

    @pl.when(j == 0)
    def _():
        h_scr[...] = _rms(x_ref[...], g_ref[...]).astype(BF16)

    def project():
        return jnp.dot(h_scr[...], w_ref[...].astype(BF16), preferred_element_type=F32)

    @pl.when(j < n_u)
    def _():
        u_ref[...] = project()

    @pl.when(j >= n_u)
    def _():
        scale = jnp.where(j < n_u + n_q, q_scale, 1.0).astype(F32)
        qkv_ref[...] = (project() * scale).astype(BF16)


def _in_proj(x, g, w_in, d_ssm, d_na):
    seq, d_model = x.shape
    tm = min(1024, seq)
    tn = 1024
    n_u, n_q = d_ssm // tn, d_na // tn
    n_cols = w_in.shape[1] // tn
    kern = functools.partial(_inproj_kernel, n_u=n_u, n_q=n_q, q_scale=NA_HEAD_DIM ** -0.5 * LOG2_E)
    return pl.pallas_call(
        kern,
        grid=(seq // tm, n_cols),
        in_specs=[
            pl.BlockSpec((tm, d_model), lambda i, j: (i, 0)),
            pl.BlockSpec((1, d_model), lambda i, j: (0, 0)),
            pl.BlockSpec((d_model, tn), lambda i, j: (0, j)),
        ],
        out_specs=[
            pl.BlockSpec((tm, tn), lambda i, j: (i, jnp.minimum(j, n_u - 1))),
            pl.BlockSpec((tm, tn), lambda i, j: (i, jnp.maximum(j - n_u, 0))),
        ],
        out_shape=[
            jax.ShapeDtypeStruct((seq, d_ssm), F32),
            jax.ShapeDtypeStruct((seq, 3 * d_na), BF16),
        ],
        scratch_shapes=[pltpu.VMEM((tm, d_model), BF16)],
        compiler_params=_params("arbitrary", "arbitrary"),
        name="in_proj",
    )(x, g.reshape(1, d_model), w_in)


def _cmul(ar, ai, br, bi):
    return ar * br - ai * bi, ar * bi + ai * br


def _s5_tables(fwd, bwd, n_tiles):
    t_len = S5_CHUNK
    a_re, a_im, log_dt, b_re, b_im, c_re, c_im = (jnp.stack([f, b]).astype(F32) for f, b in zip(fwd, bwd))
    _, n_grp, n_st = a_re.shape
    n_ch = b_re.shape[-1]
    dt = jnp.exp(log_dt)[:, :, None]
    xr, xi = a_re * dt, a_im * dt
    steps = np.arange(t_len)

    def power(x_r, x_i, exps):
        e = jnp.asarray(exps, F32).reshape(exps.shape + (1,) * (x_r.ndim - 1))
        mag = jnp.exp(x_r[None] * e)
        return mag * jnp.cos(x_i[None] * e), mag * jnp.sin(x_i[None] * e)

    def per_step(exps):
        p_r, p_i = power(xr, xi, exps)
        return jnp.transpose(p_r, (1, 2, 0, 3)), jnp.transpose(p_i, (1, 2, 0, 3))

    a1_r, a1_i = power(xr, xi, np.ones((1, 2)))
    nr, ni = a1_r[0] - 1.0, a1_i[0]
    den = a_re * a_re + a_im * a_im
    qr, qi = (nr * a_re + ni * a_im) / den, (ni * a_re - nr * a_im) / den
    bb_r, bb_i = _cmul(qr[:, :, None, :], qi[:, :, None, :], jnp.swapaxes(b_re, 2, 3), jnp.swapaxes(b_im, 2, 3))
    am_r, am_i = per_step(np.stack([steps, steps[::-1]], axis=1))
    y_r, y_i = _cmul(c_re[:, :, None], c_im[:, :, None], am_r[:, :, :, None, :], am_i[:, :, :, None, :])
    y = jnp.concatenate([y_r, y_i], axis=-1).reshape(2, n_grp, t_len * n_ch, 2 * n_st)
    lag = jnp.einsum('dgck,dgqk->dgcq', jnp.concatenate([bb_r, -bb_i], axis=-1), y,
                     precision=lax.Precision.HIGH)
    st = per_step(np.stack([t_len - 1 - steps, steps], axis=1)) + (bb_r, bb_i)
    wo = per_step(np.stack([steps + 1, t_len - steps], axis=1)) + (c_re, c_im)
    xrf, xif = xr.reshape(2, -1), xi.reshape(2, -1)
    tiles = np.arange(n_tiles)
    ends = power(xrf, xif, np.array([[t_len, t_len], [t_len * n_tiles, t_len * n_tiles]]))
    ap = jnp.stack(ends, axis=2).reshape(8, -1)
    at_r, at_i = power(xrf, xif, t_len * np.stack([tiles, tiles[::-1]], axis=1))
    pw = jnp.transpose(jnp.stack([at_r, at_i], axis=2), (1, 2, 0, 3)).reshape(4, n_tiles, -1)
    return lag, st, wo, ap, pw


def _s5_a_kernel(u_ref, lag_ref, pe_r_ref, pe_i_ref, bb_r_ref, bb_i_ref, y_ref, s_ref, perm, w_grp, e_pair, ucat,
                 *, n_tiles, seg_stride):
    n_grp = lag_ref.shape[1]
    cw = S5_CHUNK * SSM_GROUP
    n_plane, n_state = 4, pe_r_ref.shape[-1]
    lanes = u_ref.shape[-1]
    pc = n_plane * 2 * n_state

    @pl.when(pl.program_id(0) == 0)
    def _():
        e_pair[...] = jnp.zeros_like(e_pair)
        rl = lax.broadcasted_iota(jnp.int32, (cw, n_grp * cw), 0)
        cl = lax.broadcasted_iota(jnp.int32, (cw, n_grp * cw), 1)
        for blk in range(n_grp):
            row = blk * cw + rl
            tgt = ((row % lanes) // SSM_GROUP) * cw + (row // lanes) * SSM_GROUP + row % SSM_GROUP
            perm[blk * cw:(blk + 1) * cw, :] = jnp.where(cl == tgt, 1.0, 0.0).astype(BF16)

    for g in range(n_grp):
        blank = jnp.zeros((SSM_GROUP, cw), F32)
        strip = jnp.concatenate([blank, lag_ref[0, g], lag_ref[1, g], blank], axis=1)
        for t in range(S5_CHUNK):
            r0 = t * SSM_GROUP
            fwd_lo = cw - t * SSM_GROUP
            bwd_lo = 2 * cw + (S5_CHUNK - 1 - t) * SSM_GROUP
            piece = strip[:, fwd_lo:fwd_lo + cw] + strip[:, bwd_lo:bwd_lo + cw]
            w_grp[g, r0:r0 + SSM_GROUP, :] = piece.astype(BF16)
            for d in range(2):
                planes = _cmul(pe_r_ref[d, g, t:t + 1, :], pe_i_ref[d, g, t:t + 1, :], bb_r_ref[d, g], bb_i_ref[d, g])
                for ri in range(2):
                    c0 = (2 * d + ri) * 2 * n_state + (g % 2) * n_state
                    rows = slice((g % 2) * cw + r0, (g % 2) * cw + r0 + SSM_GROUP)
                    e_pair[g // 2, rows, c0:c0 + n_state] = planes[ri].astype(BF16)

    def gather(jj, _):
        for t in range(S5_CHUNK):
            rows = [u_ref[pl.ds(S5_CHUNK * (2 * jj + h) + t, S5_SEGS, stride=seg_stride), :] for h in range(2)]
            dst = pl.ds(pl.multiple_of(jj * 2 * S5_SEGS, 2 * S5_SEGS), 2 * S5_SEGS)
            ucat[dst, t * lanes:(t + 1) * lanes] = jnp.concatenate(rows, axis=0).astype(BF16)
        return 0

    lax.fori_loop(0, n_tiles // 2, gather, 0)
    ug = jnp.dot(ucat[...], perm[...], preferred_element_type=F32).astype(BF16)
    for g in range(n_grp):
        y_ref[:, g * cw:(g + 1) * cw] = jnp.dot(ug[:, g * cw:(g + 1) * cw], w_grp[g], preferred_element_type=F32)
    for q in range(n_grp // 2):
        s_ref[:, q * pc:(q + 1) * pc] = jnp.dot(ug[:, q * 2 * cw:(q + 1) * 2 * cw], e_pair[q],
                                                preferred_element_type=F32)


def _s5_bc_kernel(s_ref, yi_ref, po_r_ref, po_i_ref, c_r_ref, c_i_ref, ap_ref, pw_ref, y_ref, sin, perm, ynat, c_blk,
                  sin_bf, *, n_tiles, seg_stride):
    n_plane = 4
    _, n_grp, _, n_state = po_r_ref.shape
    cw = S5_CHUNK * SSM_GROUP
    n_pair = n_grp // 2
    pw_ = 2 * n_state
    pc = n_plane * pw_
    lanes = y_ref.shape[-1]

    @pl.when(pl.program_id(0) == 0)
    def _():
        c_blk[...] = jnp.zeros_like(c_blk)
        rl = lax.broadcasted_iota(jnp.int32, (cw, n_grp * cw), 0)
        cl = lax.broadcasted_iota(jnp.int32, (cw, n_grp * cw), 1)
        for g in range(n_grp):
            tgt = (rl // SSM_GROUP) * lanes + g * SSM_GROUP + (rl % SSM_GROUP)
            perm[g * cw:(g + 1) * cw, :] = jnp.where(cl == tgt, 1.0, 0.0).astype(BF16)

    for g in range(n_grp):
        gl = g % 2
        for d in range(2):
            for t in range(S5_CHUNK):
                w_r, w_i = _cmul(c_r_ref[d, g], c_i_ref[d, g], po_r_ref[d, g, t:t + 1, :], po_i_ref[d, g, t:t + 1, :])
                rows = slice(gl * cw + t * SSM_GROUP, gl * cw + (t + 1) * SSM_GROUP)
                for ri, plane in enumerate((w_r, -w_i)):
                    c0 = (2 * d + ri) * pw_ + gl * n_state
                    c_blk[g // 2, rows, c0:c0 + n_state] = plane.astype(BF16)

    chains = [(q, d) for q in range(n_pair) for d in range(2)]

    def plane_lanes(q, d):
        base = q * pc + d * 2 * pw_
        return slice(base, base + pw_), slice(base + pw_, base + 2 * pw_)

    def coef(row, q):
        return ap_ref[row:row + 1, q * pw_:(q + 1) * pw_]

    def tile_rows(n, d):
        j = n if d == 0 else n_tiles - 1 - n
        return pl.ds(pl.multiple_of(j * S5_SEGS, S5_SEGS), S5_SEGS)

    def step(n, carry):
        out = []
        for (q, d), (zr, zi) in zip(chains, carry):
            re, im = plane_lanes(q, d)
            rows = tile_rows(n, d)
            sin[rows, re] = zr
            sin[rows, im] = zi
            ar, ai = coef(2 * d, q), coef(2 * d + 1, q)
            out.append((ar * zr - ai * zi + s_ref[rows, re], ar * zi + ai * zr + s_ref[rows, im]))
        return tuple(out)

    z0 = jnp.zeros((S5_SEGS, pw_), F32)
    ends = lax.fori_loop(0, n_tiles, step, tuple((z0, z0) for _ in chains))

    carries = []
    for (q, d), (zr, zi) in zip(chains, ends):
        sr, si = coef(4 + 2 * d, q), coef(5 + 2 * d, q)
        cr = jnp.zeros((1, pw_), F32)
        ci = jnp.zeros((1, pw_), F32)
        seg_r = [None] * S5_SEGS
        seg_i = [None] * S5_SEGS
        for s in (range(S5_SEGS) if d == 0 else range(S5_SEGS - 1, -1, -1)):
            seg_r[s], seg_i[s] = cr, ci
            cr, ci = (zr[s:s + 1] + sr * cr - si * ci, zi[s:s + 1] + sr * ci + si * cr)
        carries.append((jnp.concatenate(seg_r, axis=0), jnp.concatenate(seg_i, axis=0)))

    def fix(n8, _):
        tiles = pl.ds(pl.multiple_of(n8 * 8, 8), 8)
        for (q, d), (car_r, car_i) in zip(chains, carries):
            re, im = plane_lanes(q, d)
            pr8 = pw_ref[2 * d, tiles, q * pw_:(q + 1) * pw_]
            pi8 = pw_ref[2 * d + 1, tiles, q * pw_:(q + 1) * pw_]
            car2_r = jnp.concatenate([car_r, car_r], axis=0)
            car2_i = jnp.concatenate([car_i, car_i], axis=0)
            for r in range(0, 8, 2):
                rows = pl.ds(pl.multiple_of((n8 * 8 + r) * S5_SEGS, 2 * S5_SEGS), 2 * S5_SEGS)
                pr = jnp.concatenate([jnp.broadcast_to(pr8[r + h:r + h + 1], (S5_SEGS, pw_)) for h in range(2)], axis=0)
                pi = jnp.concatenate([jnp.broadcast_to(pi8[r + h:r + h + 1], (S5_SEGS, pw_)) for h in range(2)], axis=0)
                sin_bf[rows, re] = (sin[rows, re] + (pr * car2_r - pi * car2_i)).astype(BF16)
                sin_bf[rows, im] = (sin[rows, im] + (pr * car2_i + pi * car2_r)).astype(BF16)
        return 0

    lax.fori_loop(0, n_tiles // 8, fix, 0)

    parts = []
    for q in range(n_pair):
        lhs = sin_bf[:, q * pc:(q + 1) * pc]
        carried = lax.dot_general(lhs, c_blk[q], (((1,), (1,)), ((), ())), preferred_element_type=F32)
        parts.append(yi_ref[:, q * 2 * cw:(q + 1) * 2 * cw] + carried)
    y_chunk = jnp.concatenate(parts, axis=1).astype(BF16)
    ynat[...] = jnp.dot(y_chunk, perm[...], preferred_element_type=F32)

    def scatter(j, _):
        src = pl.ds(pl.multiple_of(j * S5_SEGS, S5_SEGS), S5_SEGS)
        for t in range(S5_CHUNK):
            y_ref[pl.ds(S5_CHUNK * j + t, S5_SEGS, stride=seg_stride), :] = ynat[src, t * lanes:(t + 1) * lanes]
        return 0

    lax.fori_loop(0, n_tiles, scatter, 0)


def _s5_scan(u, fwd, bwd):
    seq, d_ssm = u.shape
    n_groups = d_ssm // SSM_GROUP
    n_state = SSM_STATE
    cw = S5_CHUNK * SSM_GROUP
    n_rows = seq // S5_CHUNK
    n_tiles = n_rows // S5_SEGS
    lanes = 128
    gpb = lanes // SSM_GROUP
    n_blocks = d_ssm // lanes
    bw = gpb * cw
    sw = gpb * 4 * n_state
    seg_stride = n_tiles * S5_CHUNK

    lag, st, wo, ap, pw = _s5_tables(fwd, bwd, n_tiles)

    def per_block(t):
        return pl.BlockSpec((t.shape[0], gpb) + t.shape[2:], lambda i: (0, i) + (0,) * (t.ndim - 2))

    y_intra, s_loc = pl.pallas_call(
        functools.partial(_s5_a_kernel, n_tiles=n_tiles, seg_stride=seg_stride),
        grid=(n_blocks,),
        in_specs=[
            pl.BlockSpec((seq, lanes), lambda i: (0, i)),
            per_block(lag), *[per_block(t) for t in st],
        ],
        out_specs=[
            pl.BlockSpec((n_rows, bw), lambda i: (0, i)),
            pl.BlockSpec((n_rows, sw), lambda i: (0, i)),
        ],
        out_shape=[
            jax.ShapeDtypeStruct((n_rows, n_blocks * bw), F32),
            jax.ShapeDtypeStruct((n_rows, n_blocks * sw), F32),
        ],
        scratch_shapes=[
            pltpu.VMEM((bw, bw), BF16),
            pltpu.VMEM((gpb, cw, cw), BF16),
            pltpu.VMEM((gpb // 2, 2 * cw, 8 * n_state), BF16),
            pltpu.VMEM((n_rows, S5_CHUNK * lanes), BF16),
        ],
        compiler_params=_params("arbitrary"),
        name="s5_a",
    )(u, lag, *st)

    pl_lanes = gpb * n_state
    return pl.pallas_call(
        functools.partial(_s5_bc_kernel, n_tiles=n_tiles, seg_stride=seg_stride),
        grid=(n_blocks,),
        in_specs=[
            pl.BlockSpec((n_rows, sw), lambda i: (0, i)),
            pl.BlockSpec((n_rows, bw), lambda i: (0, i)),
            *[per_block(t) for t in wo],
            pl.BlockSpec((8, pl_lanes), lambda i: (0, i)),
            pl.BlockSpec((4, n_tiles, pl_lanes), lambda i: (0, 0, i)),
        ],
        out_specs=pl.BlockSpec((seq, lanes), lambda i: (0, i)),
        out_shape=jax.ShapeDtypeStruct((seq, d_ssm), F32),
        scratch_shapes=[
            pltpu.VMEM((n_rows, sw), F32),
            pltpu.VMEM((bw, bw), BF16),
            pltpu.VMEM((n_rows, bw), F32),
            pltpu.VMEM((gpb // 2, 2 * cw, 8 * n_state), BF16),
            pltpu.VMEM((n_rows, sw), BF16),
        ],
        compiler_params=_params("arbitrary"),
        name="s5_bc",
    )(s_loc, y_intra, *wo, ap, pw)


def _s5_post_kernel(y_ref, u_ref, d_ref, w_ref, b_ref, g_ref, o_ref):
    y = y_ref[...] + d_ref[...] * u_ref[...]
    c0 = np.float32(np.sqrt(2.0 / np.pi))
    y = 0.5 * y * (1.0 + jnp.tanh(c0 * (y + np.float32(0.044715) * (y * y * y))))
    z = jnp.dot(y.astype(BF16), w_ref[...].astype(BF16), preferred_element_type=F32) + b_ref[...]
    o = y * (1.0 / (1.0 + jnp.exp(-z)))
    o_ref[...] = _rms(o, g_ref[...]).astype(BF16)


def _s5_post(y, u, d_skip, w_glu, b_glu, g):
    seq, d = y.shape
    tm = min(512, seq)
    row = lambda i: (i, 0)
    fix = lambda i: (0, 0)
    return pl.pallas_call(
        _s5_post_kernel,
        grid=(seq // tm,),
        in_specs=[
            pl.BlockSpec((tm, d), row), pl.BlockSpec((tm, d), row), pl.BlockSpec((1, d), fix),
            pl.BlockSpec((d, d), fix), pl.BlockSpec((1, d), fix), pl.BlockSpec((1, d), fix),
        ],
        out_specs=pl.BlockSpec((tm, d), row),
        out_shape=jax.ShapeDtypeStruct((seq, d), BF16),
        compiler_params=_params("arbitrary"),
        name="s5_post",
    )(y, u, d_skip.reshape(1, d), w_glu, b_glu.reshape(1, d), g.reshape(1, d))


def _na_bias_table(rpb):
    n_heads = rpb.shape[0]
    cols = np.arange(GRID_W)
    col_start = np.clip(cols - WIN_COLS // 2, 0, GRID_W - WIN_COLS)
    key_cols = np.arange(GRID_W)
    in_win = (key_cols[None, :] >= col_start[:, None]) & (key_cols[None, :] < col_start[:, None] + WIN_COLS)
    dx = key_cols[None, :] - cols[:, None] + (WIN_COLS - 1)
    pick_x = (dx[:, :, None] == np.arange(2 * WIN_COLS - 1)).astype(np.float32)
    halves = []
    for half in range(2):
        pad = ((0, 0), (half * GRID_W, (1 - half) * GRID_W))
        pick = np.pad(pick_x, pad + ((0, 0),))
        keep = np.pad(in_win, pad, constant_values=True)
        b = jnp.einsum('hyx,ckx->hyck', rpb.astype(F32) * LOG2_E, pick, precision=lax.Precision.HIGHEST)
        b = jnp.where(keep[None, None], b, MASK_NEG)
        b = b.reshape(n_heads // HEADS_PER_DOT, HEADS_PER_DOT, b.shape[1], GRID_W, 2 * GRID_W)
        halves.append(jnp.swapaxes(b, 1, 2).reshape(n_heads // HEADS_PER_DOT, b.shape[2],
                                                    HEADS_PER_DOT * GRID_W, 2 * GRID_W))
    return halves


def _na_kernel(q_ref, k_ref, v_ref, b0_ref, b1_ref, g_ref, o_ref, *, rows, rows_per_step):
    n_keys = WIN_ROWS * GRID_W
    d_na = k_ref.shape[-1]
    pw = HEADS_PER_DOT * NA_HEAD_DIM
    row_head = lax.broadcasted_iota(jnp.int32, (HEADS_PER_DOT * GRID_W, pw), 0) // GRID_W
    col_head = lax.broadcasted_iota(jnp.int32, (HEADS_PER_DOT * GRID_W, pw), 1) // NA_HEAD_DIM
    diag = row_head == col_head
    out_head = lax.broadcasted_iota(jnp.int32, (GRID_W, pw), 1) // NA_HEAD_DIM
    first = pl.program_id(0) * rows_per_step
    block_start = jnp.clip(first - WIN_ROWS // 2, 0, rows - k_ref.shape[0])

    def one_row(i, _):
        r = first + i
        win_start = jnp.clip(r - WIN_ROWS // 2, 0, rows - WIN_ROWS)
        dy0 = win_start - r + (WIN_ROWS - 1)
        k = k_ref[pl.ds(win_start - block_start, WIN_ROWS)].reshape(n_keys, d_na)
        v = v_ref[pl.ds(win_start - block_start, WIN_ROWS)].reshape(n_keys, d_na)
        q_rows = pl.ds(pl.multiple_of(i * GRID_W, GRID_W), GRID_W)
        outs = []
        for p in range(d_na // pw):
            sl = slice(p * pw, (p + 1) * pw)
            q4 = q_ref[q_rows, sl]
            qbd = jnp.where(diag, jnp.concatenate([q4] * HEADS_PER_DOT, axis=0), jnp.zeros((), BF16))
            s = lax.dot_general(qbd, k[:, sl], (((1,), (1,)), ((), ())), preferred_element_type=F32)
            s = s + jnp.concatenate([b0_ref[p, dy0 + 2 * j] + b1_ref[p, dy0 + 2 * j + 1]
                                     for j in range(WIN_ROWS // 2)], axis=1)
            m = jnp.max(s, axis=-1, keepdims=True)
            e = jnp.exp2(s - m)
            l = jnp.sum(e, axis=-1, keepdims=True)
            o = jnp.dot(e.astype(BF16), v[:, sl], preferred_element_type=F32) / l
            acc = jnp.zeros((GRID_W, pw), F32)
            for h in range(HEADS_PER_DOT):
                acc = acc + jnp.where(out_head == h, o[h * GRID_W:(h + 1) * GRID_W], 0.0)
            outs.append(acc)
        y = jnp.concatenate(outs, axis=1)
        o_ref[q_rows, :] = _rms(y, g_ref[...]).astype(BF16)
        return 0

    lax.fori_loop(0, rows_per_step, one_row, 0, unroll=True)


def _neighbourhood_attention(qkv, rpb, g):
    seq = qkv.shape[0]
    d_na = qkv.shape[1] // 3
    rows = seq // GRID_W
    bias = _na_bias_table(rpb)
    qkv3 = qkv.reshape(rows, GRID_W, 3 * d_na)

    rps = 4
    key_rows = 2 * WIN_ROWS

    def block_start(b):
        return jnp.clip(b * rps - WIN_ROWS // 2, 0, rows - key_rows)

    window = (pl.Element(key_rows), pl.Element(GRID_W), pl.Element(d_na))

    return pl.pallas_call(
        functools.partial(_na_kernel, rows=rows, rows_per_step=rps),
        grid=(rows // rps,),
        in_specs=[
            pl.BlockSpec((rps * GRID_W, d_na), lambda b: (b, 0)),
            pl.BlockSpec(window, lambda b: (block_start(b), 0, d_na)),
            pl.BlockSpec(window, lambda b: (block_start(b), 0, 2 * d_na)),
            pl.BlockSpec(bias[0].shape, lambda b: (0, 0, 0, 0)),
            pl.BlockSpec(bias[1].shape, lambda b: (0, 0, 0, 0)),
            pl.BlockSpec((1, d_na), lambda b: (0, 0)),
        ],
        out_specs=pl.BlockSpec((rps * GRID_W, d_na), lambda b: (b, 0)),
        out_shape=jax.ShapeDtypeStruct((seq, d_na), BF16),
        compiler_params=_params("arbitrary"),
        name="na",
    )(qkv, qkv3, qkv3, bias[0], bias[1], g.reshape(1, d_na))


def _outproj_kernel(a_ref, b_ref, w_ref, x_ref, o_ref):
    da = a_ref.shape[-1]
    acc = jnp.dot(a_ref[...], w_ref[:da, :].astype(BF16), preferred_element_type=F32)
    acc = acc + jnp.dot(b_ref[...], w_ref[da:, :].astype(BF16), preferred_element_type=F32)
    o_ref[...] = x_ref[...] + acc


def _out_proj(y_ssm, y_na, w_out, x):
    seq, d_model = x.shape
    da, db = y_ssm.shape[1], y_na.shape[1]
    tm = min(2048, seq)
    tn = 512
    return pl.pallas_call(
        _outproj_kernel,
        grid=(seq // tm, d_model // tn),
        in_specs=[
            pl.BlockSpec((tm, da), lambda i, j: (i, 0)),
            pl.BlockSpec((tm, db), lambda i, j: (i, 0)),
            pl.BlockSpec((da + db, tn), lambda i, j: (0, j)),
            pl.BlockSpec((tm, tn), lambda i, j: (i, j)),
        ],
        out_specs=pl.BlockSpec((tm, tn), lambda i, j: (i, j)),
        out_shape=jax.ShapeDtypeStruct((seq, d_model), F32),
        compiler_params=_params("arbitrary", "arbitrary"),
        name="out_proj",
    )(y_ssm, y_na, w_out, x)


def _router_kernel(x_ref, g_ref, wt_ref, h_ref, a_ref):
    h = _rms(x_ref[...], g_ref[...])
    h_hi = h.astype(BF16)
    h_ref[...] = h_hi
    h_lo = (h - h_hi.astype(F32)).astype(BF16)
    w = wt_ref[...]
    w_hi = w.astype(BF16)
    w_lo = (w - w_hi.astype(F32)).astype(BF16)
    n_exp = w.shape[0]
    nt = (((1,), (1,)), ((), ()))
    both = lax.dot_general(jnp.concatenate([w_hi, w_lo], axis=0), h_hi, nt, preferred_element_type=F32)
    logits = both[:n_exp] + both[n_exp:] + lax.dot_general(w_hi, h_lo, nt, preferred_element_type=F32)
    m = jnp.max(logits, axis=0, keepdims=True)
    e = jnp.exp(logits - m)
    a_ref[...] = e / jnp.sum(e, axis=0, keepdims=True)


def _router(x1, g, w_router):
    seq, d_model = x1.shape
    n_exp = w_router.shape[1]
    tm = min(512, seq)
    return pl.pallas_call(
        _router_kernel,
        grid=(seq // tm,),
        in_specs=[
            pl.BlockSpec((tm, d_model), lambda i: (i, 0)),
            pl.BlockSpec((1, d_model), lambda i: (0, 0)),
            pl.BlockSpec((n_exp, d_model), lambda i: (0, 0)),
        ],
        out_specs=[
            pl.BlockSpec((tm, d_model), lambda i: (i, 0)),
            pl.BlockSpec((n_exp, tm), lambda i: (0, i)),
        ],
        out_shape=[
            jax.ShapeDtypeStruct((seq, d_model), BF16),
            jax.ShapeDtypeStruct((n_exp, seq), F32),
        ],
        compiler_params=_params("arbitrary"),
        name="router",
    )(x1, g.reshape(1, d_model), w_router.T)


def _topk_kernel(a_ref, posw_ref, gate_ref, ws_ref, nr_ref, *, cap, blk, win):
    a = a_ref[...]
    n_exp, seq = a.shape
    n_blk = seq // blk
    bits = pltpu.bitcast(a, jnp.int32)

    def bit_step(i, thr):
        cand = thr | jnp.left_shift(jnp.int32(1), 30 - i)
        cnt = jnp.sum((bits >= cand).astype(jnp.int32), axis=-1, keepdims=True)
        return jnp.where(cnt >= cap, cand, thr)

    thr = lax.fori_loop(0, 31, bit_step, jnp.zeros((n_exp, 1), jnp.int32))
    gt = bits > thr
    eq = bits == thr
    need = cap - jnp.sum(gt.astype(jnp.int32), axis=-1, keepdims=True)

    tri = (lax.broadcasted_iota(jnp.int32, (blk, blk), 0)
           <= lax.broadcasted_iota(jnp.int32, (blk, blk), 1)).astype(BF16)
    blk_of_tok = lax.broadcasted_iota(jnp.int32, (seq, n_blk), 0) // blk
    tok_to_blk = (blk_of_tok == lax.broadcasted_iota(jnp.int32, (seq, n_blk), 1)).astype(BF16)
    blk_before = (lax.broadcasted_iota(jnp.int32, (n_blk, n_blk), 0)
                  < lax.broadcasted_iota(jnp.int32, (n_blk, n_blk), 1)).astype(BF16)
    erow = lax.broadcasted_iota(jnp.int32, (2 * n_blk, seq), 0)
    ecol = lax.broadcasted_iota(jnp.int32, (2 * n_blk, seq), 1) // blk
    expand = jnp.where(erow == ecol, 32.0, jnp.where(erow - n_blk == ecol, 1.0, 0.0)).astype(BF16)

    def prefix_counts(mask):
        mb = jnp.where(mask, 1.0, 0.0).astype(BF16)
        local = jnp.concatenate(
            [jnp.dot(mb[:, b * blk:(b + 1) * blk], tri, preferred_element_type=F32) for b in range(n_blk)],
            axis=1)
        per_blk = jnp.dot(mb, tok_to_blk, preferred_element_type=F32)
        start = jnp.dot(per_blk.astype(BF16), blk_before, preferred_element_type=F32)
        hi = jnp.floor(start * (1.0 / 32.0))
        parts = jnp.concatenate([hi, start - 32.0 * hi], axis=1).astype(BF16)
        start_tok = jnp.dot(parts, expand, preferred_element_type=F32)
        return local + start_tok, start, start_tok, per_blk

    eq_incl, _, _, _ = prefix_counts(eq)
    sel = gt | (eq & (eq_incl - 1.0 < need.astype(F32)))
    incl, start, start_tok, per_blk = prefix_counts(sel)

    def window(s):
        return jnp.floor(s * (1.0 / MOE_WIN_ALIGN)) * MOE_WIN_ALIGN

    posw_ref[...] = jnp.where(sel, (incl - 1.0 - window(start_tok)).astype(jnp.int32), -1)
    gate_ref[...] = jnp.where(sel, a, 0.0)
    ws_ref[...] = window(start).astype(jnp.int32)
    span = start - window(start) + per_blk
    rounds = jnp.floor((span + float(win - 1)) * (1.0 / win))
    nr_ref[...] = jnp.max(rounds, axis=0, keepdims=True).astype(jnp.int32)


def _topk(aff_t, cap, blk, win):
    n_exp, seq = aff_t.shape
    n_blk = seq // blk
    full = lambda *_: (0, 0)
    return pl.pallas_call(
        functools.partial(_topk_kernel, cap=cap, blk=blk, win=win),
        grid=(1,),
        in_specs=[pl.BlockSpec((n_exp, seq), full)],
        out_specs=[pl.BlockSpec((n_exp, seq), full), pl.BlockSpec((n_exp, seq), full),
                   pl.BlockSpec((n_exp, n_blk), full), pl.BlockSpec((1, n_blk), full)],
        out_shape=[
            jax.ShapeDtypeStruct((n_exp, seq), jnp.int32),
            jax.ShapeDtypeStruct((n_exp, seq), F32),
            jax.ShapeDtypeStruct((n_exp, n_blk), jnp.int32),
            jax.ShapeDtypeStruct((1, n_blk), jnp.int32),
        ],
        compiler_params=_params("arbitrary"),
        name="topk",
    )(aff_t)


def _window(ws_ref, e, b, r, n_blk, win, cap):
    ws = ws_ref[e * n_blk + b] + r * win
    start = jnp.minimum(ws, cap - win)
    return pl.multiple_of(start, MOE_WIN_ALIGN), ws - start


def _gather_kernel(ws_ref, nr_ref, h_ref, rel_ref, xe_ref, *, blk, win, n_blk):
    n_exp, cap, _ = xe_ref.shape
    xe_ref[...] = jnp.zeros_like(xe_ref)
    slot = lax.broadcasted_iota(jnp.int32, (win, blk), 0)

    def one_round(b, r):
        rows = h_ref[pl.ds(pl.multiple_of(b * blk, blk), blk), :]
        rel = rel_ref[b]
        starts, hots = [], []
        for e in range(n_exp):
            start, shift = _window(ws_ref, e, b, r, n_blk, win, cap)
            relr = rel[e:e + 1, :] - r * win
            key = jnp.where(relr >= 0, relr + shift, -1)
            hots.append(jnp.where(slot == key, 1.0, 0.0).astype(BF16))
            starts.append(start)
        res = jnp.dot(jnp.concatenate(hots, axis=0), rows, preferred_element_type=F32)
        for e in range(n_exp):
            dst = pl.ds(starts[e], win)
            xe_ref[e, dst, :] = (xe_ref[e, dst, :].astype(F32) + res[e * win:(e + 1) * win]).astype(BF16)

    def extra_rounds(b):
        def body(r, carry):
            one_round(b, r)
            return carry
        lax.fori_loop(1, nr_ref[b], body, 0)

    def block_pair(bp, _):
        for h in range(2):
            one_round(2 * bp + h, 0)
        for h in range(2):
            extra_rounds(2 * bp + h)
        return 0

    lax.fori_loop(0, n_blk // 2, block_pair, 0)


def _moe_gather(ws_flat, n_rounds, h2, rel3, cap, win):
    seq, d_model = h2.shape
    n_blk, n_exp, blk = rel3.shape
    dq = d_model // 4
    grid_spec = pltpu.PrefetchScalarGridSpec(
        num_scalar_prefetch=2,
        grid=(4,),
        in_specs=[
            pl.BlockSpec((seq, dq), lambda c, ws, nr: (0, c)),
            pl.BlockSpec((n_blk, n_exp, blk), lambda c, ws, nr: (0, 0, 0)),
        ],
        out_specs=pl.BlockSpec((n_exp, cap, dq), lambda c, ws, nr: (0, 0, c)),
    )
    return pl.pallas_call(
        functools.partial(_gather_kernel, blk=blk, win=win, n_blk=n_blk),
        grid_spec=grid_spec,
        out_shape=jax.ShapeDtypeStruct((n_exp, cap, d_model), BF16),
        compiler_params=_params("arbitrary"),
        name="moe_gather",
    )(ws_flat, n_rounds, h2, rel3)


def _ffn_kernel(x_ref, wg_ref, wu_ref, wd_ref, y_ref, act_ref, *, n_f):
    s = pl.program_id(1)
    tf = wg_ref.shape[-1]

    @pl.when(s < n_f)
    def _():
        x = x_ref[0]
        g = jnp.dot(x, wg_ref[0].astype(BF16), preferred_element_type=F32)
        u = jnp.dot(x, wu_ref[0].astype(BF16), preferred_element_type=F32)
        act_ref[s] = (g * (1.0 / (1.0 + jnp.exp(-g))) * u).astype(BF16)

    @pl.when(s >= n_f)
    def _():
        acc = jnp.dot(act_ref[0], wd_ref[0, 0:tf, :].astype(BF16), preferred_element_type=F32)
        for f in range(1, n_f):
            acc = acc + jnp.dot(act_ref[f], wd_ref[0, f * tf:(f + 1) * tf, :].astype(BF16),
                                preferred_element_type=F32)
        y_ref[0] = acc.astype(BF16)


def _moe_ffn(xe, w_gate, w_up, w_down):
    n_exp, cap, d_model = xe.shape
    d_ff = w_gate.shape[-1]
    tf = 512
    tn = 1024
    n_f, n_n = d_ff // tf, d_model // tn
    up_tile = lambda e, s: (e, 0, jnp.minimum(s, n_f - 1))
    down_tile = lambda e, s: (e, 0, jnp.maximum(s - n_f, 0))
    return pl.pallas_call(
        functools.partial(_ffn_kernel, n_f=n_f),
        grid=(n_exp, n_f + n_n),
        in_specs=[
            pl.BlockSpec((1, cap, d_model), lambda e, s: (e, 0, 0)),
            pl.BlockSpec((1, d_model, tf), up_tile),
            pl.BlockSpec((1, d_model, tf), up_tile),
            pl.BlockSpec((1, d_ff, tn), down_tile),
        ],
        out_specs=pl.BlockSpec((1, cap, tn), down_tile),
        out_shape=jax.ShapeDtypeStruct((n_exp, cap, d_model), BF16),
        scratch_shapes=[pltpu.VMEM((n_f, cap, tf), BF16)],
        compiler_params=_params("arbitrary", "arbitrary"),
        name="moe_ffn",
    )(xe, w_gate, w_up, w_down)


def _combine_kernel(ws_ref, nr_ref, ye_ref, x_ref, rel_ref, gate_ref, o_ref, *, blk, win, n_blk):
    n_exp, cap, _ = ye_ref.shape
    sub = x_ref.shape[0] // blk
    slot = lax.broadcasted_iota(jnp.int32, (win, blk), 0)

    def one_round(b, r):
        rel = rel_ref[b]
        gate = gate_ref[b]
        gates, wins = [], []
        for e in range(n_exp):
            start, shift = _window(ws_ref, e, b, r, n_blk, win, cap)
            relr = rel[e:e + 1, :] - r * win
            key = jnp.where(relr >= 0, relr + shift, -1)
            gates.append(jnp.where(slot == key, gate[e:e + 1, :], 0.0).astype(BF16))
            wins.append(ye_ref[e, pl.ds(start, win), :])
        return lax.dot_general(jnp.concatenate(gates, axis=0), jnp.concatenate(wins, axis=0),
                               (((0,), (0,)), ((), ())), preferred_element_type=F32)

    for s in range(sub):
        tok = slice(s * blk, (s + 1) * blk)
        o_ref[tok, :] = x_ref[tok, :] + one_round(pl.program_id(1) * sub + s, 0)
    for s in range(sub):
        tok = slice(s * blk, (s + 1) * blk)
        b = pl.program_id(1) * sub + s

        def extra(r, carry, b=b, tok=tok):
            o_ref[tok, :] += one_round(b, r)
            return carry

        lax.fori_loop(1, nr_ref[b], extra, 0)


def _moe_combine(ws_flat, n_rounds, ye, x1, rel3, gate3, win):
    seq, d_model = x1.shape
    n_exp, cap, _ = ye.shape
    n_blk, _, blk = rel3.shape
    dq = d_model // 4
    tile = min(4, n_blk) * blk
    whole = lambda c, t, ws, nr: (0, 0, 0)
    grid_spec = pltpu.PrefetchScalarGridSpec(
        num_scalar_prefetch=2,
        grid=(4, seq // tile),
        in_specs=[
            pl.BlockSpec((n_exp, cap, dq), lambda c, t, ws, nr: (0, 0, c)),
            pl.BlockSpec((tile, dq), lambda c, t, ws, nr: (t, c)),
            pl.BlockSpec(rel3.shape, whole),
            pl.BlockSpec(gate3.shape, whole),
        ],
        out_specs=pl.BlockSpec((tile, dq), lambda c, t, ws, nr: (t, c)),
    )
    return pl.pallas_call(
        functools.partial(_combine_kernel, blk=blk, win=win, n_blk=n_blk),
        grid_spec=grid_spec,
        out_shape=jax.ShapeDtypeStruct((seq, d_model), F32),
        compiler_params=_params("arbitrary", "arbitrary"),
        name="moe_combine",
    )(ws_flat, n_rounds, ye, x1, rel3, gate3)


def _final_norm_kernel(x_ref, g_ref, o_ref):
    o_ref[...] = _rms(x_ref[...], g_ref[...])


def _final_norm(x, g):
    seq, d_model = x.shape
    tm = min(512, seq)
    return pl.pallas_call(
        _final_norm_kernel,
        grid=(seq // tm,),
        in_specs=[pl.BlockSpec((tm, d_model), lambda i: (i, 0)), pl.BlockSpec((1, d_model), lambda i: (0, 0))],
        out_specs=pl.BlockSpec((tm, d_model), lambda i: (i, 0)),
        out_shape=jax.ShapeDtypeStruct((seq, d_model), F32),
        compiler_params=_params("arbitrary"),
        name="final_norm",
    )(x, g.reshape(1, d_model))


def _layer(x, norm_mix_g, w_in, fwd, bwd, ssm_d, w_glu, b_glu, na_rpb, g_ssm_out, g_na_out, w_out,
           norm_ffn_g, w_router, w_gate, w_up, w_down):
    seq, d_model = x.shape
    d_ssm = ssm_d.shape[0]
    d_na = g_na_out.shape[0]
    n_exp = w_router.shape[1]
    cap = EC_CAPACITY_FACTOR * seq // n_exp
    blk = min(MOE_TOK_BLOCK, cap // 2)
    win = min(MOE_WIN, cap)

    u, qkv = _in_proj(x, norm_mix_g, w_in, d_ssm, d_na)
    y_ssm = _s5_post(_s5_scan(u, fwd, bwd), u, ssm_d, w_glu, b_glu, g_ssm_out)
    y_na = _neighbourhood_attention(qkv, na_rpb, g_na_out)
    x1 = _out_proj(y_ssm, y_na, w_out, x)

    h2, aff_t = _router(x1, norm_ffn_g, w_router)
    rel, gate, ws, n_rounds = _topk(aff_t, cap, blk, win)
    ws_flat = ws.reshape(-1)
    n_rounds = n_rounds.reshape(-1)
    rel3 = jnp.swapaxes(rel.reshape(n_exp, seq // blk, blk), 0, 1)
    gate3 = jnp.swapaxes(gate.reshape(n_exp, seq // blk, blk), 0, 1)
    xe = _moe_gather(ws_flat, n_rounds, h2, rel3, cap, win)
    ye = _moe_ffn(xe, w_gate, w_up, w_down)
    return _moe_combine(ws_flat, n_rounds, ye, x1, rel3, gate3, win)


def kernel(x, norm_mix_g, w_in, a_re_fwd, a_im_fwd, log_dt_fwd, b_re_fwd, b_im_fwd, c_re_fwd, c_im_fwd, a_re_bwd, a_im_bwd, log_dt_bwd, b_re_bwd, b_im_bwd, c_re_bwd, c_im_bwd, ssm_d, w_glu, b_glu, na_rpb, g_ssm_out, g_na_out, w_out, norm_ffn_g, w_router, w_gate, w_up, w_down, norm_final_g):
    bsz = x.shape[0]
    depth = w_in.shape[0]
    outs = []
    for b in range(bsz):
        xb = x[b]
        for l in range(depth):
            fwd = (a_re_fwd[l], a_im_fwd[l], log_dt_fwd[l], b_re_fwd[l], b_im_fwd[l], c_re_fwd[l], c_im_fwd[l])
            bwd = (a_re_bwd[l], a_im_bwd[l], log_dt_bwd[l], b_re_bwd[l], b_im_bwd[l], c_re_bwd[l], c_im_bwd[l])
            xb = _layer(xb, norm_mix_g[l], w_in[l], fwd, bwd, ssm_d[l], w_glu[l], b_glu[l], na_rpb[l],
                        g_ssm_out[l], g_na_out[l], w_out[l], norm_ffn_g[l], w_router[l],
                        w_gate[l], w_up[l], w_down[l])
        outs.append(_final_norm(xb, norm_final_g))
    return jnp.stack(outs)
```

```python
import functools

import numpy as np
import jax
import jax.numpy as jnp
from jax import lax
from jax.experimental import pallas as pl
from jax.experimental.pallas import tpu as pltpu

F32 = jnp.float32
BF16 = jnp.bfloat16

RMS_EPS = 1e-6
SSM_GROUP = 16
SSM_STATE = 64
NA_HEADS = 16
NA_HEAD_DIM = 64
GRID_W = 64
WIN_ROWS = 8
WIN_COLS = 16
N_EXPERTS = 16
EC_CAPACITY_FACTOR = 2

S5_CHUNK = 16
S5_SEGS = 8
HEADS_PER_DOT = 4
MOE_TOK_BLOCK = 256
MOE_WIN_ALIGN = 16
MOE_WIN = 64
MASK_NEG = -1e30
LOG2_E = float(np.log2(np.e))

VMEM_LIMIT_BYTES = 56 * 1024 * 1024


def _params(*semantics):
    return pltpu.CompilerParams(dimension_semantics=semantics, vmem_limit_bytes=VMEM_LIMIT_BYTES)


def _rms(x, g):
    ms = jnp.mean(x * x, axis=-1, keepdims=True)
    return x * lax.rsqrt(ms + RMS_EPS) * g


def _inproj_kernel(x_ref, g_ref, w_ref, u_ref, qkv_ref, h_scr, *, n_u, n_q, q_scale):
    j = pl.program_id(1)

    @pl.when(j == 0)
    def _():
        h_scr[...] = _rms(x_ref[...], g_ref[...]).astype(BF16)

    def project():
        return jnp.dot(h_scr[...], w_ref[...].astype(BF16), preferred_element_type=F32)

    @pl.when(j < n_u)
    def _():
        u_ref[...] = project()

    @pl.when(j >= n_u)
    def _():
        scale = jnp.where(j < n_u + n_q, q_scale, 1.0).astype(F32)
        qkv_ref[...] = (project() * scale).astype(BF16)


def _in_proj(x, g, w_in, d_ssm, d_na):
    seq, d_model = x.shape
    tm = min(1024, seq)
    tn = 1024
    n_u, n_q = d_ssm // tn, d_na // tn
    n_cols = w_in.shape[1] // tn
    kern = functools.partial(_inproj_kernel, n_u=n_u, n_q=n_q, q_scale=NA_HEAD_DIM ** -0.5 * LOG2_E)
    return pl.pallas_call(
        kern,
        grid=(seq // tm, n_cols),
        in_specs=[
            pl.BlockSpec((tm, d_model), lambda i, j: (i, 0)),
            pl.BlockSpec((1, d_model), lambda i, j: (0, 0)),
            pl.BlockSpec((d_model, tn), lambda i, j: (0, j)),
        ],
        out_specs=[
            pl.BlockSpec((tm, tn), lambda i, j: (i, jnp.minimum(j, n_u - 1))),
            pl.BlockSpec((tm, tn), lambda i, j: (i, jnp.maximum(j - n_u, 0))),
        ],
        out_shape=[
            jax.ShapeDtypeStruct((seq, d_ssm), F32),
            jax.ShapeDtypeStruct((seq, 3 * d_na), BF16),
        ],
        scratch_shapes=[pltpu.VMEM((tm, d_model), BF16)],
        compiler_params=_params("arbitrary", "arbitrary"),
        name="in_proj",
    )(x, g.reshape(1, d_model), w_in)


def _cmul(ar, ai, br, bi):
    return ar * br - ai * bi, ar * bi + ai * br


def _s5_tables(fwd, bwd, n_tiles):
    t_len = S5_CHUNK
    a_re, a_im, log_dt, b_re, b_im, c_re, c_im = (jnp.stack([f, b]).astype(F32) for f, b in zip(fwd, bwd))
    _, n_grp, n_st = a_re.shape
    n_ch = b_re.shape[-1]
    dt = jnp.exp(log_dt)[:, :, None]
    xr, xi = a_re * dt, a_im * dt
    steps = np.arange(t_len)

    def power(x_r, x_i, exps):
        e = jnp.asarray(exps, F32).reshape(exps.shape + (1,) * (x_r.ndim - 1))
        mag = jnp.exp(x_r[None] * e)
        return mag * jnp.cos(x_i[None] * e), mag * jnp.sin(x_i[None] * e)

    def per_step(exps):
        p_r, p_i = power(xr, xi, exps)
        return jnp.transpose(p_r, (1, 2, 0, 3)), jnp.transpose(p_i, (1, 2, 0, 3))

    a1_r, a1_i = power(xr, xi, np.ones((1, 2)))
    nr, ni = a1_r[0] - 1.0, a1_i[0]
    den = a_re * a_re + a_im * a_im
    qr, qi = (nr * a_re + ni * a_im) / den, (ni * a_re - nr * a_im) / den
    bb_r, bb_i = _cmul(qr[:, :, None, :], qi[:, :, None, :], jnp.swapaxes(b_re, 2, 3), jnp.swapaxes(b_im, 2, 3))
    am_r, am_i = per_step(np.stack([steps, steps[::-1]], axis=1))
    y_r, y_i = _cmul(c_re[:, :, None], c_im[:, :, None], am_r[:, :, :, None, :], am_i[:, :, :, None, :])
    y = jnp.concatenate([y_r, y_i], axis=-1).reshape(2, n_grp, t_len * n_ch, 2 * n_st)
    lag = jnp.einsum('dgck,dgqk->dgcq', jnp.concatenate([bb_r, -bb_i], axis=-1), y,
                     precision=lax.Precision.HIGH)
    st = per_step(np.stack([t_len - 1 - steps, steps], axis=1)) + (bb_r, bb_i)
    wo = per_step(np.stack([steps + 1, t_len - steps], axis=1)) + (c_re, c_im)
    xrf, xif = xr.reshape(2, -1), xi.reshape(2, -1)
    tiles = np.arange(n_tiles)
    ends = power(xrf, xif, np.array([[t_len, t_len], [t_len * n_tiles, t_len * n_tiles]]))
    ap = jnp.stack(ends, axis=2).reshape(8, -1)
    at_r, at_i = power(xrf, xif, t_len * np.stack([tiles, tiles[::-1]], axis=1))
    pw = jnp.transpose(jnp.stack([at_r, at_i], axis=2), (1, 2, 0, 3)).reshape(4, n_tiles, -1)
    return lag, st, wo, ap, pw


def _transpose_lane_blocks(groups, width):
    groups = [list(g) for g in groups]
    n = len(groups[0])
    first = groups[0][0]
    lanes = first.shape[-1]
    axis = first.ndim - 1
    block = lax.broadcasted_iota(jnp.int32, first.shape, axis) // width
    d = n // 2
    while d:
        upper = (block & d) != 0
        pairs = [(g, i) for g in groups for i in range(n) if i & d == 0]
        up = [pltpu.roll(g[i | d], d * width, axis) for g, i in pairs]
        down = [pltpu.roll(g[i], lanes - d * width, axis) for g, i in pairs]
        for (g, i), u, dn in zip(pairs, up, down):
            g[i], g[i | d] = jnp.where(upper, u, g[i]), jnp.where(upper, g[i | d], dn)
        d //= 2
    return groups


def _s5_a_kernel(u_ref, lag_ref, pe_r_ref, pe_i_ref, bb_r_ref, bb_i_ref, y_ref, s_ref, w_grp, e_pair, ucat,
                 *, n_tiles, seg_stride):
    n_grp = lag_ref.shape[1]
    cw = S5_CHUNK * SSM_GROUP
    n_plane, n_state = 4, pe_r_ref.shape[-1]
    lanes = u_ref.shape[-1]
    pc = n_plane * 2 * n_state

    @pl.when(pl.program_id(0) == 0)
    def _():
        e_pair[...] = jnp.zeros_like(e_pair)

    for g in range(n_grp):
        blank = jnp.zeros((SSM_GROUP, cw), F32)
        strip = jnp.concatenate([blank, lag_ref[0, g], lag_ref[1, g], blank], axis=1)
        for t in range(S5_CHUNK):
            r0 = t * SSM_GROUP
            fwd_lo = cw - t * SSM_GROUP
            bwd_lo = 2 * cw + (S5_CHUNK - 1 - t) * SSM_GROUP
            piece = strip[:, fwd_lo:fwd_lo + cw] + strip[:, bwd_lo:bwd_lo + cw]
            w_grp[g, r0:r0 + SSM_GROUP, :] = piece.astype(BF16)
            for d in range(2):
                planes = _cmul(pe_r_ref[d, g, t:t + 1, :], pe_i_ref[d, g, t:t + 1, :], bb_r_ref[d, g], bb_i_ref[d, g])
                for ri in range(2):
                    c0 = (2 * d + ri) * 2 * n_state + (g % 2) * n_state
                    rows = slice((g % 2) * cw + r0, (g % 2) * cw + r0 + SSM_GROUP)
                    e_pair[g // 2, rows, c0:c0 + n_state] = planes[ri].astype(BF16)

    per_trip = 4

    def gather(trip, _):
        groups, where = [], []
        for i in range(per_trip):
            jj = per_trip * trip + i
            pieces = []
            for t in range(S5_CHUNK):
                rows = [u_ref[pl.ds(S5_CHUNK * (2 * jj + h) + t, S5_SEGS, stride=seg_stride), :] for h in range(2)]
                pieces.append(jnp.concatenate(rows, axis=0))
            dst = pl.ds(pl.multiple_of(jj * 2 * S5_SEGS, 2 * S5_SEGS), 2 * S5_SEGS)
            for half in range(S5_CHUNK // n_grp):
                groups.append(pieces[half * n_grp:(half + 1) * n_grp])
                where.append((dst, half))
        for (dst, half), tiles in zip(where, _transpose_lane_blocks(groups, SSM_GROUP)):
            for g in range(n_grp):
                c0 = g * cw + half * lanes
                ucat[dst, c0:c0 + lanes] = tiles[g].astype(BF16)
        return 0

    lax.fori_loop(0, n_tiles // (2 * per_trip), gather, 0)
    ug = ucat[...]
    for g in range(n_grp):
        y_ref[:, g * cw:(g + 1) * cw] = jnp.dot(ug[:, g * cw:(g + 1) * cw], w_grp[g], preferred_element_type=F32)
    for q in range(n_grp // 2):
        s_ref[:, q * pc:(q + 1) * pc] = jnp.dot(ug[:, q * 2 * cw:(q + 1) * 2 * cw], e_pair[q],
                                                preferred_element_type=F32)


def _s5_bc_kernel(s_ref, yi_ref, po_r_ref, po_i_ref, c_r_ref, c_i_ref, ap_ref, pw_ref, y_ref, sin, ychunk, c_blk,
                  sin_bf, *, n_tiles, seg_stride):
    n_plane = 4
    _, n_grp, _, n_state = po_r_ref.shape
    cw = S5_CHUNK * SSM_GROUP
    n_pair = n_grp // 2
    pw_ = 2 * n_state
    pc = n_plane * pw_
    lanes = y_ref.shape[-1]

    @pl.when(pl.program_id(0) == 0)
    def _():
        c_blk[...] = jnp.zeros_like(c_blk)

    for g in range(n_grp):
        gl = g % 2
        for d in range(2):
            for t in range(S5_CHUNK):
                w_r, w_i = _cmul(c_r_ref[d, g], c_i_ref[d, g], po_r_ref[d, g, t:t + 1, :], po_i_ref[d, g, t:t + 1, :])
                rows = slice(gl * cw + t * SSM_GROUP, gl * cw + (t + 1) * SSM_GROUP)
                for ri, plane in enumerate((w_r, -w_i)):
                    c0 = (2 * d + ri) * pw_ + gl * n_state
                    c_blk[g // 2, rows, c0:c0 + n_state] = plane.astype(BF16)

    chains = [(q, d) for q in range(n_pair) for d in range(2)]

    def plane_lanes(q, d):
        base = q * pc + d * 2 * pw_
        return slice(base, base + pw_), slice(base + pw_, base + 2 * pw_)

    def coef(row, q):
        return ap_ref[row:row + 1, q * pw_:(q + 1) * pw_]

    def tile_rows(n, d):
        j = n if d == 0 else n_tiles - 1 - n
        return pl.ds(pl.multiple_of(j * S5_SEGS, S5_SEGS), S5_SEGS)

    def step(n, carry):
        out = []
        for (q, d), (zr, zi) in zip(chains, carry):
            re, im = plane_lanes(q, d)
            rows = tile_rows(n, d)
            sin[rows, re] = zr
            sin[rows, im] = zi
            ar, ai = coef(2 * d, q), coef(2 * d + 1, q)
            out.append((ar * zr - ai * zi + s_ref[rows, re], ar * zi + ai * zr + s_ref[rows, im]))
        return tuple(out)

    z0 = jnp.zeros((S5_SEGS, pw_), F32)
    ends = lax.fori_loop(0, n_tiles, step, tuple((z0, z0) for _ in chains))

    carries = []
    for (q, d), (zr, zi) in zip(chains, ends):
        sr, si = coef(4 + 2 * d, q), coef(5 + 2 * d, q)
        cr = jnp.zeros((1, pw_), F32)
        ci = jnp.zeros((1, pw_), F32)
        seg_r = [None] * S5_SEGS
        seg_i = [None] * S5_SEGS
        for s in (range(S5_SEGS) if d == 0 else range(S5_SEGS - 1, -1, -1)):
            seg_r[s], seg_i[s] = cr, ci
            cr, ci = (zr[s:s + 1] + sr * cr - si * ci, zi[s:s + 1] + sr * ci + si * cr)
        carries.append((jnp.concatenate(seg_r, axis=0), jnp.concatenate(seg_i, axis=0)))

    def fix(n8, _):
        tiles = pl.ds(pl.multiple_of(n8 * 8, 8), 8)
        for (q, d), (car_r, car_i) in zip(chains, carries):
            re, im = plane_lanes(q, d)
            pr8 = pw_ref[2 * d, tiles, q * pw_:(q + 1) * pw_]
            pi8 = pw_ref[2 * d + 1, tiles, q * pw_:(q + 1) * pw_]
            car2_r = jnp.concatenate([car_r, car_r], axis=0)
            car2_i = jnp.concatenate([car_i, car_i], axis=0)
            for r in range(0, 8, 2):
                rows = pl.ds(pl.multiple_of((n8 * 8 + r) * S5_SEGS, 2 * S5_SEGS), 2 * S5_SEGS)
                pr = jnp.concatenate([jnp.broadcast_to(pr8[r + h:r + h + 1], (S5_SEGS, pw_)) for h in range(2)], axis=0)
                pi = jnp.concatenate([jnp.broadcast_to(pi8[r + h:r + h + 1], (S5_SEGS, pw_)) for h in range(2)], axis=0)
                sin_bf[rows, re] = (sin[rows, re] + (pr * car2_r - pi * car2_i)).astype(BF16)
                sin_bf[rows, im] = (sin[rows, im] + (pr * car2_i + pi * car2_r)).astype(BF16)
        return 0

    lax.fori_loop(0, n_tiles // 8, fix, 0)

    for q in range(n_pair):
        lhs = sin_bf[:, q * pc:(q + 1) * pc]
        carried = lax.dot_general(lhs, c_blk[q], (((1,), (1,)), ((), ())), preferred_element_type=F32)
        ychunk[:, q * 2 * cw:(q + 1) * 2 * cw] = yi_ref[:, q * 2 * cw:(q + 1) * 2 * cw] + carried

    per_trip = 8

    def scatter(jj, _):
        groups, where = [], []
        for h in range(per_trip):
            j = per_trip * jj + h
            src = pl.ds(pl.multiple_of(j * S5_SEGS, S5_SEGS), S5_SEGS)
            for half in range(S5_CHUNK // n_grp):
                groups.append([ychunk[src, g * cw + half * lanes:g * cw + (half + 1) * lanes] for g in range(n_grp)])
                where.append(S5_CHUNK * j + half * n_grp)
        for t0, tiles in zip(where, _transpose_lane_blocks(groups, SSM_GROUP)):
            for k, tile in enumerate(tiles):
                y_ref[pl.ds(t0 + k, S5_SEGS, stride=seg_stride), :] = tile
        return 0

    lax.fori_loop(0, n_tiles // per_trip, scatter, 0)


def _s5_scan(u, fwd, bwd):
    seq, d_ssm = u.shape
    n_groups = d_ssm // SSM_GROUP
    n_state = SSM_STATE
    cw = S5_CHUNK * SSM_GROUP
    n_rows = seq // S5_CHUNK
    n_tiles = n_rows // S5_SEGS
    lanes = 128
    gpb = lanes // SSM_GROUP
    n_blocks = d_ssm // lanes
    bw = gpb * cw
    sw = gpb * 4 * n_state
    seg_stride = n_tiles * S5_CHUNK

    lag, st, wo, ap, pw = _s5_tables(fwd, bwd, n_tiles)

    def per_block(t):
        return pl.BlockSpec((t.shape[0], gpb) + t.shape[2:], lambda i: (0, i) + (0,) * (t.ndim - 2))

    y_intra, s_loc = pl.pallas_call(
        functools.partial(_s5_a_kernel, n_tiles=n_tiles, seg_stride=seg_stride),
        grid=(n_blocks,),
        in_specs=[
            pl.BlockSpec((seq, lanes), lambda i: (0, i)),
            per_block(lag), *[per_block(t) for t in st],
        ],
        out_specs=[
            pl.BlockSpec((n_rows, bw), lambda i: (0, i)),
            pl.BlockSpec((n_rows, sw), lambda i: (0, i)),
        ],
        out_shape=[
            jax.ShapeDtypeStruct((n_rows, n_blocks * bw), F32),
            jax.ShapeDtypeStruct((n_rows, n_blocks * sw), F32),
        ],
        scratch_shapes=[
            pltpu.VMEM((gpb, cw, cw), BF16),
            pltpu.VMEM((gpb // 2, 2 * cw, 8 * n_state), BF16),
            pltpu.VMEM((n_rows, S5_CHUNK * lanes), BF16),
        ],
        compiler_params=_params("arbitrary"),
        name="s5_a",
    )(u, lag, *st)

    pl_lanes = gpb * n_state
    return pl.pallas_call(
        functools.partial(_s5_bc_kernel, n_tiles=n_tiles, seg_stride=seg_stride),
        grid=(n_blocks,),
        in_specs=[
            pl.BlockSpec((n_rows, sw), lambda i: (0, i)),
            pl.BlockSpec((n_rows, bw), lambda i: (0, i)),
            *[per_block(t) for t in wo],
            pl.BlockSpec((8, pl_lanes), lambda i: (0, i)),
            pl.BlockSpec((4, n_tiles, pl_lanes), lambda i: (0, 0, i)),
        ],
        out_specs=pl.BlockSpec((seq, lanes), lambda i: (0, i)),
        out_shape=jax.ShapeDtypeStruct((seq, d_ssm), F32),
        scratch_shapes=[
            pltpu.VMEM((n_rows, sw), F32),
            pltpu.VMEM((n_rows, bw), F32),
            pltpu.VMEM((gpb // 2, 2 * cw, 8 * n_state), BF16),
            pltpu.VMEM((n_rows, sw), BF16),
        ],
        compiler_params=_params("arbitrary"),
        name="s5_bc",
    )(s_loc, y_intra, *wo, ap, pw)


def _s5_post_kernel(y_ref, u_ref, d_ref, w_ref, b_ref, g_ref, o_ref):
    y = y_ref[...] + d_ref[...] * u_ref[...]
    c0 = np.float32(np.sqrt(2.0 / np.pi))
    y = 0.5 * y * (1.0 + jnp.tanh(c0 * (y + np.float32(0.044715) * (y * y * y))))
    z = jnp.dot(y.astype(BF16), w_ref[...].astype(BF16), preferred_element_type=F32) + b_ref[...]
    o = y * (1.0 / (1.0 + jnp.exp(-z)))
    o_ref[...] = _rms(o, g_ref[...]).astype(BF16)


def _s5_post(y, u, d_skip, w_glu, b_glu, g):
    seq, d = y.shape
    tm = min(512, seq)
    row = lambda i: (i, 0)
    fix = lambda i: (0, 0)
    return pl.pallas_call(
        _s5_post_kernel,
        grid=(seq // tm,),
        in_specs=[
            pl.BlockSpec((tm, d), row), pl.BlockSpec((tm, d), row), pl.BlockSpec((1, d), fix),
            pl.BlockSpec((d, d), fix), pl.BlockSpec((1, d), fix), pl.BlockSpec((1, d), fix),
        ],
        out_specs=pl.BlockSpec((tm, d), row),
        out_shape=jax.ShapeDtypeStruct((seq, d), BF16),
        compiler_params=_params("arbitrary"),
        name="s5_post",
    )(y, u, d_skip.reshape(1, d), w_glu, b_glu.reshape(1, d), g.reshape(1, d))


def _na_bias_table(rpb):
    n_heads = rpb.shape[0]
    cols = np.arange(GRID_W)
    col_start = np.clip(cols - WIN_COLS // 2, 0, GRID_W - WIN_COLS)
    key_cols = np.arange(GRID_W)
    in_win = (key_cols[None, :] >= col_start[:, None]) & (key_cols[None, :] < col_start[:, None] + WIN_COLS)
    dx = key_cols[None, :] - cols[:, None] + (WIN_COLS - 1)
    pick_x = (dx[:, :, None] == np.arange(2 * WIN_COLS - 1)).astype(np.float32)
    halves = []
    for half in range(2):
        pad = ((0, 0), (half * GRID_W, (1 - half) * GRID_W))
        pick = np.pad(pick_x, pad + ((0, 0),))
        keep = np.pad(in_win, pad, constant_values=True)
        b = jnp.einsum('hyx,ckx->hyck', rpb.astype(F32) * LOG2_E, pick, precision=lax.Precision.HIGHEST)
        b = jnp.where(keep[None, None], b, MASK_NEG)
        b = b.reshape(n_heads // HEADS_PER_DOT, HEADS_PER_DOT, b.shape[1], GRID_W, 2 * GRID_W)
        halves.append(jnp.swapaxes(b, 1, 2).reshape(n_heads // HEADS_PER_DOT, b.shape[2],
                                                    HEADS_PER_DOT * GRID_W, 2 * GRID_W))
    return halves


def _na_kernel(q_ref, k_ref, v_ref, b0_ref, b1_ref, g_ref, o_ref, *, rows, rows_per_step):
    n_keys = WIN_ROWS * GRID_W
    d_na = k_ref.shape[-1]
    pw = HEADS_PER_DOT * NA_HEAD_DIM
    row_head = lax.broadcasted_iota(jnp.int32, (HEADS_PER_DOT * GRID_W, pw), 0) // GRID_W
    col_head = lax.broadcasted_iota(jnp.int32, (HEADS_PER_DOT * GRID_W, pw), 1) // NA_HEAD_DIM
    diag = row_head == col_head
    out_head = lax.broadcasted_iota(jnp.int32, (GRID_W, pw), 1) // NA_HEAD_DIM
    first = pl.program_id(0) * rows_per_step
    block_start = jnp.clip(first - WIN_ROWS // 2, 0, rows - k_ref.shape[0])

    def one_row(i, _):
        r = first + i
        win_start = jnp.clip(r - WIN_ROWS // 2, 0, rows - WIN_ROWS)
        dy0 = win_start - r + (WIN_ROWS - 1)
        k = k_ref[pl.ds(win_start - block_start, WIN_ROWS)].reshape(n_keys, d_na)
        v = v_ref[pl.ds(win_start - block_start, WIN_ROWS)].reshape(n_keys, d_na)
        q_rows = pl.ds(pl.multiple_of(i * GRID_W, GRID_W), GRID_W)
        outs = []
        for p in range(d_na // pw):
            sl = slice(p * pw, (p + 1) * pw)
            q4 = q_ref[q_rows, sl]
            qbd = jnp.where(diag, jnp.concatenate([q4] * HEADS_PER_DOT, axis=0), jnp.zeros((), BF16))
            s = lax.dot_general(qbd, k[:, sl], (((1,), (1,)), ((), ())), preferred_element_type=F32)
            s = s + jnp.concatenate([b0_ref[p, dy0 + 2 * j] + b1_ref[p, dy0 + 2 * j + 1]
                                     for j in range(WIN_ROWS // 2)], axis=1)
            m = jnp.max(s, axis=-1, keepdims=True)
            e = jnp.exp2(s - m)
            l = jnp.sum(e, axis=-1, keepdims=True)
            o = jnp.dot(e.astype(BF16), v[:, sl], preferred_element_type=F32) / l
            acc = jnp.zeros((GRID_W, pw), F32)
            for h in range(HEADS_PER_DOT):
                acc = acc + jnp.where(out_head == h, o[h * GRID_W:(h + 1) * GRID_W], 0.0)
            outs.append(acc)
        y = jnp.concatenate(outs, axis=1)
        o_ref[q_rows, :] = _rms(y, g_ref[...]).astype(BF16)
        return 0

    lax.fori_loop(0, rows_per_step, one_row, 0, unroll=True)


def _neighbourhood_attention(qkv, rpb, g):
    seq = qkv.shape[0]
    d_na = qkv.shape[1] // 3
    rows = seq // GRID_W
    bias = _na_bias_table(rpb)
    qkv3 = qkv.reshape(rows, GRID_W, 3 * d_na)

    rps = 4
    key_rows = 2 * WIN_ROWS

    def block_start(b):
        return jnp.clip(b * rps - WIN_ROWS // 2, 0, rows - key_rows)

    window = (pl.Element(key_rows), pl.Element(GRID_W), pl.Element(d_na))

    return pl.pallas_call(
        functools.partial(_na_kernel, rows=rows, rows_per_step=rps),
        grid=(rows // rps,),
        in_specs=[
            pl.BlockSpec((rps * GRID_W, d_na), lambda b: (b, 0)),
            pl.BlockSpec(window, lambda b: (block_start(b), 0, d_na)),
            pl.BlockSpec(window, lambda b: (block_start(b), 0, 2 * d_na)),
            pl.BlockSpec(bias[0].shape, lambda b: (0, 0, 0, 0)),
            pl.BlockSpec(bias[1].shape, lambda b: (0, 0, 0, 0)),
            pl.BlockSpec((1, d_na), lambda b: (0, 0)),
        ],
        out_specs=pl.BlockSpec((rps * GRID_W, d_na), lambda b: (b, 0)),
        out_shape=jax.ShapeDtypeStruct((seq, d_na), BF16),
        compiler_params=_params("arbitrary"),
        name="na",
    )(qkv, qkv3, qkv3, bias[0], bias[1], g.reshape(1, d_na))


def _outproj_kernel(a_ref, b_ref, w_ref, x_ref, o_ref):
    da = a_ref.shape[-1]
    acc = jnp.dot(a_ref[...], w_ref[:da, :].astype(BF16), preferred_element_type=F32)
    acc = acc + jnp.dot(b_ref[...], w_ref[da:, :].astype(BF16), preferred_element_type=F32)
    o_ref[...] = x_ref[...] + acc


def _out_proj(y_ssm, y_na, w_out, x):
    seq, d_model = x.shape
    da, db = y_ssm.shape[1], y_na.shape[1]
    tm = min(2048, seq)
    tn = 512
    return pl.pallas_call(
        _outproj_kernel,
        grid=(seq // tm, d_model // tn),
        in_specs=[
            pl.BlockSpec((tm, da), lambda i, j: (i, 0)),
            pl.BlockSpec((tm, db), lambda i, j: (i, 0)),
            pl.BlockSpec((da + db, tn), lambda i, j: (0, j)),
            pl.BlockSpec((tm, tn), lambda i, j: (i, j)),
        ],
        out_specs=pl.BlockSpec((tm, tn), lambda i, j: (i, j)),
        out_shape=jax.ShapeDtypeStruct((seq, d_model), F32),
        compiler_params=_params("arbitrary", "arbitrary"),
        name="out_proj",
    )(y_ssm, y_na, w_out, x)


def _router_kernel(x_ref, g_ref, wt_ref, h_ref, a_ref):
    h = _rms(x_ref[...], g_ref[...])
    h_hi = h.astype(BF16)
    h_ref[...] = h_hi
    h_lo = (h - h_hi.astype(F32)).astype(BF16)
    w = wt_ref[...]
    w_hi = w.astype(BF16)
    w_lo = (w - w_hi.astype(F32)).astype(BF16)
    n_exp = w.shape[0]
    nt = (((1,), (1,)), ((), ()))
    both = lax.dot_general(jnp.concatenate([w_hi, w_lo], axis=0), h_hi, nt, preferred_element_type=F32)
    logits = both[:n_exp] + both[n_exp:] + lax.dot_general(w_hi, h_lo, nt, preferred_element_type=F32)
    m = jnp.max(logits, axis=0, keepdims=True)
    e = jnp.exp(logits - m)
    a_ref[...] = e / jnp.sum(e, axis=0, keepdims=True)


def _router(x1, g, w_router):
    seq, d_model = x1.shape
    n_exp = w_router.shape[1]
    tm = min(512, seq)
    return pl.pallas_call(
        _router_kernel,
        grid=(seq // tm,),
        in_specs=[
            pl.BlockSpec((tm, d_model), lambda i: (i, 0)),
            pl.BlockSpec((1, d_model), lambda i: (0, 0)),
            pl.BlockSpec((n_exp, d_model), lambda i: (0, 0)),
        ],
        out_specs=[
            pl.BlockSpec((tm, d_model), lambda i: (i, 0)),
            pl.BlockSpec((n_exp, tm), lambda i: (0, i)),
        ],
        out_shape=[
            jax.ShapeDtypeStruct((seq, d_model), BF16),
            jax.ShapeDtypeStruct((n_exp, seq), F32),
        ],
        compiler_params=_params("arbitrary"),
        name="router",
    )(x1, g.reshape(1, d_model), w_router.T)


def _topk_kernel(a_ref, posw_ref, gate_ref, ws_ref, nr_ref, *, cap, blk, win):
    a = a_ref[...]
    n_exp, seq = a.shape
    n_blk = seq // blk
    bits = pltpu.bitcast(a, jnp.int32)

    def bit_step(i, thr):
        cand = thr | jnp.left_shift(jnp.int32(1), 30 - i)
        cnt = jnp.sum((bits >= cand).astype(jnp.int32), axis=-1, keepdims=True)
        return jnp.where(cnt >= cap, cand, thr)

    thr = lax.fori_loop(0, 31, bit_step, jnp.zeros((n_exp, 1), jnp.int32))
    gt = bits > thr
    eq = bits == thr
    need = cap - jnp.sum(gt.astype(jnp.int32), axis=-1, keepdims=True)

    tri = (lax.broadcasted_iota(jnp.int32, (blk, blk), 0)
           <= lax.broadcasted_iota(jnp.int32, (blk, blk), 1)).astype(BF16)
    blk_of_tok = lax.broadcasted_iota(jnp.int32, (seq, n_blk), 0) // blk
    tok_to_blk = (blk_of_tok == lax.broadcasted_iota(jnp.int32, (seq, n_blk), 1)).astype(BF16)
    blk_before = (lax.broadcasted_iota(jnp.int32, (n_blk, n_blk), 0)
                  < lax.broadcasted_iota(jnp.int32, (n_blk, n_blk), 1)).astype(BF16)
    erow = lax.broadcasted_iota(jnp.int32, (2 * n_blk, seq), 0)
    ecol = lax.broadcasted_iota(jnp.int32, (2 * n_blk, seq), 1) // blk
    expand = jnp.where(erow == ecol, 32.0, jnp.where(erow - n_blk == ecol, 1.0, 0.0)).astype(BF16)

    def prefix_counts(mask):
        mb = jnp.where(mask, 1.0, 0.0).astype(BF16)
        local = jnp.concatenate(
            [jnp.dot(mb[:, b * blk:(b + 1) * blk], tri, preferred_element_type=F32) for b in range(n_blk)],
            axis=1)
        per_blk = jnp.dot(mb, tok_to_blk, preferred_element_type=F32)
        start = jnp.dot(per_blk.astype(BF16), blk_before, preferred_element_type=F32)
        hi = jnp.floor(start * (1.0 / 32.0))
        parts = jnp.concatenate([hi, start - 32.0 * hi], axis=1).astype(BF16)
        start_tok = jnp.dot(parts, expand, preferred_element_type=F32)
        return local + start_tok, start, start_tok, per_blk

    eq_incl, _, _, _ = prefix_counts(eq)
    sel = gt | (eq & (eq_incl - 1.0 < need.astype(F32)))
    incl, start, start_tok, per_blk = prefix_counts(sel)

    def window(s):
        return jnp.floor(s * (1.0 / MOE_WIN_ALIGN)) * MOE_WIN_ALIGN

    posw_ref[...] = jnp.where(sel, (incl - 1.0 - window(start_tok)).astype(jnp.int32), -1)
    gate_ref[...] = jnp.where(sel, a, 0.0)
    ws_ref[...] = window(start).astype(jnp.int32)
    span = start - window(start) + per_blk
    rounds = jnp.floor((span + float(win - 1)) * (1.0 / win))
    nr_ref[...] = jnp.max(rounds, axis=0, keepdims=True).astype(jnp.int32)


def _topk(aff_t, cap, blk, win):
    n_exp, seq = aff_t.shape
    n_blk = seq // blk
    full = lambda *_: (0, 0)
    return pl.pallas_call(
        functools.partial(_topk_kernel, cap=cap, blk=blk, win=win),
        grid=(1,),
        in_specs=[pl.BlockSpec((n_exp, seq), full)],
        out_specs=[pl.BlockSpec((n_exp, seq), full), pl.BlockSpec((n_exp, seq), full),
                   pl.BlockSpec((n_exp, n_blk), full), pl.BlockSpec((1, n_blk), full)],
        out_shape=[
            jax.ShapeDtypeStruct((n_exp, seq), jnp.int32),
            jax.ShapeDtypeStruct((n_exp, seq), F32),
            jax.ShapeDtypeStruct((n_exp, n_blk), jnp.int32),
            jax.ShapeDtypeStruct((1, n_blk), jnp.int32),
        ],
        compiler_params=_params("arbitrary"),
        name="topk",
    )(aff_t)


def _window(ws_ref, e, b, r, n_blk, win, cap):
    ws = ws_ref[e * n_blk + b] + r * win
    start = jnp.minimum(ws, cap - win)
    return pl.multiple_of(start, MOE_WIN_ALIGN), ws - start


def _gather_kernel(ws_ref, nr_ref, h_ref, rel_ref, xe_ref, *, blk, win, n_blk):
    n_exp, cap, _ = xe_ref.shape
    xe_ref[...] = jnp.zeros_like(xe_ref)
    slot = lax.broadcasted_iota(jnp.int32, (win, blk), 0)

    def one_round(b, r):
        rows = h_ref[pl.ds(pl.multiple_of(b * blk, blk), blk), :]
        rel = rel_ref[b]
        starts, hots = [], []
        for e in range(n_exp):
            start, shift = _window(ws_ref, e, b, r, n_blk, win, cap)
            relr = rel[e:e + 1, :] - r * win
            key = jnp.where(relr >= 0, relr + shift, -1)
            hots.append(jnp.where(slot == key, 1.0, 0.0).astype(BF16))
            starts.append(start)
        res = jnp.dot(jnp.concatenate(hots, axis=0), rows, preferred_element_type=F32)
        for e in range(n_exp):
            dst = pl.ds(starts[e], win)
            xe_ref[e, dst, :] = (xe_ref[e, dst, :].astype(F32) + res[e * win:(e + 1) * win]).astype(BF16)

    def extra_rounds(b):
        def body(r, carry):
            one_round(b, r)
            return carry
        lax.fori_loop(1, nr_ref[b], body, 0)

    def block_pair(bp, _):
        for h in range(2):
            one_round(2 * bp + h, 0)
        for h in range(2):
            extra_rounds(2 * bp + h)
        return 0

    lax.fori_loop(0, n_blk // 2, block_pair, 0)


def _moe_gather(ws_flat, n_rounds, h2, rel3, cap, win):
    seq, d_model = h2.shape
    n_blk, n_exp, blk = rel3.shape
    dq = d_model // 4
    grid_spec = pltpu.PrefetchScalarGridSpec(
        num_scalar_prefetch=2,
        grid=(4,),
        in_specs=[
            pl.BlockSpec((seq, dq), lambda c, ws, nr: (0, c)),
            pl.BlockSpec((n_blk, n_exp, blk), lambda c, ws, nr: (0, 0, 0)),
        ],
        out_specs=pl.BlockSpec((n_exp, cap, dq), lambda c, ws, nr: (0, 0, c)),
    )
    return pl.pallas_call(
        functools.partial(_gather_kernel, blk=blk, win=win, n_blk=n_blk),
        grid_spec=grid_spec,
        out_shape=jax.ShapeDtypeStruct((n_exp, cap, d_model), BF16),
        compiler_params=_params("arbitrary"),
        name="moe_gather",
    )(ws_flat, n_rounds, h2, rel3)


def _ffn_kernel(x_ref, wg_ref, wu_ref, wd_ref, y_ref, act_ref, *, n_f):
    s = pl.program_id(1)
    tf = wg_ref.shape[-1]

    @pl.when(s < n_f)
    def _():
        x = x_ref[0]
        g = jnp.dot(x, wg_ref[0].astype(BF16), preferred_element_type=F32)
        u = jnp.dot(x, wu_ref[0].astype(BF16), preferred_element_type=F32)
        act_ref[s] = (g * (1.0 / (1.0 + jnp.exp(-g))) * u).astype(BF16)

    @pl.when(s >= n_f)
    def _():
        acc = jnp.dot(act_ref[0], wd_ref[0, 0:tf, :].astype(BF16), preferred_element_type=F32)
        for f in range(1, n_f):
            acc = acc + jnp.dot(act_ref[f], wd_ref[0, f * tf:(f + 1) * tf, :].astype(BF16),
                                preferred_element_type=F32)
        y_ref[0] = acc.astype(BF16)


def _moe_ffn(xe, w_gate, w_up, w_down):
    n_exp, cap, d_model = xe.shape
    d_ff = w_gate.shape[-1]
    tf = 512
    tn = 1024
    n_f, n_n = d_ff // tf, d_model // tn
    up_tile = lambda e, s: (e, 0, jnp.minimum(s, n_f - 1))
    down_tile = lambda e, s: (e, 0, jnp.maximum(s - n_f, 0))
    return pl.pallas_call(
        functools.partial(_ffn_kernel, n_f=n_f),
        grid=(n_exp, n_f + n_n),
        in_specs=[
            pl.BlockSpec((1, cap, d_model), lambda e, s: (e, 0, 0)),
            pl.BlockSpec((1, d_model, tf), up_tile),
            pl.BlockSpec((1, d_model, tf), up_tile),
            pl.BlockSpec((1, d_ff, tn), down_tile),
        ],
        out_specs=pl.BlockSpec((1, cap, tn), down_tile),
        out_shape=jax.ShapeDtypeStruct((n_exp, cap, d_model), BF16),
        scratch_shapes=[pltpu.VMEM((n_f, cap, tf), BF16)],
        compiler_params=_params("arbitrary", "arbitrary"),
        name="moe_ffn",
    )(xe, w_gate, w_up, w_down)


def _combine_kernel(ws_ref, nr_ref, ye_ref, x_ref, rel_ref, gate_ref, o_ref, *, blk, win, n_blk):
    n_exp, cap, _ = ye_ref.shape
    sub = x_ref.shape[0] // blk
    slot = lax.broadcasted_iota(jnp.int32, (win, blk), 0)

    def one_round(b, r):
        rel = rel_ref[b]
        gate = gate_ref[b]
        gates, wins = [], []
        for e in range(n_exp):
            start, shift = _window(ws_ref, e, b, r, n_blk, win, cap)
            relr = rel[e:e + 1, :] - r * win
            key = jnp.where(relr >= 0, relr + shift, -1)
            gates.append(jnp.where(slot == key, gate[e:e + 1, :], 0.0).astype(BF16))
            wins.append(ye_ref[e, pl.ds(start, win), :])
        return lax.dot_general(jnp.concatenate(gates, axis=0), jnp.concatenate(wins, axis=0),
                               (((0,), (0,)), ((), ())), preferred_element_type=F32)

    for s in range(sub):
        tok = slice(s * blk, (s + 1) * blk)
        o_ref[tok, :] = x_ref[tok, :] + one_round(pl.program_id(1) * sub + s, 0)
    for s in range(sub):
        tok = slice(s * blk, (s + 1) * blk)
        b = pl.program_id(1) * sub + s

        def extra(r, carry, b=b, tok=tok):
            o_ref[tok, :] += one_round(b, r)
            return carry

        lax.fori_loop(1, nr_ref[b], extra, 0)


def _moe_combine(ws_flat, n_rounds, ye, x1, rel3, gate3, win):
    seq, d_model = x1.shape
    n_exp, cap, _ = ye.shape
    n_blk, _, blk = rel3.shape
    dq = d_model // 4
    tile = min(4, n_blk) * blk
    whole = lambda c, t, ws, nr: (0, 0, 0)
    grid_spec = pltpu.PrefetchScalarGridSpec(
        num_scalar_prefetch=2,
        grid=(4, seq // tile),
        in_specs=[
            pl.BlockSpec((n_exp, cap, dq), lambda c, t, ws, nr: (0, 0, c)),
            pl.BlockSpec((tile, dq), lambda c, t, ws, nr: (t, c)),
            pl.BlockSpec(rel3.shape, whole),
            pl.BlockSpec(gate3.shape, whole),
        ],
        out_specs=pl.BlockSpec((tile, dq), lambda c, t, ws, nr: (t, c)),
    )
    return pl.pallas_call(
        functools.partial(_combine_kernel, blk=blk, win=win, n_blk=n_blk),
        grid_spec=grid_spec,
        out_shape=jax.ShapeDtypeStruct((seq, d_model), F32),
        compiler_params=_params("arbitrary", "arbitrary"),
        name="moe_combine",
    )(ws_flat, n_rounds, ye, x1, rel3, gate3)


def _final_norm_kernel(x_ref, g_ref, o_ref):
    o_ref[...] = _rms(x_ref[...], g_ref[...])


def _final_norm(x, g):
    seq, d_model = x.shape
    tm = min(512, seq)
    return pl.pallas_call(
        _final_norm_kernel,
        grid=(seq // tm,),
        in_specs=[pl.BlockSpec((tm, d_model), lambda i: (i, 0)), pl.BlockSpec((1, d_model), lambda i: (0, 0))],
        out_specs=pl.BlockSpec((tm, d_model), lambda i: (i, 0)),
        out_shape=jax.ShapeDtypeStruct((seq, d_model), F32),
        compiler_params=_params("arbitrary"),
        name="final_norm",
    )(x, g.reshape(1, d_model))


def _layer(x, norm_mix_g, w_in, fwd, bwd, ssm_d, w_glu, b_glu, na_rpb, g_ssm_out, g_na_out, w_out,
           norm_ffn_g, w_router, w_gate, w_up, w_down):
    seq, d_model = x.shape
    d_ssm = ssm_d.shape[0]
    d_na = g_na_out.shape[0]
    n_exp = w_router.shape[1]
    cap = EC_CAPACITY_FACTOR * seq // n_exp
    blk = min(MOE_TOK_BLOCK, cap // 2)
    win = min(MOE_WIN, cap)

    u, qkv = _in_proj(x, norm_mix_g, w_in, d_ssm, d_na)
    y_ssm = _s5_post(_s5_scan(u, fwd, bwd), u, ssm_d, w_glu, b_glu, g_ssm_out)
    y_na = _neighbourhood_attention(qkv, na_rpb, g_na_out)
    x1 = _out_proj(y_ssm, y_na, w_out, x)

    h2, aff_t = _router(x1, norm_ffn_g, w_router)
    rel, gate, ws, n_rounds = _topk(aff_t, cap, blk, win)
    ws_flat = ws.reshape(-1)
    n_rounds = n_rounds.reshape(-1)
    rel3 = jnp.swapaxes(rel.reshape(n_exp, seq // blk, blk), 0, 1)
    gate3 = jnp.swapaxes(gate.reshape(n_exp, seq // blk, blk), 0, 1)
    xe = _moe_gather(ws_flat, n_rounds, h2, rel3, cap, win)
    ye = _moe_ffn(xe, w_gate, w_up, w_down)
    return _moe_combine(ws_flat, n_rounds, ye, x1, rel3, gate3, win)


def kernel(x, norm_mix_g, w_in, a_re_fwd, a_im_fwd, log_dt_fwd, b_re_fwd, b_im_fwd, c_re_fwd, c_im_fwd, a_re_bwd, a_im_bwd, log_dt_bwd, b_re_bwd, b_im_bwd, c_re_bwd, c_im_bwd, ssm_d, w_glu, b_glu, na_rpb, g_ssm_out, g_na_out, w_out, norm_ffn_g, w_router, w_gate, w_up, w_down, norm_final_g):
    bsz = x.shape[0]
    depth = w_in.shape[0]
    outs = []
    for b in range(bsz):
        xb = x[b]
        for l in range(depth):
            fwd = (a_re_fwd[l], a_im_fwd[l], log_dt_fwd[l], b_re_fwd[l], b_im_fwd[l], c_re_fwd[l], c_im_fwd[l])
            bwd = (a_re_bwd[l], a_im_bwd[l], log_dt_bwd[l], b_re_bwd[l], b_im_bwd[l], c_re_bwd[l], c_im_bwd[l])
            xb = _layer(xb, norm_mix_g[l], w_in[l], fwd, bwd, ssm_d[l], w_glu[l], b_glu[l], na_rpb[l],
                        g_ssm_out[l], g_na_out[l], w_out[l], norm_ffn_g[l], w_router[l],
                        w_gate[l], w_up[l], w_down[l])
        outs.append(_final_norm(xb, norm_final_g))
    return jnp.stack(outs)
```

```python
import functools

import numpy as np
import jax
import jax.numpy as jnp
from jax import lax
from jax.experimental import pallas as pl
from jax.experimental.pallas import tpu as pltpu

F32 = jnp.float32
BF16 = jnp.bfloat16

RMS_EPS = 1e-6
SSM_GROUP = 16
SSM_STATE = 64
NA_HEADS = 16
NA_HEAD_DIM = 64
GRID_W = 64
WIN_ROWS = 8
WIN_COLS = 16
N_EXPERTS = 16
EC_CAPACITY_FACTOR = 2

S5_CHUNK = 16
S5_SEGS = 8
HEADS_PER_DOT = 4
MOE_TOK_BLOCK = 256
MOE_WIN_ALIGN = 16
MOE_WIN = 64
MASK_NEG = -1e30
LOG2_E = float(np.log2(np.e))

VMEM_LIMIT_BYTES = 56 * 1024 * 1024


def _params(*semantics):
    return pltpu.CompilerParams(dimension_semantics=semantics, vmem_limit_bytes=VMEM_LIMIT_BYTES)


def _rms(x, g):
    ms = jnp.mean(x * x, axis=-1, keepdims=True)
    return x * lax.rsqrt(ms + RMS_EPS) * g


def _inproj_kernel(x_ref, g_ref, w_ref, u_ref, qkv_ref, h_scr, *, n_u, n_q, q_scale):
    j = pl.program_id(1)

    @pl.when(j == 0)
    def _():
        h_scr[...] = _rms(x_ref[...], g_ref[...]).astype(BF16)

    def project():
        return jnp.dot(h_scr[...], w_ref[...].astype(BF16), preferred_element_type=F32)

    @pl.when(j < n_u)
    def _():
        u_ref[...] = project()

    @pl.when(j >= n_u)
    def _():
        scale = jnp.where(j < n_u + n_q, q_scale, 1.0).astype(F32)
        qkv_ref[...] = (project() * scale).astype(BF16)


def _in_proj(x, g, w_in, d_ssm, d_na):
    seq, d_model = x.shape
    tm = min(1024, seq)
    tn = 1024
    n_u, n_q = d_ssm // tn, d_na // tn
    n_cols = w_in.shape[1] // tn
    kern = functools.partial(_inproj_kernel, n_u=n_u, n_q=n_q, q_scale=NA_HEAD_DIM ** -0.5 * LOG2_E)
    return pl.pallas_call(
        kern,
        grid=(seq // tm, n_cols),
        in_specs=[
            pl.BlockSpec((tm, d_model), lambda i, j: (i, 0)),
            pl.BlockSpec((1, d_model), lambda i, j: (0, 0)),
            pl.BlockSpec((d_model, tn), lambda i, j: (0, j)),
        ],
        out_specs=[
            pl.BlockSpec((tm, tn), lambda i, j: (i, jnp.minimum(j, n_u - 1))),
            pl.BlockSpec((tm, tn), lambda i, j: (i, jnp.maximum(j - n_u, 0))),
        ],
        out_shape=[
            jax.ShapeDtypeStruct((seq, d_ssm), F32),
            jax.ShapeDtypeStruct((seq, 3 * d_na), BF16),
        ],
        scratch_shapes=[pltpu.VMEM((tm, d_model), BF16)],
        compiler_params=_params("arbitrary", "arbitrary"),
        name="in_proj",
    )(x, g.reshape(1, d_model), w_in)


def _cmul(ar, ai, br, bi):
    return ar * br - ai * bi, ar * bi + ai * br


def _s5_tables(fwd, bwd, n_tiles):
    t_len = S5_CHUNK
    a_re, a_im, log_dt, b_re, b_im, c_re, c_im = (jnp.stack([f, b]).astype(F32) for f, b in zip(fwd, bwd))
    _, n_grp, n_st = a_re.shape
    n_ch = b_re.shape[-1]
    dt = jnp.exp(log_dt)[:, :, None]
    xr, xi = a_re * dt, a_im * dt
    steps = np.arange(t_len)

    def power(x_r, x_i, exps):
        e = jnp.asarray(exps, F32).reshape(exps.shape + (1,) * (x_r.ndim - 1))
        mag = jnp.exp(x_r[None] * e)
        return mag * jnp.cos(x_i[None] * e), mag * jnp.sin(x_i[None] * e)

    def per_step(exps):
        p_r, p_i = power(xr, xi, exps)
        return jnp.transpose(p_r, (1, 2, 0, 3)), jnp.transpose(p_i, (1, 2, 0, 3))

    a1_r, a1_i = power(xr, xi, np.ones((1, 2)))
    nr, ni = a1_r[0] - 1.0, a1_i[0]
    den = a_re * a_re + a_im * a_im
    qr, qi = (nr * a_re + ni * a_im) / den, (ni * a_re - nr * a_im) / den
    bb_r, bb_i = _cmul(qr[:, :, None, :], qi[:, :, None, :], jnp.swapaxes(b_re, 2, 3), jnp.swapaxes(b_im, 2, 3))
    am_r, am_i = per_step(np.stack([steps, steps[::-1]], axis=1))
    y_r, y_i = _cmul(c_re[:, :, None], c_im[:, :, None], am_r[:, :, :, None, :], am_i[:, :, :, None, :])
    y = jnp.concatenate([y_r, y_i], axis=-1).reshape(2, n_grp, t_len * n_ch, 2 * n_st)
    lag = jnp.einsum('dgck,dgqk->dgcq', jnp.concatenate([bb_r, -bb_i], axis=-1), y,
                     precision=lax.Precision.HIGH)
    st = per_step(np.stack([t_len - 1 - steps, steps], axis=1)) + (bb_r, bb_i)
    wo = per_step(np.stack([steps + 1, t_len - steps], axis=1)) + (c_re, c_im)
    xrf, xif = xr.reshape(2, -1), xi.reshape(2, -1)
    tiles = np.arange(n_tiles)
    ends = power(xrf, xif, np.array([[t_len, t_len], [t_len * n_tiles, t_len * n_tiles]]))
    ap = jnp.stack(ends, axis=2).reshape(8, -1)
    at_r, at_i = power(xrf, xif, t_len * np.stack([tiles, tiles[::-1]], axis=1))
    pw = jnp.transpose(jnp.stack([at_r, at_i], axis=2), (1, 2, 0, 3)).reshape(4, n_tiles, -1)
    return lag, st, wo, ap, pw


def _transpose_lane_blocks(groups, width):
    groups = [list(g) for g in groups]
    n = len(groups[0])
    first = groups[0][0]
    lanes = first.shape[-1]
    axis = first.ndim - 1
    block = lax.broadcasted_iota(jnp.int32, first.shape, axis) // width
    d = n // 2
    while d:
        upper = (block & d) != 0
        pairs = [(g, i) for g in groups for i in range(n) if i & d == 0]
        up = [pltpu.roll(g[i | d], d * width, axis) for g, i in pairs]
        down = [pltpu.roll(g[i], lanes - d * width, axis) for g, i in pairs]
        for (g, i), u, dn in zip(pairs, up, down):
            g[i], g[i | d] = jnp.where(upper, u, g[i]), jnp.where(upper, g[i | d], dn)
        d //= 2
    return groups


def _s5_a_kernel(u_ref, lag_ref, pe_r_ref, pe_i_ref, bb_r_ref, bb_i_ref, y_ref, s_ref, w_grp, e_pair, ucat,
                 *, n_tiles, seg_stride):
    n_grp = lag_ref.shape[1]
    cw = S5_CHUNK * SSM_GROUP
    n_plane, n_state = 4, pe_r_ref.shape[-1]
    lanes = u_ref.shape[-1]
    pc = n_plane * 2 * n_state

    @pl.when(pl.program_id(0) == 0)
    def _():
        e_pair[...] = jnp.zeros_like(e_pair)

    for g in range(n_grp):
        blank = jnp.zeros((SSM_GROUP, cw), F32)
        strip = jnp.concatenate([blank, lag_ref[0, g], lag_ref[1, g], blank], axis=1)
        for t in range(S5_CHUNK):
            r0 = t * SSM_GROUP
            fwd_lo = cw - t * SSM_GROUP
            bwd_lo = 2 * cw + (S5_CHUNK - 1 - t) * SSM_GROUP
            piece = strip[:, fwd_lo:fwd_lo + cw] + strip[:, bwd_lo:bwd_lo + cw]
            w_grp[g, r0:r0 + SSM_GROUP, :] = piece.astype(BF16)
            for d in range(2):
                planes = _cmul(pe_r_ref[d, g, t:t + 1, :], pe_i_ref[d, g, t:t + 1, :], bb_r_ref[d, g], bb_i_ref[d, g])
                for ri in range(2):
                    c0 = (2 * d + ri) * 2 * n_state + (g % 2) * n_state
                    rows = slice((g % 2) * cw + r0, (g % 2) * cw + r0 + SSM_GROUP)
                    e_pair[g // 2, rows, c0:c0 + n_state] = planes[ri].astype(BF16)

    per_trip = 4

    def gather(trip, _):
        groups, where = [], []
        for i in range(per_trip):
            jj = per_trip * trip + i
            pieces = []
            for t in range(S5_CHUNK):
                rows = [u_ref[pl.ds(S5_CHUNK * (2 * jj + h) + t, S5_SEGS, stride=seg_stride), :] for h in range(2)]
                pieces.append(jnp.concatenate(rows, axis=0))
            dst = pl.ds(pl.multiple_of(jj * 2 * S5_SEGS, 2 * S5_SEGS), 2 * S5_SEGS)
            for half in range(S5_CHUNK // n_grp):
                groups.append(pieces[half * n_grp:(half + 1) * n_grp])
                where.append((dst, half))
        for (dst, half), tiles in zip(where, _transpose_lane_blocks(groups, SSM_GROUP)):
            for g in range(n_grp):
                c0 = g * cw + half * lanes
                ucat[dst, c0:c0 + lanes] = tiles[g].astype(BF16)
        return 0

    lax.fori_loop(0, n_tiles // (2 * per_trip), gather, 0)
    ug = ucat[...]
    for g in range(n_grp):
        y_ref[:, g * cw:(g + 1) * cw] = jnp.dot(ug[:, g * cw:(g + 1) * cw], w_grp[g], preferred_element_type=F32)
    for q in range(n_grp // 2):
        s_ref[:, q * pc:(q + 1) * pc] = jnp.dot(ug[:, q * 2 * cw:(q + 1) * 2 * cw], e_pair[q],
                                                preferred_element_type=F32)


def _s5_bc_kernel(s_ref, yi_ref, po_r_ref, po_i_ref, c_r_ref, c_i_ref, ap_ref, pw_ref, y_ref, sin, ychunk, c_blk,
                  sin_bf, *, n_tiles, seg_stride):
    n_plane = 4
    _, n_grp, _, n_state = po_r_ref.shape
    cw = S5_CHUNK * SSM_GROUP
    n_pair = n_grp // 2
    pw_ = 2 * n_state
    pc = n_plane * pw_
    lanes = y_ref.shape[-1]

    @pl.when(pl.program_id(0) == 0)
    def _():
        c_blk[...] = jnp.zeros_like(c_blk)

    for g in range(n_grp):
        gl = g % 2
        for d in range(2):
            for t in range(S5_CHUNK):
                w_r, w_i = _cmul(c_r_ref[d, g], c_i_ref[d, g], po_r_ref[d, g, t:t + 1, :], po_i_ref[d, g, t:t + 1, :])
                rows = slice(gl * cw + t * SSM_GROUP, gl * cw + (t + 1) * SSM_GROUP)
                for ri, plane in enumerate((w_r, -w_i)):
                    c0 = (2 * d + ri) * pw_ + gl * n_state
                    c_blk[g // 2, rows, c0:c0 + n_state] = plane.astype(BF16)

    chains = [(q, d) for q in range(n_pair) for d in range(2)]

    def plane_lanes(q, d):
        base = q * pc + d * 2 * pw_
        return slice(base, base + pw_), slice(base + pw_, base + 2 * pw_)

    def coef(row, q):
        return ap_ref[row:row + 1, q * pw_:(q + 1) * pw_]

    def tile_rows(n, d):
        j = n if d == 0 else n_tiles - 1 - n
        return pl.ds(pl.multiple_of(j * S5_SEGS, S5_SEGS), S5_SEGS)

    def step(n, carry):
        out = []
        for (q, d), (zr, zi) in zip(chains, carry):
            re, im = plane_lanes(q, d)
            rows = tile_rows(n, d)
            sin[rows, re] = zr
            sin[rows, im] = zi
            ar, ai = coef(2 * d, q), coef(2 * d + 1, q)
            out.append((ar * zr - ai * zi + s_ref[rows, re], ar * zi + ai * zr + s_ref[rows, im]))
        return tuple(out)

    z0 = jnp.zeros((S5_SEGS, pw_), F32)
    ends = lax.fori_loop(0, n_tiles, step, tuple((z0, z0) for _ in chains))

    carries = []
    for (q, d), (zr, zi) in zip(chains, ends):
        sr, si = coef(4 + 2 * d, q), coef(5 + 2 * d, q)
        cr = jnp.zeros((1, pw_), F32)
        ci = jnp.zeros((1, pw_), F32)
        seg_r = [None] * S5_SEGS
        seg_i = [None] * S5_SEGS
        for s in (range(S5_SEGS) if d == 0 else range(S5_SEGS - 1, -1, -1)):
            seg_r[s], seg_i[s] = cr, ci
            cr, ci = (zr[s:s + 1] + sr * cr - si * ci, zi[s:s + 1] + sr * ci + si * cr)
        carries.append((jnp.concatenate(seg_r, axis=0), jnp.concatenate(seg_i, axis=0)))

    def fix(n8, _):
        tiles = pl.ds(pl.multiple_of(n8 * 8, 8), 8)
        for (q, d), (car_r, car_i) in zip(chains, carries):
            re, im = plane_lanes(q, d)
            pr8 = pw_ref[2 * d, tiles, q * pw_:(q + 1) * pw_]
            pi8 = pw_ref[2 * d + 1, tiles, q * pw_:(q + 1) * pw_]
            car2_r = jnp.concatenate([car_r, car_r], axis=0)
            car2_i = jnp.concatenate([car_i, car_i], axis=0)
            for r in range(0, 8, 2):
                rows = pl.ds(pl.multiple_of((n8 * 8 + r) * S5_SEGS, 2 * S5_SEGS), 2 * S5_SEGS)
                pr = jnp.concatenate([jnp.broadcast_to(pr8[r + h:r + h + 1], (S5_SEGS, pw_)) for h in range(2)], axis=0)
                pi = jnp.concatenate([jnp.broadcast_to(pi8[r + h:r + h + 1], (S5_SEGS, pw_)) for h in range(2)], axis=0)
                sin_bf[rows, re] = (sin[rows, re] + (pr * car2_r - pi * car2_i)).astype(BF16)
                sin_bf[rows, im] = (sin[rows, im] + (pr * car2_i + pi * car2_r)).astype(BF16)
        return 0

    lax.fori_loop(0, n_tiles // 8, fix, 0)

    for q in range(n_pair):
        lhs = sin_bf[:, q * pc:(q + 1) * pc]
        carried = lax.dot_general(lhs, c_blk[q], (((1,), (1,)), ((), ())), preferred_element_type=F32)
        ychunk[:, q * 2 * cw:(q + 1) * 2 * cw] = yi_ref[:, q * 2 * cw:(q + 1) * 2 * cw] + carried

    per_trip = 8

    def scatter(jj, _):
        groups, where = [], []
        for h in range(per_trip):
            j = per_trip * jj + h
            src = pl.ds(pl.multiple_of(j * S5_SEGS, S5_SEGS), S5_SEGS)
            for half in range(S5_CHUNK // n_grp):
                groups.append([ychunk[src, g * cw + half * lanes:g * cw + (half + 1) * lanes] for g in range(n_grp)])
                where.append(S5_CHUNK * j + half * n_grp)
        for t0, tiles in zip(where, _transpose_lane_blocks(groups, SSM_GROUP)):
            for k, tile in enumerate(tiles):
                y_ref[pl.ds(t0 + k, S5_SEGS, stride=seg_stride), :] = tile
        return 0

    lax.fori_loop(0, n_tiles // per_trip, scatter, 0)


def _s5_scan(u, fwd, bwd):
    seq, d_ssm = u.shape
    n_groups = d_ssm // SSM_GROUP
    n_state = SSM_STATE
    cw = S5_CHUNK * SSM_GROUP
    n_rows = seq // S5_CHUNK
    n_tiles = n_rows // S5_SEGS
    lanes = 128
    gpb = lanes // SSM_GROUP
    n_blocks = d_ssm // lanes
    bw = gpb * cw
    sw = gpb * 4 * n_state
    seg_stride = n_tiles * S5_CHUNK

    lag, st, wo, ap, pw = _s5_tables(fwd, bwd, n_tiles)

    def per_block(t):
        return pl.BlockSpec((t.shape[0], gpb) + t.shape[2:], lambda i: (0, i) + (0,) * (t.ndim - 2))

    y_intra, s_loc = pl.pallas_call(
        functools.partial(_s5_a_kernel, n_tiles=n_tiles, seg_stride=seg_stride),
        grid=(n_blocks,),
        in_specs=[
            pl.BlockSpec((seq, lanes), lambda i: (0, i)),
            per_block(lag), *[per_block(t) for t in st],
        ],
        out_specs=[
            pl.BlockSpec((n_rows, bw), lambda i: (0, i)),
            pl.BlockSpec((n_rows, sw), lambda i: (0, i)),
        ],
        out_shape=[
            jax.ShapeDtypeStruct((n_rows, n_blocks * bw), F32),
            jax.ShapeDtypeStruct((n_rows, n_blocks * sw), F32),
        ],
        scratch_shapes=[
            pltpu.VMEM((gpb, cw, cw), BF16),
            pltpu.VMEM((gpb // 2, 2 * cw, 8 * n_state), BF16),
            pltpu.VMEM((n_rows, S5_CHUNK * lanes), BF16),
        ],
        compiler_params=_params("arbitrary"),
        name="s5_a",
    )(u, lag, *st)

    pl_lanes = gpb * n_state
    return pl.pallas_call(
        functools.partial(_s5_bc_kernel, n_tiles=n_tiles, seg_stride=seg_stride),
        grid=(n_blocks,),
        in_specs=[
            pl.BlockSpec((n_rows, sw), lambda i: (0, i)),
            pl.BlockSpec((n_rows, bw), lambda i: (0, i)),
            *[per_block(t) for t in wo],
            pl.BlockSpec((8, pl_lanes), lambda i: (0, i)),
            pl.BlockSpec((4, n_tiles, pl_lanes), lambda i: (0, 0, i)),
        ],
        out_specs=pl.BlockSpec((seq, lanes), lambda i: (0, i)),
        out_shape=jax.ShapeDtypeStruct((seq, d_ssm), F32),
        scratch_shapes=[
            pltpu.VMEM((n_rows, sw), F32),
            pltpu.VMEM((n_rows, bw), F32),
            pltpu.VMEM((gpb // 2, 2 * cw, 8 * n_state), BF16),
            pltpu.VMEM((n_rows, sw), BF16),
        ],
        compiler_params=_params("arbitrary"),
        name="s5_bc",
    )(s_loc, y_intra, *wo, ap, pw)


def _s5_post_kernel(y_ref, u_ref, d_ref, w_ref, b_ref, g_ref, o_ref):
    y = y_ref[...] + d_ref[...] * u_ref[...]
    c0 = np.float32(np.sqrt(2.0 / np.pi))
    y = 0.5 * y * (1.0 + jnp.tanh(c0 * (y + np.float32(0.044715) * (y * y * y))))
    z = jnp.dot(y.astype(BF16), w_ref[...].astype(BF16), preferred_element_type=F32) + b_ref[...]
    o = y * (1.0 / (1.0 + jnp.exp(-z)))
    o_ref[...] = _rms(o, g_ref[...]).astype(BF16)


def _s5_post(y, u, d_skip, w_glu, b_glu, g):
    seq, d = y.shape
    tm = min(512, seq)
    row = lambda i: (i, 0)
    fix = lambda i: (0, 0)
    return pl.pallas_call(
        _s5_post_kernel,
        grid=(seq // tm,),
        in_specs=[
            pl.BlockSpec((tm, d), row), pl.BlockSpec((tm, d), row), pl.BlockSpec((1, d), fix),
            pl.BlockSpec((d, d), fix), pl.BlockSpec((1, d), fix), pl.BlockSpec((1, d), fix),
        ],
        out_specs=pl.BlockSpec((tm, d), row),
        out_shape=jax.ShapeDtypeStruct((seq, d), BF16),
        compiler_params=_params("arbitrary"),
        name="s5_post",
    )(y, u, d_skip.reshape(1, d), w_glu, b_glu.reshape(1, d), g.reshape(1, d))


def _na_bias_table(rpb):
    n_heads = rpb.shape[0]
    n_quad = n_heads // HEADS_PER_DOT
    n_dy, n_dx = rpb.shape[1], rpb.shape[2]
    cols = np.arange(GRID_W)
    col_start = np.clip(cols - WIN_COLS // 2, 0, GRID_W - WIN_COLS)
    key_cols = np.arange(GRID_W)
    in_win = (key_cols[None, :] >= col_start[:, None]) & (key_cols[None, :] < col_start[:, None] + WIN_COLS)
    dx = key_cols[None, :] - cols[:, None] + (WIN_COLS - 1)
    pick_x = (dx[:, :, None] == np.arange(n_dx)).astype(np.float32)
    pick = np.zeros((2, GRID_W, 2 * GRID_W, n_dx), np.float32)
    for j in range(2):
        pick[j, :, j * GRID_W:(j + 1) * GRID_W] = pick_x
    dy = 2 * np.arange(WIN_ROWS)[None, :, None] + np.arange(2)[:, None, None] + np.arange(2)[None, None, :]
    pick_y = (dy[..., None] == np.arange(n_dy)).astype(np.float32)
    hp = lax.Precision.HIGHEST
    rows = jnp.einsum('spjy,hyx->spjhx', pick_y, rpb.astype(F32) * LOG2_E, precision=hp)
    rows = rows.reshape(2, WIN_ROWS, 2, n_quad, HEADS_PER_DOT, n_dx)
    b = jnp.einsum('spjqhx,jclx->sqphcl', rows, pick, precision=hp)
    keep = np.concatenate([in_win, in_win], axis=1)
    b = jnp.where(keep, b, MASK_NEG)
    return b.reshape(2, n_quad, WIN_ROWS, HEADS_PER_DOT * GRID_W, 2 * GRID_W)


def _na_kernel(q_ref, k_ref, v_ref, b_ref, g_ref, o_ref, *, rows, rows_per_step):
    n_keys = WIN_ROWS * GRID_W
    d_na = k_ref.shape[-1]
    pw = HEADS_PER_DOT * NA_HEAD_DIM
    row_head = lax.broadcasted_iota(jnp.int32, (HEADS_PER_DOT * GRID_W, pw), 0) // GRID_W
    col_head = lax.broadcasted_iota(jnp.int32, (HEADS_PER_DOT * GRID_W, pw), 1) // NA_HEAD_DIM
    diag = row_head == col_head
    out_head = lax.broadcasted_iota(jnp.int32, (GRID_W, pw), 1) // NA_HEAD_DIM
    first = pl.program_id(0) * rows_per_step
    block_start = jnp.clip(first - WIN_ROWS // 2, 0, rows - k_ref.shape[0])

    def one_row(i, _):
        r = first + i
        win_start = jnp.clip(r - WIN_ROWS // 2, 0, rows - WIN_ROWS)
        dy0 = win_start - r + (WIN_ROWS - 1)
        k = k_ref[pl.ds(win_start - block_start, WIN_ROWS)].reshape(n_keys, d_na)
        v = v_ref[pl.ds(win_start - block_start, WIN_ROWS)].reshape(n_keys, d_na)
        q_rows = pl.ds(pl.multiple_of(i * GRID_W, GRID_W), GRID_W)
        outs = []
        for p in range(d_na // pw):
            sl = slice(p * pw, (p + 1) * pw)
            q4 = q_ref[q_rows, sl]
            qbd = jnp.where(diag, jnp.concatenate([q4] * HEADS_PER_DOT, axis=0), jnp.zeros((), BF16))
            s = lax.dot_general(qbd, k[:, sl], (((1,), (1,)), ((), ())), preferred_element_type=F32)
            s = s + jnp.concatenate([b_ref[dy0 % 2, p, dy0 // 2 + j] for j in range(WIN_ROWS // 2)], axis=1)
            m = jnp.max(s, axis=-1, keepdims=True)
            e = jnp.exp2(s - m)
            l = jnp.sum(e, axis=-1, keepdims=True)
            o = jnp.dot(e.astype(BF16), v[:, sl], preferred_element_type=F32) / l
            acc = jnp.zeros((GRID_W, pw), F32)
            for h in range(HEADS_PER_DOT):
                acc = acc + jnp.where(out_head == h, o[h * GRID_W:(h + 1) * GRID_W], 0.0)
            outs.append(acc)
        y = jnp.concatenate(outs, axis=1)
        o_ref[q_rows, :] = _rms(y, g_ref[...]).astype(BF16)
        return 0

    lax.fori_loop(0, rows_per_step, one_row, 0, unroll=True)


def _neighbourhood_attention(qkv, rpb, g):
    seq = qkv.shape[0]
    d_na = qkv.shape[1] // 3
    rows = seq // GRID_W
    bias = _na_bias_table(rpb)
    qkv3 = qkv.reshape(rows, GRID_W, 3 * d_na)

    rps = 4
    key_rows = 2 * WIN_ROWS

    def block_start(b):
        return jnp.clip(b * rps - WIN_ROWS // 2, 0, rows - key_rows)

    window = (pl.Element(key_rows), pl.Element(GRID_W), pl.Element(d_na))

    return pl.pallas_call(
        functools.partial(_na_kernel, rows=rows, rows_per_step=rps),
        grid=(rows // rps,),
        in_specs=[
            pl.BlockSpec((rps * GRID_W, d_na), lambda b: (b, 0)),
            pl.BlockSpec(window, lambda b: (block_start(b), 0, d_na)),
            pl.BlockSpec(window, lambda b: (block_start(b), 0, 2 * d_na)),
            pl.BlockSpec(bias.shape, lambda b: (0, 0, 0, 0, 0)),
            pl.BlockSpec((1, d_na), lambda b: (0, 0)),
        ],
        out_specs=pl.BlockSpec((rps * GRID_W, d_na), lambda b: (b, 0)),
        out_shape=jax.ShapeDtypeStruct((seq, d_na), BF16),
        compiler_params=_params("arbitrary"),
        name="na",
    )(qkv, qkv3, qkv3, bias, g.reshape(1, d_na))


def _outproj_kernel(a_ref, b_ref, w_ref, x_ref, o_ref):
    da = a_ref.shape[-1]
    acc = jnp.dot(a_ref[...], w_ref[:da, :].astype(BF16), preferred_element_type=F32)
    acc = acc + jnp.dot(b_ref[...], w_ref[da:, :].astype(BF16), preferred_element_type=F32)
    o_ref[...] = x_ref[...] + acc


def _out_proj(y_ssm, y_na, w_out, x):
    seq, d_model = x.shape
    da, db = y_ssm.shape[1], y_na.shape[1]
    tm = min(2048, seq)
    tn = 512
    return pl.pallas_call(
        _outproj_kernel,
        grid=(seq // tm, d_model // tn),
        in_specs=[
            pl.BlockSpec((tm, da), lambda i, j: (i, 0)),
            pl.BlockSpec((tm, db), lambda i, j: (i, 0)),
            pl.BlockSpec((da + db, tn), lambda i, j: (0, j)),
            pl.BlockSpec((tm, tn), lambda i, j: (i, j)),
        ],
        out_specs=pl.BlockSpec((tm, tn), lambda i, j: (i, j)),
        out_shape=jax.ShapeDtypeStruct((seq, d_model), F32),
        compiler_params=_params("arbitrary", "arbitrary"),
        name="out_proj",
    )(y_ssm, y_na, w_out, x)


def _router_kernel(x_ref, g_ref, wt_ref, h_ref, a_ref):
    h = _rms(x_ref[...], g_ref[...])
    h_hi = h.astype(BF16)
    h_ref[...] = h_hi
    h_lo = (h - h_hi.astype(F32)).astype(BF16)
    w = wt_ref[...]
    w_hi = w.astype(BF16)
    w_lo = (w - w_hi.astype(F32)).astype(BF16)
    n_exp = w.shape[0]
    nt = (((1,), (1,)), ((), ()))
    both = lax.dot_general(jnp.concatenate([w_hi, w_lo], axis=0), h_hi, nt, preferred_element_type=F32)
    logits = both[:n_exp] + both[n_exp:] + lax.dot_general(w_hi, h_lo, nt, preferred_element_type=F32)
    m = jnp.max(logits, axis=0, keepdims=True)
    e = jnp.exp(logits - m)
    a_ref[...] = e / jnp.sum(e, axis=0, keepdims=True)


def _router(x1, g, w_router):
    seq, d_model = x1.shape
    n_exp = w_router.shape[1]
    tm = min(512, seq)
    return pl.pallas_call(
        _router_kernel,
        grid=(seq // tm,),
        in_specs=[
            pl.BlockSpec((tm, d_model), lambda i: (i, 0)),
            pl.BlockSpec((1, d_model), lambda i: (0, 0)),
            pl.BlockSpec((n_exp, d_model), lambda i: (0, 0)),
        ],
        out_specs=[
            pl.BlockSpec((tm, d_model), lambda i: (i, 0)),
            pl.BlockSpec((n_exp, tm), lambda i: (0, i)),
        ],
        out_shape=[
            jax.ShapeDtypeStruct((seq, d_model), BF16),
            jax.ShapeDtypeStruct((n_exp, seq), F32),
        ],
        compiler_params=_params("arbitrary"),
        name="router",
    )(x1, g.reshape(1, d_model), w_router.T)


def _topk_kernel(a_ref, posw_ref, gate_ref, ws_ref, nr_ref, *, cap, blk, win):
    a = a_ref[...]
    n_exp, seq = a.shape
    n_blk = seq // blk
    bits = pltpu.bitcast(a, jnp.int32)

    def bit_step(i, thr):
        cand = thr | jnp.left_shift(jnp.int32(1), 30 - i)
        cnt = jnp.sum((bits >= cand).astype(jnp.int32), axis=-1, keepdims=True)
        return jnp.where(cnt >= cap, cand, thr)

    thr = lax.fori_loop(0, 31, bit_step, jnp.zeros((n_exp, 1), jnp.int32))
    gt = bits > thr
    eq = bits == thr
    need = cap - jnp.sum(gt.astype(jnp.int32), axis=-1, keepdims=True)

    tri = (lax.broadcasted_iota(jnp.int32, (blk, blk), 0)
           <= lax.broadcasted_iota(jnp.int32, (blk, blk), 1)).astype(BF16)
    blk_of_tok = lax.broadcasted_iota(jnp.int32, (seq, n_blk), 0) // blk
    tok_to_blk = (blk_of_tok == lax.broadcasted_iota(jnp.int32, (seq, n_blk), 1)).astype(BF16)
    blk_before = (lax.broadcasted_iota(jnp.int32, (n_blk, n_blk), 0)
                  < lax.broadcasted_iota(jnp.int32, (n_blk, n_blk), 1)).astype(BF16)
    erow = lax.broadcasted_iota(jnp.int32, (2 * n_blk, seq), 0)
    ecol = lax.broadcasted_iota(jnp.int32, (2 * n_blk, seq), 1) // blk
    expand = jnp.where(erow == ecol, 32.0, jnp.where(erow - n_blk == ecol, 1.0, 0.0)).astype(BF16)

    def prefix_counts(mask):
        mb = jnp.where(mask, 1.0, 0.0).astype(BF16)
        local = jnp.concatenate(
            [jnp.dot(mb[:, b * blk:(b + 1) * blk], tri, preferred_element_type=F32) for b in range(n_blk)],
            axis=1)
        per_blk = jnp.dot(mb, tok_to_blk, preferred_element_type=F32)
        start = jnp.dot(per_blk.astype(BF16), blk_before, preferred_element_type=F32)
        hi = jnp.floor(start * (1.0 / 32.0))
        parts = jnp.concatenate([hi, start - 32.0 * hi], axis=1).astype(BF16)
        start_tok = jnp.dot(parts, expand, preferred_element_type=F32)
        return local + start_tok, start, start_tok, per_blk

    eq_incl, _, _, _ = prefix_counts(eq)
    sel = gt | (eq & (eq_incl - 1.0 < need.astype(F32)))
    incl, start, start_tok, per_blk = prefix_counts(sel)

    def window(s):
        return jnp.floor(s * (1.0 / MOE_WIN_ALIGN)) * MOE_WIN_ALIGN

    posw_ref[...] = jnp.where(sel, (incl - 1.0 - window(start_tok)).astype(jnp.int32), -1)
    gate_ref[...] = jnp.where(sel, a, 0.0)
    ws_ref[...] = window(start).astype(jnp.int32)
    span = start - window(start) + per_blk
    rounds = jnp.floor((span + float(win - 1)) * (1.0 / win))
    nr_ref[...] = jnp.max(rounds, axis=0, keepdims=True).astype(jnp.int32)


def _topk(aff_t, cap, blk, win):
    n_exp, seq = aff_t.shape
    n_blk = seq // blk
    full = lambda *_: (0, 0)
    return pl.pallas_call(
        functools.partial(_topk_kernel, cap=cap, blk=blk, win=win),
        grid=(1,),
        in_specs=[pl.BlockSpec((n_exp, seq), full)],
        out_specs=[pl.BlockSpec((n_exp, seq), full), pl.BlockSpec((n_exp, seq), full),
                   pl.BlockSpec((n_exp, n_blk), full), pl.BlockSpec((1, n_blk), full)],
        out_shape=[
            jax.ShapeDtypeStruct((n_exp, seq), jnp.int32),
            jax.ShapeDtypeStruct((n_exp, seq), F32),
            jax.ShapeDtypeStruct((n_exp, n_blk), jnp.int32),
            jax.ShapeDtypeStruct((1, n_blk), jnp.int32),
        ],
        compiler_params=_params("arbitrary"),
        name="topk",
    )(aff_t)


def _window(ws_ref, e, b, r, n_blk, win, cap):
    ws = ws_ref[e * n_blk + b] + r * win
    start = jnp.minimum(ws, cap - win)
    return pl.multiple_of(start, MOE_WIN_ALIGN), ws - start


def _gather_kernel(ws_ref, nr_ref, h_ref, rel_ref, xe_ref, *, blk, win, n_blk):
    n_exp, cap, _ = xe_ref.shape
    xe_ref[...] = jnp.zeros_like(xe_ref)
    slot = lax.broadcasted_iota(jnp.int32, (win, blk), 0)

    def one_round(b, r):
        rows = h_ref[pl.ds(pl.multiple_of(b * blk, blk), blk), :]
        rel = rel_ref[b]
        starts, hots = [], []
        for e in range(n_exp):
            start, shift = _window(ws_ref, e, b, r, n_blk, win, cap)
            relr = rel[e:e + 1, :] - r * win
            key = jnp.where(relr >= 0, relr + shift, -1)
            hots.append(jnp.where(slot == key, 1.0, 0.0).astype(BF16))
            starts.append(start)
        res = jnp.dot(jnp.concatenate(hots, axis=0), rows, preferred_element_type=F32)
        for e in range(n_exp):
            dst = pl.ds(starts[e], win)
            xe_ref[e, dst, :] = (xe_ref[e, dst, :].astype(F32) + res[e * win:(e + 1) * win]).astype(BF16)

    def extra_rounds(b):
        def body(r, carry):
            one_round(b, r)
            return carry
        lax.fori_loop(1, nr_ref[b], body, 0)

    def block_pair(bp, _):
        for h in range(2):
            one_round(2 * bp + h, 0)
        for h in range(2):
            extra_rounds(2 * bp + h)
        return 0

    lax.fori_loop(0, n_blk // 2, block_pair, 0)


def _moe_gather(ws_flat, n_rounds, h2, rel3, cap, win):
    seq, d_model = h2.shape
    n_blk, n_exp, blk = rel3.shape
    dq = d_model // 4
    grid_spec = pltpu.PrefetchScalarGridSpec(
        num_scalar_prefetch=2,
        grid=(4,),
        in_specs=[
            pl.BlockSpec((seq, dq), lambda c, ws, nr: (0, c)),
            pl.BlockSpec((n_blk, n_exp, blk), lambda c, ws, nr: (0, 0, 0)),
        ],
        out_specs=pl.BlockSpec((n_exp, cap, dq), lambda c, ws, nr: (0, 0, c)),
    )
    return pl.pallas_call(
        functools.partial(_gather_kernel, blk=blk, win=win, n_blk=n_blk),
        grid_spec=grid_spec,
        out_shape=jax.ShapeDtypeStruct((n_exp, cap, d_model), BF16),
        compiler_params=_params("arbitrary"),
        name="moe_gather",
    )(ws_flat, n_rounds, h2, rel3)


def _ffn_kernel(x_ref, wg_ref, wu_ref, wd_ref, y_ref, act_ref, *, n_f):
    s = pl.program_id(1)
    tf = wg_ref.shape[-1]

    @pl.when(s < n_f)
    def _():
        x = x_ref[0]
        g = jnp.dot(x, wg_ref[0].astype(BF16), preferred_element_type=F32)
        u = jnp.dot(x, wu_ref[0].astype(BF16), preferred_element_type=F32)
        act_ref[s] = (g * (1.0 / (1.0 + jnp.exp(-g))) * u).astype(BF16)

    @pl.when(s >= n_f)
    def _():
        acc = jnp.dot(act_ref[0], wd_ref[0, 0:tf, :].astype(BF16), preferred_element_type=F32)
        for f in range(1, n_f):
            acc = acc + jnp.dot(act_ref[f], wd_ref[0, f * tf:(f + 1) * tf, :].astype(BF16),
                                preferred_element_type=F32)
        y_ref[0] = acc.astype(BF16)


def _moe_ffn(xe, w_gate, w_up, w_down):
    n_exp, cap, d_model = xe.shape
    d_ff = w_gate.shape[-1]
    tf = 512
    tn = 1024
    n_f, n_n = d_ff // tf, d_model // tn
    up_tile = lambda e, s: (e, 0, jnp.minimum(s, n_f - 1))
    down_tile = lambda e, s: (e, 0, jnp.maximum(s - n_f, 0))
    return pl.pallas_call(
        functools.partial(_ffn_kernel, n_f=n_f),
        grid=(n_exp, n_f + n_n),
        in_specs=[
            pl.BlockSpec((1, cap, d_model), lambda e, s: (e, 0, 0)),
            pl.BlockSpec((1, d_model, tf), up_tile),
            pl.BlockSpec((1, d_model, tf), up_tile),
            pl.BlockSpec((1, d_ff, tn), down_tile),
        ],
        out_specs=pl.BlockSpec((1, cap, tn), down_tile),
        out_shape=jax.ShapeDtypeStruct((n_exp, cap, d_model), BF16),
        scratch_shapes=[pltpu.VMEM((n_f, cap, tf), BF16)],
        compiler_params=_params("arbitrary", "arbitrary"),
        name="moe_ffn",
    )(xe, w_gate, w_up, w_down)


def _combine_kernel(ws_ref, nr_ref, ye_ref, x_ref, rel_ref, gate_ref, o_ref, *, blk, win, n_blk):
    n_exp, cap, _ = ye_ref.shape
    sub = x_ref.shape[0] // blk
    slot = lax.broadcasted_iota(jnp.int32, (win, blk), 0)

    def one_round(b, r):
        rel = rel_ref[b]
        gate = gate_ref[b]
        gates, wins = [], []
        for e in range(n_exp):
            start, shift = _window(ws_ref, e, b, r, n_blk, win, cap)
            relr = rel[e:e + 1, :] - r * win
            key = jnp.where(relr >= 0, relr + shift, -1)
            gates.append(jnp.where(slot == key, gate[e:e + 1, :], 0.0).astype(BF16))
            wins.append(ye_ref[e, pl.ds(start, win), :])
        return lax.dot_general(jnp.concatenate(gates, axis=0), jnp.concatenate(wins, axis=0),
                               (((0,), (0,)), ((), ())), preferred_element_type=F32)

    for s in range(sub):
        tok = slice(s * blk, (s + 1) * blk)
        o_ref[tok, :] = x_ref[tok, :] + one_round(pl.program_id(1) * sub + s, 0)
    for s in range(sub):
        tok = slice(s * blk, (s + 1) * blk)
        b = pl.program_id(1) * sub + s

        def extra(r, carry, b=b, tok=tok):
            o_ref[tok, :] += one_round(b, r)
            return carry

        lax.fori_loop(1, nr_ref[b], extra, 0)


def _moe_combine(ws_flat, n_rounds, ye, x1, rel3, gate3, win):
    seq, d_model = x1.shape
    n_exp, cap, _ = ye.shape
    n_blk, _, blk = rel3.shape
    dq = d_model // 4
    tile = min(4, n_blk) * blk
    whole = lambda c, t, ws, nr: (0, 0, 0)
    grid_spec = pltpu.PrefetchScalarGridSpec(
        num_scalar_prefetch=2,
        grid=(4, seq // tile),
        in_specs=[
            pl.BlockSpec((n_exp, cap, dq), lambda c, t, ws, nr: (0, 0, c)),
            pl.BlockSpec((tile, dq), lambda c, t, ws, nr: (t, c)),
            pl.BlockSpec(rel3.shape, whole),
            pl.BlockSpec(gate3.shape, whole),
        ],
        out_specs=pl.BlockSpec((tile, dq), lambda c, t, ws, nr: (t, c)),
    )
    return pl.pallas_call(
        functools.partial(_combine_kernel, blk=blk, win=win, n_blk=n_blk),
        grid_spec=grid_spec,
        out_shape=jax.ShapeDtypeStruct((seq, d_model), F32),
        compiler_params=_params("arbitrary", "arbitrary"),
        name="moe_combine",
    )(ws_flat, n_rounds, ye, x1, rel3, gate3)


def _final_norm_kernel(x_ref, g_ref, o_ref):
    o_ref[...] = _rms(x_ref[...], g_ref[...])


def _final_norm(x, g):
    seq, d_model = x.shape
    tm = min(512, seq)
    return pl.pallas_call(
        _final_norm_kernel,
        grid=(seq // tm,),
        in_specs=[pl.BlockSpec((tm, d_model), lambda i: (i, 0)), pl.BlockSpec((1, d_model), lambda i: (0, 0))],
        out_specs=pl.BlockSpec((tm, d_model), lambda i: (i, 0)),
        out_shape=jax.ShapeDtypeStruct((seq, d_model), F32),
        compiler_params=_params("arbitrary"),
        name="final_norm",
    )(x, g.reshape(1, d_model))


def _layer(x, norm_mix_g, w_in, fwd, bwd, ssm_d, w_glu, b_glu, na_rpb, g_ssm_out, g_na_out, w_out,
           norm_ffn_g, w_router, w_gate, w_up, w_down):
    seq, d_model = x.shape
    d_ssm = ssm_d.shape[0]
    d_na = g_na_out.shape[0]
    n_exp = w_router.shape[1]
    cap = EC_CAPACITY_FACTOR * seq // n_exp
    blk = min(MOE_TOK_BLOCK, cap // 2)
    win = min(MOE_WIN, cap)

    u, qkv = _in_proj(x, norm_mix_g, w_in, d_ssm, d_na)
    y_ssm = _s5_post(_s5_scan(u, fwd, bwd), u, ssm_d, w_glu, b_glu, g_ssm_out)
    y_na = _neighbourhood_attention(qkv, na_rpb, g_na_out)
    x1 = _out_proj(y_ssm, y_na, w_out, x)

    h2, aff_t = _router(x1, norm_ffn_g, w_router)
    rel, gate, ws, n_rounds = _topk(aff_t, cap, blk, win)
    ws_flat = ws.reshape(-1)
    n_rounds = n_rounds.reshape(-1)
    rel3 = jnp.swapaxes(rel.reshape(n_exp, seq // blk, blk), 0, 1)
    gate3 = jnp.swapaxes(gate.reshape(n_exp, seq // blk, blk), 0, 1)
    xe = _moe_gather(ws_flat, n_rounds, h2, rel3, cap, win)
    ye = _moe_ffn(xe, w_gate, w_up, w_down)
    return _moe_combine(ws_flat, n_rounds, ye, x1, rel3, gate3, win)


def kernel(x, norm_mix_g, w_in, a_re_fwd, a_im_fwd, log_dt_fwd, b_re_fwd, b_im_fwd, c_re_fwd, c_im_fwd, a_re_bwd, a_im_bwd, log_dt_bwd, b_re_bwd, b_im_bwd, c_re_bwd, c_im_bwd, ssm_d, w_glu, b_glu, na_rpb, g_ssm_out, g_na_out, w_out, norm_ffn_g, w_router, w_gate, w_up, w_down, norm_final_g):
    bsz = x.shape[0]
    depth = w_in.shape[0]
    outs = []
    for b in range(bsz):
        xb = x[b]
        for l in range(depth):
            fwd = (a_re_fwd[l], a_im_fwd[l], log_dt_fwd[l], b_re_fwd[l], b_im_fwd[l], c_re_fwd[l], c_im_fwd[l])
            bwd = (a_re_bwd[l], a_im_bwd[l], log_dt_bwd[l], b_re_bwd[l], b_im_bwd[l], c_re_bwd[l], c_im_bwd[l])
            xb = _layer(xb, norm_mix_g[l], w_in[l], fwd, bwd, ssm_d[l], w_glu[l], b_glu[l], na_rpb[l],
                        g_ssm_out[l], g_na_out[l], w_out[l], norm_ffn_g[l], w_router[l],
                        w_gate[l], w_up[l], w_down[l])
        outs.append(_final_norm(xb, norm_final_g))
    return jnp.stack(outs)
```

```python
import functools

import numpy as np
import jax
import jax.numpy as jnp
from jax import lax
from jax.experimental import pallas as pl
from jax.experimental.pallas import tpu as pltpu

F32 = jnp.float32
BF16 = jnp.bfloat16

RMS_EPS = 1e-6
SSM_GROUP = 16
SSM_STATE = 64
NA_HEADS = 16
NA_HEAD_DIM = 64
GRID_W = 64
WIN_ROWS = 8
WIN_COLS = 16
N_EXPERTS = 16
EC_CAPACITY_FACTOR = 2

S5_CHUNK = 16
S5_SEGS = 8
HEADS_PER_DOT = 4
MOE_TOK_BLOCK = 256
MOE_WIN_ALIGN = 16
MOE_WIN = 64
MASK_NEG = -1e30
LOG2_E = float(np.log2(np.e))

VMEM_LIMIT_BYTES = 56 * 1024 * 1024


def _params(*semantics):
    return pltpu.CompilerParams(dimension_semantics=semantics, vmem_limit_bytes=VMEM_LIMIT_BYTES)


def _rms(x, g):
    ms = jnp.mean(x * x, axis=-1, keepdims=True)
    return x * lax.rsqrt(ms + RMS_EPS) * g


def _inproj_kernel(x_ref, g_ref, w_ref, u_ref, qkv_ref, h_scr, *, n_u, n_q, q_scale):
    j = pl.program_id(1)

    @pl.when(j == 0)
    def _():
        h_scr[...] = _rms(x_ref[...], g_ref[...]).astype(BF16)

    def project():
        return jnp.dot(h_scr[...], w_ref[...].astype(BF16), preferred_element_type=F32)

    @pl.when(j < n_u)
    def _():
        u_ref[...] = project()

    @pl.when(j >= n_u)
    def _():
        scale = jnp.where(j < n_u + n_q, q_scale, 1.0).astype(F32)
        qkv_ref[...] = (project() * scale).astype(BF16)


def _in_proj(x, g, w_in, d_ssm, d_na):
    seq, d_model = x.shape
    tm = min(1024, seq)
    tn = 1024
    n_u, n_q = d_ssm // tn, d_na // tn
    n_cols = w_in.shape[1] // tn
    kern = functools.partial(_inproj_kernel, n_u=n_u, n_q=n_q, q_scale=NA_HEAD_DIM ** -0.5 * LOG2_E)
    return pl.pallas_call(
        kern,
        grid=(seq // tm, n_cols),
        in_specs=[
            pl.BlockSpec((tm, d_model), lambda i, j: (i, 0)),
            pl.BlockSpec((1, d_model), lambda i, j: (0, 0)),
            pl.BlockSpec((d_model, tn), lambda i, j: (0, j)),
        ],
        out_specs=[
            pl.BlockSpec((tm, tn), lambda i, j: (i, jnp.minimum(j, n_u - 1))),
            pl.BlockSpec((tm, tn), lambda i, j: (i, jnp.maximum(j - n_u, 0))),
        ],
        out_shape=[
            jax.ShapeDtypeStruct((seq, d_ssm), F32),
            jax.ShapeDtypeStruct((seq, 3 * d_na), BF16),
        ],
        scratch_shapes=[pltpu.VMEM((tm, d_model), BF16)],
        compiler_params=_params("arbitrary", "arbitrary"),
        name="in_proj",
    )(x, g.reshape(1, d_model), w_in)


def _cmul(ar, ai, br, bi):
    return ar * br - ai * bi, ar * bi + ai * br


def _s5_tables(fwd, bwd, n_tiles):
    t_len = S5_CHUNK
    a_re, a_im, log_dt, b_re, b_im, c_re, c_im = (jnp.stack([f, b]).astype(F32) for f, b in zip(fwd, bwd))
    _, n_grp, n_st = a_re.shape
    n_ch = b_re.shape[-1]
    dt = jnp.exp(log_dt)[:, :, None]
    xr, xi = a_re * dt, a_im * dt
    steps = np.arange(t_len)

    def power(x_r, x_i, exps):
        e = jnp.asarray(exps, F32).reshape(exps.shape + (1,) * (x_r.ndim - 1))
        mag = jnp.exp(x_r[None] * e)
        return mag * jnp.cos(x_i[None] * e), mag * jnp.sin(x_i[None] * e)

    def per_step(exps):
        p_r, p_i = power(xr, xi, exps)
        return jnp.transpose(p_r, (1, 2, 0, 3)), jnp.transpose(p_i, (1, 2, 0, 3))

    a1_r, a1_i = power(xr, xi, np.ones((1, 2)))
    nr, ni = a1_r[0] - 1.0, a1_i[0]
    den = a_re * a_re + a_im * a_im
    qr, qi = (nr * a_re + ni * a_im) / den, (ni * a_re - nr * a_im) / den
    bb_r, bb_i = _cmul(qr[:, :, None, :], qi[:, :, None, :], jnp.swapaxes(b_re, 2, 3), jnp.swapaxes(b_im, 2, 3))
    am_r, am_i = per_step(np.stack([steps, steps[::-1]], axis=1))
    y_r, y_i = _cmul(c_re[:, :, None], c_im[:, :, None], am_r[:, :, :, None, :], am_i[:, :, :, None, :])
    y = jnp.concatenate([y_r, y_i], axis=-1).reshape(2, n_grp, t_len * n_ch, 2 * n_st)
    lag = jnp.einsum('dgck,dgqk->dgcq', jnp.concatenate([bb_r, -bb_i], axis=-1), y,
                     precision=lax.Precision.HIGH)
    st = per_step(np.stack([t_len - 1 - steps, steps], axis=1)) + (bb_r, bb_i)
    wo = per_step(np.stack([steps + 1, t_len - steps], axis=1)) + (c_re, c_im)
    xrf, xif = xr.reshape(2, -1), xi.reshape(2, -1)
    tiles = np.arange(n_tiles)
    ends = power(xrf, xif, np.array([[t_len, t_len], [t_len * n_tiles, t_len * n_tiles]]))
    ap = jnp.stack(ends, axis=2).reshape(8, -1)
    at_r, at_i = power(xrf, xif, t_len * np.stack([tiles, tiles[::-1]], axis=1))
    pw = jnp.transpose(jnp.stack([at_r, at_i], axis=2), (1, 2, 0, 3)).reshape(4, n_tiles, -1)
    return lag, st, wo, ap, pw


def _transpose_lane_blocks(groups, width):
    groups = [list(g) for g in groups]
    n = len(groups[0])
    first = groups[0][0]
    lanes = first.shape[-1]
    axis = first.ndim - 1
    block = lax.broadcasted_iota(jnp.int32, first.shape, axis) // width
    d = n // 2
    while d:
        upper = (block & d) != 0
        pairs = [(g, i) for g in groups for i in range(n) if i & d == 0]
        up = [pltpu.roll(g[i | d], d * width, axis) for g, i in pairs]
        down = [pltpu.roll(g[i], lanes - d * width, axis) for g, i in pairs]
        for (g, i), u, dn in zip(pairs, up, down):
            g[i], g[i | d] = jnp.where(upper, u, g[i]), jnp.where(upper, g[i | d], dn)
        d //= 2
    return groups


def _s5_a_kernel(u_ref, lag_ref, pe_r_ref, pe_i_ref, bb_r_ref, bb_i_ref, y_ref, s_ref, w_grp, e_pair, ucat,
                 *, n_tiles, seg_stride):
    n_grp = lag_ref.shape[1]
    cw = S5_CHUNK * SSM_GROUP
    n_plane, n_state = 4, pe_r_ref.shape[-1]
    lanes = u_ref.shape[-1]
    pc = n_plane * 2 * n_state

    @pl.when(pl.program_id(0) == 0)
    def _():
        e_pair[...] = jnp.zeros_like(e_pair)

    for g in range(n_grp):
        blank = jnp.zeros((SSM_GROUP, cw), F32)
        strip = jnp.concatenate([blank, lag_ref[0, g], lag_ref[1, g], blank], axis=1)
        for t in range(S5_CHUNK):
            r0 = t * SSM_GROUP
            fwd_lo = cw - t * SSM_GROUP
            bwd_lo = 2 * cw + (S5_CHUNK - 1 - t) * SSM_GROUP
            piece = strip[:, fwd_lo:fwd_lo + cw] + strip[:, bwd_lo:bwd_lo + cw]
            w_grp[g, r0:r0 + SSM_GROUP, :] = piece.astype(BF16)
            for d in range(2):
                planes = _cmul(pe_r_ref[d, g, t:t + 1, :], pe_i_ref[d, g, t:t + 1, :], bb_r_ref[d, g], bb_i_ref[d, g])
                for ri in range(2):
                    c0 = (2 * d + ri) * 2 * n_state + (g % 2) * n_state
                    rows = slice((g % 2) * cw + r0, (g % 2) * cw + r0 + SSM_GROUP)
                    e_pair[g // 2, rows, c0:c0 + n_state] = planes[ri].astype(BF16)

    per_trip = 4

    def gather(trip, _):
        groups, where = [], []
        for i in range(per_trip):
            jj = per_trip * trip + i
            pieces = []
            for t in range(S5_CHUNK):
                rows = [u_ref[pl.ds(S5_CHUNK * (2 * jj + h) + t, S5_SEGS, stride=seg_stride), :] for h in range(2)]
                pieces.append(jnp.concatenate(rows, axis=0))
            dst = pl.ds(pl.multiple_of(jj * 2 * S5_SEGS, 2 * S5_SEGS), 2 * S5_SEGS)
            for half in range(S5_CHUNK // n_grp):
                groups.append(pieces[half * n_grp:(half + 1) * n_grp])
                where.append((dst, half))
        for (dst, half), tiles in zip(where, _transpose_lane_blocks(groups, SSM_GROUP)):
            for g in range(n_grp):
                c0 = g * cw + half * lanes
                ucat[dst, c0:c0 + lanes] = tiles[g].astype(BF16)
        return 0

    lax.fori_loop(0, n_tiles // (2 * per_trip), gather, 0)
    ug = ucat[...]
    for g in range(n_grp):
        y_ref[:, g * cw:(g + 1) * cw] = jnp.dot(ug[:, g * cw:(g + 1) * cw], w_grp[g], preferred_element_type=F32)
    for q in range(n_grp // 2):
        s_ref[:, q * pc:(q + 1) * pc] = jnp.dot(ug[:, q * 2 * cw:(q + 1) * 2 * cw], e_pair[q],
                                                preferred_element_type=F32)


def _s5_bc_kernel(s_ref, yi_ref, po_r_ref, po_i_ref, c_r_ref, c_i_ref, ap_ref, pw_ref, y_ref, sin, ychunk, c_blk,
                  sin_bf, *, n_tiles, seg_stride):
    n_plane = 4
    _, n_grp, _, n_state = po_r_ref.shape
    cw = S5_CHUNK * SSM_GROUP
    n_pair = n_grp // 2
    pw_ = 2 * n_state
    pc = n_plane * pw_
    lanes = y_ref.shape[-1]

    @pl.when(pl.program_id(0) == 0)
    def _():
        c_blk[...] = jnp.zeros_like(c_blk)

    for g in range(n_grp):
        gl = g % 2
        for d in range(2):
            for t in range(S5_CHUNK):
                w_r, w_i = _cmul(c_r_ref[d, g], c_i_ref[d, g], po_r_ref[d, g, t:t + 1, :], po_i_ref[d, g, t:t + 1, :])
                rows = slice(gl * cw + t * SSM_GROUP, gl * cw + (t + 1) * SSM_GROUP)
                for ri, plane in enumerate((w_r, -w_i)):
                    c0 = (2 * d + ri) * pw_ + gl * n_state
                    c_blk[g // 2, rows, c0:c0 + n_state] = plane.astype(BF16)

    chains = [(q, d) for q in range(n_pair) for d in range(2)]

    def plane_lanes(q, d):
        base = q * pc + d * 2 * pw_
        return slice(base, base + pw_), slice(base + pw_, base + 2 * pw_)

    def coef(row, q):
        return ap_ref[row:row + 1, q * pw_:(q + 1) * pw_]

    def tile_rows(n, d):
        j = n if d == 0 else n_tiles - 1 - n
        return pl.ds(pl.multiple_of(j * S5_SEGS, S5_SEGS), S5_SEGS)

    def step(n, carry):
        out = []
        for (q, d), (zr, zi) in zip(chains, carry):
            re, im = plane_lanes(q, d)
            rows = tile_rows(n, d)
            sin[rows, re] = zr
            sin[rows, im] = zi
            ar, ai = coef(2 * d, q), coef(2 * d + 1, q)
            out.append((ar * zr - ai * zi + s_ref[rows, re], ar * zi + ai * zr + s_ref[rows, im]))
        return tuple(out)

    z0 = jnp.zeros((S5_SEGS, pw_), F32)
    ends = lax.fori_loop(0, n_tiles, step, tuple((z0, z0) for _ in chains))

    carries = []
    for (q, d), (zr, zi) in zip(chains, ends):
        sr, si = coef(4 + 2 * d, q), coef(5 + 2 * d, q)
        cr = jnp.zeros((1, pw_), F32)
        ci = jnp.zeros((1, pw_), F32)
        seg_r = [None] * S5_SEGS
        seg_i = [None] * S5_SEGS
        for s in (range(S5_SEGS) if d == 0 else range(S5_SEGS - 1, -1, -1)):
            seg_r[s], seg_i[s] = cr, ci
            cr, ci = (zr[s:s + 1] + sr * cr - si * ci, zi[s:s + 1] + sr * ci + si * cr)
        carries.append((jnp.concatenate(seg_r, axis=0), jnp.concatenate(seg_i, axis=0)))

    def fix(n8, _):
        tiles = pl.ds(pl.multiple_of(n8 * 8, 8), 8)
        for (q, d), (car_r, car_i) in zip(chains, carries):
            re, im = plane_lanes(q, d)
            pr8 = pw_ref[2 * d, tiles, q * pw_:(q + 1) * pw_]
            pi8 = pw_ref[2 * d + 1, tiles, q * pw_:(q + 1) * pw_]
            car2_r = jnp.concatenate([car_r, car_r], axis=0)
            car2_i = jnp.concatenate([car_i, car_i], axis=0)
            for r in range(0, 8, 2):
                rows = pl.ds(pl.multiple_of((n8 * 8 + r) * S5_SEGS, 2 * S5_SEGS), 2 * S5_SEGS)
                pr = jnp.concatenate([jnp.broadcast_to(pr8[r + h:r + h + 1], (S5_SEGS, pw_)) for h in range(2)], axis=0)
                pi = jnp.concatenate([jnp.broadcast_to(pi8[r + h:r + h + 1], (S5_SEGS, pw_)) for h in range(2)], axis=0)
                sin_bf[rows, re] = (sin[rows, re] + (pr * car2_r - pi * car2_i)).astype(BF16)
                sin_bf[rows, im] = (sin[rows, im] + (pr * car2_i + pi * car2_r)).astype(BF16)
        return 0

    lax.fori_loop(0, n_tiles // 8, fix, 0)

    for q in range(n_pair):
        lhs = sin_bf[:, q * pc:(q + 1) * pc]
        carried = lax.dot_general(lhs, c_blk[q], (((1,), (1,)), ((), ())), preferred_element_type=F32)
        ychunk[:, q * 2 * cw:(q + 1) * 2 * cw] = yi_ref[:, q * 2 * cw:(q + 1) * 2 * cw] + carried

    per_trip = 8

    def scatter(jj, _):
        groups, where = [], []
        for h in range(per_trip):
            j = per_trip * jj + h
            src = pl.ds(pl.multiple_of(j * S5_SEGS, S5_SEGS), S5_SEGS)
            for half in range(S5_CHUNK // n_grp):
                groups.append([ychunk[src, g * cw + half * lanes:g * cw + (half + 1) * lanes] for g in range(n_grp)])
                where.append(S5_CHUNK * j + half * n_grp)
        for t0, tiles in zip(where, _transpose_lane_blocks(groups, SSM_GROUP)):
            for k, tile in enumerate(tiles):
                y_ref[pl.ds(t0 + k, S5_SEGS, stride=seg_stride), :] = tile
        return 0

    lax.fori_loop(0, n_tiles // per_trip, scatter, 0)


def _s5_scan(u, fwd, bwd):
    seq, d_ssm = u.shape
    n_groups = d_ssm // SSM_GROUP
    n_state = SSM_STATE
    cw = S5_CHUNK * SSM_GROUP
    n_rows = seq // S5_CHUNK
    n_tiles = n_rows // S5_SEGS
    lanes = 128
    gpb = lanes // SSM_GROUP
    n_blocks = d_ssm // lanes
    bw = gpb * cw
    sw = gpb * 4 * n_state
    seg_stride = n_tiles * S5_CHUNK

    lag, st, wo, ap, pw = _s5_tables(fwd, bwd, n_tiles)

    def per_block(t):
        return pl.BlockSpec((t.shape[0], gpb) + t.shape[2:], lambda i: (0, i) + (0,) * (t.ndim - 2))

    y_intra, s_loc = pl.pallas_call(
        functools.partial(_s5_a_kernel, n_tiles=n_tiles, seg_stride=seg_stride),
        grid=(n_blocks,),
        in_specs=[
            pl.BlockSpec((seq, lanes), lambda i: (0, i)),
            per_block(lag), *[per_block(t) for t in st],
        ],
        out_specs=[
            pl.BlockSpec((n_rows, bw), lambda i: (0, i)),
            pl.BlockSpec((n_rows, sw), lambda i: (0, i)),
        ],
        out_shape=[
            jax.ShapeDtypeStruct((n_rows, n_blocks * bw), F32),
            jax.ShapeDtypeStruct((n_rows, n_blocks * sw), F32),
        ],
        scratch_shapes=[
            pltpu.VMEM((gpb, cw, cw), BF16),
            pltpu.VMEM((gpb // 2, 2 * cw, 8 * n_state), BF16),
            pltpu.VMEM((n_rows, S5_CHUNK * lanes), BF16),
        ],
        compiler_params=_params("arbitrary"),
        name="s5_a",
    )(u, lag, *st)

    pl_lanes = gpb * n_state
    return pl.pallas_call(
        functools.partial(_s5_bc_kernel, n_tiles=n_tiles, seg_stride=seg_stride),
        grid=(n_blocks,),
        in_specs=[
            pl.BlockSpec((n_rows, sw), lambda i: (0, i)),
            pl.BlockSpec((n_rows, bw), lambda i: (0, i)),
            *[per_block(t) for t in wo],
            pl.BlockSpec((8, pl_lanes), lambda i: (0, i)),
            pl.BlockSpec((4, n_tiles, pl_lanes), lambda i: (0, 0, i)),
        ],
        out_specs=pl.BlockSpec((seq, lanes), lambda i: (0, i)),
        out_shape=jax.ShapeDtypeStruct((seq, d_ssm), F32),
        scratch_shapes=[
            pltpu.VMEM((n_rows, sw), F32),
            pltpu.VMEM((n_rows, bw), F32),
            pltpu.VMEM((gpb // 2, 2 * cw, 8 * n_state), BF16),
            pltpu.VMEM((n_rows, sw), BF16),
        ],
        compiler_params=_params("arbitrary"),
        name="s5_bc",
    )(s_loc, y_intra, *wo, ap, pw)


def _s5_post_kernel(y_ref, u_ref, d_ref, w_ref, b_ref, g_ref, o_ref):
    y = y_ref[...] + d_ref[...] * u_ref[...]
    c0 = np.float32(np.sqrt(2.0 / np.pi))
    y = 0.5 * y * (1.0 + jnp.tanh(c0 * (y + np.float32(0.044715) * (y * y * y))))
    z = jnp.dot(y.astype(BF16), w_ref[...].astype(BF16), preferred_element_type=F32) + b_ref[...]
    o = y * (1.0 / (1.0 + jnp.exp(-z)))
    o_ref[...] = _rms(o, g_ref[...]).astype(BF16)


def _s5_post(y, u, d_skip, w_glu, b_glu, g):
    seq, d = y.shape
    tm = min(1024, seq)
    row = lambda i: (i, 0)
    fix = lambda i: (0, 0)
    return pl.pallas_call(
        _s5_post_kernel,
        grid=(seq // tm,),
        in_specs=[
            pl.BlockSpec((tm, d), row), pl.BlockSpec((tm, d), row), pl.BlockSpec((1, d), fix),
            pl.BlockSpec((d, d), fix), pl.BlockSpec((1, d), fix), pl.BlockSpec((1, d), fix),
        ],
        out_specs=pl.BlockSpec((tm, d), row),
        out_shape=jax.ShapeDtypeStruct((seq, d), BF16),
        compiler_params=_params("arbitrary"),
        name="s5_post",
    )(y, u, d_skip.reshape(1, d), w_glu, b_glu.reshape(1, d), g.reshape(1, d))


def _na_bias_table(rpb):
    n_heads = rpb.shape[0]
    n_quad = n_heads // HEADS_PER_DOT
    n_dy, n_dx = rpb.shape[1], rpb.shape[2]
    cols = np.arange(GRID_W)
    col_start = np.clip(cols - WIN_COLS // 2, 0, GRID_W - WIN_COLS)
    key_cols = np.arange(GRID_W)
    in_win = (key_cols[None, :] >= col_start[:, None]) & (key_cols[None, :] < col_start[:, None] + WIN_COLS)
    dx = key_cols[None, :] - cols[:, None] + (WIN_COLS - 1)
    pick_x = (dx[:, :, None] == np.arange(n_dx)).astype(np.float32)
    pick = np.zeros((2, GRID_W, 2 * GRID_W, n_dx), np.float32)
    for j in range(2):
        pick[j, :, j * GRID_W:(j + 1) * GRID_W] = pick_x
    dy = 2 * np.arange(WIN_ROWS)[None, :, None] + np.arange(2)[:, None, None] + np.arange(2)[None, None, :]
    pick_y = (dy[..., None] == np.arange(n_dy)).astype(np.float32)
    hp = lax.Precision.HIGHEST
    rows = jnp.einsum('spjy,hyx->spjhx', pick_y, rpb.astype(F32) * LOG2_E, precision=hp)
    rows = rows.reshape(2, WIN_ROWS, 2, n_quad, HEADS_PER_DOT, n_dx)
    b = jnp.einsum('spjqhx,jclx->sqphcl', rows, pick, precision=hp)
    keep = np.concatenate([in_win, in_win], axis=1)
    b = jnp.where(keep, b, MASK_NEG)
    return b.reshape(2, n_quad, WIN_ROWS, HEADS_PER_DOT * GRID_W, 2 * GRID_W)


def _na_kernel(q_ref, k_ref, v_ref, b_ref, g_ref, o_ref, *, rows, rows_per_step):
    n_keys = WIN_ROWS * GRID_W
    d_na = k_ref.shape[-1]
    pw = HEADS_PER_DOT * NA_HEAD_DIM
    row_head = lax.broadcasted_iota(jnp.int32, (HEADS_PER_DOT * GRID_W, pw), 0) // GRID_W
    col_head = lax.broadcasted_iota(jnp.int32, (HEADS_PER_DOT * GRID_W, pw), 1) // NA_HEAD_DIM
    diag = row_head == col_head
    out_head = lax.broadcasted_iota(jnp.int32, (GRID_W, pw), 1) // NA_HEAD_DIM
    first = pl.program_id(0) * rows_per_step
    block_start = jnp.clip(first - WIN_ROWS // 2, 0, rows - k_ref.shape[0])

    def one_row(i, _):
        r = first + i
        win_start = jnp.clip(r - WIN_ROWS // 2, 0, rows - WIN_ROWS)
        dy0 = win_start - r + (WIN_ROWS - 1)
        k = k_ref[pl.ds(win_start - block_start, WIN_ROWS)].reshape(n_keys, d_na)
        v = v_ref[pl.ds(win_start - block_start, WIN_ROWS)].reshape(n_keys, d_na)
        q_rows = pl.ds(pl.multiple_of(i * GRID_W, GRID_W), GRID_W)
        outs = []
        for p in range(d_na // pw):
            sl = slice(p * pw, (p + 1) * pw)
            q4 = q_ref[q_rows, sl]
            qbd = jnp.where(diag, jnp.concatenate([q4] * HEADS_PER_DOT, axis=0), jnp.zeros((), BF16))
            s = lax.dot_general(qbd, k[:, sl], (((1,), (1,)), ((), ())), preferred_element_type=F32)
            s = s + jnp.concatenate([b_ref[dy0 % 2, p, dy0 // 2 + j] for j in range(WIN_ROWS // 2)], axis=1)
            m = jnp.max(s, axis=-1, keepdims=True)
            e = jnp.exp2(s - m)
            l = jnp.sum(e, axis=-1, keepdims=True)
            o = jnp.dot(e.astype(BF16), v[:, sl], preferred_element_type=F32) / l
            acc = jnp.zeros((GRID_W, pw), F32)
            for h in range(HEADS_PER_DOT):
                acc = acc + jnp.where(out_head == h, o[h * GRID_W:(h + 1) * GRID_W], 0.0)
            outs.append(acc)
        y = jnp.concatenate(outs, axis=1)
        o_ref[q_rows, :] = _rms(y, g_ref[...]).astype(BF16)
        return 0

    lax.fori_loop(0, rows_per_step, one_row, 0, unroll=True)


def _neighbourhood_attention(qkv, rpb, g):
    seq = qkv.shape[0]
    d_na = qkv.shape[1] // 3
    rows = seq // GRID_W
    bias = _na_bias_table(rpb)
    qkv3 = qkv.reshape(rows, GRID_W, 3 * d_na)

    rps = 4
    key_rows = 2 * WIN_ROWS

    def block_start(b):
        return jnp.clip(b * rps - WIN_ROWS // 2, 0, rows - key_rows)

    window = (pl.Element(key_rows), pl.Element(GRID_W), pl.Element(d_na))

    return pl.pallas_call(
        functools.partial(_na_kernel, rows=rows, rows_per_step=rps),
        grid=(rows // rps,),
        in_specs=[
            pl.BlockSpec((rps * GRID_W, d_na), lambda b: (b, 0)),
            pl.BlockSpec(window, lambda b: (block_start(b), 0, d_na)),
            pl.BlockSpec(window, lambda b: (block_start(b), 0, 2 * d_na)),
            pl.BlockSpec(bias.shape, lambda b: (0, 0, 0, 0, 0)),
            pl.BlockSpec((1, d_na), lambda b: (0, 0)),
        ],
        out_specs=pl.BlockSpec((rps * GRID_W, d_na), lambda b: (b, 0)),
        out_shape=jax.ShapeDtypeStruct((seq, d_na), BF16),
        compiler_params=_params("arbitrary"),
        name="na",
    )(qkv, qkv3, qkv3, bias, g.reshape(1, d_na))


def _outproj_kernel(a_ref, b_ref, w_ref, x_ref, o_ref):
    da = a_ref.shape[-1]
    acc = jnp.dot(a_ref[...], w_ref[:da, :].astype(BF16), preferred_element_type=F32)
    acc = acc + jnp.dot(b_ref[...], w_ref[da:, :].astype(BF16), preferred_element_type=F32)
    o_ref[...] = x_ref[...] + acc


def _out_proj(y_ssm, y_na, w_out, x):
    seq, d_model = x.shape
    da, db = y_ssm.shape[1], y_na.shape[1]
    tm = min(2048, seq)
    tn = 512
    return pl.pallas_call(
        _outproj_kernel,
        grid=(seq // tm, d_model // tn),
        in_specs=[
            pl.BlockSpec((tm, da), lambda i, j: (i, 0)),
            pl.BlockSpec((tm, db), lambda i, j: (i, 0)),
            pl.BlockSpec((da + db, tn), lambda i, j: (0, j)),
            pl.BlockSpec((tm, tn), lambda i, j: (i, j)),
        ],
        out_specs=pl.BlockSpec((tm, tn), lambda i, j: (i, j)),
        out_shape=jax.ShapeDtypeStruct((seq, d_model), F32),
        compiler_params=_params("arbitrary", "arbitrary"),
        name="out_proj",
    )(y_ssm, y_na, w_out, x)


def _router_kernel(x_ref, g_ref, wt_ref, h_ref, a_ref):
    h = _rms(x_ref[...], g_ref[...])
    h_hi = h.astype(BF16)
    h_ref[...] = h_hi
    h_lo = (h - h_hi.astype(F32)).astype(BF16)
    w = wt_ref[...]
    w_hi = w.astype(BF16)
    w_lo = (w - w_hi.astype(F32)).astype(BF16)
    n_exp = w.shape[0]
    nt = (((1,), (1,)), ((), ()))
    both = lax.dot_general(jnp.concatenate([w_hi, w_lo], axis=0), h_hi, nt, preferred_element_type=F32)
    logits = both[:n_exp] + both[n_exp:] + lax.dot_general(w_hi, h_lo, nt, preferred_element_type=F32)
    m = jnp.max(logits, axis=0, keepdims=True)
    e = jnp.exp(logits - m)
    a_ref[...] = e / jnp.sum(e, axis=0, keepdims=True)


def _router(x1, g, w_router):
    seq, d_model = x1.shape
    n_exp = w_router.shape[1]
    tm = min(1024, seq)
    return pl.pallas_call(
        _router_kernel,
        grid=(seq // tm,),
        in_specs=[
            pl.BlockSpec((tm, d_model), lambda i: (i, 0)),
            pl.BlockSpec((1, d_model), lambda i: (0, 0)),
            pl.BlockSpec((n_exp, d_model), lambda i: (0, 0)),
        ],
        out_specs=[
            pl.BlockSpec((tm, d_model), lambda i: (i, 0)),
            pl.BlockSpec((n_exp, tm), lambda i: (0, i)),
        ],
        out_shape=[
            jax.ShapeDtypeStruct((seq, d_model), BF16),
            jax.ShapeDtypeStruct((n_exp, seq), F32),
        ],
        compiler_params=_params("arbitrary"),
        name="router",
    )(x1, g.reshape(1, d_model), w_router.T)


def _topk_kernel(a_ref, posw_ref, gate_ref, ws_ref, nr_ref, *, cap, blk, win):
    a = a_ref[...]
    n_exp, seq = a.shape
    n_blk = seq // blk
    bits = pltpu.bitcast(a, jnp.int32)

    def bit_step(i, thr):
        cand = thr | jnp.left_shift(jnp.int32(1), 30 - i)
        cnt = jnp.sum((bits >= cand).astype(jnp.int32), axis=-1, keepdims=True)
        return jnp.where(cnt >= cap, cand, thr)

    thr = lax.fori_loop(0, 31, bit_step, jnp.zeros((n_exp, 1), jnp.int32))
    gt = bits > thr
    eq = bits == thr
    need = cap - jnp.sum(gt.astype(jnp.int32), axis=-1, keepdims=True)

    tri = (lax.broadcasted_iota(jnp.int32, (blk, blk), 0)
           <= lax.broadcasted_iota(jnp.int32, (blk, blk), 1)).astype(BF16)
    blk_of_tok = lax.broadcasted_iota(jnp.int32, (seq, n_blk), 0) // blk
    tok_to_blk = (blk_of_tok == lax.broadcasted_iota(jnp.int32, (seq, n_blk), 1)).astype(BF16)
    blk_before = (lax.broadcasted_iota(jnp.int32, (n_blk, n_blk), 0)
                  < lax.broadcasted_iota(jnp.int32, (n_blk, n_blk), 1)).astype(BF16)
    erow = lax.broadcasted_iota(jnp.int32, (2 * n_blk, seq), 0)
    ecol = lax.broadcasted_iota(jnp.int32, (2 * n_blk, seq), 1) // blk
    expand = jnp.where(erow == ecol, 32.0, jnp.where(erow - n_blk == ecol, 1.0, 0.0)).astype(BF16)

    def prefix_counts(mask):
        mb = jnp.where(mask, 1.0, 0.0).astype(BF16)
        local = jnp.concatenate(
            [jnp.dot(mb[:, b * blk:(b + 1) * blk], tri, preferred_element_type=F32) for b in range(n_blk)],
            axis=1)
        per_blk = jnp.dot(mb, tok_to_blk, preferred_element_type=F32)
        start = jnp.dot(per_blk.astype(BF16), blk_before, preferred_element_type=F32)
        hi = jnp.floor(start * (1.0 / 32.0))
        parts = jnp.concatenate([hi, start - 32.0 * hi], axis=1).astype(BF16)
        start_tok = jnp.dot(parts, expand, preferred_element_type=F32)
        return local + start_tok, start, start_tok, per_blk

    eq_incl, _, _, _ = prefix_counts(eq)
    sel = gt | (eq & (eq_incl - 1.0 < need.astype(F32)))
    incl, start, start_tok, per_blk = prefix_counts(sel)

    def window(s):
        return jnp.floor(s * (1.0 / MOE_WIN_ALIGN)) * MOE_WIN_ALIGN

    posw_ref[...] = jnp.where(sel, (incl - 1.0 - window(start_tok)).astype(jnp.int32), -1)
    gate_ref[...] = jnp.where(sel, a, 0.0)
    ws_ref[...] = window(start).astype(jnp.int32)
    span = start - window(start) + per_blk
    rounds = jnp.floor((span + float(win - 1)) * (1.0 / win))
    nr_ref[...] = jnp.max(rounds, axis=0, keepdims=True).astype(jnp.int32)


def _topk(aff_t, cap, blk, win):
    n_exp, seq = aff_t.shape
    n_blk = seq // blk
    full = lambda *_: (0, 0)
    return pl.pallas_call(
        functools.partial(_topk_kernel, cap=cap, blk=blk, win=win),
        grid=(1,),
        in_specs=[pl.BlockSpec((n_exp, seq), full)],
        out_specs=[pl.BlockSpec((n_exp, seq), full), pl.BlockSpec((n_exp, seq), full),
                   pl.BlockSpec((n_exp, n_blk), full), pl.BlockSpec((1, n_blk), full)],
        out_shape=[
            jax.ShapeDtypeStruct((n_exp, seq), jnp.int32),
            jax.ShapeDtypeStruct((n_exp, seq), F32),
            jax.ShapeDtypeStruct((n_exp, n_blk), jnp.int32),
            jax.ShapeDtypeStruct((1, n_blk), jnp.int32),
        ],
        compiler_params=_params("arbitrary"),
        name="topk",
    )(aff_t)


def _window(ws_ref, e, b, r, n_blk, win, cap):
    ws = ws_ref[e * n_blk + b] + r * win
    start = jnp.minimum(ws, cap - win)
    return pl.multiple_of(start, MOE_WIN_ALIGN), ws - start


def _gather_kernel(ws_ref, nr_ref, h_ref, rel_ref, xe_ref, *, blk, win, n_blk):
    n_exp, cap, _ = xe_ref.shape
    xe_ref[...] = jnp.zeros_like(xe_ref)
    slot = lax.broadcasted_iota(jnp.int32, (win, blk), 0)

    def one_round(b, r):
        rows = h_ref[pl.ds(pl.multiple_of(b * blk, blk), blk), :]
        rel = rel_ref[b]
        starts, hots = [], []
        for e in range(n_exp):
            start, shift = _window(ws_ref, e, b, r, n_blk, win, cap)
            relr = rel[e:e + 1, :] - r * win
            key = jnp.where(relr >= 0, relr + shift, -1)
            hots.append(jnp.where(slot == key, 1.0, 0.0).astype(BF16))
            starts.append(start)
        res = jnp.dot(jnp.concatenate(hots, axis=0), rows, preferred_element_type=F32)
        for e in range(n_exp):
            dst = pl.ds(starts[e], win)
            xe_ref[e, dst, :] = (xe_ref[e, dst, :].astype(F32) + res[e * win:(e + 1) * win]).astype(BF16)

    def extra_rounds(b):
        def body(r, carry):
            one_round(b, r)
            return carry
        lax.fori_loop(1, nr_ref[b], body, 0)

    def block_pair(bp, _):
        for h in range(2):
            one_round(2 * bp + h, 0)
        for h in range(2):
            extra_rounds(2 * bp + h)
        return 0

    lax.fori_loop(0, n_blk // 2, block_pair, 0)


def _moe_gather(ws_flat, n_rounds, h2, rel3, cap, win):
    seq, d_model = h2.shape
    n_blk, n_exp, blk = rel3.shape
    dq = d_model // 4
    grid_spec = pltpu.PrefetchScalarGridSpec(
        num_scalar_prefetch=2,
        grid=(4,),
        in_specs=[
            pl.BlockSpec((seq, dq), lambda c, ws, nr: (0, c)),
            pl.BlockSpec((n_blk, n_exp, blk), lambda c, ws, nr: (0, 0, 0)),
        ],
        out_specs=pl.BlockSpec((n_exp, cap, dq), lambda c, ws, nr: (0, 0, c)),
    )
    return pl.pallas_call(
        functools.partial(_gather_kernel, blk=blk, win=win, n_blk=n_blk),
        grid_spec=grid_spec,
        out_shape=jax.ShapeDtypeStruct((n_exp, cap, d_model), BF16),
        compiler_params=_params("arbitrary"),
        name="moe_gather",
    )(ws_flat, n_rounds, h2, rel3)


def _ffn_kernel(x_ref, wg_ref, wu_ref, wd_ref, y_ref, act_ref, *, n_f):
    s = pl.program_id(1)
    tf = wg_ref.shape[-1]

    @pl.when(s < n_f)
    def _():
        x = x_ref[0]
        g = jnp.dot(x, wg_ref[0].astype(BF16), preferred_element_type=F32)
        u = jnp.dot(x, wu_ref[0].astype(BF16), preferred_element_type=F32)
        act_ref[s] = (g * (1.0 / (1.0 + jnp.exp(-g))) * u).astype(BF16)

    @pl.when(s >= n_f)
    def _():
        acc = jnp.dot(act_ref[0], wd_ref[0, 0:tf, :].astype(BF16), preferred_element_type=F32)
        for f in range(1, n_f):
            acc = acc + jnp.dot(act_ref[f], wd_ref[0, f * tf:(f + 1) * tf, :].astype(BF16),
                                preferred_element_type=F32)
        y_ref[0] = acc.astype(BF16)


def _moe_ffn(xe, w_gate, w_up, w_down):
    n_exp, cap, d_model = xe.shape
    d_ff = w_gate.shape[-1]
    tf = 512
    tn = 1024
    n_f, n_n = d_ff // tf, d_model // tn
    up_tile = lambda e, s: (e, 0, jnp.minimum(s, n_f - 1))
    down_tile = lambda e, s: (e, 0, jnp.maximum(s - n_f, 0))
    return pl.pallas_call(
        functools.partial(_ffn_kernel, n_f=n_f),
        grid=(n_exp, n_f + n_n),
        in_specs=[
            pl.BlockSpec((1, cap, d_model), lambda e, s: (e, 0, 0)),
            pl.BlockSpec((1, d_model, tf), up_tile),
            pl.BlockSpec((1, d_model, tf), up_tile),
            pl.BlockSpec((1, d_ff, tn), down_tile),
        ],
        out_specs=pl.BlockSpec((1, cap, tn), down_tile),
        out_shape=jax.ShapeDtypeStruct((n_exp, cap, d_model), BF16),
        scratch_shapes=[pltpu.VMEM((n_f, cap, tf), BF16)],
        compiler_params=_params("arbitrary", "arbitrary"),
        name="moe_ffn",
    )(xe, w_gate, w_up, w_down)


def _combine_kernel(ws_ref, nr_ref, ye_ref, x_ref, rel_ref, gate_ref, o_ref, *, blk, win, n_blk):
    n_exp, cap, _ = ye_ref.shape
    sub = x_ref.shape[0] // blk
    slot = lax.broadcasted_iota(jnp.int32, (win, blk), 0)

    def one_round(b, r):
        rel = rel_ref[b]
        gate = gate_ref[b]
        gates, wins = [], []
        for e in range(n_exp):
            start, shift = _window(ws_ref, e, b, r, n_blk, win, cap)
            relr = rel[e:e + 1, :] - r * win
            key = jnp.where(relr >= 0, relr + shift, -1)
            gates.append(jnp.where(slot == key, gate[e:e + 1, :], 0.0).astype(BF16))
            wins.append(ye_ref[e, pl.ds(start, win), :])
        return lax.dot_general(jnp.concatenate(gates, axis=0), jnp.concatenate(wins, axis=0),
                               (((0,), (0,)), ((), ())), preferred_element_type=F32)

    for s in range(sub):
        tok = slice(s * blk, (s + 1) * blk)
        o_ref[tok, :] = x_ref[tok, :] + one_round(pl.program_id(1) * sub + s, 0)
    for s in range(sub):
        tok = slice(s * blk, (s + 1) * blk)
        b = pl.program_id(1) * sub + s

        def extra(r, carry, b=b, tok=tok):
            o_ref[tok, :] += one_round(b, r)
            return carry

        lax.fori_loop(1, nr_ref[b], extra, 0)


def _moe_combine(ws_flat, n_rounds, ye, x1, rel3, gate3, win):
    seq, d_model = x1.shape
    n_exp, cap, _ = ye.shape
    n_blk, _, blk = rel3.shape
    dq = d_model // 4
    tile = min(4, n_blk) * blk
    whole = lambda c, t, ws, nr: (0, 0, 0)
    grid_spec = pltpu.PrefetchScalarGridSpec(
        num_scalar_prefetch=2,
        grid=(4, seq // tile),
        in_specs=[
            pl.BlockSpec((n_exp, cap, dq), lambda c, t, ws, nr: (0, 0, c)),
            pl.BlockSpec((tile, dq), lambda c, t, ws, nr: (t, c)),
            pl.BlockSpec(rel3.shape, whole),
            pl.BlockSpec(gate3.shape, whole),
        ],
        out_specs=pl.BlockSpec((tile, dq), lambda c, t, ws, nr: (t, c)),
    )
    return pl.pallas_call(
        functools.partial(_combine_kernel, blk=blk, win=win, n_blk=n_blk),
        grid_spec=grid_spec,
        out_shape=jax.ShapeDtypeStruct((seq, d_model), F32),
        compiler_params=_params("arbitrary", "arbitrary"),
        name="moe_combine",
    )(ws_flat, n_rounds, ye, x1, rel3, gate3)


def _final_norm_kernel(x_ref, g_ref, o_ref):
    o_ref[...] = _rms(x_ref[...], g_ref[...])


def _final_norm(x, g):
    seq, d_model = x.shape
    tm = min(1024, seq)
    return pl.pallas_call(
        _final_norm_kernel,
        grid=(seq // tm,),
        in_specs=[pl.BlockSpec((tm, d_model), lambda i: (i, 0)), pl.BlockSpec((1, d_model), lambda i: (0, 0))],
        out_specs=pl.BlockSpec((tm, d_model), lambda i: (i, 0)),
        out_shape=jax.ShapeDtypeStruct((seq, d_model), F32),
        compiler_params=_params("arbitrary"),
        name="final_norm",
    )(x, g.reshape(1, d_model))


def _layer(x, norm_mix_g, w_in, fwd, bwd, ssm_d, w_glu, b_glu, na_rpb, g_ssm_out, g_na_out, w_out,
           norm_ffn_g, w_router, w_gate, w_up, w_down):
    seq, d_model = x.shape
    d_ssm = ssm_d.shape[0]
    d_na = g_na_out.shape[0]
    n_exp = w_router.shape[1]
    cap = EC_CAPACITY_FACTOR * seq // n_exp
    blk = min(MOE_TOK_BLOCK, cap // 2)
    win = min(MOE_WIN, cap)

    u, qkv = _in_proj(x, norm_mix_g, w_in, d_ssm, d_na)
    y_ssm = _s5_post(_s5_scan(u, fwd, bwd), u, ssm_d, w_glu, b_glu, g_ssm_out)
    y_na = _neighbourhood_attention(qkv, na_rpb, g_na_out)
    x1 = _out_proj(y_ssm, y_na, w_out, x)

    h2, aff_t = _router(x1, norm_ffn_g, w_router)
    rel, gate, ws, n_rounds = _topk(aff_t, cap, blk, win)
    ws_flat = ws.reshape(-1)
    n_rounds = n_rounds.reshape(-1)
    rel3 = jnp.swapaxes(rel.reshape(n_exp, seq // blk, blk), 0, 1)
    gate3 = jnp.swapaxes(gate.reshape(n_exp, seq // blk, blk), 0, 1)
    xe = _moe_gather(ws_flat, n_rounds, h2, rel3, cap, win)
    ye = _moe_ffn(xe, w_gate, w_up, w_down)
    return _moe_combine(ws_flat, n_rounds, ye, x1, rel3, gate3, win)


def kernel(x, norm_mix_g, w_in, a_re_fwd, a_im_fwd, log_dt_fwd, b_re_fwd, b_im_fwd, c_re_fwd, c_im_fwd, a_re_bwd, a_im_bwd, log_dt_bwd, b_re_bwd, b_im_bwd, c_re_bwd, c_im_bwd, ssm_d, w_glu, b_glu, na_rpb, g_ssm_out, g_na_out, w_out, norm_ffn_g, w_router, w_gate, w_up, w_down, norm_final_g):
    bsz = x.shape[0]
    depth = w_in.shape[0]
    outs = []
    for b in range(bsz):
        xb = x[b]
        for l in range(depth):
            fwd = (a_re_fwd[l], a_im_fwd[l], log_dt_fwd[l], b_re_fwd[l], b_im_fwd[l], c_re_fwd[l], c_im_fwd[l])
            bwd = (a_re_bwd[l], a_im_bwd[l], log_dt_bwd[l], b_re_bwd[l], b_im_bwd[l], c_re_bwd[l], c_im_bwd[l])
            xb = _layer(xb, norm_mix_g[l], w_in[l], fwd, bwd, ssm_d[l], w_glu[l], b_glu[l], na_rpb[l],
                        g_ssm_out[l], g_na_out[l], w_out[l], norm_ffn_g[l], w_router[l],
                        w_gate[l], w_up[l], w_down[l])
        outs.append(_final_norm(xb, norm_final_g))
    return jnp.stack(outs)
```

```python
import functools

import numpy as np
import jax
import jax.numpy as jnp
from jax import lax
from jax.experimental import pallas as pl
from jax.experimental.pallas import tpu as pltpu

F32 = jnp.float32
BF16 = jnp.bfloat16

RMS_EPS = 1e-6
SSM_GROUP = 16
SSM_STATE = 64
NA_HEADS = 16
NA_HEAD_DIM = 64
GRID_W = 64
WIN_ROWS = 8
WIN_COLS = 16
N_EXPERTS = 16
EC_CAPACITY_FACTOR = 2

S5_CHUNK = 16
S5_SEGS = 8
HEADS_PER_DOT = 4
MOE_TOK_BLOCK = 256
MOE_WIN_ALIGN = 16
MOE_WIN = 64
MASK_NEG = -1e30
LOG2_E = float(np.log2(np.e))

VMEM_LIMIT_BYTES = 56 * 1024 * 1024


def _params(*semantics):
    return pltpu.CompilerParams(dimension_semantics=semantics, vmem_limit_bytes=VMEM_LIMIT_BYTES)


def _rms(x, g):
    ms = jnp.mean(x * x, axis=-1, keepdims=True)
    return x * lax.rsqrt(ms + RMS_EPS) * g


def _inproj_kernel(x_ref, g_ref, w_ref, u_ref, qkv_ref, h_scr, *, n_u, n_q, q_scale):
    j = pl.program_id(1)

    @pl.when(j == 0)
    def _():
        h_scr[...] = _rms(x_ref[...], g_ref[...]).astype(BF16)

    def project():
        return jnp.dot(h_scr[...], w_ref[...].astype(BF16), preferred_element_type=F32)

    @pl.when(j < n_u)
    def _():
        u_ref[...] = project()

    @pl.when(j >= n_u)
    def _():
        scale = jnp.where(j < n_u + n_q, q_scale, 1.0).astype(F32)
        qkv_ref[...] = (project() * scale).astype(BF16)


def _in_proj(x, g, w_in, d_ssm, d_na):
    seq, d_model = x.shape
    tm = min(1024, seq)
    tn = 1024
    n_u, n_q = d_ssm // tn, d_na // tn
    n_cols = w_in.shape[1] // tn
    kern = functools.partial(_inproj_kernel, n_u=n_u, n_q=n_q, q_scale=NA_HEAD_DIM ** -0.5 * LOG2_E)
    return pl.pallas_call(
        kern,
        grid=(seq // tm, n_cols),
        in_specs=[
            pl.BlockSpec((tm, d_model), lambda i, j: (i, 0)),
            pl.BlockSpec((1, d_model), lambda i, j: (0, 0)),
            pl.BlockSpec((d_model, tn), lambda i, j: (0, j)),
        ],
        out_specs=[
            pl.BlockSpec((tm, tn), lambda i, j: (i, jnp.minimum(j, n_u - 1))),
            pl.BlockSpec((tm, tn), lambda i, j: (i, jnp.maximum(j - n_u, 0))),
        ],
        out_shape=[
            jax.ShapeDtypeStruct((seq, d_ssm), F32),
            jax.ShapeDtypeStruct((seq, 3 * d_na), BF16),
        ],
        scratch_shapes=[pltpu.VMEM((tm, d_model), BF16)],
        compiler_params=_params("arbitrary", "arbitrary"),
        name="in_proj",
    )(x, g.reshape(1, d_model), w_in)


def _cmul(ar, ai, br, bi):
    return ar * br - ai * bi, ar * bi + ai * br


def _s5_tables(fwd, bwd, n_tiles):
    t_len = S5_CHUNK
    a_re, a_im, log_dt, b_re, b_im, c_re, c_im = (jnp.stack([f, b]).astype(F32) for f, b in zip(fwd, bwd))
    _, n_grp, n_st = a_re.shape
    n_ch = b_re.shape[-1]
    dt = jnp.exp(log_dt)[:, :, None]
    xr, xi = a_re * dt, a_im * dt
    steps = np.arange(t_len)

    def power(x_r, x_i, exps):
        e = jnp.asarray(exps, F32).reshape(exps.shape + (1,) * (x_r.ndim - 1))
        mag = jnp.exp(x_r[None] * e)
        return mag * jnp.cos(x_i[None] * e), mag * jnp.sin(x_i[None] * e)

    def per_step(exps):
        p_r, p_i = power(xr, xi, exps)
        return jnp.transpose(p_r, (1, 2, 0, 3)), jnp.transpose(p_i, (1, 2, 0, 3))

    a1_r, a1_i = power(xr, xi, np.ones((1, 2)))
    nr, ni = a1_r[0] - 1.0, a1_i[0]
    den = a_re * a_re + a_im * a_im
    qr, qi = (nr * a_re + ni * a_im) / den, (ni * a_re - nr * a_im) / den
    bb_r, bb_i = _cmul(qr[:, :, None, :], qi[:, :, None, :], jnp.swapaxes(b_re, 2, 3), jnp.swapaxes(b_im, 2, 3))
    am_r, am_i = per_step(np.stack([steps, steps[::-1]], axis=1))
    y_r, y_i = _cmul(c_re[:, :, None], c_im[:, :, None], am_r[:, :, :, None, :], am_i[:, :, :, None, :])
    y = jnp.concatenate([y_r, y_i], axis=-1).reshape(2, n_grp, t_len * n_ch, 2 * n_st)
    lag = jnp.einsum('dgck,dgqk->dgcq', jnp.concatenate([bb_r, -bb_i], axis=-1), y,
                     precision=lax.Precision.HIGH)
    st = per_step(np.stack([t_len - 1 - steps, steps], axis=1)) + (bb_r, bb_i)
    wo = per_step(np.stack([steps + 1, t_len - steps], axis=1)) + (c_re, c_im)
    xrf, xif = xr.reshape(2, -1), xi.reshape(2, -1)
    tiles = np.arange(n_tiles)
    ends = power(xrf, xif, np.array([[t_len, t_len], [t_len * n_tiles, t_len * n_tiles]]))
    ap = jnp.stack(ends, axis=2).reshape(8, -1)
    at_r, at_i = power(xrf, xif, t_len * np.stack([tiles, tiles[::-1]], axis=1))
    pw = jnp.transpose(jnp.stack([at_r, at_i], axis=2), (1, 2, 0, 3)).reshape(4, n_tiles, -1)
    return lag, st, wo, ap, pw


def _transpose_lane_blocks(groups, width):
    groups = [list(g) for g in groups]
    n = len(groups[0])
    first = groups[0][0]
    lanes = first.shape[-1]
    axis = first.ndim - 1
    block = lax.broadcasted_iota(jnp.int32, first.shape, axis) // width
    d = n // 2
    while d:
        upper = (block & d) != 0
        pairs = [(g, i) for g in groups for i in range(n) if i & d == 0]
        up = [pltpu.roll(g[i | d], d * width, axis) for g, i in pairs]
        down = [pltpu.roll(g[i], lanes - d * width, axis) for g, i in pairs]
        for (g, i), u, dn in zip(pairs, up, down):
            g[i], g[i | d] = jnp.where(upper, u, g[i]), jnp.where(upper, g[i | d], dn)
        d //= 2
    return groups


def _s5_a_kernel(u_ref, lag_ref, pe_r_ref, pe_i_ref, bb_r_ref, bb_i_ref, y_ref, s_ref, w_grp, e_pair, ucat,
                 *, n_tiles, seg_stride):
    n_grp = lag_ref.shape[1]
    cw = S5_CHUNK * SSM_GROUP
    n_plane, n_state = 4, pe_r_ref.shape[-1]
    lanes = u_ref.shape[-1]
    pc = n_plane * 2 * n_state

    @pl.when(pl.program_id(0) == 0)
    def _():
        e_pair[...] = jnp.zeros_like(e_pair)

    for g in range(n_grp):
        blank = jnp.zeros((SSM_GROUP, cw), F32)
        strip = jnp.concatenate([blank, lag_ref[0, g], lag_ref[1, g], blank], axis=1)
        for t in range(S5_CHUNK):
            r0 = t * SSM_GROUP
            fwd_lo = cw - t * SSM_GROUP
            bwd_lo = 2 * cw + (S5_CHUNK - 1 - t) * SSM_GROUP
            piece = strip[:, fwd_lo:fwd_lo + cw] + strip[:, bwd_lo:bwd_lo + cw]
            w_grp[g, r0:r0 + SSM_GROUP, :] = piece.astype(BF16)
            for d in range(2):
                planes = _cmul(pe_r_ref[d, g, t:t + 1, :], pe_i_ref[d, g, t:t + 1, :], bb_r_ref[d, g], bb_i_ref[d, g])
                for ri in range(2):
                    c0 = (2 * d + ri) * 2 * n_state + (g % 2) * n_state
                    rows = slice((g % 2) * cw + r0, (g % 2) * cw + r0 + SSM_GROUP)
                    e_pair[g // 2, rows, c0:c0 + n_state] = planes[ri].astype(BF16)

    per_trip = 4

    def gather(trip, _):
        groups, where = [], []
        for i in range(per_trip):
            jj = per_trip * trip + i
            pieces = []
            for t in range(S5_CHUNK):
                rows = [u_ref[pl.ds(S5_CHUNK * (2 * jj + h) + t, S5_SEGS, stride=seg_stride), :] for h in range(2)]
                pieces.append(jnp.concatenate(rows, axis=0))
            dst = pl.ds(pl.multiple_of(jj * 2 * S5_SEGS, 2 * S5_SEGS), 2 * S5_SEGS)
            for half in range(S5_CHUNK // n_grp):
                groups.append(pieces[half * n_grp:(half + 1) * n_grp])
                where.append((dst, half))
        for (dst, half), tiles in zip(where, _transpose_lane_blocks(groups, SSM_GROUP)):
            for g in range(n_grp):
                c0 = g * cw + half * lanes
                ucat[dst, c0:c0 + lanes] = tiles[g].astype(BF16)
        return 0

    lax.fori_loop(0, n_tiles // (2 * per_trip), gather, 0)
    ug = ucat[...]
    for g in range(n_grp):
        y_ref[:, g * cw:(g + 1) * cw] = jnp.dot(ug[:, g * cw:(g + 1) * cw], w_grp[g], preferred_element_type=F32)
    for q in range(n_grp // 2):
        s_ref[:, q * pc:(q + 1) * pc] = jnp.dot(ug[:, q * 2 * cw:(q + 1) * 2 * cw], e_pair[q],
                                                preferred_element_type=F32)


def _s5_bc_kernel(s_ref, yi_ref, po_r_ref, po_i_ref, c_r_ref, c_i_ref, ap_ref, pw_ref, y_ref, sin, ychunk, c_blk,
                  sin_bf, *, n_tiles, seg_stride):
    n_plane = 4
    _, n_grp, _, n_state = po_r_ref.shape
    cw = S5_CHUNK * SSM_GROUP
    n_pair = n_grp // 2
    pw_ = 2 * n_state
    pc = n_plane * pw_
    lanes = y_ref.shape[-1]

    @pl.when(pl.program_id(0) == 0)
    def _():
        c_blk[...] = jnp.zeros_like(c_blk)

    for g in range(n_grp):
        gl = g % 2
        for d in range(2):
            for t in range(S5_CHUNK):
                w_r, w_i = _cmul(c_r_ref[d, g], c_i_ref[d, g], po_r_ref[d, g, t:t + 1, :], po_i_ref[d, g, t:t + 1, :])
                rows = slice(gl * cw + t * SSM_GROUP, gl * cw + (t + 1) * SSM_GROUP)
                for ri, plane in enumerate((w_r, -w_i)):
                    c0 = (2 * d + ri) * pw_ + gl * n_state
                    c_blk[g // 2, rows, c0:c0 + n_state] = plane.astype(BF16)

    chains = [(q, d) for q in range(n_pair) for d in range(2)]

    def plane_lanes(q, d):
        base = q * pc + d * 2 * pw_
        return slice(base, base + pw_), slice(base + pw_, base + 2 * pw_)

    def coef(row, q):
        return ap_ref[row:row + 1, q * pw_:(q + 1) * pw_]

    def tile_rows(n, d):
        j = n if d == 0 else n_tiles - 1 - n
        return pl.ds(pl.multiple_of(j * S5_SEGS, S5_SEGS), S5_SEGS)

    def step(n, carry):
        out = []
        for (q, d), (zr, zi) in zip(chains, carry):
            re, im = plane_lanes(q, d)
            rows = tile_rows(n, d)
            sin[rows, re] = zr
            sin[rows, im] = zi
            ar, ai = coef(2 * d, q), coef(2 * d + 1, q)
            out.append((ar * zr - ai * zi + s_ref[rows, re], ar * zi + ai * zr + s_ref[rows, im]))
        return tuple(out)

    z0 = jnp.zeros((S5_SEGS, pw_), F32)
    ends = lax.fori_loop(0, n_tiles, step, tuple((z0, z0) for _ in chains))

    carries = []
    for (q, d), (zr, zi) in zip(chains, ends):
        sr, si = coef(4 + 2 * d, q), coef(5 + 2 * d, q)
        cr = jnp.zeros((1, pw_), F32)
        ci = jnp.zeros((1, pw_), F32)
        seg_r = [None] * S5_SEGS
        seg_i = [None] * S5_SEGS
        for s in (range(S5_SEGS) if d == 0 else range(S5_SEGS - 1, -1, -1)):
            seg_r[s], seg_i[s] = cr, ci
            cr, ci = (zr[s:s + 1] + sr * cr - si * ci, zi[s:s + 1] + sr * ci + si * cr)
        carries.append((jnp.concatenate(seg_r, axis=0), jnp.concatenate(seg_i, axis=0)))

    def fix(n8, _):
        tiles = pl.ds(pl.multiple_of(n8 * 8, 8), 8)
        for (q, d), (car_r, car_i) in zip(chains, carries):
            re, im = plane_lanes(q, d)
            pr8 = pw_ref[2 * d, tiles, q * pw_:(q + 1) * pw_]
            pi8 = pw_ref[2 * d + 1, tiles, q * pw_:(q + 1) * pw_]
            car2_r = jnp.concatenate([car_r, car_r], axis=0)
            car2_i = jnp.concatenate([car_i, car_i], axis=0)
            for r in range(0, 8, 2):
                rows = pl.ds(pl.multiple_of((n8 * 8 + r) * S5_SEGS, 2 * S5_SEGS), 2 * S5_SEGS)
                pr = jnp.concatenate([jnp.broadcast_to(pr8[r + h:r + h + 1], (S5_SEGS, pw_)) for h in range(2)], axis=0)
                pi = jnp.concatenate([jnp.broadcast_to(pi8[r + h:r + h + 1], (S5_SEGS, pw_)) for h in range(2)], axis=0)
                sin_bf[rows, re] = (sin[rows, re] + (pr * car2_r - pi * car2_i)).astype(BF16)
                sin_bf[rows, im] = (sin[rows, im] + (pr * car2_i + pi * car2_r)).astype(BF16)
        return 0

    lax.fori_loop(0, n_tiles // 8, fix, 0)

    for q in range(n_pair):
        lhs = sin_bf[:, q * pc:(q + 1) * pc]
        carried = lax.dot_general(lhs, c_blk[q], (((1,), (1,)), ((), ())), preferred_element_type=F32)
        ychunk[:, q * 2 * cw:(q + 1) * 2 * cw] = yi_ref[:, q * 2 * cw:(q + 1) * 2 * cw] + carried

    per_trip = 8

    def scatter(jj, _):
        groups, where = [], []
        for h in range(per_trip):
            j = per_trip * jj + h
            src = pl.ds(pl.multiple_of(j * S5_SEGS, S5_SEGS), S5_SEGS)
            for half in range(S5_CHUNK // n_grp):
                groups.append([ychunk[src, g * cw + half * lanes:g * cw + (half + 1) * lanes] for g in range(n_grp)])
                where.append(S5_CHUNK * j + half * n_grp)
        for t0, tiles in zip(where, _transpose_lane_blocks(groups, SSM_GROUP)):
            for k, tile in enumerate(tiles):
                y_ref[pl.ds(t0 + k, S5_SEGS, stride=seg_stride), :] = tile
        return 0

    lax.fori_loop(0, n_tiles // per_trip, scatter, 0)


def _s5_scan(u, fwd, bwd):
    seq, d_ssm = u.shape
    n_groups = d_ssm // SSM_GROUP
    n_state = SSM_STATE
    cw = S5_CHUNK * SSM_GROUP
    n_rows = seq // S5_CHUNK
    n_tiles = n_rows // S5_SEGS
    lanes = 128
    gpb = lanes // SSM_GROUP
    n_blocks = d_ssm // lanes
    bw = gpb * cw
    sw = gpb * 4 * n_state
    seg_stride = n_tiles * S5_CHUNK

    lag, st, wo, ap, pw = _s5_tables(fwd, bwd, n_tiles)

    def per_block(t):
        return pl.BlockSpec((t.shape[0], gpb) + t.shape[2:], lambda i: (0, i) + (0,) * (t.ndim - 2))

    y_intra, s_loc = pl.pallas_call(
        functools.partial(_s5_a_kernel, n_tiles=n_tiles, seg_stride=seg_stride),
        grid=(n_blocks,),
        in_specs=[
            pl.BlockSpec((seq, lanes), lambda i: (0, i)),
            per_block(lag), *[per_block(t) for t in st],
        ],
        out_specs=[
            pl.BlockSpec((n_rows, bw), lambda i: (0, i)),
            pl.BlockSpec((n_rows, sw), lambda i: (0, i)),
        ],
        out_shape=[
            jax.ShapeDtypeStruct((n_rows, n_blocks * bw), F32),
            jax.ShapeDtypeStruct((n_rows, n_blocks * sw), F32),
        ],
        scratch_shapes=[
            pltpu.VMEM((gpb, cw, cw), BF16),
            pltpu.VMEM((gpb // 2, 2 * cw, 8 * n_state), BF16),
            pltpu.VMEM((n_rows, S5_CHUNK * lanes), BF16),
        ],
        compiler_params=_params("arbitrary"),
        name="s5_a",
    )(u, lag, *st)

    pl_lanes = gpb * n_state
    return pl.pallas_call(
        functools.partial(_s5_bc_kernel, n_tiles=n_tiles, seg_stride=seg_stride),
        grid=(n_blocks,),
        in_specs=[
            pl.BlockSpec((n_rows, sw), lambda i: (0, i)),
            pl.BlockSpec((n_rows, bw), lambda i: (0, i)),
            *[per_block(t) for t in wo],
            pl.BlockSpec((8, pl_lanes), lambda i: (0, i)),
            pl.BlockSpec((4, n_tiles, pl_lanes), lambda i: (0, 0, i)),
        ],
        out_specs=pl.BlockSpec((seq, lanes), lambda i: (0, i)),
        out_shape=jax.ShapeDtypeStruct((seq, d_ssm), F32),
        scratch_shapes=[
            pltpu.VMEM((n_rows, sw), F32),
            pltpu.VMEM((n_rows, bw), F32),
            pltpu.VMEM((gpb // 2, 2 * cw, 8 * n_state), BF16),
            pltpu.VMEM((n_rows, sw), BF16),
        ],
        compiler_params=_params("arbitrary"),
        name="s5_bc",
    )(s_loc, y_intra, *wo, ap, pw)


def _s5_post_kernel(y_ref, u_ref, d_ref, w_ref, b_ref, g_ref, o_ref):
    y = y_ref[...] + d_ref[...] * u_ref[...]
    c0 = np.float32(np.sqrt(2.0 / np.pi))
    y = 0.5 * y * (1.0 + jnp.tanh(c0 * (y + np.float32(0.044715) * (y * y * y))))
    z = jnp.dot(y.astype(BF16), w_ref[...].astype(BF16), preferred_element_type=F32) + b_ref[...]
    o = y * (1.0 / (1.0 + jnp.exp(-z)))
    o_ref[...] = _rms(o, g_ref[...]).astype(BF16)


def _s5_post(y, u, d_skip, w_glu, b_glu, g):
    seq, d = y.shape
    tm = min(1024, seq)
    row = lambda i: (i, 0)
    fix = lambda i: (0, 0)
    return pl.pallas_call(
        _s5_post_kernel,
        grid=(seq // tm,),
        in_specs=[
            pl.BlockSpec((tm, d), row), pl.BlockSpec((tm, d), row), pl.BlockSpec((1, d), fix),
            pl.BlockSpec((d, d), fix), pl.BlockSpec((1, d), fix), pl.BlockSpec((1, d), fix),
        ],
        out_specs=pl.BlockSpec((tm, d), row),
        out_shape=jax.ShapeDtypeStruct((seq, d), BF16),
        compiler_params=_params("arbitrary"),
        name="s5_post",
    )(y, u, d_skip.reshape(1, d), w_glu, b_glu.reshape(1, d), g.reshape(1, d))


def _na_bias_table(rpb):
    n_heads = rpb.shape[0]
    n_quad = n_heads // HEADS_PER_DOT
    n_dy, n_dx = rpb.shape[1], rpb.shape[2]
    cols = np.arange(GRID_W)
    col_start = np.clip(cols - WIN_COLS // 2, 0, GRID_W - WIN_COLS)
    key_cols = np.arange(GRID_W)
    in_win = (key_cols[None, :] >= col_start[:, None]) & (key_cols[None, :] < col_start[:, None] + WIN_COLS)
    dx = key_cols[None, :] - cols[:, None] + (WIN_COLS - 1)
    pick_x = (dx[:, :, None] == np.arange(n_dx)).astype(np.float32)
    pick = np.zeros((2, GRID_W, 2 * GRID_W, n_dx), np.float32)
    for j in range(2):
        pick[j, :, j * GRID_W:(j + 1) * GRID_W] = pick_x
    dy = 2 * np.arange(WIN_ROWS)[None, :, None] + np.arange(2)[:, None, None] + np.arange(2)[None, None, :]
    pick_y = (dy[..., None] == np.arange(n_dy)).astype(np.float32)
    hp = lax.Precision.HIGHEST
    rows = jnp.einsum('spjy,hyx->spjhx', pick_y, rpb.astype(F32) * LOG2_E, precision=hp)
    rows = rows.reshape(2, WIN_ROWS, 2, n_quad, HEADS_PER_DOT, n_dx)
    b = jnp.einsum('spjqhx,jclx->sqphcl', rows, pick, precision=hp)
    keep = np.concatenate([in_win, in_win], axis=1)
    b = jnp.where(keep, b, MASK_NEG)
    return b.reshape(2, n_quad, WIN_ROWS, HEADS_PER_DOT * GRID_W, 2 * GRID_W)


def _na_kernel(q_ref, k_ref, v_ref, b_ref, g_ref, o_ref, *, rows, rows_per_step):
    n_keys = WIN_ROWS * GRID_W
    d_na = k_ref.shape[-1]
    pw = HEADS_PER_DOT * NA_HEAD_DIM
    row_head = lax.broadcasted_iota(jnp.int32, (HEADS_PER_DOT * GRID_W, pw), 0) // GRID_W
    col_head = lax.broadcasted_iota(jnp.int32, (HEADS_PER_DOT * GRID_W, pw), 1) // NA_HEAD_DIM
    diag = row_head == col_head
    out_head = lax.broadcasted_iota(jnp.int32, (GRID_W, pw), 1) // NA_HEAD_DIM
    first = pl.program_id(0) * rows_per_step
    block_start = jnp.clip(first - WIN_ROWS // 2, 0, rows - k_ref.shape[0])

    def one_row(i, _):
        r = first + i
        win_start = jnp.clip(r - WIN_ROWS // 2, 0, rows - WIN_ROWS)
        dy0 = win_start - r + (WIN_ROWS - 1)
        k = k_ref[pl.ds(win_start - block_start, WIN_ROWS)].reshape(n_keys, d_na)
        v = v_ref[pl.ds(win_start - block_start, WIN_ROWS)].reshape(n_keys, d_na)
        q_rows = pl.ds(pl.multiple_of(i * GRID_W, GRID_W), GRID_W)
        outs = []
        for p in range(d_na // pw):
            sl = slice(p * pw, (p + 1) * pw)
            q4 = q_ref[q_rows, sl]
            qbd = jnp.where(diag, jnp.concatenate([q4] * HEADS_PER_DOT, axis=0), jnp.zeros((), BF16))
            s = lax.dot_general(qbd, k[:, sl], (((1,), (1,)), ((), ())), preferred_element_type=F32)
            s = s + jnp.concatenate([b_ref[dy0 % 2, p, dy0 // 2 + j] for j in range(WIN_ROWS // 2)], axis=1)
            m = jnp.max(s, axis=-1, keepdims=True)
            e = jnp.exp2(s - m)
            l = jnp.sum(e, axis=-1, keepdims=True)
            o = jnp.dot(e.astype(BF16), v[:, sl], preferred_element_type=F32) / l
            acc = jnp.zeros((GRID_W, pw), F32)
            for h in range(HEADS_PER_DOT):
                acc = acc + jnp.where(out_head == h, o[h * GRID_W:(h + 1) * GRID_W], 0.0)
            outs.append(acc)
        y = jnp.concatenate(outs, axis=1)
        o_ref[q_rows, :] = _rms(y, g_ref[...]).astype(BF16)
        return 0

    lax.fori_loop(0, rows_per_step, one_row, 0, unroll=True)


def _neighbourhood_attention(qkv, rpb, g):
    seq = qkv.shape[0]
    d_na = qkv.shape[1] // 3
    rows = seq // GRID_W
    bias = _na_bias_table(rpb)
    qkv3 = qkv.reshape(rows, GRID_W, 3 * d_na)

    rps = 4
    key_rows = 2 * WIN_ROWS

    def block_start(b):
        return jnp.clip(b * rps - WIN_ROWS // 2, 0, rows - key_rows)

    window = (pl.Element(key_rows), pl.Element(GRID_W), pl.Element(d_na))

    return pl.pallas_call(
        functools.partial(_na_kernel, rows=rows, rows_per_step=rps),
        grid=(rows // rps,),
        in_specs=[
            pl.BlockSpec((rps * GRID_W, d_na), lambda b: (b, 0)),
            pl.BlockSpec(window, lambda b: (block_start(b), 0, d_na)),
            pl.BlockSpec(window, lambda b: (block_start(b), 0, 2 * d_na)),
            pl.BlockSpec(bias.shape, lambda b: (0, 0, 0, 0, 0)),
            pl.BlockSpec((1, d_na), lambda b: (0, 0)),
        ],
        out_specs=pl.BlockSpec((rps * GRID_W, d_na), lambda b: (b, 0)),
        out_shape=jax.ShapeDtypeStruct((seq, d_na), BF16),
        compiler_params=_params("arbitrary"),
        name="na",
    )(qkv, qkv3, qkv3, bias, g.reshape(1, d_na))


def _outproj_kernel(a_ref, b_ref, w_ref, x_ref, o_ref):
    da = a_ref.shape[-1]
    acc = jnp.dot(a_ref[...], w_ref[:da, :].astype(BF16), preferred_element_type=F32)
    acc = acc + jnp.dot(b_ref[...], w_ref[da:, :].astype(BF16), preferred_element_type=F32)
    o_ref[...] = x_ref[...] + acc


def _out_proj(y_ssm, y_na, w_out, x):
    seq, d_model = x.shape
    da, db = y_ssm.shape[1], y_na.shape[1]
    tm = min(2048, seq)
    tn = 512
    return pl.pallas_call(
        _outproj_kernel,
        grid=(seq // tm, d_model // tn),
        in_specs=[
            pl.BlockSpec((tm, da), lambda i, j: (i, 0)),
            pl.BlockSpec((tm, db), lambda i, j: (i, 0)),
            pl.BlockSpec((da + db, tn), lambda i, j: (0, j)),
            pl.BlockSpec((tm, tn), lambda i, j: (i, j)),
        ],
        out_specs=pl.BlockSpec((tm, tn), lambda i, j: (i, j)),
        out_shape=jax.ShapeDtypeStruct((seq, d_model), F32),
        compiler_params=_params("arbitrary", "arbitrary"),
        name="out_proj",
    )(y_ssm, y_na, w_out, x)


def _router_kernel(x_ref, g_ref, wt_ref, h_ref, a_ref):
    h = _rms(x_ref[...], g_ref[...])
    h_hi = h.astype(BF16)
    h_ref[...] = h_hi
    h_lo = (h - h_hi.astype(F32)).astype(BF16)
    w = wt_ref[...]
    w_hi = w.astype(BF16)
    w_lo = (w - w_hi.astype(F32)).astype(BF16)
    n_exp = w.shape[0]
    nt = (((1,), (1,)), ((), ()))
    both = lax.dot_general(jnp.concatenate([w_hi, w_lo], axis=0), h_hi, nt, preferred_element_type=F32)
    logits = both[:n_exp] + both[n_exp:] + lax.dot_general(w_hi, h_lo, nt, preferred_element_type=F32)
    m = jnp.max(logits, axis=0, keepdims=True)
    e = jnp.exp(logits - m)
    a_ref[...] = e / jnp.sum(e, axis=0, keepdims=True)


def _router(x1, g, w_router):
    seq, d_model = x1.shape
    n_exp = w_router.shape[1]
    tm = min(1024, seq)
    return pl.pallas_call(
        _router_kernel,
        grid=(seq // tm,),
        in_specs=[
            pl.BlockSpec((tm, d_model), lambda i: (i, 0)),
            pl.BlockSpec((1, d_model), lambda i: (0, 0)),
            pl.BlockSpec((n_exp, d_model), lambda i: (0, 0)),
        ],
        out_specs=[
            pl.BlockSpec((tm, d_model), lambda i: (i, 0)),
            pl.BlockSpec((n_exp, tm), lambda i: (0, i)),
        ],
        out_shape=[
            jax.ShapeDtypeStruct((seq, d_model), BF16),
            jax.ShapeDtypeStruct((n_exp, seq), F32),
        ],
        compiler_params=_params("arbitrary"),
        name="router",
    )(x1, g.reshape(1, d_model), w_router.T)


def _topk_kernel(a_ref, posw_ref, gate_ref, ws_ref, nr_ref, *, cap, blk, win):
    a = a_ref[...]
    n_exp, seq = a.shape
    n_blk = seq // blk
    bits = pltpu.bitcast(a, jnp.int32)

    def bit_step(i, thr):
        cand = thr | jnp.left_shift(jnp.int32(1), 30 - i)
        cnt = jnp.sum((bits >= cand).astype(jnp.int32), axis=-1, keepdims=True)
        return jnp.where(cnt >= cap, cand, thr)

    thr = lax.fori_loop(0, 31, bit_step, jnp.zeros((n_exp, 1), jnp.int32))
    gt = bits > thr
    eq = bits == thr
    need = cap - jnp.sum(gt.astype(jnp.int32), axis=-1, keepdims=True)

    tri = (lax.broadcasted_iota(jnp.int32, (blk, blk), 0)
           <= lax.broadcasted_iota(jnp.int32, (blk, blk), 1)).astype(BF16)
    blk_of_tok = lax.broadcasted_iota(jnp.int32, (seq, n_blk), 0) // blk
    tok_to_blk = (blk_of_tok == lax.broadcasted_iota(jnp.int32, (seq, n_blk), 1)).astype(BF16)
    blk_before = (lax.broadcasted_iota(jnp.int32, (n_blk, n_blk), 0)
                  < lax.broadcasted_iota(jnp.int32, (n_blk, n_blk), 1)).astype(BF16)
    erow = lax.broadcasted_iota(jnp.int32, (2 * n_blk, seq), 0)
    ecol = lax.broadcasted_iota(jnp.int32, (2 * n_blk, seq), 1) // blk
    expand = jnp.where(erow == ecol, 32.0, jnp.where(erow - n_blk == ecol, 1.0, 0.0)).astype(BF16)

    def prefix_counts(mask):
        mb = jnp.where(mask, 1.0, 0.0).astype(BF16)
        local = jnp.concatenate(
            [jnp.dot(mb[:, b * blk:(b + 1) * blk], tri, preferred_element_type=F32) for b in range(n_blk)],
            axis=1)
        per_blk = jnp.dot(mb, tok_to_blk, preferred_element_type=F32)
        start = jnp.dot(per_blk.astype(BF16), blk_before, preferred_element_type=F32)
        hi = jnp.floor(start * (1.0 / 32.0))
        parts = jnp.concatenate([hi, start - 32.0 * hi], axis=1).astype(BF16)
        start_tok = jnp.dot(parts, expand, preferred_element_type=F32)
        return local + start_tok, start, start_tok, per_blk

    eq_incl, _, _, _ = prefix_counts(eq)
    sel = gt | (eq & (eq_incl - 1.0 < need.astype(F32)))
    incl, start, start_tok, per_blk = prefix_counts(sel)

    def window(s):
        return jnp.floor(s * (1.0 / MOE_WIN_ALIGN)) * MOE_WIN_ALIGN

    posw_ref[...] = jnp.where(sel, (incl - 1.0 - window(start_tok)).astype(jnp.int32), -1)
    gate_ref[...] = jnp.where(sel, a, 0.0)
    ws_ref[...] = window(start).astype(jnp.int32)
    span = start - window(start) + per_blk
    rounds = jnp.floor((span + float(win - 1)) * (1.0 / win))
    nr_ref[...] = jnp.max(rounds, axis=0, keepdims=True).astype(jnp.int32)


def _topk(aff_t, cap, blk, win):
    n_exp, seq = aff_t.shape
    n_blk = seq // blk
    full = lambda *_: (0, 0)
    return pl.pallas_call(
        functools.partial(_topk_kernel, cap=cap, blk=blk, win=win),
        grid=(1,),
        in_specs=[pl.BlockSpec((n_exp, seq), full)],
        out_specs=[pl.BlockSpec((n_exp, seq), full), pl.BlockSpec((n_exp, seq), full),
                   pl.BlockSpec((n_exp, n_blk), full), pl.BlockSpec((1, n_blk), full)],
        out_shape=[
            jax.ShapeDtypeStruct((n_exp, seq), jnp.int32),
            jax.ShapeDtypeStruct((n_exp, seq), F32),
            jax.ShapeDtypeStruct((n_exp, n_blk), jnp.int32),
            jax.ShapeDtypeStruct((1, n_blk), jnp.int32),
        ],
        compiler_params=_params("arbitrary"),
        name="topk",
    )(aff_t)


def _window(ws_ref, e, b, r, n_blk, win, cap):
    ws = ws_ref[e * n_blk + b] + r * win
    start = jnp.minimum(ws, cap - win)
    return pl.multiple_of(start, MOE_WIN_ALIGN), ws - start


def _gather_kernel(ws_ref, nr_ref, h_ref, rel_ref, xe_ref, *, blk, win, n_blk):
    n_exp, cap, _ = xe_ref.shape
    xe_ref[...] = jnp.zeros_like(xe_ref)
    slot = lax.broadcasted_iota(jnp.int32, (win, blk), 0)

    def one_round(b, r):
        rows = h_ref[pl.ds(pl.multiple_of(b * blk, blk), blk), :]
        rel = rel_ref[b]
        starts, hots = [], []
        for e in range(n_exp):
            start, shift = _window(ws_ref, e, b, r, n_blk, win, cap)
            relr = rel[e:e + 1, :] - r * win
            key = jnp.where(relr >= 0, relr + shift, -1)
            hots.append(jnp.where(slot == key, 1.0, 0.0).astype(BF16))
            starts.append(start)
        res = jnp.dot(jnp.concatenate(hots, axis=0), rows, preferred_element_type=F32)
        for e in range(n_exp):
            dst = pl.ds(starts[e], win)
            xe_ref[e, dst, :] = (xe_ref[e, dst, :].astype(F32) + res[e * win:(e + 1) * win]).astype(BF16)

    def extra_rounds(b):
        def body(r, carry):
            one_round(b, r)
            return carry
        lax.fori_loop(1, nr_ref[b], body, 0)

    per_trip = 4

    def blocks(trip, _):
        for h in range(per_trip):
            one_round(per_trip * trip + h, 0)
        for h in range(per_trip):
            extra_rounds(per_trip * trip + h)
        return 0

    lax.fori_loop(0, n_blk // per_trip, blocks, 0)


def _moe_gather(ws_flat, n_rounds, h2, rel3, cap, win):
    seq, d_model = h2.shape
    n_blk, n_exp, blk = rel3.shape
    dq = d_model // 4
    grid_spec = pltpu.PrefetchScalarGridSpec(
        num_scalar_prefetch=2,
        grid=(4,),
        in_specs=[
            pl.BlockSpec((seq, dq), lambda c, ws, nr: (0, c)),
            pl.BlockSpec((n_blk, n_exp, blk), lambda c, ws, nr: (0, 0, 0)),
        ],
        out_specs=pl.BlockSpec((n_exp, cap, dq), lambda c, ws, nr: (0, 0, c)),
    )
    return pl.pallas_call(
        functools.partial(_gather_kernel, blk=blk, win=win, n_blk=n_blk),
        grid_spec=grid_spec,
        out_shape=jax.ShapeDtypeStruct((n_exp, cap, d_model), BF16),
        compiler_params=_params("arbitrary"),
        name="moe_gather",
    )(ws_flat, n_rounds, h2, rel3)


def _ffn_kernel(x_ref, wg_ref, wu_ref, wd_ref, y_ref, act_ref, *, n_f):
    s = pl.program_id(1)
    tf = wg_ref.shape[-1]

    @pl.when(s < n_f)
    def _():
        x = x_ref[0]
        g = jnp.dot(x, wg_ref[0].astype(BF16), preferred_element_type=F32)
        u = jnp.dot(x, wu_ref[0].astype(BF16), preferred_element_type=F32)
        act_ref[s] = (g * (1.0 / (1.0 + jnp.exp(-g))) * u).astype(BF16)

    @pl.when(s >= n_f)
    def _():
        acc = jnp.dot(act_ref[0], wd_ref[0, 0:tf, :].astype(BF16), preferred_element_type=F32)
        for f in range(1, n_f):
            acc = acc + jnp.dot(act_ref[f], wd_ref[0, f * tf:(f + 1) * tf, :].astype(BF16),
                                preferred_element_type=F32)
        y_ref[0] = acc.astype(BF16)


def _moe_ffn(xe, w_gate, w_up, w_down):
    n_exp, cap, d_model = xe.shape
    d_ff = w_gate.shape[-1]
    tf = 512
    tn = 1024
    n_f, n_n = d_ff // tf, d_model // tn
    up_tile = lambda e, s: (e, 0, jnp.minimum(s, n_f - 1))
    down_tile = lambda e, s: (e, 0, jnp.maximum(s - n_f, 0))
    return pl.pallas_call(
        functools.partial(_ffn_kernel, n_f=n_f),
        grid=(n_exp, n_f + n_n),
        in_specs=[
            pl.BlockSpec((1, cap, d_model), lambda e, s: (e, 0, 0)),
            pl.BlockSpec((1, d_model, tf), up_tile),
            pl.BlockSpec((1, d_model, tf), up_tile),
            pl.BlockSpec((1, d_ff, tn), down_tile),
        ],
        out_specs=pl.BlockSpec((1, cap, tn), down_tile),
        out_shape=jax.ShapeDtypeStruct((n_exp, cap, d_model), BF16),
        scratch_shapes=[pltpu.VMEM((n_f, cap, tf), BF16)],
        compiler_params=_params("arbitrary", "arbitrary"),
        name="moe_ffn",
    )(xe, w_gate, w_up, w_down)


def _combine_kernel(ws_ref, nr_ref, ye_ref, x_ref, rel_ref, gate_ref, o_ref, *, blk, win, n_blk):
    n_exp, cap, _ = ye_ref.shape
    sub = x_ref.shape[0] // blk
    slot = lax.broadcasted_iota(jnp.int32, (win, blk), 0)

    def one_round(b, r):
        rel = rel_ref[b]
        gate = gate_ref[b]
        gates, wins = [], []
        for e in range(n_exp):
            start, shift = _window(ws_ref, e, b, r, n_blk, win, cap)
            relr = rel[e:e + 1, :] - r * win
            key = jnp.where(relr >= 0, relr + shift, -1)
            gates.append(jnp.where(slot == key, gate[e:e + 1, :], 0.0).astype(BF16))
            wins.append(ye_ref[e, pl.ds(start, win), :])
        return lax.dot_general(jnp.concatenate(gates, axis=0), jnp.concatenate(wins, axis=0),
                               (((0,), (0,)), ((), ())), preferred_element_type=F32)

    for s in range(sub):
        tok = slice(s * blk, (s + 1) * blk)
        o_ref[tok, :] = x_ref[tok, :] + one_round(pl.program_id(1) * sub + s, 0)
    for s in range(sub):
        tok = slice(s * blk, (s + 1) * blk)
        b = pl.program_id(1) * sub + s

        def extra(r, carry, b=b, tok=tok):
            o_ref[tok, :] += one_round(b, r)
            return carry

        lax.fori_loop(1, nr_ref[b], extra, 0)


def _moe_combine(ws_flat, n_rounds, ye, x1, rel3, gate3, win):
    seq, d_model = x1.shape
    n_exp, cap, _ = ye.shape
    n_blk, _, blk = rel3.shape
    dq = d_model // 4
    tile = min(8, n_blk) * blk
    whole = lambda c, t, ws, nr: (0, 0, 0)
    grid_spec = pltpu.PrefetchScalarGridSpec(
        num_scalar_prefetch=2,
        grid=(4, seq // tile),
        in_specs=[
            pl.BlockSpec((n_exp, cap, dq), lambda c, t, ws, nr: (0, 0, c)),
            pl.BlockSpec((tile, dq), lambda c, t, ws, nr: (t, c)),
            pl.BlockSpec(rel3.shape, whole),
            pl.BlockSpec(gate3.shape, whole),
        ],
        out_specs=pl.BlockSpec((tile, dq), lambda c, t, ws, nr: (t, c)),
    )
    return pl.pallas_call(
        functools.partial(_combine_kernel, blk=blk, win=win, n_blk=n_blk),
        grid_spec=grid_spec,
        out_shape=jax.ShapeDtypeStruct((seq, d_model), F32),
        compiler_params=_params("arbitrary", "arbitrary"),
        name="moe_combine",
    )(ws_flat, n_rounds, ye, x1, rel3, gate3)


def _final_norm_kernel(x_ref, g_ref, o_ref):
    o_ref[...] = _rms(x_ref[...], g_ref[...])


def _final_norm(x, g):
    seq, d_model = x.shape
    tm = min(1024, seq)
    return pl.pallas_call(
        _final_norm_kernel,
        grid=(seq // tm,),
        in_specs=[pl.BlockSpec((tm, d_model), lambda i: (i, 0)), pl.BlockSpec((1, d_model), lambda i: (0, 0))],
        out_specs=pl.BlockSpec((tm, d_model), lambda i: (i, 0)),
        out_shape=jax.ShapeDtypeStruct((seq, d_model), F32),
        compiler_params=_params("arbitrary"),
        name="final_norm",
    )(x, g.reshape(1, d_model))


def _layer(x, norm_mix_g, w_in, fwd, bwd, ssm_d, w_glu, b_glu, na_rpb, g_ssm_out, g_na_out, w_out,
           norm_ffn_g, w_router, w_gate, w_up, w_down):
    seq, d_model = x.shape
    d_ssm = ssm_d.shape[0]
    d_na = g_na_out.shape[0]
    n_exp = w_router.shape[1]
    cap = EC_CAPACITY_FACTOR * seq // n_exp
    blk = min(MOE_TOK_BLOCK, cap // 2)
    win = min(MOE_WIN, cap)

    u, qkv = _in_proj(x, norm_mix_g, w_in, d_ssm, d_na)
    y_ssm = _s5_post(_s5_scan(u, fwd, bwd), u, ssm_d, w_glu, b_glu, g_ssm_out)
    y_na = _neighbourhood_attention(qkv, na_rpb, g_na_out)
    x1 = _out_proj(y_ssm, y_na, w_out, x)

    h2, aff_t = _router(x1, norm_ffn_g, w_router)
    rel, gate, ws, n_rounds = _topk(aff_t, cap, blk, win)
    ws_flat = ws.reshape(-1)
    n_rounds = n_rounds.reshape(-1)
    rel3 = jnp.swapaxes(rel.reshape(n_exp, seq // blk, blk), 0, 1)
    gate3 = jnp.swapaxes(gate.reshape(n_exp, seq // blk, blk), 0, 1)
    xe = _moe_gather(ws_flat, n_rounds, h2, rel3, cap, win)
    ye = _moe_ffn(xe, w_gate, w_up, w_down)
    return _moe_combine(ws_flat, n_rounds, ye, x1, rel3, gate3, win)


def kernel(x, norm_mix_g, w_in, a_re_fwd, a_im_fwd, log_dt_fwd, b_re_fwd, b_im_fwd, c_re_fwd, c_im_fwd, a_re_bwd, a_im_bwd, log_dt_bwd, b_re_bwd, b_im_bwd, c_re_bwd, c_im_bwd, ssm_d, w_glu, b_glu, na_rpb, g_ssm_out, g_na_out, w_out, norm_ffn_g, w_router, w_gate, w_up, w_down, norm_final_g):
    bsz = x.shape[0]
    depth = w_in.shape[0]
    outs = []
    for b in range(bsz):
        xb = x[b]
        for l in range(depth):
            fwd = (a_re_fwd[l], a_im_fwd[l], log_dt_fwd[l], b_re_fwd[l], b_im_fwd[l], c_re_fwd[l], c_im_fwd[l])
            bwd = (a_re_bwd[l], a_im_bwd[l], log_dt_bwd[l], b_re_bwd[l], b_im_bwd[l], c_re_bwd[l], c_im_bwd[l])
            xb = _layer(xb, norm_mix_g[l], w_in[l], fwd, bwd, ssm_d[l], w_glu[l], b_glu[l], na_rpb[l],
                        g_ssm_out[l], g_na_out[l], w_out[l], norm_ffn_g[l], w_router[l],
                        w_gate[l], w_up[l], w_down[l])
        outs.append(_final_norm(xb, norm_final_g))
    return jnp.stack(outs)
```

```python
import functools

import numpy as np
import jax
import jax.numpy as jnp
from jax import lax
from jax.experimental import pallas as pl
from jax.experimental.pallas import tpu as pltpu

F32 = jnp.float32
BF16 = jnp.bfloat16

RMS_EPS = 1e-6
SSM_GROUP = 16
SSM_STATE = 64
NA_HEADS = 16
NA_HEAD_DIM = 64
GRID_W = 64
WIN_ROWS = 8
WIN_COLS = 16
N_EXPERTS = 16
EC_CAPACITY_FACTOR = 2

S5_CHUNK = 16
S5_SEGS = 8
HEADS_PER_DOT = 4
MOE_TOK_BLOCK = 256
MOE_WIN_ALIGN = 16
MOE_WIN = 64
MASK_NEG = -1e30
LOG2_E = float(np.log2(np.e))

VMEM_LIMIT_BYTES = 56 * 1024 * 1024


def _params(*semantics):
    return pltpu.CompilerParams(dimension_semantics=semantics, vmem_limit_bytes=VMEM_LIMIT_BYTES)


def _rms(x, g):
    ms = jnp.mean(x * x, axis=-1, keepdims=True)
    return x * lax.rsqrt(ms + RMS_EPS) * g


def _inproj_kernel(x_ref, g_ref, w_ref, u_ref, qkv_ref, h_scr, *, n_u, n_q, q_scale):
    j = pl.program_id(1)

    @pl.when(j == 0)
    def _():
        h_scr[...] = _rms(x_ref[...], g_ref[...]).astype(BF16)

    def project():
        return jnp.dot(h_scr[...], w_ref[...].astype(BF16), preferred_element_type=F32)

    @pl.when(j < n_u)
    def _():
        u_ref[...] = project()

    @pl.when(j >= n_u)
    def _():
        scale = jnp.where(j < n_u + n_q, q_scale, 1.0).astype(F32)
        qkv_ref[...] = (project() * scale).astype(BF16)


def _in_proj(x, g, w_in, d_ssm, d_na):
    seq, d_model = x.shape
    tm = min(1024, seq)
    tn = 1024
    n_u, n_q = d_ssm // tn, d_na // tn
    n_cols = w_in.shape[1] // tn
    kern = functools.partial(_inproj_kernel, n_u=n_u, n_q=n_q, q_scale=NA_HEAD_DIM ** -0.5 * LOG2_E)
    return pl.pallas_call(
        kern,
        grid=(seq // tm, n_cols),
        in_specs=[
            pl.BlockSpec((tm, d_model), lambda i, j: (i, 0)),
            pl.BlockSpec((1, d_model), lambda i, j: (0, 0)),
            pl.BlockSpec((d_model, tn), lambda i, j: (0, j)),
        ],
        out_specs=[
            pl.BlockSpec((tm, tn), lambda i, j: (i, jnp.minimum(j, n_u - 1))),
            pl.BlockSpec((tm, tn), lambda i, j: (i, jnp.maximum(j - n_u, 0))),
        ],
        out_shape=[
            jax.ShapeDtypeStruct((seq, d_ssm), F32),
            jax.ShapeDtypeStruct((seq, 3 * d_na), BF16),
        ],
        scratch_shapes=[pltpu.VMEM((tm, d_model), BF16)],
        compiler_params=_params("arbitrary", "arbitrary"),
        name="in_proj",
    )(x, g.reshape(1, d_model), w_in)


def _cmul(ar, ai, br, bi):
    return ar * br - ai * bi, ar * bi + ai * br


def _s5_tables(fwd, bwd, n_tiles):
    t_len = S5_CHUNK
    a_re, a_im, log_dt, b_re, b_im, c_re, c_im = (jnp.stack([f, b]).astype(F32) for f, b in zip(fwd, bwd))
    _, n_grp, n_st = a_re.shape
    n_ch = b_re.shape[-1]
    dt = jnp.exp(log_dt)[:, :, None]
    xr, xi = a_re * dt, a_im * dt
    steps = np.arange(t_len)

    def power(x_r, x_i, exps):
        e = jnp.asarray(exps, F32).reshape(exps.shape + (1,) * (x_r.ndim - 1))
        mag = jnp.exp(x_r[None] * e)
        return mag * jnp.cos(x_i[None] * e), mag * jnp.sin(x_i[None] * e)

    def per_step(exps):
        p_r, p_i = power(xr, xi, exps)
        return jnp.transpose(p_r, (1, 2, 0, 3)), jnp.transpose(p_i, (1, 2, 0, 3))

    a1_r, a1_i = power(xr, xi, np.ones((1, 2)))
    nr, ni = a1_r[0] - 1.0, a1_i[0]
    den = a_re * a_re + a_im * a_im
    qr, qi = (nr * a_re + ni * a_im) / den, (ni * a_re - nr * a_im) / den
    bb_r, bb_i = _cmul(qr[:, :, None, :], qi[:, :, None, :], jnp.swapaxes(b_re, 2, 3), jnp.swapaxes(b_im, 2, 3))
    am_r, am_i = per_step(np.stack([steps, steps[::-1]], axis=1))
    y_r, y_i = _cmul(c_re[:, :, None], c_im[:, :, None], am_r[:, :, :, None, :], am_i[:, :, :, None, :])
    y = jnp.concatenate([y_r, y_i], axis=-1).reshape(2, n_grp, t_len * n_ch, 2 * n_st)
    lag = jnp.einsum('dgck,dgqk->dgcq', jnp.concatenate([bb_r, -bb_i], axis=-1), y,
                     precision=lax.Precision.HIGH)
    st = per_step(np.stack([t_len - 1 - steps, steps], axis=1)) + (bb_r, bb_i)
    wo = per_step(np.stack([steps + 1, t_len - steps], axis=1)) + (c_re, c_im)
    xrf, xif = xr.reshape(2, -1), xi.reshape(2, -1)
    tiles = np.arange(n_tiles)
    ends = power(xrf, xif, np.array([[t_len, t_len], [t_len * n_tiles, t_len * n_tiles]]))
    ap = jnp.stack(ends, axis=2).reshape(8, -1)
    at_r, at_i = power(xrf, xif, t_len * np.stack([tiles, tiles[::-1]], axis=1))
    pw = jnp.transpose(jnp.stack([at_r, at_i], axis=2), (1, 2, 0, 3)).reshape(4, n_tiles, -1)
    return lag, st, wo, ap, pw


def _transpose_lane_blocks(groups, width):
    groups = [list(g) for g in groups]
    n = len(groups[0])
    first = groups[0][0]
    lanes = first.shape[-1]
    axis = first.ndim - 1
    block = lax.broadcasted_iota(jnp.int32, first.shape, axis) // width
    d = n // 2
    while d:
        upper = (block & d) != 0
        pairs = [(g, i) for g in groups for i in range(n) if i & d == 0]
        up = [pltpu.roll(g[i | d], d * width, axis) for g, i in pairs]
        down = [pltpu.roll(g[i], lanes - d * width, axis) for g, i in pairs]
        for (g, i), u, dn in zip(pairs, up, down):
            g[i], g[i | d] = jnp.where(upper, u, g[i]), jnp.where(upper, g[i | d], dn)
        d //= 2
    return groups


def _s5_a_kernel(u_ref, lag_ref, pe_r_ref, pe_i_ref, bb_r_ref, bb_i_ref, y_ref, s_ref, w_grp, e_pair, ucat,
                 *, n_tiles, seg_stride):
    n_grp = lag_ref.shape[1]
    cw = S5_CHUNK * SSM_GROUP
    n_plane, n_state = 4, pe_r_ref.shape[-1]
    lanes = u_ref.shape[-1]
    pc = n_plane * 2 * n_state

    @pl.when(pl.program_id(0) == 0)
    def _():
        e_pair[...] = jnp.zeros_like(e_pair)

    for g in range(n_grp):
        blank = jnp.zeros((SSM_GROUP, cw), F32)
        strip = jnp.concatenate([blank, lag_ref[0, g], lag_ref[1, g], blank], axis=1)
        for t in range(S5_CHUNK):
            r0 = t * SSM_GROUP
            fwd_lo = cw - t * SSM_GROUP
            bwd_lo = 2 * cw + (S5_CHUNK - 1 - t) * SSM_GROUP
            piece = strip[:, fwd_lo:fwd_lo + cw] + strip[:, bwd_lo:bwd_lo + cw]
            w_grp[g, r0:r0 + SSM_GROUP, :] = piece.astype(BF16)
            for d in range(2):
                planes = _cmul(pe_r_ref[d, g, t:t + 1, :], pe_i_ref[d, g, t:t + 1, :], bb_r_ref[d, g], bb_i_ref[d, g])
                for ri in range(2):
                    c0 = (2 * d + ri) * 2 * n_state + (g % 2) * n_state
                    rows = slice((g % 2) * cw + r0, (g % 2) * cw + r0 + SSM_GROUP)
                    e_pair[g // 2, rows, c0:c0 + n_state] = planes[ri].astype(BF16)

    per_trip = 4

    def gather(trip, _):
        groups, where = [], []
        for i in range(per_trip):
            jj = per_trip * trip + i
            pieces = []
            for t in range(S5_CHUNK):
                rows = [u_ref[pl.ds(S5_CHUNK * (2 * jj + h) + t, S5_SEGS, stride=seg_stride), :] for h in range(2)]
                pieces.append(jnp.concatenate(rows, axis=0))
            dst = pl.ds(pl.multiple_of(jj * 2 * S5_SEGS, 2 * S5_SEGS), 2 * S5_SEGS)
            for half in range(S5_CHUNK // n_grp):
                groups.append(pieces[half * n_grp:(half + 1) * n_grp])
                where.append((dst, half))
        for (dst, half), tiles in zip(where, _transpose_lane_blocks(groups, SSM_GROUP)):
            for g in range(n_grp):
                c0 = g * cw + half * lanes
                ucat[dst, c0:c0 + lanes] = tiles[g].astype(BF16)
        return 0

    lax.fori_loop(0, n_tiles // (2 * per_trip), gather, 0)
    ug = ucat[...]
    for g in range(n_grp):
        y_ref[:, g * cw:(g + 1) * cw] = jnp.dot(ug[:, g * cw:(g + 1) * cw], w_grp[g], preferred_element_type=F32)
    for q in range(n_grp // 2):
        s_ref[:, q * pc:(q + 1) * pc] = jnp.dot(ug[:, q * 2 * cw:(q + 1) * 2 * cw], e_pair[q],
                                                preferred_element_type=F32)


def _s5_bc_kernel(s_ref, yi_ref, po_r_ref, po_i_ref, c_r_ref, c_i_ref, ap_ref, pw_ref, y_ref, sin, ychunk, c_blk,
                  sin_bf, *, n_tiles, seg_stride):
    n_plane = 4
    _, n_grp, _, n_state = po_r_ref.shape
    cw = S5_CHUNK * SSM_GROUP
    n_pair = n_grp // 2
    pw_ = 2 * n_state
    pc = n_plane * pw_
    lanes = y_ref.shape[-1]

    @pl.when(pl.program_id(0) == 0)
    def _():
        c_blk[...] = jnp.zeros_like(c_blk)

    for g in range(n_grp):
        gl = g % 2
        for d in range(2):
            for t in range(S5_CHUNK):
                w_r, w_i = _cmul(c_r_ref[d, g], c_i_ref[d, g], po_r_ref[d, g, t:t + 1, :], po_i_ref[d, g, t:t + 1, :])
                rows = slice(gl * cw + t * SSM_GROUP, gl * cw + (t + 1) * SSM_GROUP)
                for ri, plane in enumerate((w_r, -w_i)):
                    c0 = (2 * d + ri) * pw_ + gl * n_state
                    c_blk[g // 2, rows, c0:c0 + n_state] = plane.astype(BF16)

    chains = [(q, d) for q in range(n_pair) for d in range(2)]

    def plane_lanes(q, d):
        base = q * pc + d * 2 * pw_
        return slice(base, base + pw_), slice(base + pw_, base + 2 * pw_)

    def coef(row, q):
        return ap_ref[row:row + 1, q * pw_:(q + 1) * pw_]

    def tile_rows(n, d):
        j = n if d == 0 else n_tiles - 1 - n
        return pl.ds(pl.multiple_of(j * S5_SEGS, S5_SEGS), S5_SEGS)

    def step(n, carry):
        out = []
        for (q, d), (zr, zi) in zip(chains, carry):
            re, im = plane_lanes(q, d)
            rows = tile_rows(n, d)
            sin[rows, re] = zr
            sin[rows, im] = zi
            ar, ai = coef(2 * d, q), coef(2 * d + 1, q)
            out.append((ar * zr - ai * zi + s_ref[rows, re], ar * zi + ai * zr + s_ref[rows, im]))
        return tuple(out)

    z0 = jnp.zeros((S5_SEGS, pw_), F32)
    ends = lax.fori_loop(0, n_tiles, step, tuple((z0, z0) for _ in chains))

    carries = []
    for (q, d), (zr, zi) in zip(chains, ends):
        sr, si = coef(4 + 2 * d, q), coef(5 + 2 * d, q)
        cr = jnp.zeros((1, pw_), F32)
        ci = jnp.zeros((1, pw_), F32)
        seg_r = [None] * S5_SEGS
        seg_i = [None] * S5_SEGS
        for s in (range(S5_SEGS) if d == 0 else range(S5_SEGS - 1, -1, -1)):
            seg_r[s], seg_i[s] = cr, ci
            cr, ci = (zr[s:s + 1] + sr * cr - si * ci, zi[s:s + 1] + sr * ci + si * cr)
        carries.append((jnp.concatenate(seg_r, axis=0), jnp.concatenate(seg_i, axis=0)))

    def fix(n8, _):
        tiles = pl.ds(pl.multiple_of(n8 * 8, 8), 8)
        for (q, d), (car_r, car_i) in zip(chains, carries):
            re, im = plane_lanes(q, d)
            pr8 = pw_ref[2 * d, tiles, q * pw_:(q + 1) * pw_]
            pi8 = pw_ref[2 * d + 1, tiles, q * pw_:(q + 1) * pw_]
            car2_r = jnp.concatenate([car_r, car_r], axis=0)
            car2_i = jnp.concatenate([car_i, car_i], axis=0)
            for r in range(0, 8, 2):
                rows = pl.ds(pl.multiple_of((n8 * 8 + r) * S5_SEGS, 2 * S5_SEGS), 2 * S5_SEGS)
                pr = jnp.concatenate([jnp.broadcast_to(pr8[r + h:r + h + 1], (S5_SEGS, pw_)) for h in range(2)], axis=0)
                pi = jnp.concatenate([jnp.broadcast_to(pi8[r + h:r + h + 1], (S5_SEGS, pw_)) for h in range(2)], axis=0)
                sin_bf[rows, re] = (sin[rows, re] + (pr * car2_r - pi * car2_i)).astype(BF16)
                sin_bf[rows, im] = (sin[rows, im] + (pr * car2_i + pi * car2_r)).astype(BF16)
        return 0

    lax.fori_loop(0, n_tiles // 8, fix, 0)

    for q in range(n_pair):
        lhs = sin_bf[:, q * pc:(q + 1) * pc]
        carried = lax.dot_general(lhs, c_blk[q], (((1,), (1,)), ((), ())), preferred_element_type=F32)
        ychunk[:, q * 2 * cw:(q + 1) * 2 * cw] = yi_ref[:, q * 2 * cw:(q + 1) * 2 * cw] + carried

    per_trip = 8

    def scatter(jj, _):
        groups, where = [], []
        for h in range(per_trip):
            j = per_trip * jj + h
            src = pl.ds(pl.multiple_of(j * S5_SEGS, S5_SEGS), S5_SEGS)
            for half in range(S5_CHUNK // n_grp):
                groups.append([ychunk[src, g * cw + half * lanes:g * cw + (half + 1) * lanes] for g in range(n_grp)])
                where.append(S5_CHUNK * j + half * n_grp)
        for t0, tiles in zip(where, _transpose_lane_blocks(groups, SSM_GROUP)):
            for k, tile in enumerate(tiles):
                y_ref[pl.ds(t0 + k, S5_SEGS, stride=seg_stride), :] = tile
        return 0

    lax.fori_loop(0, n_tiles // per_trip, scatter, 0)


def _s5_scan(u, fwd, bwd):
    seq, d_ssm = u.shape
    n_groups = d_ssm // SSM_GROUP
    n_state = SSM_STATE
    cw = S5_CHUNK * SSM_GROUP
    n_rows = seq // S5_CHUNK
    n_tiles = n_rows // S5_SEGS
    lanes = 128
    gpb = lanes // SSM_GROUP
    n_blocks = d_ssm // lanes
    bw = gpb * cw
    sw = gpb * 4 * n_state
    seg_stride = n_tiles * S5_CHUNK

    lag, st, wo, ap, pw = _s5_tables(fwd, bwd, n_tiles)

    def per_block(t):
        return pl.BlockSpec((t.shape[0], gpb) + t.shape[2:], lambda i: (0, i) + (0,) * (t.ndim - 2))

    y_intra, s_loc = pl.pallas_call(
        functools.partial(_s5_a_kernel, n_tiles=n_tiles, seg_stride=seg_stride),
        grid=(n_blocks,),
        in_specs=[
            pl.BlockSpec((seq, lanes), lambda i: (0, i)),
            per_block(lag), *[per_block(t) for t in st],
        ],
        out_specs=[
            pl.BlockSpec((n_rows, bw), lambda i: (0, i)),
            pl.BlockSpec((n_rows, sw), lambda i: (0, i)),
        ],
        out_shape=[
            jax.ShapeDtypeStruct((n_rows, n_blocks * bw), F32),
            jax.ShapeDtypeStruct((n_rows, n_blocks * sw), F32),
        ],
        scratch_shapes=[
            pltpu.VMEM((gpb, cw, cw), BF16),
            pltpu.VMEM((gpb // 2, 2 * cw, 8 * n_state), BF16),
            pltpu.VMEM((n_rows, S5_CHUNK * lanes), BF16),
        ],
        compiler_params=_params("arbitrary"),
        name="s5_a",
    )(u, lag, *st)

    pl_lanes = gpb * n_state
    return pl.pallas_call(
        functools.partial(_s5_bc_kernel, n_tiles=n_tiles, seg_stride=seg_stride),
        grid=(n_blocks,),
        in_specs=[
            pl.BlockSpec((n_rows, sw), lambda i: (0, i)),
            pl.BlockSpec((n_rows, bw), lambda i: (0, i)),
            *[per_block(t) for t in wo],
            pl.BlockSpec((8, pl_lanes), lambda i: (0, i)),
            pl.BlockSpec((4, n_tiles, pl_lanes), lambda i: (0, 0, i)),
        ],
        out_specs=pl.BlockSpec((seq, lanes), lambda i: (0, i)),
        out_shape=jax.ShapeDtypeStruct((seq, d_ssm), F32),
        scratch_shapes=[
            pltpu.VMEM((n_rows, sw), F32),
            pltpu.VMEM((n_rows, bw), F32),
            pltpu.VMEM((gpb // 2, 2 * cw, 8 * n_state), BF16),
            pltpu.VMEM((n_rows, sw), BF16),
        ],
        compiler_params=_params("arbitrary"),
        name="s5_bc",
    )(s_loc, y_intra, *wo, ap, pw)


def _s5_post_kernel(y_ref, u_ref, d_ref, w_ref, b_ref, g_ref, o_ref):
    y = y_ref[...] + d_ref[...] * u_ref[...]
    c0 = np.float32(np.sqrt(2.0 / np.pi))
    c01 = np.float32(np.sqrt(2.0 / np.pi) * 0.044715)
    half_y = 0.5 * y
    y = half_y + half_y * jnp.tanh(y * (c0 + c01 * (y * y)))
    z = jnp.dot(y.astype(BF16), w_ref[...].astype(BF16), preferred_element_type=F32) + b_ref[...]
    half_y = 0.5 * y
    o = half_y + half_y * jnp.tanh(0.5 * z)
    o_ref[...] = _rms(o, g_ref[...]).astype(BF16)


def _s5_post(y, u, d_skip, w_glu, b_glu, g):
    seq, d = y.shape
    tm = min(1024, seq)
    row = lambda i: (i, 0)
    fix = lambda i: (0, 0)
    return pl.pallas_call(
        _s5_post_kernel,
        grid=(seq // tm,),
        in_specs=[
            pl.BlockSpec((tm, d), row), pl.BlockSpec((tm, d), row), pl.BlockSpec((1, d), fix),
            pl.BlockSpec((d, d), fix), pl.BlockSpec((1, d), fix), pl.BlockSpec((1, d), fix),
        ],
        out_specs=pl.BlockSpec((tm, d), row),
        out_shape=jax.ShapeDtypeStruct((seq, d), BF16),
        compiler_params=_params("arbitrary"),
        name="s5_post",
    )(y, u, d_skip.reshape(1, d), w_glu, b_glu.reshape(1, d), g.reshape(1, d))


def _na_bias_table(rpb):
    n_heads = rpb.shape[0]
    n_quad = n_heads // HEADS_PER_DOT
    n_dy, n_dx = rpb.shape[1], rpb.shape[2]
    cols = np.arange(GRID_W)
    col_start = np.clip(cols - WIN_COLS // 2, 0, GRID_W - WIN_COLS)
    key_cols = np.arange(GRID_W)
    in_win = (key_cols[None, :] >= col_start[:, None]) & (key_cols[None, :] < col_start[:, None] + WIN_COLS)
    dx = key_cols[None, :] - cols[:, None] + (WIN_COLS - 1)
    pick_x = (dx[:, :, None] == np.arange(n_dx)).astype(np.float32)
    pick = np.zeros((2, GRID_W, 2 * GRID_W, n_dx), np.float32)
    for j in range(2):
        pick[j, :, j * GRID_W:(j + 1) * GRID_W] = pick_x
    dy = 2 * np.arange(WIN_ROWS)[None, :, None] + np.arange(2)[:, None, None] + np.arange(2)[None, None, :]
    pick_y = (dy[..., None] == np.arange(n_dy)).astype(np.float32)
    hp = lax.Precision.HIGHEST
    rows = jnp.einsum('spjy,hyx->spjhx', pick_y, rpb.astype(F32) * LOG2_E, precision=hp)
    rows = rows.reshape(2, WIN_ROWS, 2, n_quad, HEADS_PER_DOT, n_dx)
    b = jnp.einsum('spjqhx,jclx->sqphcl', rows, pick, precision=lax.Precision.HIGH)
    keep = np.concatenate([in_win, in_win], axis=1)
    b = jnp.where(keep, b, MASK_NEG)
    return b.reshape(2, n_quad, WIN_ROWS, HEADS_PER_DOT * GRID_W, 2 * GRID_W)


def _na_kernel(q_ref, k_ref, v_ref, b_ref, g_ref, o_ref, *, rows, rows_per_step):
    n_keys = WIN_ROWS * GRID_W
    d_na = k_ref.shape[-1]
    pw = HEADS_PER_DOT * NA_HEAD_DIM
    row_head = lax.broadcasted_iota(jnp.int32, (HEADS_PER_DOT * GRID_W, pw), 0) // GRID_W
    col_head = lax.broadcasted_iota(jnp.int32, (HEADS_PER_DOT * GRID_W, pw), 1) // NA_HEAD_DIM
    diag = row_head == col_head
    out_head = lax.broadcasted_iota(jnp.int32, (GRID_W, pw), 1) // NA_HEAD_DIM
    first = pl.program_id(0) * rows_per_step
    block_start = jnp.clip(first - WIN_ROWS // 2, 0, rows - k_ref.shape[0])

    def one_row(i, _):
        r = first + i
        win_start = jnp.clip(r - WIN_ROWS // 2, 0, rows - WIN_ROWS)
        dy0 = win_start - r + (WIN_ROWS - 1)
        k = k_ref[pl.ds(win_start - block_start, WIN_ROWS)].reshape(n_keys, d_na)
        v = v_ref[pl.ds(win_start - block_start, WIN_ROWS)].reshape(n_keys, d_na)
        q_rows = pl.ds(pl.multiple_of(i * GRID_W, GRID_W), GRID_W)
        outs = []
        for p in range(d_na // pw):
            sl = slice(p * pw, (p + 1) * pw)
            q4 = q_ref[q_rows, sl]
            qbd = jnp.where(diag, jnp.concatenate([q4] * HEADS_PER_DOT, axis=0), jnp.zeros((), BF16))
            s = lax.dot_general(qbd, k[:, sl], (((1,), (1,)), ((), ())), preferred_element_type=F32)
            s = s + jnp.concatenate([b_ref[dy0 % 2, p, dy0 // 2 + j] for j in range(WIN_ROWS // 2)], axis=1)
            m = jnp.max(s, axis=-1, keepdims=True)
            e = jnp.exp2(s - m)
            l = jnp.sum(e, axis=-1, keepdims=True)
            o = jnp.dot(e.astype(BF16), v[:, sl], preferred_element_type=F32) / l
            acc = jnp.zeros((GRID_W, pw), F32)
            for h in range(HEADS_PER_DOT):
                acc = acc + jnp.where(out_head == h, o[h * GRID_W:(h + 1) * GRID_W], 0.0)
            outs.append(acc)
        y = jnp.concatenate(outs, axis=1)
        o_ref[q_rows, :] = _rms(y, g_ref[...]).astype(BF16)
        return 0

    lax.fori_loop(0, rows_per_step, one_row, 0, unroll=4)


def _neighbourhood_attention(qkv, rpb, g):
    seq = qkv.shape[0]
    d_na = qkv.shape[1] // 3
    rows = seq // GRID_W
    bias = _na_bias_table(rpb)
    qkv3 = qkv.reshape(rows, GRID_W, 3 * d_na)

    rps = 8
    key_rows = 2 * WIN_ROWS
    assert rps - 1 + WIN_ROWS <= key_rows and rows % rps == 0

    def block_start(b):
        return jnp.clip(b * rps - WIN_ROWS // 2, 0, rows - key_rows)

    window = (pl.Element(key_rows), pl.Element(GRID_W), pl.Element(d_na))

    return pl.pallas_call(
        functools.partial(_na_kernel, rows=rows, rows_per_step=rps),
        grid=(rows // rps,),
        in_specs=[
            pl.BlockSpec((rps * GRID_W, d_na), lambda b: (b, 0)),
            pl.BlockSpec(window, lambda b: (block_start(b), 0, d_na)),
            pl.BlockSpec(window, lambda b: (block_start(b), 0, 2 * d_na)),
            pl.BlockSpec(bias.shape, lambda b: (0, 0, 0, 0, 0)),
            pl.BlockSpec((1, d_na), lambda b: (0, 0)),
        ],
        out_specs=pl.BlockSpec((rps * GRID_W, d_na), lambda b: (b, 0)),
        out_shape=jax.ShapeDtypeStruct((seq, d_na), BF16),
        compiler_params=_params("arbitrary"),
        name="na",
    )(qkv, qkv3, qkv3, bias, g.reshape(1, d_na))


def _outproj_kernel(a_ref, b_ref, w_ref, x_ref, o_ref):
    da = a_ref.shape[-1]
    acc = jnp.dot(a_ref[...], w_ref[:da, :].astype(BF16), preferred_element_type=F32)
    acc = acc + jnp.dot(b_ref[...], w_ref[da:, :].astype(BF16), preferred_element_type=F32)
    o_ref[...] = x_ref[...] + acc


def _out_proj(y_ssm, y_na, w_out, x):
    seq, d_model = x.shape
    da, db = y_ssm.shape[1], y_na.shape[1]
    tm = min(2048, seq)
    tn = 512
    return pl.pallas_call(
        _outproj_kernel,
        grid=(seq // tm, d_model // tn),
        in_specs=[
            pl.BlockSpec((tm, da), lambda i, j: (i, 0)),
            pl.BlockSpec((tm, db), lambda i, j: (i, 0)),
            pl.BlockSpec((da + db, tn), lambda i, j: (0, j)),
            pl.BlockSpec((tm, tn), lambda i, j: (i, j)),
        ],
        out_specs=pl.BlockSpec((tm, tn), lambda i, j: (i, j)),
        out_shape=jax.ShapeDtypeStruct((seq, d_model), F32),
        compiler_params=_params("arbitrary", "arbitrary"),
        name="out_proj",
    )(y_ssm, y_na, w_out, x)


def _router_kernel(x_ref, g_ref, wt_ref, h_ref, a_ref):
    h = _rms(x_ref[...], g_ref[...])
    h_hi = h.astype(BF16)
    h_ref[...] = h_hi
    h_lo = (h - h_hi.astype(F32)).astype(BF16)
    w = wt_ref[...]
    w_hi = w.astype(BF16)
    w_lo = (w - w_hi.astype(F32)).astype(BF16)
    n_exp = w.shape[0]
    nt = (((1,), (1,)), ((), ()))
    both = lax.dot_general(jnp.concatenate([w_hi, w_lo], axis=0), h_hi, nt, preferred_element_type=F32)
    logits = both[:n_exp] + both[n_exp:] + lax.dot_general(w_hi, h_lo, nt, preferred_element_type=F32)
    m = jnp.max(logits, axis=0, keepdims=True)
    e = jnp.exp(logits - m)
    a_ref[...] = e / jnp.sum(e, axis=0, keepdims=True)


def _router(x1, g, w_router):
    seq, d_model = x1.shape
    n_exp = w_router.shape[1]
    tm = min(1024, seq)
    return pl.pallas_call(
        _router_kernel,
        grid=(seq // tm,),
        in_specs=[
            pl.BlockSpec((tm, d_model), lambda i: (i, 0)),
            pl.BlockSpec((1, d_model), lambda i: (0, 0)),
            pl.BlockSpec((n_exp, d_model), lambda i: (0, 0)),
        ],
        out_specs=[
            pl.BlockSpec((tm, d_model), lambda i: (i, 0)),
            pl.BlockSpec((n_exp, tm), lambda i: (0, i)),
        ],
        out_shape=[
            jax.ShapeDtypeStruct((seq, d_model), BF16),
            jax.ShapeDtypeStruct((n_exp, seq), F32),
        ],
        compiler_params=_params("arbitrary"),
        name="router",
    )(x1, g.reshape(1, d_model), w_router.T)


def _topk_kernel(a_ref, posw_ref, gate_ref, ws_ref, nr_ref, *, cap, blk, win):
    a = a_ref[...]
    n_exp, seq = a.shape
    n_blk = seq // blk
    bits = pltpu.bitcast(a, jnp.int32)

    def bit_step(i, thr):
        cand = thr | jnp.left_shift(jnp.int32(1), 30 - i)
        cnt = jnp.sum((bits >= cand).astype(jnp.int32), axis=-1, keepdims=True)
        return jnp.where(cnt >= cap, cand, thr)

    thr = lax.fori_loop(0, 31, bit_step, jnp.zeros((n_exp, 1), jnp.int32))
    gt = bits > thr
    eq = bits == thr
    need = cap - jnp.sum(gt.astype(jnp.int32), axis=-1, keepdims=True)

    tri = (lax.broadcasted_iota(jnp.int32, (blk, blk), 0)
           <= lax.broadcasted_iota(jnp.int32, (blk, blk), 1)).astype(BF16)
    blk_of_tok = lax.broadcasted_iota(jnp.int32, (seq, n_blk), 0) // blk
    tok_to_blk = (blk_of_tok == lax.broadcasted_iota(jnp.int32, (seq, n_blk), 1)).astype(BF16)
    blk_before = (lax.broadcasted_iota(jnp.int32, (n_blk, n_blk), 0)
                  < lax.broadcasted_iota(jnp.int32, (n_blk, n_blk), 1)).astype(BF16)
    erow = lax.broadcasted_iota(jnp.int32, (2 * n_blk, seq), 0)
    ecol = lax.broadcasted_iota(jnp.int32, (2 * n_blk, seq), 1) // blk
    expand = jnp.where(erow == ecol, 32.0, jnp.where(erow - n_blk == ecol, 1.0, 0.0)).astype(BF16)

    def prefix_counts(mask):
        mb = jnp.where(mask, 1.0, 0.0).astype(BF16)
        local = jnp.concatenate(
            [jnp.dot(mb[:, b * blk:(b + 1) * blk], tri, preferred_element_type=F32) for b in range(n_blk)],
            axis=1)
        per_blk = jnp.dot(mb, tok_to_blk, preferred_element_type=F32)
        start = jnp.dot(per_blk.astype(BF16), blk_before, preferred_element_type=F32)
        hi = jnp.floor(start * (1.0 / 32.0))
        parts = jnp.concatenate([hi, start - 32.0 * hi], axis=1).astype(BF16)
        start_tok = jnp.dot(parts, expand, preferred_element_type=F32)
        return local + start_tok, start, start_tok, per_blk

    eq_incl, _, _, _ = prefix_counts(eq)
    sel = gt | (eq & (eq_incl - 1.0 < need.astype(F32)))
    incl, start, start_tok, per_blk = prefix_counts(sel)

    def window(s):
        return jnp.floor(s * (1.0 / MOE_WIN_ALIGN)) * MOE_WIN_ALIGN

    posw_ref[...] = jnp.where(sel, (incl - 1.0 - window(start_tok)).astype(jnp.int32), -1)
    gate_ref[...] = jnp.where(sel, a, 0.0)
    ws_ref[...] = window(start).astype(jnp.int32)
    span = start - window(start) + per_blk
    rounds = jnp.floor((span + float(win - 1)) * (1.0 / win))
    nr_ref[...] = jnp.max(rounds, axis=0, keepdims=True).astype(jnp.int32)


def _topk(aff_t, cap, blk, win):
    n_exp, seq = aff_t.shape
    n_blk = seq // blk
    full = lambda *_: (0, 0)
    return pl.pallas_call(
        functools.partial(_topk_kernel, cap=cap, blk=blk, win=win),
        grid=(1,),
        in_specs=[pl.BlockSpec((n_exp, seq), full)],
        out_specs=[pl.BlockSpec((n_exp, seq), full), pl.BlockSpec((n_exp, seq), full),
                   pl.BlockSpec((n_exp, n_blk), full), pl.BlockSpec((1, n_blk), full)],
        out_shape=[
            jax.ShapeDtypeStruct((n_exp, seq), jnp.int32),
            jax.ShapeDtypeStruct((n_exp, seq), F32),
            jax.ShapeDtypeStruct((n_exp, n_blk), jnp.int32),
            jax.ShapeDtypeStruct((1, n_blk), jnp.int32),
        ],
        compiler_params=_params("arbitrary"),
        name="topk",
    )(aff_t)


def _window(ws_ref, e, b, r, n_blk, win, cap):
    ws = ws_ref[e * n_blk + b] + r * win
    start = jnp.minimum(ws, cap - win)
    return pl.multiple_of(start, MOE_WIN_ALIGN), ws - start


def _gather_kernel(ws_ref, nr_ref, h_ref, rel_ref, xe_ref, *, blk, win, n_blk):
    n_exp, cap, _ = xe_ref.shape
    xe_ref[...] = jnp.zeros_like(xe_ref)
    slot = lax.broadcasted_iota(jnp.int32, (win, blk), 0)

    def one_round(b, r):
        rows = h_ref[pl.ds(pl.multiple_of(b * blk, blk), blk), :]
        rel = rel_ref[b]
        starts, hots = [], []
        for e in range(n_exp):
            start, shift = _window(ws_ref, e, b, r, n_blk, win, cap)
            relr = rel[e:e + 1, :] - r * win
            key = jnp.where(relr >= 0, relr + shift, -1)
            hots.append(jnp.where(slot == key, 1.0, 0.0).astype(BF16))
            starts.append(start)
        res = jnp.dot(jnp.concatenate(hots, axis=0), rows, preferred_element_type=F32)
        for e in range(n_exp):
            dst = pl.ds(starts[e], win)
            xe_ref[e, dst, :] = (xe_ref[e, dst, :].astype(F32) + res[e * win:(e + 1) * win]).astype(BF16)

    def extra_rounds(b):
        def body(r, carry):
            one_round(b, r)
            return carry
        lax.fori_loop(1, nr_ref[b], body, 0)

    per_trip = 4

    def blocks(trip, _):
        for h in range(per_trip):
            one_round(per_trip * trip + h, 0)
        for h in range(per_trip):
            extra_rounds(per_trip * trip + h)
        return 0

    lax.fori_loop(0, n_blk // per_trip, blocks, 0)


def _moe_gather(ws_flat, n_rounds, h2, rel3, cap, win):
    seq, d_model = h2.shape
    n_blk, n_exp, blk = rel3.shape
    dq = d_model // 4
    grid_spec = pltpu.PrefetchScalarGridSpec(
        num_scalar_prefetch=2,
        grid=(4,),
        in_specs=[
            pl.BlockSpec((seq, dq), lambda c, ws, nr: (0, c)),
            pl.BlockSpec((n_blk, n_exp, blk), lambda c, ws, nr: (0, 0, 0)),
        ],
        out_specs=pl.BlockSpec((n_exp, cap, dq), lambda c, ws, nr: (0, 0, c)),
    )
    return pl.pallas_call(
        functools.partial(_gather_kernel, blk=blk, win=win, n_blk=n_blk),
        grid_spec=grid_spec,
        out_shape=jax.ShapeDtypeStruct((n_exp, cap, d_model), BF16),
        compiler_params=_params("arbitrary"),
        name="moe_gather",
    )(ws_flat, n_rounds, h2, rel3)


def _ffn_kernel(x_ref, wg_ref, wu_ref, wd_ref, y_ref, act_ref, *, n_f):
    s = pl.program_id(1)
    tf = wg_ref.shape[-1]

    @pl.when(s < n_f)
    def _():
        x = x_ref[0]
        g = jnp.dot(x, wg_ref[0].astype(BF16), preferred_element_type=F32)
        u = jnp.dot(x, wu_ref[0].astype(BF16), preferred_element_type=F32)
        act_ref[s] = (g * (1.0 / (1.0 + jnp.exp(-g))) * u).astype(BF16)

    @pl.when(s >= n_f)
    def _():
        acc = jnp.dot(act_ref[0], wd_ref[0, 0:tf, :].astype(BF16), preferred_element_type=F32)
        for f in range(1, n_f):
            acc = acc + jnp.dot(act_ref[f], wd_ref[0, f * tf:(f + 1) * tf, :].astype(BF16),
                                preferred_element_type=F32)
        y_ref[0] = acc.astype(BF16)


def _moe_ffn(xe, w_gate, w_up, w_down):
    n_exp, cap, d_model = xe.shape
    d_ff = w_gate.shape[-1]
    tf = 512
    tn = 1024
    n_f, n_n = d_ff // tf, d_model // tn
    up_tile = lambda e, s: (e, 0, jnp.minimum(s, n_f - 1))
    down_tile = lambda e, s: (e, 0, jnp.maximum(s - n_f, 0))
    return pl.pallas_call(
        functools.partial(_ffn_kernel, n_f=n_f),
        grid=(n_exp, n_f + n_n),
        in_specs=[
            pl.BlockSpec((1, cap, d_model), lambda e, s: (e, 0, 0)),
            pl.BlockSpec((1, d_model, tf), up_tile),
            pl.BlockSpec((1, d_model, tf), up_tile),
            pl.BlockSpec((1, d_ff, tn), down_tile),
        ],
        out_specs=pl.BlockSpec((1, cap, tn), down_tile),
        out_shape=jax.ShapeDtypeStruct((n_exp, cap, d_model), BF16),
        scratch_shapes=[pltpu.VMEM((n_f, cap, tf), BF16)],
        compiler_params=_params("arbitrary", "arbitrary"),
        name="moe_ffn",
    )(xe, w_gate, w_up, w_down)


def _combine_kernel(ws_ref, nr_ref, ye_ref, x_ref, rel_ref, gate_ref, o_ref, *, blk, win, n_blk):
    n_exp, cap, _ = ye_ref.shape
    sub = x_ref.shape[0] // blk
    slot = lax.broadcasted_iota(jnp.int32, (win, blk), 0)

    def one_round(b, r):
        rel = rel_ref[b]
        gate = gate_ref[b]
        gates, wins = [], []
        for e in range(n_exp):
            start, shift = _window(ws_ref, e, b, r, n_blk, win, cap)
            relr = rel[e:e + 1, :] - r * win
            key = jnp.where(relr >= 0, relr + shift, -1)
            gates.append(jnp.where(slot == key, gate[e:e + 1, :], 0.0).astype(BF16))
            wins.append(ye_ref[e, pl.ds(start, win), :])
        return lax.dot_general(jnp.concatenate(gates, axis=0), jnp.concatenate(wins, axis=0),
                               (((0,), (0,)), ((), ())), preferred_element_type=F32)

    for s in range(sub):
        tok = slice(s * blk, (s + 1) * blk)
        o_ref[tok, :] = x_ref[tok, :] + one_round(pl.program_id(1) * sub + s, 0)
    for s in range(sub):
        tok = slice(s * blk, (s + 1) * blk)
        b = pl.program_id(1) * sub + s

        def extra(r, carry, b=b, tok=tok):
            o_ref[tok, :] += one_round(b, r)
            return carry

        lax.fori_loop(1, nr_ref[b], extra, 0)


def _moe_combine(ws_flat, n_rounds, ye, x1, rel3, gate3, win):
    seq, d_model = x1.shape
    n_exp, cap, _ = ye.shape
    n_blk, _, blk = rel3.shape
    dq = d_model // 4
    tile = min(8, n_blk) * blk
    whole = lambda c, t, ws, nr: (0, 0, 0)
    grid_spec = pltpu.PrefetchScalarGridSpec(
        num_scalar_prefetch=2,
        grid=(4, seq // tile),
        in_specs=[
            pl.BlockSpec((n_exp, cap, dq), lambda c, t, ws, nr: (0, 0, c)),
            pl.BlockSpec((tile, dq), lambda c, t, ws, nr: (t, c)),
            pl.BlockSpec(rel3.shape, whole),
            pl.BlockSpec(gate3.shape, whole),
        ],
        out_specs=pl.BlockSpec((tile, dq), lambda c, t, ws, nr: (t, c)),
    )
    return pl.pallas_call(
        functools.partial(_combine_kernel, blk=blk, win=win, n_blk=n_blk),
        grid_spec=grid_spec,
        out_shape=jax.ShapeDtypeStruct((seq, d_model), F32),
        compiler_params=_params("arbitrary", "arbitrary"),
        name="moe_combine",
    )(ws_flat, n_rounds, ye, x1, rel3, gate3)


def _final_norm_kernel(x_ref, g_ref, o_ref):
    o_ref[...] = _rms(x_ref[...], g_ref[...])


def _final_norm(x, g):
    seq, d_model = x.shape
    tm = min(1024, seq)
    return pl.pallas_call(
        _final_norm_kernel,
        grid=(seq // tm,),
        in_specs=[pl.BlockSpec((tm, d_model), lambda i: (i, 0)), pl.BlockSpec((1, d_model), lambda i: (0, 0))],
        out_specs=pl.BlockSpec((tm, d_model), lambda i: (i, 0)),
        out_shape=jax.ShapeDtypeStruct((seq, d_model), F32),
        compiler_params=_params("arbitrary"),
        name="final_norm",
    )(x, g.reshape(1, d_model))


def _layer(x, norm_mix_g, w_in, fwd, bwd, ssm_d, w_glu, b_glu, na_rpb, g_ssm_out, g_na_out, w_out,
           norm_ffn_g, w_router, w_gate, w_up, w_down):
    seq, d_model = x.shape
    d_ssm = ssm_d.shape[0]
    d_na = g_na_out.shape[0]
    n_exp = w_router.shape[1]
    cap = EC_CAPACITY_FACTOR * seq // n_exp
    blk = min(MOE_TOK_BLOCK, cap // 2)
    win = min(MOE_WIN, cap)

    u, qkv = _in_proj(x, norm_mix_g, w_in, d_ssm, d_na)
    y_ssm = _s5_post(_s5_scan(u, fwd, bwd), u, ssm_d, w_glu, b_glu, g_ssm_out)
    y_na = _neighbourhood_attention(qkv, na_rpb, g_na_out)
    x1 = _out_proj(y_ssm, y_na, w_out, x)

    h2, aff_t = _router(x1, norm_ffn_g, w_router)
    rel, gate, ws, n_rounds = _topk(aff_t, cap, blk, win)
    ws_flat = ws.reshape(-1)
    n_rounds = n_rounds.reshape(-1)
    rel3 = jnp.swapaxes(rel.reshape(n_exp, seq // blk, blk), 0, 1)
    gate3 = jnp.swapaxes(gate.reshape(n_exp, seq // blk, blk), 0, 1)
    xe = _moe_gather(ws_flat, n_rounds, h2, rel3, cap, win)
    ye = _moe_ffn(xe, w_gate, w_up, w_down)
    return _moe_combine(ws_flat, n_rounds, ye, x1, rel3, gate3, win)


def kernel(x, norm_mix_g, w_in, a_re_fwd, a_im_fwd, log_dt_fwd, b_re_fwd, b_im_fwd, c_re_fwd, c_im_fwd, a_re_bwd, a_im_bwd, log_dt_bwd, b_re_bwd, b_im_bwd, c_re_bwd, c_im_bwd, ssm_d, w_glu, b_glu, na_rpb, g_ssm_out, g_na_out, w_out, norm_ffn_g, w_router, w_gate, w_up, w_down, norm_final_g):
    bsz = x.shape[0]
    depth = w_in.shape[0]
    outs = []
    for b in range(bsz):
        xb = x[b]
        for l in range(depth):
            fwd = (a_re_fwd[l], a_im_fwd[l], log_dt_fwd[l], b_re_fwd[l], b_im_fwd[l], c_re_fwd[l], c_im_fwd[l])
            bwd = (a_re_bwd[l], a_im_bwd[l], log_dt_bwd[l], b_re_bwd[l], b_im_bwd[l], c_re_bwd[l], c_im_bwd[l])
            xb = _layer(xb, norm_mix_g[l], w_in[l], fwd, bwd, ssm_d[l], w_glu[l], b_glu[l], na_rpb[l],
                        g_ssm_out[l], g_na_out[l], w_out[l], norm_ffn_g[l], w_router[l],
                        w_gate[l], w_up[l], w_down[l])
        outs.append(_final_norm(xb, norm_final_g))
    return jnp.stack(outs)
```

```python
import functools

import numpy as np
import jax
import jax.numpy as jnp
from jax import lax
from jax.experimental import pallas as pl
from jax.experimental.pallas import tpu as pltpu

F32 = jnp.float32
BF16 = jnp.bfloat16

RMS_EPS = 1e-6
SSM_GROUP = 16
SSM_STATE = 64
NA_HEADS = 16
NA_HEAD_DIM = 64
GRID_W = 64
WIN_ROWS = 8
WIN_COLS = 16
N_EXPERTS = 16
EC_CAPACITY_FACTOR = 2

S5_CHUNK = 16
S5_SEGS = 8
HEADS_PER_DOT = 4
MOE_TOK_BLOCK = 256
MOE_WIN_ALIGN = 16
MOE_WIN = 64
MASK_NEG = -1e30
LOG2_E = float(np.log2(np.e))

VMEM_LIMIT_BYTES = 56 * 1024 * 1024


def _params(*semantics):
    return pltpu.CompilerParams(dimension_semantics=semantics, vmem_limit_bytes=VMEM_LIMIT_BYTES)


def _rms(x, g):
    ms = jnp.mean(x * x, axis=-1, keepdims=True)
    return x * lax.rsqrt(ms + RMS_EPS) * g


def _inproj_kernel(x_ref, g_ref, w_ref, u_ref, qkv_ref, h_scr, *, n_u, n_q, q_scale):
    j = pl.program_id(1)

    @pl.when(j == 0)
    def _():
        h_scr[...] = _rms(x_ref[...], g_ref[...]).astype(BF16)

    def project():
        return jnp.dot(h_scr[...], w_ref[...].astype(BF16), preferred_element_type=F32)

    @pl.when(j < n_u)
    def _():
        u_ref[...] = project()

    @pl.when(j >= n_u)
    def _():
        scale = jnp.where(j < n_u + n_q, q_scale, 1.0).astype(F32)
        qkv_ref[...] = (project() * scale).astype(BF16)


def _in_proj(x, g, w_in, d_ssm, d_na):
    seq, d_model = x.shape
    tm = min(1024, seq)
    tn = 1024
    n_u, n_q = d_ssm // tn, d_na // tn
    n_cols = w_in.shape[1] // tn
    kern = functools.partial(_inproj_kernel, n_u=n_u, n_q=n_q, q_scale=NA_HEAD_DIM ** -0.5 * LOG2_E)
    return pl.pallas_call(
        kern,
        grid=(seq // tm, n_cols),
        in_specs=[
            pl.BlockSpec((tm, d_model), lambda i, j: (i, 0)),
            pl.BlockSpec((1, d_model), lambda i, j: (0, 0)),
            pl.BlockSpec((d_model, tn), lambda i, j: (0, j)),
        ],
        out_specs=[
            pl.BlockSpec((tm, tn), lambda i, j: (i, jnp.minimum(j, n_u - 1))),
            pl.BlockSpec((tm, tn), lambda i, j: (i, jnp.maximum(j - n_u, 0))),
        ],
        out_shape=[
            jax.ShapeDtypeStruct((seq, d_ssm), F32),
            jax.ShapeDtypeStruct((seq, 3 * d_na), BF16),
        ],
        scratch_shapes=[pltpu.VMEM((tm, d_model), BF16)],
        compiler_params=_params("arbitrary", "arbitrary"),
        name="in_proj",
    )(x, g.reshape(1, d_model), w_in)


def _cmul(ar, ai, br, bi):
    return ar * br - ai * bi, ar * bi + ai * br


def _s5_tables(fwd, bwd, n_tiles):
    t_len = S5_CHUNK
    a_re, a_im, log_dt, b_re, b_im, c_re, c_im = (jnp.stack([f, b]).astype(F32) for f, b in zip(fwd, bwd))
    _, n_grp, n_st = a_re.shape
    n_ch = b_re.shape[-1]
    dt = jnp.exp(log_dt)[:, :, None]
    xr, xi = a_re * dt, a_im * dt
    steps = np.arange(t_len)

    def power(x_r, x_i, exps):
        e = jnp.asarray(exps, F32).reshape(exps.shape + (1,) * (x_r.ndim - 1))
        mag = jnp.exp(x_r[None] * e)
        return mag * jnp.cos(x_i[None] * e), mag * jnp.sin(x_i[None] * e)

    def per_step(exps):
        p_r, p_i = power(xr, xi, exps)
        return jnp.transpose(p_r, (1, 2, 0, 3)), jnp.transpose(p_i, (1, 2, 0, 3))

    a1_r, a1_i = power(xr, xi, np.ones((1, 2)))
    nr, ni = a1_r[0] - 1.0, a1_i[0]
    den = a_re * a_re + a_im * a_im
    qr, qi = (nr * a_re + ni * a_im) / den, (ni * a_re - nr * a_im) / den
    bb_r, bb_i = _cmul(qr[:, :, None, :], qi[:, :, None, :], jnp.swapaxes(b_re, 2, 3), jnp.swapaxes(b_im, 2, 3))
    am_r, am_i = per_step(np.stack([steps, steps[::-1]], axis=1))
    c_a, c_b = jnp.concatenate([c_re, c_im], axis=-1), jnp.concatenate([-c_im, c_re], axis=-1)
    a_a, a_b = jnp.concatenate([am_r, am_r], axis=-1), jnp.concatenate([am_i, am_i], axis=-1)
    y = c_a[:, :, None] * a_a[:, :, :, None, :] + c_b[:, :, None] * a_b[:, :, :, None, :]
    y = y.reshape(2, n_grp, t_len * n_ch, 2 * n_st)
    lag = jnp.einsum('dgck,dgqk->dgcq', jnp.concatenate([bb_r, -bb_i], axis=-1), y,
                     precision=lax.Precision.HIGH)
    st = per_step(np.stack([t_len - 1 - steps, steps], axis=1)) + (bb_r, bb_i)
    wo = per_step(np.stack([steps + 1, t_len - steps], axis=1)) + (c_re, c_im)
    xrf, xif = xr.reshape(2, -1), xi.reshape(2, -1)
    tiles = np.arange(n_tiles)
    ends = power(xrf, xif, np.array([[t_len, t_len], [t_len * n_tiles, t_len * n_tiles]]))
    ap = jnp.stack(ends, axis=2).reshape(8, -1)
    at_r, at_i = power(xrf, xif, t_len * np.stack([tiles, tiles[::-1]], axis=1))
    pw = jnp.transpose(jnp.stack([at_r, at_i], axis=2), (1, 2, 0, 3)).reshape(4, n_tiles, -1)
    return lag, st, wo, ap, pw


def _transpose_lane_blocks(groups, width):
    groups = [list(g) for g in groups]
    n = len(groups[0])
    first = groups[0][0]
    lanes = first.shape[-1]
    axis = first.ndim - 1
    block = lax.broadcasted_iota(jnp.int32, first.shape, axis) // width
    d = n // 2
    while d:
        upper = (block & d) != 0
        pairs = [(g, i) for g in groups for i in range(n) if i & d == 0]
        up = [pltpu.roll(g[i | d], d * width, axis) for g, i in pairs]
        down = [pltpu.roll(g[i], lanes - d * width, axis) for g, i in pairs]
        for (g, i), u, dn in zip(pairs, up, down):
            g[i], g[i | d] = jnp.where(upper, u, g[i]), jnp.where(upper, g[i | d], dn)
        d //= 2
    return groups


def _s5_a_kernel(u_ref, lag_ref, pe_r_ref, pe_i_ref, bb_r_ref, bb_i_ref, y_ref, s_ref, w_grp, e_pair, ucat,
                 *, n_tiles, seg_stride):
    n_grp = lag_ref.shape[1]
    cw = S5_CHUNK * SSM_GROUP
    n_plane, n_state = 4, pe_r_ref.shape[-1]
    lanes = u_ref.shape[-1]
    pc = n_plane * 2 * n_state

    @pl.when(pl.program_id(0) == 0)
    def _():
        e_pair[...] = jnp.zeros_like(e_pair)

    for g in range(n_grp):
        blank = jnp.zeros((SSM_GROUP, cw), F32)
        strip = jnp.concatenate([blank, lag_ref[0, g], lag_ref[1, g], blank], axis=1)
        for t in range(S5_CHUNK):
            r0 = t * SSM_GROUP
            fwd_lo = cw - t * SSM_GROUP
            bwd_lo = 2 * cw + (S5_CHUNK - 1 - t) * SSM_GROUP
            piece = strip[:, fwd_lo:fwd_lo + cw] + strip[:, bwd_lo:bwd_lo + cw]
            w_grp[g, r0:r0 + SSM_GROUP, :] = piece.astype(BF16)
            for d in range(2):
                planes = _cmul(pe_r_ref[d, g, t:t + 1, :], pe_i_ref[d, g, t:t + 1, :], bb_r_ref[d, g], bb_i_ref[d, g])
                for ri in range(2):
                    c0 = (2 * d + ri) * 2 * n_state + (g % 2) * n_state
                    rows = slice((g % 2) * cw + r0, (g % 2) * cw + r0 + SSM_GROUP)
                    e_pair[g // 2, rows, c0:c0 + n_state] = planes[ri].astype(BF16)

    per_trip = 4

    def gather(trip, _):
        groups, where = [], []
        for i in range(per_trip):
            jj = per_trip * trip + i
            pieces = []
            for t in range(S5_CHUNK):
                rows = [u_ref[pl.ds(S5_CHUNK * (2 * jj + h) + t, S5_SEGS, stride=seg_stride), :] for h in range(2)]
                pieces.append(jnp.concatenate(rows, axis=0))
            dst = pl.ds(pl.multiple_of(jj * 2 * S5_SEGS, 2 * S5_SEGS), 2 * S5_SEGS)
            for half in range(S5_CHUNK // n_grp):
                groups.append(pieces[half * n_grp:(half + 1) * n_grp])
                where.append((dst, half))
        for (dst, half), tiles in zip(where, _transpose_lane_blocks(groups, SSM_GROUP)):
            for g in range(n_grp):
                c0 = g * cw + half * lanes
                ucat[dst, c0:c0 + lanes] = tiles[g].astype(BF16)
        return 0

    lax.fori_loop(0, n_tiles // (2 * per_trip), gather, 0)
    ug = ucat[...]
    for g in range(n_grp):
        y_ref[:, g * cw:(g + 1) * cw] = jnp.dot(ug[:, g * cw:(g + 1) * cw], w_grp[g], preferred_element_type=F32)
    for q in range(n_grp // 2):
        s_ref[:, q * pc:(q + 1) * pc] = jnp.dot(ug[:, q * 2 * cw:(q + 1) * 2 * cw], e_pair[q],
                                                preferred_element_type=F32)


def _s5_bc_kernel(s_ref, yi_ref, po_r_ref, po_i_ref, c_r_ref, c_i_ref, ap_ref, pw_ref, y_ref, sin, ychunk, c_blk,
                  sin_bf, *, n_tiles, seg_stride):
    n_plane = 4
    _, n_grp, _, n_state = po_r_ref.shape
    cw = S5_CHUNK * SSM_GROUP
    n_pair = n_grp // 2
    pw_ = 2 * n_state
    pc = n_plane * pw_
    lanes = y_ref.shape[-1]

    @pl.when(pl.program_id(0) == 0)
    def _():
        c_blk[...] = jnp.zeros_like(c_blk)

    for g in range(n_grp):
        gl = g % 2
        for d in range(2):
            for t in range(S5_CHUNK):
                w_r, w_i = _cmul(c_r_ref[d, g], c_i_ref[d, g], po_r_ref[d, g, t:t + 1, :], po_i_ref[d, g, t:t + 1, :])
                rows = slice(gl * cw + t * SSM_GROUP, gl * cw + (t + 1) * SSM_GROUP)
                for ri, plane in enumerate((w_r, -w_i)):
                    c0 = (2 * d + ri) * pw_ + gl * n_state
                    c_blk[g // 2, rows, c0:c0 + n_state] = plane.astype(BF16)

    chains = [(q, d) for q in range(n_pair) for d in range(2)]

    def plane_lanes(q, d):
        base = q * pc + d * 2 * pw_
        return slice(base, base + pw_), slice(base + pw_, base + 2 * pw_)

    def coef(row, q):
        return ap_ref[row:row + 1, q * pw_:(q + 1) * pw_]

    def tile_rows(n, d):
        j = n if d == 0 else n_tiles - 1 - n
        return pl.ds(pl.multiple_of(j * S5_SEGS, S5_SEGS), S5_SEGS)

    def step(n, carry):
        out = []
        for (q, d), (zr, zi) in zip(chains, carry):
            re, im = plane_lanes(q, d)
            rows = tile_rows(n, d)
            sin[rows, re] = zr
            sin[rows, im] = zi
            ar, ai = coef(2 * d, q), coef(2 * d + 1, q)
            out.append((ar * zr - ai * zi + s_ref[rows, re], ar * zi + ai * zr + s_ref[rows, im]))
        return tuple(out)

    z0 = jnp.zeros((S5_SEGS, pw_), F32)
    ends = lax.fori_loop(0, n_tiles, step, tuple((z0, z0) for _ in chains))

    carries = []
    for (q, d), (zr, zi) in zip(chains, ends):
        sr, si = coef(4 + 2 * d, q), coef(5 + 2 * d, q)
        cr = jnp.zeros((1, pw_), F32)
        ci = jnp.zeros((1, pw_), F32)
        seg_r = [None] * S5_SEGS
        seg_i = [None] * S5_SEGS
        for s in (range(S5_SEGS) if d == 0 else range(S5_SEGS - 1, -1, -1)):
            seg_r[s], seg_i[s] = cr, ci
            cr, ci = (zr[s:s + 1] + sr * cr - si * ci, zi[s:s + 1] + sr * ci + si * cr)
        carries.append((jnp.concatenate(seg_r, axis=0), jnp.concatenate(seg_i, axis=0)))

    def fix(n8, _):
        tiles = pl.ds(pl.multiple_of(n8 * 8, 8), 8)
        for (q, d), (car_r, car_i) in zip(chains, carries):
            re, im = plane_lanes(q, d)
            pr8 = pw_ref[2 * d, tiles, q * pw_:(q + 1) * pw_]
            pi8 = pw_ref[2 * d + 1, tiles, q * pw_:(q + 1) * pw_]
            car2_r = jnp.concatenate([car_r, car_r], axis=0)
            car2_i = jnp.concatenate([car_i, car_i], axis=0)
            for r in range(0, 8, 2):
                rows = pl.ds(pl.multiple_of((n8 * 8 + r) * S5_SEGS, 2 * S5_SEGS), 2 * S5_SEGS)
                pr = jnp.concatenate([jnp.broadcast_to(pr8[r + h:r + h + 1], (S5_SEGS, pw_)) for h in range(2)], axis=0)
                pi = jnp.concatenate([jnp.broadcast_to(pi8[r + h:r + h + 1], (S5_SEGS, pw_)) for h in range(2)], axis=0)
                sin_bf[rows, re] = (sin[rows, re] + (pr * car2_r - pi * car2_i)).astype(BF16)
                sin_bf[rows, im] = (sin[rows, im] + (pr * car2_i + pi * car2_r)).astype(BF16)
        return 0

    lax.fori_loop(0, n_tiles // 8, fix, 0)

    for q in range(n_pair):
        lhs = sin_bf[:, q * pc:(q + 1) * pc]
        carried = lax.dot_general(lhs, c_blk[q], (((1,), (1,)), ((), ())), preferred_element_type=F32)
        ychunk[:, q * 2 * cw:(q + 1) * 2 * cw] = yi_ref[:, q * 2 * cw:(q + 1) * 2 * cw] + carried

    per_trip = 8

    def scatter(jj, _):
        groups, where = [], []
        for h in range(per_trip):
            j = per_trip * jj + h
            src = pl.ds(pl.multiple_of(j * S5_SEGS, S5_SEGS), S5_SEGS)
            for half in range(S5_CHUNK // n_grp):
                groups.append([ychunk[src, g * cw + half * lanes:g * cw + (half + 1) * lanes] for g in range(n_grp)])
                where.append(S5_CHUNK * j + half * n_grp)
        for t0, tiles in zip(where, _transpose_lane_blocks(groups, SSM_GROUP)):
            for k, tile in enumerate(tiles):
                y_ref[pl.ds(t0 + k, S5_SEGS, stride=seg_stride), :] = tile
        return 0

    lax.fori_loop(0, n_tiles // per_trip, scatter, 0)


def _s5_scan(u, fwd, bwd):
    seq, d_ssm = u.shape
    n_groups = d_ssm // SSM_GROUP
    n_state = SSM_STATE
    cw = S5_CHUNK * SSM_GROUP
    n_rows = seq // S5_CHUNK
    n_tiles = n_rows // S5_SEGS
    lanes = 128
    gpb = lanes // SSM_GROUP
    n_blocks = d_ssm // lanes
    bw = gpb * cw
    sw = gpb * 4 * n_state
    seg_stride = n_tiles * S5_CHUNK

    lag, st, wo, ap, pw = _s5_tables(fwd, bwd, n_tiles)

    def per_block(t):
        return pl.BlockSpec((t.shape[0], gpb) + t.shape[2:], lambda i: (0, i) + (0,) * (t.ndim - 2))

    y_intra, s_loc = pl.pallas_call(
        functools.partial(_s5_a_kernel, n_tiles=n_tiles, seg_stride=seg_stride),
        grid=(n_blocks,),
        in_specs=[
            pl.BlockSpec((seq, lanes), lambda i: (0, i)),
            per_block(lag), *[per_block(t) for t in st],
        ],
        out_specs=[
            pl.BlockSpec((n_rows, bw), lambda i: (0, i)),
            pl.BlockSpec((n_rows, sw), lambda i: (0, i)),
        ],
        out_shape=[
            jax.ShapeDtypeStruct((n_rows, n_blocks * bw), F32),
            jax.ShapeDtypeStruct((n_rows, n_blocks * sw), F32),
        ],
        scratch_shapes=[
            pltpu.VMEM((gpb, cw, cw), BF16),
            pltpu.VMEM((gpb // 2, 2 * cw, 8 * n_state), BF16),
            pltpu.VMEM((n_rows, S5_CHUNK * lanes), BF16),
        ],
        compiler_params=_params("arbitrary"),
        name="s5_a",
    )(u, lag, *st)

    pl_lanes = gpb * n_state
    return pl.pallas_call(
        functools.partial(_s5_bc_kernel, n_tiles=n_tiles, seg_stride=seg_stride),
        grid=(n_blocks,),
        in_specs=[
            pl.BlockSpec((n_rows, sw), lambda i: (0, i)),
            pl.BlockSpec((n_rows, bw), lambda i: (0, i)),
            *[per_block(t) for t in wo],
            pl.BlockSpec((8, pl_lanes), lambda i: (0, i)),
            pl.BlockSpec((4, n_tiles, pl_lanes), lambda i: (0, 0, i)),
        ],
        out_specs=pl.BlockSpec((seq, lanes), lambda i: (0, i)),
        out_shape=jax.ShapeDtypeStruct((seq, d_ssm), F32),
        scratch_shapes=[
            pltpu.VMEM((n_rows, sw), F32),
            pltpu.VMEM((n_rows, bw), F32),
            pltpu.VMEM((gpb // 2, 2 * cw, 8 * n_state), BF16),
            pltpu.VMEM((n_rows, sw), BF16),
        ],
        compiler_params=_params("arbitrary"),
        name="s5_bc",
    )(s_loc, y_intra, *wo, ap, pw)


def _s5_post_kernel(y_ref, u_ref, d_ref, w_ref, b_ref, g_ref, o_ref):
    y = y_ref[...] + d_ref[...] * u_ref[...]
    c0 = np.float32(np.sqrt(2.0 / np.pi))
    c01 = np.float32(np.sqrt(2.0 / np.pi) * 0.044715)
    half_y = 0.5 * y
    y = half_y + half_y * jnp.tanh(y * (c0 + c01 * (y * y)))
    z = jnp.dot(y.astype(BF16), w_ref[...].astype(BF16), preferred_element_type=F32) + b_ref[...]
    half_y = 0.5 * y
    o = half_y + half_y * jnp.tanh(0.5 * z)
    o_ref[...] = _rms(o, g_ref[...]).astype(BF16)


def _s5_post(y, u, d_skip, w_glu, b_glu, g):
    seq, d = y.shape
    tm = min(1024, seq)
    row = lambda i: (i, 0)
    fix = lambda i: (0, 0)
    return pl.pallas_call(
        _s5_post_kernel,
        grid=(seq // tm,),
        in_specs=[
            pl.BlockSpec((tm, d), row), pl.BlockSpec((tm, d), row), pl.BlockSpec((1, d), fix),
            pl.BlockSpec((d, d), fix), pl.BlockSpec((1, d), fix), pl.BlockSpec((1, d), fix),
        ],
        out_specs=pl.BlockSpec((tm, d), row),
        out_shape=jax.ShapeDtypeStruct((seq, d), BF16),
        compiler_params=_params("arbitrary"),
        name="s5_post",
    )(y, u, d_skip.reshape(1, d), w_glu, b_glu.reshape(1, d), g.reshape(1, d))


def _na_bias_table(rpb):
    n_heads = rpb.shape[0]
    n_quad = n_heads // HEADS_PER_DOT
    n_dy, n_dx = rpb.shape[1], rpb.shape[2]
    cols = np.arange(GRID_W)
    col_start = np.clip(cols - WIN_COLS // 2, 0, GRID_W - WIN_COLS)
    key_cols = np.arange(GRID_W)
    in_win = (key_cols[None, :] >= col_start[:, None]) & (key_cols[None, :] < col_start[:, None] + WIN_COLS)
    dx = key_cols[None, :] - cols[:, None] + (WIN_COLS - 1)
    pick_x = (dx[:, :, None] == np.arange(n_dx)).astype(np.float32)
    pick = np.zeros((2, GRID_W, 2 * GRID_W, n_dx), np.float32)
    for j in range(2):
        pick[j, :, j * GRID_W:(j + 1) * GRID_W] = pick_x
    dy = 2 * np.arange(WIN_ROWS)[None, :, None] + np.arange(2)[:, None, None] + np.arange(2)[None, None, :]
    pick_y = (dy[..., None] == np.arange(n_dy)).astype(np.float32)
    hp = lax.Precision.HIGHEST
    rows = jnp.einsum('spjy,hyx->spjhx', pick_y, rpb.astype(F32) * LOG2_E, precision=hp)
    rows = rows.reshape(2, WIN_ROWS, 2, n_quad, HEADS_PER_DOT, n_dx)
    b = jnp.einsum('spjqhx,jclx->sqphcl', rows, pick, precision=lax.Precision.HIGH)
    keep = np.concatenate([in_win, in_win], axis=1)
    b = jnp.where(keep, b, MASK_NEG)
    return b.reshape(2, n_quad, WIN_ROWS, HEADS_PER_DOT * GRID_W, 2 * GRID_W)


def _na_kernel(q_ref, k_ref, v_ref, b_ref, g_ref, o_ref, *, rows, rows_per_step):
    n_keys = WIN_ROWS * GRID_W
    d_na = k_ref.shape[-1]
    pw = HEADS_PER_DOT * NA_HEAD_DIM
    row_head = lax.broadcasted_iota(jnp.int32, (HEADS_PER_DOT * GRID_W, pw), 0) // GRID_W
    col_head = lax.broadcasted_iota(jnp.int32, (HEADS_PER_DOT * GRID_W, pw), 1) // NA_HEAD_DIM
    diag = row_head == col_head
    out_head = lax.broadcasted_iota(jnp.int32, (GRID_W, pw), 1) // NA_HEAD_DIM
    first = pl.program_id(0) * rows_per_step
    block_start = jnp.clip(first - WIN_ROWS // 2, 0, rows - k_ref.shape[0])

    def one_row(i, _):
        r = first + i
        win_start = jnp.clip(r - WIN_ROWS // 2, 0, rows - WIN_ROWS)
        dy0 = win_start - r + (WIN_ROWS - 1)
        k = k_ref[pl.ds(win_start - block_start, WIN_ROWS)].reshape(n_keys, d_na)
        v = v_ref[pl.ds(win_start - block_start, WIN_ROWS)].reshape(n_keys, d_na)
        q_rows = pl.ds(pl.multiple_of(i * GRID_W, GRID_W), GRID_W)
        outs = []
        for p in range(d_na // pw):
            sl = slice(p * pw, (p + 1) * pw)
            q4 = q_ref[q_rows, sl]
            qbd = jnp.where(diag, jnp.concatenate([q4] * HEADS_PER_DOT, axis=0), jnp.zeros((), BF16))
            s = lax.dot_general(qbd, k[:, sl], (((1,), (1,)), ((), ())), preferred_element_type=F32)
            s = s + jnp.concatenate([b_ref[dy0 % 2, p, dy0 // 2 + j] for j in range(WIN_ROWS // 2)], axis=1)
            m = jnp.max(s, axis=-1, keepdims=True)
            e = jnp.exp2(s - m)
            l = jnp.sum(e, axis=-1, keepdims=True)
            o = jnp.dot(e.astype(BF16), v[:, sl], preferred_element_type=F32) / l
            acc = jnp.zeros((GRID_W, pw), F32)
            for h in range(HEADS_PER_DOT):
                acc = acc + jnp.where(out_head == h, o[h * GRID_W:(h + 1) * GRID_W], 0.0)
            outs.append(acc)
        y = jnp.concatenate(outs, axis=1)
        o_ref[q_rows, :] = _rms(y, g_ref[...]).astype(BF16)
        return 0

    lax.fori_loop(0, rows_per_step, one_row, 0, unroll=4)


def _neighbourhood_attention(qkv, rpb, g):
    seq = qkv.shape[0]
    d_na = qkv.shape[1] // 3
    rows = seq // GRID_W
    bias = _na_bias_table(rpb)
    qkv3 = qkv.reshape(rows, GRID_W, 3 * d_na)

    rps = 8
    key_rows = 2 * WIN_ROWS
    assert rps - 1 + WIN_ROWS <= key_rows and rows % rps == 0

    def block_start(b):
        return jnp.clip(b * rps - WIN_ROWS // 2, 0, rows - key_rows)

    window = (pl.Element(key_rows), pl.Element(GRID_W), pl.Element(d_na))

    return pl.pallas_call(
        functools.partial(_na_kernel, rows=rows, rows_per_step=rps),
        grid=(rows // rps,),
        in_specs=[
            pl.BlockSpec((rps * GRID_W, d_na), lambda b: (b, 0)),
            pl.BlockSpec(window, lambda b: (block_start(b), 0, d_na)),
            pl.BlockSpec(window, lambda b: (block_start(b), 0, 2 * d_na)),
            pl.BlockSpec(bias.shape, lambda b: (0, 0, 0, 0, 0)),
            pl.BlockSpec((1, d_na), lambda b: (0, 0)),
        ],
        out_specs=pl.BlockSpec((rps * GRID_W, d_na), lambda b: (b, 0)),
        out_shape=jax.ShapeDtypeStruct((seq, d_na), BF16),
        compiler_params=_params("arbitrary"),
        name="na",
    )(qkv, qkv3, qkv3, bias, g.reshape(1, d_na))


def _outproj_kernel(a_ref, b_ref, w_ref, x_ref, o_ref):
    da = a_ref.shape[-1]
    acc = jnp.dot(a_ref[...], w_ref[:da, :].astype(BF16), preferred_element_type=F32)
    acc = acc + jnp.dot(b_ref[...], w_ref[da:, :].astype(BF16), preferred_element_type=F32)
    o_ref[...] = x_ref[...] + acc


def _out_proj(y_ssm, y_na, w_out, x):
    seq, d_model = x.shape
    da, db = y_ssm.shape[1], y_na.shape[1]
    tm = min(2048, seq)
    tn = 512
    return pl.pallas_call(
        _outproj_kernel,
        grid=(seq // tm, d_model // tn),
        in_specs=[
            pl.BlockSpec((tm, da), lambda i, j: (i, 0)),
            pl.BlockSpec((tm, db), lambda i, j: (i, 0)),
            pl.BlockSpec((da + db, tn), lambda i, j: (0, j)),
            pl.BlockSpec((tm, tn), lambda i, j: (i, j)),
        ],
        out_specs=pl.BlockSpec((tm, tn), lambda i, j: (i, j)),
        out_shape=jax.ShapeDtypeStruct((seq, d_model), F32),
        compiler_params=_params("arbitrary", "arbitrary"),
        name="out_proj",
    )(y_ssm, y_na, w_out, x)


def _router_kernel(x_ref, g_ref, wt_ref, h_ref, a_ref):
    h = _rms(x_ref[...], g_ref[...])
    h_hi = h.astype(BF16)
    h_ref[...] = h_hi
    h_lo = (h - h_hi.astype(F32)).astype(BF16)
    w = wt_ref[...]
    w_hi = w.astype(BF16)
    w_lo = (w - w_hi.astype(F32)).astype(BF16)
    n_exp = w.shape[0]
    nt = (((1,), (1,)), ((), ()))
    both = lax.dot_general(jnp.concatenate([w_hi, w_lo], axis=0), h_hi, nt, preferred_element_type=F32)
    logits = both[:n_exp] + both[n_exp:] + lax.dot_general(w_hi, h_lo, nt, preferred_element_type=F32)
    m = jnp.max(logits, axis=0, keepdims=True)
    e = jnp.exp(logits - m)
    a_ref[...] = e / jnp.sum(e, axis=0, keepdims=True)


def _router(x1, g, w_router):
    seq, d_model = x1.shape
    n_exp = w_router.shape[1]
    tm = min(1024, seq)
    return pl.pallas_call(
        _router_kernel,
        grid=(seq // tm,),
        in_specs=[
            pl.BlockSpec((tm, d_model), lambda i: (i, 0)),
            pl.BlockSpec((1, d_model), lambda i: (0, 0)),
            pl.BlockSpec((n_exp, d_model), lambda i: (0, 0)),
        ],
        out_specs=[
            pl.BlockSpec((tm, d_model), lambda i: (i, 0)),
            pl.BlockSpec((n_exp, tm), lambda i: (0, i)),
        ],
        out_shape=[
            jax.ShapeDtypeStruct((seq, d_model), BF16),
            jax.ShapeDtypeStruct((n_exp, seq), F32),
        ],
        compiler_params=_params("arbitrary"),
        name="router",
    )(x1, g.reshape(1, d_model), w_router.T)


def _topk_kernel(a_ref, posw_ref, gate_ref, ws_ref, nr_ref, *, cap, blk, win):
    a = a_ref[...]
    n_exp, seq = a.shape
    n_blk = seq // blk
    bits = pltpu.bitcast(a, jnp.int32)

    def bit_step(i, thr):
        cand = thr | jnp.left_shift(jnp.int32(1), 30 - i)
        cnt = jnp.sum((bits >= cand).astype(jnp.int32), axis=-1, keepdims=True)
        return jnp.where(cnt >= cap, cand, thr)

    thr = lax.fori_loop(0, 31, bit_step, jnp.zeros((n_exp, 1), jnp.int32))
    gt = bits > thr
    eq = bits == thr
    need = cap - jnp.sum(gt.astype(jnp.int32), axis=-1, keepdims=True)

    tri = (lax.broadcasted_iota(jnp.int32, (blk, blk), 0)
           <= lax.broadcasted_iota(jnp.int32, (blk, blk), 1)).astype(BF16)
    blk_of_tok = lax.broadcasted_iota(jnp.int32, (seq, n_blk), 0) // blk
    tok_to_blk = (blk_of_tok == lax.broadcasted_iota(jnp.int32, (seq, n_blk), 1)).astype(BF16)
    blk_before = (lax.broadcasted_iota(jnp.int32, (n_blk, n_blk), 0)
                  < lax.broadcasted_iota(jnp.int32, (n_blk, n_blk), 1)).astype(BF16)
    erow = lax.broadcasted_iota(jnp.int32, (2 * n_blk, seq), 0)
    ecol = lax.broadcasted_iota(jnp.int32, (2 * n_blk, seq), 1) // blk
    expand = jnp.where(erow == ecol, 32.0, jnp.where(erow - n_blk == ecol, 1.0, 0.0)).astype(BF16)

    def prefix_counts(mask):
        mb = jnp.where(mask, 1.0, 0.0).astype(BF16)
        local = jnp.concatenate(
            [jnp.dot(mb[:, b * blk:(b + 1) * blk], tri, preferred_element_type=F32) for b in range(n_blk)],
            axis=1)
        per_blk = jnp.dot(mb, tok_to_blk, preferred_element_type=F32)
        start = jnp.dot(per_blk.astype(BF16), blk_before, preferred_element_type=F32)
        hi = jnp.floor(start * (1.0 / 32.0))
        parts = jnp.concatenate([hi, start - 32.0 * hi], axis=1).astype(BF16)
        start_tok = jnp.dot(parts, expand, preferred_element_type=F32)
        return local + start_tok, start, start_tok, per_blk

    eq_incl, _, _, _ = prefix_counts(eq)
    sel = gt | (eq & (eq_incl - 1.0 < need.astype(F32)))
    incl, start, start_tok, per_blk = prefix_counts(sel)

    def window(s):
        return jnp.floor(s * (1.0 / MOE_WIN_ALIGN)) * MOE_WIN_ALIGN

    posw_ref[...] = jnp.where(sel, (incl - 1.0 - window(start_tok)).astype(jnp.int32), -1)
    gate_ref[...] = jnp.where(sel, a, 0.0)
    ws_ref[...] = window(start).astype(jnp.int32)
    span = start - window(start) + per_blk
    rounds = jnp.floor((span + float(win - 1)) * (1.0 / win))
    nr_ref[...] = jnp.max(rounds, axis=0, keepdims=True).astype(jnp.int32)


def _topk(aff_t, cap, blk, win):
    n_exp, seq = aff_t.shape
    n_blk = seq // blk
    full = lambda *_: (0, 0)
    return pl.pallas_call(
        functools.partial(_topk_kernel, cap=cap, blk=blk, win=win),
        grid=(1,),
        in_specs=[pl.BlockSpec((n_exp, seq), full)],
        out_specs=[pl.BlockSpec((n_exp, seq), full), pl.BlockSpec((n_exp, seq), full),
                   pl.BlockSpec((n_exp, n_blk), full), pl.BlockSpec((1, n_blk), full)],
        out_shape=[
            jax.ShapeDtypeStruct((n_exp, seq), jnp.int32),
            jax.ShapeDtypeStruct((n_exp, seq), F32),
            jax.ShapeDtypeStruct((n_exp, n_blk), jnp.int32),
            jax.ShapeDtypeStruct((1, n_blk), jnp.int32),
        ],
        compiler_params=_params("arbitrary"),
        name="topk",
    )(aff_t)


def _window(ws_ref, e, b, r, n_blk, win, cap):
    ws = ws_ref[e * n_blk + b] + r * win
    start = jnp.minimum(ws, cap - win)
    return pl.multiple_of(start, MOE_WIN_ALIGN), ws - start


def _gather_kernel(ws_ref, nr_ref, h_ref, rel_ref, xe_ref, *, blk, win, n_blk):
    n_exp, cap, _ = xe_ref.shape
    xe_ref[...] = jnp.zeros_like(xe_ref)
    slot = lax.broadcasted_iota(jnp.int32, (win, blk), 0)

    def one_round(b, r):
        rows = h_ref[pl.ds(pl.multiple_of(b * blk, blk), blk), :]
        rel = rel_ref[b]
        starts, hots = [], []
        for e in range(n_exp):
            start, shift = _window(ws_ref, e, b, r, n_blk, win, cap)
            relr = rel[e:e + 1, :] - r * win
            key = jnp.where(relr >= 0, relr + shift, -1)
            hots.append(jnp.where(slot == key, 1.0, 0.0).astype(BF16))
            starts.append(start)
        res = jnp.dot(jnp.concatenate(hots, axis=0), rows, preferred_element_type=F32)
        for e in range(n_exp):
            dst = pl.ds(starts[e], win)
            xe_ref[e, dst, :] = (xe_ref[e, dst, :].astype(F32) + res[e * win:(e + 1) * win]).astype(BF16)

    def extra_rounds(b):
        def body(r, carry):
            one_round(b, r)
            return carry
        lax.fori_loop(1, nr_ref[b], body, 0)

    per_trip = 4

    def blocks(trip, _):
        for h in range(per_trip):
            one_round(per_trip * trip + h, 0)
        for h in range(per_trip):
            extra_rounds(per_trip * trip + h)
        return 0

    lax.fori_loop(0, n_blk // per_trip, blocks, 0)


def _moe_gather(ws_flat, n_rounds, h2, rel3, cap, win):
    seq, d_model = h2.shape
    n_blk, n_exp, blk = rel3.shape
    dq = d_model // 4
    grid_spec = pltpu.PrefetchScalarGridSpec(
        num_scalar_prefetch=2,
        grid=(4,),
        in_specs=[
            pl.BlockSpec((seq, dq), lambda c, ws, nr: (0, c)),
            pl.BlockSpec((n_blk, n_exp, blk), lambda c, ws, nr: (0, 0, 0)),
        ],
        out_specs=pl.BlockSpec((n_exp, cap, dq), lambda c, ws, nr: (0, 0, c)),
    )
    return pl.pallas_call(
        functools.partial(_gather_kernel, blk=blk, win=win, n_blk=n_blk),
        grid_spec=grid_spec,
        out_shape=jax.ShapeDtypeStruct((n_exp, cap, d_model), BF16),
        compiler_params=_params("arbitrary"),
        name="moe_gather",
    )(ws_flat, n_rounds, h2, rel3)


def _ffn_kernel(x_ref, wg_ref, wu_ref, wd_ref, y_ref, act_ref, *, n_f):
    s = pl.program_id(1)
    tf = wg_ref.shape[-1]

    @pl.when(s < n_f)
    def _():
        x = x_ref[0]
        g = jnp.dot(x, wg_ref[0].astype(BF16), preferred_element_type=F32)
        u = jnp.dot(x, wu_ref[0].astype(BF16), preferred_element_type=F32)
        act_ref[s] = (g * (1.0 / (1.0 + jnp.exp(-g))) * u).astype(BF16)

    @pl.when(s >= n_f)
    def _():
        acc = jnp.dot(act_ref[0], wd_ref[0, 0:tf, :].astype(BF16), preferred_element_type=F32)
        for f in range(1, n_f):
            acc = acc + jnp.dot(act_ref[f], wd_ref[0, f * tf:(f + 1) * tf, :].astype(BF16),
                                preferred_element_type=F32)
        y_ref[0] = acc.astype(BF16)


def _moe_ffn(xe, w_gate, w_up, w_down):
    n_exp, cap, d_model = xe.shape
    d_ff = w_gate.shape[-1]
    tf = 512
    tn = 1024
    n_f, n_n = d_ff // tf, d_model // tn
    up_tile = lambda e, s: (e, 0, jnp.minimum(s, n_f - 1))
    down_tile = lambda e, s: (e, 0, jnp.maximum(s - n_f, 0))
    return pl.pallas_call(
        functools.partial(_ffn_kernel, n_f=n_f),
        grid=(n_exp, n_f + n_n),
        in_specs=[
            pl.BlockSpec((1, cap, d_model), lambda e, s: (e, 0, 0)),
            pl.BlockSpec((1, d_model, tf), up_tile),
            pl.BlockSpec((1, d_model, tf), up_tile),
            pl.BlockSpec((1, d_ff, tn), down_tile),
        ],
        out_specs=pl.BlockSpec((1, cap, tn), down_tile),
        out_shape=jax.ShapeDtypeStruct((n_exp, cap, d_model), BF16),
        scratch_shapes=[pltpu.VMEM((n_f, cap, tf), BF16)],
        compiler_params=_params("arbitrary", "arbitrary"),
        name="moe_ffn",
    )(xe, w_gate, w_up, w_down)


def _combine_kernel(ws_ref, nr_ref, ye_ref, x_ref, rel_ref, gate_ref, o_ref, *, blk, win, n_blk):
    n_exp, cap, _ = ye_ref.shape
    sub = x_ref.shape[0] // blk
    slot = lax.broadcasted_iota(jnp.int32, (win, blk), 0)

    def one_round(b, r):
        rel = rel_ref[b]
        gate = gate_ref[b]
        gates, wins = [], []
        for e in range(n_exp):
            start, shift = _window(ws_ref, e, b, r, n_blk, win, cap)
            relr = rel[e:e + 1, :] - r * win
            key = jnp.where(relr >= 0, relr + shift, -1)
            gates.append(jnp.where(slot == key, gate[e:e + 1, :], 0.0).astype(BF16))
            wins.append(ye_ref[e, pl.ds(start, win), :])
        return lax.dot_general(jnp.concatenate(gates, axis=0), jnp.concatenate(wins, axis=0),
                               (((0,), (0,)), ((), ())), preferred_element_type=F32)

    for s in range(sub):
        tok = slice(s * blk, (s + 1) * blk)
        o_ref[tok, :] = x_ref[tok, :] + one_round(pl.program_id(1) * sub + s, 0)
    for s in range(sub):
        tok = slice(s * blk, (s + 1) * blk)
        b = pl.program_id(1) * sub + s

        def extra(r, carry, b=b, tok=tok):
            o_ref[tok, :] += one_round(b, r)
            return carry

        lax.fori_loop(1, nr_ref[b], extra, 0)


def _moe_combine(ws_flat, n_rounds, ye, x1, rel3, gate3, win):
    seq, d_model = x1.shape
    n_exp, cap, _ = ye.shape
    n_blk, _, blk = rel3.shape
    dq = d_model // 4
    tile = min(8, n_blk) * blk
    whole = lambda c, t, ws, nr: (0, 0, 0)
    grid_spec = pltpu.PrefetchScalarGridSpec(
        num_scalar_prefetch=2,
        grid=(4, seq // tile),
        in_specs=[
            pl.BlockSpec((n_exp, cap, dq), lambda c, t, ws, nr: (0, 0, c)),
            pl.BlockSpec((tile, dq), lambda c, t, ws, nr: (t, c)),
            pl.BlockSpec(rel3.shape, whole),
            pl.BlockSpec(gate3.shape, whole),
        ],
        out_specs=pl.BlockSpec((tile, dq), lambda c, t, ws, nr: (t, c)),
    )
    return pl.pallas_call(
        functools.partial(_combine_kernel, blk=blk, win=win, n_blk=n_blk),
        grid_spec=grid_spec,
        out_shape=jax.ShapeDtypeStruct((seq, d_model), F32),
        compiler_params=_params("arbitrary", "arbitrary"),
        name="moe_combine",
    )(ws_flat, n_rounds, ye, x1, rel3, gate3)


def _final_norm_kernel(x_ref, g_ref, o_ref):
    o_ref[...] = _rms(x_ref[...], g_ref[...])


def _final_norm(x, g):
    seq, d_model = x.shape
    tm = min(1024, seq)
    return pl.pallas_call(
        _final_norm_kernel,
        grid=(seq // tm,),
        in_specs=[pl.BlockSpec((tm, d_model), lambda i: (i, 0)), pl.BlockSpec((1, d_model), lambda i: (0, 0))],
        out_specs=pl.BlockSpec((tm, d_model), lambda i: (i, 0)),
        out_shape=jax.ShapeDtypeStruct((seq, d_model), F32),
        compiler_params=_params("arbitrary"),
        name="final_norm",
    )(x, g.reshape(1, d_model))


def _layer(x, norm_mix_g, w_in, fwd, bwd, ssm_d, w_glu, b_glu, na_rpb, g_ssm_out, g_na_out, w_out,
           norm_ffn_g, w_router, w_gate, w_up, w_down):
    seq, d_model = x.shape
    d_ssm = ssm_d.shape[0]
    d_na = g_na_out.shape[0]
    n_exp = w_router.shape[1]
    cap = EC_CAPACITY_FACTOR * seq // n_exp
    blk = min(MOE_TOK_BLOCK, cap // 2)
    win = min(MOE_WIN, cap)

    u, qkv = _in_proj(x, norm_mix_g, w_in, d_ssm, d_na)
    y_ssm = _s5_post(_s5_scan(u, fwd, bwd), u, ssm_d, w_glu, b_glu, g_ssm_out)
    y_na = _neighbourhood_attention(qkv, na_rpb, g_na_out)
    x1 = _out_proj(y_ssm, y_na, w_out, x)

    h2, aff_t = _router(x1, norm_ffn_g, w_router)
    rel, gate, ws, n_rounds = _topk(aff_t, cap, blk, win)
    ws_flat = ws.reshape(-1)
    n_rounds = n_rounds.reshape(-1)
    rel3 = jnp.swapaxes(rel.reshape(n_exp, seq // blk, blk), 0, 1)
    gate3 = jnp.swapaxes(gate.reshape(n_exp, seq // blk, blk), 0, 1)
    xe = _moe_gather(ws_flat, n_rounds, h2, rel3, cap, win)
    ye = _moe_ffn(xe, w_gate, w_up, w_down)
    return _moe_combine(ws_flat, n_rounds, ye, x1, rel3, gate3, win)


def kernel(x, norm_mix_g, w_in, a_re_fwd, a_im_fwd, log_dt_fwd, b_re_fwd, b_im_fwd, c_re_fwd, c_im_fwd, a_re_bwd, a_im_bwd, log_dt_bwd, b_re_bwd, b_im_bwd, c_re_bwd, c_im_bwd, ssm_d, w_glu, b_glu, na_rpb, g_ssm_out, g_na_out, w_out, norm_ffn_g, w_router, w_gate, w_up, w_down, norm_final_g):
    bsz = x.shape[0]
    depth = w_in.shape[0]
    outs = []
    for b in range(bsz):
        xb = x[b]
        for l in range(depth):
            fwd = (a_re_fwd[l], a_im_fwd[l], log_dt_fwd[l], b_re_fwd[l], b_im_fwd[l], c_re_fwd[l], c_im_fwd[l])
            bwd = (a_re_bwd[l], a_im_bwd[l], log_dt_bwd[l], b_re_bwd[l], b_im_bwd[l], c_re_bwd[l], c_im_bwd[l])
            xb = _layer(xb, norm_mix_g[l], w_in[l], fwd, bwd, ssm_d[l], w_glu[l], b_glu[l], na_rpb[l],
                        g_ssm_out[l], g_na_out[l], w_out[l], norm_ffn_g[l], w_router[l],
                        w_gate[l], w_up[l], w_down[l])
        outs.append(_final_norm(xb, norm_final_g))
    return jnp.stack(outs)
```

```python
import functools

import numpy as np
import jax
import jax.numpy as jnp
from jax import lax
from jax.experimental import pallas as pl
from jax.experimental.pallas import tpu as pltpu

F32 = jnp.float32
BF16 = jnp.bfloat16

RMS_EPS = 1e-6
SSM_GROUP = 16
SSM_STATE = 64
NA_HEADS = 16
NA_HEAD_DIM = 64
GRID_W = 64
WIN_ROWS = 8
WIN_COLS = 16
N_EXPERTS = 16
EC_CAPACITY_FACTOR = 2

S5_CHUNK = 16
S5_SEGS = 8
HEADS_PER_DOT = 4
MOE_TOK_BLOCK = 256
MOE_WIN_ALIGN = 16
MOE_WIN = 64
MASK_NEG = -1e30
LOG2_E = float(np.log2(np.e))

VMEM_LIMIT_BYTES = 56 * 1024 * 1024


def _params(*semantics):
    return pltpu.CompilerParams(dimension_semantics=semantics, vmem_limit_bytes=VMEM_LIMIT_BYTES)


def _rms(x, g):
    ms = jnp.mean(x * x, axis=-1, keepdims=True)
    return x * lax.rsqrt(ms + RMS_EPS) * g


def _inproj_kernel(x_ref, g_ref, w_ref, u_ref, qkv_ref, h_scr, *, n_u, n_q, q_scale):
    j = pl.program_id(1)

    @pl.when(j == 0)
    def _():
        h_scr[...] = _rms(x_ref[...], g_ref[...]).astype(BF16)

    def project():
        return jnp.dot(h_scr[...], w_ref[...].astype(BF16), preferred_element_type=F32)

    @pl.when(j < n_u)
    def _():
        u_ref[...] = project()

    @pl.when(j >= n_u)
    def _():
        scale = jnp.where(j < n_u + n_q, q_scale, 1.0).astype(F32)
        qkv_ref[...] = (project() * scale).astype(BF16)


def _in_proj(x, g, w_in, d_ssm, d_na):
    seq, d_model = x.shape
    tm = min(1024, seq)
    tn = 1024
    n_u, n_q = d_ssm // tn, d_na // tn
    n_cols = w_in.shape[1] // tn
    kern = functools.partial(_inproj_kernel, n_u=n_u, n_q=n_q, q_scale=NA_HEAD_DIM ** -0.5 * LOG2_E)
    return pl.pallas_call(
        kern,
        grid=(seq // tm, n_cols),
        in_specs=[
            pl.BlockSpec((tm, d_model), lambda i, j: (i, 0)),
            pl.BlockSpec((1, d_model), lambda i, j: (0, 0)),
            pl.BlockSpec((d_model, tn), lambda i, j: (0, j)),
        ],
        out_specs=[
            pl.BlockSpec((tm, tn), lambda i, j: (i, jnp.minimum(j, n_u - 1))),
            pl.BlockSpec((tm, tn), lambda i, j: (i, jnp.maximum(j - n_u, 0))),
        ],
        out_shape=[
            jax.ShapeDtypeStruct((seq, d_ssm), F32),
            jax.ShapeDtypeStruct((seq, 3 * d_na), BF16),
        ],
        scratch_shapes=[pltpu.VMEM((tm, d_model), BF16)],
        compiler_params=_params("arbitrary", "arbitrary"),
        name="in_proj",
    )(x, g.reshape(1, d_model), w_in)


def _cmul(ar, ai, br, bi):
    return ar * br - ai * bi, ar * bi + ai * br


def _s5_tables(fwd, bwd, n_tiles):
    t_len = S5_CHUNK
    a_re, a_im, log_dt, b_re, b_im, c_re, c_im = (jnp.stack([f, b]).astype(F32) for f, b in zip(fwd, bwd))
    _, n_grp, n_st = a_re.shape
    n_ch = b_re.shape[-1]
    dt = jnp.exp(log_dt)[:, :, None]
    xr, xi = a_re * dt, a_im * dt
    steps = np.arange(t_len)

    def power(x_r, x_i, exps):
        e = jnp.asarray(exps, F32).reshape(exps.shape + (1,) * (x_r.ndim - 1))
        mag = jnp.exp(x_r[None] * e)
        return mag * jnp.cos(x_i[None] * e), mag * jnp.sin(x_i[None] * e)

    def per_step(exps):
        p_r, p_i = power(xr, xi, exps)
        return jnp.transpose(p_r, (1, 2, 0, 3)), jnp.transpose(p_i, (1, 2, 0, 3))

    a1_r, a1_i = power(xr, xi, np.ones((1, 2)))
    nr, ni = a1_r[0] - 1.0, a1_i[0]
    den = a_re * a_re + a_im * a_im
    qr, qi = (nr * a_re + ni * a_im) / den, (ni * a_re - nr * a_im) / den
    bb_r, bb_i = _cmul(qr[:, :, None, :], qi[:, :, None, :], jnp.swapaxes(b_re, 2, 3), jnp.swapaxes(b_im, 2, 3))
    am_r, am_i = per_step(np.stack([steps, steps[::-1]], axis=1))
    c_a, c_b = jnp.concatenate([c_re, c_im], axis=-1), jnp.concatenate([-c_im, c_re], axis=-1)
    a_a, a_b = jnp.concatenate([am_r, am_r], axis=-1), jnp.concatenate([am_i, am_i], axis=-1)
    y = c_a[:, :, None] * a_a[:, :, :, None, :] + c_b[:, :, None] * a_b[:, :, :, None, :]
    y = y.reshape(2, n_grp, t_len * n_ch, 2 * n_st)
    lag = jnp.einsum('dgck,dgqk->dgcq', jnp.concatenate([bb_r, -bb_i], axis=-1), y,
                     precision=lax.Precision.HIGH)
    st = per_step(np.stack([t_len - 1 - steps, steps], axis=1)) + (bb_r, bb_i)
    wo = per_step(np.stack([steps + 1, t_len - steps], axis=1)) + (c_re, c_im)
    xrf, xif = xr.reshape(2, -1), xi.reshape(2, -1)
    tiles = np.arange(n_tiles)
    ends = power(xrf, xif, np.array([[t_len, t_len], [t_len * n_tiles, t_len * n_tiles]]))
    ap = jnp.stack(ends, axis=2).reshape(8, -1)
    at_r, at_i = power(xrf, xif, t_len * np.stack([tiles, tiles[::-1]], axis=1))
    pw = jnp.transpose(jnp.stack([at_r, at_i], axis=2), (1, 2, 0, 3)).reshape(4, n_tiles, -1)
    return lag, st, wo, ap, pw


def _transpose_lane_blocks(groups, width):
    groups = [list(g) for g in groups]
    n = len(groups[0])
    first = groups[0][0]
    lanes = first.shape[-1]
    axis = first.ndim - 1
    block = lax.broadcasted_iota(jnp.int32, first.shape, axis) // width
    d = n // 2
    while d:
        upper = (block & d) != 0
        pairs = [(g, i) for g in groups for i in range(n) if i & d == 0]
        up = [pltpu.roll(g[i | d], d * width, axis) for g, i in pairs]
        down = [pltpu.roll(g[i], lanes - d * width, axis) for g, i in pairs]
        for (g, i), u, dn in zip(pairs, up, down):
            g[i], g[i | d] = jnp.where(upper, u, g[i]), jnp.where(upper, g[i | d], dn)
        d //= 2
    return groups


def _s5_a_kernel(u_ref, lag_ref, pe_r_ref, pe_i_ref, bb_r_ref, bb_i_ref, y_ref, s_ref, w_grp, e_pair, ucat,
                 *, n_tiles, seg_stride):
    n_grp = lag_ref.shape[1]
    cw = S5_CHUNK * SSM_GROUP
    n_plane, n_state = 4, pe_r_ref.shape[-1]
    lanes = u_ref.shape[-1]
    pc = n_plane * 2 * n_state

    @pl.when(pl.program_id(0) == 0)
    def _():
        e_pair[...] = jnp.zeros_like(e_pair)

    for g in range(n_grp):
        blank = jnp.zeros((SSM_GROUP, cw), F32)
        strip = (jnp.concatenate([lag_ref[1, g], blank], axis=1)
                 + pltpu.roll(jnp.concatenate([lag_ref[0, g], blank], axis=1), cw - SSM_GROUP, 1))
        for t in range(S5_CHUNK):
            r0 = t * SSM_GROUP
            lo = (S5_CHUNK - 1 - t) * SSM_GROUP
            w_grp[g, r0:r0 + SSM_GROUP, :] = strip[:, lo:lo + cw].astype(BF16)
            for d in range(2):
                planes = _cmul(pe_r_ref[d, g, t:t + 1, :], pe_i_ref[d, g, t:t + 1, :], bb_r_ref[d, g], bb_i_ref[d, g])
                for ri in range(2):
                    c0 = (2 * d + ri) * 2 * n_state + (g % 2) * n_state
                    rows = slice((g % 2) * cw + r0, (g % 2) * cw + r0 + SSM_GROUP)
                    e_pair[g // 2, rows, c0:c0 + n_state] = planes[ri].astype(BF16)

    per_trip = 4

    def gather(trip, _):
        groups, where = [], []
        for i in range(per_trip):
            jj = per_trip * trip + i
            pieces = []
            for t in range(S5_CHUNK):
                rows = [u_ref[pl.ds(S5_CHUNK * (2 * jj + h) + t, S5_SEGS, stride=seg_stride), :] for h in range(2)]
                pieces.append(jnp.concatenate(rows, axis=0))
            dst = pl.ds(pl.multiple_of(jj * 2 * S5_SEGS, 2 * S5_SEGS), 2 * S5_SEGS)
            for half in range(S5_CHUNK // n_grp):
                groups.append(pieces[half * n_grp:(half + 1) * n_grp])
                where.append((dst, half))
        for (dst, half), tiles in zip(where, _transpose_lane_blocks(groups, SSM_GROUP)):
            for g in range(n_grp):
                c0 = g * cw + half * lanes
                ucat[dst, c0:c0 + lanes] = tiles[g].astype(BF16)
        return 0

    lax.fori_loop(0, n_tiles // (2 * per_trip), gather, 0)
    ug = ucat[...]
    for g in range(n_grp):
        y_ref[:, g * cw:(g + 1) * cw] = jnp.dot(ug[:, g * cw:(g + 1) * cw], w_grp[g], preferred_element_type=F32)
    for q in range(n_grp // 2):
        s_ref[:, q * pc:(q + 1) * pc] = jnp.dot(ug[:, q * 2 * cw:(q + 1) * 2 * cw], e_pair[q],
                                                preferred_element_type=F32)


def _s5_bc_kernel(s_ref, yi_ref, po_r_ref, po_i_ref, c_r_ref, c_i_ref, ap_ref, pw_ref, y_ref, sin, ychunk, c_blk,
                  sin_bf, *, n_tiles, seg_stride):
    n_plane = 4
    _, n_grp, _, n_state = po_r_ref.shape
    cw = S5_CHUNK * SSM_GROUP
    n_pair = n_grp // 2
    pw_ = 2 * n_state
    pc = n_plane * pw_
    lanes = y_ref.shape[-1]

    @pl.when(pl.program_id(0) == 0)
    def _():
        c_blk[...] = jnp.zeros_like(c_blk)

    for g in range(n_grp):
        gl = g % 2
        for d in range(2):
            for t in range(S5_CHUNK):
                w_r, w_i = _cmul(c_r_ref[d, g], c_i_ref[d, g], po_r_ref[d, g, t:t + 1, :], po_i_ref[d, g, t:t + 1, :])
                rows = slice(gl * cw + t * SSM_GROUP, gl * cw + (t + 1) * SSM_GROUP)
                for ri, plane in enumerate((w_r, -w_i)):
                    c0 = (2 * d + ri) * pw_ + gl * n_state
                    c_blk[g // 2, rows, c0:c0 + n_state] = plane.astype(BF16)

    chains = [(q, d) for q in range(n_pair) for d in range(2)]

    def plane_lanes(q, d):
        base = q * pc + d * 2 * pw_
        return slice(base, base + pw_), slice(base + pw_, base + 2 * pw_)

    def coef(row, q):
        return ap_ref[row:row + 1, q * pw_:(q + 1) * pw_]

    def tile_rows(n, d):
        j = n if d == 0 else n_tiles - 1 - n
        return pl.ds(pl.multiple_of(j * S5_SEGS, S5_SEGS), S5_SEGS)

    def step(n, carry):
        out = []
        for (q, d), (zr, zi) in zip(chains, carry):
            re, im = plane_lanes(q, d)
            rows = tile_rows(n, d)
            sin[rows, re] = zr
            sin[rows, im] = zi
            ar, ai = coef(2 * d, q), coef(2 * d + 1, q)
            out.append((ar * zr - ai * zi + s_ref[rows, re], ar * zi + ai * zr + s_ref[rows, im]))
        return tuple(out)

    z0 = jnp.zeros((S5_SEGS, pw_), F32)
    ends = lax.fori_loop(0, n_tiles, step, tuple((z0, z0) for _ in chains))

    carries = []
    for (q, d), (zr, zi) in zip(chains, ends):
        sr, si = coef(4 + 2 * d, q), coef(5 + 2 * d, q)
        cr = jnp.zeros((1, pw_), F32)
        ci = jnp.zeros((1, pw_), F32)
        seg_r = [None] * S5_SEGS
        seg_i = [None] * S5_SEGS
        for s in (range(S5_SEGS) if d == 0 else range(S5_SEGS - 1, -1, -1)):
            seg_r[s], seg_i[s] = cr, ci
            cr, ci = (zr[s:s + 1] + sr * cr - si * ci, zi[s:s + 1] + sr * ci + si * cr)
        carries.append((jnp.concatenate(seg_r, axis=0), jnp.concatenate(seg_i, axis=0)))

    def fix(n8, _):
        tiles = pl.ds(pl.multiple_of(n8 * 8, 8), 8)
        for (q, d), (car_r, car_i) in zip(chains, carries):
            re, im = plane_lanes(q, d)
            pr8 = pw_ref[2 * d, tiles, q * pw_:(q + 1) * pw_]
            pi8 = pw_ref[2 * d + 1, tiles, q * pw_:(q + 1) * pw_]
            car2_r = jnp.concatenate([car_r, car_r], axis=0)
            car2_i = jnp.concatenate([car_i, car_i], axis=0)
            for r in range(0, 8, 2):
                rows = pl.ds(pl.multiple_of((n8 * 8 + r) * S5_SEGS, 2 * S5_SEGS), 2 * S5_SEGS)
                pr = jnp.concatenate([jnp.broadcast_to(pr8[r + h:r + h + 1], (S5_SEGS, pw_)) for h in range(2)], axis=0)
                pi = jnp.concatenate([jnp.broadcast_to(pi8[r + h:r + h + 1], (S5_SEGS, pw_)) for h in range(2)], axis=0)
                sin_bf[rows, re] = (sin[rows, re] + (pr * car2_r - pi * car2_i)).astype(BF16)
                sin_bf[rows, im] = (sin[rows, im] + (pr * car2_i + pi * car2_r)).astype(BF16)
        return 0

    lax.fori_loop(0, n_tiles // 8, fix, 0)

    for q in range(n_pair):
        lhs = sin_bf[:, q * pc:(q + 1) * pc]
        carried = lax.dot_general(lhs, c_blk[q], (((1,), (1,)), ((), ())), preferred_element_type=F32)
        ychunk[:, q * 2 * cw:(q + 1) * 2 * cw] = yi_ref[:, q * 2 * cw:(q + 1) * 2 * cw] + carried

    per_trip = 8

    def scatter(jj, _):
        groups, where = [], []
        for h in range(per_trip):
            j = per_trip * jj + h
            src = pl.ds(pl.multiple_of(j * S5_SEGS, S5_SEGS), S5_SEGS)
            for half in range(S5_CHUNK // n_grp):
                groups.append([ychunk[src, g * cw + half * lanes:g * cw + (half + 1) * lanes] for g in range(n_grp)])
                where.append(S5_CHUNK * j + half * n_grp)
        for t0, tiles in zip(where, _transpose_lane_blocks(groups, SSM_GROUP)):
            for k, tile in enumerate(tiles):
                y_ref[pl.ds(t0 + k, S5_SEGS, stride=seg_stride), :] = tile
        return 0

    lax.fori_loop(0, n_tiles // per_trip, scatter, 0)


def _s5_scan(u, fwd, bwd):
    seq, d_ssm = u.shape
    n_groups = d_ssm // SSM_GROUP
    n_state = SSM_STATE
    cw = S5_CHUNK * SSM_GROUP
    n_rows = seq // S5_CHUNK
    n_tiles = n_rows // S5_SEGS
    lanes = 128
    gpb = lanes // SSM_GROUP
    n_blocks = d_ssm // lanes
    bw = gpb * cw
    sw = gpb * 4 * n_state
    seg_stride = n_tiles * S5_CHUNK

    lag, st, wo, ap, pw = _s5_tables(fwd, bwd, n_tiles)

    def per_block(t):
        return pl.BlockSpec((t.shape[0], gpb) + t.shape[2:], lambda i: (0, i) + (0,) * (t.ndim - 2))

    y_intra, s_loc = pl.pallas_call(
        functools.partial(_s5_a_kernel, n_tiles=n_tiles, seg_stride=seg_stride),
        grid=(n_blocks,),
        in_specs=[
            pl.BlockSpec((seq, lanes), lambda i: (0, i)),
            per_block(lag), *[per_block(t) for t in st],
        ],
        out_specs=[
            pl.BlockSpec((n_rows, bw), lambda i: (0, i)),
            pl.BlockSpec((n_rows, sw), lambda i: (0, i)),
        ],
        out_shape=[
            jax.ShapeDtypeStruct((n_rows, n_blocks * bw), F32),
            jax.ShapeDtypeStruct((n_rows, n_blocks * sw), F32),
        ],
        scratch_shapes=[
            pltpu.VMEM((gpb, cw, cw), BF16),
            pltpu.VMEM((gpb // 2, 2 * cw, 8 * n_state), BF16),
            pltpu.VMEM((n_rows, S5_CHUNK * lanes), BF16),
        ],
        compiler_params=_params("arbitrary"),
        name="s5_a",
    )(u, lag, *st)

    pl_lanes = gpb * n_state
    return pl.pallas_call(
        functools.partial(_s5_bc_kernel, n_tiles=n_tiles, seg_stride=seg_stride),
        grid=(n_blocks,),
        in_specs=[
            pl.BlockSpec((n_rows, sw), lambda i: (0, i)),
            pl.BlockSpec((n_rows, bw), lambda i: (0, i)),
            *[per_block(t) for t in wo],
            pl.BlockSpec((8, pl_lanes), lambda i: (0, i)),
            pl.BlockSpec((4, n_tiles, pl_lanes), lambda i: (0, 0, i)),
        ],
        out_specs=pl.BlockSpec((seq, lanes), lambda i: (0, i)),
        out_shape=jax.ShapeDtypeStruct((seq, d_ssm), F32),
        scratch_shapes=[
            pltpu.VMEM((n_rows, sw), F32),
            pltpu.VMEM((n_rows, bw), F32),
            pltpu.VMEM((gpb // 2, 2 * cw, 8 * n_state), BF16),
            pltpu.VMEM((n_rows, sw), BF16),
        ],
        compiler_params=_params("arbitrary"),
        name="s5_bc",
    )(s_loc, y_intra, *wo, ap, pw)


def _s5_post_kernel(y_ref, u_ref, d_ref, w_ref, b_ref, g_ref, o_ref):
    y = y_ref[...] + d_ref[...] * u_ref[...]
    c0 = np.float32(np.sqrt(2.0 / np.pi))
    c01 = np.float32(np.sqrt(2.0 / np.pi) * 0.044715)
    half_y = 0.5 * y
    y = half_y + half_y * jnp.tanh(y * (c0 + c01 * (y * y)))
    z = jnp.dot(y.astype(BF16), w_ref[...].astype(BF16), preferred_element_type=F32) + b_ref[...]
    half_y = 0.5 * y
    o = half_y + half_y * jnp.tanh(0.5 * z)
    o_ref[...] = _rms(o, g_ref[...]).astype(BF16)


def _s5_post(y, u, d_skip, w_glu, b_glu, g):
    seq, d = y.shape
    tm = min(1024, seq)
    row = lambda i: (i, 0)
    fix = lambda i: (0, 0)
    return pl.pallas_call(
        _s5_post_kernel,
        grid=(seq // tm,),
        in_specs=[
            pl.BlockSpec((tm, d), row), pl.BlockSpec((tm, d), row), pl.BlockSpec((1, d), fix),
            pl.BlockSpec((d, d), fix), pl.BlockSpec((1, d), fix), pl.BlockSpec((1, d), fix),
        ],
        out_specs=pl.BlockSpec((tm, d), row),
        out_shape=jax.ShapeDtypeStruct((seq, d), BF16),
        compiler_params=_params("arbitrary"),
        name="s5_post",
    )(y, u, d_skip.reshape(1, d), w_glu, b_glu.reshape(1, d), g.reshape(1, d))


def _na_bias_table(rpb):
    n_heads = rpb.shape[0]
    n_quad = n_heads // HEADS_PER_DOT
    n_dy, n_dx = rpb.shape[1], rpb.shape[2]
    cols = np.arange(GRID_W)
    col_start = np.clip(cols - WIN_COLS // 2, 0, GRID_W - WIN_COLS)
    key_cols = np.arange(GRID_W)
    in_win = (key_cols[None, :] >= col_start[:, None]) & (key_cols[None, :] < col_start[:, None] + WIN_COLS)
    dx = key_cols[None, :] - cols[:, None] + (WIN_COLS - 1)
    pick_x = (dx[:, :, None] == np.arange(n_dx)).astype(np.float32)
    pick = np.zeros((2, GRID_W, 2 * GRID_W, n_dx), np.float32)
    for j in range(2):
        pick[j, :, j * GRID_W:(j + 1) * GRID_W] = pick_x
    dy = 2 * np.arange(WIN_ROWS)[None, :, None] + np.arange(2)[:, None, None] + np.arange(2)[None, None, :]
    pick_y = (dy[..., None] == np.arange(n_dy)).astype(np.float32)
    hp = lax.Precision.HIGHEST
    rows = jnp.einsum('spjy,hyx->spjhx', pick_y, rpb.astype(F32) * LOG2_E, precision=hp)
    rows = rows.reshape(2, WIN_ROWS, 2, n_quad, HEADS_PER_DOT, n_dx)
    b = jnp.einsum('spjqhx,jclx->sqphcl', rows, pick, precision=lax.Precision.HIGH)
    keep = np.concatenate([in_win, in_win], axis=1)
    b = jnp.where(keep, b, MASK_NEG)
    return b.reshape(2, n_quad, WIN_ROWS, HEADS_PER_DOT * GRID_W, 2 * GRID_W)


def _na_kernel(q_ref, k_ref, v_ref, b_ref, g_ref, o_ref, *, rows, rows_per_step):
    n_keys = WIN_ROWS * GRID_W
    d_na = k_ref.shape[-1]
    pw = HEADS_PER_DOT * NA_HEAD_DIM
    row_head = lax.broadcasted_iota(jnp.int32, (HEADS_PER_DOT * GRID_W, pw), 0) // GRID_W
    col_head = lax.broadcasted_iota(jnp.int32, (HEADS_PER_DOT * GRID_W, pw), 1) // NA_HEAD_DIM
    diag = row_head == col_head
    out_head = lax.broadcasted_iota(jnp.int32, (GRID_W, pw), 1) // NA_HEAD_DIM
    first = pl.program_id(0) * rows_per_step
    block_start = jnp.clip(first - WIN_ROWS // 2, 0, rows - k_ref.shape[0])

    def one_row(i, _):
        r = first + i
        win_start = jnp.clip(r - WIN_ROWS // 2, 0, rows - WIN_ROWS)
        dy0 = win_start - r + (WIN_ROWS - 1)
        k = k_ref[pl.ds(win_start - block_start, WIN_ROWS)].reshape(n_keys, d_na)
        v = v_ref[pl.ds(win_start - block_start, WIN_ROWS)].reshape(n_keys, d_na)
        q_rows = pl.ds(pl.multiple_of(i * GRID_W, GRID_W), GRID_W)
        outs = []
        for p in range(d_na // pw):
            sl = slice(p * pw, (p + 1) * pw)
            q4 = q_ref[q_rows, sl]
            qbd = jnp.where(diag, jnp.concatenate([q4] * HEADS_PER_DOT, axis=0), jnp.zeros((), BF16))
            s = lax.dot_general(qbd, k[:, sl], (((1,), (1,)), ((), ())), preferred_element_type=F32)
            s = s + jnp.concatenate([b_ref[dy0 % 2, p, dy0 // 2 + j] for j in range(WIN_ROWS // 2)], axis=1)
            m = jnp.max(s, axis=-1, keepdims=True)
            e = jnp.exp2(s - m)
            l = jnp.sum(e, axis=-1, keepdims=True)
            o = jnp.dot(e.astype(BF16), v[:, sl], preferred_element_type=F32) / l
            acc = jnp.zeros((GRID_W, pw), F32)
            for h in range(HEADS_PER_DOT):
                acc = acc + jnp.where(out_head == h, o[h * GRID_W:(h + 1) * GRID_W], 0.0)
            outs.append(acc)
        y = jnp.concatenate(outs, axis=1)
        o_ref[q_rows, :] = _rms(y, g_ref[...]).astype(BF16)
        return 0

    lax.fori_loop(0, rows_per_step, one_row, 0, unroll=4)


def _neighbourhood_attention(qkv, rpb, g):
    seq = qkv.shape[0]
    d_na = qkv.shape[1] // 3
    rows = seq // GRID_W
    bias = _na_bias_table(rpb)
    qkv3 = qkv.reshape(rows, GRID_W, 3 * d_na)

    rps = 8
    key_rows = 2 * WIN_ROWS
    assert rps - 1 + WIN_ROWS <= key_rows and rows % rps == 0

    def block_start(b):
        return jnp.clip(b * rps - WIN_ROWS // 2, 0, rows - key_rows)

    window = (pl.Element(key_rows), pl.Element(GRID_W), pl.Element(d_na))

    return pl.pallas_call(
        functools.partial(_na_kernel, rows=rows, rows_per_step=rps),
        grid=(rows // rps,),
        in_specs=[
            pl.BlockSpec((rps * GRID_W, d_na), lambda b: (b, 0)),
            pl.BlockSpec(window, lambda b: (block_start(b), 0, d_na)),
            pl.BlockSpec(window, lambda b: (block_start(b), 0, 2 * d_na)),
            pl.BlockSpec(bias.shape, lambda b: (0, 0, 0, 0, 0)),
            pl.BlockSpec((1, d_na), lambda b: (0, 0)),
        ],
        out_specs=pl.BlockSpec((rps * GRID_W, d_na), lambda b: (b, 0)),
        out_shape=jax.ShapeDtypeStruct((seq, d_na), BF16),
        compiler_params=_params("arbitrary"),
        name="na",
    )(qkv, qkv3, qkv3, bias, g.reshape(1, d_na))


def _outproj_kernel(a_ref, b_ref, w_ref, x_ref, o_ref):
    da = a_ref.shape[-1]
    acc = jnp.dot(a_ref[...], w_ref[:da, :].astype(BF16), preferred_element_type=F32)
    acc = acc + jnp.dot(b_ref[...], w_ref[da:, :].astype(BF16), preferred_element_type=F32)
    o_ref[...] = x_ref[...] + acc


def _out_proj(y_ssm, y_na, w_out, x):
    seq, d_model = x.shape
    da, db = y_ssm.shape[1], y_na.shape[1]
    tm = min(2048, seq)
    tn = 512
    return pl.pallas_call(
        _outproj_kernel,
        grid=(seq // tm, d_model // tn),
        in_specs=[
            pl.BlockSpec((tm, da), lambda i, j: (i, 0)),
            pl.BlockSpec((tm, db), lambda i, j: (i, 0)),
            pl.BlockSpec((da + db, tn), lambda i, j: (0, j)),
            pl.BlockSpec((tm, tn), lambda i, j: (i, j)),
        ],
        out_specs=pl.BlockSpec((tm, tn), lambda i, j: (i, j)),
        out_shape=jax.ShapeDtypeStruct((seq, d_model), F32),
        compiler_params=_params("arbitrary", "arbitrary"),
        name="out_proj",
    )(y_ssm, y_na, w_out, x)


def _router_kernel(x_ref, g_ref, wt_ref, h_ref, a_ref):
    h = _rms(x_ref[...], g_ref[...])
    h_hi = h.astype(BF16)
    h_ref[...] = h_hi
    h_lo = (h - h_hi.astype(F32)).astype(BF16)
    w = wt_ref[...]
    w_hi = w.astype(BF16)
    w_lo = (w - w_hi.astype(F32)).astype(BF16)
    n_exp = w.shape[0]
    nt = (((1,), (1,)), ((), ()))
    both = lax.dot_general(jnp.concatenate([w_hi, w_lo], axis=0), h_hi, nt, preferred_element_type=F32)
    logits = both[:n_exp] + both[n_exp:] + lax.dot_general(w_hi, h_lo, nt, preferred_element_type=F32)
    m = jnp.max(logits, axis=0, keepdims=True)
    e = jnp.exp(logits - m)
    a_ref[...] = e / jnp.sum(e, axis=0, keepdims=True)


def _router(x1, g, w_router):
    seq, d_model = x1.shape
    n_exp = w_router.shape[1]
    tm = min(1024, seq)
    return pl.pallas_call(
        _router_kernel,
        grid=(seq // tm,),
        in_specs=[
            pl.BlockSpec((tm, d_model), lambda i: (i, 0)),
            pl.BlockSpec((1, d_model), lambda i: (0, 0)),
            pl.BlockSpec((n_exp, d_model), lambda i: (0, 0)),
        ],
        out_specs=[
            pl.BlockSpec((tm, d_model), lambda i: (i, 0)),
            pl.BlockSpec((n_exp, tm), lambda i: (0, i)),
        ],
        out_shape=[
            jax.ShapeDtypeStruct((seq, d_model), BF16),
            jax.ShapeDtypeStruct((n_exp, seq), F32),
        ],
        compiler_params=_params("arbitrary"),
        name="router",
    )(x1, g.reshape(1, d_model), w_router.T)


def _topk_kernel(a_ref, posw_ref, gate_ref, ws_ref, nr_ref, *, cap, blk, win):
    a = a_ref[...]
    n_exp, seq = a.shape
    n_blk = seq // blk
    bits = pltpu.bitcast(a, jnp.int32)

    def bit_step(i, thr):
        cand = thr | jnp.left_shift(jnp.int32(1), 30 - i)
        cnt = jnp.sum((bits >= cand).astype(jnp.int32), axis=-1, keepdims=True)
        return jnp.where(cnt >= cap, cand, thr)

    thr = lax.fori_loop(0, 31, bit_step, jnp.zeros((n_exp, 1), jnp.int32))
    gt = bits > thr
    eq = bits == thr
    need = cap - jnp.sum(gt.astype(jnp.int32), axis=-1, keepdims=True)

    tri = (lax.broadcasted_iota(jnp.int32, (blk, blk), 0)
           <= lax.broadcasted_iota(jnp.int32, (blk, blk), 1)).astype(BF16)
    blk_of_tok = lax.broadcasted_iota(jnp.int32, (seq, n_blk), 0) // blk
    tok_to_blk = (blk_of_tok == lax.broadcasted_iota(jnp.int32, (seq, n_blk), 1)).astype(BF16)
    blk_before = (lax.broadcasted_iota(jnp.int32, (n_blk, n_blk), 0)
                  < lax.broadcasted_iota(jnp.int32, (n_blk, n_blk), 1)).astype(BF16)
    erow = lax.broadcasted_iota(jnp.int32, (2 * n_blk, seq), 0)
    ecol = lax.broadcasted_iota(jnp.int32, (2 * n_blk, seq), 1) // blk
    expand = jnp.where(erow == ecol, 32.0, jnp.where(erow - n_blk == ecol, 1.0, 0.0)).astype(BF16)

    def prefix_counts(mask):
        mb = jnp.where(mask, 1.0, 0.0).astype(BF16)
        local = jnp.concatenate(
            [jnp.dot(mb[:, b * blk:(b + 1) * blk], tri, preferred_element_type=F32) for b in range(n_blk)],
            axis=1)
        per_blk = jnp.dot(mb, tok_to_blk, preferred_element_type=F32)
        start = jnp.dot(per_blk.astype(BF16), blk_before, preferred_element_type=F32)
        hi = jnp.floor(start * (1.0 / 32.0))
        parts = jnp.concatenate([hi, start - 32.0 * hi], axis=1).astype(BF16)
        start_tok = jnp.dot(parts, expand, preferred_element_type=F32)
        return local + start_tok, start, start_tok, per_blk

    eq_incl, _, _, _ = prefix_counts(eq)
    sel = gt | (eq & (eq_incl - 1.0 < need.astype(F32)))
    incl, start, start_tok, per_blk = prefix_counts(sel)

    def window(s):
        return jnp.floor(s * (1.0 / MOE_WIN_ALIGN)) * MOE_WIN_ALIGN

    posw_ref[...] = jnp.where(sel, (incl - 1.0 - window(start_tok)).astype(jnp.int32), -1)
    gate_ref[...] = jnp.where(sel, a, 0.0)
    ws_ref[...] = window(start).astype(jnp.int32)
    span = start - window(start) + per_blk
    rounds = jnp.floor((span + float(win - 1)) * (1.0 / win))
    nr_ref[...] = jnp.max(rounds, axis=0, keepdims=True).astype(jnp.int32)


def _topk(aff_t, cap, blk, win):
    n_exp, seq = aff_t.shape
    n_blk = seq // blk
    full = lambda *_: (0, 0)
    return pl.pallas_call(
        functools.partial(_topk_kernel, cap=cap, blk=blk, win=win),
        grid=(1,),
        in_specs=[pl.BlockSpec((n_exp, seq), full)],
        out_specs=[pl.BlockSpec((n_exp, seq), full), pl.BlockSpec((n_exp, seq), full),
                   pl.BlockSpec((n_exp, n_blk), full), pl.BlockSpec((1, n_blk), full)],
        out_shape=[
            jax.ShapeDtypeStruct((n_exp, seq), jnp.int32),
            jax.ShapeDtypeStruct((n_exp, seq), F32),
            jax.ShapeDtypeStruct((n_exp, n_blk), jnp.int32),
            jax.ShapeDtypeStruct((1, n_blk), jnp.int32),
        ],
        compiler_params=_params("arbitrary"),
        name="topk",
    )(aff_t)


def _window(ws_ref, e, b, r, n_blk, win, cap):
    ws = ws_ref[e * n_blk + b] + r * win
    start = jnp.minimum(ws, cap - win)
    return pl.multiple_of(start, MOE_WIN_ALIGN), ws - start


def _gather_kernel(ws_ref, nr_ref, h_ref, rel_ref, xe_ref, *, blk, win, n_blk):
    n_exp, cap, _ = xe_ref.shape
    xe_ref[...] = jnp.zeros_like(xe_ref)
    slot = lax.broadcasted_iota(jnp.int32, (win, blk), 0)

    def one_round(b, r):
        rows = h_ref[pl.ds(pl.multiple_of(b * blk, blk), blk), :]
        rel = rel_ref[b]
        starts, hots = [], []
        for e in range(n_exp):
            start, shift = _window(ws_ref, e, b, r, n_blk, win, cap)
            relr = rel[e:e + 1, :] - r * win
            key = jnp.where(relr >= 0, relr + shift, -1)
            hots.append(jnp.where(slot == key, 1.0, 0.0).astype(BF16))
            starts.append(start)
        res = jnp.dot(jnp.concatenate(hots, axis=0), rows, preferred_element_type=F32)
        for e in range(n_exp):
            dst = pl.ds(starts[e], win)
            xe_ref[e, dst, :] = (xe_ref[e, dst, :].astype(F32) + res[e * win:(e + 1) * win]).astype(BF16)

    def extra_rounds(b):
        def body(r, carry):
            one_round(b, r)
            return carry
        lax.fori_loop(1, nr_ref[b], body, 0)

    per_trip = 4

    def blocks(trip, _):
        for h in range(per_trip):
            one_round(per_trip * trip + h, 0)
        for h in range(per_trip):
            extra_rounds(per_trip * trip + h)
        return 0

    lax.fori_loop(0, n_blk // per_trip, blocks, 0)


def _moe_gather(ws_flat, n_rounds, h2, rel3, cap, win):
    seq, d_model = h2.shape
    n_blk, n_exp, blk = rel3.shape
    dq = d_model // 4
    grid_spec = pltpu.PrefetchScalarGridSpec(
        num_scalar_prefetch=2,
        grid=(4,),
        in_specs=[
            pl.BlockSpec((seq, dq), lambda c, ws, nr: (0, c)),
            pl.BlockSpec((n_blk, n_exp, blk), lambda c, ws, nr: (0, 0, 0)),
        ],
        out_specs=pl.BlockSpec((n_exp, cap, dq), lambda c, ws, nr: (0, 0, c)),
    )
    return pl.pallas_call(
        functools.partial(_gather_kernel, blk=blk, win=win, n_blk=n_blk),
        grid_spec=grid_spec,
        out_shape=jax.ShapeDtypeStruct((n_exp, cap, d_model), BF16),
        compiler_params=_params("arbitrary"),
        name="moe_gather",
    )(ws_flat, n_rounds, h2, rel3)


def _ffn_kernel(x_ref, wg_ref, wu_ref, wd_ref, y_ref, act_ref, *, n_f):
    s = pl.program_id(1)
    tf = wg_ref.shape[-1]

    @pl.when(s < n_f)
    def _():
        x = x_ref[0]
        g = jnp.dot(x, wg_ref[0].astype(BF16), preferred_element_type=F32)
        u = jnp.dot(x, wu_ref[0].astype(BF16), preferred_element_type=F32)
        act_ref[s] = (g * (1.0 / (1.0 + jnp.exp(-g))) * u).astype(BF16)

    @pl.when(s >= n_f)
    def _():
        acc = jnp.dot(act_ref[0], wd_ref[0, 0:tf, :].astype(BF16), preferred_element_type=F32)
        for f in range(1, n_f):
            acc = acc + jnp.dot(act_ref[f], wd_ref[0, f * tf:(f + 1) * tf, :].astype(BF16),
                                preferred_element_type=F32)
        y_ref[0] = acc.astype(BF16)


def _moe_ffn(xe, w_gate, w_up, w_down):
    n_exp, cap, d_model = xe.shape
    d_ff = w_gate.shape[-1]
    tf = 512
    tn = 1024
    n_f, n_n = d_ff // tf, d_model // tn
    up_tile = lambda e, s: (e, 0, jnp.minimum(s, n_f - 1))
    down_tile = lambda e, s: (e, 0, jnp.maximum(s - n_f, 0))
    return pl.pallas_call(
        functools.partial(_ffn_kernel, n_f=n_f),
        grid=(n_exp, n_f + n_n),
        in_specs=[
            pl.BlockSpec((1, cap, d_model), lambda e, s: (e, 0, 0)),
            pl.BlockSpec((1, d_model, tf), up_tile),
            pl.BlockSpec((1, d_model, tf), up_tile),
            pl.BlockSpec((1, d_ff, tn), down_tile),
        ],
        out_specs=pl.BlockSpec((1, cap, tn), down_tile),
        out_shape=jax.ShapeDtypeStruct((n_exp, cap, d_model), BF16),
        scratch_shapes=[pltpu.VMEM((n_f, cap, tf), BF16)],
        compiler_params=_params("arbitrary", "arbitrary"),
        name="moe_ffn",
    )(xe, w_gate, w_up, w_down)


def _combine_kernel(ws_ref, nr_ref, ye_ref, x_ref, rel_ref, gate_ref, o_ref, *, blk, win, n_blk):
    n_exp, cap, _ = ye_ref.shape
    sub = x_ref.shape[0] // blk
    slot = lax.broadcasted_iota(jnp.int32, (win, blk), 0)

    def one_round(b, r):
        rel = rel_ref[b]
        gate = gate_ref[b]
        gates, wins = [], []
        for e in range(n_exp):
            start, shift = _window(ws_ref, e, b, r, n_blk, win, cap)
            relr = rel[e:e + 1, :] - r * win
            key = jnp.where(relr >= 0, relr + shift, -1)
            gates.append(jnp.where(slot == key, gate[e:e + 1, :], 0.0).astype(BF16))
            wins.append(ye_ref[e, pl.ds(start, win), :])
        return lax.dot_general(jnp.concatenate(gates, axis=0), jnp.concatenate(wins, axis=0),
                               (((0,), (0,)), ((), ())), preferred_element_type=F32)

    for s in range(sub):
        tok = slice(s * blk, (s + 1) * blk)
        o_ref[tok, :] = x_ref[tok, :] + one_round(pl.program_id(1) * sub + s, 0)
    for s in range(sub):
        tok = slice(s * blk, (s + 1) * blk)
        b = pl.program_id(1) * sub + s

        def extra(r, carry, b=b, tok=tok):
            o_ref[tok, :] += one_round(b, r)
            return carry

        lax.fori_loop(1, nr_ref[b], extra, 0)


def _moe_combine(ws_flat, n_rounds, ye, x1, rel3, gate3, win):
    seq, d_model = x1.shape
    n_exp, cap, _ = ye.shape
    n_blk, _, blk = rel3.shape
    dq = d_model // 4
    tile = min(8, n_blk) * blk
    whole = lambda c, t, ws, nr: (0, 0, 0)
    grid_spec = pltpu.PrefetchScalarGridSpec(
        num_scalar_prefetch=2,
        grid=(4, seq // tile),
        in_specs=[
            pl.BlockSpec((n_exp, cap, dq), lambda c, t, ws, nr: (0, 0, c)),
            pl.BlockSpec((tile, dq), lambda c, t, ws, nr: (t, c)),
            pl.BlockSpec(rel3.shape, whole),
            pl.BlockSpec(gate3.shape, whole),
        ],
        out_specs=pl.BlockSpec((tile, dq), lambda c, t, ws, nr: (t, c)),
    )
    return pl.pallas_call(
        functools.partial(_combine_kernel, blk=blk, win=win, n_blk=n_blk),
        grid_spec=grid_spec,
        out_shape=jax.ShapeDtypeStruct((seq, d_model), F32),
        compiler_params=_params("arbitrary", "arbitrary"),
        name="moe_combine",
    )(ws_flat, n_rounds, ye, x1, rel3, gate3)


def _final_norm_kernel(x_ref, g_ref, o_ref):
    o_ref[...] = _rms(x_ref[...], g_ref[...])


def _final_norm(x, g):
    seq, d_model = x.shape
    tm = min(1024, seq)
    return pl.pallas_call(
        _final_norm_kernel,
        grid=(seq // tm,),
        in_specs=[pl.BlockSpec((tm, d_model), lambda i: (i, 0)), pl.BlockSpec((1, d_model), lambda i: (0, 0))],
        out_specs=pl.BlockSpec((tm, d_model), lambda i: (i, 0)),
        out_shape=jax.ShapeDtypeStruct((seq, d_model), F32),
        compiler_params=_params("arbitrary"),
        name="final_norm",
    )(x, g.reshape(1, d_model))


def _layer(x, norm_mix_g, w_in, fwd, bwd, ssm_d, w_glu, b_glu, na_rpb, g_ssm_out, g_na_out, w_out,
           norm_ffn_g, w_router, w_gate, w_up, w_down):
    seq, d_model = x.shape
    d_ssm = ssm_d.shape[0]
    d_na = g_na_out.shape[0]
    n_exp = w_router.shape[1]
    cap = EC_CAPACITY_FACTOR * seq // n_exp
    blk = min(MOE_TOK_BLOCK, cap // 2)
    win = min(MOE_WIN, cap)

    u, qkv = _in_proj(x, norm_mix_g, w_in, d_ssm, d_na)
    y_ssm = _s5_post(_s5_scan(u, fwd, bwd), u, ssm_d, w_glu, b_glu, g_ssm_out)
    y_na = _neighbourhood_attention(qkv, na_rpb, g_na_out)
    x1 = _out_proj(y_ssm, y_na, w_out, x)

    h2, aff_t = _router(x1, norm_ffn_g, w_router)
    rel, gate, ws, n_rounds = _topk(aff_t, cap, blk, win)
    ws_flat = ws.reshape(-1)
    n_rounds = n_rounds.reshape(-1)
    rel3 = jnp.swapaxes(rel.reshape(n_exp, seq // blk, blk), 0, 1)
    gate3 = jnp.swapaxes(gate.reshape(n_exp, seq // blk, blk), 0, 1)
    xe = _moe_gather(ws_flat, n_rounds, h2, rel3, cap, win)
    ye = _moe_ffn(xe, w_gate, w_up, w_down)
    return _moe_combine(ws_flat, n_rounds, ye, x1, rel3, gate3, win)


def kernel(x, norm_mix_g, w_in, a_re_fwd, a_im_fwd, log_dt_fwd, b_re_fwd, b_im_fwd, c_re_fwd, c_im_fwd, a_re_bwd, a_im_bwd, log_dt_bwd, b_re_bwd, b_im_bwd, c_re_bwd, c_im_bwd, ssm_d, w_glu, b_glu, na_rpb, g_ssm_out, g_na_out, w_out, norm_ffn_g, w_router, w_gate, w_up, w_down, norm_final_g):
    bsz = x.shape[0]
    depth = w_in.shape[0]
    outs = []
    for b in range(bsz):
        xb = x[b]
        for l in range(depth):
            fwd = (a_re_fwd[l], a_im_fwd[l], log_dt_fwd[l], b_re_fwd[l], b_im_fwd[l], c_re_fwd[l], c_im_fwd[l])
            bwd = (a_re_bwd[l], a_im_bwd[l], log_dt_bwd[l], b_re_bwd[l], b_im_bwd[l], c_re_bwd[l], c_im_bwd[l])
            xb = _layer(xb, norm_mix_g[l], w_in[l], fwd, bwd, ssm_d[l], w_glu[l], b_glu[l], na_rpb[l],
                        g_ssm_out[l], g_na_out[l], w_out[l], norm_ffn_g[l], w_router[l],
                        w_gate[l], w_up[l], w_down[l])
        outs.append(_final_norm(xb, norm_final_g))
    return jnp.stack(outs)
```

```python
import functools

import numpy as np
import jax
import jax.numpy as jnp
from jax import lax
from jax.experimental import pallas as pl
from jax.experimental.pallas import tpu as pltpu

F32 = jnp.float32
BF16 = jnp.bfloat16

RMS_EPS = 1e-6
SSM_GROUP = 16
SSM_STATE = 64
NA_HEADS = 16
NA_HEAD_DIM = 64
GRID_W = 64
WIN_ROWS = 8
WIN_COLS = 16
N_EXPERTS = 16
EC_CAPACITY_FACTOR = 2

S5_CHUNK = 16
S5_SEGS = 8
HEADS_PER_DOT = 4
MOE_TOK_BLOCK = 256
MOE_WIN_ALIGN = 16
MOE_WIN = 64
MASK_NEG = -1e30
LOG2_E = float(np.log2(np.e))

VMEM_LIMIT_BYTES = 56 * 1024 * 1024


def _params(*semantics):
    return pltpu.CompilerParams(dimension_semantics=semantics, vmem_limit_bytes=VMEM_LIMIT_BYTES)


def _rms(x, g):
    ms = jnp.mean(x * x, axis=-1, keepdims=True)
    return x * lax.rsqrt(ms + RMS_EPS) * g


def _inproj_kernel(x_ref, g_ref, w_ref, u_ref, qkv_ref, h_scr, *, n_u, n_q, q_scale):
    j = pl.program_id(1)

    def project(h):
        return jnp.dot(h, w_ref[...].astype(BF16), preferred_element_type=F32)

    @pl.when(j == 0)
    def _():
        h = _rms(x_ref[...], g_ref[...]).astype(BF16)
        h_scr[...] = h
        u_ref[...] = project(h)

    @pl.when((j > 0) & (j < n_u))
    def _():
        u_ref[...] = project(h_scr[...])

    @pl.when(j >= n_u)
    def _():
        scale = jnp.where(j < n_u + n_q, q_scale, 1.0).astype(F32)
        qkv_ref[...] = (project(h_scr[...]) * scale).astype(BF16)


def _in_proj(x, g, w_in, d_ssm, d_na):
    seq, d_model = x.shape
    tm = min(1024, seq)
    tn = 1024
    n_u, n_q = d_ssm // tn, d_na // tn
    n_cols = w_in.shape[1] // tn
    kern = functools.partial(_inproj_kernel, n_u=n_u, n_q=n_q, q_scale=NA_HEAD_DIM ** -0.5 * LOG2_E)
    return pl.pallas_call(
        kern,
        grid=(seq // tm, n_cols),
        in_specs=[
            pl.BlockSpec((tm, d_model), lambda i, j: (i, 0)),
            pl.BlockSpec((1, d_model), lambda i, j: (0, 0)),
            pl.BlockSpec((d_model, tn), lambda i, j: (0, j)),
        ],
        out_specs=[
            pl.BlockSpec((tm, tn), lambda i, j: (i, jnp.minimum(j, n_u - 1))),
            pl.BlockSpec((tm, tn), lambda i, j: (i, jnp.maximum(j - n_u, 0))),
        ],
        out_shape=[
            jax.ShapeDtypeStruct((seq, d_ssm), F32),
            jax.ShapeDtypeStruct((seq, 3 * d_na), BF16),
        ],
        scratch_shapes=[pltpu.VMEM((tm, d_model), BF16)],
        compiler_params=_params("arbitrary", "arbitrary"),
        name="in_proj",
    )(x, g.reshape(1, d_model), w_in)


def _cmul(ar, ai, br, bi):
    return ar * br - ai * bi, ar * bi + ai * br


def _s5_tables(fwd, bwd, n_tiles):
    t_len = S5_CHUNK
    a_re, a_im, log_dt, b_re, b_im, c_re, c_im = (jnp.stack([f, b]).astype(F32) for f, b in zip(fwd, bwd))
    _, n_grp, n_st = a_re.shape
    n_ch = b_re.shape[-1]
    dt = jnp.exp(log_dt)[:, :, None]
    xr, xi = a_re * dt, a_im * dt
    steps = np.arange(t_len)

    def power(x_r, x_i, exps):
        e = jnp.asarray(exps, F32).reshape(exps.shape + (1,) * (x_r.ndim - 1))
        mag = jnp.exp(x_r[None] * e)
        return mag * jnp.cos(x_i[None] * e), mag * jnp.sin(x_i[None] * e)

    def per_step(exps):
        p_r, p_i = power(xr, xi, exps)
        return jnp.transpose(p_r, (1, 2, 0, 3)), jnp.transpose(p_i, (1, 2, 0, 3))

    a1_r, a1_i = power(xr, xi, np.ones((1, 2)))
    nr, ni = a1_r[0] - 1.0, a1_i[0]
    den = a_re * a_re + a_im * a_im
    qr, qi = (nr * a_re + ni * a_im) / den, (ni * a_re - nr * a_im) / den
    bb_r, bb_i = _cmul(qr[:, :, None, :], qi[:, :, None, :], jnp.swapaxes(b_re, 2, 3), jnp.swapaxes(b_im, 2, 3))
    am_r, am_i = per_step(np.stack([steps, steps[::-1]], axis=1))
    y_r, y_i = _cmul(c_re[:, :, None], c_im[:, :, None], am_r[:, :, :, None, :], am_i[:, :, :, None, :])
    y = jnp.concatenate([y_r, y_i], axis=-1).reshape(2, n_grp, t_len * n_ch, 2 * n_st)
    lag = jnp.einsum('dgck,dgqk->dgcq', jnp.concatenate([bb_r, -bb_i], axis=-1), y,
                     precision=lax.Precision.HIGH)
    st = per_step(np.stack([t_len - 1 - steps, steps], axis=1)) + (bb_r, bb_i)
    wo = per_step(np.stack([steps + 1, t_len - steps], axis=1)) + (c_re, c_im)
    xrf, xif = xr.reshape(2, -1), xi.reshape(2, -1)
    tiles = np.arange(n_tiles)
    ends = power(xrf, xif, np.array([[t_len, t_len], [t_len * n_tiles, t_len * n_tiles]]))
    ap = jnp.stack(ends, axis=2).reshape(8, -1)
    at_r, at_i = power(xrf, xif, t_len * np.stack([tiles, tiles[::-1]], axis=1))
    pw = jnp.transpose(jnp.stack([at_r, at_i], axis=2), (1, 2, 0, 3)).reshape(4, n_tiles, -1)
    return lag, st, wo, ap, pw


def _transpose_lane_blocks(groups, width):
    groups = [list(g) for g in groups]
    n = len(groups[0])
    first = groups[0][0]
    lanes = first.shape[-1]
    axis = first.ndim - 1
    block = lax.broadcasted_iota(jnp.int32, first.shape, axis) // width
    d = n // 2
    while d:
        upper = (block & d) != 0
        pairs = [(g, i) for g in groups for i in range(n) if i & d == 0]
        up = [pltpu.roll(g[i | d], d * width, axis) for g, i in pairs]
        down = [pltpu.roll(g[i], lanes - d * width, axis) for g, i in pairs]
        for (g, i), u, dn in zip(pairs, up, down):
            g[i], g[i | d] = jnp.where(upper, u, g[i]), jnp.where(upper, g[i | d], dn)
        d //= 2
    return groups


def _s5_a_kernel(u_ref, lag_ref, pe_r_ref, pe_i_ref, bb_r_ref, bb_i_ref, y_ref, s_ref, w_grp, e_pair, ucat,
                 *, n_tiles, seg_stride):
    n_grp = lag_ref.shape[1]
    cw = S5_CHUNK * SSM_GROUP
    n_plane, n_state = 4, pe_r_ref.shape[-1]
    lanes = u_ref.shape[-1]
    pc = n_plane * 2 * n_state

    @pl.when(pl.program_id(0) == 0)
    def _():
        e_pair[...] = jnp.zeros_like(e_pair)

    for g in range(n_grp):
        blank = jnp.zeros((SSM_GROUP, cw), F32)
        strip = jnp.concatenate([blank, lag_ref[0, g], lag_ref[1, g], blank], axis=1)
        for t in range(S5_CHUNK):
            r0 = t * SSM_GROUP
            fwd_lo = cw - t * SSM_GROUP
            bwd_lo = 2 * cw + (S5_CHUNK - 1 - t) * SSM_GROUP
            piece = strip[:, fwd_lo:fwd_lo + cw] + strip[:, bwd_lo:bwd_lo + cw]
            w_grp[g, r0:r0 + SSM_GROUP, :] = piece.astype(BF16)
            for d in range(2):
                planes = _cmul(pe_r_ref[d, g, t:t + 1, :], pe_i_ref[d, g, t:t + 1, :], bb_r_ref[d, g], bb_i_ref[d, g])
                for ri in range(2):
                    c0 = (2 * d + ri) * 2 * n_state + (g % 2) * n_state
                    rows = slice((g % 2) * cw + r0, (g % 2) * cw + r0 + SSM_GROUP)
                    e_pair[g // 2, rows, c0:c0 + n_state] = planes[ri].astype(BF16)

    per_trip = 4

    def gather(trip, _):
        groups, where = [], []
        for i in range(per_trip):
            jj = per_trip * trip + i
            pieces = []
            for t in range(S5_CHUNK):
                rows = [u_ref[pl.ds(S5_CHUNK * (2 * jj + h) + t, S5_SEGS, stride=seg_stride), :] for h in range(2)]
                pieces.append(jnp.concatenate(rows, axis=0))
            dst = pl.ds(pl.multiple_of(jj * 2 * S5_SEGS, 2 * S5_SEGS), 2 * S5_SEGS)
            for half in range(S5_CHUNK // n_grp):
                groups.append(pieces[half * n_grp:(half + 1) * n_grp])
                where.append((dst, half))
        for (dst, half), tiles in zip(where, _transpose_lane_blocks(groups, SSM_GROUP)):
            for g in range(n_grp):
                c0 = g * cw + half * lanes
                ucat[dst, c0:c0 + lanes] = tiles[g].astype(BF16)
        return 0

    lax.fori_loop(0, n_tiles // (2 * per_trip), gather, 0)
    ug = ucat[...]
    for g in range(n_grp):
        y_ref[:, g * cw:(g + 1) * cw] = jnp.dot(ug[:, g * cw:(g + 1) * cw], w_grp[g], preferred_element_type=F32)
    for q in range(n_grp // 2):
        s_ref[:, q * pc:(q + 1) * pc] = jnp.dot(ug[:, q * 2 * cw:(q + 1) * 2 * cw], e_pair[q],
                                                preferred_element_type=F32)


def _s5_bc_kernel(s_ref, yi_ref, po_r_ref, po_i_ref, c_r_ref, c_i_ref, ap_ref, pw_ref, y_ref, sin, ychunk, c_blk,
                  sin_bf, *, n_tiles, seg_stride):
    n_plane = 4
    _, n_grp, _, n_state = po_r_ref.shape
    cw = S5_CHUNK * SSM_GROUP
    n_pair = n_grp // 2
    pw_ = 2 * n_state
    pc = n_plane * pw_
    lanes = y_ref.shape[-1]

    @pl.when(pl.program_id(0) == 0)
    def _():
        c_blk[...] = jnp.zeros_like(c_blk)

    for g in range(n_grp):
        gl = g % 2
        for d in range(2):
            for t in range(S5_CHUNK):
                w_r, w_i = _cmul(c_r_ref[d, g], c_i_ref[d, g], po_r_ref[d, g, t:t + 1, :], po_i_ref[d, g, t:t + 1, :])
                rows = slice(gl * cw + t * SSM_GROUP, gl * cw + (t + 1) * SSM_GROUP)
                for ri, plane in enumerate((w_r, -w_i)):
                    c0 = (2 * d + ri) * pw_ + gl * n_state
                    c_blk[g // 2, rows, c0:c0 + n_state] = plane.astype(BF16)

    chains = [(q, d) for q in range(n_pair) for d in range(2)]

    def plane_lanes(q, d):
        base = q * pc + d * 2 * pw_
        return slice(base, base + pw_), slice(base + pw_, base + 2 * pw_)

    def coef(row, q):
        return ap_ref[row:row + 1, q * pw_:(q + 1) * pw_]

    def tile_rows(n, d):
        j = n if d == 0 else n_tiles - 1 - n
        return pl.ds(pl.multiple_of(j * S5_SEGS, S5_SEGS), S5_SEGS)

    def step(n, carry):
        out = []
        for (q, d), (zr, zi) in zip(chains, carry):
            re, im = plane_lanes(q, d)
            rows = tile_rows(n, d)
            sin[rows, re] = zr
            sin[rows, im] = zi
            ar, ai = coef(2 * d, q), coef(2 * d + 1, q)
            out.append((ar * zr - ai * zi + s_ref[rows, re], ar * zi + ai * zr + s_ref[rows, im]))
        return tuple(out)

    z0 = jnp.zeros((S5_SEGS, pw_), F32)
    ends = lax.fori_loop(0, n_tiles, step, tuple((z0, z0) for _ in chains))

    carries = []
    for (q, d), (zr, zi) in zip(chains, ends):
        sr, si = coef(4 + 2 * d, q), coef(5 + 2 * d, q)
        cr = jnp.zeros((1, pw_), F32)
        ci = jnp.zeros((1, pw_), F32)
        seg_r = [None] * S5_SEGS
        seg_i = [None] * S5_SEGS
        for s in (range(S5_SEGS) if d == 0 else range(S5_SEGS - 1, -1, -1)):
            seg_r[s], seg_i[s] = cr, ci
            cr, ci = (zr[s:s + 1] + sr * cr - si * ci, zi[s:s + 1] + sr * ci + si * cr)
        carries.append((jnp.concatenate(seg_r, axis=0), jnp.concatenate(seg_i, axis=0)))

    def fix(n8, _):
        tiles = pl.ds(pl.multiple_of(n8 * 8, 8), 8)
        for (q, d), (car_r, car_i) in zip(chains, carries):
            re, im = plane_lanes(q, d)
            pr8 = pw_ref[2 * d, tiles, q * pw_:(q + 1) * pw_]
            pi8 = pw_ref[2 * d + 1, tiles, q * pw_:(q + 1) * pw_]
            car2_r = jnp.concatenate([car_r, car_r], axis=0)
            car2_i = jnp.concatenate([car_i, car_i], axis=0)
            for r in range(0, 8, 2):
                rows = pl.ds(pl.multiple_of((n8 * 8 + r) * S5_SEGS, 2 * S5_SEGS), 2 * S5_SEGS)
                pr = jnp.concatenate([jnp.broadcast_to(pr8[r + h:r + h + 1], (S5_SEGS, pw_)) for h in range(2)], axis=0)
                pi = jnp.concatenate([jnp.broadcast_to(pi8[r + h:r + h + 1], (S5_SEGS, pw_)) for h in range(2)], axis=0)
                sin_bf[rows, re] = (sin[rows, re] + (pr * car2_r - pi * car2_i)).astype(BF16)
                sin_bf[rows, im] = (sin[rows, im] + (pr * car2_i + pi * car2_r)).astype(BF16)
        return 0

    lax.fori_loop(0, n_tiles // 8, fix, 0)

    for q in range(n_pair):
        lhs = sin_bf[:, q * pc:(q + 1) * pc]
        carried = lax.dot_general(lhs, c_blk[q], (((1,), (1,)), ((), ())), preferred_element_type=F32)
        ychunk[:, q * 2 * cw:(q + 1) * 2 * cw] = yi_ref[:, q * 2 * cw:(q + 1) * 2 * cw] + carried

    per_trip = 8

    def scatter(jj, _):
        groups, where = [], []
        for h in range(per_trip):
            j = per_trip * jj + h
            src = pl.ds(pl.multiple_of(j * S5_SEGS, S5_SEGS), S5_SEGS)
            for half in range(S5_CHUNK // n_grp):
                groups.append([ychunk[src, g * cw + half * lanes:g * cw + (half + 1) * lanes] for g in range(n_grp)])
                where.append(S5_CHUNK * j + half * n_grp)
        for t0, tiles in zip(where, _transpose_lane_blocks(groups, SSM_GROUP)):
            for k, tile in enumerate(tiles):
                y_ref[pl.ds(t0 + k, S5_SEGS, stride=seg_stride), :] = tile
        return 0

    lax.fori_loop(0, n_tiles // per_trip, scatter, 0)


def _s5_scan(u, fwd, bwd):
    seq, d_ssm = u.shape
    n_groups = d_ssm // SSM_GROUP
    n_state = SSM_STATE
    cw = S5_CHUNK * SSM_GROUP
    n_rows = seq // S5_CHUNK
    n_tiles = n_rows // S5_SEGS
    lanes = 128
    gpb = lanes // SSM_GROUP
    n_blocks = d_ssm // lanes
    bw = gpb * cw
    sw = gpb * 4 * n_state
    seg_stride = n_tiles * S5_CHUNK

    lag, st, wo, ap, pw = _s5_tables(fwd, bwd, n_tiles)

    def per_block(t):
        return pl.BlockSpec((t.shape[0], gpb) + t.shape[2:], lambda i: (0, i) + (0,) * (t.ndim - 2))

    y_intra, s_loc = pl.pallas_call(
        functools.partial(_s5_a_kernel, n_tiles=n_tiles, seg_stride=seg_stride),
        grid=(n_blocks,),
        in_specs=[
            pl.BlockSpec((seq, lanes), lambda i: (0, i)),
            per_block(lag), *[per_block(t) for t in st],
        ],
        out_specs=[
            pl.BlockSpec((n_rows, bw), lambda i: (0, i)),
            pl.BlockSpec((n_rows, sw), lambda i: (0, i)),
        ],
        out_shape=[
            jax.ShapeDtypeStruct((n_rows, n_blocks * bw), F32),
            jax.ShapeDtypeStruct((n_rows, n_blocks * sw), F32),
        ],
        scratch_shapes=[
            pltpu.VMEM((gpb, cw, cw), BF16),
            pltpu.VMEM((gpb // 2, 2 * cw, 8 * n_state), BF16),
            pltpu.VMEM((n_rows, S5_CHUNK * lanes), BF16),
        ],
        compiler_params=_params("arbitrary"),
        name="s5_a",
    )(u, lag, *st)

    pl_lanes = gpb * n_state
    return pl.pallas_call(
        functools.partial(_s5_bc_kernel, n_tiles=n_tiles, seg_stride=seg_stride),
        grid=(n_blocks,),
        in_specs=[
            pl.BlockSpec((n_rows, sw), lambda i: (0, i)),
            pl.BlockSpec((n_rows, bw), lambda i: (0, i)),
            *[per_block(t) for t in wo],
            pl.BlockSpec((8, pl_lanes), lambda i: (0, i)),
            pl.BlockSpec((4, n_tiles, pl_lanes), lambda i: (0, 0, i)),
        ],
        out_specs=pl.BlockSpec((seq, lanes), lambda i: (0, i)),
        out_shape=jax.ShapeDtypeStruct((seq, d_ssm), F32),
        scratch_shapes=[
            pltpu.VMEM((n_rows, sw), F32),
            pltpu.VMEM((n_rows, bw), F32),
            pltpu.VMEM((gpb // 2, 2 * cw, 8 * n_state), BF16),
            pltpu.VMEM((n_rows, sw), BF16),
        ],
        compiler_params=_params("arbitrary"),
        name="s5_bc",
    )(s_loc, y_intra, *wo, ap, pw)


def _s5_post_kernel(y_ref, u_ref, d_ref, w_ref, b_ref, g_ref, o_ref):
    y = y_ref[...] + d_ref[...] * u_ref[...]
    c0 = np.float32(np.sqrt(2.0 / np.pi))
    c01 = np.float32(np.sqrt(2.0 / np.pi) * 0.044715)
    half_y = 0.5 * y
    y = half_y + half_y * jnp.tanh(y * (c0 + c01 * (y * y)))
    z = jnp.dot(y.astype(BF16), w_ref[...].astype(BF16), preferred_element_type=F32) + b_ref[...]
    half_y = 0.5 * y
    o = half_y + half_y * jnp.tanh(0.5 * z)
    o_ref[...] = _rms(o, g_ref[...]).astype(BF16)


def _s5_post(y, u, d_skip, w_glu, b_glu, g):
    seq, d = y.shape
    tm = min(1024, seq)
    row = lambda i: (i, 0)
    fix = lambda i: (0, 0)
    return pl.pallas_call(
        _s5_post_kernel,
        grid=(seq // tm,),
        in_specs=[
            pl.BlockSpec((tm, d), row), pl.BlockSpec((tm, d), row), pl.BlockSpec((1, d), fix),
            pl.BlockSpec((d, d), fix), pl.BlockSpec((1, d), fix), pl.BlockSpec((1, d), fix),
        ],
        out_specs=pl.BlockSpec((tm, d), row),
        out_shape=jax.ShapeDtypeStruct((seq, d), BF16),
        compiler_params=_params("arbitrary"),
        name="s5_post",
    )(y, u, d_skip.reshape(1, d), w_glu, b_glu.reshape(1, d), g.reshape(1, d))


def _na_bias_table(rpb):
    n_heads = rpb.shape[0]
    n_quad = n_heads // HEADS_PER_DOT
    n_dy, n_dx = rpb.shape[1], rpb.shape[2]
    cols = np.arange(GRID_W)
    col_start = np.clip(cols - WIN_COLS // 2, 0, GRID_W - WIN_COLS)
    key_cols = np.arange(GRID_W)
    in_win = (key_cols[None, :] >= col_start[:, None]) & (key_cols[None, :] < col_start[:, None] + WIN_COLS)
    dx = key_cols[None, :] - cols[:, None] + (WIN_COLS - 1)
    pick_x = (dx[:, :, None] == np.arange(n_dx)).astype(np.float32)
    pick = np.zeros((2, GRID_W, 2 * GRID_W, n_dx), np.float32)
    for j in range(2):
        pick[j, :, j * GRID_W:(j + 1) * GRID_W] = pick_x
    dy = 2 * np.arange(WIN_ROWS)[None, :, None] + np.arange(2)[:, None, None] + np.arange(2)[None, None, :]
    pick_y = (dy[..., None] == np.arange(n_dy)).astype(np.float32)
    hp = lax.Precision.HIGHEST
    rows = jnp.einsum('spjy,hyx->spjhx', pick_y, rpb.astype(F32) * LOG2_E, precision=hp)
    rows = rows.reshape(2, WIN_ROWS, 2, n_quad, HEADS_PER_DOT, n_dx)
    b = jnp.einsum('spjqhx,jclx->sqphcl', rows, pick, precision=lax.Precision.HIGH)
    keep = np.concatenate([in_win, in_win], axis=1)
    b = jnp.where(keep, b, MASK_NEG)
    return b.reshape(2, n_quad, WIN_ROWS, HEADS_PER_DOT * GRID_W, 2 * GRID_W)


def _na_kernel(q_ref, k_ref, v_ref, b_ref, g_ref, o_ref, *, rows, rows_per_step):
    n_keys = WIN_ROWS * GRID_W
    d_na = k_ref.shape[-1]
    pw = HEADS_PER_DOT * NA_HEAD_DIM
    row_head = lax.broadcasted_iota(jnp.int32, (HEADS_PER_DOT * GRID_W, pw), 0) // GRID_W
    col_head = lax.broadcasted_iota(jnp.int32, (HEADS_PER_DOT * GRID_W, pw), 1) // NA_HEAD_DIM
    diag = row_head == col_head
    out_head = lax.broadcasted_iota(jnp.int32, (GRID_W, pw), 1) // NA_HEAD_DIM
    first = pl.program_id(0) * rows_per_step
    block_start = jnp.clip(first - WIN_ROWS // 2, 0, rows - k_ref.shape[0])

    def one_row(i, _):
        r = first + i
        win_start = jnp.clip(r - WIN_ROWS // 2, 0, rows - WIN_ROWS)
        dy0 = win_start - r + (WIN_ROWS - 1)
        k = k_ref[pl.ds(win_start - block_start, WIN_ROWS)].reshape(n_keys, d_na)
        v = v_ref[pl.ds(win_start - block_start, WIN_ROWS)].reshape(n_keys, d_na)
        q_rows = pl.ds(pl.multiple_of(i * GRID_W, GRID_W), GRID_W)
        outs = []
        for p in range(d_na // pw):
            sl = slice(p * pw, (p + 1) * pw)
            q4 = q_ref[q_rows, sl]
            qbd = jnp.where(diag, jnp.concatenate([q4] * HEADS_PER_DOT, axis=0), jnp.zeros((), BF16))
            s = lax.dot_general(qbd, k[:, sl], (((1,), (1,)), ((), ())), preferred_element_type=F32)
            s = s + jnp.concatenate([b_ref[dy0 % 2, p, dy0 // 2 + j] for j in range(WIN_ROWS // 2)], axis=1)
            m = jnp.max(s, axis=-1, keepdims=True)
            e = jnp.exp2(s - m)
            l = jnp.sum(e, axis=-1, keepdims=True)
            o = jnp.dot(e.astype(BF16), v[:, sl], preferred_element_type=F32) / l
            acc = jnp.zeros((GRID_W, pw), F32)
            for h in range(HEADS_PER_DOT):
                acc = acc + jnp.where(out_head == h, o[h * GRID_W:(h + 1) * GRID_W], 0.0)
            outs.append(acc)
        y = jnp.concatenate(outs, axis=1)
        o_ref[q_rows, :] = _rms(y, g_ref[...]).astype(BF16)
        return 0

    lax.fori_loop(0, rows_per_step, one_row, 0, unroll=4)


def _neighbourhood_attention(qkv, rpb, g):
    seq = qkv.shape[0]
    d_na = qkv.shape[1] // 3
    rows = seq // GRID_W
    bias = _na_bias_table(rpb)
    qkv3 = qkv.reshape(rows, GRID_W, 3 * d_na)

    rps = 8
    key_rows = 2 * WIN_ROWS
    assert rps - 1 + WIN_ROWS <= key_rows and rows % rps == 0

    def block_start(b):
        return jnp.clip(b * rps - WIN_ROWS // 2, 0, rows - key_rows)

    window = (pl.Element(key_rows), pl.Element(GRID_W), pl.Element(d_na))

    return pl.pallas_call(
        functools.partial(_na_kernel, rows=rows, rows_per_step=rps),
        grid=(rows // rps,),
        in_specs=[
            pl.BlockSpec((rps * GRID_W, d_na), lambda b: (b, 0)),
            pl.BlockSpec(window, lambda b: (block_start(b), 0, d_na)),
            pl.BlockSpec(window, lambda b: (block_start(b), 0, 2 * d_na)),
            pl.BlockSpec(bias.shape, lambda b: (0, 0, 0, 0, 0)),
            pl.BlockSpec((1, d_na), lambda b: (0, 0)),
        ],
        out_specs=pl.BlockSpec((rps * GRID_W, d_na), lambda b: (b, 0)),
        out_shape=jax.ShapeDtypeStruct((seq, d_na), BF16),
        compiler_params=_params("arbitrary"),
        name="na",
    )(qkv, qkv3, qkv3, bias, g.reshape(1, d_na))


def _outproj_kernel(a_ref, b_ref, w_ref, x_ref, o_ref):
    da = a_ref.shape[-1]
    acc = jnp.dot(a_ref[...], w_ref[:da, :].astype(BF16), preferred_element_type=F32)
    acc = acc + jnp.dot(b_ref[...], w_ref[da:, :].astype(BF16), preferred_element_type=F32)
    o_ref[...] = x_ref[...] + acc


def _out_proj(y_ssm, y_na, w_out, x):
    seq, d_model = x.shape
    da, db = y_ssm.shape[1], y_na.shape[1]
    tm = min(2048, seq)
    tn = 512
    return pl.pallas_call(
        _outproj_kernel,
        grid=(seq // tm, d_model // tn),
        in_specs=[
            pl.BlockSpec((tm, da), lambda i, j: (i, 0)),
            pl.BlockSpec((tm, db), lambda i, j: (i, 0)),
            pl.BlockSpec((da + db, tn), lambda i, j: (0, j)),
            pl.BlockSpec((tm, tn), lambda i, j: (i, j)),
        ],
        out_specs=pl.BlockSpec((tm, tn), lambda i, j: (i, j)),
        out_shape=jax.ShapeDtypeStruct((seq, d_model), F32),
        compiler_params=_params("arbitrary", "arbitrary"),
        name="out_proj",
    )(y_ssm, y_na, w_out, x)


def _router_kernel(x_ref, g_ref, wt_ref, h_ref, a_ref):
    h = _rms(x_ref[...], g_ref[...])
    h_hi = h.astype(BF16)
    h_ref[...] = h_hi
    h_lo = (h - h_hi.astype(F32)).astype(BF16)
    w = wt_ref[...]
    w_hi = w.astype(BF16)
    w_lo = (w - w_hi.astype(F32)).astype(BF16)
    n_exp = w.shape[0]
    nt = (((1,), (1,)), ((), ()))
    both = lax.dot_general(jnp.concatenate([w_hi, w_lo], axis=0), h_hi, nt, preferred_element_type=F32)
    logits = both[:n_exp] + both[n_exp:] + lax.dot_general(w_hi, h_lo, nt, preferred_element_type=F32)
    m = jnp.max(logits, axis=0, keepdims=True)
    e = jnp.exp(logits - m)
    a_ref[...] = e / jnp.sum(e, axis=0, keepdims=True)


def _router(x1, g, w_router):
    seq, d_model = x1.shape
    n_exp = w_router.shape[1]
    tm = min(1024, seq)
    return pl.pallas_call(
        _router_kernel,
        grid=(seq // tm,),
        in_specs=[
            pl.BlockSpec((tm, d_model), lambda i: (i, 0)),
            pl.BlockSpec((1, d_model), lambda i: (0, 0)),
            pl.BlockSpec((n_exp, d_model), lambda i: (0, 0)),
        ],
        out_specs=[
            pl.BlockSpec((tm, d_model), lambda i: (i, 0)),
            pl.BlockSpec((n_exp, tm), lambda i: (0, i)),
        ],
        out_shape=[
            jax.ShapeDtypeStruct((seq, d_model), BF16),
            jax.ShapeDtypeStruct((n_exp, seq), F32),
        ],
        compiler_params=_params("arbitrary"),
        name="router",
    )(x1, g.reshape(1, d_model), w_router.T)


def _topk_kernel(a_ref, posw_ref, gate_ref, ws_ref, nr_ref, *, cap, blk, win):
    a = a_ref[...]
    n_exp, seq = a.shape
    n_blk = seq // blk
    bits = pltpu.bitcast(a, jnp.int32)

    def bit_step(i, thr):
        cand = thr | jnp.left_shift(jnp.int32(1), 30 - i)
        cnt = jnp.sum((bits >= cand).astype(jnp.int32), axis=-1, keepdims=True)
        return jnp.where(cnt >= cap, cand, thr)

    thr = lax.fori_loop(0, 31, bit_step, jnp.zeros((n_exp, 1), jnp.int32))
    gt = bits > thr
    eq = bits == thr
    need = cap - jnp.sum(gt.astype(jnp.int32), axis=-1, keepdims=True)

    tri = (lax.broadcasted_iota(jnp.int32, (blk, blk), 0)
           <= lax.broadcasted_iota(jnp.int32, (blk, blk), 1)).astype(BF16)
    blk_of_tok = lax.broadcasted_iota(jnp.int32, (seq, n_blk), 0) // blk
    tok_to_blk = (blk_of_tok == lax.broadcasted_iota(jnp.int32, (seq, n_blk), 1)).astype(BF16)
    blk_before = (lax.broadcasted_iota(jnp.int32, (n_blk, n_blk), 0)
                  < lax.broadcasted_iota(jnp.int32, (n_blk, n_blk), 1)).astype(BF16)
    erow = lax.broadcasted_iota(jnp.int32, (2 * n_blk, seq), 0)
    ecol = lax.broadcasted_iota(jnp.int32, (2 * n_blk, seq), 1) // blk
    expand = jnp.where(erow == ecol, 32.0, jnp.where(erow - n_blk == ecol, 1.0, 0.0)).astype(BF16)

    def prefix_counts(mask):
        mb = jnp.where(mask, 1.0, 0.0).astype(BF16)
        local = jnp.concatenate(
            [jnp.dot(mb[:, b * blk:(b + 1) * blk], tri, preferred_element_type=F32) for b in range(n_blk)],
            axis=1)
        per_blk = jnp.dot(mb, tok_to_blk, preferred_element_type=F32)
        start = jnp.dot(per_blk.astype(BF16), blk_before, preferred_element_type=F32)
        hi = jnp.floor(start * (1.0 / 32.0))
        parts = jnp.concatenate([hi, start - 32.0 * hi], axis=1).astype(BF16)
        start_tok = jnp.dot(parts, expand, preferred_element_type=F32)
        return local + start_tok, start, start_tok, per_blk

    eq_incl, _, _, _ = prefix_counts(eq)
    sel = gt | (eq & (eq_incl - 1.0 < need.astype(F32)))
    incl, start, start_tok, per_blk = prefix_counts(sel)

    def window(s):
        return jnp.floor(s * (1.0 / MOE_WIN_ALIGN)) * MOE_WIN_ALIGN

    posw_ref[...] = jnp.where(sel, (incl - 1.0 - window(start_tok)).astype(jnp.int32), -1)
    gate_ref[...] = jnp.where(sel, a, 0.0)
    ws_ref[...] = window(start).astype(jnp.int32)
    span = start - window(start) + per_blk
    rounds = jnp.floor((span + float(win - 1)) * (1.0 / win))
    nr_ref[...] = jnp.max(rounds, axis=0, keepdims=True).astype(jnp.int32)


def _topk(aff_t, cap, blk, win):
    n_exp, seq = aff_t.shape
    n_blk = seq // blk
    full = lambda *_: (0, 0)
    return pl.pallas_call(
        functools.partial(_topk_kernel, cap=cap, blk=blk, win=win),
        grid=(1,),
        in_specs=[pl.BlockSpec((n_exp, seq), full)],
        out_specs=[pl.BlockSpec((n_exp, seq), full), pl.BlockSpec((n_exp, seq), full),
                   pl.BlockSpec((n_exp, n_blk), full), pl.BlockSpec((1, n_blk), full)],
        out_shape=[
            jax.ShapeDtypeStruct((n_exp, seq), jnp.int32),
            jax.ShapeDtypeStruct((n_exp, seq), F32),
            jax.ShapeDtypeStruct((n_exp, n_blk), jnp.int32),
            jax.ShapeDtypeStruct((1, n_blk), jnp.int32),
        ],
        compiler_params=_params("arbitrary"),
        name="topk",
    )(aff_t)


def _window(ws_ref, e, b, r, n_blk, win, cap):
    ws = ws_ref[e * n_blk + b] + r * win
    start = jnp.minimum(ws, cap - win)
    return pl.multiple_of(start, MOE_WIN_ALIGN), ws - start


def _gather_kernel(ws_ref, nr_ref, h_ref, rel_ref, xe_ref, *, blk, win, n_blk):
    n_exp, cap, _ = xe_ref.shape
    xe_ref[...] = jnp.zeros_like(xe_ref)
    slot = lax.broadcasted_iota(jnp.int32, (win, blk), 0)

    def one_round(b, r):
        rows = h_ref[pl.ds(pl.multiple_of(b * blk, blk), blk), :]
        rel = rel_ref[b]
        starts, hots = [], []
        for e in range(n_exp):
            start, shift = _window(ws_ref, e, b, r, n_blk, win, cap)
            relr = rel[e:e + 1, :] - r * win
            key = jnp.where(relr >= 0, relr + shift, -1)
            hots.append(jnp.where(slot == key, 1.0, 0.0).astype(BF16))
            starts.append(start)
        res = jnp.dot(jnp.concatenate(hots, axis=0), rows, preferred_element_type=F32)
        for e in range(n_exp):
            dst = pl.ds(starts[e], win)
            xe_ref[e, dst, :] = (xe_ref[e, dst, :].astype(F32) + res[e * win:(e + 1) * win]).astype(BF16)

    def extra_rounds(b):
        def body(r, carry):
            one_round(b, r)
            return carry
        lax.fori_loop(1, nr_ref[b], body, 0)

    per_trip = 4

    def blocks(trip, _):
        for h in range(per_trip):
            one_round(per_trip * trip + h, 0)
        for h in range(per_trip):
            extra_rounds(per_trip * trip + h)
        return 0

    lax.fori_loop(0, n_blk // per_trip, blocks, 0)


def _moe_gather(ws_flat, n_rounds, h2, rel3, cap, win):
    seq, d_model = h2.shape
    n_blk, n_exp, blk = rel3.shape
    dq = d_model // 4
    grid_spec = pltpu.PrefetchScalarGridSpec(
        num_scalar_prefetch=2,
        grid=(4,),
        in_specs=[
            pl.BlockSpec((seq, dq), lambda c, ws, nr: (0, c)),
            pl.BlockSpec((n_blk, n_exp, blk), lambda c, ws, nr: (0, 0, 0)),
        ],
        out_specs=pl.BlockSpec((n_exp, cap, dq), lambda c, ws, nr: (0, 0, c)),
    )
    return pl.pallas_call(
        functools.partial(_gather_kernel, blk=blk, win=win, n_blk=n_blk),
        grid_spec=grid_spec,
        out_shape=jax.ShapeDtypeStruct((n_exp, cap, d_model), BF16),
        compiler_params=_params("arbitrary"),
        name="moe_gather",
    )(ws_flat, n_rounds, h2, rel3)


def _ffn_kernel(x_ref, wg_ref, wu_ref, wd_ref, y_ref, act_ref, *, n_f):
    s = pl.program_id(1)
    tf = wg_ref.shape[-1]

    @pl.when(s < n_f)
    def _():
        x = x_ref[0]
        g = jnp.dot(x, wg_ref[0].astype(BF16), preferred_element_type=F32)
        u = jnp.dot(x, wu_ref[0].astype(BF16), preferred_element_type=F32)
        act_ref[s] = (g * (1.0 / (1.0 + jnp.exp(-g))) * u).astype(BF16)

    @pl.when(s >= n_f)
    def _():
        acc = jnp.dot(act_ref[0], wd_ref[0, 0:tf, :].astype(BF16), preferred_element_type=F32)
        for f in range(1, n_f):
            acc = acc + jnp.dot(act_ref[f], wd_ref[0, f * tf:(f + 1) * tf, :].astype(BF16),
                                preferred_element_type=F32)
        y_ref[0] = acc.astype(BF16)


def _moe_ffn(xe, w_gate, w_up, w_down):
    n_exp, cap, d_model = xe.shape
    d_ff = w_gate.shape[-1]
    tf = 512
    tn = 1024
    n_f, n_n = d_ff // tf, d_model // tn
    up_tile = lambda e, s: (e, 0, jnp.minimum(s, n_f - 1))
    down_tile = lambda e, s: (e, 0, jnp.maximum(s - n_f, 0))
    return pl.pallas_call(
        functools.partial(_ffn_kernel, n_f=n_f),
        grid=(n_exp, n_f + n_n),
        in_specs=[
            pl.BlockSpec((1, cap, d_model), lambda e, s: (e, 0, 0)),
            pl.BlockSpec((1, d_model, tf), up_tile),
            pl.BlockSpec((1, d_model, tf), up_tile),
            pl.BlockSpec((1, d_ff, tn), down_tile),
        ],
        out_specs=pl.BlockSpec((1, cap, tn), down_tile),
        out_shape=jax.ShapeDtypeStruct((n_exp, cap, d_model), BF16),
        scratch_shapes=[pltpu.VMEM((n_f, cap, tf), BF16)],
        compiler_params=_params("arbitrary", "arbitrary"),
        name="moe_ffn",
    )(xe, w_gate, w_up, w_down)


def _combine_kernel(ws_ref, nr_ref, ye_ref, x_ref, rel_ref, gate_ref, o_ref, *, blk, win, n_blk):
    n_exp, cap, _ = ye_ref.shape
    sub = x_ref.shape[0] // blk
    slot = lax.broadcasted_iota(jnp.int32, (win, blk), 0)

    def one_round(b, r):
        rel = rel_ref[b]
        gate = gate_ref[b]
        gates, wins = [], []
        for e in range(n_exp):
            start, shift = _window(ws_ref, e, b, r, n_blk, win, cap)
            relr = rel[e:e + 1, :] - r * win
            key = jnp.where(relr >= 0, relr + shift, -1)
            gates.append(jnp.where(slot == key, gate[e:e + 1, :], 0.0).astype(BF16))
            wins.append(ye_ref[e, pl.ds(start, win), :])
        return lax.dot_general(jnp.concatenate(gates, axis=0), jnp.concatenate(wins, axis=0),
                               (((0,), (0,)), ((), ())), preferred_element_type=F32)

    for s in range(sub):
        tok = slice(s * blk, (s + 1) * blk)
        o_ref[tok, :] = x_ref[tok, :] + one_round(pl.program_id(1) * sub + s, 0)
    for s in range(sub):
        tok = slice(s * blk, (s + 1) * blk)
        b = pl.program_id(1) * sub + s

        def extra(r, carry, b=b, tok=tok):
            o_ref[tok, :] += one_round(b, r)
            return carry

        lax.fori_loop(1, nr_ref[b], extra, 0)


def _moe_combine(ws_flat, n_rounds, ye, x1, rel3, gate3, win):
    seq, d_model = x1.shape
    n_exp, cap, _ = ye.shape
    n_blk, _, blk = rel3.shape
    dq = d_model // 4
    tile = min(8, n_blk) * blk
    whole = lambda c, t, ws, nr: (0, 0, 0)
    grid_spec = pltpu.PrefetchScalarGridSpec(
        num_scalar_prefetch=2,
        grid=(4, seq // tile),
        in_specs=[
            pl.BlockSpec((n_exp, cap, dq), lambda c, t, ws, nr: (0, 0, c)),
            pl.BlockSpec((tile, dq), lambda c, t, ws, nr: (t, c)),
            pl.BlockSpec(rel3.shape, whole),
            pl.BlockSpec(gate3.shape, whole),
        ],
        out_specs=pl.BlockSpec((tile, dq), lambda c, t, ws, nr: (t, c)),
    )
    return pl.pallas_call(
        functools.partial(_combine_kernel, blk=blk, win=win, n_blk=n_blk),
        grid_spec=grid_spec,
        out_shape=jax.ShapeDtypeStruct((seq, d_model), F32),
        compiler_params=_params("arbitrary", "arbitrary"),
        name="moe_combine",
    )(ws_flat, n_rounds, ye, x1, rel3, gate3)


def _final_norm_kernel(x_ref, g_ref, o_ref):
    o_ref[...] = _rms(x_ref[...], g_ref[...])


def _final_norm(x, g):
    seq, d_model = x.shape
    tm = min(1024, seq)
    return pl.pallas_call(
        _final_norm_kernel,
        grid=(seq // tm,),
        in_specs=[pl.BlockSpec((tm, d_model), lambda i: (i, 0)), pl.BlockSpec((1, d_model), lambda i: (0, 0))],
        out_specs=pl.BlockSpec((tm, d_model), lambda i: (i, 0)),
        out_shape=jax.ShapeDtypeStruct((seq, d_model), F32),
        compiler_params=_params("arbitrary"),
        name="final_norm",
    )(x, g.reshape(1, d_model))


def _layer(x, norm_mix_g, w_in, fwd, bwd, ssm_d, w_glu, b_glu, na_rpb, g_ssm_out, g_na_out, w_out,
           norm_ffn_g, w_router, w_gate, w_up, w_down):
    seq, d_model = x.shape
    d_ssm = ssm_d.shape[0]
    d_na = g_na_out.shape[0]
    n_exp = w_router.shape[1]
    cap = EC_CAPACITY_FACTOR * seq // n_exp
    blk = min(MOE_TOK_BLOCK, cap // 2)
    win = min(MOE_WIN, cap)

    u, qkv = _in_proj(x, norm_mix_g, w_in, d_ssm, d_na)
    y_ssm = _s5_post(_s5_scan(u, fwd, bwd), u, ssm_d, w_glu, b_glu, g_ssm_out)
    y_na = _neighbourhood_attention(qkv, na_rpb, g_na_out)
    x1 = _out_proj(y_ssm, y_na, w_out, x)

    h2, aff_t = _router(x1, norm_ffn_g, w_router)
    rel, gate, ws, n_rounds = _topk(aff_t, cap, blk, win)
    ws_flat = ws.reshape(-1)
    n_rounds = n_rounds.reshape(-1)
    rel3 = jnp.swapaxes(rel.reshape(n_exp, seq // blk, blk), 0, 1)
    gate3 = jnp.swapaxes(gate.reshape(n_exp, seq // blk, blk), 0, 1)
    xe = _moe_gather(ws_flat, n_rounds, h2, rel3, cap, win)
    ye = _moe_ffn(xe, w_gate, w_up, w_down)
    return _moe_combine(ws_flat, n_rounds, ye, x1, rel3, gate3, win)


def kernel(x, norm_mix_g, w_in, a_re_fwd, a_im_fwd, log_dt_fwd, b_re_fwd, b_im_fwd, c_re_fwd, c_im_fwd, a_re_bwd, a_im_bwd, log_dt_bwd, b_re_bwd, b_im_bwd, c_re_bwd, c_im_bwd, ssm_d, w_glu, b_glu, na_rpb, g_ssm_out, g_na_out, w_out, norm_ffn_g, w_router, w_gate, w_up, w_down, norm_final_g):
    bsz = x.shape[0]
    depth = w_in.shape[0]
    outs = []
    for b in range(bsz):
        xb = x[b]
        for l in range(depth):
            fwd = (a_re_fwd[l], a_im_fwd[l], log_dt_fwd[l], b_re_fwd[l], b_im_fwd[l], c_re_fwd[l], c_im_fwd[l])
            bwd = (a_re_bwd[l], a_im_bwd[l], log_dt_bwd[l], b_re_bwd[l], b_im_bwd[l], c_re_bwd[l], c_im_bwd[l])
            xb = _layer(xb, norm_mix_g[l], w_in[l], fwd, bwd, ssm_d[l], w_glu[l], b_glu[l], na_rpb[l],
                        g_ssm_out[l], g_na_out[l], w_out[l], norm_ffn_g[l], w_router[l],
                        w_gate[l], w_up[l], w_down[l])
        outs.append(_final_norm(xb, norm_final_g))
    return jnp.stack(outs)
```

```python
import functools

import numpy as np
import jax
import jax.numpy as jnp
from jax import lax
from jax.experimental import pallas as pl
from jax.experimental.pallas import tpu as pltpu

F32 = jnp.float32
BF16 = jnp.bfloat16

RMS_EPS = 1e-6
SSM_GROUP = 16
SSM_STATE = 64
NA_HEADS = 16
NA_HEAD_DIM = 64
GRID_W = 64
WIN_ROWS = 8
WIN_COLS = 16
N_EXPERTS = 16
EC_CAPACITY_FACTOR = 2

S5_CHUNK = 16
S5_SEGS = 8
HEADS_PER_DOT = 4
MOE_TOK_BLOCK = 256
MOE_WIN_ALIGN = 16
MOE_WIN = 64
MASK_NEG = -1e30
LOG2_E = float(np.log2(np.e))

VMEM_LIMIT_BYTES = 56 * 1024 * 1024


def _params(*semantics):
    return pltpu.CompilerParams(dimension_semantics=semantics, vmem_limit_bytes=VMEM_LIMIT_BYTES)


def _rms(x, g):
    ms = jnp.mean(x * x, axis=-1, keepdims=True)
    return x * lax.rsqrt(ms + RMS_EPS) * g


def _inproj_kernel(x_ref, g_ref, w_ref, u_ref, qkv_ref, h_scr, *, n_u, n_q, q_scale):
    j = pl.program_id(1)

    def project(h):
        return jnp.dot(h, w_ref[...].astype(BF16), preferred_element_type=F32)

    @pl.when(j == 0)
    def _():
        h = _rms(x_ref[...], g_ref[...]).astype(BF16)
        h_scr[...] = h
        u_ref[...] = project(h)

    @pl.when((j > 0) & (j < n_u))
    def _():
        u_ref[...] = project(h_scr[...])

    @pl.when(j >= n_u)
    def _():
        scale = jnp.where(j < n_u + n_q, q_scale, 1.0).astype(F32)
        qkv_ref[...] = (project(h_scr[...]) * scale).astype(BF16)


def _in_proj(x, g, w_in, d_ssm, d_na):
    seq, d_model = x.shape
    tm = min(1024, seq)
    tn = 1024
    n_u, n_q = d_ssm // tn, d_na // tn
    n_cols = w_in.shape[1] // tn
    kern = functools.partial(_inproj_kernel, n_u=n_u, n_q=n_q, q_scale=NA_HEAD_DIM ** -0.5 * LOG2_E)
    return pl.pallas_call(
        kern,
        grid=(seq // tm, n_cols),
        in_specs=[
            pl.BlockSpec((tm, d_model), lambda i, j: (i, 0)),
            pl.BlockSpec((1, d_model), lambda i, j: (0, 0)),
            pl.BlockSpec((d_model, tn), lambda i, j: (0, j)),
        ],
        out_specs=[
            pl.BlockSpec((tm, tn), lambda i, j: (i, jnp.minimum(j, n_u - 1))),
            pl.BlockSpec((tm, tn), lambda i, j: (i, jnp.maximum(j - n_u, 0))),
        ],
        out_shape=[
            jax.ShapeDtypeStruct((seq, d_ssm), F32),
            jax.ShapeDtypeStruct((seq, 3 * d_na), BF16),
        ],
        scratch_shapes=[pltpu.VMEM((tm, d_model), BF16)],
        compiler_params=_params("arbitrary", "arbitrary"),
        name="in_proj",
    )(x, g.reshape(1, d_model), w_in)


def _cmul(ar, ai, br, bi):
    return ar * br - ai * bi, ar * bi + ai * br


def _s5_tables(fwd, bwd, n_tiles):
    t_len = S5_CHUNK
    a_re, a_im, log_dt, b_re, b_im, c_re, c_im = (jnp.stack([f, b]).astype(F32) for f, b in zip(fwd, bwd))
    _, n_grp, n_st = a_re.shape
    n_ch = b_re.shape[-1]
    dt = jnp.exp(log_dt)[:, :, None]
    xr, xi = a_re * dt, a_im * dt
    steps = np.arange(t_len)

    def power(x_r, x_i, exps):
        e = jnp.asarray(exps, F32).reshape(exps.shape + (1,) * (x_r.ndim - 1))
        mag = jnp.exp(x_r[None] * e)
        return mag * jnp.cos(x_i[None] * e), mag * jnp.sin(x_i[None] * e)

    def per_step(exps):
        p_r, p_i = power(xr, xi, exps)
        return jnp.transpose(p_r, (1, 2, 0, 3)), jnp.transpose(p_i, (1, 2, 0, 3))

    a1_r, a1_i = power(xr, xi, np.ones((1, 2)))
    nr, ni = a1_r[0] - 1.0, a1_i[0]
    den = a_re * a_re + a_im * a_im
    qr, qi = (nr * a_re + ni * a_im) / den, (ni * a_re - nr * a_im) / den
    bb_r, bb_i = _cmul(qr[:, :, None, :], qi[:, :, None, :], jnp.swapaxes(b_re, 2, 3), jnp.swapaxes(b_im, 2, 3))
    am_r, am_i = per_step(np.stack([steps, steps[::-1]], axis=1))
    y_r, y_i = _cmul(c_re[:, :, None], c_im[:, :, None], am_r[:, :, :, None, :], am_i[:, :, :, None, :])
    y = jnp.concatenate([y_r, y_i], axis=-1).reshape(2, n_grp, t_len * n_ch, 2 * n_st)
    lag = jnp.einsum('dgck,dgqk->dgcq', jnp.concatenate([bb_r, -bb_i], axis=-1), y,
                     precision=lax.Precision.HIGH)
    st = per_step(np.stack([t_len - 1 - steps, steps], axis=1)) + (bb_r, bb_i)
    wo = per_step(np.stack([steps + 1, t_len - steps], axis=1)) + (c_re, c_im)
    xrf, xif = xr.reshape(2, -1), xi.reshape(2, -1)
    tiles = np.arange(n_tiles)
    ends = power(xrf, xif, np.array([[t_len, t_len], [t_len * n_tiles, t_len * n_tiles]]))
    ap = jnp.stack(ends, axis=2).reshape(8, -1)
    at_r, at_i = power(xrf, xif, t_len * np.stack([tiles, tiles[::-1]], axis=1))
    pw = jnp.transpose(jnp.stack([at_r, at_i], axis=2), (1, 2, 0, 3)).reshape(4, n_tiles, -1)
    return lag, st, wo, ap, pw


def _transpose_lane_blocks(groups, width):
    groups = [list(g) for g in groups]
    n = len(groups[0])
    first = groups[0][0]
    lanes = first.shape[-1]
    axis = first.ndim - 1
    block = lax.broadcasted_iota(jnp.int32, first.shape, axis) // width
    d = n // 2
    while d:
        upper = (block & d) != 0
        pairs = [(g, i) for g in groups for i in range(n) if i & d == 0]
        up = [pltpu.roll(g[i | d], d * width, axis) for g, i in pairs]
        down = [pltpu.roll(g[i], lanes - d * width, axis) for g, i in pairs]
        for (g, i), u, dn in zip(pairs, up, down):
            g[i], g[i | d] = jnp.where(upper, u, g[i]), jnp.where(upper, g[i | d], dn)
        d //= 2
    return groups


def _s5_a_kernel(u_ref, lag_ref, pe_r_ref, pe_i_ref, bb_r_ref, bb_i_ref, y_ref, s_ref, w_grp, e_pair, ucat,
                 *, n_tiles, seg_stride):
    n_grp = lag_ref.shape[1]
    cw = S5_CHUNK * SSM_GROUP
    n_plane, n_state = 4, pe_r_ref.shape[-1]
    lanes = u_ref.shape[-1]
    pc = n_plane * 2 * n_state

    @pl.when(pl.program_id(0) == 0)
    def _():
        e_pair[...] = jnp.zeros_like(e_pair)

    def place_tables(g):
        blank = jnp.zeros((SSM_GROUP, cw), F32)
        strip = jnp.concatenate([blank, lag_ref[0, g], lag_ref[1, g], blank], axis=1)
        for t in range(S5_CHUNK):
            r0 = t * SSM_GROUP
            fwd_lo = cw - t * SSM_GROUP
            bwd_lo = 2 * cw + (S5_CHUNK - 1 - t) * SSM_GROUP
            piece = strip[:, fwd_lo:fwd_lo + cw] + strip[:, bwd_lo:bwd_lo + cw]
            w_grp[g, r0:r0 + SSM_GROUP, :] = piece.astype(BF16)
            for d in range(2):
                planes = _cmul(pe_r_ref[d, g, t:t + 1, :], pe_i_ref[d, g, t:t + 1, :], bb_r_ref[d, g], bb_i_ref[d, g])
                for ri in range(2):
                    c0 = (2 * d + ri) * 2 * n_state + (g % 2) * n_state
                    rows = slice((g % 2) * cw + r0, (g % 2) * cw + r0 + SSM_GROUP)
                    e_pair[g // 2, rows, c0:c0 + n_state] = planes[ri].astype(BF16)

    per_trip = 4

    def gather(trip, _):
        groups, where = [], []
        for i in range(per_trip):
            jj = per_trip * trip + i
            pieces = []
            for t in range(S5_CHUNK):
                rows = [u_ref[pl.ds(S5_CHUNK * (2 * jj + h) + t, S5_SEGS, stride=seg_stride), :] for h in range(2)]
                pieces.append(jnp.concatenate(rows, axis=0))
            dst = pl.ds(pl.multiple_of(jj * 2 * S5_SEGS, 2 * S5_SEGS), 2 * S5_SEGS)
            for half in range(S5_CHUNK // n_grp):
                groups.append(pieces[half * n_grp:(half + 1) * n_grp])
                where.append((dst, half))
        for (dst, half), tiles in zip(where, _transpose_lane_blocks(groups, SSM_GROUP)):
            for g in range(n_grp):
                c0 = g * cw + half * lanes
                ucat[dst, c0:c0 + lanes] = tiles[g].astype(BF16)
        return 0

    lax.fori_loop(0, n_tiles // (2 * per_trip), gather, 0)
    for g in range(n_grp):
        place_tables(g)
        y_ref[:, g * cw:(g + 1) * cw] = jnp.dot(ucat[:, g * cw:(g + 1) * cw], w_grp[g], preferred_element_type=F32)
        if g % 2 == 1:
            q = g // 2
            s_ref[:, q * pc:(q + 1) * pc] = jnp.dot(ucat[:, q * 2 * cw:(q + 1) * 2 * cw], e_pair[q],
                                                    preferred_element_type=F32)


def _s5_bc_kernel(s_ref, yi_ref, po_r_ref, po_i_ref, c_r_ref, c_i_ref, ap_ref, pw_ref, y_ref, sin, ychunk, c_blk,
                  sin_bf, *, n_tiles, seg_stride):
    n_plane = 4
    _, n_grp, _, n_state = po_r_ref.shape
    cw = S5_CHUNK * SSM_GROUP
    n_pair = n_grp // 2
    pw_ = 2 * n_state
    pc = n_plane * pw_
    lanes = y_ref.shape[-1]

    @pl.when(pl.program_id(0) == 0)
    def _():
        c_blk[...] = jnp.zeros_like(c_blk)

    def place_table(g):
        gl = g % 2
        for d in range(2):
            for t in range(S5_CHUNK):
                w_r, w_i = _cmul(c_r_ref[d, g], c_i_ref[d, g], po_r_ref[d, g, t:t + 1, :], po_i_ref[d, g, t:t + 1, :])
                rows = slice(gl * cw + t * SSM_GROUP, gl * cw + (t + 1) * SSM_GROUP)
                for ri, plane in enumerate((w_r, -w_i)):
                    c0 = (2 * d + ri) * pw_ + gl * n_state
                    c_blk[g // 2, rows, c0:c0 + n_state] = plane.astype(BF16)

    chains = [(q, d) for q in range(n_pair) for d in range(2)]

    def plane_lanes(q, d):
        base = q * pc + d * 2 * pw_
        return slice(base, base + pw_), slice(base + pw_, base + 2 * pw_)

    def coef(row, q):
        return ap_ref[row:row + 1, q * pw_:(q + 1) * pw_]

    def tile_rows(n, d):
        j = n if d == 0 else n_tiles - 1 - n
        return pl.ds(pl.multiple_of(j * S5_SEGS, S5_SEGS), S5_SEGS)

    def step(n, carry):
        out = []
        for (q, d), (zr, zi) in zip(chains, carry):
            re, im = plane_lanes(q, d)
            rows = tile_rows(n, d)
            sin[rows, re] = zr
            sin[rows, im] = zi
            ar, ai = coef(2 * d, q), coef(2 * d + 1, q)
            out.append((ar * zr - ai * zi + s_ref[rows, re], ar * zi + ai * zr + s_ref[rows, im]))
        return tuple(out)

    z0 = jnp.zeros((S5_SEGS, pw_), F32)
    ends = lax.fori_loop(0, n_tiles, step, tuple((z0, z0) for _ in chains))

    carries = []
    for (q, d), (zr, zi) in zip(chains, ends):
        sr, si = coef(4 + 2 * d, q), coef(5 + 2 * d, q)
        cr = jnp.zeros((1, pw_), F32)
        ci = jnp.zeros((1, pw_), F32)
        seg_r = [None] * S5_SEGS
        seg_i = [None] * S5_SEGS
        for s in (range(S5_SEGS) if d == 0 else range(S5_SEGS - 1, -1, -1)):
            seg_r[s], seg_i[s] = cr, ci
            cr, ci = (zr[s:s + 1] + sr * cr - si * ci, zi[s:s + 1] + sr * ci + si * cr)
        carries.append((jnp.concatenate(seg_r, axis=0), jnp.concatenate(seg_i, axis=0)))

    def fix(n8, _):
        tiles = pl.ds(pl.multiple_of(n8 * 8, 8), 8)
        for (q, d), (car_r, car_i) in zip(chains, carries):
            re, im = plane_lanes(q, d)
            pr8 = pw_ref[2 * d, tiles, q * pw_:(q + 1) * pw_]
            pi8 = pw_ref[2 * d + 1, tiles, q * pw_:(q + 1) * pw_]
            car2_r = jnp.concatenate([car_r, car_r], axis=0)
            car2_i = jnp.concatenate([car_i, car_i], axis=0)
            for r in range(0, 8, 2):
                rows = pl.ds(pl.multiple_of((n8 * 8 + r) * S5_SEGS, 2 * S5_SEGS), 2 * S5_SEGS)
                pr = jnp.concatenate([jnp.broadcast_to(pr8[r + h:r + h + 1], (S5_SEGS, pw_)) for h in range(2)], axis=0)
                pi = jnp.concatenate([jnp.broadcast_to(pi8[r + h:r + h + 1], (S5_SEGS, pw_)) for h in range(2)], axis=0)
                sin_bf[rows, re] = (sin[rows, re] + (pr * car2_r - pi * car2_i)).astype(BF16)
                sin_bf[rows, im] = (sin[rows, im] + (pr * car2_i + pi * car2_r)).astype(BF16)
        return 0

    lax.fori_loop(0, n_tiles // 8, fix, 0)

    for q in range(n_pair):
        place_table(2 * q)
        place_table(2 * q + 1)
        lhs = sin_bf[:, q * pc:(q + 1) * pc]
        carried = lax.dot_general(lhs, c_blk[q], (((1,), (1,)), ((), ())), preferred_element_type=F32)
        ychunk[:, q * 2 * cw:(q + 1) * 2 * cw] = yi_ref[:, q * 2 * cw:(q + 1) * 2 * cw] + carried

    per_trip = 8

    def scatter(jj, _):
        groups, where = [], []
        for h in range(per_trip):
            j = per_trip * jj + h
            src = pl.ds(pl.multiple_of(j * S5_SEGS, S5_SEGS), S5_SEGS)
            for half in range(S5_CHUNK // n_grp):
                groups.append([ychunk[src, g * cw + half * lanes:g * cw + (half + 1) * lanes] for g in range(n_grp)])
                where.append(S5_CHUNK * j + half * n_grp)
        for t0, tiles in zip(where, _transpose_lane_blocks(groups, SSM_GROUP)):
            for k, tile in enumerate(tiles):
                y_ref[pl.ds(t0 + k, S5_SEGS, stride=seg_stride), :] = tile
        return 0

    lax.fori_loop(0, n_tiles // per_trip, scatter, 0)


def _s5_scan(u, fwd, bwd):
    seq, d_ssm = u.shape
    n_groups = d_ssm // SSM_GROUP
    n_state = SSM_STATE
    cw = S5_CHUNK * SSM_GROUP
    n_rows = seq // S5_CHUNK
    n_tiles = n_rows // S5_SEGS
    lanes = 128
    gpb = lanes // SSM_GROUP
    n_blocks = d_ssm // lanes
    bw = gpb * cw
    sw = gpb * 4 * n_state
    seg_stride = n_tiles * S5_CHUNK

    lag, st, wo, ap, pw = _s5_tables(fwd, bwd, n_tiles)

    def per_block(t):
        return pl.BlockSpec((t.shape[0], gpb) + t.shape[2:], lambda i: (0, i) + (0,) * (t.ndim - 2))

    y_intra, s_loc = pl.pallas_call(
        functools.partial(_s5_a_kernel, n_tiles=n_tiles, seg_stride=seg_stride),
        grid=(n_blocks,),
        in_specs=[
            pl.BlockSpec((seq, lanes), lambda i: (0, i)),
            per_block(lag), *[per_block(t) for t in st],
        ],
        out_specs=[
            pl.BlockSpec((n_rows, bw), lambda i: (0, i)),
            pl.BlockSpec((n_rows, sw), lambda i: (0, i)),
        ],
        out_shape=[
            jax.ShapeDtypeStruct((n_rows, n_blocks * bw), F32),
            jax.ShapeDtypeStruct((n_rows, n_blocks * sw), F32),
        ],
        scratch_shapes=[
            pltpu.VMEM((gpb, cw, cw), BF16),
            pltpu.VMEM((gpb // 2, 2 * cw, 8 * n_state), BF16),
            pltpu.VMEM((n_rows, S5_CHUNK * lanes), BF16),
        ],
        compiler_params=_params("arbitrary"),
        name="s5_a",
    )(u, lag, *st)

    pl_lanes = gpb * n_state
    return pl.pallas_call(
        functools.partial(_s5_bc_kernel, n_tiles=n_tiles, seg_stride=seg_stride),
        grid=(n_blocks,),
        in_specs=[
            pl.BlockSpec((n_rows, sw), lambda i: (0, i)),
            pl.BlockSpec((n_rows, bw), lambda i: (0, i)),
            *[per_block(t) for t in wo],
            pl.BlockSpec((8, pl_lanes), lambda i: (0, i)),
            pl.BlockSpec((4, n_tiles, pl_lanes), lambda i: (0, 0, i)),
        ],
        out_specs=pl.BlockSpec((seq, lanes), lambda i: (0, i)),
        out_shape=jax.ShapeDtypeStruct((seq, d_ssm), F32),
        scratch_shapes=[
            pltpu.VMEM((n_rows, sw), F32),
            pltpu.VMEM((n_rows, bw), F32),
            pltpu.VMEM((gpb // 2, 2 * cw, 8 * n_state), BF16),
            pltpu.VMEM((n_rows, sw), BF16),
        ],
        compiler_params=_params("arbitrary"),
        name="s5_bc",
    )(s_loc, y_intra, *wo, ap, pw)


def _s5_post_kernel(y_ref, u_ref, d_ref, w_ref, b_ref, g_ref, o_ref):
    y = y_ref[...] + d_ref[...] * u_ref[...]
    c0 = np.float32(np.sqrt(2.0 / np.pi))
    c01 = np.float32(np.sqrt(2.0 / np.pi) * 0.044715)
    half_y = 0.5 * y
    y = half_y + half_y * jnp.tanh(y * (c0 + c01 * (y * y)))
    z = jnp.dot(y.astype(BF16), w_ref[...].astype(BF16), preferred_element_type=F32) + b_ref[...]
    half_y = 0.5 * y
    o = half_y + half_y * jnp.tanh(0.5 * z)
    o_ref[...] = _rms(o, g_ref[...]).astype(BF16)


def _s5_post(y, u, d_skip, w_glu, b_glu, g):
    seq, d = y.shape
    tm = min(1024, seq)
    row = lambda i: (i, 0)
    fix = lambda i: (0, 0)
    return pl.pallas_call(
        _s5_post_kernel,
        grid=(seq // tm,),
        in_specs=[
            pl.BlockSpec((tm, d), row), pl.BlockSpec((tm, d), row), pl.BlockSpec((1, d), fix),
            pl.BlockSpec((d, d), fix), pl.BlockSpec((1, d), fix), pl.BlockSpec((1, d), fix),
        ],
        out_specs=pl.BlockSpec((tm, d), row),
        out_shape=jax.ShapeDtypeStruct((seq, d), BF16),
        compiler_params=_params("arbitrary"),
        name="s5_post",
    )(y, u, d_skip.reshape(1, d), w_glu, b_glu.reshape(1, d), g.reshape(1, d))


def _na_bias_table(rpb):
    n_heads = rpb.shape[0]
    n_quad = n_heads // HEADS_PER_DOT
    n_dy, n_dx = rpb.shape[1], rpb.shape[2]
    cols = np.arange(GRID_W)
    col_start = np.clip(cols - WIN_COLS // 2, 0, GRID_W - WIN_COLS)
    key_cols = np.arange(GRID_W)
    in_win = (key_cols[None, :] >= col_start[:, None]) & (key_cols[None, :] < col_start[:, None] + WIN_COLS)
    dx = key_cols[None, :] - cols[:, None] + (WIN_COLS - 1)
    pick_x = (dx[:, :, None] == np.arange(n_dx)).astype(np.float32)
    pick = np.zeros((2, GRID_W, 2 * GRID_W, n_dx), np.float32)
    for j in range(2):
        pick[j, :, j * GRID_W:(j + 1) * GRID_W] = pick_x
    dy = 2 * np.arange(WIN_ROWS)[None, :, None] + np.arange(2)[:, None, None] + np.arange(2)[None, None, :]
    pick_y = (dy[..., None] == np.arange(n_dy)).astype(np.float32)
    hp = lax.Precision.HIGHEST
    rows = jnp.einsum('spjy,hyx->spjhx', pick_y, rpb.astype(F32) * LOG2_E, precision=hp)
    rows = rows.reshape(2, WIN_ROWS, 2, n_quad, HEADS_PER_DOT, n_dx)
    b = jnp.einsum('spjqhx,jclx->sqphcl', rows, pick, precision=lax.Precision.HIGH)
    keep = np.concatenate([in_win, in_win], axis=1)
    b = jnp.where(keep, b, MASK_NEG)
    return b.reshape(2, n_quad, WIN_ROWS, HEADS_PER_DOT * GRID_W, 2 * GRID_W)


def _na_kernel(q_ref, k_ref, v_ref, b_ref, g_ref, o_ref, *, rows, rows_per_step):
    n_keys = WIN_ROWS * GRID_W
    d_na = k_ref.shape[-1]
    pw = HEADS_PER_DOT * NA_HEAD_DIM
    row_head = lax.broadcasted_iota(jnp.int32, (HEADS_PER_DOT * GRID_W, pw), 0) // GRID_W
    col_head = lax.broadcasted_iota(jnp.int32, (HEADS_PER_DOT * GRID_W, pw), 1) // NA_HEAD_DIM
    diag = row_head == col_head
    out_head = lax.broadcasted_iota(jnp.int32, (GRID_W, pw), 1) // NA_HEAD_DIM
    first = pl.program_id(0) * rows_per_step
    block_start = jnp.clip(first - WIN_ROWS // 2, 0, rows - k_ref.shape[0])

    def one_row(i, _):
        r = first + i
        win_start = jnp.clip(r - WIN_ROWS // 2, 0, rows - WIN_ROWS)
        dy0 = win_start - r + (WIN_ROWS - 1)
        k = k_ref[pl.ds(win_start - block_start, WIN_ROWS)].reshape(n_keys, d_na)
        v = v_ref[pl.ds(win_start - block_start, WIN_ROWS)].reshape(n_keys, d_na)
        q_rows = pl.ds(pl.multiple_of(i * GRID_W, GRID_W), GRID_W)
        outs = []
        for p in range(d_na // pw):
            sl = slice(p * pw, (p + 1) * pw)
            q4 = q_ref[q_rows, sl]
            qbd = jnp.where(diag, jnp.concatenate([q4] * HEADS_PER_DOT, axis=0), jnp.zeros((), BF16))
            s = lax.dot_general(qbd, k[:, sl], (((1,), (1,)), ((), ())), preferred_element_type=F32)
            s = s + jnp.concatenate([b_ref[dy0 % 2, p, dy0 // 2 + j] for j in range(WIN_ROWS // 2)], axis=1)
            m = jnp.max(s, axis=-1, keepdims=True)
            e = jnp.exp2(s - m)
            l = jnp.sum(e, axis=-1, keepdims=True)
            o = jnp.dot(e.astype(BF16), v[:, sl], preferred_element_type=F32) / l
            acc = jnp.zeros((GRID_W, pw), F32)
            for h in range(HEADS_PER_DOT):
                acc = acc + jnp.where(out_head == h, o[h * GRID_W:(h + 1) * GRID_W], 0.0)
            outs.append(acc)
        y = jnp.concatenate(outs, axis=1)
        o_ref[q_rows, :] = _rms(y, g_ref[...]).astype(BF16)
        return 0

    lax.fori_loop(0, rows_per_step, one_row, 0, unroll=4)


def _neighbourhood_attention(qkv, rpb, g):
    seq = qkv.shape[0]
    d_na = qkv.shape[1] // 3
    rows = seq // GRID_W
    bias = _na_bias_table(rpb)
    qkv3 = qkv.reshape(rows, GRID_W, 3 * d_na)

    rps = 8
    key_rows = 2 * WIN_ROWS
    assert rps - 1 + WIN_ROWS <= key_rows and rows % rps == 0

    def block_start(b):
        return jnp.clip(b * rps - WIN_ROWS // 2, 0, rows - key_rows)

    window = (pl.Element(key_rows), pl.Element(GRID_W), pl.Element(d_na))

    return pl.pallas_call(
        functools.partial(_na_kernel, rows=rows, rows_per_step=rps),
        grid=(rows // rps,),
        in_specs=[
            pl.BlockSpec((rps * GRID_W, d_na), lambda b: (b, 0)),
            pl.BlockSpec(window, lambda b: (block_start(b), 0, d_na)),
            pl.BlockSpec(window, lambda b: (block_start(b), 0, 2 * d_na)),
            pl.BlockSpec(bias.shape, lambda b: (0, 0, 0, 0, 0)),
            pl.BlockSpec((1, d_na), lambda b: (0, 0)),
        ],
        out_specs=pl.BlockSpec((rps * GRID_W, d_na), lambda b: (b, 0)),
        out_shape=jax.ShapeDtypeStruct((seq, d_na), BF16),
        compiler_params=_params("arbitrary"),
        name="na",
    )(qkv, qkv3, qkv3, bias, g.reshape(1, d_na))


def _outproj_kernel(a_ref, b_ref, w_ref, x_ref, o_ref):
    da = a_ref.shape[-1]
    acc = jnp.dot(a_ref[...], w_ref[:da, :].astype(BF16), preferred_element_type=F32)
    acc = acc + jnp.dot(b_ref[...], w_ref[da:, :].astype(BF16), preferred_element_type=F32)
    o_ref[...] = x_ref[...] + acc


def _out_proj(y_ssm, y_na, w_out, x):
    seq, d_model = x.shape
    da, db = y_ssm.shape[1], y_na.shape[1]
    tm = min(2048, seq)
    tn = 512
    return pl.pallas_call(
        _outproj_kernel,
        grid=(seq // tm, d_model // tn),
        in_specs=[
            pl.BlockSpec((tm, da), lambda i, j: (i, 0)),
            pl.BlockSpec((tm, db), lambda i, j: (i, 0)),
            pl.BlockSpec((da + db, tn), lambda i, j: (0, j)),
            pl.BlockSpec((tm, tn), lambda i, j: (i, j)),
        ],
        out_specs=pl.BlockSpec((tm, tn), lambda i, j: (i, j)),
        out_shape=jax.ShapeDtypeStruct((seq, d_model), F32),
        compiler_params=_params("arbitrary", "arbitrary"),
        name="out_proj",
    )(y_ssm, y_na, w_out, x)


def _router_kernel(x_ref, g_ref, wt_ref, h_ref, a_ref):
    h = _rms(x_ref[...], g_ref[...])
    h_hi = h.astype(BF16)
    h_ref[...] = h_hi
    h_lo = (h - h_hi.astype(F32)).astype(BF16)
    w = wt_ref[...]
    w_hi = w.astype(BF16)
    w_lo = (w - w_hi.astype(F32)).astype(BF16)
    n_exp = w.shape[0]
    nt = (((1,), (1,)), ((), ()))
    both = lax.dot_general(jnp.concatenate([w_hi, w_lo], axis=0), h_hi, nt, preferred_element_type=F32)
    logits = both[:n_exp] + both[n_exp:] + lax.dot_general(w_hi, h_lo, nt, preferred_element_type=F32)
    m = jnp.max(logits, axis=0, keepdims=True)
    e = jnp.exp(logits - m)
    a_ref[...] = e / jnp.sum(e, axis=0, keepdims=True)


def _router(x1, g, w_router):
    seq, d_model = x1.shape
    n_exp = w_router.shape[1]
    tm = min(1024, seq)
    return pl.pallas_call(
        _router_kernel,
        grid=(seq // tm,),
        in_specs=[
            pl.BlockSpec((tm, d_model), lambda i: (i, 0)),
            pl.BlockSpec((1, d_model), lambda i: (0, 0)),
            pl.BlockSpec((n_exp, d_model), lambda i: (0, 0)),
        ],
        out_specs=[
            pl.BlockSpec((tm, d_model), lambda i: (i, 0)),
            pl.BlockSpec((n_exp, tm), lambda i: (0, i)),
        ],
        out_shape=[
            jax.ShapeDtypeStruct((seq, d_model), BF16),
            jax.ShapeDtypeStruct((n_exp, seq), F32),
        ],
        compiler_params=_params("arbitrary"),
        name="router",
    )(x1, g.reshape(1, d_model), w_router.T)


def _topk_kernel(a_ref, posw_ref, gate_ref, ws_ref, nr_ref, *, cap, blk, win):
    a = a_ref[...]
    n_exp, seq = a.shape
    n_blk = seq // blk
    bits = pltpu.bitcast(a, jnp.int32)

    def bit_step(i, thr):
        cand = thr | jnp.left_shift(jnp.int32(1), 30 - i)
        cnt = jnp.sum((bits >= cand).astype(jnp.int32), axis=-1, keepdims=True)
        return jnp.where(cnt >= cap, cand, thr)

    thr = lax.fori_loop(0, 31, bit_step, jnp.zeros((n_exp, 1), jnp.int32))
    gt = bits > thr
    eq = bits == thr
    need = cap - jnp.sum(gt.astype(jnp.int32), axis=-1, keepdims=True)

    tri = (lax.broadcasted_iota(jnp.int32, (blk, blk), 0)
           <= lax.broadcasted_iota(jnp.int32, (blk, blk), 1)).astype(BF16)
    blk_of_tok = lax.broadcasted_iota(jnp.int32, (seq, n_blk), 0) // blk
    tok_to_blk = (blk_of_tok == lax.broadcasted_iota(jnp.int32, (seq, n_blk), 1)).astype(BF16)
    blk_before = (lax.broadcasted_iota(jnp.int32, (n_blk, n_blk), 0)
                  < lax.broadcasted_iota(jnp.int32, (n_blk, n_blk), 1)).astype(BF16)
    erow = lax.broadcasted_iota(jnp.int32, (2 * n_blk, seq), 0)
    ecol = lax.broadcasted_iota(jnp.int32, (2 * n_blk, seq), 1) // blk
    expand = jnp.where(erow == ecol, 32.0, jnp.where(erow - n_blk == ecol, 1.0, 0.0)).astype(BF16)

    def prefix_counts(mask):
        mb = jnp.where(mask, 1.0, 0.0).astype(BF16)
        local = jnp.concatenate(
            [jnp.dot(mb[:, b * blk:(b + 1) * blk], tri, preferred_element_type=F32) for b in range(n_blk)],
            axis=1)
        per_blk = jnp.dot(mb, tok_to_blk, preferred_element_type=F32)
        start = jnp.dot(per_blk.astype(BF16), blk_before, preferred_element_type=F32)
        hi = jnp.floor(start * (1.0 / 32.0))
        parts = jnp.concatenate([hi, start - 32.0 * hi], axis=1).astype(BF16)
        start_tok = jnp.dot(parts, expand, preferred_element_type=F32)
        return local + start_tok, start, start_tok, per_blk

    eq_incl, _, _, _ = prefix_counts(eq)
    sel = gt | (eq & (eq_incl - 1.0 < need.astype(F32)))
    incl, start, start_tok, per_blk = prefix_counts(sel)

    def window(s):
        return jnp.floor(s * (1.0 / MOE_WIN_ALIGN)) * MOE_WIN_ALIGN

    posw_ref[...] = jnp.where(sel, (incl - 1.0 - window(start_tok)).astype(jnp.int32), -1)
    gate_ref[...] = jnp.where(sel, a, 0.0)
    ws_ref[...] = window(start).astype(jnp.int32)
    span = start - window(start) + per_blk
    rounds = jnp.floor((span + float(win - 1)) * (1.0 / win))
    nr_ref[...] = jnp.max(rounds, axis=0, keepdims=True).astype(jnp.int32)


def _topk(aff_t, cap, blk, win):
    n_exp, seq = aff_t.shape
    n_blk = seq // blk
    full = lambda *_: (0, 0)
    return pl.pallas_call(
        functools.partial(_topk_kernel, cap=cap, blk=blk, win=win),
        grid=(1,),
        in_specs=[pl.BlockSpec((n_exp, seq), full)],
        out_specs=[pl.BlockSpec((n_exp, seq), full), pl.BlockSpec((n_exp, seq), full),
                   pl.BlockSpec((n_exp, n_blk), full), pl.BlockSpec((1, n_blk), full)],
        out_shape=[
            jax.ShapeDtypeStruct((n_exp, seq), jnp.int32),
            jax.ShapeDtypeStruct((n_exp, seq), F32),
            jax.ShapeDtypeStruct((n_exp, n_blk), jnp.int32),
            jax.ShapeDtypeStruct((1, n_blk), jnp.int32),
        ],
        compiler_params=_params("arbitrary"),
        name="topk",
    )(aff_t)


def _window(ws_ref, e, b, r, n_blk, win, cap):
    ws = ws_ref[e * n_blk + b] + r * win
    start = jnp.minimum(ws, cap - win)
    return pl.multiple_of(start, MOE_WIN_ALIGN), ws - start


def _gather_kernel(ws_ref, nr_ref, h_ref, rel_ref, xe_ref, *, blk, win, n_blk):
    n_exp, cap, _ = xe_ref.shape
    xe_ref[...] = jnp.zeros_like(xe_ref)
    slot = lax.broadcasted_iota(jnp.int32, (win, blk), 0)

    def one_round(b, r):
        rows = h_ref[pl.ds(pl.multiple_of(b * blk, blk), blk), :]
        rel = rel_ref[b]
        starts, hots = [], []
        for e in range(n_exp):
            start, shift = _window(ws_ref, e, b, r, n_blk, win, cap)
            relr = rel[e:e + 1, :] - r * win
            key = jnp.where(relr >= 0, relr + shift, -1)
            hots.append(jnp.where(slot == key, 1.0, 0.0).astype(BF16))
            starts.append(start)
        res = jnp.dot(jnp.concatenate(hots, axis=0), rows, preferred_element_type=F32)
        for e in range(n_exp):
            dst = pl.ds(starts[e], win)
            xe_ref[e, dst, :] = (xe_ref[e, dst, :].astype(F32) + res[e * win:(e + 1) * win]).astype(BF16)

    def extra_rounds(b):
        def body(r, carry):
            one_round(b, r)
            return carry
        lax.fori_loop(1, nr_ref[b], body, 0)

    per_trip = 4

    def blocks(trip, _):
        for h in range(per_trip):
            one_round(per_trip * trip + h, 0)
        for h in range(per_trip):
            extra_rounds(per_trip * trip + h)
        return 0

    lax.fori_loop(0, n_blk // per_trip, blocks, 0)


def _moe_gather(ws_flat, n_rounds, h2, rel3, cap, win):
    seq, d_model = h2.shape
    n_blk, n_exp, blk = rel3.shape
    dq = d_model // 4
    grid_spec = pltpu.PrefetchScalarGridSpec(
        num_scalar_prefetch=2,
        grid=(4,),
        in_specs=[
            pl.BlockSpec((seq, dq), lambda c, ws, nr: (0, c)),
            pl.BlockSpec((n_blk, n_exp, blk), lambda c, ws, nr: (0, 0, 0)),
        ],
        out_specs=pl.BlockSpec((n_exp, cap, dq), lambda c, ws, nr: (0, 0, c)),
    )
    return pl.pallas_call(
        functools.partial(_gather_kernel, blk=blk, win=win, n_blk=n_blk),
        grid_spec=grid_spec,
        out_shape=jax.ShapeDtypeStruct((n_exp, cap, d_model), BF16),
        compiler_params=_params("arbitrary"),
        name="moe_gather",
    )(ws_flat, n_rounds, h2, rel3)


def _ffn_kernel(x_ref, wg_ref, wu_ref, wd_ref, y_ref, act_ref, *, n_f):
    s = pl.program_id(1)
    tf = wg_ref.shape[-1]

    @pl.when(s < n_f)
    def _():
        x = x_ref[0]
        g = jnp.dot(x, wg_ref[0].astype(BF16), preferred_element_type=F32)
        u = jnp.dot(x, wu_ref[0].astype(BF16), preferred_element_type=F32)
        act_ref[s] = (g * (1.0 / (1.0 + jnp.exp(-g))) * u).astype(BF16)

    @pl.when(s >= n_f)
    def _():
        acc = jnp.dot(act_ref[0], wd_ref[0, 0:tf, :].astype(BF16), preferred_element_type=F32)
        for f in range(1, n_f):
            acc = acc + jnp.dot(act_ref[f], wd_ref[0, f * tf:(f + 1) * tf, :].astype(BF16),
                                preferred_element_type=F32)
        y_ref[0] = acc.astype(BF16)


def _moe_ffn(xe, w_gate, w_up, w_down):
    n_exp, cap, d_model = xe.shape
    d_ff = w_gate.shape[-1]
    tf = 512
    tn = 1024
    n_f, n_n = d_ff // tf, d_model // tn
    up_tile = lambda e, s: (e, 0, jnp.minimum(s, n_f - 1))
    down_tile = lambda e, s: (e, 0, jnp.maximum(s - n_f, 0))
    return pl.pallas_call(
        functools.partial(_ffn_kernel, n_f=n_f),
        grid=(n_exp, n_f + n_n),
        in_specs=[
            pl.BlockSpec((1, cap, d_model), lambda e, s: (e, 0, 0)),
            pl.BlockSpec((1, d_model, tf), up_tile),
            pl.BlockSpec((1, d_model, tf), up_tile),
            pl.BlockSpec((1, d_ff, tn), down_tile),
        ],
        out_specs=pl.BlockSpec((1, cap, tn), down_tile),
        out_shape=jax.ShapeDtypeStruct((n_exp, cap, d_model), BF16),
        scratch_shapes=[pltpu.VMEM((n_f, cap, tf), BF16)],
        compiler_params=_params("arbitrary", "arbitrary"),
        name="moe_ffn",
    )(xe, w_gate, w_up, w_down)


def _combine_kernel(ws_ref, nr_ref, ye_ref, x_ref, rel_ref, gate_ref, o_ref, *, blk, win, n_blk):
    n_exp, cap, _ = ye_ref.shape
    sub = x_ref.shape[0] // blk
    slot = lax.broadcasted_iota(jnp.int32, (win, blk), 0)

    def one_round(b, r):
        rel = rel_ref[b]
        gate = gate_ref[b]
        gates, wins = [], []
        for e in range(n_exp):
            start, shift = _window(ws_ref, e, b, r, n_blk, win, cap)
            relr = rel[e:e + 1, :] - r * win
            key = jnp.where(relr >= 0, relr + shift, -1)
            gates.append(jnp.where(slot == key, gate[e:e + 1, :], 0.0).astype(BF16))
            wins.append(ye_ref[e, pl.ds(start, win), :])
        return lax.dot_general(jnp.concatenate(gates, axis=0), jnp.concatenate(wins, axis=0),
                               (((0,), (0,)), ((), ())), preferred_element_type=F32)

    for s in range(sub):
        tok = slice(s * blk, (s + 1) * blk)
        o_ref[tok, :] = x_ref[tok, :] + one_round(pl.program_id(1) * sub + s, 0)
    for s in range(sub):
        tok = slice(s * blk, (s + 1) * blk)
        b = pl.program_id(1) * sub + s

        def extra(r, carry, b=b, tok=tok):
            o_ref[tok, :] += one_round(b, r)
            return carry

        lax.fori_loop(1, nr_ref[b], extra, 0)


def _moe_combine(ws_flat, n_rounds, ye, x1, rel3, gate3, win):
    seq, d_model = x1.shape
    n_exp, cap, _ = ye.shape
    n_blk, _, blk = rel3.shape
    dq = d_model // 4
    tile = min(8, n_blk) * blk
    whole = lambda c, t, ws, nr: (0, 0, 0)
    grid_spec = pltpu.PrefetchScalarGridSpec(
        num_scalar_prefetch=2,
        grid=(4, seq // tile),
        in_specs=[
            pl.BlockSpec((n_exp, cap, dq), lambda c, t, ws, nr: (0, 0, c)),
            pl.BlockSpec((tile, dq), lambda c, t, ws, nr: (t, c)),
            pl.BlockSpec(rel3.shape, whole),
            pl.BlockSpec(gate3.shape, whole),
        ],
        out_specs=pl.BlockSpec((tile, dq), lambda c, t, ws, nr: (t, c)),
    )
    return pl.pallas_call(
        functools.partial(_combine_kernel, blk=blk, win=win, n_blk=n_blk),
        grid_spec=grid_spec,
        out_shape=jax.ShapeDtypeStruct((seq, d_model), F32),
        compiler_params=_params("arbitrary", "arbitrary"),
        name="moe_combine",
    )(ws_flat, n_rounds, ye, x1, rel3, gate3)


def _final_norm_kernel(x_ref, g_ref, o_ref):
    o_ref[...] = _rms(x_ref[...], g_ref[...])


def _final_norm(x, g):
    seq, d_model = x.shape
    tm = min(1024, seq)
    return pl.pallas_call(
        _final_norm_kernel,
        grid=(seq // tm,),
        in_specs=[pl.BlockSpec((tm, d_model), lambda i: (i, 0)), pl.BlockSpec((1, d_model), lambda i: (0, 0))],
        out_specs=pl.BlockSpec((tm, d_model), lambda i: (i, 0)),
        out_shape=jax.ShapeDtypeStruct((seq, d_model), F32),
        compiler_params=_params("arbitrary"),
        name="final_norm",
    )(x, g.reshape(1, d_model))


def _layer(x, norm_mix_g, w_in, fwd, bwd, ssm_d, w_glu, b_glu, na_rpb, g_ssm_out, g_na_out, w_out,
           norm_ffn_g, w_router, w_gate, w_up, w_down):
    seq, d_model = x.shape
    d_ssm = ssm_d.shape[0]
    d_na = g_na_out.shape[0]
    n_exp = w_router.shape[1]
    cap = EC_CAPACITY_FACTOR * seq // n_exp
    blk = min(MOE_TOK_BLOCK, cap // 2)
    win = min(MOE_WIN, cap)

    u, qkv = _in_proj(x, norm_mix_g, w_in, d_ssm, d_na)
    y_ssm = _s5_post(_s5_scan(u, fwd, bwd), u, ssm_d, w_glu, b_glu, g_ssm_out)
    y_na = _neighbourhood_attention(qkv, na_rpb, g_na_out)
    x1 = _out_proj(y_ssm, y_na, w_out, x)

    h2, aff_t = _router(x1, norm_ffn_g, w_router)
    rel, gate, ws, n_rounds = _topk(aff_t, cap, blk, win)
    ws_flat = ws.reshape(-1)
    n_rounds = n_rounds.reshape(-1)
    rel3 = jnp.swapaxes(rel.reshape(n_exp, seq // blk, blk), 0, 1)
    gate3 = jnp.swapaxes(gate.reshape(n_exp, seq // blk, blk), 0, 1)
    xe = _moe_gather(ws_flat, n_rounds, h2, rel3, cap, win)
    ye = _moe_ffn(xe, w_gate, w_up, w_down)
    return _moe_combine(ws_flat, n_rounds, ye, x1, rel3, gate3, win)


def kernel(x, norm_mix_g, w_in, a_re_fwd, a_im_fwd, log_dt_fwd, b_re_fwd, b_im_fwd, c_re_fwd, c_im_fwd, a_re_bwd, a_im_bwd, log_dt_bwd, b_re_bwd, b_im_bwd, c_re_bwd, c_im_bwd, ssm_d, w_glu, b_glu, na_rpb, g_ssm_out, g_na_out, w_out, norm_ffn_g, w_router, w_gate, w_up, w_down, norm_final_g):
    bsz = x.shape[0]
    depth = w_in.shape[0]
    outs = []
    for b in range(bsz):
        xb = x[b]
        for l in range(depth):
            fwd = (a_re_fwd[l], a_im_fwd[l], log_dt_fwd[l], b_re_fwd[l], b_im_fwd[l], c_re_fwd[l], c_im_fwd[l])
            bwd = (a_re_bwd[l], a_im_bwd[l], log_dt_bwd[l], b_re_bwd[l], b_im_bwd[l], c_re_bwd[l], c_im_bwd[l])
            xb = _layer(xb, norm_mix_g[l], w_in[l], fwd, bwd, ssm_d[l], w_glu[l], b_glu[l], na_rpb[l],
                        g_ssm_out[l], g_na_out[l], w_out[l], norm_ffn_g[l], w_router[l],
                        w_gate[l], w_up[l], w_down[l])
        outs.append(_final_norm(xb, norm_final_g))
    return jnp.stack(outs)
```
